```python
import jax, jax.numpy as jnp
from jax import lax
import numpy as np

D_MODEL = 1024
BATCH = 8
SEQ = 8192
DEPTH = 1

MEM_LEN = 256
FOX_HEADS = 16
FOX_HEAD_DIM = 64
FOX_WIDTH = FOX_HEADS * FOX_HEAD_DIM
Q_BLOCK = 128
CONV_CH = D_MODEL
CONV_WIDTH = 31
MEM_HEADS = 4
MEM_HEAD_DIM = D_MODEL // MEM_HEADS
MEM_WIDTH = MEM_HEADS * MEM_HEAD_DIM
N_BRANCHES = 3
FFN_HIDDEN = -(-8 * D_MODEL // (3 * 256)) * 256
RMS_EPS = 1e-6
LN_EPS = 1e-5
IN_COLS = (FOX_WIDTH, FOX_WIDTH, FOX_WIDTH, FOX_HEADS, 2 * CONV_CH, MEM_WIDTH, N_BRANCHES * D_MODEL)
IN_SPLITS = tuple(int(s) for s in np.cumsum(IN_COLS)[:-1])
IN_WIDTH = int(sum(IN_COLS))

kernel_name = "hybrid_fox_conformer_memxattn_swiglu"


def rmsnorm(x, g):
    xf = x.astype(jnp.float32)
    r = xf * lax.rsqrt(jnp.mean(xf * xf, axis=-1, keepdims=True) + RMS_EPS)
    return (r * g.astype(jnp.float32)).astype(x.dtype)


def layernorm(x, g, b):
    xf = x.astype(jnp.float32)
    mu = jnp.mean(xf, axis=-1, keepdims=True)
    xc = xf - mu
    r = xc * lax.rsqrt(jnp.mean(xc * xc, axis=-1, keepdims=True) + LN_EPS)
    return (r * g.astype(jnp.float32) + b.astype(jnp.float32)).astype(x.dtype)


def forgetting_attention(q, k, v, log_f):
    S = q.shape[2]
    scale = FOX_HEAD_DIM ** -0.5
    c = jnp.cumsum(log_f, axis=-1)
    outs = []
    for i in range(S // Q_BLOCK):
        start, end = i * Q_BLOCK, (i + 1) * Q_BLOCK
        qb = q[:, :, start:end]
        kb = k[:, :, :end]
        vb = v[:, :, :end]
        s = jnp.einsum('bhqd,bhkd->bhqk', qb, kb).astype(jnp.float32) * scale
        s = s + (c[:, :, start:end, None] - c[:, :, None, :end])
        q_pos = start + jnp.arange(Q_BLOCK)
        k_pos = jnp.arange(end)
        mask = k_pos[None, :] <= q_pos[:, None]
        s = jnp.where(mask, s, -jnp.inf)
        p = jax.nn.softmax(s, axis=-1)
        outs.append(jnp.einsum('bhqk,bhkd->bhqd', p.astype(vb.dtype), vb))
    return jnp.concatenate(outs, axis=2)


def conformer_conv(u, conv_w, conv_b, ln_g, ln_b):
    a, gate = jnp.split(u, 2, axis=-1)
    g = a * jax.nn.sigmoid(gate)
    w = conv_w.astype(g.dtype).reshape(CONV_WIDTH, 1, CONV_CH)
    y = lax.conv_general_dilated(
        g, w, window_strides=(1,), padding=[(CONV_WIDTH - 1, 0)],
        dimension_numbers=('NWC', 'WIO', 'NWC'), feature_group_count=CONV_CH)
    y = y + conv_b.astype(y.dtype)
    y = layernorm(y, ln_g, ln_b)
    return jax.nn.silu(y)


def memory_attention(q, mem_n, w_kv):
    B, S = q.shape[0], q.shape[1]
    kv = mem_n @ w_kv
    k, v = jnp.split(kv, 2, axis=-1)
    k = k.reshape(B, -1, MEM_HEADS, MEM_HEAD_DIM)
    v = v.reshape(B, -1, MEM_HEADS, MEM_HEAD_DIM)
    s = jnp.einsum('bshd,bmhd->bhsm', q, k).astype(jnp.float32) * (MEM_HEAD_DIM ** -0.5)
    p = jax.nn.softmax(s, axis=-1)
    o = jnp.einsum('bhsm,bmhd->bshd', p.astype(v.dtype), v)
    return o.reshape(B, S, MEM_WIDTH)


def _fwd_setup_inputs(seed: int = 0) -> dict:
    key = jax.random.key(seed)
    ks = jax.random.split(key, 24)
    f32 = jnp.float32
    L = DEPTH

    def nrm(k, shape, scale):
        return jax.random.normal(k, shape, f32) * scale

    def gain(k, shape):
        return 1.0 + 0.05 * jax.random.normal(k, shape, f32)

    return {
        "x": jax.random.normal(ks[0], (BATCH, SEQ, D_MODEL), f32),
        "mem": jax.random.normal(ks[1], (BATCH, MEM_LEN, D_MODEL), f32),
        "norm_mix_pre": gain(ks[2], (L, D_MODEL)),
        "norm_mix_post": gain(ks[3], (L, D_MODEL)),
        "norm_mem": gain(ks[4], (L, D_MODEL)),
        "w_in": nrm(ks[5], (L, D_MODEL, IN_WIDTH), D_MODEL ** -0.5),
        "b_forget": 3.0 + 0.5 * jax.random.normal(ks[6], (L, FOX_HEADS), f32),
        "conv_w": nrm(ks[7], (L, CONV_WIDTH, CONV_CH), CONV_WIDTH ** -0.5),
        "conv_b": nrm(ks[8], (L, CONV_CH), 0.02),
        "conv_ln_g": gain(ks[9], (L, CONV_CH)),
        "conv_ln_b": nrm(ks[10], (L, CONV_CH), 0.02),
        "w_kv_mem": nrm(ks[11], (L, D_MODEL, 2 * MEM_WIDTH), D_MODEL ** -0.5),
        "w_proj_attn": nrm(ks[12], (L, FOX_WIDTH, D_MODEL), FOX_WIDTH ** -0.5),
        "w_proj_conv": nrm(ks[13], (L, CONV_CH, D_MODEL), CONV_CH ** -0.5),
        "w_proj_mem": nrm(ks[14], (L, MEM_WIDTH, D_MODEL), MEM_WIDTH ** -0.5),
        "w_out": nrm(ks[15], (L, D_MODEL, D_MODEL), D_MODEL ** -0.5),
        "norm_ffn_pre": gain(ks[16], (L, D_MODEL)),
        "norm_ffn_post": gain(ks[17], (L, D_MODEL)),
        "w_gate_up": nrm(ks[18], (L, D_MODEL, 2 * FFN_HIDDEN), D_MODEL ** -0.5),
        "w_down": nrm(ks[19], (L, FFN_HIDDEN, D_MODEL), FFN_HIDDEN ** -0.5),
    }


def _fwd_reference(x, mem, norm_mix_pre, norm_mix_post, norm_mem, w_in, b_forget, conv_w, conv_b,
              conv_ln_g, conv_ln_b, w_kv_mem, w_proj_attn, w_proj_conv, w_proj_mem, w_out,
              norm_ffn_pre, norm_ffn_post, w_gate_up, w_down):
    B, S, D = x.shape
    for l in range(DEPTH):
        h = rmsnorm(x, norm_mix_pre[l])
        proj = h @ w_in[l]
        q, k, v, f_logit, glu_in, q_mem, gate_logit = jnp.split(proj, IN_SPLITS, axis=-1)

        def heads(t):
            return t.reshape(B, S, FOX_HEADS, FOX_HEAD_DIM).transpose(0, 2, 1, 3)
        log_f = jax.nn.log_sigmoid(
            f_logit.astype(jnp.float32) + b_forget[l].astype(jnp.float32)).transpose(0, 2, 1)
        o_attn = forgetting_attention(heads(q), heads(k), heads(v), log_f)
        o_attn = o_attn.transpose(0, 2, 1, 3).reshape(B, S, FOX_WIDTH)

        o_conv = conformer_conv(glu_in, conv_w[l], conv_b[l], conv_ln_g[l], conv_ln_b[l])

        mem_n = rmsnorm(mem, norm_mem[l])
        o_mem = memory_attention(q_mem.reshape(B, S, MEM_HEADS, MEM_HEAD_DIM), mem_n, w_kv_mem[l])

        gates = jax.nn.sigmoid(gate_logit).reshape(B, S, N_BRANCHES, D)
        merged = (gates[:, :, 0] * (o_attn @ w_proj_attn[l])
                  + gates[:, :, 1] * (o_conv @ w_proj_conv[l])
                  + gates[:, :, 2] * (o_mem @ w_proj_mem[l]))
        x = x + rmsnorm(merged @ w_out[l], norm_mix_post[l])

        h2 = rmsnorm(x, norm_ffn_pre[l])
        gu = h2 @ w_gate_up[l]
        g_ffn, u_ffn = jnp.split(gu, 2, axis=-1)
        ffn = (jax.nn.silu(g_ffn) * u_ffn) @ w_down[l]
        x = x + rmsnorm(ffn, norm_ffn_post[l])
    return x


import jax as _jax
import jax.numpy as _jnp

TWIN_FORMAT = 'train_step'
FWD_PARAMS = ['x', 'mem', 'norm_mix_pre', 'norm_mix_post', 'norm_mem', 'w_in', 'b_forget', 'conv_w', 'conv_b', 'conv_ln_g', 'conv_ln_b', 'w_kv_mem', 'w_proj_attn', 'w_proj_conv', 'w_proj_mem', 'w_out', 'norm_ffn_pre', 'norm_ffn_post', 'w_gate_up', 'w_down']
TWIN_WEIGHTS = ['norm_mix_pre', 'norm_mix_post', 'norm_mem', 'w_in', 'b_forget', 'conv_w', 'conv_b', 'conv_ln_g', 'conv_ln_b', 'w_kv_mem', 'w_proj_attn', 'w_proj_conv', 'w_proj_mem', 'w_out', 'norm_ffn_pre', 'norm_ffn_post', 'w_gate_up', 'w_down']
TWIN_DIFF_INPUT = 'x'
TWIN_INPUTS = ['x', 'mem', 'norm_mix_pre', 'norm_mix_post', 'norm_mem', 'w_in', 'b_forget', 'conv_w', 'conv_b', 'conv_ln_g', 'conv_ln_b', 'w_kv_mem', 'w_proj_attn', 'w_proj_conv', 'w_proj_mem', 'w_out', 'norm_ffn_pre', 'norm_ffn_post', 'w_gate_up', 'w_down', 'loss_target', 'm_norm_mix_pre', 'm_norm_mix_post', 'm_norm_mem', 'm_w_in', 'm_b_forget', 'm_conv_w', 'm_conv_b', 'm_conv_ln_g', 'm_conv_ln_b', 'm_w_kv_mem', 'm_w_proj_attn', 'm_w_proj_conv', 'm_w_proj_mem', 'm_w_out', 'm_norm_ffn_pre', 'm_norm_ffn_post', 'm_w_gate_up', 'm_w_down', 'v_norm_mix_pre', 'v_norm_mix_post', 'v_norm_mem', 'v_w_in', 'v_b_forget', 'v_conv_w', 'v_conv_b', 'v_conv_ln_g', 'v_conv_ln_b', 'v_w_kv_mem', 'v_w_proj_attn', 'v_w_proj_conv', 'v_w_proj_mem', 'v_w_out', 'v_norm_ffn_pre', 'v_norm_ffn_post', 'v_w_gate_up', 'v_w_down']
TWIN_OUTPUTS = ['loss', 'grad_x', 'grad_norm_mix_pre', 'grad_norm_mix_post', 'grad_norm_mem', 'grad_w_in', 'grad_b_forget', 'grad_conv_w', 'grad_conv_b', 'grad_conv_ln_g', 'grad_conv_ln_b', 'grad_w_kv_mem', 'grad_w_proj_attn', 'grad_w_proj_conv', 'grad_w_proj_mem', 'grad_w_out', 'grad_norm_ffn_pre', 'grad_norm_ffn_post', 'grad_w_gate_up', 'grad_w_down', 'delta_norm_mix_pre', 'delta_norm_mix_post', 'delta_norm_mem', 'delta_w_in', 'delta_b_forget', 'delta_conv_w', 'delta_conv_b', 'delta_conv_ln_g', 'delta_conv_ln_b', 'delta_w_kv_mem', 'delta_w_proj_attn', 'delta_w_proj_conv', 'delta_w_proj_mem', 'delta_w_out', 'delta_norm_ffn_pre', 'delta_norm_ffn_post', 'delta_w_gate_up', 'delta_w_down', 'new_m_norm_mix_pre', 'new_m_norm_mix_post', 'new_m_norm_mem', 'new_m_w_in', 'new_m_b_forget', 'new_m_conv_w', 'new_m_conv_b', 'new_m_conv_ln_g', 'new_m_conv_ln_b', 'new_m_w_kv_mem', 'new_m_w_proj_attn', 'new_m_w_proj_conv', 'new_m_w_proj_mem', 'new_m_w_out', 'new_m_norm_ffn_pre', 'new_m_norm_ffn_post', 'new_m_w_gate_up', 'new_m_w_down', 'new_v_norm_mix_pre', 'new_v_norm_mix_post', 'new_v_norm_mem', 'new_v_w_in', 'new_v_b_forget', 'new_v_conv_w', 'new_v_conv_b', 'new_v_conv_ln_g', 'new_v_conv_ln_b', 'new_v_w_kv_mem', 'new_v_w_proj_attn', 'new_v_w_proj_conv', 'new_v_w_proj_mem', 'new_v_w_out', 'new_v_norm_ffn_pre', 'new_v_norm_ffn_post', 'new_v_w_gate_up', 'new_v_w_down']
TWIN_LEAF_KINDS = {'loss': 'loss', 'grad_x': 'grad_x', 'grad_norm_mix_pre': 'grad_w', 'grad_norm_mix_post': 'grad_w', 'grad_norm_mem': 'grad_w', 'grad_w_in': 'grad_w', 'grad_b_forget': 'grad_w', 'grad_conv_w': 'grad_w', 'grad_conv_b': 'grad_w', 'grad_conv_ln_g': 'grad_w', 'grad_conv_ln_b': 'grad_w', 'grad_w_kv_mem': 'grad_w', 'grad_w_proj_attn': 'grad_w', 'grad_w_proj_conv': 'grad_w', 'grad_w_proj_mem': 'grad_w', 'grad_w_out': 'grad_w', 'grad_norm_ffn_pre': 'grad_w', 'grad_norm_ffn_post': 'grad_w', 'grad_w_gate_up': 'grad_w', 'grad_w_down': 'grad_w', 'delta_norm_mix_pre': 'delta_w', 'delta_norm_mix_post': 'delta_w', 'delta_norm_mem': 'delta_w', 'delta_w_in': 'delta_w', 'delta_b_forget': 'delta_w', 'delta_conv_w': 'delta_w', 'delta_conv_b': 'delta_w', 'delta_conv_ln_g': 'delta_w', 'delta_conv_ln_b': 'delta_w', 'delta_w_kv_mem': 'delta_w', 'delta_w_proj_attn': 'delta_w', 'delta_w_proj_conv': 'delta_w', 'delta_w_proj_mem': 'delta_w', 'delta_w_out': 'delta_w', 'delta_norm_ffn_pre': 'delta_w', 'delta_norm_ffn_post': 'delta_w', 'delta_w_gate_up': 'delta_w', 'delta_w_down': 'delta_w', 'new_m_norm_mix_pre': 'new_m', 'new_m_norm_mix_post': 'new_m', 'new_m_norm_mem': 'new_m', 'new_m_w_in': 'new_m', 'new_m_b_forget': 'new_m', 'new_m_conv_w': 'new_m', 'new_m_conv_b': 'new_m', 'new_m_conv_ln_g': 'new_m', 'new_m_conv_ln_b': 'new_m', 'new_m_w_kv_mem': 'new_m', 'new_m_w_proj_attn': 'new_m', 'new_m_w_proj_conv': 'new_m', 'new_m_w_proj_mem': 'new_m', 'new_m_w_out': 'new_m', 'new_m_norm_ffn_pre': 'new_m', 'new_m_norm_ffn_post': 'new_m', 'new_m_w_gate_up': 'new_m', 'new_m_w_down': 'new_m', 'new_v_norm_mix_pre': 'new_v', 'new_v_norm_mix_post': 'new_v', 'new_v_norm_mem': 'new_v', 'new_v_w_in': 'new_v', 'new_v_b_forget': 'new_v', 'new_v_conv_w': 'new_v', 'new_v_conv_b': 'new_v', 'new_v_conv_ln_g': 'new_v', 'new_v_conv_ln_b': 'new_v', 'new_v_w_kv_mem': 'new_v', 'new_v_w_proj_attn': 'new_v', 'new_v_w_proj_conv': 'new_v', 'new_v_w_proj_mem': 'new_v', 'new_v_w_out': 'new_v', 'new_v_norm_ffn_pre': 'new_v', 'new_v_norm_ffn_post': 'new_v', 'new_v_w_gate_up': 'new_v', 'new_v_w_down': 'new_v'}


def _forward(args):
    return _fwd_reference(*[args[k] for k in FWD_PARAMS])


def _output_shape():
    def fwd():
        inp = _fwd_setup_inputs(0)
        return _fwd_reference(*[inp[k] for k in FWD_PARAMS])
    out = _jax.eval_shape(fwd)
    return out.shape, out.dtype

N_MICROBATCH = 1
ADAM_LR = 0.001
ADAM_B1 = 0.9
ADAM_B2 = 0.999
ADAM_EPS = 1e-08
ADAM_WD = 0.01
ADAM_STEP = 10
PER_EXAMPLE_BATCH_AXIS = {'x': 0, 'mem': 0, 'loss_target': 0}
SHARED_INPUTS = []
_WEIGHT_DTYPES = {'norm_mix_pre': _jnp.float32, 'norm_mix_post': _jnp.float32, 'norm_mem': _jnp.float32, 'w_in': _jnp.float32, 'b_forget': _jnp.float32, 'conv_w': _jnp.float32, 'conv_b': _jnp.float32, 'conv_ln_g': _jnp.float32, 'conv_ln_b': _jnp.float32, 'w_kv_mem': _jnp.float32, 'w_proj_attn': _jnp.float32, 'w_proj_conv': _jnp.float32, 'w_proj_mem': _jnp.float32, 'w_out': _jnp.float32, 'norm_ffn_pre': _jnp.float32, 'norm_ffn_post': _jnp.float32, 'w_gate_up': _jnp.float32, 'w_down': _jnp.float32}
MOMENT_SCALE = {'norm_mix_pre': 8.421138e-01, 'norm_mix_post': 6.456616e+01, 'norm_mem': 2.755303e-01, 'w_in': 2.767122e-01, 'b_forget': 1.514930e+00, 'conv_w': 7.317966e-01, 'conv_b': 9.381064e+00, 'conv_ln_g': 3.451121e+00, 'conv_ln_b': 5.307216e+00, 'w_kv_mem': 1.703669e-01, 'w_proj_attn': 2.868324e-01, 'w_proj_conv': 2.024217e+00, 'w_proj_mem': 2.199096e-01, 'w_out': 2.189289e+00, 'norm_ffn_pre': 2.104871e+00, 'norm_ffn_post': 6.407389e+01, 'w_gate_up': 7.871072e-01, 'w_down': 1.722746e+00}


def _to_microbatches(a, axis):
    t = _jnp.moveaxis(a, axis, 0)
    t = t.reshape((N_MICROBATCH, t.shape[0] // N_MICROBATCH) + t.shape[1:])
    return _jnp.moveaxis(t, 1, axis + 1)


def setup_inputs(seed: int = 0) -> dict:
    inp = _fwd_setup_inputs(seed)
    key = _jax.random.fold_in(_jax.random.key(seed), 7919)
    shape, _ = _output_shape()
    out = dict(inp)
    out["loss_target"] = _jax.random.normal(_jax.random.fold_in(key, 0), shape, _jnp.float32)
    for i, name in enumerate(TWIN_WEIGHTS):
        w = inp[name].astype(_jnp.float32)
        if MOMENT_SCALE is None:
            s = _jnp.sqrt(_jnp.mean(_jnp.square(w)) + 1e-30)
        else:
            s = MOMENT_SCALE[name]
        km, kv = _jax.random.split(_jax.random.fold_in(key, i + 1))
        out[name] = w
        out["m_" + name] = s * _jax.random.normal(km, w.shape, _jnp.float32)
        out["v_" + name] = (s * s) * _jax.random.uniform(kv, w.shape, _jnp.float32, 0.5, 1.5)
    if N_MICROBATCH > 1:
        for name, axis in PER_EXAMPLE_BATCH_AXIS.items():
            out[name] = _to_microbatches(out[name], axis)
    return {'x': out['x'], 'mem': out['mem'], 'norm_mix_pre': out['norm_mix_pre'], 'norm_mix_post': out['norm_mix_post'], 'norm_mem': out['norm_mem'], 'w_in': out['w_in'], 'b_forget': out['b_forget'], 'conv_w': out['conv_w'], 'conv_b': out['conv_b'], 'conv_ln_g': out['conv_ln_g'], 'conv_ln_b': out['conv_ln_b'], 'w_kv_mem': out['w_kv_mem'], 'w_proj_attn': out['w_proj_attn'], 'w_proj_conv': out['w_proj_conv'], 'w_proj_mem': out['w_proj_mem'], 'w_out': out['w_out'], 'norm_ffn_pre': out['norm_ffn_pre'], 'norm_ffn_post': out['norm_ffn_post'], 'w_gate_up': out['w_gate_up'], 'w_down': out['w_down'], 'loss_target': out['loss_target'], 'm_norm_mix_pre': out['m_norm_mix_pre'], 'm_norm_mix_post': out['m_norm_mix_post'], 'm_norm_mem': out['m_norm_mem'], 'm_w_in': out['m_w_in'], 'm_b_forget': out['m_b_forget'], 'm_conv_w': out['m_conv_w'], 'm_conv_b': out['m_conv_b'], 'm_conv_ln_g': out['m_conv_ln_g'], 'm_conv_ln_b': out['m_conv_ln_b'], 'm_w_kv_mem': out['m_w_kv_mem'], 'm_w_proj_attn': out['m_w_proj_attn'], 'm_w_proj_conv': out['m_w_proj_conv'], 'm_w_proj_mem': out['m_w_proj_mem'], 'm_w_out': out['m_w_out'], 'm_norm_ffn_pre': out['m_norm_ffn_pre'], 'm_norm_ffn_post': out['m_norm_ffn_post'], 'm_w_gate_up': out['m_w_gate_up'], 'm_w_down': out['m_w_down'], 'v_norm_mix_pre': out['v_norm_mix_pre'], 'v_norm_mix_post': out['v_norm_mix_post'], 'v_norm_mem': out['v_norm_mem'], 'v_w_in': out['v_w_in'], 'v_b_forget': out['v_b_forget'], 'v_conv_w': out['v_conv_w'], 'v_conv_b': out['v_conv_b'], 'v_conv_ln_g': out['v_conv_ln_g'], 'v_conv_ln_b': out['v_conv_ln_b'], 'v_w_kv_mem': out['v_w_kv_mem'], 'v_w_proj_attn': out['v_w_proj_attn'], 'v_w_proj_conv': out['v_w_proj_conv'], 'v_w_proj_mem': out['v_w_proj_mem'], 'v_w_out': out['v_w_out'], 'v_norm_ffn_pre': out['v_norm_ffn_pre'], 'v_norm_ffn_post': out['v_norm_ffn_post'], 'v_w_gate_up': out['v_w_gate_up'], 'v_w_down': out['v_w_down']}


def _loss(weights, diff, rest, loss_target):
    with _jax.named_scope("forward"):
        args = {**rest, TWIN_DIFF_INPUT: diff, **{k: w.astype(_WEIGHT_DTYPES[k]) for k, w in weights.items()}}
        y = _forward(args)
    with _jax.named_scope("loss_head"):
        err = _jnp.square(y.astype(_jnp.float32) - loss_target)
        return 0.5 * _jnp.sum(_jnp.mean(err, axis=-1)) if err.ndim else 0.5 * err


def _adamw(w, g, m, v):
    m = ADAM_B1 * m + (1.0 - ADAM_B1) * g
    v = ADAM_B2 * v + (1.0 - ADAM_B2) * _jnp.square(g)
    m_hat = m / (1.0 - ADAM_B1 ** ADAM_STEP)
    v_hat = v / (1.0 - ADAM_B2 ** ADAM_STEP)
    delta = -ADAM_LR * (m_hat / (_jnp.sqrt(v_hat) + ADAM_EPS) + ADAM_WD * w)
    return delta, m, v


def reference(x, mem, norm_mix_pre, norm_mix_post, norm_mem, w_in, b_forget, conv_w, conv_b, conv_ln_g, conv_ln_b, w_kv_mem, w_proj_attn, w_proj_conv, w_proj_mem, w_out, norm_ffn_pre, norm_ffn_post, w_gate_up, w_down, loss_target, m_norm_mix_pre, m_norm_mix_post, m_norm_mem, m_w_in, m_b_forget, m_conv_w, m_conv_b, m_conv_ln_g, m_conv_ln_b, m_w_kv_mem, m_w_proj_attn, m_w_proj_conv, m_w_proj_mem, m_w_out, m_norm_ffn_pre, m_norm_ffn_post, m_w_gate_up, m_w_down, v_norm_mix_pre, v_norm_mix_post, v_norm_mem, v_w_in, v_b_forget, v_conv_w, v_conv_b, v_conv_ln_g, v_conv_ln_b, v_w_kv_mem, v_w_proj_attn, v_w_proj_conv, v_w_proj_mem, v_w_out, v_norm_ffn_pre, v_norm_ffn_post, v_w_gate_up, v_w_down):
    given = dict(x=x, mem=mem, norm_mix_pre=norm_mix_pre, norm_mix_post=norm_mix_post, norm_mem=norm_mem, w_in=w_in, b_forget=b_forget, conv_w=conv_w, conv_b=conv_b, conv_ln_g=conv_ln_g, conv_ln_b=conv_ln_b, w_kv_mem=w_kv_mem, w_proj_attn=w_proj_attn, w_proj_conv=w_proj_conv, w_proj_mem=w_proj_mem, w_out=w_out, norm_ffn_pre=norm_ffn_pre, norm_ffn_post=norm_ffn_post, w_gate_up=w_gate_up, w_down=w_down, loss_target=loss_target, m_norm_mix_pre=m_norm_mix_pre, m_norm_mix_post=m_norm_mix_post, m_norm_mem=m_norm_mem, m_w_in=m_w_in, m_b_forget=m_b_forget, m_conv_w=m_conv_w, m_conv_b=m_conv_b, m_conv_ln_g=m_conv_ln_g, m_conv_ln_b=m_conv_ln_b, m_w_kv_mem=m_w_kv_mem, m_w_proj_attn=m_w_proj_attn, m_w_proj_conv=m_w_proj_conv, m_w_proj_mem=m_w_proj_mem, m_w_out=m_w_out, m_norm_ffn_pre=m_norm_ffn_pre, m_norm_ffn_post=m_norm_ffn_post, m_w_gate_up=m_w_gate_up, m_w_down=m_w_down, v_norm_mix_pre=v_norm_mix_pre, v_norm_mix_post=v_norm_mix_post, v_norm_mem=v_norm_mem, v_w_in=v_w_in, v_b_forget=v_b_forget, v_conv_w=v_conv_w, v_conv_b=v_conv_b, v_conv_ln_g=v_conv_ln_g, v_conv_ln_b=v_conv_ln_b, v_w_kv_mem=v_w_kv_mem, v_w_proj_attn=v_w_proj_attn, v_w_proj_conv=v_w_proj_conv, v_w_proj_mem=v_w_proj_mem, v_w_out=v_w_out, v_norm_ffn_pre=v_norm_ffn_pre, v_norm_ffn_post=v_norm_ffn_post, v_w_gate_up=v_w_gate_up, v_w_down=v_w_down)
    weights = {n: given[n] for n in TWIN_WEIGHTS}
    shared = {n: given[n] for n in SHARED_INPUTS}
    per_example = {n: given[n] for n in ['x', 'mem']}
    grad_fn = _jax.value_and_grad(_loss, argnums=(0, 1))

    def one_microbatch(ex, loss_target):
        ex = dict(ex)
        diff = ex.pop(TWIN_DIFF_INPUT)
        return grad_fn(weights, diff, {**shared, **ex}, loss_target)

    if N_MICROBATCH == 1:
        loss, (grad_w, grad_x) = one_microbatch(per_example, given["loss_target"])
    else:
        def body(carry, xs):
            loss_sum, grad_sum = carry
            l_k, (gw_k, gx_k) = one_microbatch(xs[0], xs[1])
            with _jax.named_scope("update"):
                return (loss_sum + l_k, _jax.tree.map(_jnp.add, grad_sum, gw_k)), gx_k

        init = (_jnp.zeros((), _jnp.float32), _jax.tree.map(_jnp.zeros_like, weights))
        (loss, grad_w), grad_x = _jax.lax.scan(body, init, (per_example, given["loss_target"]))
    with _jax.named_scope("update"):
        delta_w, new_m, new_v = {}, {}, {}
        for n in TWIN_WEIGHTS:
            delta_w[n], new_m[n], new_v[n] = _adamw(weights[n], grad_w[n], given["m_" + n], given["v_" + n])
    return (loss, grad_x, *[grad_w[n] for n in TWIN_WEIGHTS], *[delta_w[n] for n in TWIN_WEIGHTS],
            *[new_m[n] for n in TWIN_WEIGHTS], *[new_v[n] for n in TWIN_WEIGHTS])
```

```python
import functools

import jax
import jax.numpy as jnp
from jax import lax
from jax.experimental import pallas as pl
from jax.experimental.pallas import tpu as pltpu

F32 = jnp.float32
BF16 = jnp.bfloat16

D_MODEL = 1024
N_DEV = 8
FOX_HEADS = 16
FOX_HEAD_DIM = 64
HEAD_PAIRS = FOX_HEADS // 2
MEM_HEADS = 4
MEM_HEAD_DIM = D_MODEL // MEM_HEADS
CONV_WIDTH = 31
CONV_HALO = 32
FFN_HIDDEN = 2816
RMS_EPS = 1e-6
LN_EPS = 1e-5
ADAM_LR = 0.001
ADAM_B1 = 0.9
ADAM_B2 = 0.999
ADAM_EPS = 1e-08
ADAM_WD = 0.01
ADAM_STEP = 10
NEG_BIG = -1e30
LANES = 128

PB_Q, PB_K, PB_V, PB_A, PB_GATE, PB_QMEM, PB_G0 = 0, 1, 2, 3, 4, 5, 6
P_WIDTH = 9 * D_MODEL

NT_DIMS = (((1,), (1,)), ((), ()))
TN_DIMS = (((0,), (0,)), ((), ()))


def _cparams(sem, vmem_mb=None):
    kw = dict(dimension_semantics=sem)
    if vmem_mb is not None:
        kw["vmem_limit_bytes"] = vmem_mb * 1024 * 1024
    return pltpu.CompilerParams(**kw)


def _tile(dim, want):
    t = min(dim, want)
    assert dim % t == 0, (dim, want)
    return t


def _sigmoid(z):
    return 1.0 / (1.0 + jnp.exp(-z))


def _matmul(a, b, *, mode, out_dtype, name, tm=1024, tn=1024, tk=1024):
    if mode == "nn":
        (M, K), (K2, N) = a.shape, b.shape
    elif mode == "nt":
        (M, K), (N, K2) = a.shape, b.shape
    else:
        (K, M), (K2, N) = a.shape, b.shape
    assert K == K2, (a.shape, b.shape, mode)
    tm, tn, tk = _tile(M, tm), _tile(N, tn), _tile(K, tk)
    nk = K // tk
    dims = {"nn": (((1,), (0,)), ((), ())), "nt": NT_DIMS, "tn": TN_DIMS}[mode]

    def body(a_ref, b_ref, o_ref, *scratch):
        part = lax.dot_general(a_ref[...], b_ref[...], dims, preferred_element_type=F32)
        if nk == 1:
            o_ref[...] = part.astype(o_ref.dtype)
        else:
            acc_ref, = scratch
            k = pl.program_id(2)

            @pl.when(k == 0)
            def _():
                acc_ref[...] = part

            @pl.when(k > 0)
            def _():
                acc_ref[...] += part

            @pl.when(k == nk - 1)
            def _():
                o_ref[...] = acc_ref[...].astype(o_ref.dtype)

    a_spec = pl.BlockSpec((tk, tm), lambda j, i, k: (k, i)) if mode == "tn" else pl.BlockSpec((tm, tk), lambda j, i, k: (i, k))
    b_spec = pl.BlockSpec((tn, tk), lambda j, i, k: (j, k)) if mode == "nt" else pl.BlockSpec((tk, tn), lambda j, i, k: (k, j))
    return pl.pallas_call(
        body,
        name=name,
        grid=(N // tn, M // tm, nk),
        in_specs=[a_spec, b_spec],
        out_specs=pl.BlockSpec((tm, tn), lambda j, i, k: (i, j)),
        out_shape=jax.ShapeDtypeStruct((M, N), out_dtype),
        scratch_shapes=[pltpu.VMEM((tm, tn), F32)] if nk > 1 else [],
        compiler_params=_cparams(("parallel", "parallel", "arbitrary"), 56),
    )(a, b)


def _rms_fwd(x, g, *, name, tm=512):
    S, D = x.shape
    tm = _tile(S, tm)

    def body(x_ref, g_ref, o_ref):
        xv = x_ref[...]
        rstd = lax.rsqrt(jnp.mean(xv * xv, axis=-1, keepdims=True) + RMS_EPS)
        o_ref[...] = (xv * rstd * g_ref[...]).astype(o_ref.dtype)

    return pl.pallas_call(
        body, name=name, grid=(S // tm,),
        in_specs=[pl.BlockSpec((tm, D), lambda i: (i, 0)), pl.BlockSpec((1, D), lambda i: (0, 0))],
        out_specs=pl.BlockSpec((tm, D), lambda i: (i, 0)),
        out_shape=jax.ShapeDtypeStruct((S, D), BF16),
        compiler_params=_cparams(("parallel",)),
    )(x, g)


def _rms_bwd(xin, dys, g, res, *, out_dtype, name, tm=512):
    S, D = xin.shape
    tm = _tile(S, tm)
    n_dy = len(dys)
    has_res = res is not None

    def body(*refs):
        x_ref, g_ref = refs[0], refs[1]
        dy_refs = refs[2:2 + n_dy]
        pos = 2 + n_dy
        res_ref = refs[pos] if has_res else None
        pos += int(has_res)
        dx_ref, dg_ref = refs[pos], refs[pos + 1]
        i = pl.program_id(0)
        xv = x_ref[...]
        dy = dy_refs[0][...].astype(F32)
        for r in dy_refs[1:]:
            dy = dy + r[...].astype(F32)
        rstd = lax.rsqrt(jnp.mean(xv * xv, axis=-1, keepdims=True) + RMS_EPS)
        xhat = xv * rstd
        gy = dy * g_ref[...]
        dx = rstd * (gy - xhat * jnp.mean(gy * xhat, axis=-1, keepdims=True))
        if has_res:
            dx = dx + res_ref[...]
        dx_ref[...] = dx.astype(dx_ref.dtype)
        part = jnp.sum(dy * xhat, axis=0, keepdims=True)

        @pl.when(i == 0)
        def _():
            dg_ref[...] = part

        @pl.when(i > 0)
        def _():
            dg_ref[...] += part

    row = pl.BlockSpec((tm, D), lambda i: (i, 0))
    vec = pl.BlockSpec((1, D), lambda i: (0, 0))
    ins = [xin, g] + list(dys) + ([res] if has_res else [])
    return pl.pallas_call(
        body, name=name, grid=(S // tm,),
        in_specs=[row, vec] + [row] * n_dy + ([row] if has_res else []),
        out_specs=[row, vec],
        out_shape=[jax.ShapeDtypeStruct((S, D), out_dtype), jax.ShapeDtypeStruct((1, D), F32)],
        compiler_params=_cparams(("arbitrary",)),
    )(*ins)


def _head_mask(hh, shape):
    lane = lax.broadcasted_iota(jnp.int32, shape, len(shape) - 1)
    return (lane // FOX_HEAD_DIM) == hh


def _attn_fwd(P, ccol, crow, *, tq, name):
    S = P.shape[0]
    tq = _tile(S, tq)
    nq = S // tq
    scale = FOX_HEAD_DIM ** -0.5

    def body(q_ref, k_ref, v_ref, cc_ref, cr_ref, o_ref, o32_ref, lse_ref):
        i = pl.program_id(1)
        q = q_ref[...] * jnp.asarray(scale, BF16)
        row = lax.broadcasted_iota(jnp.int32, (tq, tq), 0)
        col = lax.broadcasted_iota(jnp.int32, (tq, tq), 1)
        causal = col <= row
        out = jnp.zeros((tq, LANES), F32)
        for hh in range(2):
            qh = jnp.where(_head_mask(hh, (tq, LANES)), q, jnp.zeros_like(q))
            cb = cc_ref[hh, 0:1, :]

            def step(j, carry, masked, qh=qh, cb=cb, hh=hh):
                m, l, acc = carry
                start = pl.multiple_of(j * tq, tq)
                kj = k_ref[pl.ds(start, tq), :]
                vj = v_ref[pl.ds(start, tq), :]
                bias = cb - cr_ref[0, hh:hh + 1, pl.ds(start, tq)]
                s = lax.dot_general(qh, kj, NT_DIMS, preferred_element_type=F32) + bias
                if masked:
                    s = jnp.where(causal, s, NEG_BIG)
                m_new = jnp.maximum(m, jnp.max(s, axis=1, keepdims=True))
                alpha = jnp.exp(m - m_new)
                p = jnp.exp(s - m_new)
                l = alpha * l + jnp.sum(p, axis=1, keepdims=True)
                p_hi = p.astype(BF16)
                p_lo = (p - p_hi.astype(F32)).astype(BF16)
                acc = (alpha * acc + jnp.dot(p_hi, vj, preferred_element_type=F32)
                       + jnp.dot(p_lo, vj, preferred_element_type=F32))
                return m_new, l, acc

            carry = (jnp.full((tq, 1), NEG_BIG, F32), jnp.zeros((tq, 1), F32), jnp.zeros((tq, LANES), F32))
            carry = lax.fori_loop(0, i, functools.partial(step, masked=False), carry)
            m, l, acc = step(i, carry, True)
            out = jnp.where(_head_mask(hh, (tq, LANES)), acc / l, out)
            lse_ref[hh] = m + jnp.log(l)
        o_ref[...] = out.astype(o_ref.dtype)
        o32_ref[...] = out

    nblk = D_MODEL // LANES
    return pl.pallas_call(
        body, name=name, grid=(HEAD_PAIRS, nq),
        in_specs=[
            pl.BlockSpec((tq, LANES), lambda p, i: (i, PB_Q * nblk + p)),
            pl.BlockSpec((S, LANES), lambda p, i: (0, PB_K * nblk + p)),
            pl.BlockSpec((S, LANES), lambda p, i: (0, PB_V * nblk + p)),
            pl.BlockSpec((2, tq, 1), lambda p, i: (p, i, 0)),
            pl.BlockSpec((1, 2, S), lambda p, i: (p, 0, 0)),
        ],
        out_specs=[
            pl.BlockSpec((tq, LANES), lambda p, i: (i, p)),
            pl.BlockSpec((tq, LANES), lambda p, i: (i, p)),
            pl.BlockSpec((2, tq, 1), lambda p, i: (p, i, 0)),
        ],
        out_shape=[jax.ShapeDtypeStruct((S, D_MODEL), BF16), jax.ShapeDtypeStruct((S, D_MODEL), F32),
                   jax.ShapeDtypeStruct((FOX_HEADS, S, 1), F32)],
        compiler_params=_cparams(("parallel", "arbitrary"), 56),
    )(P, P, P, ccol, crow)


def _attn_delta(do, o, *, name, tm=512):
    S = do.shape[0]
    tm = _tile(S, tm)

    def body(do_ref, o_ref, d_ref):
        prod = do_ref[...].astype(F32) * o_ref[...].astype(F32)
        for hh in range(2):
            d_ref[hh] = jnp.sum(jnp.where(_head_mask(hh, (tm, LANES)), prod, 0.0), axis=1, keepdims=True)

    blk = pl.BlockSpec((tm, LANES), lambda p, i: (i, p))
    return pl.pallas_call(
        body, name=name, grid=(HEAD_PAIRS, S // tm),
        in_specs=[blk, blk],
        out_specs=pl.BlockSpec((2, tm, 1), lambda p, i: (p, i, 0)),
        out_shape=jax.ShapeDtypeStruct((FOX_HEADS, S, 1), F32),
        compiler_params=_cparams(("parallel", "parallel")),
    )(do, o)


def _attn_bwd(P, do, lse, delta, ccol, crow, *, tq, name):
    S = P.shape[0]
    tq = _tile(S, tq)
    nq = S // tq
    scale = FOX_HEAD_DIM ** -0.5

    def body(q_ref, k_ref, v_ref, do_ref, lse_ref, dl_ref, cc_ref, cr_ref, dq_ref, dk_ref, dv_ref, dc_ref):
        i = pl.program_id(1)

        @pl.when(i == 0)
        def _():
            dk_ref[...] = jnp.zeros_like(dk_ref)
            dv_ref[...] = jnp.zeros_like(dv_ref)
            dc_ref[...] = jnp.zeros_like(dc_ref)

        q = q_ref[...] * jnp.asarray(scale, BF16)
        do_v = do_ref[...]
        row = lax.broadcasted_iota(jnp.int32, (tq, tq), 0)
        col = lax.broadcasted_iota(jnp.int32, (tq, tq), 1)
        causal = col <= row
        dq = jnp.zeros((tq, LANES), F32)
        for hh in range(2):
            hm = _head_mask(hh, (tq, LANES))
            qh = jnp.where(hm, q, jnp.zeros_like(q))
            doh = jnp.where(hm, do_v, jnp.zeros_like(do_v))
            cb = cc_ref[hh, 0:1, :]
            lse_h = lse_ref[hh]
            dl_h = dl_ref[hh]

            def step(j, dq_acc, masked, qh=qh, doh=doh, cb=cb, lse_h=lse_h, dl_h=dl_h, hh=hh, hm=hm):
                start = pl.multiple_of(j * tq, tq)
                kj = k_ref[pl.ds(start, tq), :]
                vj = v_ref[pl.ds(start, tq), :]
                bias = cb - cr_ref[0, hh:hh + 1, pl.ds(start, tq)]
                s = lax.dot_general(qh, kj, NT_DIMS, preferred_element_type=F32) + bias
                if masked:
                    s = jnp.where(causal, s, NEG_BIG)
                p = jnp.exp(s - lse_h)
                dp = lax.dot_general(doh, vj, NT_DIMS, preferred_element_type=F32)
                ds = p * (dp - dl_h)
                pb = p.astype(BF16)
                dsb = ds.astype(BF16)
                dv_ref[pl.ds(start, tq), :] += lax.dot_general(pb, doh, TN_DIMS, preferred_element_type=F32)
                dk_ref[pl.ds(start, tq), :] += lax.dot_general(dsb, qh, TN_DIMS, preferred_element_type=F32)
                dc_ref[0, hh:hh + 1, pl.ds(start, tq)] -= jnp.sum(ds, axis=0, keepdims=True)
                kh = jnp.where(hm, kj, jnp.zeros_like(kj))
                return dq_acc + jnp.dot(dsb, kh, preferred_element_type=F32)

            dq = lax.fori_loop(0, i, functools.partial(step, masked=False), dq)
            dq = step(i, dq, True)
        dq_ref[...] = (dq * scale).astype(dq_ref.dtype)

    nblk = D_MODEL // LANES
    qblk = pl.BlockSpec((tq, LANES), lambda p, i: (i, p))
    stat = pl.BlockSpec((2, tq, 1), lambda p, i: (p, i, 0))
    full = pl.BlockSpec((S, LANES), lambda p, i: (0, p))
    return pl.pallas_call(
        body, name=name, grid=(HEAD_PAIRS, nq),
        in_specs=[
            pl.BlockSpec((tq, LANES), lambda p, i: (i, PB_Q * nblk + p)),
            pl.BlockSpec((S, LANES), lambda p, i: (0, PB_K * nblk + p)),
            pl.BlockSpec((S, LANES), lambda p, i: (0, PB_V * nblk + p)),
            qblk, stat, stat, stat,
            pl.BlockSpec((1, 2, S), lambda p, i: (p, 0, 0)),
        ],
        out_specs=[qblk, full, full, pl.BlockSpec((1, 2, S), lambda p, i: (p, 0, 0))],
        out_shape=[
            jax.ShapeDtypeStruct((S, D_MODEL), BF16),
            jax.ShapeDtypeStruct((S, D_MODEL), F32),
            jax.ShapeDtypeStruct((S, D_MODEL), F32),
            jax.ShapeDtypeStruct((HEAD_PAIRS, 2, S), F32),
        ],
        compiler_params=_cparams(("parallel", "arbitrary"), 56),
    )(P, P, P, do, lse, delta, ccol, crow)


def _cumsum_lanes(x, *, reverse, name):
    R, S = x.shape
    nb = S // LANES

    def body(x_ref, o_ref):
        r = lax.broadcasted_iota(jnp.int32, (LANES, LANES), 0)
        c = lax.broadcasted_iota(jnp.int32, (LANES, LANES), 1)
        tri = ((r >= c) if reverse else (r <= c)).astype(F32)

        def step(b, carry):
            blk = (nb - 1 - b) if reverse else b
            start = pl.multiple_of(blk * LANES, LANES)
            xb = x_ref[:, pl.ds(start, LANES)]
            y = jnp.dot(xb, tri, precision=lax.Precision.HIGHEST, preferred_element_type=F32) + carry
            o_ref[:, pl.ds(start, LANES)] = y
            return carry + jnp.sum(xb, axis=1, keepdims=True)

        lax.fori_loop(0, nb, step, jnp.zeros((R, 1), F32))

    return pl.pallas_call(
        body, name=name,
        in_specs=[pl.BlockSpec(memory_space=pltpu.VMEM)],
        out_specs=pl.BlockSpec(memory_space=pltpu.VMEM),
        out_shape=jax.ShapeDtypeStruct((R, S), F32),
    )(x)


def _forget_fwd(f_logit, b_pad, *, name, tm=1024):
    S = f_logit.shape[0]
    tm = _tile(S, tm)

    def body(f_ref, b_ref, o_ref):
        z = f_ref[...] + b_ref[...]
        o_ref[...] = jnp.minimum(z, 0.0) - jnp.log(1.0 + jnp.exp(-jnp.abs(z)))

    blk = pl.BlockSpec((tm, LANES), lambda i: (i, 0))
    return pl.pallas_call(
        body, name=name, grid=(S // tm,),
        in_specs=[blk, pl.BlockSpec((1, LANES), lambda i: (0, 0))],
        out_specs=blk, out_shape=jax.ShapeDtypeStruct((S, LANES), F32),
        compiler_params=_cparams(("parallel",)),
    )(f_logit, b_pad)


def _forget_bwd(f_logit, b_pad, dlogf, *, name, tm=1024):
    S = f_logit.shape[0]
    tm = _tile(S, tm)

    def body(f_ref, b_ref, d_ref, o_ref, db_ref):
        i = pl.program_id(0)
        z = f_ref[...] + b_ref[...]
        dz = d_ref[...] * (1.0 - _sigmoid(z))
        o_ref[...] = dz.astype(o_ref.dtype)
        part = jnp.sum(dz, axis=0, keepdims=True)

        @pl.when(i == 0)
        def _():
            db_ref[...] = part

        @pl.when(i > 0)
        def _():
            db_ref[...] += part

    blk = pl.BlockSpec((tm, LANES), lambda i: (i, 0))
    vec = pl.BlockSpec((1, LANES), lambda i: (0, 0))
    return pl.pallas_call(
        body, name=name, grid=(S // tm,),
        in_specs=[blk, vec, blk], out_specs=[blk, vec],
        out_shape=[jax.ShapeDtypeStruct((S, LANES), BF16), jax.ShapeDtypeStruct((1, LANES), F32)],
        compiler_params=_cparams(("arbitrary",)),
    )(f_logit, b_pad, dlogf)


def _layernorm_stats(y):
    mu = jnp.mean(y, axis=-1, keepdims=True)
    yc = y - mu
    rstd = lax.rsqrt(jnp.mean(yc * yc, axis=-1, keepdims=True) + LN_EPS)
    return yc * rstd, rstd


def _conv_fwd(P, w_pad, conv_b, ln_g, ln_b, *, name, tm=256):
    S = P.shape[0]
    C = D_MODEL
    tm = _tile(S, tm)
    hb = tm // CONV_HALO

    def body(a_ref, gt_ref, ah_ref, gh_ref, w_ref, cb_ref, g_ref, b_ref, o_ref, y_ref, glu_ref):
        i = pl.program_id(0)
        halo = ah_ref[...].astype(F32) * _sigmoid(gh_ref[...].astype(F32))
        glu_ref[0:CONV_HALO, :] = jnp.where(i > 0, halo, 0.0)
        glu_ref[CONV_HALO:, :] = a_ref[...].astype(F32) * _sigmoid(gt_ref[...].astype(F32))
        acc = jnp.zeros((tm, C), F32)
        for k in range(CONV_WIDTH):
            acc = acc + w_ref[k:k + 1, :] * glu_ref[pl.ds(CONV_HALO - (CONV_WIDTH - 1) + k, tm), :]
        y = acc + cb_ref[...]
        y_ref[...] = y
        xhat, _ = _layernorm_stats(y)
        z = xhat * g_ref[...] + b_ref[...]
        o_ref[...] = (z * _sigmoid(z)).astype(o_ref.dtype)

    vec = pl.BlockSpec((1, C), lambda i: (0, 0))
    row = pl.BlockSpec((tm, C), lambda i: (i, 0))
    return pl.pallas_call(
        body, name=name, grid=(S // tm,),
        in_specs=[
            pl.BlockSpec((tm, C), lambda i: (i, PB_A)),
            pl.BlockSpec((tm, C), lambda i: (i, PB_GATE)),
            pl.BlockSpec((CONV_HALO, C), lambda i: (jnp.maximum(i * hb - 1, 0), PB_A)),
            pl.BlockSpec((CONV_HALO, C), lambda i: (jnp.maximum(i * hb - 1, 0), PB_GATE)),
            pl.BlockSpec((CONV_HALO, C), lambda i: (0, 0)),
            vec, vec, vec,
        ],
        out_specs=[row, row],
        out_shape=[jax.ShapeDtypeStruct((S, C), BF16), jax.ShapeDtypeStruct((S, C), F32)],
        scratch_shapes=[pltpu.VMEM((tm + CONV_HALO, C), F32)],
        compiler_params=_cparams(("parallel",), 56),
    )(P, P, P, P, w_pad, conv_b, ln_g, ln_b)


def _conv_bwd(P, do, y, w_pad, ln_g, ln_b, *, name, tm=256):
    S = P.shape[0]
    C = D_MODEL
    tm = _tile(S, tm)
    hb = tm // CONV_HALO
    n_tiles = S // tm
    last_halo = S // CONV_HALO - 1

    def body(a_ref, gt_ref, ah_ref, gh_ref, do_ref, y_ref, don_ref, yn_ref, w_ref, g_ref, b_ref,
             dglu_ref, dw_ref, small_ref, glu_ref, dy_ref):
        i = pl.program_id(0)

        @pl.when(i == 0)
        def _():
            dw_ref[...] = jnp.zeros_like(dw_ref)
            small_ref[...] = jnp.zeros_like(small_ref)

        def ln_bwd(do_v, y_v):
            xhat, rstd = _layernorm_stats(y_v)
            z = xhat * g_ref[...] + b_ref[...]
            sg = _sigmoid(z)
            dz = do_v * (sg * (1.0 + z * (1.0 - sg)))
            dxh = dz * g_ref[...]
            dy = rstd * (dxh - jnp.mean(dxh, axis=-1, keepdims=True) - xhat * jnp.mean(dxh * xhat, axis=-1, keepdims=True))
            return dy, dz, xhat

        dy, dz, xhat = ln_bwd(do_ref[...], y_ref[...])
        dy_next, _, _ = ln_bwd(don_ref[...], yn_ref[...])
        small_ref[0:1, :] += jnp.sum(dz * xhat, axis=0, keepdims=True)
        small_ref[1:2, :] += jnp.sum(dz, axis=0, keepdims=True)
        small_ref[2:3, :] += jnp.sum(dy, axis=0, keepdims=True)
        dy_ref[0:tm, :] = dy
        dy_ref[tm:, :] = jnp.where(i < n_tiles - 1, dy_next, 0.0)

        a = a_ref[...].astype(F32)
        sig = _sigmoid(gt_ref[...].astype(F32))
        halo = ah_ref[...].astype(F32) * _sigmoid(gh_ref[...].astype(F32))
        glu_ref[0:CONV_HALO, :] = jnp.where(i > 0, halo, 0.0)
        glu_ref[CONV_HALO:, :] = a * sig

        dg = jnp.zeros((tm, C), F32)
        for k in range(CONV_WIDTH):
            shifted = glu_ref[pl.ds(CONV_HALO - (CONV_WIDTH - 1) + k, tm), :]
            dw_ref[k:k + 1, :] += jnp.sum(dy * shifted, axis=0, keepdims=True)
            dg = dg + w_ref[k:k + 1, :] * dy_ref[pl.ds(CONV_WIDTH - 1 - k, tm), :]
        dglu_ref[:, 0:C] = (dg * sig).astype(dglu_ref.dtype)
        dglu_ref[:, C:] = (dg * a * sig * (1.0 - sig)).astype(dglu_ref.dtype)

    vec = pl.BlockSpec((1, C), lambda i: (0, 0))
    row = pl.BlockSpec((tm, C), lambda i: (i, 0))
    nxt = pl.BlockSpec((CONV_HALO, C), lambda i: (jnp.minimum((i + 1) * hb, last_halo), 0))
    return pl.pallas_call(
        body, name=name, grid=(n_tiles,),
        in_specs=[
            pl.BlockSpec((tm, C), lambda i: (i, PB_A)),
            pl.BlockSpec((tm, C), lambda i: (i, PB_GATE)),
            pl.BlockSpec((CONV_HALO, C), lambda i: (jnp.maximum(i * hb - 1, 0), PB_A)),
            pl.BlockSpec((CONV_HALO, C), lambda i: (jnp.maximum(i * hb - 1, 0), PB_GATE)),
            row, row, nxt, nxt,
            pl.BlockSpec((CONV_HALO, C), lambda i: (0, 0)),
            vec, vec,
        ],
        out_specs=[
            pl.BlockSpec((tm, 2 * C), lambda i: (i, 0)),
            pl.BlockSpec((CONV_HALO, C), lambda i: (0, 0)),
            pl.BlockSpec((8, C), lambda i: (0, 0)),
        ],
        out_shape=[
            jax.ShapeDtypeStruct((S, 2 * C), BF16),
            jax.ShapeDtypeStruct((CONV_HALO, C), F32),
            jax.ShapeDtypeStruct((8, C), F32),
        ],
        scratch_shapes=[pltpu.VMEM((tm + CONV_HALO, C), F32), pltpu.VMEM((tm + CONV_HALO, C), F32)],
        compiler_params=_cparams(("arbitrary",), 56),
    )(P, P, P, P, do, y, do, y, w_pad, ln_g, ln_b)


def _mem_softmax(qh, kh):
    s = lax.dot_general(qh, kh, NT_DIMS, preferred_element_type=F32)
    e = jnp.exp(s - jnp.max(s, axis=1, keepdims=True))
    return e / jnp.sum(e, axis=1, keepdims=True)


def _mem_fwd(P, kv, *, name, tm=512):
    S, M = P.shape[0], kv.shape[0]
    tm = _tile(S, tm)
    scale = MEM_HEAD_DIM ** -0.5

    def body(q_ref, k_ref, v_ref, o_ref):
        for h in range(MEM_HEADS):
            sl = slice(h * MEM_HEAD_DIM, (h + 1) * MEM_HEAD_DIM)
            qh = q_ref[:, sl] * jnp.asarray(scale, BF16)
            p = _mem_softmax(qh, k_ref[:, sl])
            o_ref[:, sl] = jnp.dot(p.astype(BF16), v_ref[:, sl], preferred_element_type=F32).astype(o_ref.dtype)

    return pl.pallas_call(
        body, name=name, grid=(S // tm,),
        in_specs=[
            pl.BlockSpec((tm, D_MODEL), lambda i: (i, PB_QMEM)),
            pl.BlockSpec((M, D_MODEL), lambda i: (0, 0)),
            pl.BlockSpec((M, D_MODEL), lambda i: (0, 1)),
        ],
        out_specs=pl.BlockSpec((tm, D_MODEL), lambda i: (i, 0)),
        out_shape=jax.ShapeDtypeStruct((S, D_MODEL), BF16),
        compiler_params=_cparams(("parallel",)),
    )(P, kv, kv)


def _mem_bwd(P, do, kv, *, name, tm=512):
    S, M = P.shape[0], kv.shape[0]
    tm = _tile(S, tm)
    scale = MEM_HEAD_DIM ** -0.5

    def body(q_ref, k_ref, v_ref, do_ref, dq_ref, dkv_ref):
        i = pl.program_id(0)

        @pl.when(i == 0)
        def _():
            dkv_ref[...] = jnp.zeros_like(dkv_ref)

        for h in range(MEM_HEADS):
            sl = slice(h * MEM_HEAD_DIM, (h + 1) * MEM_HEAD_DIM)
            slv = slice(D_MODEL + h * MEM_HEAD_DIM, D_MODEL + (h + 1) * MEM_HEAD_DIM)
            qh = q_ref[:, sl] * jnp.asarray(scale, BF16)
            kh, vh, doh = k_ref[:, sl], v_ref[:, sl], do_ref[:, sl]
            p = _mem_softmax(qh, kh)
            dp = lax.dot_general(doh, vh, NT_DIMS, preferred_element_type=F32)
            ds = p * (dp - jnp.sum(p * dp, axis=1, keepdims=True))
            dsb = ds.astype(BF16)
            dq_ref[:, sl] = (jnp.dot(dsb, kh, preferred_element_type=F32) * scale).astype(dq_ref.dtype)
            dkv_ref[:, sl] += lax.dot_general(dsb, qh, TN_DIMS, preferred_element_type=F32)
            dkv_ref[:, slv] += lax.dot_general(p.astype(BF16), doh, TN_DIMS, preferred_element_type=F32)

    row = pl.BlockSpec((tm, D_MODEL), lambda i: (i, 0))
    return pl.pallas_call(
        body, name=name, grid=(S // tm,),
        in_specs=[
            pl.BlockSpec((tm, D_MODEL), lambda i: (i, PB_QMEM)),
            pl.BlockSpec((M, D_MODEL), lambda i: (0, 0)),
            pl.BlockSpec((M, D_MODEL), lambda i: (0, 1)),
            row,
        ],
        out_specs=[row, pl.BlockSpec((M, 2 * D_MODEL), lambda i: (0, 0))],
        out_shape=[jax.ShapeDtypeStruct((S, D_MODEL), BF16), jax.ShapeDtypeStruct((M, 2 * D_MODEL), F32)],
        compiler_params=_cparams(("arbitrary",)),
    )(P, kv, kv, do)


def _merge_fwd(o_attn, o_conv, o_mem, P, wpa, wpc, wpm, w_out, *, name, tm=256):
    S = P.shape[0]
    D = D_MODEL
    tm = _tile(S, tm)

    def body(oa_ref, oc_ref, om_ref, g0_ref, g1_ref, g2_ref, wa_ref, wc_ref, wm_ref, wo_ref, mg_ref, y_ref, pb_ref):
        merged = jnp.zeros((tm, D), F32)
        for b, (o_ref, g_ref, w_ref) in enumerate(((oa_ref, g0_ref, wa_ref), (oc_ref, g1_ref, wc_ref), (om_ref, g2_ref, wm_ref))):
            pb = jnp.dot(o_ref[...], w_ref[...], preferred_element_type=F32)
            pb_ref[b] = pb.astype(pb_ref.dtype)
            merged = merged + _sigmoid(g_ref[...].astype(F32)) * pb
        mb = merged.astype(BF16)
        mg_ref[...] = mb
        y_ref[...] = jnp.dot(mb, wo_ref[...], preferred_element_type=F32)

    row = pl.BlockSpec((tm, D), lambda i: (i, 0))
    wsp = pl.BlockSpec((D, D), lambda i: (0, 0))
    return pl.pallas_call(
        body, name=name, grid=(S // tm,),
        in_specs=[row, row, row] + [pl.BlockSpec((tm, D), lambda i, b=b: (i, PB_G0 + b)) for b in range(3)] + [wsp] * 4,
        out_specs=[row, row, pl.BlockSpec((3, tm, D), lambda i: (0, i, 0))],
        out_shape=[jax.ShapeDtypeStruct((S, D), BF16), jax.ShapeDtypeStruct((S, D), F32), jax.ShapeDtypeStruct((3, S, D), BF16)],
        compiler_params=_cparams(("parallel",), 56),
    )(o_attn, o_conv, o_mem, P, P, P, wpa, wpc, wpm, w_out)


def _merge_bwd(dmerged, P, pb, *, name, tm=256):
    S = P.shape[0]
    D = D_MODEL
    tm = _tile(S, tm)

    def body(dm_ref, g0_ref, g1_ref, g2_ref, pb_ref, dpb_ref, dgl_ref):
        dm = dm_ref[...].astype(F32)
        for b, g_ref in enumerate((g0_ref, g1_ref, g2_ref)):
            g = _sigmoid(g_ref[...].astype(F32))
            dpb_ref[b] = (dm * g).astype(dpb_ref.dtype)
            dgl_ref[:, b * D:(b + 1) * D] = (dm * pb_ref[b].astype(F32) * g * (1.0 - g)).astype(dgl_ref.dtype)

    row = pl.BlockSpec((tm, D), lambda i: (i, 0))
    blk3 = pl.BlockSpec((3, tm, D), lambda i: (0, i, 0))
    return pl.pallas_call(
        body, name=name, grid=(S // tm,),
        in_specs=[row] + [pl.BlockSpec((tm, D), lambda i, b=b: (i, PB_G0 + b)) for b in range(3)] + [blk3],
        out_specs=[blk3, pl.BlockSpec((tm, 3 * D), lambda i: (i, 0))],
        out_shape=[jax.ShapeDtypeStruct((3, S, D), BF16), jax.ShapeDtypeStruct((S, 3 * D), BF16)],
        compiler_params=_cparams(("parallel",), 56),
    )(dmerged, P, P, P, pb)


def _resid_norm(x, y1, g_post, g_pre, *, name, tm=512):
    S, D = x.shape
    tm = _tile(S, tm)

    def body(x_ref, y_ref, gp_ref, gq_ref, x1_ref, h2_ref):
        yv = y_ref[...]
        x1 = x_ref[...] + yv * lax.rsqrt(jnp.mean(yv * yv, axis=-1, keepdims=True) + RMS_EPS) * gp_ref[...]
        x1_ref[...] = x1
        h2_ref[...] = (x1 * lax.rsqrt(jnp.mean(x1 * x1, axis=-1, keepdims=True) + RMS_EPS) * gq_ref[...]).astype(h2_ref.dtype)

    row = pl.BlockSpec((tm, D), lambda i: (i, 0))
    vec = pl.BlockSpec((1, D), lambda i: (0, 0))
    return pl.pallas_call(
        body, name=name, grid=(S // tm,),
        in_specs=[row, row, vec, vec], out_specs=[row, row],
        out_shape=[jax.ShapeDtypeStruct((S, D), F32), jax.ShapeDtypeStruct((S, D), BF16)],
        compiler_params=_cparams(("parallel",)),
    )(x, y1, g_post, g_pre)


def _swiglu_fwd(h2, w_gu, *, name, tm=512, tn=1408):
    S, D = h2.shape
    Fh = w_gu.shape[1] // 2
    tm, tn = _tile(S, tm), _tile(Fh, tn)
    nj = Fh // tn

    def body(h_ref, wg_ref, wu_ref, g_ref, u_ref, a_ref):
        hv = h_ref[...]
        g = jnp.dot(hv, wg_ref[...], preferred_element_type=F32)
        u = jnp.dot(hv, wu_ref[...], preferred_element_type=F32)
        g_ref[...] = g.astype(g_ref.dtype)
        u_ref[...] = u.astype(u_ref.dtype)
        a_ref[...] = (g * _sigmoid(g) * u).astype(a_ref.dtype)

    out = pl.BlockSpec((tm, tn), lambda j, i: (i, j))
    sds = jax.ShapeDtypeStruct((S, Fh), BF16)
    return pl.pallas_call(
        body, name=name, grid=(nj, S // tm),
        in_specs=[
            pl.BlockSpec((tm, D), lambda j, i: (i, 0)),
            pl.BlockSpec((D, tn), lambda j, i: (0, j)),
            pl.BlockSpec((D, tn), lambda j, i: (0, j + nj)),
        ],
        out_specs=[out, out, out], out_shape=[sds, sds, sds],
        compiler_params=_cparams(("parallel", "parallel"), 56),
    )(h2, w_gu, w_gu)


def _swiglu_bwd(dact, g, u, *, name, tm=512, tn=1408):
    S, Fh = g.shape
    tm, tn = _tile(S, tm), _tile(Fh, tn)
    nj = Fh // tn

    def body(da_ref, g_ref, u_ref, o_ref):
        j = pl.program_id(1)
        da = da_ref[...].astype(F32)
        gv = g_ref[...].astype(F32)
        sg = _sigmoid(gv)

        @pl.when(j < nj)
        def _():
            o_ref[...] = (da * u_ref[...].astype(F32) * (sg * (1.0 + gv * (1.0 - sg)))).astype(o_ref.dtype)

        @pl.when(j >= nj)
        def _():
            o_ref[...] = (da * gv * sg).astype(o_ref.dtype)

    blk = pl.BlockSpec((tm, tn), lambda i, j: (i, j % nj))
    return pl.pallas_call(
        body, name=name, grid=(S // tm, 2 * nj),
        in_specs=[blk, blk, blk],
        out_specs=pl.BlockSpec((tm, tn), lambda i, j: (i, j)),
        out_shape=jax.ShapeDtypeStruct((S, 2 * Fh), BF16),
        compiler_params=_cparams(("parallel", "arbitrary")),
    )(dact, g, u)


def _final(x1, ffn, target, g, *, name, tm=512):
    S, D = x1.shape
    tm = _tile(S, tm)

    def body(x_ref, f_ref, t_ref, g_ref, dout_ref, dffn_ref, loss_ref, dg_ref):
        i = pl.program_id(0)
        fv = f_ref[...]
        rstd = lax.rsqrt(jnp.mean(fv * fv, axis=-1, keepdims=True) + RMS_EPS)
        r = fv * rstd
        e = x_ref[...] + r * g_ref[...] - t_ref[...]
        dout = e * (1.0 / D)
        dout_ref[...] = dout
        gy = dout * g_ref[...]
        dffn_ref[...] = (rstd * (gy - r * jnp.mean(gy * r, axis=-1, keepdims=True))).astype(dffn_ref.dtype)
        lpart = jnp.full((8, LANES), 0.5 * jnp.sum(jnp.mean(e * e, axis=-1, keepdims=True)), F32)
        gpart = jnp.sum(dout * r, axis=0, keepdims=True)

        @pl.when(i == 0)
        def _():
            loss_ref[...] = lpart
            dg_ref[...] = gpart

        @pl.when(i > 0)
        def _():
            loss_ref[...] += lpart
            dg_ref[...] += gpart

    row = pl.BlockSpec((tm, D), lambda i: (i, 0))
    vec = pl.BlockSpec((1, D), lambda i: (0, 0))
    return pl.pallas_call(
        body, name=name, grid=(S // tm,),
        in_specs=[row, row, row, vec],
        out_specs=[row, row, pl.BlockSpec((8, LANES), lambda i: (0, 0)), vec],
        out_shape=[jax.ShapeDtypeStruct((S, D), F32), jax.ShapeDtypeStruct((S, D), BF16),
                   jax.ShapeDtypeStruct((8, LANES), F32), jax.ShapeDtypeStruct((1, D), F32)],
        compiler_params=_cparams(("arbitrary",)),
    )(x1, ffn, target, g)


def _local_step(x, mem, target, gains, w, *, tq=512):
    S = x.shape[0]
    b_pad = jnp.pad(gains["b_forget"], ((0, 0), (0, LANES - FOX_HEADS)))

    h = _rms_fwd(x, gains["norm_mix_pre"], name="rms_mix_pre")
    P = _matmul(h, w["wp"], mode="nn", out_dtype=BF16, name="proj_in")
    f_logit = _matmul(h, w["wf"], mode="nn", out_dtype=F32, name="proj_forget")
    logf = _forget_fwd(f_logit, b_pad, name="forget_fwd")
    c_row16 = _cumsum_lanes(logf[:, :FOX_HEADS].T, reverse=False, name="forget_cumsum")
    ccol = c_row16[:, :, None]
    crow = c_row16.reshape(HEAD_PAIRS, 2, S)
    o_attn, o_attn32, lse = _attn_fwd(P, ccol, crow, tq=tq, name="attn_fwd")
    o_conv, y_conv = _conv_fwd(P, w["conv_w"], gains["conv_b"], gains["conv_ln_g"], gains["conv_ln_b"], name="conv_fwd")
    mem_n = _rms_fwd(mem, gains["norm_mem"], name="rms_mem")
    kv = _matmul(mem_n, w["w_kv"], mode="nn", out_dtype=BF16, name="mem_kv")
    o_mem = _mem_fwd(P, kv, name="mem_fwd")
    merged, y1, pb = _merge_fwd(o_attn, o_conv, o_mem, P, w["wpa"], w["wpc"], w["wpm"], w["w_out"], name="merge_fwd")
    x1, h2 = _resid_norm(x, y1, gains["norm_mix_post"], gains["norm_ffn_pre"], name="resid_norm")
    g_ffn, u_ffn, act = _swiglu_fwd(h2, w["w_gu"], name="swiglu_fwd")
    ffn = _matmul(act, w["w_down"], mode="nn", out_dtype=F32, name="ffn_down", tk=1408)

    dout, dffn, loss_tile, d_norm_ffn_post = _final(x1, ffn, target, gains["norm_ffn_post"], name="loss_head")
    dact = _matmul(dffn, w["w_down"], mode="nt", out_dtype=BF16, name="d_act", tn=1408)
    d_w_down = _matmul(act, dffn, mode="tn", out_dtype=F32, name="dw_down", tm=1408)
    dgu = _swiglu_bwd(dact, g_ffn, u_ffn, name="swiglu_bwd")
    dh2 = _matmul(dgu, w["w_gu"], mode="nt", out_dtype=F32, name="d_h2", tk=1408)
    d_w_gu = _matmul(h2, dgu, mode="tn", out_dtype=F32, name="dw_gate_up", tn=1408)
    dx1, d_norm_ffn_pre = _rms_bwd(x1, [dh2], gains["norm_ffn_pre"], dout, out_dtype=F32, name="rms_ffn_pre_bwd")

    dy1, d_norm_mix_post = _rms_bwd(y1, [dx1], gains["norm_mix_post"], None, out_dtype=BF16, name="rms_mix_post_bwd")
    dmerged = _matmul(dy1, w["w_out"], mode="nt", out_dtype=F32, name="d_merged")
    d_w_out = _matmul(merged, dy1, mode="tn", out_dtype=F32, name="dw_out")
    dpb, dgl = _merge_bwd(dmerged, P, pb, name="merge_bwd")
    do_attn = _matmul(dpb[0], w["wpa"], mode="nt", out_dtype=BF16, name="d_o_attn")
    do_conv = _matmul(dpb[1], w["wpc"], mode="nt", out_dtype=F32, name="d_o_conv")
    do_mem = _matmul(dpb[2], w["wpm"], mode="nt", out_dtype=BF16, name="d_o_mem")
    d_wpa = _matmul(o_attn, dpb[0], mode="tn", out_dtype=F32, name="dw_proj_attn")
    d_wpc = _matmul(o_conv, dpb[1], mode="tn", out_dtype=F32, name="dw_proj_conv")
    d_wpm = _matmul(o_mem, dpb[2], mode="tn", out_dtype=F32, name="dw_proj_mem")

    dq_mem, dkv = _mem_bwd(P, do_mem, kv, name="mem_bwd")
    dkv_b = dkv.astype(BF16)
    d_w_kv = _matmul(mem_n, dkv_b, mode="tn", out_dtype=F32, name="dw_kv")
    dmem_n = _matmul(dkv_b, w["w_kv"], mode="nt", out_dtype=F32, name="d_mem_n")
    _, d_norm_mem = _rms_bwd(mem, [dmem_n], gains["norm_mem"], None, out_dtype=BF16, name="rms_mem_bwd")

    dglu, d_conv_w, conv_small = _conv_bwd(P, do_conv, y_conv, w["conv_w"], gains["conv_ln_g"], gains["conv_ln_b"], name="conv_bwd")

    delta = _attn_delta(do_attn, o_attn32, name="attn_delta")
    dq, dk, dv, dc = _attn_bwd(P, do_attn, lse, delta, ccol, crow, tq=tq, name="attn_bwd")
    dlogf16 = _cumsum_lanes(dc.reshape(FOX_HEADS, S), reverse=True, name="forget_cumsum_bwd")
    dlogf = jnp.pad(dlogf16.T, ((0, 0), (0, LANES - FOX_HEADS)))
    df, d_b_pad = _forget_bwd(f_logit, b_pad, dlogf, name="forget_bwd")

    dP = jnp.concatenate([dq, dk.astype(BF16), dv.astype(BF16), dglu, dq_mem, dgl], axis=1)
    dh_p = _matmul(dP, w["wp"], mode="nt", out_dtype=F32, name="d_h")
    dh_f = _matmul(df, w["wf"], mode="nt", out_dtype=F32, name="d_h_forget")
    d_wp = _matmul(h, dP, mode="tn", out_dtype=F32, name="dw_in")
    d_wf = _matmul(h, df, mode="tn", out_dtype=F32, name="dw_forget")
    grad_x, d_norm_mix_pre = _rms_bwd(x, [dh_p, dh_f], gains["norm_mix_pre"], dx1, out_dtype=F32, name="rms_mix_pre_bwd")

    grads = dict(wp=d_wp, wf=d_wf, conv_w=d_conv_w, w_kv=d_w_kv, wpa=d_wpa, wpc=d_wpc, wpm=d_wpm, w_out=d_w_out,
                 w_gu=d_w_gu, w_down=d_w_down)
    small = dict(norm_mix_pre=d_norm_mix_pre, norm_mix_post=d_norm_mix_post, norm_mem=d_norm_mem,
                 conv_b=conv_small[2:3], conv_ln_g=conv_small[0:1], conv_ln_b=conv_small[1:2],
                 norm_ffn_pre=d_norm_ffn_pre, norm_ffn_post=d_norm_ffn_post, b_forget=d_b_pad[:, :FOX_HEADS])
    return loss_tile, grad_x, grads, small


def _mesh_pos():
    return lax.axis_index("x"), lax.axis_index("y"), lax.axis_index("c")


def _flip(pos, d):
    x, y, c = pos
    return (1 - x if d & 4 else x, 1 - y if d & 2 else y, 1 - c if d & 1 else c)


def _flat(pos):
    x, y, c = pos
    return 4 * x + 2 * y + c


def _exchange(src, *, scatter, name):
    R, C = src.shape[-2:]

    def body(src_ref, out_ref, send_sems, recv_sems, local_sem):
        me = _mesh_pos()
        my = _flat(me)

        def copy(d):
            peer = _flip(me, d)
            return pltpu.make_async_remote_copy(
                src_ref=src_ref.at[_flat(peer)] if scatter else src_ref,
                dst_ref=out_ref.at[my],
                send_sem=send_sems.at[d - 1], recv_sem=recv_sems.at[d - 1],
                device_id=peer, device_id_type=pl.DeviceIdType.MESH)

        def arrival(d):
            peer = _flip(me, d)
            return pltpu.make_async_remote_copy(
                src_ref=src_ref.at[my] if scatter else src_ref,
                dst_ref=out_ref.at[_flat(peer)],
                send_sem=send_sems.at[d - 1], recv_sem=recv_sems.at[d - 1],
                device_id=peer, device_id_type=pl.DeviceIdType.MESH)

        local = pltpu.make_async_copy(src_ref.at[my] if scatter else src_ref, out_ref.at[my], local_sem)
        local.start()
        sends = [copy(d) for d in range(1, N_DEV)]
        for cp in sends:
            cp.start()
        for d in range(1, N_DEV):
            arrival(d).wait_recv()
        for cp in sends:
            cp.wait_send()
        local.wait()

    return pl.pallas_call(
        body, name=name,
        in_specs=[pl.BlockSpec(memory_space=pl.ANY)],
        out_specs=pl.BlockSpec(memory_space=pl.ANY),
        out_shape=jax.ShapeDtypeStruct((N_DEV, R, C), src.dtype),
        scratch_shapes=[pltpu.SemaphoreType.DMA((N_DEV - 1,)), pltpu.SemaphoreType.DMA((N_DEV - 1,)), pltpu.SemaphoreType.DMA(())],
    )(src)


def _sum_slots(slots, *, name, tr=256):
    n, R, C = slots.shape
    tr = _tile(R, tr)

    def body(s_ref, o_ref):
        acc = s_ref[0].astype(F32)
        for j in range(1, n):
            acc = acc + s_ref[j].astype(F32)
        o_ref[...] = acc

    return pl.pallas_call(
        body, name=name, grid=(R // tr,),
        in_specs=[pl.BlockSpec((n, tr, C), lambda i: (0, i, 0))],
        out_specs=pl.BlockSpec((tr, C), lambda i: (i, 0)),
        out_shape=jax.ShapeDtypeStruct((R, C), F32),
        compiler_params=_cparams(("parallel",)),
    )(slots)


def _adamw(w, g, m, v, *, name):
    R, C = w.shape
    tr = R if R <= 512 else 256
    assert R % tr == 0, (R, tr)

    def body(w_ref, g_ref, m_ref, v_ref, d_ref, m2_ref, v2_ref):
        gv = g_ref[...]
        m2 = ADAM_B1 * m_ref[...] + (1.0 - ADAM_B1) * gv
        v2 = ADAM_B2 * v_ref[...] + (1.0 - ADAM_B2) * (gv * gv)
        m_hat = m2 / (1.0 - ADAM_B1 ** ADAM_STEP)
        v_hat = v2 / (1.0 - ADAM_B2 ** ADAM_STEP)
        d_ref[...] = -ADAM_LR * (m_hat / (jnp.sqrt(v_hat) + ADAM_EPS) + ADAM_WD * w_ref[...])
        m2_ref[...] = m2
        v2_ref[...] = v2

    blk = pl.BlockSpec((tr, C), lambda i: (i, 0))
    sds = jax.ShapeDtypeStruct((R, C), F32)
    return pl.pallas_call(
        body, name=name, grid=(R // tr,),
        in_specs=[blk] * 4, out_specs=[blk] * 3, out_shape=[sds] * 3,
        compiler_params=_cparams(("parallel",)),
    )(w, g, m, v)


SLAB_COLS = 1024
SHARDED = (
    ("w_in", (1024, 1154), "col"), ("conv_w", (31, 128), "col"), ("w_kv_mem", (1024, 256), "col"),
    ("w_proj_attn", (128, 1024), "row"), ("w_proj_conv", (128, 1024), "row"), ("w_proj_mem", (128, 1024), "row"),
    ("w_out", (128, 1024), "row"), ("w_gate_up", (1024, 704), "col"), ("w_down", (352, 1024), "row"),
)
SMALL = ("norm_mix_pre", "norm_mix_post", "norm_mem", "conv_b", "conv_ln_g", "conv_ln_b", "norm_ffn_pre", "norm_ffn_post", "b_forget")
SMALL_ROWS = 16
LOSS_ROW = len(SMALL)
WEIGHT_ORDER = ("norm_mix_pre", "norm_mix_post", "norm_mem", "w_in", "b_forget", "conv_w", "conv_b", "conv_ln_g", "conv_ln_b",
                "w_kv_mem", "w_proj_attn", "w_proj_conv", "w_proj_mem", "w_out", "norm_ffn_pre", "norm_ffn_post", "w_gate_up", "w_down")


def _pack(blocks, widths):
    lead = blocks[SHARDED[0][0]].shape[:-2]
    flat = [blocks[name].reshape(lead + (widths[name],)) for name, _, _ in SHARDED]
    total = sum(widths.values())
    rows = -(-total // (SLAB_COLS * 16)) * 16
    flat.append(jnp.zeros(lead + (rows * SLAB_COLS - total,), flat[0].dtype))
    return jnp.concatenate(flat, axis=-1).reshape(lead + (rows, SLAB_COLS))


def _unpack(slab, shapes):
    lead = slab.shape[:-2]
    flat = slab.reshape(lead + (-1,))
    out, off = {}, 0
    for name, _, _ in SHARDED:
        r, c = shapes[name]
        out[name] = flat[..., off:off + r * c].reshape(lead + (r, c))
        off += r * c
    return out


def _to_full(blocks8, kind):
    n, r, c = blocks8.shape
    return blocks8.transpose(1, 0, 2).reshape(r, n * c) if kind == "col" else blocks8.reshape(n * r, c)


def _to_blocks(full, kind):
    if kind == "col":
        r, nc = full.shape
        return full.reshape(r, N_DEV, nc // N_DEV).transpose(1, 0, 2)
    nr, c = full.shape
    return full.reshape(N_DEV, nr // N_DEV, c)


def kernel(x, mem, norm_mix_pre, norm_mix_post, norm_mem, w_in, b_forget, conv_w, conv_b, conv_ln_g, conv_ln_b, w_kv_mem, w_proj_attn, w_proj_conv, w_proj_mem, w_out, norm_ffn_pre, norm_ffn_post, w_gate_up, w_down, loss_target, m_norm_mix_pre, m_norm_mix_post, m_norm_mem, m_w_in, m_b_forget, m_conv_w, m_conv_b, m_conv_ln_g, m_conv_ln_b, m_w_kv_mem, m_w_proj_attn, m_w_proj_conv, m_w_proj_mem, m_w_out, m_norm_ffn_pre, m_norm_ffn_post, m_w_gate_up, m_w_down, v_norm_mix_pre, v_norm_mix_post, v_norm_mem, v_w_in, v_b_forget, v_conv_w, v_conv_b, v_conv_ln_g, v_conv_ln_b, v_w_kv_mem, v_w_proj_attn, v_w_proj_conv, v_w_proj_mem, v_w_out, v_norm_ffn_pre, v_norm_ffn_post, v_w_gate_up, v_w_down):
    given = dict(locals())
    weights = {n: given[n] for n in WEIGHT_ORDER}
    moments_m = {n: given["m_" + n] for n in WEIGHT_ORDER}
    moments_v = {n: given["v_" + n] for n in WEIGHT_ORDER}
    shard_shape = {name: shape for name, shape, _ in SHARDED}
    kind = {name: k for name, _, k in SHARDED}

    send_shape = dict(shard_shape, conv_w=(31, 256))
    send_width = {n: r * c for n, (r, c) in send_shape.items()}
    blocks = {n: weights[n][0].astype(BF16) for n, _, _ in SHARDED if n != "conv_w"}
    blocks["conv_w"] = lax.bitcast_convert_type(conv_w[0], BF16).reshape(31, 256)
    gathered = _unpack(_exchange(_pack(blocks, send_width), scatter=False, name="gather_weights"), send_shape)
    conv_w_all = lax.bitcast_convert_type(gathered["conv_w"].reshape(N_DEV, 31, 128, 2), F32)
    full = {n: _to_full(gathered[n], kind[n]) for n, _, _ in SHARDED if n != "conv_w"}
    conv_w_full = _to_full(conv_w_all, "col")
    n_qkv = 3 * D_MODEL
    w_in_full = full["w_in"]
    wdict = dict(
        wp=jnp.concatenate([w_in_full[:, :n_qkv], w_in_full[:, n_qkv + FOX_HEADS:]], axis=1),
        wf=jnp.pad(w_in_full[:, n_qkv:n_qkv + FOX_HEADS], ((0, 0), (0, LANES - FOX_HEADS))),
        conv_w=jnp.pad(conv_w_full, ((0, CONV_HALO - CONV_WIDTH), (0, 0))),
        w_kv=full["w_kv_mem"], wpa=full["w_proj_attn"], wpc=full["w_proj_conv"], wpm=full["w_proj_mem"],
        w_out=full["w_out"], w_gu=full["w_gate_up"], w_down=full["w_down"],
    )
    gains = {n: weights[n] for n in SMALL}

    loss_tile, grad_x, g, small = _local_step(x[0], mem[0], loss_target[0], gains, wdict)

    g_full = dict(
        w_in=jnp.concatenate([g["wp"][:, :n_qkv], g["wf"][:, :FOX_HEADS], g["wp"][:, n_qkv:]], axis=1),
        conv_w=g["conv_w"][:CONV_WIDTH], w_kv_mem=g["w_kv"], w_proj_attn=g["wpa"], w_proj_conv=g["wpc"],
        w_proj_mem=g["wpm"], w_out=g["w_out"], w_gate_up=g["w_gu"], w_down=g["w_down"],
    )
    width = {n: r * c for n, (r, c) in shard_shape.items()}
    g_slab = _pack({n: _to_blocks(g_full[n], kind[n]).astype(BF16) for n, _, _ in SHARDED}, width)
    g_recv = _exchange(g_slab, scatter=True, name="scatter_grads")
    grads = _unpack(_sum_slots(g_recv, name="sum_grads", tr=176), shard_shape)

    rows = [jnp.pad(small[n], ((0, 0), (0, D_MODEL - small[n].shape[1]))) for n in SMALL]
    rows.append(jnp.broadcast_to(loss_tile[0:1, 0:1], (1, D_MODEL)))
    rows.append(jnp.zeros((SMALL_ROWS - len(rows), D_MODEL), F32))
    small_sum = _sum_slots(_exchange(jnp.concatenate(rows, axis=0), scatter=False, name="gather_small"), name="sum_small")
    loss = small_sum[LOSS_ROW, 0]
    for i, n in enumerate(SMALL):
        grads[n] = small_sum[i:i + 1, :weights[n].shape[1]]

    def slab_of(d, fill):
        rows = [jnp.pad(d[n], ((0, 0), (0, D_MODEL - d[n].shape[1])), constant_values=fill) for n in SMALL]
        rows.append(jnp.full((SMALL_ROWS - len(rows), D_MODEL), fill, F32))
        return jnp.concatenate(rows, axis=0)

    delta, new_m, new_v = {}, {}, {}
    small_g = jnp.where(lax.broadcasted_iota(jnp.int32, (SMALL_ROWS, 1), 0) < LOSS_ROW, small_sum, 0.0)
    sd, sm, sv = _adamw(slab_of(weights, 0.0), small_g, slab_of(moments_m, 0.0), slab_of(moments_v, 1.0), name="adamw_small")
    for i, n in enumerate(SMALL):
        c = weights[n].shape[1]
        delta[n], new_m[n], new_v[n] = sd[i:i + 1, :c], sm[i:i + 1, :c], sv[i:i + 1, :c]
    for n, _, _ in SHARDED:
        d, m2, v2 = _adamw(weights[n][0], grads[n], moments_m[n][0], moments_v[n][0], name="adamw_" + n)
        delta[n], new_m[n], new_v[n] = d[None], m2[None], v2[None]
        grads[n] = grads[n][None]

    return (loss, grad_x[None], *[grads[n] for n in WEIGHT_ORDER], *[delta[n] for n in WEIGHT_ORDER],
            *[new_m[n] for n in WEIGHT_ORDER], *[new_v[n] for n in WEIGHT_ORDER])
```

```python
import functools

import jax
import jax.numpy as jnp
from jax import lax
from jax.experimental import pallas as pl
from jax.experimental.pallas import tpu as pltpu

F32 = jnp.float32
BF16 = jnp.bfloat16

D_MODEL = 1024
N_DEV = 8
FOX_HEADS = 16
FOX_HEAD_DIM = 64
HEAD_PAIRS = FOX_HEADS // 2
MEM_HEADS = 4
MEM_HEAD_DIM = D_MODEL // MEM_HEADS
CONV_WIDTH = 31
CONV_HALO = 32
FFN_HIDDEN = 2816
RMS_EPS = 1e-6
LN_EPS = 1e-5
ADAM_LR = 0.001
ADAM_B1 = 0.9
ADAM_B2 = 0.999
ADAM_EPS = 1e-08
ADAM_WD = 0.01
ADAM_STEP = 10
NEG_BIG = -1e30
LANES = 128

PB_Q, PB_K, PB_V, PB_A, PB_GATE, PB_QMEM, PB_G0 = 0, 1, 2, 3, 4, 5, 6
P_WIDTH = 9 * D_MODEL

NT_DIMS = (((1,), (1,)), ((), ()))
TN_DIMS = (((0,), (0,)), ((), ()))


def _cparams(sem, vmem_mb=None):
    kw = dict(dimension_semantics=sem)
    if vmem_mb is not None:
        kw["vmem_limit_bytes"] = vmem_mb * 1024 * 1024
    return pltpu.CompilerParams(**kw)


def _tile(dim, want):
    t = min(dim, want)
    assert dim % t == 0, (dim, want)
    return t


def _sigmoid(z):
    return 1.0 / (1.0 + jnp.exp(-z))


def _matmul(a, b, *, mode, out_dtype, name, tm=1024, tn=1024, tk=1024):
    if mode == "nn":
        (M, K), (K2, N) = a.shape, b.shape
    elif mode == "nt":
        (M, K), (N, K2) = a.shape, b.shape
    else:
        (K, M), (K2, N) = a.shape, b.shape
    assert K == K2, (a.shape, b.shape, mode)
    tm, tn, tk = _tile(M, tm), _tile(N, tn), _tile(K, tk)
    nk = K // tk
    dims = {"nn": (((1,), (0,)), ((), ())), "nt": NT_DIMS, "tn": TN_DIMS}[mode]

    def body(a_ref, b_ref, o_ref, *scratch):
        part = lax.dot_general(a_ref[...], b_ref[...], dims, preferred_element_type=F32)
        if nk == 1:
            o_ref[...] = part.astype(o_ref.dtype)
        else:
            acc_ref, = scratch
            k = pl.program_id(2)

            @pl.when(k == 0)
            def _():
                acc_ref[...] = part

            @pl.when(k > 0)
            def _():
                acc_ref[...] += part

            @pl.when(k == nk - 1)
            def _():
                o_ref[...] = acc_ref[...].astype(o_ref.dtype)

    a_spec = pl.BlockSpec((tk, tm), lambda j, i, k: (k, i)) if mode == "tn" else pl.BlockSpec((tm, tk), lambda j, i, k: (i, k))
    b_spec = pl.BlockSpec((tn, tk), lambda j, i, k: (j, k)) if mode == "nt" else pl.BlockSpec((tk, tn), lambda j, i, k: (k, j))
    return pl.pallas_call(
        body,
        name=name,
        grid=(N // tn, M // tm, nk),
        in_specs=[a_spec, b_spec],
        out_specs=pl.BlockSpec((tm, tn), lambda j, i, k: (i, j)),
        out_shape=jax.ShapeDtypeStruct((M, N), out_dtype),
        scratch_shapes=[pltpu.VMEM((tm, tn), F32)] if nk > 1 else [],
        compiler_params=_cparams(("parallel", "parallel", "arbitrary"), 56),
    )(a, b)


def _rms_fwd(x, g, *, name, tm=512):
    S, D = x.shape
    tm = _tile(S, tm)

    def body(x_ref, g_ref, o_ref):
        xv = x_ref[...]
        rstd = lax.rsqrt(jnp.mean(xv * xv, axis=-1, keepdims=True) + RMS_EPS)
        o_ref[...] = (xv * rstd * g_ref[...]).astype(o_ref.dtype)

    return pl.pallas_call(
        body, name=name, grid=(S // tm,),
        in_specs=[pl.BlockSpec((tm, D), lambda i: (i, 0)), pl.BlockSpec((1, D), lambda i: (0, 0))],
        out_specs=pl.BlockSpec((tm, D), lambda i: (i, 0)),
        out_shape=jax.ShapeDtypeStruct((S, D), BF16),
        compiler_params=_cparams(("parallel",)),
    )(x, g)


def _rms_bwd(xin, dys, g, res, *, out_dtype, name, tm=512):
    S, D = xin.shape
    tm = _tile(S, tm)
    n_dy = len(dys)
    has_res = res is not None

    def body(*refs):
        x_ref, g_ref = refs[0], refs[1]
        dy_refs = refs[2:2 + n_dy]
        pos = 2 + n_dy
        res_ref = refs[pos] if has_res else None
        pos += int(has_res)
        dx_ref, dg_ref = refs[pos], refs[pos + 1]
        i = pl.program_id(0)
        xv = x_ref[...]
        dy = dy_refs[0][...].astype(F32)
        for r in dy_refs[1:]:
            dy = dy + r[...].astype(F32)
        rstd = lax.rsqrt(jnp.mean(xv * xv, axis=-1, keepdims=True) + RMS_EPS)
        xhat = xv * rstd
        gy = dy * g_ref[...]
        dx = rstd * (gy - xhat * jnp.mean(gy * xhat, axis=-1, keepdims=True))
        if has_res:
            dx = dx + res_ref[...]
        dx_ref[...] = dx.astype(dx_ref.dtype)
        part = jnp.sum(dy * xhat, axis=0, keepdims=True)

        @pl.when(i == 0)
        def _():
            dg_ref[...] = part

        @pl.when(i > 0)
        def _():
            dg_ref[...] += part

    row = pl.BlockSpec((tm, D), lambda i: (i, 0))
    vec = pl.BlockSpec((1, D), lambda i: (0, 0))
    ins = [xin, g] + list(dys) + ([res] if has_res else [])
    return pl.pallas_call(
        body, name=name, grid=(S // tm,),
        in_specs=[row, vec] + [row] * n_dy + ([row] if has_res else []),
        out_specs=[row, vec],
        out_shape=[jax.ShapeDtypeStruct((S, D), out_dtype), jax.ShapeDtypeStruct((1, D), F32)],
        compiler_params=_cparams(("arbitrary",)),
    )(*ins)


def _head_mask(hh, shape):
    lane = lax.broadcasted_iota(jnp.int32, shape, len(shape) - 1)
    return (lane // FOX_HEAD_DIM) == hh


def _attn_fwd(P, ccol, crow, *, tq, name):
    S = P.shape[0]
    tq = _tile(S, tq)
    nq = S // tq
    scale = FOX_HEAD_DIM ** -0.5

    def body(q_ref, k_ref, v_ref, cc_ref, cr_ref, o_ref, o32_ref, lse_ref):
        i = pl.program_id(1)
        q = q_ref[...] * jnp.asarray(scale, BF16)
        row = lax.broadcasted_iota(jnp.int32, (tq, tq), 0)
        col = lax.broadcasted_iota(jnp.int32, (tq, tq), 1)
        causal = col <= row
        hms = [_head_mask(hh, (tq, LANES)) for hh in range(2)]
        qhs = [jnp.where(hm, q, jnp.zeros_like(q)) for hm in hms]
        cbs = [cc_ref[hh, 0:1, :] for hh in range(2)]

        def step(j, carry, masked):
            start = pl.multiple_of(j * tq, tq)
            kj = k_ref[pl.ds(start, tq), :]
            vj = v_ref[pl.ds(start, tq), :]
            new = []
            for hh in range(2):
                m, l, acc = carry[hh]
                bias = cbs[hh] - cr_ref[0, hh:hh + 1, pl.ds(start, tq)]
                s = lax.dot_general(qhs[hh], kj, NT_DIMS, preferred_element_type=F32) + bias
                if masked:
                    s = jnp.where(causal, s, NEG_BIG)
                m_new = jnp.maximum(m, jnp.max(s, axis=1, keepdims=True))
                alpha = jnp.exp(m - m_new)
                p = jnp.exp(s - m_new)
                l = alpha * l + jnp.sum(p, axis=1, keepdims=True)
                p_hi = p.astype(BF16)
                p_lo = (p - p_hi.astype(F32)).astype(BF16)
                acc = (alpha * acc + jnp.dot(p_hi, vj, preferred_element_type=F32)
                       + jnp.dot(p_lo, vj, preferred_element_type=F32))
                new.append((m_new, l, acc))
            return tuple(new)

        init = (jnp.full((tq, 1), NEG_BIG, F32), jnp.zeros((tq, 1), F32), jnp.zeros((tq, LANES), F32))
        carry = lax.fori_loop(0, i, functools.partial(step, masked=False), (init, init))
        carry = step(i, carry, True)
        out = jnp.where(hms[0], carry[0][2] / carry[0][1], carry[1][2] / carry[1][1])
        for hh in range(2):
            lse_ref[hh] = carry[hh][0] + jnp.log(carry[hh][1])
        o_ref[...] = out.astype(o_ref.dtype)
        o32_ref[...] = out

    nblk = D_MODEL // LANES
    return pl.pallas_call(
        body, name=name, grid=(HEAD_PAIRS, nq),
        in_specs=[
            pl.BlockSpec((tq, LANES), lambda p, i: (i, PB_Q * nblk + p)),
            pl.BlockSpec((S, LANES), lambda p, i: (0, PB_K * nblk + p)),
            pl.BlockSpec((S, LANES), lambda p, i: (0, PB_V * nblk + p)),
            pl.BlockSpec((2, tq, 1), lambda p, i: (p, i, 0)),
            pl.BlockSpec((1, 2, S), lambda p, i: (p, 0, 0)),
        ],
        out_specs=[
            pl.BlockSpec((tq, LANES), lambda p, i: (i, p)),
            pl.BlockSpec((tq, LANES), lambda p, i: (i, p)),
            pl.BlockSpec((2, tq, 1), lambda p, i: (p, i, 0)),
        ],
        out_shape=[jax.ShapeDtypeStruct((S, D_MODEL), BF16), jax.ShapeDtypeStruct((S, D_MODEL), F32),
                   jax.ShapeDtypeStruct((FOX_HEADS, S, 1), F32)],
        compiler_params=_cparams(("parallel", "arbitrary"), 56),
    )(P, P, P, ccol, crow)


def _attn_delta(do, o, *, name, tm=512):
    S = do.shape[0]
    tm = _tile(S, tm)

    def body(do_ref, o_ref, d_ref):
        prod = do_ref[...].astype(F32) * o_ref[...].astype(F32)
        for hh in range(2):
            d_ref[hh] = jnp.sum(jnp.where(_head_mask(hh, (tm, LANES)), prod, 0.0), axis=1, keepdims=True)

    blk = pl.BlockSpec((tm, LANES), lambda p, i: (i, p))
    return pl.pallas_call(
        body, name=name, grid=(HEAD_PAIRS, S // tm),
        in_specs=[blk, blk],
        out_specs=pl.BlockSpec((2, tm, 1), lambda p, i: (p, i, 0)),
        out_shape=jax.ShapeDtypeStruct((FOX_HEADS, S, 1), F32),
        compiler_params=_cparams(("parallel", "parallel")),
    )(do, o)


def _attn_bwd(P, do, lse, delta, ccol, crow, *, tq, name):
    S = P.shape[0]
    tq = _tile(S, tq)
    nq = S // tq
    scale = FOX_HEAD_DIM ** -0.5

    def body(q_ref, k_ref, v_ref, do_ref, lse_ref, dl_ref, cc_ref, cr_ref, dq_ref, dk_out, dv_out, dc_ref, dk_ref, dv_ref):
        i = pl.program_id(1)

        @pl.when(i == 0)
        def _():
            dk_ref[...] = jnp.zeros_like(dk_ref)
            dv_ref[...] = jnp.zeros_like(dv_ref)
            dc_ref[...] = jnp.zeros_like(dc_ref)

        q = q_ref[...] * jnp.asarray(scale, BF16)
        do_v = do_ref[...]
        row = lax.broadcasted_iota(jnp.int32, (tq, tq), 0)
        col = lax.broadcasted_iota(jnp.int32, (tq, tq), 1)
        causal = col <= row
        hms = [_head_mask(hh, (tq, LANES)) for hh in range(2)]
        qhs = [jnp.where(hm, q, jnp.zeros_like(q)) for hm in hms]
        dohs = [jnp.where(hm, do_v, jnp.zeros_like(do_v)) for hm in hms]
        cbs = [cc_ref[hh, 0:1, :] for hh in range(2)]
        lses = [lse_ref[hh] for hh in range(2)]
        dls = [dl_ref[hh] for hh in range(2)]

        def step(j, dq_acc, masked):
            start = pl.multiple_of(j * tq, tq)
            kj = k_ref[pl.ds(start, tq), :]
            vj = v_ref[pl.ds(start, tq), :]
            dv_part = jnp.zeros((tq, LANES), F32)
            dk_part = jnp.zeros((tq, LANES), F32)
            for hh in range(2):
                bias = cbs[hh] - cr_ref[0, hh:hh + 1, pl.ds(start, tq)]
                s = lax.dot_general(qhs[hh], kj, NT_DIMS, preferred_element_type=F32) + bias
                if masked:
                    s = jnp.where(causal, s, NEG_BIG)
                p = jnp.exp(s - lses[hh])
                dp = lax.dot_general(dohs[hh], vj, NT_DIMS, preferred_element_type=F32)
                ds = p * (dp - dls[hh])
                pb = p.astype(BF16)
                dsb = ds.astype(BF16)
                dv_part = dv_part + lax.dot_general(pb, dohs[hh], TN_DIMS, preferred_element_type=F32)
                dk_part = dk_part + lax.dot_general(dsb, qhs[hh], TN_DIMS, preferred_element_type=F32)
                dc_ref[0, hh:hh + 1, pl.ds(start, tq)] -= jnp.sum(ds, axis=0, keepdims=True)
                kh = jnp.where(hms[hh], kj, jnp.zeros_like(kj))
                dq_acc = dq_acc + jnp.dot(dsb, kh, preferred_element_type=F32)
            dv_ref[pl.ds(start, tq), :] += dv_part
            dk_ref[pl.ds(start, tq), :] += dk_part
            return dq_acc

        dq = lax.fori_loop(0, i, functools.partial(step, masked=False), jnp.zeros((tq, LANES), F32))
        dq = step(i, dq, True)
        dq_ref[...] = (dq * scale).astype(dq_ref.dtype)

        @pl.when(i == nq - 1)
        def _():
            dk_out[...] = dk_ref[...].astype(dk_out.dtype)
            dv_out[...] = dv_ref[...].astype(dv_out.dtype)

    nblk = D_MODEL // LANES
    qblk = pl.BlockSpec((tq, LANES), lambda p, i: (i, p))
    stat = pl.BlockSpec((2, tq, 1), lambda p, i: (p, i, 0))
    full = pl.BlockSpec((S, LANES), lambda p, i: (0, p))
    return pl.pallas_call(
        body, name=name, grid=(HEAD_PAIRS, nq),
        in_specs=[
            pl.BlockSpec((tq, LANES), lambda p, i: (i, PB_Q * nblk + p)),
            pl.BlockSpec((S, LANES), lambda p, i: (0, PB_K * nblk + p)),
            pl.BlockSpec((S, LANES), lambda p, i: (0, PB_V * nblk + p)),
            qblk, stat, stat, stat,
            pl.BlockSpec((1, 2, S), lambda p, i: (p, 0, 0)),
        ],
        out_specs=[qblk, full, full, pl.BlockSpec((1, 2, S), lambda p, i: (p, 0, 0))],
        out_shape=[
            jax.ShapeDtypeStruct((S, D_MODEL), BF16),
            jax.ShapeDtypeStruct((S, D_MODEL), BF16),
            jax.ShapeDtypeStruct((S, D_MODEL), BF16),
            jax.ShapeDtypeStruct((HEAD_PAIRS, 2, S), F32),
        ],
        scratch_shapes=[pltpu.VMEM((S, LANES), F32), pltpu.VMEM((S, LANES), F32)],
        compiler_params=_cparams(("parallel", "arbitrary"), 56),
    )(P, P, P, do, lse, delta, ccol, crow)


def _cumsum_lanes(x, *, reverse, name):
    R, S = x.shape
    nb = S // LANES

    def body(x_ref, o_ref):
        r = lax.broadcasted_iota(jnp.int32, (LANES, LANES), 0)
        c = lax.broadcasted_iota(jnp.int32, (LANES, LANES), 1)
        tri = ((r >= c) if reverse else (r <= c)).astype(F32)

        def step(b, carry):
            blk = (nb - 1 - b) if reverse else b
            start = pl.multiple_of(blk * LANES, LANES)
            xb = x_ref[:, pl.ds(start, LANES)]
            y = jnp.dot(xb, tri, precision=lax.Precision.HIGHEST, preferred_element_type=F32) + carry
            o_ref[:, pl.ds(start, LANES)] = y
            return carry + jnp.sum(xb, axis=1, keepdims=True)

        lax.fori_loop(0, nb, step, jnp.zeros((R, 1), F32))

    return pl.pallas_call(
        body, name=name,
        in_specs=[pl.BlockSpec(memory_space=pltpu.VMEM)],
        out_specs=pl.BlockSpec(memory_space=pltpu.VMEM),
        out_shape=jax.ShapeDtypeStruct((R, S), F32),
    )(x)


def _forget_fwd(f_logit, b_pad, *, name, tm=1024):
    S = f_logit.shape[0]
    tm = _tile(S, tm)

    def body(f_ref, b_ref, o_ref):
        z = f_ref[...] + b_ref[...]
        o_ref[...] = jnp.minimum(z, 0.0) - jnp.log(1.0 + jnp.exp(-jnp.abs(z)))

    blk = pl.BlockSpec((tm, LANES), lambda i: (i, 0))
    return pl.pallas_call(
        body, name=name, grid=(S // tm,),
        in_specs=[blk, pl.BlockSpec((1, LANES), lambda i: (0, 0))],
        out_specs=blk, out_shape=jax.ShapeDtypeStruct((S, LANES), F32),
        compiler_params=_cparams(("parallel",)),
    )(f_logit, b_pad)


def _forget_bwd(f_logit, b_pad, dlogf, *, name, tm=1024):
    S = f_logit.shape[0]
    tm = _tile(S, tm)

    def body(f_ref, b_ref, d_ref, o_ref, db_ref):
        i = pl.program_id(0)
        z = f_ref[...] + b_ref[...]
        dz = d_ref[...] * (1.0 - _sigmoid(z))
        o_ref[...] = dz.astype(o_ref.dtype)
        part = jnp.sum(dz, axis=0, keepdims=True)

        @pl.when(i == 0)
        def _():
            db_ref[...] = part

        @pl.when(i > 0)
        def _():
            db_ref[...] += part

    blk = pl.BlockSpec((tm, LANES), lambda i: (i, 0))
    vec = pl.BlockSpec((1, LANES), lambda i: (0, 0))
    return pl.pallas_call(
        body, name=name, grid=(S // tm,),
        in_specs=[blk, vec, blk], out_specs=[blk, vec],
        out_shape=[jax.ShapeDtypeStruct((S, LANES), BF16), jax.ShapeDtypeStruct((1, LANES), F32)],
        compiler_params=_cparams(("arbitrary",)),
    )(f_logit, b_pad, dlogf)


def _layernorm_stats(y):
    mu = jnp.mean(y, axis=-1, keepdims=True)
    yc = y - mu
    rstd = lax.rsqrt(jnp.mean(yc * yc, axis=-1, keepdims=True) + LN_EPS)
    return yc * rstd, rstd


def _conv_fwd(P, w_pad, conv_b, ln_g, ln_b, *, name, tm=256):
    S = P.shape[0]
    C = D_MODEL
    tm = _tile(S, tm)
    hb = tm // CONV_HALO

    def body(a_ref, gt_ref, ah_ref, gh_ref, w_ref, cb_ref, g_ref, b_ref, o_ref, y_ref, glu_ref):
        i = pl.program_id(0)
        halo = ah_ref[...].astype(F32) * _sigmoid(gh_ref[...].astype(F32))
        glu_ref[0:CONV_HALO, :] = jnp.where(i > 0, halo, 0.0)
        glu_ref[CONV_HALO:, :] = a_ref[...].astype(F32) * _sigmoid(gt_ref[...].astype(F32))
        acc = jnp.zeros((tm, C), F32)
        for k in range(CONV_WIDTH):
            acc = acc + w_ref[k:k + 1, :] * glu_ref[pl.ds(CONV_HALO - (CONV_WIDTH - 1) + k, tm), :]
        y = acc + cb_ref[...]
        y_ref[...] = y
        xhat, _ = _layernorm_stats(y)
        z = xhat * g_ref[...] + b_ref[...]
        o_ref[...] = (z * _sigmoid(z)).astype(o_ref.dtype)

    vec = pl.BlockSpec((1, C), lambda i: (0, 0))
    row = pl.BlockSpec((tm, C), lambda i: (i, 0))
    return pl.pallas_call(
        body, name=name, grid=(S // tm,),
        in_specs=[
            pl.BlockSpec((tm, C), lambda i: (i, PB_A)),
            pl.BlockSpec((tm, C), lambda i: (i, PB_GATE)),
            pl.BlockSpec((CONV_HALO, C), lambda i: (jnp.maximum(i * hb - 1, 0), PB_A)),
            pl.BlockSpec((CONV_HALO, C), lambda i: (jnp.maximum(i * hb - 1, 0), PB_GATE)),
            pl.BlockSpec((CONV_HALO, C), lambda i: (0, 0)),
            vec, vec, vec,
        ],
        out_specs=[row, row],
        out_shape=[jax.ShapeDtypeStruct((S, C), BF16), jax.ShapeDtypeStruct((S, C), F32)],
        scratch_shapes=[pltpu.VMEM((tm + CONV_HALO, C), F32)],
        compiler_params=_cparams(("parallel",), 56),
    )(P, P, P, P, w_pad, conv_b, ln_g, ln_b)


def _conv_bwd(P, do, y, w_pad, ln_g, ln_b, *, name, tm=256):
    S = P.shape[0]
    C = D_MODEL
    tm = _tile(S, tm)
    hb = tm // CONV_HALO
    n_tiles = S // tm
    last_halo = S // CONV_HALO - 1

    def body(a_ref, gt_ref, ah_ref, gh_ref, do_ref, y_ref, don_ref, yn_ref, w_ref, g_ref, b_ref,
             dglu_ref, dw_ref, small_ref, glu_ref, dy_ref):
        i = pl.program_id(0)

        @pl.when(i == 0)
        def _():
            dw_ref[...] = jnp.zeros_like(dw_ref)
            small_ref[...] = jnp.zeros_like(small_ref)

        def ln_bwd(do_v, y_v):
            xhat, rstd = _layernorm_stats(y_v)
            z = xhat * g_ref[...] + b_ref[...]
            sg = _sigmoid(z)
            dz = do_v * (sg * (1.0 + z * (1.0 - sg)))
            dxh = dz * g_ref[...]
            dy = rstd * (dxh - jnp.mean(dxh, axis=-1, keepdims=True) - xhat * jnp.mean(dxh * xhat, axis=-1, keepdims=True))
            return dy, dz, xhat

        dy, dz, xhat = ln_bwd(do_ref[...], y_ref[...])
        dy_next, _, _ = ln_bwd(don_ref[...], yn_ref[...])
        small_ref[0:1, :] += jnp.sum(dz * xhat, axis=0, keepdims=True)
        small_ref[1:2, :] += jnp.sum(dz, axis=0, keepdims=True)
        small_ref[2:3, :] += jnp.sum(dy, axis=0, keepdims=True)
        dy_ref[0:tm, :] = dy
        dy_ref[tm:, :] = jnp.where(i < n_tiles - 1, dy_next, 0.0)

        a = a_ref[...].astype(F32)
        sig = _sigmoid(gt_ref[...].astype(F32))
        halo = ah_ref[...].astype(F32) * _sigmoid(gh_ref[...].astype(F32))
        glu_ref[0:CONV_HALO, :] = jnp.where(i > 0, halo, 0.0)
        glu_ref[CONV_HALO:, :] = a * sig

        dg = jnp.zeros((tm, C), F32)
        for k in range(CONV_WIDTH):
            shifted = glu_ref[pl.ds(CONV_HALO - (CONV_WIDTH - 1) + k, tm), :]
            dw_ref[k:k + 1, :] += jnp.sum(dy * shifted, axis=0, keepdims=True)
            dg = dg + w_ref[k:k + 1, :] * dy_ref[pl.ds(CONV_WIDTH - 1 - k, tm), :]
        dglu_ref[:, 0:C] = (dg * sig).astype(dglu_ref.dtype)
        dglu_ref[:, C:] = (dg * a * sig * (1.0 - sig)).astype(dglu_ref.dtype)

    vec = pl.BlockSpec((1, C), lambda i: (0, 0))
    row = pl.BlockSpec((tm, C), lambda i: (i, 0))
    nxt = pl.BlockSpec((CONV_HALO, C), lambda i: (jnp.minimum((i + 1) * hb, last_halo), 0))
    return pl.pallas_call(
        body, name=name, grid=(n_tiles,),
        in_specs=[
            pl.BlockSpec((tm, C), lambda i: (i, PB_A)),
            pl.BlockSpec((tm, C), lambda i: (i, PB_GATE)),
            pl.BlockSpec((CONV_HALO, C), lambda i: (jnp.maximum(i * hb - 1, 0), PB_A)),
            pl.BlockSpec((CONV_HALO, C), lambda i: (jnp.maximum(i * hb - 1, 0), PB_GATE)),
            row, row, nxt, nxt,
            pl.BlockSpec((CONV_HALO, C), lambda i: (0, 0)),
            vec, vec,
        ],
        out_specs=[
            pl.BlockSpec((tm, 2 * C), lambda i: (i, 0)),
            pl.BlockSpec((CONV_HALO, C), lambda i: (0, 0)),
            pl.BlockSpec((8, C), lambda i: (0, 0)),
        ],
        out_shape=[
            jax.ShapeDtypeStruct((S, 2 * C), BF16),
            jax.ShapeDtypeStruct((CONV_HALO, C), F32),
            jax.ShapeDtypeStruct((8, C), F32),
        ],
        scratch_shapes=[pltpu.VMEM((tm + CONV_HALO, C), F32), pltpu.VMEM((tm + CONV_HALO, C), F32)],
        compiler_params=_cparams(("arbitrary",), 56),
    )(P, P, P, P, do, y, do, y, w_pad, ln_g, ln_b)


def _mem_softmax(qh, kh):
    s = lax.dot_general(qh, kh, NT_DIMS, preferred_element_type=F32)
    e = jnp.exp(s - jnp.max(s, axis=1, keepdims=True))
    return e / jnp.sum(e, axis=1, keepdims=True)


def _mem_fwd(P, kv, *, name, tm=512):
    S, M = P.shape[0], kv.shape[0]
    tm = _tile(S, tm)
    scale = MEM_HEAD_DIM ** -0.5

    def body(q_ref, k_ref, v_ref, o_ref):
        for h in range(MEM_HEADS):
            sl = slice(h * MEM_HEAD_DIM, (h + 1) * MEM_HEAD_DIM)
            qh = q_ref[:, sl] * jnp.asarray(scale, BF16)
            p = _mem_softmax(qh, k_ref[:, sl])
            o_ref[:, sl] = jnp.dot(p.astype(BF16), v_ref[:, sl], preferred_element_type=F32).astype(o_ref.dtype)

    return pl.pallas_call(
        body, name=name, grid=(S // tm,),
        in_specs=[
            pl.BlockSpec((tm, D_MODEL), lambda i: (i, PB_QMEM)),
            pl.BlockSpec((M, D_MODEL), lambda i: (0, 0)),
            pl.BlockSpec((M, D_MODEL), lambda i: (0, 1)),
        ],
        out_specs=pl.BlockSpec((tm, D_MODEL), lambda i: (i, 0)),
        out_shape=jax.ShapeDtypeStruct((S, D_MODEL), BF16),
        compiler_params=_cparams(("parallel",)),
    )(P, kv, kv)


def _mem_bwd(P, do, kv, *, name, tm=512):
    S, M = P.shape[0], kv.shape[0]
    tm = _tile(S, tm)
    scale = MEM_HEAD_DIM ** -0.5

    def body(q_ref, k_ref, v_ref, do_ref, dq_ref, dkv_ref):
        i = pl.program_id(0)

        @pl.when(i == 0)
        def _():
            dkv_ref[...] = jnp.zeros_like(dkv_ref)

        for h in range(MEM_HEADS):
            sl = slice(h * MEM_HEAD_DIM, (h + 1) * MEM_HEAD_DIM)
            slv = slice(D_MODEL + h * MEM_HEAD_DIM, D_MODEL + (h + 1) * MEM_HEAD_DIM)
            qh = q_ref[:, sl] * jnp.asarray(scale, BF16)
            kh, vh, doh = k_ref[:, sl], v_ref[:, sl], do_ref[:, sl]
            p = _mem_softmax(qh, kh)
            dp = lax.dot_general(doh, vh, NT_DIMS, preferred_element_type=F32)
            ds = p * (dp - jnp.sum(p * dp, axis=1, keepdims=True))
            dsb = ds.astype(BF16)
            dq_ref[:, sl] = (jnp.dot(dsb, kh, preferred_element_type=F32) * scale).astype(dq_ref.dtype)
            dkv_ref[:, sl] += lax.dot_general(dsb, qh, TN_DIMS, preferred_element_type=F32)
            dkv_ref[:, slv] += lax.dot_general(p.astype(BF16), doh, TN_DIMS, preferred_element_type=F32)

    row = pl.BlockSpec((tm, D_MODEL), lambda i: (i, 0))
    return pl.pallas_call(
        body, name=name, grid=(S // tm,),
        in_specs=[
            pl.BlockSpec((tm, D_MODEL), lambda i: (i, PB_QMEM)),
            pl.BlockSpec((M, D_MODEL), lambda i: (0, 0)),
            pl.BlockSpec((M, D_MODEL), lambda i: (0, 1)),
            row,
        ],
        out_specs=[row, pl.BlockSpec((M, 2 * D_MODEL), lambda i: (0, 0))],
        out_shape=[jax.ShapeDtypeStruct((S, D_MODEL), BF16), jax.ShapeDtypeStruct((M, 2 * D_MODEL), F32)],
        compiler_params=_cparams(("arbitrary",)),
    )(P, kv, kv, do)


def _merge_fwd(o_attn, o_conv, o_mem, P, wpa, wpc, wpm, w_out, *, name, tm=256):
    S = P.shape[0]
    D = D_MODEL
    tm = _tile(S, tm)

    def body(oa_ref, oc_ref, om_ref, g0_ref, g1_ref, g2_ref, wa_ref, wc_ref, wm_ref, wo_ref, mg_ref, y_ref, pb_ref):
        merged = jnp.zeros((tm, D), F32)
        for b, (o_ref, g_ref, w_ref) in enumerate(((oa_ref, g0_ref, wa_ref), (oc_ref, g1_ref, wc_ref), (om_ref, g2_ref, wm_ref))):
            pb = jnp.dot(o_ref[...], w_ref[...], preferred_element_type=F32)
            pb_ref[b] = pb.astype(pb_ref.dtype)
            merged = merged + _sigmoid(g_ref[...].astype(F32)) * pb
        mb = merged.astype(BF16)
        mg_ref[...] = mb
        y_ref[...] = jnp.dot(mb, wo_ref[...], preferred_element_type=F32)

    row = pl.BlockSpec((tm, D), lambda i: (i, 0))
    wsp = pl.BlockSpec((D, D), lambda i: (0, 0))
    return pl.pallas_call(
        body, name=name, grid=(S // tm,),
        in_specs=[row, row, row] + [pl.BlockSpec((tm, D), lambda i, b=b: (i, PB_G0 + b)) for b in range(3)] + [wsp] * 4,
        out_specs=[row, row, pl.BlockSpec((3, tm, D), lambda i: (0, i, 0))],
        out_shape=[jax.ShapeDtypeStruct((S, D), BF16), jax.ShapeDtypeStruct((S, D), F32), jax.ShapeDtypeStruct((3, S, D), BF16)],
        compiler_params=_cparams(("parallel",), 56),
    )(o_attn, o_conv, o_mem, P, P, P, wpa, wpc, wpm, w_out)


def _merge_bwd(dmerged, P, pb, *, name, tm=256):
    S = P.shape[0]
    D = D_MODEL
    tm = _tile(S, tm)

    def body(dm_ref, g0_ref, g1_ref, g2_ref, pb_ref, dpb_ref, dgl_ref):
        dm = dm_ref[...].astype(F32)
        for b, g_ref in enumerate((g0_ref, g1_ref, g2_ref)):
            g = _sigmoid(g_ref[...].astype(F32))
            dpb_ref[b] = (dm * g).astype(dpb_ref.dtype)
            dgl_ref[:, b * D:(b + 1) * D] = (dm * pb_ref[b].astype(F32) * g * (1.0 - g)).astype(dgl_ref.dtype)

    row = pl.BlockSpec((tm, D), lambda i: (i, 0))
    blk3 = pl.BlockSpec((3, tm, D), lambda i: (0, i, 0))
    return pl.pallas_call(
        body, name=name, grid=(S // tm,),
        in_specs=[row] + [pl.BlockSpec((tm, D), lambda i, b=b: (i, PB_G0 + b)) for b in range(3)] + [blk3],
        out_specs=[blk3, pl.BlockSpec((tm, 3 * D), lambda i: (i, 0))],
        out_shape=[jax.ShapeDtypeStruct((3, S, D), BF16), jax.ShapeDtypeStruct((S, 3 * D), BF16)],
        compiler_params=_cparams(("parallel",), 56),
    )(dmerged, P, P, P, pb)


def _resid_norm(x, y1, g_post, g_pre, *, name, tm=512):
    S, D = x.shape
    tm = _tile(S, tm)

    def body(x_ref, y_ref, gp_ref, gq_ref, x1_ref, h2_ref):
        yv = y_ref[...]
        x1 = x_ref[...] + yv * lax.rsqrt(jnp.mean(yv * yv, axis=-1, keepdims=True) + RMS_EPS) * gp_ref[...]
        x1_ref[...] = x1
        h2_ref[...] = (x1 * lax.rsqrt(jnp.mean(x1 * x1, axis=-1, keepdims=True) + RMS_EPS) * gq_ref[...]).astype(h2_ref.dtype)

    row = pl.BlockSpec((tm, D), lambda i: (i, 0))
    vec = pl.BlockSpec((1, D), lambda i: (0, 0))
    return pl.pallas_call(
        body, name=name, grid=(S // tm,),
        in_specs=[row, row, vec, vec], out_specs=[row, row],
        out_shape=[jax.ShapeDtypeStruct((S, D), F32), jax.ShapeDtypeStruct((S, D), BF16)],
        compiler_params=_cparams(("parallel",)),
    )(x, y1, g_post, g_pre)


def _swiglu_fwd(h2, w_gu, *, name, tm=512, tn=1408):
    S, D = h2.shape
    Fh = w_gu.shape[1] // 2
    tm, tn = _tile(S, tm), _tile(Fh, tn)
    nj = Fh // tn

    def body(h_ref, wg_ref, wu_ref, g_ref, u_ref, a_ref):
        hv = h_ref[...]
        g = jnp.dot(hv, wg_ref[...], preferred_element_type=F32)
        u = jnp.dot(hv, wu_ref[...], preferred_element_type=F32)
        g_ref[...] = g.astype(g_ref.dtype)
        u_ref[...] = u.astype(u_ref.dtype)
        a_ref[...] = (g * _sigmoid(g) * u).astype(a_ref.dtype)

    out = pl.BlockSpec((tm, tn), lambda j, i: (i, j))
    sds = jax.ShapeDtypeStruct((S, Fh), BF16)
    return pl.pallas_call(
        body, name=name, grid=(nj, S // tm),
        in_specs=[
            pl.BlockSpec((tm, D), lambda j, i: (i, 0)),
            pl.BlockSpec((D, tn), lambda j, i: (0, j)),
            pl.BlockSpec((D, tn), lambda j, i: (0, j + nj)),
        ],
        out_specs=[out, out, out], out_shape=[sds, sds, sds],
        compiler_params=_cparams(("parallel", "parallel"), 56),
    )(h2, w_gu, w_gu)


def _swiglu_bwd(dact, g, u, *, name, tm=512, tn=1408):
    S, Fh = g.shape
    tm, tn = _tile(S, tm), _tile(Fh, tn)
    nj = Fh // tn

    def body(da_ref, g_ref, u_ref, o_ref):
        j = pl.program_id(1)
        da = da_ref[...].astype(F32)
        gv = g_ref[...].astype(F32)
        sg = _sigmoid(gv)

        @pl.when(j < nj)
        def _():
            o_ref[...] = (da * u_ref[...].astype(F32) * (sg * (1.0 + gv * (1.0 - sg)))).astype(o_ref.dtype)

        @pl.when(j >= nj)
        def _():
            o_ref[...] = (da * gv * sg).astype(o_ref.dtype)

    blk = pl.BlockSpec((tm, tn), lambda i, j: (i, j % nj))
    return pl.pallas_call(
        body, name=name, grid=(S // tm, 2 * nj),
        in_specs=[blk, blk, blk],
        out_specs=pl.BlockSpec((tm, tn), lambda i, j: (i, j)),
        out_shape=jax.ShapeDtypeStruct((S, 2 * Fh), BF16),
        compiler_params=_cparams(("parallel", "arbitrary")),
    )(dact, g, u)


def _final(x1, ffn, target, g, *, name, tm=512):
    S, D = x1.shape
    tm = _tile(S, tm)

    def body(x_ref, f_ref, t_ref, g_ref, dout_ref, dffn_ref, loss_ref, dg_ref):
        i = pl.program_id(0)
        fv = f_ref[...]
        rstd = lax.rsqrt(jnp.mean(fv * fv, axis=-1, keepdims=True) + RMS_EPS)
        r = fv * rstd
        e = x_ref[...] + r * g_ref[...] - t_ref[...]
        dout = e * (1.0 / D)
        dout_ref[...] = dout
        gy = dout * g_ref[...]
        dffn_ref[...] = (rstd * (gy - r * jnp.mean(gy * r, axis=-1, keepdims=True))).astype(dffn_ref.dtype)
        lpart = jnp.full((8, LANES), 0.5 * jnp.sum(jnp.mean(e * e, axis=-1, keepdims=True)), F32)
        gpart = jnp.sum(dout * r, axis=0, keepdims=True)

        @pl.when(i == 0)
        def _():
            loss_ref[...] = lpart
            dg_ref[...] = gpart

        @pl.when(i > 0)
        def _():
            loss_ref[...] += lpart
            dg_ref[...] += gpart

    row = pl.BlockSpec((tm, D), lambda i: (i, 0))
    vec = pl.BlockSpec((1, D), lambda i: (0, 0))
    return pl.pallas_call(
        body, name=name, grid=(S // tm,),
        in_specs=[row, row, row, vec],
        out_specs=[row, row, pl.BlockSpec((8, LANES), lambda i: (0, 0)), vec],
        out_shape=[jax.ShapeDtypeStruct((S, D), F32), jax.ShapeDtypeStruct((S, D), BF16),
                   jax.ShapeDtypeStruct((8, LANES), F32), jax.ShapeDtypeStruct((1, D), F32)],
        compiler_params=_cparams(("arbitrary",)),
    )(x1, ffn, target, g)


def _local_step(x, mem, target, gains, w, *, tq=512):
    S = x.shape[0]
    b_pad = jnp.pad(gains["b_forget"], ((0, 0), (0, LANES - FOX_HEADS)))

    h = _rms_fwd(x, gains["norm_mix_pre"], name="rms_mix_pre")
    P = _matmul(h, w["wp"], mode="nn", out_dtype=BF16, name="proj_in")
    f_logit = _matmul(h, w["wf"], mode="nn", out_dtype=F32, name="proj_forget")
    logf = _forget_fwd(f_logit, b_pad, name="forget_fwd")
    c_row16 = _cumsum_lanes(logf[:, :FOX_HEADS].T, reverse=False, name="forget_cumsum")
    ccol = c_row16[:, :, None]
    crow = c_row16.reshape(HEAD_PAIRS, 2, S)
    o_attn, o_attn32, lse = _attn_fwd(P, ccol, crow, tq=tq, name="attn_fwd")
    o_conv, y_conv = _conv_fwd(P, w["conv_w"], gains["conv_b"], gains["conv_ln_g"], gains["conv_ln_b"], name="conv_fwd")
    mem_n = _rms_fwd(mem, gains["norm_mem"], name="rms_mem")
    kv = _matmul(mem_n, w["w_kv"], mode="nn", out_dtype=BF16, name="mem_kv")
    o_mem = _mem_fwd(P, kv, name="mem_fwd")
    merged, y1, pb = _merge_fwd(o_attn, o_conv, o_mem, P, w["wpa"], w["wpc"], w["wpm"], w["w_out"], name="merge_fwd")
    x1, h2 = _resid_norm(x, y1, gains["norm_mix_post"], gains["norm_ffn_pre"], name="resid_norm")
    g_ffn, u_ffn, act = _swiglu_fwd(h2, w["w_gu"], name="swiglu_fwd")
    ffn = _matmul(act, w["w_down"], mode="nn", out_dtype=F32, name="ffn_down", tk=1408)

    dout, dffn, loss_tile, d_norm_ffn_post = _final(x1, ffn, target, gains["norm_ffn_post"], name="loss_head")
    dact = _matmul(dffn, w["w_down"], mode="nt", out_dtype=BF16, name="d_act", tn=1408)
    d_w_down = _matmul(act, dffn, mode="tn", out_dtype=BF16, name="dw_down", tm=1408)
    dgu = _swiglu_bwd(dact, g_ffn, u_ffn, name="swiglu_bwd")
    dh2 = _matmul(dgu, w["w_gu"], mode="nt", out_dtype=F32, name="d_h2", tk=1408)
    d_w_gu = _matmul(h2, dgu, mode="tn", out_dtype=BF16, name="dw_gate_up", tn=1408)
    dx1, d_norm_ffn_pre = _rms_bwd(x1, [dh2], gains["norm_ffn_pre"], dout, out_dtype=F32, name="rms_ffn_pre_bwd")

    dy1, d_norm_mix_post = _rms_bwd(y1, [dx1], gains["norm_mix_post"], None, out_dtype=BF16, name="rms_mix_post_bwd")
    dmerged = _matmul(dy1, w["w_out"], mode="nt", out_dtype=F32, name="d_merged")
    d_w_out = _matmul(merged, dy1, mode="tn", out_dtype=BF16, name="dw_out")
    dpb, dgl = _merge_bwd(dmerged, P, pb, name="merge_bwd")
    do_attn = _matmul(dpb[0], w["wpa"], mode="nt", out_dtype=BF16, name="d_o_attn")
    do_conv = _matmul(dpb[1], w["wpc"], mode="nt", out_dtype=F32, name="d_o_conv")
    do_mem = _matmul(dpb[2], w["wpm"], mode="nt", out_dtype=BF16, name="d_o_mem")
    d_wpa = _matmul(o_attn, dpb[0], mode="tn", out_dtype=BF16, name="dw_proj_attn")
    d_wpc = _matmul(o_conv, dpb[1], mode="tn", out_dtype=BF16, name="dw_proj_conv")
    d_wpm = _matmul(o_mem, dpb[2], mode="tn", out_dtype=BF16, name="dw_proj_mem")

    dq_mem, dkv = _mem_bwd(P, do_mem, kv, name="mem_bwd")
    dkv_b = dkv.astype(BF16)
    d_w_kv = _matmul(mem_n, dkv_b, mode="tn", out_dtype=BF16, name="dw_kv")
    dmem_n = _matmul(dkv_b, w["w_kv"], mode="nt", out_dtype=F32, name="d_mem_n")
    _, d_norm_mem = _rms_bwd(mem, [dmem_n], gains["norm_mem"], None, out_dtype=BF16, name="rms_mem_bwd")

    dglu, d_conv_w, conv_small = _conv_bwd(P, do_conv, y_conv, w["conv_w"], gains["conv_ln_g"], gains["conv_ln_b"], name="conv_bwd")

    delta = _attn_delta(do_attn, o_attn32, name="attn_delta")
    dq, dk, dv, dc = _attn_bwd(P, do_attn, lse, delta, ccol, crow, tq=tq, name="attn_bwd")
    dlogf16 = _cumsum_lanes(dc.reshape(FOX_HEADS, S), reverse=True, name="forget_cumsum_bwd")
    dlogf = jnp.pad(dlogf16.T, ((0, 0), (0, LANES - FOX_HEADS)))
    df, d_b_pad = _forget_bwd(f_logit, b_pad, dlogf, name="forget_bwd")

    dP = jnp.concatenate([dq, dk, dv, dglu, dq_mem, dgl], axis=1)
    dh_p = _matmul(dP, w["wp"], mode="nt", out_dtype=F32, name="d_h")
    dh_f = _matmul(df, w["wf"], mode="nt", out_dtype=F32, name="d_h_forget")
    d_wp = _matmul(h, dP, mode="tn", out_dtype=BF16, name="dw_in")
    d_wf = _matmul(h, df, mode="tn", out_dtype=BF16, name="dw_forget")
    grad_x, d_norm_mix_pre = _rms_bwd(x, [dh_p, dh_f], gains["norm_mix_pre"], dx1, out_dtype=F32, name="rms_mix_pre_bwd")

    grads = dict(wp=d_wp, wf=d_wf, conv_w=d_conv_w, w_kv=d_w_kv, wpa=d_wpa, wpc=d_wpc, wpm=d_wpm, w_out=d_w_out,
                 w_gu=d_w_gu, w_down=d_w_down)
    small = dict(norm_mix_pre=d_norm_mix_pre, norm_mix_post=d_norm_mix_post, norm_mem=d_norm_mem,
                 conv_b=conv_small[2:3], conv_ln_g=conv_small[0:1], conv_ln_b=conv_small[1:2],
                 norm_ffn_pre=d_norm_ffn_pre, norm_ffn_post=d_norm_ffn_post, b_forget=d_b_pad[:, :FOX_HEADS])
    return loss_tile, grad_x, grads, small


def _mesh_pos():
    return lax.axis_index("x"), lax.axis_index("y"), lax.axis_index("c")


def _flip(pos, d):
    x, y, c = pos
    return (1 - x if d & 4 else x, 1 - y if d & 2 else y, 1 - c if d & 1 else c)


def _flat(pos):
    x, y, c = pos
    return 4 * x + 2 * y + c


def _exchange(srcs, *, scatter, name):
    T = len(srcs)
    n_peer = N_DEV - 1

    def body(*refs):
        src_refs, out_refs = refs[:T], refs[T:2 * T]
        send_sems, recv_sems, local_sems = refs[2 * T:]
        me = _mesh_pos()
        my = _flat(me)

        def copy(t, d):
            peer = _flip(me, d)
            return pltpu.make_async_remote_copy(
                src_ref=src_refs[t].at[_flat(peer)] if scatter else src_refs[t],
                dst_ref=out_refs[t].at[my],
                send_sem=send_sems.at[t * n_peer + d - 1], recv_sem=recv_sems.at[t * n_peer + d - 1],
                device_id=peer, device_id_type=pl.DeviceIdType.MESH)

        def arrival(t, d):
            peer = _flip(me, d)
            return pltpu.make_async_remote_copy(
                src_ref=src_refs[t].at[my] if scatter else src_refs[t],
                dst_ref=out_refs[t].at[_flat(peer)],
                send_sem=send_sems.at[t * n_peer + d - 1], recv_sem=recv_sems.at[t * n_peer + d - 1],
                device_id=peer, device_id_type=pl.DeviceIdType.MESH)

        local = [pltpu.make_async_copy(src_refs[t].at[my] if scatter else src_refs[t], out_refs[t].at[my], local_sems.at[t])
                 for t in range(T)]
        for cp in local:
            cp.start()
        sends = [copy(t, d) for d in range(1, N_DEV) for t in range(T)]
        for cp in sends:
            cp.start()
        for d in range(1, N_DEV):
            for t in range(T):
                arrival(t, d).wait_recv()
        for cp in sends:
            cp.wait_send()
        for cp in local:
            cp.wait()

    return pl.pallas_call(
        body, name=name,
        in_specs=[pl.BlockSpec(memory_space=pl.ANY)] * T,
        out_specs=[pl.BlockSpec(memory_space=pl.ANY)] * T,
        out_shape=[jax.ShapeDtypeStruct((N_DEV,) + tuple(s.shape[-2:]), s.dtype) for s in srcs],
        scratch_shapes=[pltpu.SemaphoreType.DMA((T * n_peer,)), pltpu.SemaphoreType.DMA((T * n_peer,)), pltpu.SemaphoreType.DMA((T,))],
    )(*srcs)


def _adamw(w, slots, m, v, *, name):
    R, C = w.shape
    tr = R if R <= 512 else 256
    assert R % tr == 0, (R, tr)

    def body(w_ref, s_ref, m_ref, v_ref, g_ref, d_ref, m2_ref, v2_ref):
        gv = s_ref[0].astype(F32)
        for j in range(1, N_DEV):
            gv = gv + s_ref[j].astype(F32)
        g_ref[...] = gv
        m2 = ADAM_B1 * m_ref[...] + (1.0 - ADAM_B1) * gv
        v2 = ADAM_B2 * v_ref[...] + (1.0 - ADAM_B2) * (gv * gv)
        m_hat = m2 / (1.0 - ADAM_B1 ** ADAM_STEP)
        v_hat = v2 / (1.0 - ADAM_B2 ** ADAM_STEP)
        d_ref[...] = -ADAM_LR * (m_hat / (jnp.sqrt(v_hat) + ADAM_EPS) + ADAM_WD * w_ref[...])
        m2_ref[...] = m2
        v2_ref[...] = v2

    blk = pl.BlockSpec((tr, C), lambda i: (i, 0))
    sds = jax.ShapeDtypeStruct((R, C), F32)
    return pl.pallas_call(
        body, name=name, grid=(R // tr,),
        in_specs=[blk, pl.BlockSpec((N_DEV, tr, C), lambda i: (0, i, 0)), blk, blk],
        out_specs=[blk] * 4, out_shape=[sds] * 4,
        compiler_params=_cparams(("parallel",)),
    )(w, slots, m, v)


SHARDED = (
    ("w_in", (1024, 1154), "col"), ("conv_w", (31, 128), "col"), ("w_kv_mem", (1024, 256), "col"),
    ("w_proj_attn", (128, 1024), "row"), ("w_proj_conv", (128, 1024), "row"), ("w_proj_mem", (128, 1024), "row"),
    ("w_out", (128, 1024), "row"), ("w_gate_up", (1024, 704), "col"), ("w_down", (352, 1024), "row"),
)
SMALL = ("norm_mix_pre", "norm_mix_post", "norm_mem", "conv_b", "conv_ln_g", "conv_ln_b", "norm_ffn_pre", "norm_ffn_post", "b_forget")
SMALL_ROWS = 16
LOSS_ROW = len(SMALL)
WEIGHT_ORDER = ("norm_mix_pre", "norm_mix_post", "norm_mem", "w_in", "b_forget", "conv_w", "conv_b", "conv_ln_g", "conv_ln_b",
                "w_kv_mem", "w_proj_attn", "w_proj_conv", "w_proj_mem", "w_out", "norm_ffn_pre", "norm_ffn_post", "w_gate_up", "w_down")


def _to_full(blocks8, kind):
    n, r, c = blocks8.shape
    if kind == "col":
        return jnp.concatenate([blocks8[j] for j in range(n)], axis=1)
    return blocks8.reshape(n * r, c)


def _to_blocks(full, kind):
    if kind == "col":
        c = full.shape[1] // N_DEV
        return jnp.stack([full[:, j * c:(j + 1) * c] for j in range(N_DEV)])
    nr, c = full.shape
    return full.reshape(N_DEV, nr // N_DEV, c)


def kernel(x, mem, norm_mix_pre, norm_mix_post, norm_mem, w_in, b_forget, conv_w, conv_b, conv_ln_g, conv_ln_b, w_kv_mem, w_proj_attn, w_proj_conv, w_proj_mem, w_out, norm_ffn_pre, norm_ffn_post, w_gate_up, w_down, loss_target, m_norm_mix_pre, m_norm_mix_post, m_norm_mem, m_w_in, m_b_forget, m_conv_w, m_conv_b, m_conv_ln_g, m_conv_ln_b, m_w_kv_mem, m_w_proj_attn, m_w_proj_conv, m_w_proj_mem, m_w_out, m_norm_ffn_pre, m_norm_ffn_post, m_w_gate_up, m_w_down, v_norm_mix_pre, v_norm_mix_post, v_norm_mem, v_w_in, v_b_forget, v_conv_w, v_conv_b, v_conv_ln_g, v_conv_ln_b, v_w_kv_mem, v_w_proj_attn, v_w_proj_conv, v_w_proj_mem, v_w_out, v_norm_ffn_pre, v_norm_ffn_post, v_w_gate_up, v_w_down):
    given = dict(locals())
    weights = {n: given[n] for n in WEIGHT_ORDER}
    moments_m = {n: given["m_" + n] for n in WEIGHT_ORDER}
    moments_v = {n: given["v_" + n] for n in WEIGHT_ORDER}
    kind = {name: k for name, _, k in SHARDED}
    names = [name for name, _, _ in SHARDED]

    blocks = {n: weights[n][0].astype(BF16) for n in names if n != "conv_w"}
    blocks["conv_w"] = lax.bitcast_convert_type(conv_w[0], BF16).reshape(CONV_WIDTH, 2 * LANES)
    gathered = dict(zip(names, _exchange([blocks[n] for n in names], scatter=False, name="gather_weights")))
    conv_w_all = lax.bitcast_convert_type(gathered["conv_w"].reshape(N_DEV, CONV_WIDTH, LANES, 2), F32)
    full = {n: _to_full(gathered[n], kind[n]) for n in names if n != "conv_w"}
    conv_w_full = _to_full(conv_w_all, "col")
    n_qkv = 3 * D_MODEL
    w_in_full = full["w_in"]
    wdict = dict(
        wp=jnp.concatenate([w_in_full[:, :n_qkv], w_in_full[:, n_qkv + FOX_HEADS:]], axis=1),
        wf=jnp.pad(w_in_full[:, n_qkv:n_qkv + FOX_HEADS], ((0, 0), (0, LANES - FOX_HEADS))),
        conv_w=jnp.pad(conv_w_full, ((0, CONV_HALO - CONV_WIDTH), (0, 0))),
        w_kv=full["w_kv_mem"], wpa=full["w_proj_attn"], wpc=full["w_proj_conv"], wpm=full["w_proj_mem"],
        w_out=full["w_out"], w_gu=full["w_gate_up"], w_down=full["w_down"],
    )
    gains = {n: weights[n] for n in SMALL}

    loss_tile, grad_x, g, small = _local_step(x[0], mem[0], loss_target[0], gains, wdict)

    g_full = dict(
        w_in=jnp.concatenate([g["wp"][:, :n_qkv], g["wf"][:, :FOX_HEADS], g["wp"][:, n_qkv:]], axis=1),
        conv_w=g["conv_w"][:CONV_WIDTH], w_kv_mem=g["w_kv"], w_proj_attn=g["wpa"], w_proj_conv=g["wpc"],
        w_proj_mem=g["wpm"], w_out=g["w_out"], w_gate_up=g["w_gu"], w_down=g["w_down"],
    )
    g_recv = dict(zip(names, _exchange([_to_blocks(g_full[n], kind[n]).astype(BF16) for n in names], scatter=True, name="scatter_grads")))

    rows = [jnp.pad(small[n], ((0, 0), (0, D_MODEL - small[n].shape[1]))) for n in SMALL]
    rows.append(jnp.broadcast_to(loss_tile[0:1, 0:1], (1, D_MODEL)))
    rows.append(jnp.zeros((SMALL_ROWS - len(rows), D_MODEL), F32))
    small_recv, = _exchange([jnp.concatenate(rows, axis=0)], scatter=False, name="gather_small")

    def slab_of(d, fill):
        rows = [jnp.pad(d[n], ((0, 0), (0, D_MODEL - d[n].shape[1])), constant_values=fill) for n in SMALL]
        rows.append(jnp.full((SMALL_ROWS - len(rows), D_MODEL), fill, F32))
        return jnp.concatenate(rows, axis=0)

    grads, delta, new_m, new_v = {}, {}, {}, {}
    sg, sd, sm, sv = _adamw(slab_of(weights, 0.0), small_recv, slab_of(moments_m, 0.0), slab_of(moments_v, 1.0), name="adamw_small")
    loss = sg[LOSS_ROW, 0]
    for i, n in enumerate(SMALL):
        c = weights[n].shape[1]
        grads[n], delta[n], new_m[n], new_v[n] = sg[i:i + 1, :c], sd[i:i + 1, :c], sm[i:i + 1, :c], sv[i:i + 1, :c]
    for n in names:
        gw, d, m2, v2 = _adamw(weights[n][0], g_recv[n], moments_m[n][0], moments_v[n][0], name="adamw_" + n)
        grads[n], delta[n], new_m[n], new_v[n] = gw[None], d[None], m2[None], v2[None]

    return (loss, grad_x[None], *[grads[n] for n in WEIGHT_ORDER], *[delta[n] for n in WEIGHT_ORDER],
            *[new_m[n] for n in WEIGHT_ORDER], *[new_v[n] for n in WEIGHT_ORDER])
```

```python
import functools

import jax
import jax.numpy as jnp
from jax import lax
from jax.experimental import pallas as pl
from jax.experimental.pallas import tpu as pltpu

F32 = jnp.float32
BF16 = jnp.bfloat16

D_MODEL = 1024
N_DEV = 8
FOX_HEADS = 16
FOX_HEAD_DIM = 64
HEAD_PAIRS = FOX_HEADS // 2
MEM_HEADS = 4
MEM_HEAD_DIM = D_MODEL // MEM_HEADS
CONV_WIDTH = 31
CONV_HALO = 32
FFN_HIDDEN = 2816
RMS_EPS = 1e-6
LN_EPS = 1e-5
ADAM_LR = 0.001
ADAM_B1 = 0.9
ADAM_B2 = 0.999
ADAM_EPS = 1e-08
ADAM_WD = 0.01
ADAM_STEP = 10
NEG_BIG = -1e30
LANES = 128

PB_Q, PB_K, PB_V, PB_A, PB_GATE, PB_QMEM, PB_G0 = 0, 1, 2, 3, 4, 5, 6
P_WIDTH = 9 * D_MODEL

NT_DIMS = (((1,), (1,)), ((), ()))
TN_DIMS = (((0,), (0,)), ((), ()))


def _cparams(sem, vmem_mb=None):
    kw = dict(dimension_semantics=sem)
    if vmem_mb is not None:
        kw["vmem_limit_bytes"] = vmem_mb * 1024 * 1024
    return pltpu.CompilerParams(**kw)


def _tile(dim, want):
    t = min(dim, want)
    assert dim % t == 0, (dim, want)
    return t


def _sigmoid(z):
    return 1.0 / (1.0 + jnp.exp(-z))


def _matmul(a, b, *, mode, out_dtype, name, tm=1024, tn=1024, tk=1024):
    if mode == "nn":
        (M, K), (K2, N) = a.shape, b.shape
    elif mode == "nt":
        (M, K), (N, K2) = a.shape, b.shape
    else:
        (K, M), (K2, N) = a.shape, b.shape
    assert K == K2, (a.shape, b.shape, mode)
    tm, tn, tk = _tile(M, tm), _tile(N, tn), _tile(K, tk)
    nk = K // tk
    dims = {"nn": (((1,), (0,)), ((), ())), "nt": NT_DIMS, "tn": TN_DIMS}[mode]

    def body(a_ref, b_ref, o_ref, *scratch):
        part = lax.dot_general(a_ref[...], b_ref[...], dims, preferred_element_type=F32)
        if nk == 1:
            o_ref[...] = part.astype(o_ref.dtype)
        else:
            acc_ref, = scratch
            k = pl.program_id(2)

            @pl.when(k == 0)
            def _():
                acc_ref[...] = part

            @pl.when(k > 0)
            def _():
                acc_ref[...] += part

            @pl.when(k == nk - 1)
            def _():
                o_ref[...] = acc_ref[...].astype(o_ref.dtype)

    a_spec = pl.BlockSpec((tk, tm), lambda j, i, k: (k, i)) if mode == "tn" else pl.BlockSpec((tm, tk), lambda j, i, k: (i, k))
    b_spec = pl.BlockSpec((tn, tk), lambda j, i, k: (j, k)) if mode == "nt" else pl.BlockSpec((tk, tn), lambda j, i, k: (k, j))
    return pl.pallas_call(
        body,
        name=name,
        grid=(N // tn, M // tm, nk),
        in_specs=[a_spec, b_spec],
        out_specs=pl.BlockSpec((tm, tn), lambda j, i, k: (i, j)),
        out_shape=jax.ShapeDtypeStruct((M, N), out_dtype),
        scratch_shapes=[pltpu.VMEM((tm, tn), F32)] if nk > 1 else [],
        compiler_params=_cparams(("parallel", "parallel", "arbitrary"), 56),
    )(a, b)


def _rms_fwd(x, g, *, name, tm=512):
    S, D = x.shape
    tm = _tile(S, tm)

    def body(x_ref, g_ref, o_ref):
        xv = x_ref[...]
        rstd = lax.rsqrt(jnp.mean(xv * xv, axis=-1, keepdims=True) + RMS_EPS)
        o_ref[...] = (xv * rstd * g_ref[...]).astype(o_ref.dtype)

    return pl.pallas_call(
        body, name=name, grid=(S // tm,),
        in_specs=[pl.BlockSpec((tm, D), lambda i: (i, 0)), pl.BlockSpec((1, D), lambda i: (0, 0))],
        out_specs=pl.BlockSpec((tm, D), lambda i: (i, 0)),
        out_shape=jax.ShapeDtypeStruct((S, D), BF16),
        compiler_params=_cparams(("parallel",)),
    )(x, g)


def _rms_bwd(xin, dys, g, res, *, out_dtype, name, tm=512):
    S, D = xin.shape
    tm = _tile(S, tm)
    n_dy = len(dys)
    has_res = res is not None

    def body(*refs):
        x_ref, g_ref = refs[0], refs[1]
        dy_refs = refs[2:2 + n_dy]
        pos = 2 + n_dy
        res_ref = refs[pos] if has_res else None
        pos += int(has_res)
        dx_ref, dg_ref = refs[pos], refs[pos + 1]
        i = pl.program_id(0)
        xv = x_ref[...]
        dy = dy_refs[0][...].astype(F32)
        for r in dy_refs[1:]:
            dy = dy + r[...].astype(F32)
        rstd = lax.rsqrt(jnp.mean(xv * xv, axis=-1, keepdims=True) + RMS_EPS)
        xhat = xv * rstd
        gy = dy * g_ref[...]
        dx = rstd * (gy - xhat * jnp.mean(gy * xhat, axis=-1, keepdims=True))
        if has_res:
            dx = dx + res_ref[...]
        dx_ref[...] = dx.astype(dx_ref.dtype)
        part = jnp.sum(dy * xhat, axis=0, keepdims=True)

        @pl.when(i == 0)
        def _():
            dg_ref[...] = part

        @pl.when(i > 0)
        def _():
            dg_ref[...] += part

    row = pl.BlockSpec((tm, D), lambda i: (i, 0))
    vec = pl.BlockSpec((1, D), lambda i: (0, 0))
    ins = [xin, g] + list(dys) + ([res] if has_res else [])
    return pl.pallas_call(
        body, name=name, grid=(S // tm,),
        in_specs=[row, vec] + [row] * n_dy + ([row] if has_res else []),
        out_specs=[row, vec],
        out_shape=[jax.ShapeDtypeStruct((S, D), out_dtype), jax.ShapeDtypeStruct((1, D), F32)],
        compiler_params=_cparams(("arbitrary",)),
    )(*ins)


def _head_mask(hh, shape):
    lane = lax.broadcasted_iota(jnp.int32, shape, len(shape) - 1)
    return (lane // FOX_HEAD_DIM) == hh


def _attn_fwd(P, ccol, crow, *, tq, name):
    S = P.shape[0]
    tq = _tile(S, tq)
    nq = S // tq
    scale = FOX_HEAD_DIM ** -0.5

    def body(q_ref, k_ref, v_ref, cc_ref, cr_ref, o_ref, o32_ref, lse_ref):
        i = pl.program_id(1)
        q = q_ref[...] * jnp.asarray(scale, BF16)
        row = lax.broadcasted_iota(jnp.int32, (tq, tq), 0)
        col = lax.broadcasted_iota(jnp.int32, (tq, tq), 1)
        causal = col <= row
        hms = [_head_mask(hh, (tq, LANES)) for hh in range(2)]
        qhs = [jnp.where(hm, q, jnp.zeros_like(q)) for hm in hms]
        cbs = [cc_ref[hh, 0:1, :] for hh in range(2)]

        def step(j, carry, masked):
            start = pl.multiple_of(j * tq, tq)
            kj = k_ref[pl.ds(start, tq), :]
            vj = v_ref[pl.ds(start, tq), :]
            new = []
            for hh in range(2):
                m, l, acc = carry[hh]
                bias = cbs[hh] - cr_ref[0, hh:hh + 1, pl.ds(start, tq)]
                s = lax.dot_general(qhs[hh], kj, NT_DIMS, preferred_element_type=F32) + bias
                if masked:
                    s = jnp.where(causal, s, NEG_BIG)
                m_new = jnp.maximum(m, jnp.max(s, axis=1, keepdims=True))
                alpha = jnp.exp(m - m_new)
                p = jnp.exp(s - m_new)
                l = alpha * l + jnp.sum(p, axis=1, keepdims=True)
                p_hi = p.astype(BF16)
                p_lo = (p - p_hi.astype(F32)).astype(BF16)
                acc = (alpha * acc + jnp.dot(p_hi, vj, preferred_element_type=F32)
                       + jnp.dot(p_lo, vj, preferred_element_type=F32))
                new.append((m_new, l, acc))
            return tuple(new)

        init = (jnp.full((tq, 1), NEG_BIG, F32), jnp.zeros((tq, 1), F32), jnp.zeros((tq, LANES), F32))
        carry = lax.fori_loop(0, i, functools.partial(step, masked=False), (init, init))
        carry = step(i, carry, True)
        out = jnp.where(hms[0], carry[0][2] / carry[0][1], carry[1][2] / carry[1][1])
        for hh in range(2):
            lse_ref[hh] = carry[hh][0] + jnp.log(carry[hh][1])
        o_ref[...] = out.astype(o_ref.dtype)
        o32_ref[...] = out

    nblk = D_MODEL // LANES
    return pl.pallas_call(
        body, name=name, grid=(HEAD_PAIRS, nq),
        in_specs=[
            pl.BlockSpec((tq, LANES), lambda p, i: (i, PB_Q * nblk + p)),
            pl.BlockSpec((S, LANES), lambda p, i: (0, PB_K * nblk + p)),
            pl.BlockSpec((S, LANES), lambda p, i: (0, PB_V * nblk + p)),
            pl.BlockSpec((2, tq, 1), lambda p, i: (p, i, 0)),
            pl.BlockSpec((1, 2, S), lambda p, i: (p, 0, 0)),
        ],
        out_specs=[
            pl.BlockSpec((tq, LANES), lambda p, i: (i, p)),
            pl.BlockSpec((tq, LANES), lambda p, i: (i, p)),
            pl.BlockSpec((2, tq, 1), lambda p, i: (p, i, 0)),
        ],
        out_shape=[jax.ShapeDtypeStruct((S, D_MODEL), BF16), jax.ShapeDtypeStruct((S, D_MODEL), F32),
                   jax.ShapeDtypeStruct((FOX_HEADS, S, 1), F32)],
        compiler_params=_cparams(("parallel", "arbitrary"), 56),
    )(P, P, P, ccol, crow)


def _attn_delta(do, o, *, name, tm=512):
    S = do.shape[0]
    tm = _tile(S, tm)

    def body(do_ref, o_ref, d_ref):
        prod = do_ref[...].astype(F32) * o_ref[...].astype(F32)
        for hh in range(2):
            d_ref[hh] = jnp.sum(jnp.where(_head_mask(hh, (tm, LANES)), prod, 0.0), axis=1, keepdims=True)

    blk = pl.BlockSpec((tm, LANES), lambda p, i: (i, p))
    return pl.pallas_call(
        body, name=name, grid=(HEAD_PAIRS, S // tm),
        in_specs=[blk, blk],
        out_specs=pl.BlockSpec((2, tm, 1), lambda p, i: (p, i, 0)),
        out_shape=jax.ShapeDtypeStruct((FOX_HEADS, S, 1), F32),
        compiler_params=_cparams(("parallel", "parallel")),
    )(do, o)


def _attn_bwd(P, do, lse, delta, ccol, crow, *, tq, name):
    S = P.shape[0]
    tq = _tile(S, tq)
    nq = S // tq
    scale = FOX_HEAD_DIM ** -0.5

    def body(q_ref, k_ref, v_ref, do_ref, lse_ref, dl_ref, cc_ref, cr_ref, dq_ref, dk_out, dv_out, dc_ref, dk_ref, dv_ref):
        i = pl.program_id(1)

        @pl.when(i == 0)
        def _():
            dk_ref[...] = jnp.zeros_like(dk_ref)
            dv_ref[...] = jnp.zeros_like(dv_ref)
            dc_ref[...] = jnp.zeros_like(dc_ref)

        q = q_ref[...] * jnp.asarray(scale, BF16)
        do_v = do_ref[...]
        row = lax.broadcasted_iota(jnp.int32, (tq, tq), 0)
        col = lax.broadcasted_iota(jnp.int32, (tq, tq), 1)
        causal = col <= row
        hms = [_head_mask(hh, (tq, LANES)) for hh in range(2)]
        qhs = [jnp.where(hm, q, jnp.zeros_like(q)) for hm in hms]
        dohs = [jnp.where(hm, do_v, jnp.zeros_like(do_v)) for hm in hms]
        cbs = [cc_ref[hh, 0:1, :] for hh in range(2)]
        lses = [lse_ref[hh] for hh in range(2)]
        dls = [dl_ref[hh] for hh in range(2)]

        def step(j, dq_acc, masked):
            start = pl.multiple_of(j * tq, tq)
            kj = k_ref[pl.ds(start, tq), :]
            vj = v_ref[pl.ds(start, tq), :]
            dv_part = jnp.zeros((tq, LANES), F32)
            dk_part = jnp.zeros((tq, LANES), F32)
            for hh in range(2):
                bias = cbs[hh] - cr_ref[0, hh:hh + 1, pl.ds(start, tq)]
                s = lax.dot_general(qhs[hh], kj, NT_DIMS, preferred_element_type=F32) + bias
                if masked:
                    s = jnp.where(causal, s, NEG_BIG)
                p = jnp.exp(s - lses[hh])
                dp = lax.dot_general(dohs[hh], vj, NT_DIMS, preferred_element_type=F32)
                ds = p * (dp - dls[hh])
                pb = p.astype(BF16)
                dsb = ds.astype(BF16)
                dv_part = dv_part + lax.dot_general(pb, dohs[hh], TN_DIMS, preferred_element_type=F32)
                dk_part = dk_part + lax.dot_general(dsb, qhs[hh], TN_DIMS, preferred_element_type=F32)
                dc_ref[0, hh:hh + 1, pl.ds(start, tq)] -= jnp.sum(ds, axis=0, keepdims=True)
                kh = jnp.where(hms[hh], kj, jnp.zeros_like(kj))
                dq_acc = dq_acc + jnp.dot(dsb, kh, preferred_element_type=F32)
            dv_ref[pl.ds(start, tq), :] += dv_part
            dk_ref[pl.ds(start, tq), :] += dk_part
            return dq_acc

        dq = lax.fori_loop(0, i, functools.partial(step, masked=False), jnp.zeros((tq, LANES), F32))
        dq = step(i, dq, True)
        dq_ref[...] = (dq * scale).astype(dq_ref.dtype)

        @pl.when(i == nq - 1)
        def _():
            dk_out[...] = dk_ref[...].astype(dk_out.dtype)
            dv_out[...] = dv_ref[...].astype(dv_out.dtype)

    nblk = D_MODEL // LANES
    qblk = pl.BlockSpec((tq, LANES), lambda p, i: (i, p))
    stat = pl.BlockSpec((2, tq, 1), lambda p, i: (p, i, 0))
    full = pl.BlockSpec((S, LANES), lambda p, i: (0, p))
    return pl.pallas_call(
        body, name=name, grid=(HEAD_PAIRS, nq),
        in_specs=[
            pl.BlockSpec((tq, LANES), lambda p, i: (i, PB_Q * nblk + p)),
            pl.BlockSpec((S, LANES), lambda p, i: (0, PB_K * nblk + p)),
            pl.BlockSpec((S, LANES), lambda p, i: (0, PB_V * nblk + p)),
            qblk, stat, stat, stat,
            pl.BlockSpec((1, 2, S), lambda p, i: (p, 0, 0)),
        ],
        out_specs=[qblk, full, full, pl.BlockSpec((1, 2, S), lambda p, i: (p, 0, 0))],
        out_shape=[
            jax.ShapeDtypeStruct((S, D_MODEL), BF16),
            jax.ShapeDtypeStruct((S, D_MODEL), BF16),
            jax.ShapeDtypeStruct((S, D_MODEL), BF16),
            jax.ShapeDtypeStruct((HEAD_PAIRS, 2, S), F32),
        ],
        scratch_shapes=[pltpu.VMEM((S, LANES), F32), pltpu.VMEM((S, LANES), F32)],
        compiler_params=_cparams(("parallel", "arbitrary"), 56),
    )(P, P, P, do, lse, delta, ccol, crow)


def _cumsum_lanes(x, *, reverse, name):
    R, S = x.shape
    nb = S // LANES

    def body(x_ref, o_ref):
        r = lax.broadcasted_iota(jnp.int32, (LANES, LANES), 0)
        c = lax.broadcasted_iota(jnp.int32, (LANES, LANES), 1)
        tri = ((r >= c) if reverse else (r <= c)).astype(F32)

        def step(b, carry):
            blk = (nb - 1 - b) if reverse else b
            start = pl.multiple_of(blk * LANES, LANES)
            xb = x_ref[:, pl.ds(start, LANES)]
            y = jnp.dot(xb, tri, precision=lax.Precision.HIGHEST, preferred_element_type=F32) + carry
            o_ref[:, pl.ds(start, LANES)] = y
            return carry + jnp.sum(xb, axis=1, keepdims=True)

        lax.fori_loop(0, nb, step, jnp.zeros((R, 1), F32))

    return pl.pallas_call(
        body, name=name,
        in_specs=[pl.BlockSpec(memory_space=pltpu.VMEM)],
        out_specs=pl.BlockSpec(memory_space=pltpu.VMEM),
        out_shape=jax.ShapeDtypeStruct((R, S), F32),
    )(x)


def _forget_fwd(f_logit, b_pad, *, name, tm=1024):
    S = f_logit.shape[0]
    tm = _tile(S, tm)

    def body(f_ref, b_ref, o_ref):
        z = f_ref[...] + b_ref[...]
        o_ref[...] = jnp.minimum(z, 0.0) - jnp.log(1.0 + jnp.exp(-jnp.abs(z)))

    blk = pl.BlockSpec((tm, LANES), lambda i: (i, 0))
    return pl.pallas_call(
        body, name=name, grid=(S // tm,),
        in_specs=[blk, pl.BlockSpec((1, LANES), lambda i: (0, 0))],
        out_specs=blk, out_shape=jax.ShapeDtypeStruct((S, LANES), F32),
        compiler_params=_cparams(("parallel",)),
    )(f_logit, b_pad)


def _forget_bwd(f_logit, b_pad, dlogf, *, name, tm=1024):
    S = f_logit.shape[0]
    tm = _tile(S, tm)

    def body(f_ref, b_ref, d_ref, o_ref, db_ref):
        i = pl.program_id(0)
        z = f_ref[...] + b_ref[...]
        dz = d_ref[...] * (1.0 - _sigmoid(z))
        o_ref[...] = dz.astype(o_ref.dtype)
        part = jnp.sum(dz, axis=0, keepdims=True)

        @pl.when(i == 0)
        def _():
            db_ref[...] = part

        @pl.when(i > 0)
        def _():
            db_ref[...] += part

    blk = pl.BlockSpec((tm, LANES), lambda i: (i, 0))
    vec = pl.BlockSpec((1, LANES), lambda i: (0, 0))
    return pl.pallas_call(
        body, name=name, grid=(S // tm,),
        in_specs=[blk, vec, blk], out_specs=[blk, vec],
        out_shape=[jax.ShapeDtypeStruct((S, LANES), BF16), jax.ShapeDtypeStruct((1, LANES), F32)],
        compiler_params=_cparams(("arbitrary",)),
    )(f_logit, b_pad, dlogf)


def _layernorm_stats(y):
    mu = jnp.mean(y, axis=-1, keepdims=True)
    yc = y - mu
    rstd = lax.rsqrt(jnp.mean(yc * yc, axis=-1, keepdims=True) + LN_EPS)
    return yc * rstd, rstd


def _conv_fwd(P, w_pad, conv_b, ln_g, ln_b, *, name, tm=256):
    S = P.shape[0]
    C = D_MODEL
    tm = _tile(S, tm)
    hb = tm // CONV_HALO

    def body(a_ref, gt_ref, ah_ref, gh_ref, w_ref, cb_ref, g_ref, b_ref, o_ref, y_ref, glu_ref):
        i = pl.program_id(0)
        halo = ah_ref[...].astype(F32) * _sigmoid(gh_ref[...].astype(F32))
        glu_ref[0:CONV_HALO, :] = jnp.where(i > 0, halo, 0.0)
        glu_ref[CONV_HALO:, :] = a_ref[...].astype(F32) * _sigmoid(gt_ref[...].astype(F32))
        acc = jnp.zeros((tm, C), F32)
        for k in range(CONV_WIDTH):
            acc = acc + w_ref[k:k + 1, :] * glu_ref[pl.ds(CONV_HALO - (CONV_WIDTH - 1) + k, tm), :]
        y = acc + cb_ref[...]
        y_ref[...] = y
        xhat, _ = _layernorm_stats(y)
        z = xhat * g_ref[...] + b_ref[...]
        o_ref[...] = (z * _sigmoid(z)).astype(o_ref.dtype)

    vec = pl.BlockSpec((1, C), lambda i: (0, 0))
    row = pl.BlockSpec((tm, C), lambda i: (i, 0))
    return pl.pallas_call(
        body, name=name, grid=(S // tm,),
        in_specs=[
            pl.BlockSpec((tm, C), lambda i: (i, PB_A)),
            pl.BlockSpec((tm, C), lambda i: (i, PB_GATE)),
            pl.BlockSpec((CONV_HALO, C), lambda i: (jnp.maximum(i * hb - 1, 0), PB_A)),
            pl.BlockSpec((CONV_HALO, C), lambda i: (jnp.maximum(i * hb - 1, 0), PB_GATE)),
            pl.BlockSpec((CONV_HALO, C), lambda i: (0, 0)),
            vec, vec, vec,
        ],
        out_specs=[row, row],
        out_shape=[jax.ShapeDtypeStruct((S, C), BF16), jax.ShapeDtypeStruct((S, C), F32)],
        scratch_shapes=[pltpu.VMEM((tm + CONV_HALO, C), F32)],
        compiler_params=_cparams(("parallel",), 56),
    )(P, P, P, P, w_pad, conv_b, ln_g, ln_b)


def _conv_bwd(P, do, y, w_pad, ln_g, ln_b, *, name, tm=256):
    S = P.shape[0]
    C = D_MODEL
    tm = _tile(S, tm)
    hb = tm // CONV_HALO
    n_tiles = S // tm
    last_halo = S // CONV_HALO - 1

    def body(a_ref, gt_ref, ah_ref, gh_ref, do_ref, y_ref, don_ref, yn_ref, w_ref, g_ref, b_ref,
             dglu_ref, dw_ref, small_ref, glu_ref, dy_ref):
        i = pl.program_id(0)

        @pl.when(i == 0)
        def _():
            dw_ref[...] = jnp.zeros_like(dw_ref)
            small_ref[...] = jnp.zeros_like(small_ref)

        def ln_bwd(do_v, y_v):
            xhat, rstd = _layernorm_stats(y_v)
            z = xhat * g_ref[...] + b_ref[...]
            sg = _sigmoid(z)
            dz = do_v * (sg * (1.0 + z * (1.0 - sg)))
            dxh = dz * g_ref[...]
            dy = rstd * (dxh - jnp.mean(dxh, axis=-1, keepdims=True) - xhat * jnp.mean(dxh * xhat, axis=-1, keepdims=True))
            return dy, dz, xhat

        dy, dz, xhat = ln_bwd(do_ref[...], y_ref[...])
        dy_next, _, _ = ln_bwd(don_ref[...], yn_ref[...])
        small_ref[0:1, :] += jnp.sum(dz * xhat, axis=0, keepdims=True)
        small_ref[1:2, :] += jnp.sum(dz, axis=0, keepdims=True)
        small_ref[2:3, :] += jnp.sum(dy, axis=0, keepdims=True)
        dy_ref[0:tm, :] = dy
        dy_ref[tm:, :] = jnp.where(i < n_tiles - 1, dy_next, 0.0)

        a = a_ref[...].astype(F32)
        sig = _sigmoid(gt_ref[...].astype(F32))
        halo = ah_ref[...].astype(F32) * _sigmoid(gh_ref[...].astype(F32))
        glu_ref[0:CONV_HALO, :] = jnp.where(i > 0, halo, 0.0)
        glu_ref[CONV_HALO:, :] = a * sig

        dg = jnp.zeros((tm, C), F32)
        for k in range(CONV_WIDTH):
            shifted = glu_ref[pl.ds(CONV_HALO - (CONV_WIDTH - 1) + k, tm), :]
            dw_ref[k:k + 1, :] += jnp.sum(dy * shifted, axis=0, keepdims=True)
            dg = dg + w_ref[k:k + 1, :] * dy_ref[pl.ds(CONV_WIDTH - 1 - k, tm), :]
        dglu_ref[:, 0:C] = (dg * sig).astype(dglu_ref.dtype)
        dglu_ref[:, C:] = (dg * a * sig * (1.0 - sig)).astype(dglu_ref.dtype)

    vec = pl.BlockSpec((1, C), lambda i: (0, 0))
    row = pl.BlockSpec((tm, C), lambda i: (i, 0))
    nxt = pl.BlockSpec((CONV_HALO, C), lambda i: (jnp.minimum((i + 1) * hb, last_halo), 0))
    return pl.pallas_call(
        body, name=name, grid=(n_tiles,),
        in_specs=[
            pl.BlockSpec((tm, C), lambda i: (i, PB_A)),
            pl.BlockSpec((tm, C), lambda i: (i, PB_GATE)),
            pl.BlockSpec((CONV_HALO, C), lambda i: (jnp.maximum(i * hb - 1, 0), PB_A)),
            pl.BlockSpec((CONV_HALO, C), lambda i: (jnp.maximum(i * hb - 1, 0), PB_GATE)),
            row, row, nxt, nxt,
            pl.BlockSpec((CONV_HALO, C), lambda i: (0, 0)),
            vec, vec,
        ],
        out_specs=[
            pl.BlockSpec((tm, 2 * C), lambda i: (i, 0)),
            pl.BlockSpec((CONV_HALO, C), lambda i: (0, 0)),
            pl.BlockSpec((8, C), lambda i: (0, 0)),
        ],
        out_shape=[
            jax.ShapeDtypeStruct((S, 2 * C), BF16),
            jax.ShapeDtypeStruct((CONV_HALO, C), F32),
            jax.ShapeDtypeStruct((8, C), F32),
        ],
        scratch_shapes=[pltpu.VMEM((tm + CONV_HALO, C), F32), pltpu.VMEM((tm + CONV_HALO, C), F32)],
        compiler_params=_cparams(("arbitrary",), 56),
    )(P, P, P, P, do, y, do, y, w_pad, ln_g, ln_b)


def _mem_softmax(qh, kh):
    s = lax.dot_general(qh, kh, NT_DIMS, preferred_element_type=F32)
    e = jnp.exp(s - jnp.max(s, axis=1, keepdims=True))
    return e / jnp.sum(e, axis=1, keepdims=True)


def _mem_fwd(P, kv, *, name, tm=512):
    S, M = P.shape[0], kv.shape[0]
    tm = _tile(S, tm)
    scale = MEM_HEAD_DIM ** -0.5

    def body(q_ref, k_ref, v_ref, o_ref):
        for h in range(MEM_HEADS):
            sl = slice(h * MEM_HEAD_DIM, (h + 1) * MEM_HEAD_DIM)
            qh = q_ref[:, sl] * jnp.asarray(scale, BF16)
            p = _mem_softmax(qh, k_ref[:, sl])
            o_ref[:, sl] = jnp.dot(p.astype(BF16), v_ref[:, sl], preferred_element_type=F32).astype(o_ref.dtype)

    return pl.pallas_call(
        body, name=name, grid=(S // tm,),
        in_specs=[
            pl.BlockSpec((tm, D_MODEL), lambda i: (i, PB_QMEM)),
            pl.BlockSpec((M, D_MODEL), lambda i: (0, 0)),
            pl.BlockSpec((M, D_MODEL), lambda i: (0, 1)),
        ],
        out_specs=pl.BlockSpec((tm, D_MODEL), lambda i: (i, 0)),
        out_shape=jax.ShapeDtypeStruct((S, D_MODEL), BF16),
        compiler_params=_cparams(("parallel",)),
    )(P, kv, kv)


def _mem_bwd(P, do, kv, *, name, tm=512):
    S, M = P.shape[0], kv.shape[0]
    tm = _tile(S, tm)
    scale = MEM_HEAD_DIM ** -0.5

    def body(q_ref, k_ref, v_ref, do_ref, dq_ref, dkv_ref):
        i = pl.program_id(0)

        @pl.when(i == 0)
        def _():
            dkv_ref[...] = jnp.zeros_like(dkv_ref)

        for h in range(MEM_HEADS):
            sl = slice(h * MEM_HEAD_DIM, (h + 1) * MEM_HEAD_DIM)
            slv = slice(D_MODEL + h * MEM_HEAD_DIM, D_MODEL + (h + 1) * MEM_HEAD_DIM)
            qh = q_ref[:, sl] * jnp.asarray(scale, BF16)
            kh, vh, doh = k_ref[:, sl], v_ref[:, sl], do_ref[:, sl]
            p = _mem_softmax(qh, kh)
            dp = lax.dot_general(doh, vh, NT_DIMS, preferred_element_type=F32)
            ds = p * (dp - jnp.sum(p * dp, axis=1, keepdims=True))
            dsb = ds.astype(BF16)
            dq_ref[:, sl] = (jnp.dot(dsb, kh, preferred_element_type=F32) * scale).astype(dq_ref.dtype)
            dkv_ref[:, sl] += lax.dot_general(dsb, qh, TN_DIMS, preferred_element_type=F32)
            dkv_ref[:, slv] += lax.dot_general(p.astype(BF16), doh, TN_DIMS, preferred_element_type=F32)

    row = pl.BlockSpec((tm, D_MODEL), lambda i: (i, 0))
    return pl.pallas_call(
        body, name=name, grid=(S // tm,),
        in_specs=[
            pl.BlockSpec((tm, D_MODEL), lambda i: (i, PB_QMEM)),
            pl.BlockSpec((M, D_MODEL), lambda i: (0, 0)),
            pl.BlockSpec((M, D_MODEL), lambda i: (0, 1)),
            row,
        ],
        out_specs=[row, pl.BlockSpec((M, 2 * D_MODEL), lambda i: (0, 0))],
        out_shape=[jax.ShapeDtypeStruct((S, D_MODEL), BF16), jax.ShapeDtypeStruct((M, 2 * D_MODEL), F32)],
        compiler_params=_cparams(("arbitrary",)),
    )(P, kv, kv, do)


def _merge_fwd(o_attn, o_conv, o_mem, P, wpa, wpc, wpm, w_out, *, name, tm=256):
    S = P.shape[0]
    D = D_MODEL
    tm = _tile(S, tm)

    def body(oa_ref, oc_ref, om_ref, g0_ref, g1_ref, g2_ref, wa_ref, wc_ref, wm_ref, wo_ref, mg_ref, y_ref, pb_ref):
        merged = jnp.zeros((tm, D), F32)
        for b, (o_ref, g_ref, w_ref) in enumerate(((oa_ref, g0_ref, wa_ref), (oc_ref, g1_ref, wc_ref), (om_ref, g2_ref, wm_ref))):
            pb = jnp.dot(o_ref[...], w_ref[...], preferred_element_type=F32)
            pb_ref[b] = pb.astype(pb_ref.dtype)
            merged = merged + _sigmoid(g_ref[...].astype(F32)) * pb
        mb = merged.astype(BF16)
        mg_ref[...] = mb
        y_ref[...] = jnp.dot(mb, wo_ref[...], preferred_element_type=F32)

    row = pl.BlockSpec((tm, D), lambda i: (i, 0))
    wsp = pl.BlockSpec((D, D), lambda i: (0, 0))
    return pl.pallas_call(
        body, name=name, grid=(S // tm,),
        in_specs=[row, row, row] + [pl.BlockSpec((tm, D), lambda i, b=b: (i, PB_G0 + b)) for b in range(3)] + [wsp] * 4,
        out_specs=[row, row, pl.BlockSpec((3, tm, D), lambda i: (0, i, 0))],
        out_shape=[jax.ShapeDtypeStruct((S, D), BF16), jax.ShapeDtypeStruct((S, D), F32), jax.ShapeDtypeStruct((3, S, D), BF16)],
        compiler_params=_cparams(("parallel",), 56),
    )(o_attn, o_conv, o_mem, P, P, P, wpa, wpc, wpm, w_out)


def _merge_bwd(dmerged, P, pb, *, name, tm=256):
    S = P.shape[0]
    D = D_MODEL
    tm = _tile(S, tm)

    def body(dm_ref, g0_ref, g1_ref, g2_ref, pb_ref, dpb_ref, dgl_ref):
        dm = dm_ref[...].astype(F32)
        for b, g_ref in enumerate((g0_ref, g1_ref, g2_ref)):
            g = _sigmoid(g_ref[...].astype(F32))
            dpb_ref[b] = (dm * g).astype(dpb_ref.dtype)
            dgl_ref[:, b * D:(b + 1) * D] = (dm * pb_ref[b].astype(F32) * g * (1.0 - g)).astype(dgl_ref.dtype)

    row = pl.BlockSpec((tm, D), lambda i: (i, 0))
    blk3 = pl.BlockSpec((3, tm, D), lambda i: (0, i, 0))
    return pl.pallas_call(
        body, name=name, grid=(S // tm,),
        in_specs=[row] + [pl.BlockSpec((tm, D), lambda i, b=b: (i, PB_G0 + b)) for b in range(3)] + [blk3],
        out_specs=[blk3, pl.BlockSpec((tm, 3 * D), lambda i: (i, 0))],
        out_shape=[jax.ShapeDtypeStruct((3, S, D), BF16), jax.ShapeDtypeStruct((S, 3 * D), BF16)],
        compiler_params=_cparams(("parallel",), 56),
    )(dmerged, P, P, P, pb)


def _resid_norm(x, y1, g_post, g_pre, *, name, tm=512):
    S, D = x.shape
    tm = _tile(S, tm)

    def body(x_ref, y_ref, gp_ref, gq_ref, x1_ref, h2_ref):
        yv = y_ref[...]
        x1 = x_ref[...] + yv * lax.rsqrt(jnp.mean(yv * yv, axis=-1, keepdims=True) + RMS_EPS) * gp_ref[...]
        x1_ref[...] = x1
        h2_ref[...] = (x1 * lax.rsqrt(jnp.mean(x1 * x1, axis=-1, keepdims=True) + RMS_EPS) * gq_ref[...]).astype(h2_ref.dtype)

    row = pl.BlockSpec((tm, D), lambda i: (i, 0))
    vec = pl.BlockSpec((1, D), lambda i: (0, 0))
    return pl.pallas_call(
        body, name=name, grid=(S // tm,),
        in_specs=[row, row, vec, vec], out_specs=[row, row],
        out_shape=[jax.ShapeDtypeStruct((S, D), F32), jax.ShapeDtypeStruct((S, D), BF16)],
        compiler_params=_cparams(("parallel",)),
    )(x, y1, g_post, g_pre)


def _swiglu_fwd(h2, w_gu, *, name, tm=512, tn=1408):
    S, D = h2.shape
    Fh = w_gu.shape[1] // 2
    tm, tn = _tile(S, tm), _tile(Fh, tn)
    nj = Fh // tn

    def body(h_ref, wg_ref, wu_ref, g_ref, u_ref, a_ref):
        hv = h_ref[...]
        g = jnp.dot(hv, wg_ref[...], preferred_element_type=F32)
        u = jnp.dot(hv, wu_ref[...], preferred_element_type=F32)
        g_ref[...] = g.astype(g_ref.dtype)
        u_ref[...] = u.astype(u_ref.dtype)
        a_ref[...] = (g * _sigmoid(g) * u).astype(a_ref.dtype)

    out = pl.BlockSpec((tm, tn), lambda j, i: (i, j))
    sds = jax.ShapeDtypeStruct((S, Fh), BF16)
    return pl.pallas_call(
        body, name=name, grid=(nj, S // tm),
        in_specs=[
            pl.BlockSpec((tm, D), lambda j, i: (i, 0)),
            pl.BlockSpec((D, tn), lambda j, i: (0, j)),
            pl.BlockSpec((D, tn), lambda j, i: (0, j + nj)),
        ],
        out_specs=[out, out, out], out_shape=[sds, sds, sds],
        compiler_params=_cparams(("parallel", "parallel"), 56),
    )(h2, w_gu, w_gu)


def _swiglu_bwd(dact, g, u, *, name, tm=512, tn=1408):
    S, Fh = g.shape
    tm, tn = _tile(S, tm), _tile(Fh, tn)
    nj = Fh // tn

    def body(da_ref, g_ref, u_ref, o_ref):
        j = pl.program_id(1)
        da = da_ref[...].astype(F32)
        gv = g_ref[...].astype(F32)
        sg = _sigmoid(gv)

        @pl.when(j < nj)
        def _():
            o_ref[...] = (da * u_ref[...].astype(F32) * (sg * (1.0 + gv * (1.0 - sg)))).astype(o_ref.dtype)

        @pl.when(j >= nj)
        def _():
            o_ref[...] = (da * gv * sg).astype(o_ref.dtype)

    blk = pl.BlockSpec((tm, tn), lambda i, j: (i, j % nj))
    return pl.pallas_call(
        body, name=name, grid=(S // tm, 2 * nj),
        in_specs=[blk, blk, blk],
        out_specs=pl.BlockSpec((tm, tn), lambda i, j: (i, j)),
        out_shape=jax.ShapeDtypeStruct((S, 2 * Fh), BF16),
        compiler_params=_cparams(("parallel", "arbitrary")),
    )(dact, g, u)


def _final(x1, ffn, target, g, *, name, tm=512):
    S, D = x1.shape
    tm = _tile(S, tm)

    def body(x_ref, f_ref, t_ref, g_ref, dout_ref, dffn_ref, loss_ref, dg_ref):
        i = pl.program_id(0)
        fv = f_ref[...]
        rstd = lax.rsqrt(jnp.mean(fv * fv, axis=-1, keepdims=True) + RMS_EPS)
        r = fv * rstd
        e = x_ref[...] + r * g_ref[...] - t_ref[...]
        dout = e * (1.0 / D)
        dout_ref[...] = dout
        gy = dout * g_ref[...]
        dffn_ref[...] = (rstd * (gy - r * jnp.mean(gy * r, axis=-1, keepdims=True))).astype(dffn_ref.dtype)
        lpart = jnp.full((8, LANES), 0.5 * jnp.sum(jnp.mean(e * e, axis=-1, keepdims=True)), F32)
        gpart = jnp.sum(dout * r, axis=0, keepdims=True)

        @pl.when(i == 0)
        def _():
            loss_ref[...] = lpart
            dg_ref[...] = gpart

        @pl.when(i > 0)
        def _():
            loss_ref[...] += lpart
            dg_ref[...] += gpart

    row = pl.BlockSpec((tm, D), lambda i: (i, 0))
    vec = pl.BlockSpec((1, D), lambda i: (0, 0))
    return pl.pallas_call(
        body, name=name, grid=(S // tm,),
        in_specs=[row, row, row, vec],
        out_specs=[row, row, pl.BlockSpec((8, LANES), lambda i: (0, 0)), vec],
        out_shape=[jax.ShapeDtypeStruct((S, D), F32), jax.ShapeDtypeStruct((S, D), BF16),
                   jax.ShapeDtypeStruct((8, LANES), F32), jax.ShapeDtypeStruct((1, D), F32)],
        compiler_params=_cparams(("arbitrary",)),
    )(x1, ffn, target, g)


def _after(token, value):
    return value if token is None else value + token[0, 0].astype(value.dtype)


def _local_step(x, mem, target, gains, w, late_weights, send_grads, *, tq=512):
    S = x.shape[0]
    b_pad = jnp.pad(gains["b_forget"], ((0, 0), (0, LANES - FOX_HEADS)))
    gains = dict(gains)

    h = _rms_fwd(x, gains["norm_mix_pre"], name="rms_mix_pre")
    P = _matmul(h, w["wp"], mode="nn", out_dtype=BF16, name="proj_in")
    f_logit = _matmul(h, w["wf"], mode="nn", out_dtype=F32, name="proj_forget")
    logf = _forget_fwd(f_logit, b_pad, name="forget_fwd")
    c_row16 = _cumsum_lanes(logf[:, :FOX_HEADS].T, reverse=False, name="forget_cumsum")
    ccol = c_row16[:, :, None]
    crow = c_row16.reshape(HEAD_PAIRS, 2, S)
    o_attn, o_attn32, lse = _attn_fwd(P, ccol, crow, tq=tq, name="attn_fwd")
    w = dict(w, **late_weights(o_attn))
    o_conv, y_conv = _conv_fwd(P, w["conv_w"], gains["conv_b"], gains["conv_ln_g"], gains["conv_ln_b"], name="conv_fwd")
    mem_n = _rms_fwd(mem, gains["norm_mem"], name="rms_mem")
    kv = _matmul(mem_n, w["w_kv"], mode="nn", out_dtype=BF16, name="mem_kv")
    o_mem = _mem_fwd(P, kv, name="mem_fwd")
    merged, y1, pb = _merge_fwd(o_attn, o_conv, o_mem, P, w["wpa"], w["wpc"], w["wpm"], w["w_out"], name="merge_fwd")
    x1, h2 = _resid_norm(x, y1, gains["norm_mix_post"], gains["norm_ffn_pre"], name="resid_norm")
    g_ffn, u_ffn, act = _swiglu_fwd(h2, w["w_gu"], name="swiglu_fwd")
    ffn = _matmul(act, w["w_down"], mode="nn", out_dtype=F32, name="ffn_down", tk=1408)

    dout, dffn, loss_tile, d_norm_ffn_post = _final(x1, ffn, target, gains["norm_ffn_post"], name="loss_head")
    dact = _matmul(dffn, w["w_down"], mode="nt", out_dtype=BF16, name="d_act", tn=1408)
    d_w_down = _matmul(act, dffn, mode="tn", out_dtype=BF16, name="dw_down", tm=1408)
    dgu = _swiglu_bwd(dact, g_ffn, u_ffn, name="swiglu_bwd")
    dh2 = _matmul(dgu, w["w_gu"], mode="nt", out_dtype=F32, name="d_h2", tk=1408)
    d_w_gu = _matmul(h2, dgu, mode="tn", out_dtype=BF16, name="dw_gate_up", tn=1408)
    sent = send_grads("ffn", dict(w_gate_up=d_w_gu, w_down=d_w_down))
    dx1, d_norm_ffn_pre = _rms_bwd(x1, [dh2], _after(sent, gains["norm_ffn_pre"]), dout, out_dtype=F32, name="rms_ffn_pre_bwd")

    dy1, d_norm_mix_post = _rms_bwd(y1, [dx1], gains["norm_mix_post"], None, out_dtype=BF16, name="rms_mix_post_bwd")
    dmerged = _matmul(dy1, w["w_out"], mode="nt", out_dtype=F32, name="d_merged")
    d_w_out = _matmul(merged, dy1, mode="tn", out_dtype=BF16, name="dw_out")
    dpb, dgl = _merge_bwd(dmerged, P, pb, name="merge_bwd")
    do_attn = _matmul(dpb[0], w["wpa"], mode="nt", out_dtype=BF16, name="d_o_attn")
    do_conv = _matmul(dpb[1], w["wpc"], mode="nt", out_dtype=F32, name="d_o_conv")
    do_mem = _matmul(dpb[2], w["wpm"], mode="nt", out_dtype=BF16, name="d_o_mem")
    d_wpa = _matmul(o_attn, dpb[0], mode="tn", out_dtype=BF16, name="dw_proj_attn")
    d_wpc = _matmul(o_conv, dpb[1], mode="tn", out_dtype=BF16, name="dw_proj_conv")
    d_wpm = _matmul(o_mem, dpb[2], mode="tn", out_dtype=BF16, name="dw_proj_mem")

    dq_mem, dkv = _mem_bwd(P, do_mem, kv, name="mem_bwd")
    dkv_b = dkv.astype(BF16)
    d_w_kv = _matmul(mem_n, dkv_b, mode="tn", out_dtype=BF16, name="dw_kv")
    dmem_n = _matmul(dkv_b, w["w_kv"], mode="nt", out_dtype=F32, name="d_mem_n")
    _, d_norm_mem = _rms_bwd(mem, [dmem_n], gains["norm_mem"], None, out_dtype=BF16, name="rms_mem_bwd")

    dglu, d_conv_w, conv_small = _conv_bwd(P, do_conv, y_conv, w["conv_w"], gains["conv_ln_g"], gains["conv_ln_b"], name="conv_bwd")
    sent = send_grads("mixer", dict(conv_w=d_conv_w[:CONV_WIDTH].astype(BF16), w_kv_mem=d_w_kv, w_proj_attn=d_wpa,
                                    w_proj_conv=d_wpc, w_proj_mem=d_wpm, w_out=d_w_out))

    delta = _attn_delta(do_attn, o_attn32, name="attn_delta")
    dq, dk, dv, dc = _attn_bwd(P, do_attn, lse, delta, ccol, _after(sent, crow), tq=tq, name="attn_bwd")
    dlogf16 = _cumsum_lanes(dc.reshape(FOX_HEADS, S), reverse=True, name="forget_cumsum_bwd")
    dlogf = jnp.pad(dlogf16.T, ((0, 0), (0, LANES - FOX_HEADS)))
    df, d_b_pad = _forget_bwd(f_logit, b_pad, dlogf, name="forget_bwd")

    dP = jnp.concatenate([dq, dk, dv, dglu, dq_mem, dgl], axis=1)
    d_wp = _matmul(h, dP, mode="tn", out_dtype=BF16, name="dw_in")
    d_wf = _matmul(h, df, mode="tn", out_dtype=BF16, name="dw_forget")
    n_qkv = 3 * D_MODEL
    sent = send_grads("w_in", dict(w_in=jnp.concatenate([d_wp[:, :n_qkv], d_wf[:, :FOX_HEADS], d_wp[:, n_qkv:]], axis=1)))
    dh_p = _matmul(dP, _after(sent, w["wp"]), mode="nt", out_dtype=F32, name="d_h")
    dh_f = _matmul(df, w["wf"], mode="nt", out_dtype=F32, name="d_h_forget")
    grad_x, d_norm_mix_pre = _rms_bwd(x, [dh_p, dh_f], gains["norm_mix_pre"], dx1, out_dtype=F32, name="rms_mix_pre_bwd")

    small = dict(norm_mix_pre=d_norm_mix_pre, norm_mix_post=d_norm_mix_post, norm_mem=d_norm_mem,
                 conv_b=conv_small[2:3], conv_ln_g=conv_small[0:1], conv_ln_b=conv_small[1:2],
                 norm_ffn_pre=d_norm_ffn_pre, norm_ffn_post=d_norm_ffn_post, b_forget=d_b_pad[:, :FOX_HEADS])
    return loss_tile, grad_x, small


def _mesh_pos():
    return lax.axis_index("x"), lax.axis_index("y"), lax.axis_index("c")


def _flip(pos, d):
    x, y, c = pos
    return (1 - x if d & 4 else x, 1 - y if d & 2 else y, 1 - c if d & 1 else c)


def _flat(pos):
    x, y, c = pos
    return 4 * x + 2 * y + c


def _exchange(srcs, *, scatter, name):
    T = len(srcs)
    n_peer = N_DEV - 1

    def body(*refs):
        src_refs, out_refs = refs[:T], refs[T:2 * T]
        token, send_sems, recv_sems, local_sems = refs[2 * T:]
        me = _mesh_pos()
        my = _flat(me)

        def copy(t, d):
            peer = _flip(me, d)
            return pltpu.make_async_remote_copy(
                src_ref=src_refs[t].at[_flat(peer)] if scatter else src_refs[t],
                dst_ref=out_refs[t].at[my],
                send_sem=send_sems.at[t * n_peer + d - 1], recv_sem=recv_sems.at[t * n_peer + d - 1],
                device_id=peer, device_id_type=pl.DeviceIdType.MESH)

        def arrival(t, d):
            peer = _flip(me, d)
            return pltpu.make_async_remote_copy(
                src_ref=src_refs[t].at[my] if scatter else src_refs[t],
                dst_ref=out_refs[t].at[_flat(peer)],
                send_sem=send_sems.at[t * n_peer + d - 1], recv_sem=recv_sems.at[t * n_peer + d - 1],
                device_id=peer, device_id_type=pl.DeviceIdType.MESH)

        local = [pltpu.make_async_copy(src_refs[t].at[my] if scatter else src_refs[t], out_refs[t].at[my], local_sems.at[t])
                 for t in range(T)]
        for cp in local:
            cp.start()
        sends = [copy(t, d) for d in range(1, N_DEV) for t in range(T)]
        for cp in sends:
            cp.start()
        for d in range(1, N_DEV):
            for t in range(T):
                arrival(t, d).wait_recv()
        for cp in sends:
            cp.wait_send()
        for cp in local:
            cp.wait()
        token[...] = jnp.zeros_like(token)

    outs = pl.pallas_call(
        body, name=name,
        in_specs=[pl.BlockSpec(memory_space=pl.ANY)] * T,
        out_specs=[pl.BlockSpec(memory_space=pl.ANY)] * T + [pl.BlockSpec(memory_space=pltpu.VMEM)],
        out_shape=[jax.ShapeDtypeStruct((N_DEV,) + tuple(s.shape[-2:]), s.dtype) for s in srcs] + [jax.ShapeDtypeStruct((8, LANES), F32)],
        scratch_shapes=[pltpu.SemaphoreType.DMA((T * n_peer,)), pltpu.SemaphoreType.DMA((T * n_peer,)), pltpu.SemaphoreType.DMA((T,))],
    )(*srcs)
    return list(outs[:T]), outs[T]


HBM_SPEC = pl.BlockSpec(memory_space=pltpu.HBM)
SEM_SPEC = pl.BlockSpec(memory_space=pltpu.SEMAPHORE)


def _split_copies(src_refs, land_refs, send_sems, recv_sems, scatter):
    me = _mesh_pos()
    my = _flat(me)
    n_peer = N_DEV - 1
    out = []
    for d in range(1, N_DEV):
        peer = _flip(me, d)
        for t, (src, land) in enumerate(zip(src_refs, land_refs)):
            k = t * n_peer + d - 1
            start = pltpu.make_async_remote_copy(
                src_ref=src.at[_flat(peer)] if scatter else src, dst_ref=land.at[my],
                send_sem=send_sems.at[k], recv_sem=recv_sems.at[k], device_id=peer, device_id_type=pl.DeviceIdType.MESH)
            arrive = pltpu.make_async_remote_copy(
                src_ref=src.at[my] if scatter else src, dst_ref=land.at[_flat(peer)],
                send_sem=send_sems.at[k], recv_sem=recv_sems.at[k], device_id=peer, device_id_type=pl.DeviceIdType.MESH)
            out.append((start, arrive))
    return out


def _exchange_start(srcs, *, scatter, name):
    T = len(srcs)
    n_sem = T * (N_DEV - 1)
    lands = [lax.empty((N_DEV,) + tuple(s.shape[-2:]), s.dtype) for s in srcs]

    def body(*refs):
        src_refs, land_refs = refs[:T], refs[T:2 * T]
        send_sems, recv_sems = refs[2 * T], refs[2 * T + 1]
        token = refs[-1]
        my = _flat(_mesh_pos())
        for src, land in zip(src_refs, land_refs):
            pltpu.sync_copy(src.at[my] if scatter else src, land.at[my])
        for start, _ in _split_copies(src_refs, land_refs, send_sems, recv_sems, scatter):
            start.start()
        token[...] = jnp.zeros_like(token)

    hbm = lambda a: pltpu.HBM(a.shape, a.dtype)
    outs = pl.pallas_call(
        body, name=name,
        in_specs=[HBM_SPEC] * (2 * T),
        out_specs=(SEM_SPEC, SEM_SPEC, *[HBM_SPEC] * (2 * T), pl.BlockSpec(memory_space=pltpu.VMEM)),
        out_shape=(pltpu.SemaphoreType.DMA((n_sem,)), pltpu.SemaphoreType.DMA((n_sem,)),
                   *[hbm(s) for s in srcs], *[hbm(a) for a in lands], jax.ShapeDtypeStruct((8, LANES), F32)),
        input_output_aliases={t: 2 + t for t in range(2 * T)},
        compiler_params=pltpu.CompilerParams(has_side_effects=pltpu.SideEffectType.DATAFLOW_SIDE_EFFECTING),
    )(*[pltpu.with_memory_space_constraint(s, pltpu.HBM) for s in srcs],
      *[pltpu.with_memory_space_constraint(a, pltpu.HBM) for a in lands])
    return (outs[0], outs[1], list(outs[2:2 + T]), list(outs[2 + T:2 + 2 * T])), outs[-1]


def _exchange_wait(handle, after, *, scatter, name):
    send_sems, recv_sems, srcs, lands = handle
    T = len(srcs)

    def body(*refs):
        src_refs, land_refs = refs[:T], refs[T:2 * T]
        s_sems, r_sems = refs[2 * T], refs[2 * T + 1]
        for start, arrive in _split_copies(src_refs, land_refs, s_sems, r_sems, scatter):
            start.wait_send()
            arrive.wait_recv()

    hbm = lambda a: pltpu.HBM(a.shape, a.dtype)
    outs = pl.pallas_call(
        body, name=name,
        in_specs=[HBM_SPEC] * (2 * T) + [SEM_SPEC, SEM_SPEC, pl.BlockSpec(memory_space=pl.ANY)],
        out_specs=[HBM_SPEC] * (2 * T),
        out_shape=[hbm(s) for s in srcs] + [hbm(a) for a in lands],
        input_output_aliases={t: t for t in range(2 * T)},
        compiler_params=pltpu.CompilerParams(has_side_effects=pltpu.SideEffectType.DATAFLOW_SIDE_EFFECTING),
    )(*srcs, *lands, send_sems, recv_sems, after)
    return list(outs[T:])


def _adamw(w, slots, m, v, *, name):
    R, C = w.shape
    tr = R if R <= 512 else 256
    assert R % tr == 0, (R, tr)

    def body(w_ref, s_ref, m_ref, v_ref, g_ref, d_ref, m2_ref, v2_ref):
        gv = s_ref[0].astype(F32)
        for j in range(1, N_DEV):
            gv = gv + s_ref[j].astype(F32)
        g_ref[...] = gv
        m2 = ADAM_B1 * m_ref[...] + (1.0 - ADAM_B1) * gv
        v2 = ADAM_B2 * v_ref[...] + (1.0 - ADAM_B2) * (gv * gv)
        m_hat = m2 / (1.0 - ADAM_B1 ** ADAM_STEP)
        v_hat = v2 / (1.0 - ADAM_B2 ** ADAM_STEP)
        d_ref[...] = -ADAM_LR * (m_hat / (jnp.sqrt(v_hat) + ADAM_EPS) + ADAM_WD * w_ref[...])
        m2_ref[...] = m2
        v2_ref[...] = v2

    blk = pl.BlockSpec((tr, C), lambda i: (i, 0))
    sds = jax.ShapeDtypeStruct((R, C), F32)
    return pl.pallas_call(
        body, name=name, grid=(R // tr,),
        in_specs=[blk, pl.BlockSpec((N_DEV, tr, C), lambda i: (0, i, 0)), blk, blk],
        out_specs=[blk] * 4, out_shape=[sds] * 4,
        compiler_params=_cparams(("parallel",)),
    )(w, slots, m, v)


SHARDED = (
    ("w_in", (1024, 1154), "col"), ("conv_w", (31, 128), "col"), ("w_kv_mem", (1024, 256), "col"),
    ("w_proj_attn", (128, 1024), "row"), ("w_proj_conv", (128, 1024), "row"), ("w_proj_mem", (128, 1024), "row"),
    ("w_out", (128, 1024), "row"), ("w_gate_up", (1024, 704), "col"), ("w_down", (352, 1024), "row"),
)
SMALL = ("norm_mix_pre", "norm_mix_post", "norm_mem", "conv_b", "conv_ln_g", "conv_ln_b", "norm_ffn_pre", "norm_ffn_post", "b_forget")
SMALL_ROWS = 16
LOSS_ROW = len(SMALL)
WEIGHT_ORDER = ("norm_mix_pre", "norm_mix_post", "norm_mem", "w_in", "b_forget", "conv_w", "conv_b", "conv_ln_g", "conv_ln_b",
                "w_kv_mem", "w_proj_attn", "w_proj_conv", "w_proj_mem", "w_out", "norm_ffn_pre", "norm_ffn_post", "w_gate_up", "w_down")


def _to_full(blocks8, kind):
    n, r, c = blocks8.shape
    if kind == "col":
        return jnp.concatenate([blocks8[j] for j in range(n)], axis=1)
    return blocks8.reshape(n * r, c)


def _to_blocks(full, kind):
    if kind == "col":
        c = full.shape[1] // N_DEV
        return jnp.stack([full[:, j * c:(j + 1) * c] for j in range(N_DEV)])
    nr, c = full.shape
    return full.reshape(N_DEV, nr // N_DEV, c)


def kernel(x, mem, norm_mix_pre, norm_mix_post, norm_mem, w_in, b_forget, conv_w, conv_b, conv_ln_g, conv_ln_b, w_kv_mem, w_proj_attn, w_proj_conv, w_proj_mem, w_out, norm_ffn_pre, norm_ffn_post, w_gate_up, w_down, loss_target, m_norm_mix_pre, m_norm_mix_post, m_norm_mem, m_w_in, m_b_forget, m_conv_w, m_conv_b, m_conv_ln_g, m_conv_ln_b, m_w_kv_mem, m_w_proj_attn, m_w_proj_conv, m_w_proj_mem, m_w_out, m_norm_ffn_pre, m_norm_ffn_post, m_w_gate_up, m_w_down, v_norm_mix_pre, v_norm_mix_post, v_norm_mem, v_w_in, v_b_forget, v_conv_w, v_conv_b, v_conv_ln_g, v_conv_ln_b, v_w_kv_mem, v_w_proj_attn, v_w_proj_conv, v_w_proj_mem, v_w_out, v_norm_ffn_pre, v_norm_ffn_post, v_w_gate_up, v_w_down):
    given = dict(locals())
    weights = {n: given[n] for n in WEIGHT_ORDER}
    moments_m = {n: given["m_" + n] for n in WEIGHT_ORDER}
    moments_v = {n: given["v_" + n] for n in WEIGHT_ORDER}
    kind = {name: k for name, _, k in SHARDED}
    names = [name for name, _, _ in SHARDED]

    blocks = {n: weights[n][0].astype(BF16) for n in names if n != "conv_w"}
    blocks["conv_w"] = lax.bitcast_convert_type(conv_w[0], BF16).reshape(CONV_WIDTH, 2 * LANES)
    late = [n for n in names if n != "w_in"]
    (w_in_blocks,), w_in_token = _exchange([blocks["w_in"]], scatter=False, name="gather_w_in")
    blocks["w_kv_mem"] = _after(w_in_token, blocks["w_kv_mem"])
    late_handle, late_token = _exchange_start([blocks[n] for n in late], scatter=False, name="gather_rest_start")
    n_qkv = 3 * D_MODEL
    w_in_full = _to_full(w_in_blocks, "col")
    w_early = dict(
        wp=jnp.concatenate([w_in_full[:, :n_qkv], w_in_full[:, n_qkv + FOX_HEADS:]], axis=1),
        wf=jnp.pad(w_in_full[:, n_qkv:n_qkv + FOX_HEADS], ((0, 0), (0, LANES - FOX_HEADS))),
    )

    def late_weights(after):
        got = dict(zip(late, _exchange_wait(late_handle, after, scatter=False, name="gather_rest_wait")))
        conv_w_all = lax.bitcast_convert_type(got["conv_w"].reshape(N_DEV, CONV_WIDTH, LANES, 2), F32)
        full = {n: _to_full(got[n], kind[n]) for n in late if n != "conv_w"}
        return dict(
            conv_w=jnp.pad(_to_full(conv_w_all, "col"), ((0, CONV_HALO - CONV_WIDTH), (0, 0))),
            w_kv=full["w_kv_mem"], wpa=full["w_proj_attn"], wpc=full["w_proj_conv"], wpm=full["w_proj_mem"],
            w_out=full["w_out"], w_gu=full["w_gate_up"], w_down=full["w_down"],
        )

    sent = []

    def send_grads(group, g):
        order = [n for n in names if n in g]
        handle, token = _exchange_start([_to_blocks(g[n], kind[n]) for n in order], scatter=True, name="scatter_" + group + "_start")
        sent.append((group, order, handle))
        return token

    gains = {n: weights[n] for n in SMALL}
    gains["norm_mix_pre"] = _after(late_token, gains["norm_mix_pre"])

    loss_tile, grad_x, small = _local_step(x[0], mem[0], loss_target[0], gains, w_early, late_weights, send_grads)

    g_recv = {}
    for group, order, handle in sent:
        g_recv.update(zip(order, _exchange_wait(handle, grad_x, scatter=True, name="scatter_" + group + "_wait")))

    rows = [jnp.pad(small[n], ((0, 0), (0, D_MODEL - small[n].shape[1]))) for n in SMALL]
    rows.append(jnp.broadcast_to(loss_tile[0:1, 0:1], (1, D_MODEL)))
    rows.append(jnp.zeros((SMALL_ROWS - len(rows), D_MODEL), F32))
    (small_recv,), _ = _exchange([jnp.concatenate(rows, axis=0)], scatter=False, name="gather_small")

    def slab_of(d, fill):
        rows = [jnp.pad(d[n], ((0, 0), (0, D_MODEL - d[n].shape[1])), constant_values=fill) for n in SMALL]
        rows.append(jnp.full((SMALL_ROWS - len(rows), D_MODEL), fill, F32))
        return jnp.concatenate(rows, axis=0)

    grads, delta, new_m, new_v = {}, {}, {}, {}
    sg, sd, sm, sv = _adamw(slab_of(weights, 0.0), small_recv, slab_of(moments_m, 0.0), slab_of(moments_v, 1.0), name="adamw_small")
    loss = sg[LOSS_ROW, 0]
    for i, n in enumerate(SMALL):
        c = weights[n].shape[1]
        grads[n], delta[n], new_m[n], new_v[n] = sg[i:i + 1, :c], sd[i:i + 1, :c], sm[i:i + 1, :c], sv[i:i + 1, :c]
    for n in names:
        gw, d, m2, v2 = _adamw(weights[n][0], g_recv[n], moments_m[n][0], moments_v[n][0], name="adamw_" + n)
        grads[n], delta[n], new_m[n], new_v[n] = gw[None], d[None], m2[None], v2[None]

    return (loss, grad_x[None], *[grads[n] for n in WEIGHT_ORDER], *[delta[n] for n in WEIGHT_ORDER],
            *[new_m[n] for n in WEIGHT_ORDER], *[new_v[n] for n in WEIGHT_ORDER])
```

```python
import functools

import jax
import jax.numpy as jnp
from jax import lax
from jax.experimental import pallas as pl
from jax.experimental.pallas import tpu as pltpu

F32 = jnp.float32
BF16 = jnp.bfloat16

D_MODEL = 1024
N_DEV = 8
FOX_HEADS = 16
FOX_HEAD_DIM = 64
HEAD_PAIRS = FOX_HEADS // 2
MEM_HEADS = 4
MEM_HEAD_DIM = D_MODEL // MEM_HEADS
CONV_WIDTH = 31
CONV_HALO = 32
FFN_HIDDEN = 2816
RMS_EPS = 1e-6
LN_EPS = 1e-5
ADAM_LR = 0.001
ADAM_B1 = 0.9
ADAM_B2 = 0.999
ADAM_EPS = 1e-08
ADAM_WD = 0.01
ADAM_STEP = 10
NEG_BIG = -1e30
LANES = 128

PB_Q, PB_K, PB_V, PB_A, PB_GATE, PB_QMEM, PB_G0 = 0, 1, 2, 3, 4, 5, 6
P_WIDTH = 9 * D_MODEL

NT_DIMS = (((1,), (1,)), ((), ()))
TN_DIMS = (((0,), (0,)), ((), ()))


def _cparams(sem, vmem_mb=None):
    kw = dict(dimension_semantics=sem)
    if vmem_mb is not None:
        kw["vmem_limit_bytes"] = vmem_mb * 1024 * 1024
    return pltpu.CompilerParams(**kw)


def _tile(dim, want):
    t = min(dim, want)
    assert dim % t == 0, (dim, want)
    return t


def _sigmoid(z):
    return 1.0 / (1.0 + jnp.exp(-z))


def _matmul(a, b, *, mode, out_dtype, name, tm=1024, tn=1024, tk=1024):
    if mode == "nn":
        (M, K), (K2, N) = a.shape, b.shape
    elif mode == "nt":
        (M, K), (N, K2) = a.shape, b.shape
    else:
        (K, M), (K2, N) = a.shape, b.shape
    assert K == K2, (a.shape, b.shape, mode)
    tm, tn, tk = _tile(M, tm), _tile(N, tn), _tile(K, tk)
    nk = K // tk
    dims = {"nn": (((1,), (0,)), ((), ())), "nt": NT_DIMS, "tn": TN_DIMS}[mode]

    def body(a_ref, b_ref, o_ref, *scratch):
        part = lax.dot_general(a_ref[...], b_ref[...], dims, preferred_element_type=F32)
        if nk == 1:
            o_ref[...] = part.astype(o_ref.dtype)
        else:
            acc_ref, = scratch
            k = pl.program_id(2)

            @pl.when(k == 0)
            def _():
                acc_ref[...] = part

            @pl.when(k > 0)
            def _():
                acc_ref[...] += part

            @pl.when(k == nk - 1)
            def _():
                o_ref[...] = acc_ref[...].astype(o_ref.dtype)

    a_spec = pl.BlockSpec((tk, tm), lambda j, i, k: (k, i)) if mode == "tn" else pl.BlockSpec((tm, tk), lambda j, i, k: (i, k))
    b_spec = pl.BlockSpec((tn, tk), lambda j, i, k: (j, k)) if mode == "nt" else pl.BlockSpec((tk, tn), lambda j, i, k: (k, j))
    return pl.pallas_call(
        body,
        name=name,
        grid=(N // tn, M // tm, nk),
        in_specs=[a_spec, b_spec],
        out_specs=pl.BlockSpec((tm, tn), lambda j, i, k: (i, j)),
        out_shape=jax.ShapeDtypeStruct((M, N), out_dtype),
        scratch_shapes=[pltpu.VMEM((tm, tn), F32)] if nk > 1 else [],
        compiler_params=_cparams(("parallel", "parallel", "arbitrary"), 56),
    )(a, b)


def _rms_fwd(x, g, *, name, tm=512):
    S, D = x.shape
    tm = _tile(S, tm)

    def body(x_ref, g_ref, o_ref):
        xv = x_ref[...]
        rstd = lax.rsqrt(jnp.mean(xv * xv, axis=-1, keepdims=True) + RMS_EPS)
        o_ref[...] = (xv * rstd * g_ref[...]).astype(o_ref.dtype)

    return pl.pallas_call(
        body, name=name, grid=(S // tm,),
        in_specs=[pl.BlockSpec((tm, D), lambda i: (i, 0)), pl.BlockSpec((1, D), lambda i: (0, 0))],
        out_specs=pl.BlockSpec((tm, D), lambda i: (i, 0)),
        out_shape=jax.ShapeDtypeStruct((S, D), BF16),
        compiler_params=_cparams(("parallel",)),
    )(x, g)


def _rms_bwd(xin, dys, g, res, *, out_dtype, name, tm=512):
    S, D = xin.shape
    tm = _tile(S, tm)
    n_dy = len(dys)
    has_res = res is not None

    def body(*refs):
        x_ref, g_ref = refs[0], refs[1]
        dy_refs = refs[2:2 + n_dy]
        pos = 2 + n_dy
        res_ref = refs[pos] if has_res else None
        pos += int(has_res)
        dx_ref, dg_ref = refs[pos], refs[pos + 1]
        i = pl.program_id(0)
        xv = x_ref[...]
        dy = dy_refs[0][...].astype(F32)
        for r in dy_refs[1:]:
            dy = dy + r[...].astype(F32)
        rstd = lax.rsqrt(jnp.mean(xv * xv, axis=-1, keepdims=True) + RMS_EPS)
        xhat = xv * rstd
        gy = dy * g_ref[...]
        dx = rstd * (gy - xhat * jnp.mean(gy * xhat, axis=-1, keepdims=True))
        if has_res:
            dx = dx + res_ref[...]
        dx_ref[...] = dx.astype(dx_ref.dtype)
        part = jnp.sum(dy * xhat, axis=0, keepdims=True)

        @pl.when(i == 0)
        def _():
            dg_ref[...] = part

        @pl.when(i > 0)
        def _():
            dg_ref[...] += part

    row = pl.BlockSpec((tm, D), lambda i: (i, 0))
    vec = pl.BlockSpec((1, D), lambda i: (0, 0))
    ins = [xin, g] + list(dys) + ([res] if has_res else [])
    return pl.pallas_call(
        body, name=name, grid=(S // tm,),
        in_specs=[row, vec] + [row] * n_dy + ([row] if has_res else []),
        out_specs=[row, vec],
        out_shape=[jax.ShapeDtypeStruct((S, D), out_dtype), jax.ShapeDtypeStruct((1, D), F32)],
        compiler_params=_cparams(("arbitrary",)),
    )(*ins)


def _head_mask(hh, shape):
    lane = lax.broadcasted_iota(jnp.int32, shape, len(shape) - 1)
    return (lane // FOX_HEAD_DIM) == hh


def _attn_fwd(P, ccol, crow, *, tq, name):
    S = P.shape[0]
    tq = _tile(S, tq)
    nq = S // tq
    scale = FOX_HEAD_DIM ** -0.5

    def body(q_ref, k_ref, v_ref, cc_ref, cr_ref, o_ref, o32_ref, lse_ref):
        i = pl.program_id(1)
        q = q_ref[...] * jnp.asarray(scale, BF16)
        row = lax.broadcasted_iota(jnp.int32, (tq, tq), 0)
        col = lax.broadcasted_iota(jnp.int32, (tq, tq), 1)
        causal = col <= row
        hms = [_head_mask(hh, (tq, LANES)) for hh in range(2)]
        qhs = [jnp.where(hm, q, jnp.zeros_like(q)) for hm in hms]
        cbs = [cc_ref[hh, 0:1, :] for hh in range(2)]

        def step(j, carry, masked):
            start = pl.multiple_of(j * tq, tq)
            kj = k_ref[pl.ds(start, tq), :]
            vj = v_ref[pl.ds(start, tq), :]
            new = []
            for hh in range(2):
                m, acc = carry[hh]
                bias = cbs[hh] - cr_ref[0, hh:hh + 1, pl.ds(start, tq)]
                s = lax.dot_general(qhs[hh], kj, NT_DIMS, preferred_element_type=F32) + bias
                if masked:
                    s = jnp.where(causal, s, NEG_BIG)
                m_new = jnp.maximum(m, jnp.max(s, axis=1, keepdims=True))
                alpha = jnp.exp(m - m_new)
                p = jnp.exp(s - m_new)
                vh = jnp.where(ones_lane[hh], jnp.ones_like(vj), vj)
                acc = alpha * acc + jnp.dot(p.astype(BF16), vh, preferred_element_type=F32)
                new.append((m_new, acc))
            return tuple(new)

        lane = lax.broadcasted_iota(jnp.int32, (tq, LANES), 1)
        ones_lane = [lane == (1 - hh) * FOX_HEAD_DIM for hh in range(2)]
        init = (jnp.full((tq, 1), NEG_BIG, F32), jnp.zeros((tq, LANES), F32))
        carry = lax.fori_loop(0, i, functools.partial(step, masked=False), (init, init))
        carry = step(i, carry, True)
        ls = [jnp.sum(jnp.where(ones_lane[hh], carry[hh][1], 0.0), axis=1, keepdims=True) for hh in range(2)]
        out = jnp.where(hms[0], carry[0][1] / ls[0], carry[1][1] / ls[1])
        for hh in range(2):
            lse_ref[hh] = carry[hh][0] + jnp.log(ls[hh])
        o_ref[...] = out.astype(o_ref.dtype)
        o32_ref[...] = out

    nblk = D_MODEL // LANES
    return pl.pallas_call(
        body, name=name, grid=(HEAD_PAIRS, nq),
        in_specs=[
            pl.BlockSpec((tq, LANES), lambda p, i: (i, PB_Q * nblk + p)),
            pl.BlockSpec((S, LANES), lambda p, i: (0, PB_K * nblk + p)),
            pl.BlockSpec((S, LANES), lambda p, i: (0, PB_V * nblk + p)),
            pl.BlockSpec((2, tq, 1), lambda p, i: (p, i, 0)),
            pl.BlockSpec((1, 2, S), lambda p, i: (p, 0, 0)),
        ],
        out_specs=[
            pl.BlockSpec((tq, LANES), lambda p, i: (i, p)),
            pl.BlockSpec((tq, LANES), lambda p, i: (i, p)),
            pl.BlockSpec((2, tq, 1), lambda p, i: (p, i, 0)),
        ],
        out_shape=[jax.ShapeDtypeStruct((S, D_MODEL), BF16), jax.ShapeDtypeStruct((S, D_MODEL), F32),
                   jax.ShapeDtypeStruct((FOX_HEADS, S, 1), F32)],
        compiler_params=_cparams(("parallel", "arbitrary"), 56),
    )(P, P, P, ccol, crow)


def _attn_bwd(P, do, o32, lse, ccol, crow, *, tq, name):
    S = P.shape[0]
    tq = _tile(S, tq)
    nq = S // tq
    scale = FOX_HEAD_DIM ** -0.5

    def body(q_ref, k_ref, v_ref, do_ref, o_ref, lse_ref, cc_ref, cr_ref, dq_ref, dk_out, dv_out, dc_ref, dcq_ref, dk_ref, dv_ref):
        i = pl.program_id(1)

        @pl.when(i == 0)
        def _():
            dk_ref[...] = jnp.zeros_like(dk_ref)
            dv_ref[...] = jnp.zeros_like(dv_ref)
            dc_ref[...] = jnp.zeros_like(dc_ref)

        q = q_ref[...] * jnp.asarray(scale, BF16)
        do_v = do_ref[...]
        row = lax.broadcasted_iota(jnp.int32, (tq, tq), 0)
        col = lax.broadcasted_iota(jnp.int32, (tq, tq), 1)
        causal = col <= row
        hms = [_head_mask(hh, (tq, LANES)) for hh in range(2)]
        qhs = [jnp.where(hm, q, jnp.zeros_like(q)) for hm in hms]
        dohs = [jnp.where(hm, do_v, jnp.zeros_like(do_v)) for hm in hms]
        cbs = [cc_ref[hh, 0:1, :] for hh in range(2)]
        lses = [lse_ref[hh] for hh in range(2)]
        prod = do_v.astype(F32) * o_ref[...]
        dls = [jnp.sum(jnp.where(hm, prod, 0.0), axis=1, keepdims=True) for hm in hms]

        def step(j, carry, masked):
            dq_acc, row_sums = carry[0], list(carry[1:])
            start = pl.multiple_of(j * tq, tq)
            kj = k_ref[pl.ds(start, tq), :]
            vj = v_ref[pl.ds(start, tq), :]
            dv_part = jnp.zeros((tq, LANES), F32)
            dk_part = jnp.zeros((tq, LANES), F32)
            for hh in range(2):
                bias = cbs[hh] - cr_ref[0, hh:hh + 1, pl.ds(start, tq)]
                s = lax.dot_general(qhs[hh], kj, NT_DIMS, preferred_element_type=F32) + bias
                if masked:
                    s = jnp.where(causal, s, NEG_BIG)
                p = jnp.exp(s - lses[hh])
                dp = lax.dot_general(dohs[hh], vj, NT_DIMS, preferred_element_type=F32)
                ds = p * (dp - dls[hh])
                pb = p.astype(BF16)
                dsb = ds.astype(BF16)
                dv_part = dv_part + lax.dot_general(pb, dohs[hh], TN_DIMS, preferred_element_type=F32)
                dk_part = dk_part + lax.dot_general(dsb, qhs[hh], TN_DIMS, preferred_element_type=F32)
                dc_ref[0, hh:hh + 1, pl.ds(start, tq)] -= jnp.sum(ds, axis=0, keepdims=True)
                row_sums[hh] = row_sums[hh] + jnp.sum(ds, axis=1, keepdims=True)
                kh = jnp.where(hms[hh], kj, jnp.zeros_like(kj))
                dq_acc = dq_acc + jnp.dot(dsb, kh, preferred_element_type=F32)
            dv_ref[pl.ds(start, tq), :] += dv_part
            dk_ref[pl.ds(start, tq), :] += dk_part
            return (dq_acc, *row_sums)

        zero_col = jnp.zeros((tq, 1), F32)
        carry = lax.fori_loop(0, i, functools.partial(step, masked=False), (jnp.zeros((tq, LANES), F32), zero_col, zero_col))
        carry = step(i, carry, True)
        dq_ref[...] = (carry[0] * scale).astype(dq_ref.dtype)
        for hh in range(2):
            dcq_ref[hh] = carry[1 + hh]

        @pl.when(i == nq - 1)
        def _():
            dk_out[...] = dk_ref[...].astype(dk_out.dtype)
            dv_out[...] = dv_ref[...].astype(dv_out.dtype)

    nblk = D_MODEL // LANES
    qblk = pl.BlockSpec((tq, LANES), lambda p, i: (i, p))
    stat = pl.BlockSpec((2, tq, 1), lambda p, i: (p, i, 0))
    full = pl.BlockSpec((S, LANES), lambda p, i: (0, p))
    return pl.pallas_call(
        body, name=name, grid=(HEAD_PAIRS, nq),
        in_specs=[
            pl.BlockSpec((tq, LANES), lambda p, i: (i, PB_Q * nblk + p)),
            pl.BlockSpec((S, LANES), lambda p, i: (0, PB_K * nblk + p)),
            pl.BlockSpec((S, LANES), lambda p, i: (0, PB_V * nblk + p)),
            qblk, qblk, stat, stat,
            pl.BlockSpec((1, 2, S), lambda p, i: (p, 0, 0)),
        ],
        out_specs=[qblk, full, full, pl.BlockSpec((1, 2, S), lambda p, i: (p, 0, 0)), stat],
        out_shape=[
            jax.ShapeDtypeStruct((S, D_MODEL), BF16),
            jax.ShapeDtypeStruct((S, D_MODEL), BF16),
            jax.ShapeDtypeStruct((S, D_MODEL), BF16),
            jax.ShapeDtypeStruct((HEAD_PAIRS, 2, S), F32),
            jax.ShapeDtypeStruct((FOX_HEADS, S, 1), F32),
        ],
        scratch_shapes=[pltpu.VMEM((S, LANES), F32), pltpu.VMEM((S, LANES), F32)],
        compiler_params=_cparams(("parallel", "arbitrary"), 56),
    )(P, P, P, do, o32, lse, ccol, crow)


def _cumsum_lanes(xs, *, reverse, name):
    R, S = xs[0].shape
    nb = S // LANES
    n_in = len(xs)

    def body(*refs):
        x_refs, o_ref = refs[:n_in], refs[n_in]
        r = lax.broadcasted_iota(jnp.int32, (LANES, LANES), 0)
        c = lax.broadcasted_iota(jnp.int32, (LANES, LANES), 1)
        tri = ((r >= c) if reverse else (r <= c)).astype(F32)

        def step(b, carry):
            blk = (nb - 1 - b) if reverse else b
            start = pl.multiple_of(blk * LANES, LANES)
            xb = x_refs[0][:, pl.ds(start, LANES)]
            for r in x_refs[1:]:
                xb = xb + r[:, pl.ds(start, LANES)]
            y = jnp.dot(xb, tri, precision=lax.Precision.HIGHEST, preferred_element_type=F32) + carry
            o_ref[:, pl.ds(start, LANES)] = y
            return carry + jnp.sum(xb, axis=1, keepdims=True)

        lax.fori_loop(0, nb, step, jnp.zeros((R, 1), F32))

    return pl.pallas_call(
        body, name=name,
        in_specs=[pl.BlockSpec(memory_space=pltpu.VMEM)] * n_in,
        out_specs=pl.BlockSpec(memory_space=pltpu.VMEM),
        out_shape=jax.ShapeDtypeStruct((R, S), F32),
    )(*xs)


def _forget_fwd(f_logit, b_pad, *, name, tm=1024):
    S = f_logit.shape[0]
    tm = _tile(S, tm)

    def body(f_ref, b_ref, o_ref):
        z = f_ref[...] + b_ref[...]
        o_ref[...] = jnp.minimum(z, 0.0) - jnp.log(1.0 + jnp.exp(-jnp.abs(z)))

    blk = pl.BlockSpec((tm, LANES), lambda i: (i, 0))
    return pl.pallas_call(
        body, name=name, grid=(S // tm,),
        in_specs=[blk, pl.BlockSpec((1, LANES), lambda i: (0, 0))],
        out_specs=blk, out_shape=jax.ShapeDtypeStruct((S, LANES), F32),
        compiler_params=_cparams(("parallel",)),
    )(f_logit, b_pad)


def _forget_bwd(f_logit, b_pad, dlogf, *, name, tm=1024):
    S = f_logit.shape[0]
    tm = _tile(S, tm)

    def body(f_ref, b_ref, d_ref, o_ref, db_ref):
        i = pl.program_id(0)
        z = f_ref[...] + b_ref[...]
        dz = d_ref[...] * (1.0 - _sigmoid(z))
        o_ref[...] = dz.astype(o_ref.dtype)
        part = jnp.sum(dz, axis=0, keepdims=True)

        @pl.when(i == 0)
        def _():
            db_ref[...] = part

        @pl.when(i > 0)
        def _():
            db_ref[...] += part

    blk = pl.BlockSpec((tm, LANES), lambda i: (i, 0))
    vec = pl.BlockSpec((1, LANES), lambda i: (0, 0))
    return pl.pallas_call(
        body, name=name, grid=(S // tm,),
        in_specs=[blk, vec, blk], out_specs=[blk, vec],
        out_shape=[jax.ShapeDtypeStruct((S, LANES), BF16), jax.ShapeDtypeStruct((1, LANES), F32)],
        compiler_params=_cparams(("arbitrary",)),
    )(f_logit, b_pad, dlogf)


def _layernorm_stats(y):
    mu = jnp.mean(y, axis=-1, keepdims=True)
    yc = y - mu
    rstd = lax.rsqrt(jnp.mean(yc * yc, axis=-1, keepdims=True) + LN_EPS)
    return yc * rstd, rstd


def _conv_fwd(P, w_pad, conv_b, ln_g, ln_b, *, name, tm=256):
    S = P.shape[0]
    C = D_MODEL
    tm = _tile(S, tm)
    hb = tm // CONV_HALO

    def body(a_ref, gt_ref, ah_ref, gh_ref, w_ref, cb_ref, g_ref, b_ref, o_ref, y_ref, glu_ref):
        i = pl.program_id(0)
        halo = ah_ref[...].astype(F32) * _sigmoid(gh_ref[...].astype(F32))
        glu_ref[0:CONV_HALO, :] = jnp.where(i > 0, halo, 0.0)
        glu_ref[CONV_HALO:, :] = a_ref[...].astype(F32) * _sigmoid(gt_ref[...].astype(F32))
        acc = jnp.zeros((tm, C), F32)
        for k in range(CONV_WIDTH):
            acc = acc + w_ref[k:k + 1, :] * glu_ref[pl.ds(CONV_HALO - (CONV_WIDTH - 1) + k, tm), :]
        y = acc + cb_ref[...]
        y_ref[...] = y
        xhat, _ = _layernorm_stats(y)
        z = xhat * g_ref[...] + b_ref[...]
        o_ref[...] = (z * _sigmoid(z)).astype(o_ref.dtype)

    vec = pl.BlockSpec((1, C), lambda i: (0, 0))
    row = pl.BlockSpec((tm, C), lambda i: (i, 0))
    return pl.pallas_call(
        body, name=name, grid=(S // tm,),
        in_specs=[
            pl.BlockSpec((tm, C), lambda i: (i, PB_A)),
            pl.BlockSpec((tm, C), lambda i: (i, PB_GATE)),
            pl.BlockSpec((CONV_HALO, C), lambda i: (jnp.maximum(i * hb - 1, 0), PB_A)),
            pl.BlockSpec((CONV_HALO, C), lambda i: (jnp.maximum(i * hb - 1, 0), PB_GATE)),
            pl.BlockSpec((CONV_HALO, C), lambda i: (0, 0)),
            vec, vec, vec,
        ],
        out_specs=[row, row],
        out_shape=[jax.ShapeDtypeStruct((S, C), BF16), jax.ShapeDtypeStruct((S, C), F32)],
        scratch_shapes=[pltpu.VMEM((tm + CONV_HALO, C), F32)],
        compiler_params=_cparams(("parallel",), 56),
    )(P, P, P, P, w_pad, conv_b, ln_g, ln_b)


def _conv_bwd(P, do, y, w_pad, ln_g, ln_b, *, name, tm=256):
    S = P.shape[0]
    C = D_MODEL
    tm = _tile(S, tm)
    hb = tm // CONV_HALO
    n_tiles = S // tm
    last_halo = S // CONV_HALO - 1

    def body(a_ref, gt_ref, ah_ref, gh_ref, do_ref, y_ref, don_ref, yn_ref, w_ref, g_ref, b_ref,
             dglu_ref, dw_ref, small_ref, glu_ref, dy_ref):
        i = pl.program_id(0)

        @pl.when(i == 0)
        def _():
            dw_ref[...] = jnp.zeros_like(dw_ref)
            small_ref[...] = jnp.zeros_like(small_ref)

        def ln_bwd(do_v, y_v):
            xhat, rstd = _layernorm_stats(y_v)
            z = xhat * g_ref[...] + b_ref[...]
            sg = _sigmoid(z)
            dz = do_v * (sg * (1.0 + z * (1.0 - sg)))
            dxh = dz * g_ref[...]
            dy = rstd * (dxh - jnp.mean(dxh, axis=-1, keepdims=True) - xhat * jnp.mean(dxh * xhat, axis=-1, keepdims=True))
            return dy, dz, xhat

        dy, dz, xhat = ln_bwd(do_ref[...], y_ref[...])
        dy_next, _, _ = ln_bwd(don_ref[...], yn_ref[...])
        small_ref[0:1, :] += jnp.sum(dz * xhat, axis=0, keepdims=True)
        small_ref[1:2, :] += jnp.sum(dz, axis=0, keepdims=True)
        small_ref[2:3, :] += jnp.sum(dy, axis=0, keepdims=True)
        dy_ref[0:tm, :] = dy
        dy_ref[tm:, :] = jnp.where(i < n_tiles - 1, dy_next, 0.0)

        a = a_ref[...].astype(F32)
        sig = _sigmoid(gt_ref[...].astype(F32))
        halo = ah_ref[...].astype(F32) * _sigmoid(gh_ref[...].astype(F32))
        glu_ref[0:CONV_HALO, :] = jnp.where(i > 0, halo, 0.0)
        glu_ref[CONV_HALO:, :] = a * sig

        dg = jnp.zeros((tm, C), F32)
        for k in range(CONV_WIDTH):
            shifted = glu_ref[pl.ds(CONV_HALO - (CONV_WIDTH - 1) + k, tm), :]
            dw_ref[k:k + 1, :] += jnp.sum(dy * shifted, axis=0, keepdims=True)
            dg = dg + w_ref[k:k + 1, :] * dy_ref[pl.ds(CONV_WIDTH - 1 - k, tm), :]
        dglu_ref[:, 0:C] = (dg * sig).astype(dglu_ref.dtype)
        dglu_ref[:, C:] = (dg * a * sig * (1.0 - sig)).astype(dglu_ref.dtype)

    vec = pl.BlockSpec((1, C), lambda i: (0, 0))
    row = pl.BlockSpec((tm, C), lambda i: (i, 0))
    nxt = pl.BlockSpec((CONV_HALO, C), lambda i: (jnp.minimum((i + 1) * hb, last_halo), 0))
    return pl.pallas_call(
        body, name=name, grid=(n_tiles,),
        in_specs=[
            pl.BlockSpec((tm, C), lambda i: (i, PB_A)),
            pl.BlockSpec((tm, C), lambda i: (i, PB_GATE)),
            pl.BlockSpec((CONV_HALO, C), lambda i: (jnp.maximum(i * hb - 1, 0), PB_A)),
            pl.BlockSpec((CONV_HALO, C), lambda i: (jnp.maximum(i * hb - 1, 0), PB_GATE)),
            row, row, nxt, nxt,
            pl.BlockSpec((CONV_HALO, C), lambda i: (0, 0)),
            vec, vec,
        ],
        out_specs=[
            pl.BlockSpec((tm, 2 * C), lambda i: (i, 0)),
            pl.BlockSpec((CONV_HALO, C), lambda i: (0, 0)),
            pl.BlockSpec((8, C), lambda i: (0, 0)),
        ],
        out_shape=[
            jax.ShapeDtypeStruct((S, 2 * C), BF16),
            jax.ShapeDtypeStruct((CONV_HALO, C), F32),
            jax.ShapeDtypeStruct((8, C), F32),
        ],
        scratch_shapes=[pltpu.VMEM((tm + CONV_HALO, C), F32), pltpu.VMEM((tm + CONV_HALO, C), F32)],
        compiler_params=_cparams(("arbitrary",), 56),
    )(P, P, P, P, do, y, do, y, w_pad, ln_g, ln_b)


def _mem_softmax(qh, kh):
    s = lax.dot_general(qh, kh, NT_DIMS, preferred_element_type=F32)
    e = jnp.exp(s - jnp.max(s, axis=1, keepdims=True))
    return e / jnp.sum(e, axis=1, keepdims=True)


def _mem_fwd(P, kv, *, name, tm=512):
    S, M = P.shape[0], kv.shape[0]
    tm = _tile(S, tm)
    scale = MEM_HEAD_DIM ** -0.5

    def body(q_ref, k_ref, v_ref, o_ref):
        for h in range(MEM_HEADS):
            sl = slice(h * MEM_HEAD_DIM, (h + 1) * MEM_HEAD_DIM)
            qh = q_ref[:, sl] * jnp.asarray(scale, BF16)
            p = _mem_softmax(qh, k_ref[:, sl])
            o_ref[:, sl] = jnp.dot(p.astype(BF16), v_ref[:, sl], preferred_element_type=F32).astype(o_ref.dtype)

    return pl.pallas_call(
        body, name=name, grid=(S // tm,),
        in_specs=[
            pl.BlockSpec((tm, D_MODEL), lambda i: (i, PB_QMEM)),
            pl.BlockSpec((M, D_MODEL), lambda i: (0, 0)),
            pl.BlockSpec((M, D_MODEL), lambda i: (0, 1)),
        ],
        out_specs=pl.BlockSpec((tm, D_MODEL), lambda i: (i, 0)),
        out_shape=jax.ShapeDtypeStruct((S, D_MODEL), BF16),
        compiler_params=_cparams(("parallel",)),
    )(P, kv, kv)


def _mem_bwd(P, do, kv, *, name, tm=512):
    S, M = P.shape[0], kv.shape[0]
    tm = _tile(S, tm)
    scale = MEM_HEAD_DIM ** -0.5

    def body(q_ref, k_ref, v_ref, do_ref, dq_ref, dkv_ref):
        i = pl.program_id(0)

        @pl.when(i == 0)
        def _():
            dkv_ref[...] = jnp.zeros_like(dkv_ref)

        for h in range(MEM_HEADS):
            sl = slice(h * MEM_HEAD_DIM, (h + 1) * MEM_HEAD_DIM)
            slv = slice(D_MODEL + h * MEM_HEAD_DIM, D_MODEL + (h + 1) * MEM_HEAD_DIM)
            qh = q_ref[:, sl] * jnp.asarray(scale, BF16)
            kh, vh, doh = k_ref[:, sl], v_ref[:, sl], do_ref[:, sl]
            p = _mem_softmax(qh, kh)
            dp = lax.dot_general(doh, vh, NT_DIMS, preferred_element_type=F32)
            ds = p * (dp - jnp.sum(p * dp, axis=1, keepdims=True))
            dsb = ds.astype(BF16)
            dq_ref[:, sl] = (jnp.dot(dsb, kh, preferred_element_type=F32) * scale).astype(dq_ref.dtype)
            dkv_ref[:, sl] += lax.dot_general(dsb, qh, TN_DIMS, preferred_element_type=F32)
            dkv_ref[:, slv] += lax.dot_general(p.astype(BF16), doh, TN_DIMS, preferred_element_type=F32)

    row = pl.BlockSpec((tm, D_MODEL), lambda i: (i, 0))
    return pl.pallas_call(
        body, name=name, grid=(S // tm,),
        in_specs=[
            pl.BlockSpec((tm, D_MODEL), lambda i: (i, PB_QMEM)),
            pl.BlockSpec((M, D_MODEL), lambda i: (0, 0)),
            pl.BlockSpec((M, D_MODEL), lambda i: (0, 1)),
            row,
        ],
        out_specs=[row, pl.BlockSpec((M, 2 * D_MODEL), lambda i: (0, 0))],
        out_shape=[jax.ShapeDtypeStruct((S, D_MODEL), BF16), jax.ShapeDtypeStruct((M, 2 * D_MODEL), F32)],
        compiler_params=_cparams(("arbitrary",)),
    )(P, kv, kv, do)


def _merge_fwd(o_attn, o_conv, o_mem, P, wpa, wpc, wpm, w_out, *, name, tm=256):
    S = P.shape[0]
    D = D_MODEL
    tm = _tile(S, tm)

    def body(oa_ref, oc_ref, om_ref, g0_ref, g1_ref, g2_ref, wa_ref, wc_ref, wm_ref, wo_ref, mg_ref, y_ref, pb_ref):
        merged = jnp.zeros((tm, D), F32)
        for b, (o_ref, g_ref, w_ref) in enumerate(((oa_ref, g0_ref, wa_ref), (oc_ref, g1_ref, wc_ref), (om_ref, g2_ref, wm_ref))):
            pb = jnp.dot(o_ref[...], w_ref[...], preferred_element_type=F32)
            pb_ref[b] = pb.astype(pb_ref.dtype)
            merged = merged + _sigmoid(g_ref[...].astype(F32)) * pb
        mb = merged.astype(BF16)
        mg_ref[...] = mb
        y_ref[...] = jnp.dot(mb, wo_ref[...], preferred_element_type=F32)

    row = pl.BlockSpec((tm, D), lambda i: (i, 0))
    wsp = pl.BlockSpec((D, D), lambda i: (0, 0))
    return pl.pallas_call(
        body, name=name, grid=(S // tm,),
        in_specs=[row, row, row] + [pl.BlockSpec((tm, D), lambda i, b=b: (i, PB_G0 + b)) for b in range(3)] + [wsp] * 4,
        out_specs=[row, row, pl.BlockSpec((3, tm, D), lambda i: (0, i, 0))],
        out_shape=[jax.ShapeDtypeStruct((S, D), BF16), jax.ShapeDtypeStruct((S, D), F32), jax.ShapeDtypeStruct((3, S, D), BF16)],
        compiler_params=_cparams(("parallel",), 56),
    )(o_attn, o_conv, o_mem, P, P, P, wpa, wpc, wpm, w_out)


def _merge_bwd(dmerged, P, pb, *, name, tm=256):
    S = P.shape[0]
    D = D_MODEL
    tm = _tile(S, tm)

    def body(dm_ref, g0_ref, g1_ref, g2_ref, pb_ref, dpb_ref, dgl_ref):
        dm = dm_ref[...].astype(F32)
        for b, g_ref in enumerate((g0_ref, g1_ref, g2_ref)):
            g = _sigmoid(g_ref[...].astype(F32))
            dpb_ref[b] = (dm * g).astype(dpb_ref.dtype)
            dgl_ref[:, b * D:(b + 1) * D] = (dm * pb_ref[b].astype(F32) * g * (1.0 - g)).astype(dgl_ref.dtype)

    row = pl.BlockSpec((tm, D), lambda i: (i, 0))
    blk3 = pl.BlockSpec((3, tm, D), lambda i: (0, i, 0))
    return pl.pallas_call(
        body, name=name, grid=(S // tm,),
        in_specs=[row] + [pl.BlockSpec((tm, D), lambda i, b=b: (i, PB_G0 + b)) for b in range(3)] + [blk3],
        out_specs=[blk3, pl.BlockSpec((tm, 3 * D), lambda i: (i, 0))],
        out_shape=[jax.ShapeDtypeStruct((3, S, D), BF16), jax.ShapeDtypeStruct((S, 3 * D), BF16)],
        compiler_params=_cparams(("parallel",), 56),
    )(dmerged, P, P, P, pb)


def _resid_norm(x, y1, g_post, g_pre, *, name, tm=512):
    S, D = x.shape
    tm = _tile(S, tm)

    def body(x_ref, y_ref, gp_ref, gq_ref, x1_ref, h2_ref):
        yv = y_ref[...]
        x1 = x_ref[...] + yv * lax.rsqrt(jnp.mean(yv * yv, axis=-1, keepdims=True) + RMS_EPS) * gp_ref[...]
        x1_ref[...] = x1
        h2_ref[...] = (x1 * lax.rsqrt(jnp.mean(x1 * x1, axis=-1, keepdims=True) + RMS_EPS) * gq_ref[...]).astype(h2_ref.dtype)

    row = pl.BlockSpec((tm, D), lambda i: (i, 0))
    vec = pl.BlockSpec((1, D), lambda i: (0, 0))
    return pl.pallas_call(
        body, name=name, grid=(S // tm,),
        in_specs=[row, row, vec, vec], out_specs=[row, row],
        out_shape=[jax.ShapeDtypeStruct((S, D), F32), jax.ShapeDtypeStruct((S, D), BF16)],
        compiler_params=_cparams(("parallel",)),
    )(x, y1, g_post, g_pre)


def _swiglu_fwd(h2, w_gu, *, name, tm=512, tn=1408):
    S, D = h2.shape
    Fh = w_gu.shape[1] // 2
    tm, tn = _tile(S, tm), _tile(Fh, tn)
    nj = Fh // tn

    def body(h_ref, wg_ref, wu_ref, g_ref, u_ref, a_ref):
        hv = h_ref[...]
        g = jnp.dot(hv, wg_ref[...], preferred_element_type=F32)
        u = jnp.dot(hv, wu_ref[...], preferred_element_type=F32)
        g_ref[...] = g.astype(g_ref.dtype)
        u_ref[...] = u.astype(u_ref.dtype)
        a_ref[...] = (g * _sigmoid(g) * u).astype(a_ref.dtype)

    out = pl.BlockSpec((tm, tn), lambda j, i: (i, j))
    sds = jax.ShapeDtypeStruct((S, Fh), BF16)
    return pl.pallas_call(
        body, name=name, grid=(nj, S // tm),
        in_specs=[
            pl.BlockSpec((tm, D), lambda j, i: (i, 0)),
            pl.BlockSpec((D, tn), lambda j, i: (0, j)),
            pl.BlockSpec((D, tn), lambda j, i: (0, j + nj)),
        ],
        out_specs=[out, out, out], out_shape=[sds, sds, sds],
        compiler_params=_cparams(("parallel", "parallel"), 56),
    )(h2, w_gu, w_gu)


def _swiglu_bwd(dffn, w_down, g, u, *, name, tm=512, tn=1408):
    S, Fh = g.shape
    D = dffn.shape[1]
    tm, tn = _tile(S, tm), _tile(Fh, tn)

    def body(df_ref, w_ref, g_ref, u_ref, o_ref):
        dfv = df_ref[...]
        for j in range(Fh // tn):
            cols = slice(j * tn, (j + 1) * tn)
            da = lax.dot_general(dfv, w_ref[cols, :], NT_DIMS, preferred_element_type=F32)
            gv = g_ref[:, cols].astype(F32)
            sg = _sigmoid(gv)
            o_ref[:, cols] = (da * u_ref[:, cols].astype(F32) * (sg * (1.0 + gv * (1.0 - sg)))).astype(o_ref.dtype)
            o_ref[:, Fh + j * tn:Fh + (j + 1) * tn] = (da * gv * sg).astype(o_ref.dtype)

    act = pl.BlockSpec((tm, Fh), lambda i: (i, 0))
    return pl.pallas_call(
        body, name=name, grid=(S // tm,),
        in_specs=[pl.BlockSpec((tm, D), lambda i: (i, 0)), pl.BlockSpec((Fh, D), lambda i: (0, 0)), act, act],
        out_specs=pl.BlockSpec((tm, 2 * Fh), lambda i: (i, 0)),
        out_shape=jax.ShapeDtypeStruct((S, 2 * Fh), BF16),
        compiler_params=_cparams(("parallel",), 56),
    )(dffn, w_down, g, u)


def _final(x1, ffn, target, g, *, name, tm=512):
    S, D = x1.shape
    tm = _tile(S, tm)

    def body(x_ref, f_ref, t_ref, g_ref, dout_ref, dffn_ref, loss_ref, dg_ref):
        i = pl.program_id(0)
        fv = f_ref[...]
        rstd = lax.rsqrt(jnp.mean(fv * fv, axis=-1, keepdims=True) + RMS_EPS)
        r = fv * rstd
        e = x_ref[...] + r * g_ref[...] - t_ref[...]
        dout = e * (1.0 / D)
        dout_ref[...] = dout
        gy = dout * g_ref[...]
        dffn_ref[...] = (rstd * (gy - r * jnp.mean(gy * r, axis=-1, keepdims=True))).astype(dffn_ref.dtype)
        lpart = jnp.full((8, LANES), 0.5 * jnp.sum(jnp.mean(e * e, axis=-1, keepdims=True)), F32)
        gpart = jnp.sum(dout * r, axis=0, keepdims=True)

        @pl.when(i == 0)
        def _():
            loss_ref[...] = lpart
            dg_ref[...] = gpart

        @pl.when(i > 0)
        def _():
            loss_ref[...] += lpart
            dg_ref[...] += gpart

    row = pl.BlockSpec((tm, D), lambda i: (i, 0))
    vec = pl.BlockSpec((1, D), lambda i: (0, 0))
    return pl.pallas_call(
        body, name=name, grid=(S // tm,),
        in_specs=[row, row, row, vec],
        out_specs=[row, row, pl.BlockSpec((8, LANES), lambda i: (0, 0)), vec],
        out_shape=[jax.ShapeDtypeStruct((S, D), F32), jax.ShapeDtypeStruct((S, D), BF16),
                   jax.ShapeDtypeStruct((8, LANES), F32), jax.ShapeDtypeStruct((1, D), F32)],
        compiler_params=_cparams(("arbitrary",)),
    )(x1, ffn, target, g)


def _after(token, value):
    return value if token is None else value + token[0, 0].astype(value.dtype)


def _local_step(x, mem, target, gains, w, late_weights, send_grads, *, tq=512):
    S = x.shape[0]
    b_pad = jnp.pad(gains["b_forget"], ((0, 0), (0, LANES - FOX_HEADS)))
    gains = dict(gains)

    h = _rms_fwd(x, gains["norm_mix_pre"], name="rms_mix_pre")
    P = _matmul(h, w["wp"], mode="nn", out_dtype=BF16, name="proj_in")
    f_logit = _matmul(h, w["wf"], mode="nn", out_dtype=F32, name="proj_forget")
    logf = _forget_fwd(f_logit, b_pad, name="forget_fwd")
    c_row16 = _cumsum_lanes([logf[:, :FOX_HEADS].T], reverse=False, name="forget_cumsum")
    ccol = c_row16[:, :, None]
    crow = c_row16.reshape(HEAD_PAIRS, 2, S)
    o_attn, o_attn32, lse = _attn_fwd(P, ccol, crow, tq=tq, name="attn_fwd")
    w = dict(w, **late_weights(o_attn))
    o_conv, y_conv = _conv_fwd(P, w["conv_w"], gains["conv_b"], gains["conv_ln_g"], gains["conv_ln_b"], name="conv_fwd")
    mem_n = _rms_fwd(mem, gains["norm_mem"], name="rms_mem")
    kv = _matmul(mem_n, w["w_kv"], mode="nn", out_dtype=BF16, name="mem_kv")
    o_mem = _mem_fwd(P, kv, name="mem_fwd")
    merged, y1, pb = _merge_fwd(o_attn, o_conv, o_mem, P, w["wpa"], w["wpc"], w["wpm"], w["w_out"], name="merge_fwd")
    x1, h2 = _resid_norm(x, y1, gains["norm_mix_post"], gains["norm_ffn_pre"], name="resid_norm")
    g_ffn, u_ffn, act = _swiglu_fwd(h2, w["w_gu"], name="swiglu_fwd")
    ffn = _matmul(act, w["w_down"], mode="nn", out_dtype=F32, name="ffn_down", tk=1408)

    dout, dffn, loss_tile, d_norm_ffn_post = _final(x1, ffn, target, gains["norm_ffn_post"], name="loss_head")
    d_w_down = _matmul(act, dffn, mode="tn", out_dtype=BF16, name="dw_down", tm=1408)
    dgu = _swiglu_bwd(dffn, w["w_down"], g_ffn, u_ffn, name="swiglu_bwd")
    dh2 = _matmul(dgu, w["w_gu"], mode="nt", out_dtype=F32, name="d_h2", tk=1408)
    d_w_gu = _matmul(h2, dgu, mode="tn", out_dtype=BF16, name="dw_gate_up", tn=1408)
    dx1, d_norm_ffn_pre = _rms_bwd(x1, [dh2], gains["norm_ffn_pre"], dout, out_dtype=F32, name="rms_ffn_pre_bwd")

    dy1, d_norm_mix_post = _rms_bwd(y1, [dx1], gains["norm_mix_post"], None, out_dtype=BF16, name="rms_mix_post_bwd")
    dmerged = _matmul(dy1, w["w_out"], mode="nt", out_dtype=F32, name="d_merged")
    d_w_out = _matmul(merged, dy1, mode="tn", out_dtype=BF16, name="dw_out")
    dpb, dgl = _merge_bwd(dmerged, P, pb, name="merge_bwd")
    do_attn = _matmul(dpb[0], w["wpa"], mode="nt", out_dtype=BF16, name="d_o_attn")
    do_conv = _matmul(dpb[1], w["wpc"], mode="nt", out_dtype=F32, name="d_o_conv")
    do_mem = _matmul(dpb[2], w["wpm"], mode="nt", out_dtype=BF16, name="d_o_mem")
    d_wpa = _matmul(o_attn, dpb[0], mode="tn", out_dtype=BF16, name="dw_proj_attn")
    d_wpc = _matmul(o_conv, dpb[1], mode="tn", out_dtype=BF16, name="dw_proj_conv")
    d_wpm = _matmul(o_mem, dpb[2], mode="tn", out_dtype=BF16, name="dw_proj_mem")

    dq_mem, dkv = _mem_bwd(P, do_mem, kv, name="mem_bwd")
    dkv_b = dkv.astype(BF16)
    d_w_kv = _matmul(mem_n, dkv_b, mode="tn", out_dtype=BF16, name="dw_kv")
    dmem_n = _matmul(dkv_b, w["w_kv"], mode="nt", out_dtype=F32, name="d_mem_n")
    _, d_norm_mem = _rms_bwd(mem, [dmem_n], gains["norm_mem"], None, out_dtype=BF16, name="rms_mem_bwd")

    dglu, d_conv_w, conv_small = _conv_bwd(P, do_conv, y_conv, w["conv_w"], gains["conv_ln_g"], gains["conv_ln_b"], name="conv_bwd")
    sent = send_grads("body", dict(conv_w=d_conv_w[:CONV_WIDTH].astype(BF16), w_kv_mem=d_w_kv, w_proj_attn=d_wpa,
                                   w_proj_conv=d_wpc, w_proj_mem=d_wpm, w_out=d_w_out, w_gate_up=d_w_gu, w_down=d_w_down))

    dq, dk, dv, dc, dcq = _attn_bwd(P, do_attn, o_attn32, lse, ccol, _after(sent, crow), tq=tq, name="attn_bwd")
    dlogf16 = _cumsum_lanes([dc.reshape(FOX_HEADS, S), dcq[:, :, 0]], reverse=True, name="forget_cumsum_bwd")
    dlogf = jnp.pad(dlogf16.T, ((0, 0), (0, LANES - FOX_HEADS)))
    df, d_b_pad = _forget_bwd(f_logit, b_pad, dlogf, name="forget_bwd")

    dP = jnp.concatenate([dq, dk, dv, dglu, dq_mem, dgl], axis=1)
    d_wp = _matmul(h, dP, mode="tn", out_dtype=BF16, name="dw_in")
    d_wf = _matmul(h, df, mode="tn", out_dtype=BF16, name="dw_forget")
    n_qkv = 3 * D_MODEL
    sent = send_grads("w_in", dict(w_in=jnp.concatenate([d_wp[:, :n_qkv], d_wf[:, :FOX_HEADS], d_wp[:, n_qkv:]], axis=1)))
    dh_p = _matmul(dP, _after(sent, w["wp"]), mode="nt", out_dtype=F32, name="d_h")
    dh_f = _matmul(df, w["wf"], mode="nt", out_dtype=F32, name="d_h_forget")
    grad_x, d_norm_mix_pre = _rms_bwd(x, [dh_p, dh_f], gains["norm_mix_pre"], dx1, out_dtype=F32, name="rms_mix_pre_bwd")

    small = dict(norm_mix_pre=d_norm_mix_pre, norm_mix_post=d_norm_mix_post, norm_mem=d_norm_mem,
                 conv_b=conv_small[2:3], conv_ln_g=conv_small[0:1], conv_ln_b=conv_small[1:2],
                 norm_ffn_pre=d_norm_ffn_pre, norm_ffn_post=d_norm_ffn_post, b_forget=d_b_pad[:, :FOX_HEADS])
    return loss_tile, grad_x, small


def _mesh_pos():
    return lax.axis_index("x"), lax.axis_index("y"), lax.axis_index("c")


def _flip(pos, d):
    x, y, c = pos
    return (1 - x if d & 4 else x, 1 - y if d & 2 else y, 1 - c if d & 1 else c)


def _flat(pos):
    x, y, c = pos
    return 4 * x + 2 * y + c


def _exchange(srcs, *, scatter, name):
    T = len(srcs)
    n_peer = N_DEV - 1

    def body(*refs):
        src_refs, out_refs = refs[:T], refs[T:2 * T]
        token, send_sems, recv_sems, local_sems = refs[2 * T:]
        me = _mesh_pos()
        my = _flat(me)

        def copy(t, d):
            peer = _flip(me, d)
            return pltpu.make_async_remote_copy(
                src_ref=src_refs[t].at[_flat(peer)] if scatter else src_refs[t],
                dst_ref=out_refs[t].at[my],
                send_sem=send_sems.at[t * n_peer + d - 1], recv_sem=recv_sems.at[t * n_peer + d - 1],
                device_id=peer, device_id_type=pl.DeviceIdType.MESH)

        def arrival(t, d):
            peer = _flip(me, d)
            return pltpu.make_async_remote_copy(
                src_ref=src_refs[t].at[my] if scatter else src_refs[t],
                dst_ref=out_refs[t].at[_flat(peer)],
                send_sem=send_sems.at[t * n_peer + d - 1], recv_sem=recv_sems.at[t * n_peer + d - 1],
                device_id=peer, device_id_type=pl.DeviceIdType.MESH)

        local = [pltpu.make_async_copy(src_refs[t].at[my] if scatter else src_refs[t], out_refs[t].at[my], local_sems.at[t])
                 for t in range(T)]
        for cp in local:
            cp.start()
        sends = [copy(t, d) for d in range(1, N_DEV) for t in range(T)]
        for cp in sends:
            cp.start()
        for d in range(1, N_DEV):
            for t in range(T):
                arrival(t, d).wait_recv()
        for cp in sends:
            cp.wait_send()
        for cp in local:
            cp.wait()
        token[...] = jnp.zeros_like(token)

    outs = pl.pallas_call(
        body, name=name,
        in_specs=[pl.BlockSpec(memory_space=pl.ANY)] * T,
        out_specs=[pl.BlockSpec(memory_space=pl.ANY)] * T + [pl.BlockSpec(memory_space=pltpu.VMEM)],
        out_shape=[jax.ShapeDtypeStruct((N_DEV,) + tuple(s.shape[-2:]), s.dtype) for s in srcs] + [jax.ShapeDtypeStruct((8, LANES), F32)],
        scratch_shapes=[pltpu.SemaphoreType.DMA((T * n_peer,)), pltpu.SemaphoreType.DMA((T * n_peer,)), pltpu.SemaphoreType.DMA((T,))],
    )(*srcs)
    return list(outs[:T]), outs[T]


HBM_SPEC = pl.BlockSpec(memory_space=pltpu.HBM)
SEM_SPEC = pl.BlockSpec(memory_space=pltpu.SEMAPHORE)


def _split_copies(src_refs, land_refs, send_sems, recv_sems, scatter):
    me = _mesh_pos()
    my = _flat(me)
    n_peer = N_DEV - 1
    out = []
    for d in range(1, N_DEV):
        peer = _flip(me, d)
        for t, (src, land) in enumerate(zip(src_refs, land_refs)):
            k = t * n_peer + d - 1
            start = pltpu.make_async_remote_copy(
                src_ref=src.at[_flat(peer)] if scatter else src, dst_ref=land.at[my],
                send_sem=send_sems.at[k], recv_sem=recv_sems.at[k], device_id=peer, device_id_type=pl.DeviceIdType.MESH)
            arrive = pltpu.make_async_remote_copy(
                src_ref=src.at[my] if scatter else src, dst_ref=land.at[_flat(peer)],
                send_sem=send_sems.at[k], recv_sem=recv_sems.at[k], device_id=peer, device_id_type=pl.DeviceIdType.MESH)
            out.append((start, arrive))
    return out


def _exchange_start(srcs, *, scatter, name):
    T = len(srcs)
    n_sem = T * (N_DEV - 1)
    lands = [lax.empty((N_DEV,) + tuple(s.shape[-2:]), s.dtype) for s in srcs]

    def body(*refs):
        src_refs, land_refs = refs[:T], refs[T:2 * T]
        send_sems, recv_sems = refs[2 * T], refs[2 * T + 1]
        token = refs[-1]
        my = _flat(_mesh_pos())
        for src, land in zip(src_refs, land_refs):
            pltpu.sync_copy(src.at[my] if scatter else src, land.at[my])
        for start, _ in _split_copies(src_refs, land_refs, send_sems, recv_sems, scatter):
            start.start()
        token[...] = jnp.zeros_like(token)

    hbm = lambda a: pltpu.HBM(a.shape, a.dtype)
    outs = pl.pallas_call(
        body, name=name,
        in_specs=[HBM_SPEC] * (2 * T),
        out_specs=(SEM_SPEC, SEM_SPEC, *[HBM_SPEC] * (2 * T), pl.BlockSpec(memory_space=pltpu.VMEM)),
        out_shape=(pltpu.SemaphoreType.DMA((n_sem,)), pltpu.SemaphoreType.DMA((n_sem,)),
                   *[hbm(s) for s in srcs], *[hbm(a) for a in lands], jax.ShapeDtypeStruct((8, LANES), F32)),
        input_output_aliases={t: 2 + t for t in range(2 * T)},
        compiler_params=pltpu.CompilerParams(has_side_effects=pltpu.SideEffectType.DATAFLOW_SIDE_EFFECTING),
    )(*[pltpu.with_memory_space_constraint(s, pltpu.HBM) for s in srcs],
      *[pltpu.with_memory_space_constraint(a, pltpu.HBM) for a in lands])
    return (outs[0], outs[1], list(outs[2:2 + T]), list(outs[2 + T:2 + 2 * T])), outs[-1]


def _exchange_wait(handle, after, *, scatter, name):
    send_sems, recv_sems, srcs, lands = handle
    T = len(srcs)

    def body(*refs):
        src_refs, land_refs = refs[:T], refs[T:2 * T]
        s_sems, r_sems = refs[2 * T], refs[2 * T + 1]
        for start, arrive in _split_copies(src_refs, land_refs, s_sems, r_sems, scatter):
            start.wait_send()
            arrive.wait_recv()

    hbm = lambda a: pltpu.HBM(a.shape, a.dtype)
    outs = pl.pallas_call(
        body, name=name,
        in_specs=[HBM_SPEC] * (2 * T) + [SEM_SPEC, SEM_SPEC, pl.BlockSpec(memory_space=pl.ANY)],
        out_specs=[HBM_SPEC] * (2 * T),
        out_shape=[hbm(s) for s in srcs] + [hbm(a) for a in lands],
        input_output_aliases={t: t for t in range(2 * T)},
        compiler_params=pltpu.CompilerParams(has_side_effects=pltpu.SideEffectType.DATAFLOW_SIDE_EFFECTING),
    )(*srcs, *lands, send_sems, recv_sems, after)
    return list(outs[T:])


def _adamw(w, slots, m, v, *, name):
    R, C = w.shape
    tr = R if R <= 512 else 256
    assert R % tr == 0, (R, tr)

    def body(w_ref, s_ref, m_ref, v_ref, g_ref, d_ref, m2_ref, v2_ref):
        gv = s_ref[0].astype(F32)
        for j in range(1, N_DEV):
            gv = gv + s_ref[j].astype(F32)
        g_ref[...] = gv
        m2 = ADAM_B1 * m_ref[...] + (1.0 - ADAM_B1) * gv
        v2 = ADAM_B2 * v_ref[...] + (1.0 - ADAM_B2) * (gv * gv)
        m_hat = m2 / (1.0 - ADAM_B1 ** ADAM_STEP)
        v_hat = v2 / (1.0 - ADAM_B2 ** ADAM_STEP)
        d_ref[...] = -ADAM_LR * (m_hat / (jnp.sqrt(v_hat) + ADAM_EPS) + ADAM_WD * w_ref[...])
        m2_ref[...] = m2
        v2_ref[...] = v2

    blk = pl.BlockSpec((tr, C), lambda i: (i, 0))
    sds = jax.ShapeDtypeStruct((R, C), F32)
    return pl.pallas_call(
        body, name=name, grid=(R // tr,),
        in_specs=[blk, pl.BlockSpec((N_DEV, tr, C), lambda i: (0, i, 0)), blk, blk],
        out_specs=[blk] * 4, out_shape=[sds] * 4,
        compiler_params=_cparams(("parallel",)),
    )(w, slots, m, v)


SHARDED = (
    ("w_in", (1024, 1154), "col"), ("conv_w", (31, 128), "col"), ("w_kv_mem", (1024, 256), "col"),
    ("w_proj_attn", (128, 1024), "row"), ("w_proj_conv", (128, 1024), "row"), ("w_proj_mem", (128, 1024), "row"),
    ("w_out", (128, 1024), "row"), ("w_gate_up", (1024, 704), "col"), ("w_down", (352, 1024), "row"),
)
SMALL = ("norm_mix_pre", "norm_mix_post", "norm_mem", "conv_b", "conv_ln_g", "conv_ln_b", "norm_ffn_pre", "norm_ffn_post", "b_forget")
SMALL_ROWS = 16
LOSS_ROW = len(SMALL)
WEIGHT_ORDER = ("norm_mix_pre", "norm_mix_post", "norm_mem", "w_in", "b_forget", "conv_w", "conv_b", "conv_ln_g", "conv_ln_b",
                "w_kv_mem", "w_proj_attn", "w_proj_conv", "w_proj_mem", "w_out", "norm_ffn_pre", "norm_ffn_post", "w_gate_up", "w_down")


def _to_full(blocks8, kind):
    n, r, c = blocks8.shape
    if kind == "col":
        return jnp.concatenate([blocks8[j] for j in range(n)], axis=1)
    return blocks8.reshape(n * r, c)


def _to_blocks(full, kind):
    if kind == "col":
        c = full.shape[1] // N_DEV
        return jnp.stack([full[:, j * c:(j + 1) * c] for j in range(N_DEV)])
    nr, c = full.shape
    return full.reshape(N_DEV, nr // N_DEV, c)


def kernel(x, mem, norm_mix_pre, norm_mix_post, norm_mem, w_in, b_forget, conv_w, conv_b, conv_ln_g, conv_ln_b, w_kv_mem, w_proj_attn, w_proj_conv, w_proj_mem, w_out, norm_ffn_pre, norm_ffn_post, w_gate_up, w_down, loss_target, m_norm_mix_pre, m_norm_mix_post, m_norm_mem, m_w_in, m_b_forget, m_conv_w, m_conv_b, m_conv_ln_g, m_conv_ln_b, m_w_kv_mem, m_w_proj_attn, m_w_proj_conv, m_w_proj_mem, m_w_out, m_norm_ffn_pre, m_norm_ffn_post, m_w_gate_up, m_w_down, v_norm_mix_pre, v_norm_mix_post, v_norm_mem, v_w_in, v_b_forget, v_conv_w, v_conv_b, v_conv_ln_g, v_conv_ln_b, v_w_kv_mem, v_w_proj_attn, v_w_proj_conv, v_w_proj_mem, v_w_out, v_norm_ffn_pre, v_norm_ffn_post, v_w_gate_up, v_w_down):
    given = dict(locals())
    weights = {n: given[n] for n in WEIGHT_ORDER}
    moments_m = {n: given["m_" + n] for n in WEIGHT_ORDER}
    moments_v = {n: given["v_" + n] for n in WEIGHT_ORDER}
    kind = {name: k for name, _, k in SHARDED}
    names = [name for name, _, _ in SHARDED]

    blocks = {n: weights[n][0].astype(BF16) for n in names if n != "conv_w"}
    blocks["conv_w"] = lax.bitcast_convert_type(conv_w[0], BF16).reshape(CONV_WIDTH, 2 * LANES)
    late = [n for n in names if n != "w_in"]
    (w_in_blocks,), w_in_token = _exchange([blocks["w_in"]], scatter=False, name="gather_w_in")
    blocks["w_kv_mem"] = _after(w_in_token, blocks["w_kv_mem"])
    late_handle, late_token = _exchange_start([blocks[n] for n in late], scatter=False, name="gather_rest_start")
    n_qkv = 3 * D_MODEL
    w_in_full = _to_full(w_in_blocks, "col")
    w_early = dict(
        wp=jnp.concatenate([w_in_full[:, :n_qkv], w_in_full[:, n_qkv + FOX_HEADS:]], axis=1),
        wf=jnp.pad(w_in_full[:, n_qkv:n_qkv + FOX_HEADS], ((0, 0), (0, LANES - FOX_HEADS))),
    )

    def late_weights(after):
        got = dict(zip(late, _exchange_wait(late_handle, after, scatter=False, name="gather_rest_wait")))
        conv_w_all = lax.bitcast_convert_type(got["conv_w"].reshape(N_DEV, CONV_WIDTH, LANES, 2), F32)
        full = {n: _to_full(got[n], kind[n]) for n in late if n != "conv_w"}
        return dict(
            conv_w=jnp.pad(_to_full(conv_w_all, "col"), ((0, CONV_HALO - CONV_WIDTH), (0, 0))),
            w_kv=full["w_kv_mem"], wpa=full["w_proj_attn"], wpc=full["w_proj_conv"], wpm=full["w_proj_mem"],
            w_out=full["w_out"], w_gu=full["w_gate_up"], w_down=full["w_down"],
        )

    sent = []

    def send_grads(group, g):
        order = [n for n in names if n in g]
        handle, token = _exchange_start([_to_blocks(g[n], kind[n]) for n in order], scatter=True, name="scatter_" + group + "_start")
        sent.append((group, order, handle))
        return token

    gains = {n: weights[n] for n in SMALL}
    gains["norm_mix_pre"] = _after(late_token, gains["norm_mix_pre"])

    loss_tile, grad_x, small = _local_step(x[0], mem[0], loss_target[0], gains, w_early, late_weights, send_grads)

    g_recv = {}
    for group, order, handle in sent:
        g_recv.update(zip(order, _exchange_wait(handle, grad_x, scatter=True, name="scatter_" + group + "_wait")))

    rows = [jnp.pad(small[n], ((0, 0), (0, D_MODEL - small[n].shape[1]))) for n in SMALL]
    rows.append(jnp.broadcast_to(loss_tile[0:1, 0:1], (1, D_MODEL)))
    rows.append(jnp.zeros((SMALL_ROWS - len(rows), D_MODEL), F32))
    (small_recv,), _ = _exchange([jnp.concatenate(rows, axis=0)], scatter=False, name="gather_small")

    def slab_of(d, fill):
        rows = [jnp.pad(d[n], ((0, 0), (0, D_MODEL - d[n].shape[1])), constant_values=fill) for n in SMALL]
        rows.append(jnp.full((SMALL_ROWS - len(rows), D_MODEL), fill, F32))
        return jnp.concatenate(rows, axis=0)

    grads, delta, new_m, new_v = {}, {}, {}, {}
    sg, sd, sm, sv = _adamw(slab_of(weights, 0.0), small_recv, slab_of(moments_m, 0.0), slab_of(moments_v, 1.0), name="adamw_small")
    loss = sg[LOSS_ROW, 0]
    for i, n in enumerate(SMALL):
        c = weights[n].shape[1]
        grads[n], delta[n], new_m[n], new_v[n] = sg[i:i + 1, :c], sd[i:i + 1, :c], sm[i:i + 1, :c], sv[i:i + 1, :c]
    for n in names:
        gw, d, m2, v2 = _adamw(weights[n][0], g_recv[n], moments_m[n][0], moments_v[n][0], name="adamw_" + n)
        grads[n], delta[n], new_m[n], new_v[n] = gw[None], d[None], m2[None], v2[None]

    return (loss, grad_x[None], *[grads[n] for n in WEIGHT_ORDER], *[delta[n] for n in WEIGHT_ORDER],
            *[new_m[n] for n in WEIGHT_ORDER], *[new_v[n] for n in WEIGHT_ORDER])
```

```python
import functools

import jax
import jax.numpy as jnp
from jax import lax
from jax.experimental import pallas as pl
from jax.experimental.pallas import tpu as pltpu

F32 = jnp.float32
BF16 = jnp.bfloat16

D_MODEL = 1024
N_DEV = 8
FOX_HEADS = 16
FOX_HEAD_DIM = 64
HEAD_PAIRS = FOX_HEADS // 2
MEM_HEADS = 4
MEM_HEAD_DIM = D_MODEL // MEM_HEADS
CONV_WIDTH = 31
CONV_HALO = 32
CONV_CHUNK = 16
SUBLANES = 8
FFN_HIDDEN = 2816
RMS_EPS = 1e-6
LN_EPS = 1e-5
ADAM_LR = 0.001
ADAM_B1 = 0.9
ADAM_B2 = 0.999
ADAM_EPS = 1e-08
ADAM_WD = 0.01
ADAM_STEP = 10
NEG_BIG = -1e30
LANES = 128

PB_Q, PB_K, PB_V, PB_A, PB_GATE, PB_QMEM, PB_G0 = 0, 1, 2, 3, 4, 5, 6
P_WIDTH = 9 * D_MODEL

NT_DIMS = (((1,), (1,)), ((), ()))
TN_DIMS = (((0,), (0,)), ((), ()))


def _cparams(sem, vmem_mb=None):
    kw = dict(dimension_semantics=sem)
    if vmem_mb is not None:
        kw["vmem_limit_bytes"] = vmem_mb * 1024 * 1024
    return pltpu.CompilerParams(**kw)


def _tile(dim, want):
    t = min(dim, want)
    assert dim % t == 0, (dim, want)
    return t


def _sigmoid(z):
    return 1.0 / (1.0 + jnp.exp(-z))


def _matmul(a, b, *, mode, out_dtype, name, tm=1024, tn=1024, tk=1024):
    if mode == "nn":
        (M, K), (K2, N) = a.shape, b.shape
    elif mode == "nt":
        (M, K), (N, K2) = a.shape, b.shape
    else:
        (K, M), (K2, N) = a.shape, b.shape
    assert K == K2, (a.shape, b.shape, mode)
    tm, tn, tk = _tile(M, tm), _tile(N, tn), _tile(K, tk)
    nk = K // tk
    dims = {"nn": (((1,), (0,)), ((), ())), "nt": NT_DIMS, "tn": TN_DIMS}[mode]

    def body(a_ref, b_ref, o_ref, *scratch):
        part = lax.dot_general(a_ref[...], b_ref[...], dims, preferred_element_type=F32)
        if nk == 1:
            o_ref[...] = part.astype(o_ref.dtype)
        else:
            acc_ref, = scratch
            k = pl.program_id(2)

            @pl.when(k == 0)
            def _():
                acc_ref[...] = part

            @pl.when(k > 0)
            def _():
                acc_ref[...] += part

            @pl.when(k == nk - 1)
            def _():
                o_ref[...] = acc_ref[...].astype(o_ref.dtype)

    a_spec = pl.BlockSpec((tk, tm), lambda j, i, k: (k, i)) if mode == "tn" else pl.BlockSpec((tm, tk), lambda j, i, k: (i, k))
    b_spec = pl.BlockSpec((tn, tk), lambda j, i, k: (j, k)) if mode == "nt" else pl.BlockSpec((tk, tn), lambda j, i, k: (k, j))
    return pl.pallas_call(
        body,
        name=name,
        grid=(N // tn, M // tm, nk),
        in_specs=[a_spec, b_spec],
        out_specs=pl.BlockSpec((tm, tn), lambda j, i, k: (i, j)),
        out_shape=jax.ShapeDtypeStruct((M, N), out_dtype),
        scratch_shapes=[pltpu.VMEM((tm, tn), F32)] if nk > 1 else [],
        compiler_params=_cparams(("parallel", "parallel", "arbitrary"), 56),
    )(a, b)


def _rms_fwd(x, g, *, name, tm=512):
    S, D = x.shape
    tm = _tile(S, tm)

    def body(x_ref, g_ref, o_ref):
        xv = x_ref[...]
        rstd = lax.rsqrt(jnp.mean(xv * xv, axis=-1, keepdims=True) + RMS_EPS)
        o_ref[...] = (xv * rstd * g_ref[...]).astype(o_ref.dtype)

    return pl.pallas_call(
        body, name=name, grid=(S // tm,),
        in_specs=[pl.BlockSpec((tm, D), lambda i: (i, 0)), pl.BlockSpec((1, D), lambda i: (0, 0))],
        out_specs=pl.BlockSpec((tm, D), lambda i: (i, 0)),
        out_shape=jax.ShapeDtypeStruct((S, D), BF16),
        compiler_params=_cparams(("parallel",)),
    )(x, g)


def _rms_bwd(xin, dys, g, res, *, out_dtype, name, tm=512):
    S, D = xin.shape
    tm = _tile(S, tm)
    n_dy = len(dys)
    has_res = res is not None

    def body(*refs):
        x_ref, g_ref = refs[0], refs[1]
        dy_refs = refs[2:2 + n_dy]
        pos = 2 + n_dy
        res_ref = refs[pos] if has_res else None
        pos += int(has_res)
        dx_ref, dg_ref = refs[pos], refs[pos + 1]
        i = pl.program_id(0)
        xv = x_ref[...]
        dy = dy_refs[0][...].astype(F32)
        for r in dy_refs[1:]:
            dy = dy + r[...].astype(F32)
        rstd = lax.rsqrt(jnp.mean(xv * xv, axis=-1, keepdims=True) + RMS_EPS)
        xhat = xv * rstd
        gy = dy * g_ref[...]
        dx = rstd * (gy - xhat * jnp.mean(gy * xhat, axis=-1, keepdims=True))
        if has_res:
            dx = dx + res_ref[...]
        dx_ref[...] = dx.astype(dx_ref.dtype)
        part = jnp.sum(dy * xhat, axis=0, keepdims=True)

        @pl.when(i == 0)
        def _():
            dg_ref[...] = part

        @pl.when(i > 0)
        def _():
            dg_ref[...] += part

    row = pl.BlockSpec((tm, D), lambda i: (i, 0))
    vec = pl.BlockSpec((1, D), lambda i: (0, 0))
    ins = [xin, g] + list(dys) + ([res] if has_res else [])
    return pl.pallas_call(
        body, name=name, grid=(S // tm,),
        in_specs=[row, vec] + [row] * n_dy + ([row] if has_res else []),
        out_specs=[row, vec],
        out_shape=[jax.ShapeDtypeStruct((S, D), out_dtype), jax.ShapeDtypeStruct((1, D), F32)],
        compiler_params=_cparams(("arbitrary",)),
    )(*ins)


def _head_mask(hh, shape):
    lane = lax.broadcasted_iota(jnp.int32, shape, len(shape) - 1)
    return (lane // FOX_HEAD_DIM) == hh


def _attn_fwd(P, ccol, crow, *, tq, name):
    S = P.shape[0]
    tq = _tile(S, tq)
    nq = S // tq
    scale = FOX_HEAD_DIM ** -0.5

    def body(q_ref, k_ref, v_ref, cc_ref, cr_ref, o_ref, o32_ref, lse_ref):
        i = pl.program_id(1)
        q = q_ref[...] * jnp.asarray(scale, BF16)
        row = lax.broadcasted_iota(jnp.int32, (tq, tq), 0)
        col = lax.broadcasted_iota(jnp.int32, (tq, tq), 1)
        causal = col <= row
        hms = [_head_mask(hh, (tq, LANES)) for hh in range(2)]
        qhs = [jnp.where(hm, q, jnp.zeros_like(q)) for hm in hms]
        cbs = [cc_ref[hh, 0:1, :] for hh in range(2)]

        def step(j, carry, masked):
            start = pl.multiple_of(j * tq, tq)
            kj = k_ref[pl.ds(start, tq), :]
            vj = v_ref[pl.ds(start, tq), :]
            new = []
            for hh in range(2):
                m, acc = carry[hh]
                bias = cbs[hh] - cr_ref[0, hh:hh + 1, pl.ds(start, tq)]
                s = lax.dot_general(qhs[hh], kj, NT_DIMS, preferred_element_type=F32) + bias
                if masked:
                    s = jnp.where(causal, s, NEG_BIG)
                m_new = jnp.maximum(m, jnp.max(s, axis=1, keepdims=True))
                alpha = jnp.exp(m - m_new)
                p = jnp.exp(s - m_new)
                vh = jnp.where(ones_lane[hh], jnp.ones_like(vj), vj)
                acc = alpha * acc + jnp.dot(p.astype(BF16), vh, preferred_element_type=F32)
                new.append((m_new, acc))
            return tuple(new)

        lane = lax.broadcasted_iota(jnp.int32, (tq, LANES), 1)
        ones_lane = [lane == (1 - hh) * FOX_HEAD_DIM for hh in range(2)]
        init = (jnp.full((tq, 1), NEG_BIG, F32), jnp.zeros((tq, LANES), F32))
        carry = lax.fori_loop(0, i, functools.partial(step, masked=False), (init, init))
        carry = step(i, carry, True)
        ls = [jnp.sum(jnp.where(ones_lane[hh], carry[hh][1], 0.0), axis=1, keepdims=True) for hh in range(2)]
        out = jnp.where(hms[0], carry[0][1] / ls[0], carry[1][1] / ls[1])
        for hh in range(2):
            lse_ref[hh] = carry[hh][0] + jnp.log(ls[hh])
        o_ref[...] = out.astype(o_ref.dtype)
        o32_ref[...] = out

    nblk = D_MODEL // LANES
    return pl.pallas_call(
        body, name=name, grid=(HEAD_PAIRS, nq),
        in_specs=[
            pl.BlockSpec((tq, LANES), lambda p, i: (i, PB_Q * nblk + p)),
            pl.BlockSpec((S, LANES), lambda p, i: (0, PB_K * nblk + p)),
            pl.BlockSpec((S, LANES), lambda p, i: (0, PB_V * nblk + p)),
            pl.BlockSpec((2, tq, 1), lambda p, i: (p, i, 0)),
            pl.BlockSpec((1, 2, S), lambda p, i: (p, 0, 0)),
        ],
        out_specs=[
            pl.BlockSpec((tq, LANES), lambda p, i: (i, p)),
            pl.BlockSpec((tq, LANES), lambda p, i: (i, p)),
            pl.BlockSpec((2, tq, 1), lambda p, i: (p, i, 0)),
        ],
        out_shape=[jax.ShapeDtypeStruct((S, D_MODEL), BF16), jax.ShapeDtypeStruct((S, D_MODEL), F32),
                   jax.ShapeDtypeStruct((FOX_HEADS, S, 1), F32)],
        compiler_params=_cparams(("parallel", "arbitrary"), 56),
    )(P, P, P, ccol, crow)


def _attn_bwd(P, do, o32, lse, ccol, crow, *, tq, name):
    S = P.shape[0]
    tq = _tile(S, tq)
    nq = S // tq
    scale = FOX_HEAD_DIM ** -0.5

    def body(q_ref, k_ref, v_ref, do_ref, o_ref, lse_ref, cc_ref, cr_ref, dq_ref, dk_out, dv_out, dc_ref, dcq_ref, dk_ref, dv_ref):
        i = pl.program_id(1)

        @pl.when(i == 0)
        def _():
            dk_ref[...] = jnp.zeros_like(dk_ref)
            dv_ref[...] = jnp.zeros_like(dv_ref)
            dc_ref[...] = jnp.zeros_like(dc_ref)

        q = q_ref[...] * jnp.asarray(scale, BF16)
        do_v = do_ref[...]
        row = lax.broadcasted_iota(jnp.int32, (tq, tq), 0)
        col = lax.broadcasted_iota(jnp.int32, (tq, tq), 1)
        causal = col <= row
        hms = [_head_mask(hh, (tq, LANES)) for hh in range(2)]
        qhs = [jnp.where(hm, q, jnp.zeros_like(q)) for hm in hms]
        dohs = [jnp.where(hm, do_v, jnp.zeros_like(do_v)) for hm in hms]
        cbs = [cc_ref[hh, 0:1, :] for hh in range(2)]
        lses = [lse_ref[hh] for hh in range(2)]
        prod = do_v.astype(F32) * o_ref[...]
        dls = [jnp.sum(jnp.where(hm, prod, 0.0), axis=1, keepdims=True) for hm in hms]

        def step(j, carry, masked):
            dq_acc, row_sums = carry[0], list(carry[1:])
            start = pl.multiple_of(j * tq, tq)
            kj = k_ref[pl.ds(start, tq), :]
            vj = v_ref[pl.ds(start, tq), :]
            dv_part = jnp.zeros((tq, LANES), F32)
            dk_part = jnp.zeros((tq, LANES), F32)
            for hh in range(2):
                bias = cbs[hh] - cr_ref[0, hh:hh + 1, pl.ds(start, tq)]
                s = lax.dot_general(qhs[hh], kj, NT_DIMS, preferred_element_type=F32) + bias
                if masked:
                    s = jnp.where(causal, s, NEG_BIG)
                p = jnp.exp(s - lses[hh])
                dp = lax.dot_general(dohs[hh], vj, NT_DIMS, preferred_element_type=F32)
                ds = p * (dp - dls[hh])
                pb = p.astype(BF16)
                dsb = ds.astype(BF16)
                dv_part = dv_part + lax.dot_general(pb, dohs[hh], TN_DIMS, preferred_element_type=F32)
                dk_part = dk_part + lax.dot_general(dsb, qhs[hh], TN_DIMS, preferred_element_type=F32)
                dc_ref[0, hh:hh + 1, pl.ds(start, tq)] -= jnp.sum(ds, axis=0, keepdims=True)
                row_sums[hh] = row_sums[hh] + jnp.sum(ds, axis=1, keepdims=True)
                kh = jnp.where(hms[hh], kj, jnp.zeros_like(kj))
                dq_acc = dq_acc + jnp.dot(dsb, kh, preferred_element_type=F32)
            dv_ref[pl.ds(start, tq), :] += dv_part
            dk_ref[pl.ds(start, tq), :] += dk_part
            return (dq_acc, *row_sums)

        zero_col = jnp.zeros((tq, 1), F32)
        carry = lax.fori_loop(0, i, functools.partial(step, masked=False), (jnp.zeros((tq, LANES), F32), zero_col, zero_col))
        carry = step(i, carry, True)
        dq_ref[...] = (carry[0] * scale).astype(dq_ref.dtype)
        for hh in range(2):
            dcq_ref[hh] = carry[1 + hh]

        @pl.when(i == nq - 1)
        def _():
            dk_out[...] = dk_ref[...].astype(dk_out.dtype)
            dv_out[...] = dv_ref[...].astype(dv_out.dtype)

    nblk = D_MODEL // LANES
    qblk = pl.BlockSpec((tq, LANES), lambda p, i: (i, p))
    stat = pl.BlockSpec((2, tq, 1), lambda p, i: (p, i, 0))
    full = pl.BlockSpec((S, LANES), lambda p, i: (0, p))
    return pl.pallas_call(
        body, name=name, grid=(HEAD_PAIRS, nq),
        in_specs=[
            pl.BlockSpec((tq, LANES), lambda p, i: (i, PB_Q * nblk + p)),
            pl.BlockSpec((S, LANES), lambda p, i: (0, PB_K * nblk + p)),
            pl.BlockSpec((S, LANES), lambda p, i: (0, PB_V * nblk + p)),
            qblk, qblk, stat, stat,
            pl.BlockSpec((1, 2, S), lambda p, i: (p, 0, 0)),
        ],
        out_specs=[qblk, full, full, pl.BlockSpec((1, 2, S), lambda p, i: (p, 0, 0)), stat],
        out_shape=[
            jax.ShapeDtypeStruct((S, D_MODEL), BF16),
            jax.ShapeDtypeStruct((S, D_MODEL), BF16),
            jax.ShapeDtypeStruct((S, D_MODEL), BF16),
            jax.ShapeDtypeStruct((HEAD_PAIRS, 2, S), F32),
            jax.ShapeDtypeStruct((FOX_HEADS, S, 1), F32),
        ],
        scratch_shapes=[pltpu.VMEM((S, LANES), F32), pltpu.VMEM((S, LANES), F32)],
        compiler_params=_cparams(("parallel", "arbitrary"), 56),
    )(P, P, P, do, o32, lse, ccol, crow)


def _cumsum_lanes(xs, *, reverse, name):
    R, S = xs[0].shape
    nb = S // LANES
    n_in = len(xs)

    def body(*refs):
        x_refs, o_ref = refs[:n_in], refs[n_in]
        r = lax.broadcasted_iota(jnp.int32, (LANES, LANES), 0)
        c = lax.broadcasted_iota(jnp.int32, (LANES, LANES), 1)
        tri = ((r >= c) if reverse else (r <= c)).astype(F32)

        def step(b, carry):
            blk = (nb - 1 - b) if reverse else b
            start = pl.multiple_of(blk * LANES, LANES)
            xb = x_refs[0][:, pl.ds(start, LANES)]
            for r in x_refs[1:]:
                xb = xb + r[:, pl.ds(start, LANES)]
            y = jnp.dot(xb, tri, precision=lax.Precision.HIGHEST, preferred_element_type=F32) + carry
            o_ref[:, pl.ds(start, LANES)] = y
            return carry + jnp.sum(xb, axis=1, keepdims=True)

        lax.fori_loop(0, nb, step, jnp.zeros((R, 1), F32))

    return pl.pallas_call(
        body, name=name,
        in_specs=[pl.BlockSpec(memory_space=pltpu.VMEM)] * n_in,
        out_specs=pl.BlockSpec(memory_space=pltpu.VMEM),
        out_shape=jax.ShapeDtypeStruct((R, S), F32),
    )(*xs)


def _forget_fwd(f_logit, b_pad, *, name, tm=1024):
    S = f_logit.shape[0]
    tm = _tile(S, tm)

    def body(f_ref, b_ref, o_ref):
        z = f_ref[...] + b_ref[...]
        o_ref[...] = jnp.minimum(z, 0.0) - jnp.log(1.0 + jnp.exp(-jnp.abs(z)))

    blk = pl.BlockSpec((tm, LANES), lambda i: (i, 0))
    return pl.pallas_call(
        body, name=name, grid=(S // tm,),
        in_specs=[blk, pl.BlockSpec((1, LANES), lambda i: (0, 0))],
        out_specs=blk, out_shape=jax.ShapeDtypeStruct((S, LANES), F32),
        compiler_params=_cparams(("parallel",)),
    )(f_logit, b_pad)


def _forget_bwd(f_logit, b_pad, dlogf, *, name, tm=1024):
    S = f_logit.shape[0]
    tm = _tile(S, tm)

    def body(f_ref, b_ref, d_ref, o_ref, db_ref):
        i = pl.program_id(0)
        z = f_ref[...] + b_ref[...]
        dz = d_ref[...] * (1.0 - _sigmoid(z))
        o_ref[...] = dz.astype(o_ref.dtype)
        part = jnp.sum(dz, axis=0, keepdims=True)

        @pl.when(i == 0)
        def _():
            db_ref[...] = part

        @pl.when(i > 0)
        def _():
            db_ref[...] += part

    blk = pl.BlockSpec((tm, LANES), lambda i: (i, 0))
    vec = pl.BlockSpec((1, LANES), lambda i: (0, 0))
    return pl.pallas_call(
        body, name=name, grid=(S // tm,),
        in_specs=[blk, vec, blk], out_specs=[blk, vec],
        out_shape=[jax.ShapeDtypeStruct((S, LANES), BF16), jax.ShapeDtypeStruct((1, LANES), F32)],
        compiler_params=_cparams(("arbitrary",)),
    )(f_logit, b_pad, dlogf)


def _layernorm_stats(y):
    mu = jnp.mean(y, axis=-1, keepdims=True)
    yc = y - mu
    rstd = lax.rsqrt(jnp.mean(yc * yc, axis=-1, keepdims=True) + LN_EPS)
    return yc * rstd, rstd


def _fill_shifted(src_ref, sh_ref, tm):
    rows = tm + CONV_HALO - SUBLANES
    for s in range(1, SUBLANES):
        sh_ref[s - 1, 0:rows, :] = src_ref[pl.ds(s, rows), :]


def _shifted_rows(src_ref, sh_ref, r0, offset):
    s, base = offset % SUBLANES, offset - offset % SUBLANES
    rows = pl.ds(pl.multiple_of(r0 + base, SUBLANES), CONV_CHUNK)
    return src_ref[rows, :] if s == 0 else sh_ref[s - 1, rows, :]


def _fill_tap_rows(w_ref, wb_ref):
    for k in range(CONV_WIDTH):
        wb_ref[k * SUBLANES:(k + 1) * SUBLANES, :] = jnp.broadcast_to(w_ref[k:k + 1, :], (SUBLANES, w_ref.shape[1]))


def _tap_sum(wb_ref, rows_of_tap):
    n = CONV_CHUNK // SUBLANES
    accs = [None] * n
    for k in range(CONV_WIDTH):
        wk = wb_ref[k * SUBLANES:(k + 1) * SUBLANES, :]
        src = rows_of_tap(k)
        for h in range(n):
            term = wk * src[h * SUBLANES:(h + 1) * SUBLANES]
            accs[h] = term if accs[h] is None else accs[h] + term
    return jnp.concatenate(accs, axis=0)


def _conv_fwd(P, w_pad, conv_b, ln_g, ln_b, *, name, tm=256):
    S = P.shape[0]
    C = D_MODEL
    tm = _tile(S, tm)
    hb = tm // CONV_HALO

    def body(a_ref, gt_ref, ah_ref, gh_ref, w_ref, cb_ref, g_ref, b_ref, o_ref, y_ref, glu_ref, sh_ref, wb_ref):
        i = pl.program_id(0)
        halo = ah_ref[...].astype(F32) * _sigmoid(gh_ref[...].astype(F32))
        glu_ref[0:CONV_HALO, :] = jnp.where(i > 0, halo, 0.0)
        glu_ref[CONV_HALO:, :] = a_ref[...].astype(F32) * _sigmoid(gt_ref[...].astype(F32))
        _fill_shifted(glu_ref, sh_ref, tm)
        _fill_tap_rows(w_ref, wb_ref)

        def chunk(r, carry):
            r0 = pl.multiple_of(r * CONV_CHUNK, CONV_CHUNK)
            acc = _tap_sum(wb_ref, lambda k: _shifted_rows(glu_ref, sh_ref, r0, CONV_HALO - (CONV_WIDTH - 1) + k))
            y_ref[pl.ds(r0, CONV_CHUNK), :] = acc + cb_ref[...]
            return carry

        lax.fori_loop(0, tm // CONV_CHUNK, chunk, 0)
        xhat, _ = _layernorm_stats(y_ref[...])
        z = xhat * g_ref[...] + b_ref[...]
        o_ref[...] = (z * _sigmoid(z)).astype(o_ref.dtype)

    vec = pl.BlockSpec((1, C), lambda i: (0, 0))
    row = pl.BlockSpec((tm, C), lambda i: (i, 0))
    return pl.pallas_call(
        body, name=name, grid=(S // tm,),
        in_specs=[
            pl.BlockSpec((tm, C), lambda i: (i, PB_A)),
            pl.BlockSpec((tm, C), lambda i: (i, PB_GATE)),
            pl.BlockSpec((CONV_HALO, C), lambda i: (jnp.maximum(i * hb - 1, 0), PB_A)),
            pl.BlockSpec((CONV_HALO, C), lambda i: (jnp.maximum(i * hb - 1, 0), PB_GATE)),
            pl.BlockSpec((CONV_HALO, C), lambda i: (0, 0)),
            vec, vec, vec,
        ],
        out_specs=[row, row],
        out_shape=[jax.ShapeDtypeStruct((S, C), BF16), jax.ShapeDtypeStruct((S, C), F32)],
        scratch_shapes=[pltpu.VMEM((tm + CONV_HALO, C), F32), pltpu.VMEM((SUBLANES - 1, tm + CONV_HALO, C), F32),
                        pltpu.VMEM((CONV_HALO * SUBLANES, C), F32)],
        compiler_params=_cparams(("parallel",), 56),
    )(P, P, P, P, w_pad, conv_b, ln_g, ln_b)


def _conv_bwd(P, do, y, w_pad, ln_g, ln_b, *, name, tm=256):
    S = P.shape[0]
    C = D_MODEL
    tm = _tile(S, tm)
    hb = tm // CONV_HALO
    n_tiles = S // tm
    last_halo = S // CONV_HALO - 1

    def body(a_ref, gt_ref, ah_ref, gh_ref, do_ref, y_ref, don_ref, yn_ref, w_ref, g_ref, b_ref,
             dglu_ref, dw_ref, small_ref, glu_ref, dy_ref, gsh_ref, dsh_ref, dwacc_ref, wb_ref):
        i = pl.program_id(0)

        @pl.when(i == 0)
        def _():
            dwacc_ref[...] = jnp.zeros_like(dwacc_ref)
            small_ref[...] = jnp.zeros_like(small_ref)

        def ln_bwd(do_v, y_v):
            xhat, rstd = _layernorm_stats(y_v)
            z = xhat * g_ref[...] + b_ref[...]
            sg = _sigmoid(z)
            dz = do_v * (sg * (1.0 + z * (1.0 - sg)))
            dxh = dz * g_ref[...]
            dy = rstd * (dxh - jnp.mean(dxh, axis=-1, keepdims=True) - xhat * jnp.mean(dxh * xhat, axis=-1, keepdims=True))
            return dy, dz, xhat

        dy, dz, xhat = ln_bwd(do_ref[...], y_ref[...])
        dy_next, _, _ = ln_bwd(don_ref[...], yn_ref[...])
        small_ref[0:1, :] += jnp.sum(dz * xhat, axis=0, keepdims=True)
        small_ref[1:2, :] += jnp.sum(dz, axis=0, keepdims=True)
        small_ref[2:3, :] += jnp.sum(dy, axis=0, keepdims=True)
        dy_ref[0:tm, :] = dy
        dy_ref[tm:, :] = jnp.where(i < n_tiles - 1, dy_next, 0.0)

        halo = ah_ref[...].astype(F32) * _sigmoid(gh_ref[...].astype(F32))
        glu_ref[0:CONV_HALO, :] = jnp.where(i > 0, halo, 0.0)
        glu_ref[CONV_HALO:, :] = a_ref[...].astype(F32) * _sigmoid(gt_ref[...].astype(F32))
        _fill_shifted(glu_ref, gsh_ref, tm)
        _fill_shifted(dy_ref, dsh_ref, tm)
        _fill_tap_rows(w_ref, wb_ref)

        def chunk(r, carry):
            r0 = pl.multiple_of(r * CONV_CHUNK, CONV_CHUNK)
            rows = pl.ds(r0, CONV_CHUNK)
            dyc = dy_ref[rows, :]
            for k in range(CONV_WIDTH):
                prod = dyc * _shifted_rows(glu_ref, gsh_ref, r0, CONV_HALO - (CONV_WIDTH - 1) + k)
                dwacc_ref[k * SUBLANES:(k + 1) * SUBLANES, :] += prod[0:SUBLANES] + prod[SUBLANES:]
            dg = _tap_sum(wb_ref, lambda k: _shifted_rows(dy_ref, dsh_ref, r0, CONV_WIDTH - 1 - k))
            a = a_ref[rows, :].astype(F32)
            sig = _sigmoid(gt_ref[rows, :].astype(F32))
            dglu_ref[rows, 0:C] = (dg * sig).astype(dglu_ref.dtype)
            dglu_ref[rows, C:] = (dg * a * sig * (1.0 - sig)).astype(dglu_ref.dtype)
            return carry

        lax.fori_loop(0, tm // CONV_CHUNK, chunk, 0)

        @pl.when(i == n_tiles - 1)
        def _():
            for k in range(CONV_WIDTH):
                dw_ref[k:k + 1, :] = jnp.sum(dwacc_ref[k * SUBLANES:(k + 1) * SUBLANES, :], axis=0, keepdims=True)
            dw_ref[CONV_WIDTH:, :] = jnp.zeros((CONV_HALO - CONV_WIDTH, C), F32)

    vec = pl.BlockSpec((1, C), lambda i: (0, 0))
    row = pl.BlockSpec((tm, C), lambda i: (i, 0))
    nxt = pl.BlockSpec((CONV_HALO, C), lambda i: (jnp.minimum((i + 1) * hb, last_halo), 0))
    return pl.pallas_call(
        body, name=name, grid=(n_tiles,),
        in_specs=[
            pl.BlockSpec((tm, C), lambda i: (i, PB_A)),
            pl.BlockSpec((tm, C), lambda i: (i, PB_GATE)),
            pl.BlockSpec((CONV_HALO, C), lambda i: (jnp.maximum(i * hb - 1, 0), PB_A)),
            pl.BlockSpec((CONV_HALO, C), lambda i: (jnp.maximum(i * hb - 1, 0), PB_GATE)),
            row, row, nxt, nxt,
            pl.BlockSpec((CONV_HALO, C), lambda i: (0, 0)),
            vec, vec,
        ],
        out_specs=[
            pl.BlockSpec((tm, 2 * C), lambda i: (i, 0)),
            pl.BlockSpec((CONV_HALO, C), lambda i: (0, 0)),
            pl.BlockSpec((8, C), lambda i: (0, 0)),
        ],
        out_shape=[
            jax.ShapeDtypeStruct((S, 2 * C), BF16),
            jax.ShapeDtypeStruct((CONV_HALO, C), F32),
            jax.ShapeDtypeStruct((8, C), F32),
        ],
        scratch_shapes=[
            pltpu.VMEM((tm + CONV_HALO, C), F32), pltpu.VMEM((tm + CONV_HALO, C), F32),
            pltpu.VMEM((SUBLANES - 1, tm + CONV_HALO, C), F32), pltpu.VMEM((SUBLANES - 1, tm + CONV_HALO, C), F32),
            pltpu.VMEM((CONV_HALO * SUBLANES, C), F32), pltpu.VMEM((CONV_HALO * SUBLANES, C), F32),
        ],
        compiler_params=_cparams(("arbitrary",), 56),
    )(P, P, P, P, do, y, do, y, w_pad, ln_g, ln_b)


def _mem_softmax(qh, kh):
    s = lax.dot_general(qh, kh, NT_DIMS, preferred_element_type=F32)
    e = jnp.exp(s - jnp.max(s, axis=1, keepdims=True))
    return e / jnp.sum(e, axis=1, keepdims=True)


def _mem_fwd(P, kv, *, name, tm=512):
    S, M = P.shape[0], kv.shape[0]
    tm = _tile(S, tm)
    scale = MEM_HEAD_DIM ** -0.5

    def body(q_ref, k_ref, v_ref, o_ref):
        for h in range(MEM_HEADS):
            sl = slice(h * MEM_HEAD_DIM, (h + 1) * MEM_HEAD_DIM)
            qh = q_ref[:, sl] * jnp.asarray(scale, BF16)
            p = _mem_softmax(qh, k_ref[:, sl])
            o_ref[:, sl] = jnp.dot(p.astype(BF16), v_ref[:, sl], preferred_element_type=F32).astype(o_ref.dtype)

    return pl.pallas_call(
        body, name=name, grid=(S // tm,),
        in_specs=[
            pl.BlockSpec((tm, D_MODEL), lambda i: (i, PB_QMEM)),
            pl.BlockSpec((M, D_MODEL), lambda i: (0, 0)),
            pl.BlockSpec((M, D_MODEL), lambda i: (0, 1)),
        ],
        out_specs=pl.BlockSpec((tm, D_MODEL), lambda i: (i, 0)),
        out_shape=jax.ShapeDtypeStruct((S, D_MODEL), BF16),
        compiler_params=_cparams(("parallel",)),
    )(P, kv, kv)


def _mem_bwd(P, do, kv, *, name, tm=512):
    S, M = P.shape[0], kv.shape[0]
    tm = _tile(S, tm)
    scale = MEM_HEAD_DIM ** -0.5

    def body(q_ref, k_ref, v_ref, do_ref, dq_ref, dkv_ref):
        i = pl.program_id(0)

        @pl.when(i == 0)
        def _():
            dkv_ref[...] = jnp.zeros_like(dkv_ref)

        for h in range(MEM_HEADS):
            sl = slice(h * MEM_HEAD_DIM, (h + 1) * MEM_HEAD_DIM)
            slv = slice(D_MODEL + h * MEM_HEAD_DIM, D_MODEL + (h + 1) * MEM_HEAD_DIM)
            qh = q_ref[:, sl] * jnp.asarray(scale, BF16)
            kh, vh, doh = k_ref[:, sl], v_ref[:, sl], do_ref[:, sl]
            p = _mem_softmax(qh, kh)
            dp = lax.dot_general(doh, vh, NT_DIMS, preferred_element_type=F32)
            ds = p * (dp - jnp.sum(p * dp, axis=1, keepdims=True))
            dsb = ds.astype(BF16)
            dq_ref[:, sl] = (jnp.dot(dsb, kh, preferred_element_type=F32) * scale).astype(dq_ref.dtype)
            dkv_ref[:, sl] += lax.dot_general(dsb, qh, TN_DIMS, preferred_element_type=F32)
            dkv_ref[:, slv] += lax.dot_general(p.astype(BF16), doh, TN_DIMS, preferred_element_type=F32)

    row = pl.BlockSpec((tm, D_MODEL), lambda i: (i, 0))
    return pl.pallas_call(
        body, name=name, grid=(S // tm,),
        in_specs=[
            pl.BlockSpec((tm, D_MODEL), lambda i: (i, PB_QMEM)),
            pl.BlockSpec((M, D_MODEL), lambda i: (0, 0)),
            pl.BlockSpec((M, D_MODEL), lambda i: (0, 1)),
            row,
        ],
        out_specs=[row, pl.BlockSpec((M, 2 * D_MODEL), lambda i: (0, 0))],
        out_shape=[jax.ShapeDtypeStruct((S, D_MODEL), BF16), jax.ShapeDtypeStruct((M, 2 * D_MODEL), F32)],
        compiler_params=_cparams(("arbitrary",)),
    )(P, kv, kv, do)


def _merge_fwd(o_attn, o_conv, o_mem, P, wpa, wpc, wpm, w_out, *, name, tm=256):
    S = P.shape[0]
    D = D_MODEL
    tm = _tile(S, tm)

    def body(oa_ref, oc_ref, om_ref, g0_ref, g1_ref, g2_ref, wa_ref, wc_ref, wm_ref, wo_ref, mg_ref, y_ref, pb_ref):
        merged = jnp.zeros((tm, D), F32)
        for b, (o_ref, g_ref, w_ref) in enumerate(((oa_ref, g0_ref, wa_ref), (oc_ref, g1_ref, wc_ref), (om_ref, g2_ref, wm_ref))):
            pb = jnp.dot(o_ref[...], w_ref[...], preferred_element_type=F32)
            pb_ref[b] = pb.astype(pb_ref.dtype)
            merged = merged + _sigmoid(g_ref[...].astype(F32)) * pb
        mb = merged.astype(BF16)
        mg_ref[...] = mb
        y_ref[...] = jnp.dot(mb, wo_ref[...], preferred_element_type=F32)

    row = pl.BlockSpec((tm, D), lambda i: (i, 0))
    wsp = pl.BlockSpec((D, D), lambda i: (0, 0))
    return pl.pallas_call(
        body, name=name, grid=(S // tm,),
        in_specs=[row, row, row] + [pl.BlockSpec((tm, D), lambda i, b=b: (i, PB_G0 + b)) for b in range(3)] + [wsp] * 4,
        out_specs=[row, row, pl.BlockSpec((3, tm, D), lambda i: (0, i, 0))],
        out_shape=[jax.ShapeDtypeStruct((S, D), BF16), jax.ShapeDtypeStruct((S, D), F32), jax.ShapeDtypeStruct((3, S, D), BF16)],
        compiler_params=_cparams(("parallel",), 56),
    )(o_attn, o_conv, o_mem, P, P, P, wpa, wpc, wpm, w_out)


def _merge_bwd(dmerged, P, pb, *, name, tm=256):
    S = P.shape[0]
    D = D_MODEL
    tm = _tile(S, tm)

    def body(dm_ref, g0_ref, g1_ref, g2_ref, pb_ref, dpb_ref, dgl_ref):
        dm = dm_ref[...].astype(F32)
        for b, g_ref in enumerate((g0_ref, g1_ref, g2_ref)):
            g = _sigmoid(g_ref[...].astype(F32))
            dpb_ref[b] = (dm * g).astype(dpb_ref.dtype)
            dgl_ref[:, b * D:(b + 1) * D] = (dm * pb_ref[b].astype(F32) * g * (1.0 - g)).astype(dgl_ref.dtype)

    row = pl.BlockSpec((tm, D), lambda i: (i, 0))
    blk3 = pl.BlockSpec((3, tm, D), lambda i: (0, i, 0))
    return pl.pallas_call(
        body, name=name, grid=(S // tm,),
        in_specs=[row] + [pl.BlockSpec((tm, D), lambda i, b=b: (i, PB_G0 + b)) for b in range(3)] + [blk3],
        out_specs=[blk3, pl.BlockSpec((tm, 3 * D), lambda i: (i, 0))],
        out_shape=[jax.ShapeDtypeStruct((3, S, D), BF16), jax.ShapeDtypeStruct((S, 3 * D), BF16)],
        compiler_params=_cparams(("parallel",), 56),
    )(dmerged, P, P, P, pb)


def _resid_norm(x, y1, g_post, g_pre, *, name, tm=512):
    S, D = x.shape
    tm = _tile(S, tm)

    def body(x_ref, y_ref, gp_ref, gq_ref, x1_ref, h2_ref):
        yv = y_ref[...]
        x1 = x_ref[...] + yv * lax.rsqrt(jnp.mean(yv * yv, axis=-1, keepdims=True) + RMS_EPS) * gp_ref[...]
        x1_ref[...] = x1
        h2_ref[...] = (x1 * lax.rsqrt(jnp.mean(x1 * x1, axis=-1, keepdims=True) + RMS_EPS) * gq_ref[...]).astype(h2_ref.dtype)

    row = pl.BlockSpec((tm, D), lambda i: (i, 0))
    vec = pl.BlockSpec((1, D), lambda i: (0, 0))
    return pl.pallas_call(
        body, name=name, grid=(S // tm,),
        in_specs=[row, row, vec, vec], out_specs=[row, row],
        out_shape=[jax.ShapeDtypeStruct((S, D), F32), jax.ShapeDtypeStruct((S, D), BF16)],
        compiler_params=_cparams(("parallel",)),
    )(x, y1, g_post, g_pre)


def _swiglu_fwd(h2, w_gu, *, name, tm=512, tn=1408):
    S, D = h2.shape
    Fh = w_gu.shape[1] // 2
    tm, tn = _tile(S, tm), _tile(Fh, tn)
    nj = Fh // tn

    def body(h_ref, wg_ref, wu_ref, g_ref, u_ref, a_ref):
        hv = h_ref[...]
        g = jnp.dot(hv, wg_ref[...], preferred_element_type=F32)
        u = jnp.dot(hv, wu_ref[...], preferred_element_type=F32)
        g_ref[...] = g.astype(g_ref.dtype)
        u_ref[...] = u.astype(u_ref.dtype)
        a_ref[...] = (g * _sigmoid(g) * u).astype(a_ref.dtype)

    out = pl.BlockSpec((tm, tn), lambda j, i: (i, j))
    sds = jax.ShapeDtypeStruct((S, Fh), BF16)
    return pl.pallas_call(
        body, name=name, grid=(nj, S // tm),
        in_specs=[
            pl.BlockSpec((tm, D), lambda j, i: (i, 0)),
            pl.BlockSpec((D, tn), lambda j, i: (0, j)),
            pl.BlockSpec((D, tn), lambda j, i: (0, j + nj)),
        ],
        out_specs=[out, out, out], out_shape=[sds, sds, sds],
        compiler_params=_cparams(("parallel", "parallel"), 56),
    )(h2, w_gu, w_gu)


def _swiglu_bwd(dffn, w_down, g, u, *, name, tm=512, tn=1408):
    S, Fh = g.shape
    D = dffn.shape[1]
    tm, tn = _tile(S, tm), _tile(Fh, tn)

    def body(df_ref, w_ref, g_ref, u_ref, o_ref):
        dfv = df_ref[...]
        for j in range(Fh // tn):
            cols = slice(j * tn, (j + 1) * tn)
            da = lax.dot_general(dfv, w_ref[cols, :], NT_DIMS, preferred_element_type=F32)
            gv = g_ref[:, cols].astype(F32)
            sg = _sigmoid(gv)
            o_ref[:, cols] = (da * u_ref[:, cols].astype(F32) * (sg * (1.0 + gv * (1.0 - sg)))).astype(o_ref.dtype)
            o_ref[:, Fh + j * tn:Fh + (j + 1) * tn] = (da * gv * sg).astype(o_ref.dtype)

    act = pl.BlockSpec((tm, Fh), lambda i: (i, 0))
    return pl.pallas_call(
        body, name=name, grid=(S // tm,),
        in_specs=[pl.BlockSpec((tm, D), lambda i: (i, 0)), pl.BlockSpec((Fh, D), lambda i: (0, 0)), act, act],
        out_specs=pl.BlockSpec((tm, 2 * Fh), lambda i: (i, 0)),
        out_shape=jax.ShapeDtypeStruct((S, 2 * Fh), BF16),
        compiler_params=_cparams(("parallel",), 56),
    )(dffn, w_down, g, u)


def _final(x1, ffn, target, g, *, name, tm=512):
    S, D = x1.shape
    tm = _tile(S, tm)

    def body(x_ref, f_ref, t_ref, g_ref, dout_ref, dffn_ref, loss_ref, dg_ref):
        i = pl.program_id(0)
        fv = f_ref[...]
        rstd = lax.rsqrt(jnp.mean(fv * fv, axis=-1, keepdims=True) + RMS_EPS)
        r = fv * rstd
        e = x_ref[...] + r * g_ref[...] - t_ref[...]
        dout = e * (1.0 / D)
        dout_ref[...] = dout
        gy = dout * g_ref[...]
        dffn_ref[...] = (rstd * (gy - r * jnp.mean(gy * r, axis=-1, keepdims=True))).astype(dffn_ref.dtype)
        lpart = jnp.full((8, LANES), 0.5 * jnp.sum(jnp.mean(e * e, axis=-1, keepdims=True)), F32)
        gpart = jnp.sum(dout * r, axis=0, keepdims=True)

        @pl.when(i == 0)
        def _():
            loss_ref[...] = lpart
            dg_ref[...] = gpart

        @pl.when(i > 0)
        def _():
            loss_ref[...] += lpart
            dg_ref[...] += gpart

    row = pl.BlockSpec((tm, D), lambda i: (i, 0))
    vec = pl.BlockSpec((1, D), lambda i: (0, 0))
    return pl.pallas_call(
        body, name=name, grid=(S // tm,),
        in_specs=[row, row, row, vec],
        out_specs=[row, row, pl.BlockSpec((8, LANES), lambda i: (0, 0)), vec],
        out_shape=[jax.ShapeDtypeStruct((S, D), F32), jax.ShapeDtypeStruct((S, D), BF16),
                   jax.ShapeDtypeStruct((8, LANES), F32), jax.ShapeDtypeStruct((1, D), F32)],
        compiler_params=_cparams(("arbitrary",)),
    )(x1, ffn, target, g)


def _after(token, value):
    return value if token is None else value + token[0, 0].astype(value.dtype)


def _local_step(x, mem, target, gains, w, late_weights, send_grads, *, tq=512):
    S = x.shape[0]
    b_pad = jnp.pad(gains["b_forget"], ((0, 0), (0, LANES - FOX_HEADS)))
    gains = dict(gains)

    h = _rms_fwd(x, gains["norm_mix_pre"], name="rms_mix_pre")
    P = _matmul(h, w["wp"], mode="nn", out_dtype=BF16, name="proj_in")
    f_logit = _matmul(h, w["wf"], mode="nn", out_dtype=F32, name="proj_forget")
    logf = _forget_fwd(f_logit, b_pad, name="forget_fwd")
    c_row16 = _cumsum_lanes([logf[:, :FOX_HEADS].T], reverse=False, name="forget_cumsum")
    ccol = c_row16[:, :, None]
    crow = c_row16.reshape(HEAD_PAIRS, 2, S)
    o_attn, o_attn32, lse = _attn_fwd(P, ccol, crow, tq=tq, name="attn_fwd")
    w = dict(w, **late_weights(o_attn))
    o_conv, y_conv = _conv_fwd(P, w["conv_w"], gains["conv_b"], gains["conv_ln_g"], gains["conv_ln_b"], name="conv_fwd")
    mem_n = _rms_fwd(mem, gains["norm_mem"], name="rms_mem")
    kv = _matmul(mem_n, w["w_kv"], mode="nn", out_dtype=BF16, name="mem_kv")
    o_mem = _mem_fwd(P, kv, name="mem_fwd")
    merged, y1, pb = _merge_fwd(o_attn, o_conv, o_mem, P, w["wpa"], w["wpc"], w["wpm"], w["w_out"], name="merge_fwd")
    x1, h2 = _resid_norm(x, y1, gains["norm_mix_post"], gains["norm_ffn_pre"], name="resid_norm")
    g_ffn, u_ffn, act = _swiglu_fwd(h2, w["w_gu"], name="swiglu_fwd")
    ffn = _matmul(act, w["w_down"], mode="nn", out_dtype=F32, name="ffn_down", tk=1408)

    dout, dffn, loss_tile, d_norm_ffn_post = _final(x1, ffn, target, gains["norm_ffn_post"], name="loss_head")
    d_w_down = _matmul(act, dffn, mode="tn", out_dtype=BF16, name="dw_down", tm=1408)
    dgu = _swiglu_bwd(dffn, w["w_down"], g_ffn, u_ffn, name="swiglu_bwd")
    dh2 = _matmul(dgu, w["w_gu"], mode="nt", out_dtype=F32, name="d_h2", tk=1408)
    d_w_gu = _matmul(h2, dgu, mode="tn", out_dtype=BF16, name="dw_gate_up", tn=1408)
    dx1, d_norm_ffn_pre = _rms_bwd(x1, [dh2], gains["norm_ffn_pre"], dout, out_dtype=F32, name="rms_ffn_pre_bwd")

    dy1, d_norm_mix_post = _rms_bwd(y1, [dx1], gains["norm_mix_post"], None, out_dtype=BF16, name="rms_mix_post_bwd")
    dmerged = _matmul(dy1, w["w_out"], mode="nt", out_dtype=F32, name="d_merged")
    d_w_out = _matmul(merged, dy1, mode="tn", out_dtype=BF16, name="dw_out")
    dpb, dgl = _merge_bwd(dmerged, P, pb, name="merge_bwd")
    do_attn = _matmul(dpb[0], w["wpa"], mode="nt", out_dtype=BF16, name="d_o_attn")
    do_conv = _matmul(dpb[1], w["wpc"], mode="nt", out_dtype=F32, name="d_o_conv")
    do_mem = _matmul(dpb[2], w["wpm"], mode="nt", out_dtype=BF16, name="d_o_mem")
    d_wpa = _matmul(o_attn, dpb[0], mode="tn", out_dtype=BF16, name="dw_proj_attn")
    d_wpc = _matmul(o_conv, dpb[1], mode="tn", out_dtype=BF16, name="dw_proj_conv")
    d_wpm = _matmul(o_mem, dpb[2], mode="tn", out_dtype=BF16, name="dw_proj_mem")

    dq_mem, dkv = _mem_bwd(P, do_mem, kv, name="mem_bwd")
    dkv_b = dkv.astype(BF16)
    d_w_kv = _matmul(mem_n, dkv_b, mode="tn", out_dtype=BF16, name="dw_kv")
    dmem_n = _matmul(dkv_b, w["w_kv"], mode="nt", out_dtype=F32, name="d_mem_n")
    _, d_norm_mem = _rms_bwd(mem, [dmem_n], gains["norm_mem"], None, out_dtype=BF16, name="rms_mem_bwd")

    dglu, d_conv_w, conv_small = _conv_bwd(P, do_conv, y_conv, w["conv_w"], gains["conv_ln_g"], gains["conv_ln_b"], name="conv_bwd")
    sent = send_grads("body", dict(conv_w=d_conv_w[:CONV_WIDTH].astype(BF16), w_kv_mem=d_w_kv, w_proj_attn=d_wpa,
                                   w_proj_conv=d_wpc, w_proj_mem=d_wpm, w_out=d_w_out, w_gate_up=d_w_gu, w_down=d_w_down))

    dq, dk, dv, dc, dcq = _attn_bwd(P, do_attn, o_attn32, lse, ccol, _after(sent, crow), tq=tq, name="attn_bwd")
    dlogf16 = _cumsum_lanes([dc.reshape(FOX_HEADS, S), dcq[:, :, 0]], reverse=True, name="forget_cumsum_bwd")
    dlogf = jnp.pad(dlogf16.T, ((0, 0), (0, LANES - FOX_HEADS)))
    df, d_b_pad = _forget_bwd(f_logit, b_pad, dlogf, name="forget_bwd")

    dP = jnp.concatenate([dq, dk, dv, dglu, dq_mem, dgl], axis=1)
    d_wp = _matmul(h, dP, mode="tn", out_dtype=BF16, name="dw_in")
    d_wf = _matmul(h, df, mode="tn", out_dtype=BF16, name="dw_forget")
    n_qkv = 3 * D_MODEL
    sent = send_grads("w_in", dict(w_in=jnp.concatenate([d_wp[:, :n_qkv], d_wf[:, :FOX_HEADS], d_wp[:, n_qkv:]], axis=1)))
    dh_p = _matmul(dP, _after(sent, w["wp"]), mode="nt", out_dtype=F32, name="d_h")
    dh_f = _matmul(df, w["wf"], mode="nt", out_dtype=F32, name="d_h_forget")
    grad_x, d_norm_mix_pre = _rms_bwd(x, [dh_p, dh_f], gains["norm_mix_pre"], dx1, out_dtype=F32, name="rms_mix_pre_bwd")

    small = dict(norm_mix_pre=d_norm_mix_pre, norm_mix_post=d_norm_mix_post, norm_mem=d_norm_mem,
                 conv_b=conv_small[2:3], conv_ln_g=conv_small[0:1], conv_ln_b=conv_small[1:2],
                 norm_ffn_pre=d_norm_ffn_pre, norm_ffn_post=d_norm_ffn_post, b_forget=d_b_pad[:, :FOX_HEADS])
    return loss_tile, grad_x, small


def _mesh_pos():
    return lax.axis_index("x"), lax.axis_index("y"), lax.axis_index("c")


def _flip(pos, d):
    x, y, c = pos
    return (1 - x if d & 4 else x, 1 - y if d & 2 else y, 1 - c if d & 1 else c)


def _flat(pos):
    x, y, c = pos
    return 4 * x + 2 * y + c


def _exchange(srcs, *, scatter, name):
    T = len(srcs)
    n_peer = N_DEV - 1

    def body(*refs):
        src_refs, out_refs = refs[:T], refs[T:2 * T]
        token, send_sems, recv_sems, local_sems = refs[2 * T:]
        me = _mesh_pos()
        my = _flat(me)

        def copy(t, d):
            peer = _flip(me, d)
            return pltpu.make_async_remote_copy(
                src_ref=src_refs[t].at[_flat(peer)] if scatter else src_refs[t],
                dst_ref=out_refs[t].at[my],
                send_sem=send_sems.at[t * n_peer + d - 1], recv_sem=recv_sems.at[t * n_peer + d - 1],
                device_id=peer, device_id_type=pl.DeviceIdType.MESH)

        def arrival(t, d):
            peer = _flip(me, d)
            return pltpu.make_async_remote_copy(
                src_ref=src_refs[t].at[my] if scatter else src_refs[t],
                dst_ref=out_refs[t].at[_flat(peer)],
                send_sem=send_sems.at[t * n_peer + d - 1], recv_sem=recv_sems.at[t * n_peer + d - 1],
                device_id=peer, device_id_type=pl.DeviceIdType.MESH)

        local = [pltpu.make_async_copy(src_refs[t].at[my] if scatter else src_refs[t], out_refs[t].at[my], local_sems.at[t])
                 for t in range(T)]
        for cp in local:
            cp.start()
        sends = [copy(t, d) for d in range(1, N_DEV) for t in range(T)]
        for cp in sends:
            cp.start()
        for d in range(1, N_DEV):
            for t in range(T):
                arrival(t, d).wait_recv()
        for cp in sends:
            cp.wait_send()
        for cp in local:
            cp.wait()
        token[...] = jnp.zeros_like(token)

    outs = pl.pallas_call(
        body, name=name,
        in_specs=[pl.BlockSpec(memory_space=pl.ANY)] * T,
        out_specs=[pl.BlockSpec(memory_space=pl.ANY)] * T + [pl.BlockSpec(memory_space=pltpu.VMEM)],
        out_shape=[jax.ShapeDtypeStruct((N_DEV,) + tuple(s.shape[-2:]), s.dtype) for s in srcs] + [jax.ShapeDtypeStruct((8, LANES), F32)],
        scratch_shapes=[pltpu.SemaphoreType.DMA((T * n_peer,)), pltpu.SemaphoreType.DMA((T * n_peer,)), pltpu.SemaphoreType.DMA((T,))],
    )(*srcs)
    return list(outs[:T]), outs[T]


HBM_SPEC = pl.BlockSpec(memory_space=pltpu.HBM)
SEM_SPEC = pl.BlockSpec(memory_space=pltpu.SEMAPHORE)


def _split_copies(src_refs, land_refs, send_sems, recv_sems, scatter):
    me = _mesh_pos()
    my = _flat(me)
    n_peer = N_DEV - 1
    out = []
    for d in range(1, N_DEV):
        peer = _flip(me, d)
        for t, (src, land) in enumerate(zip(src_refs, land_refs)):
            k = t * n_peer + d - 1
            start = pltpu.make_async_remote_copy(
                src_ref=src.at[_flat(peer)] if scatter else src, dst_ref=land.at[my],
                send_sem=send_sems.at[k], recv_sem=recv_sems.at[k], device_id=peer, device_id_type=pl.DeviceIdType.MESH)
            arrive = pltpu.make_async_remote_copy(
                src_ref=src.at[my] if scatter else src, dst_ref=land.at[_flat(peer)],
                send_sem=send_sems.at[k], recv_sem=recv_sems.at[k], device_id=peer, device_id_type=pl.DeviceIdType.MESH)
            out.append((start, arrive))
    return out


def _local_copies(src_refs, land_refs, local_sems, scatter):
    my = _flat(_mesh_pos())
    return [pltpu.make_async_copy(src.at[my] if scatter else src, land.at[my], local_sems.at[t])
            for t, (src, land) in enumerate(zip(src_refs, land_refs))]


def _exchange_start(srcs, *, scatter, name):
    T = len(srcs)
    n_sem = T * (N_DEV - 1)
    lands = [lax.empty((N_DEV,) + tuple(s.shape[-2:]), s.dtype) for s in srcs]

    def body(*refs):
        src_refs, land_refs = refs[:T], refs[T:2 * T]
        send_sems, recv_sems, local_sems = refs[2 * T:2 * T + 3]
        token = refs[-1]
        for cp in _local_copies(src_refs, land_refs, local_sems, scatter):
            cp.start()
        for start, _ in _split_copies(src_refs, land_refs, send_sems, recv_sems, scatter):
            start.start()
        token[...] = jnp.zeros_like(token)

    hbm = lambda a: pltpu.HBM(a.shape, a.dtype)
    outs = pl.pallas_call(
        body, name=name,
        in_specs=[HBM_SPEC] * (2 * T),
        out_specs=(SEM_SPEC, SEM_SPEC, SEM_SPEC, *[HBM_SPEC] * (2 * T), pl.BlockSpec(memory_space=pltpu.VMEM)),
        out_shape=(pltpu.SemaphoreType.DMA((n_sem,)), pltpu.SemaphoreType.DMA((n_sem,)), pltpu.SemaphoreType.DMA((T,)),
                   *[hbm(s) for s in srcs], *[hbm(a) for a in lands], jax.ShapeDtypeStruct((8, LANES), F32)),
        input_output_aliases={t: 3 + t for t in range(2 * T)},
        compiler_params=pltpu.CompilerParams(has_side_effects=pltpu.SideEffectType.DATAFLOW_SIDE_EFFECTING),
    )(*[pltpu.with_memory_space_constraint(s, pltpu.HBM) for s in srcs],
      *[pltpu.with_memory_space_constraint(a, pltpu.HBM) for a in lands])
    return (outs[0], outs[1], outs[2], list(outs[3:3 + T]), list(outs[3 + T:3 + 2 * T])), outs[-1]


def _exchange_wait(handle, after, *, scatter, name):
    send_sems, recv_sems, local_sems, srcs, lands = handle
    T = len(srcs)

    def body(*refs):
        src_refs, land_refs = refs[:T], refs[T:2 * T]
        s_sems, r_sems, l_sems = refs[2 * T:2 * T + 3]
        for cp in _local_copies(src_refs, land_refs, l_sems, scatter):
            cp.wait()
        for start, arrive in _split_copies(src_refs, land_refs, s_sems, r_sems, scatter):
            start.wait_send()
            arrive.wait_recv()

    hbm = lambda a: pltpu.HBM(a.shape, a.dtype)
    outs = pl.pallas_call(
        body, name=name,
        in_specs=[HBM_SPEC] * (2 * T) + [SEM_SPEC, SEM_SPEC, SEM_SPEC, pl.BlockSpec(memory_space=pl.ANY)],
        out_specs=[HBM_SPEC] * (2 * T),
        out_shape=[hbm(s) for s in srcs] + [hbm(a) for a in lands],
        input_output_aliases={t: t for t in range(2 * T)},
        compiler_params=pltpu.CompilerParams(has_side_effects=pltpu.SideEffectType.DATAFLOW_SIDE_EFFECTING),
    )(*srcs, *lands, send_sems, recv_sems, local_sems, after)
    return list(outs[T:])


def _adamw(w, slots, m, v, *, name):
    R, C = w.shape
    tr = R if R <= 512 else 256
    assert R % tr == 0, (R, tr)

    def body(w_ref, s_ref, m_ref, v_ref, g_ref, d_ref, m2_ref, v2_ref):
        gv = s_ref[0].astype(F32)
        for j in range(1, N_DEV):
            gv = gv + s_ref[j].astype(F32)
        g_ref[...] = gv
        m2 = ADAM_B1 * m_ref[...] + (1.0 - ADAM_B1) * gv
        v2 = ADAM_B2 * v_ref[...] + (1.0 - ADAM_B2) * (gv * gv)
        m_hat = m2 / (1.0 - ADAM_B1 ** ADAM_STEP)
        v_hat = v2 / (1.0 - ADAM_B2 ** ADAM_STEP)
        d_ref[...] = -ADAM_LR * (m_hat / (jnp.sqrt(v_hat) + ADAM_EPS) + ADAM_WD * w_ref[...])
        m2_ref[...] = m2
        v2_ref[...] = v2

    blk = pl.BlockSpec((tr, C), lambda i: (i, 0))
    sds = jax.ShapeDtypeStruct((R, C), F32)
    return pl.pallas_call(
        body, name=name, grid=(R // tr,),
        in_specs=[blk, pl.BlockSpec((N_DEV, tr, C), lambda i: (0, i, 0)), blk, blk],
        out_specs=[blk] * 4, out_shape=[sds] * 4,
        compiler_params=_cparams(("parallel",)),
    )(w, slots, m, v)


SHARDED = (
    ("w_in", (1024, 1154), "col"), ("conv_w", (31, 128), "col"), ("w_kv_mem", (1024, 256), "col"),
    ("w_proj_attn", (128, 1024), "row"), ("w_proj_conv", (128, 1024), "row"), ("w_proj_mem", (128, 1024), "row"),
    ("w_out", (128, 1024), "row"), ("w_gate_up", (1024, 704), "col"), ("w_down", (352, 1024), "row"),
)
SMALL = ("norm_mix_pre", "norm_mix_post", "norm_mem", "conv_b", "conv_ln_g", "conv_ln_b", "norm_ffn_pre", "norm_ffn_post", "b_forget")
SMALL_ROWS = 16
LOSS_ROW = len(SMALL)
WEIGHT_ORDER = ("norm_mix_pre", "norm_mix_post", "norm_mem", "w_in", "b_forget", "conv_w", "conv_b", "conv_ln_g", "conv_ln_b",
                "w_kv_mem", "w_proj_attn", "w_proj_conv", "w_proj_mem", "w_out", "norm_ffn_pre", "norm_ffn_post", "w_gate_up", "w_down")


def _to_full(blocks8, kind):
    n, r, c = blocks8.shape
    if kind == "col":
        return jnp.concatenate([blocks8[j] for j in range(n)], axis=1)
    return blocks8.reshape(n * r, c)


def _to_blocks(full, kind):
    if kind == "col":
        c = full.shape[1] // N_DEV
        return jnp.stack([full[:, j * c:(j + 1) * c] for j in range(N_DEV)])
    nr, c = full.shape
    return full.reshape(N_DEV, nr // N_DEV, c)


def kernel(x, mem, norm_mix_pre, norm_mix_post, norm_mem, w_in, b_forget, conv_w, conv_b, conv_ln_g, conv_ln_b, w_kv_mem, w_proj_attn, w_proj_conv, w_proj_mem, w_out, norm_ffn_pre, norm_ffn_post, w_gate_up, w_down, loss_target, m_norm_mix_pre, m_norm_mix_post, m_norm_mem, m_w_in, m_b_forget, m_conv_w, m_conv_b, m_conv_ln_g, m_conv_ln_b, m_w_kv_mem, m_w_proj_attn, m_w_proj_conv, m_w_proj_mem, m_w_out, m_norm_ffn_pre, m_norm_ffn_post, m_w_gate_up, m_w_down, v_norm_mix_pre, v_norm_mix_post, v_norm_mem, v_w_in, v_b_forget, v_conv_w, v_conv_b, v_conv_ln_g, v_conv_ln_b, v_w_kv_mem, v_w_proj_attn, v_w_proj_conv, v_w_proj_mem, v_w_out, v_norm_ffn_pre, v_norm_ffn_post, v_w_gate_up, v_w_down):
    given = dict(locals())
    weights = {n: given[n] for n in WEIGHT_ORDER}
    moments_m = {n: given["m_" + n] for n in WEIGHT_ORDER}
    moments_v = {n: given["v_" + n] for n in WEIGHT_ORDER}
    kind = {name: k for name, _, k in SHARDED}
    names = [name for name, _, _ in SHARDED]

    blocks = {n: weights[n][0].astype(BF16) for n in names if n != "conv_w"}
    blocks["conv_w"] = lax.bitcast_convert_type(conv_w[0], BF16).reshape(CONV_WIDTH, 2 * LANES)
    late = [n for n in names if n != "w_in"]
    (w_in_blocks,), w_in_token = _exchange([blocks["w_in"]], scatter=False, name="gather_w_in")
    blocks["w_kv_mem"] = _after(w_in_token, blocks["w_kv_mem"])
    late_handle, late_token = _exchange_start([blocks[n] for n in late], scatter=False, name="gather_rest_start")
    n_qkv = 3 * D_MODEL
    w_in_full = _to_full(w_in_blocks, "col")
    w_early = dict(
        wp=jnp.concatenate([w_in_full[:, :n_qkv], w_in_full[:, n_qkv + FOX_HEADS:]], axis=1),
        wf=jnp.pad(w_in_full[:, n_qkv:n_qkv + FOX_HEADS], ((0, 0), (0, LANES - FOX_HEADS))),
    )

    def late_weights(after):
        got = dict(zip(late, _exchange_wait(late_handle, after, scatter=False, name="gather_rest_wait")))
        conv_w_all = lax.bitcast_convert_type(got["conv_w"].reshape(N_DEV, CONV_WIDTH, LANES, 2), F32)
        full = {n: _to_full(got[n], kind[n]) for n in late if n != "conv_w"}
        return dict(
            conv_w=jnp.pad(_to_full(conv_w_all, "col"), ((0, CONV_HALO - CONV_WIDTH), (0, 0))),
            w_kv=full["w_kv_mem"], wpa=full["w_proj_attn"], wpc=full["w_proj_conv"], wpm=full["w_proj_mem"],
            w_out=full["w_out"], w_gu=full["w_gate_up"], w_down=full["w_down"],
        )

    sent = []

    def send_grads(group, g):
        order = [n for n in names if n in g]
        handle, token = _exchange_start([_to_blocks(g[n], kind[n]) for n in order], scatter=True, name="scatter_" + group + "_start")
        sent.append((group, order, handle))
        return token

    gains = {n: weights[n] for n in SMALL}
    gains["norm_mix_pre"] = _after(late_token, gains["norm_mix_pre"])

    loss_tile, grad_x, small = _local_step(x[0], mem[0], loss_target[0], gains, w_early, late_weights, send_grads)

    g_recv = {}
    for group, order, handle in sent:
        g_recv.update(zip(order, _exchange_wait(handle, grad_x, scatter=True, name="scatter_" + group + "_wait")))

    rows = [jnp.pad(small[n], ((0, 0), (0, D_MODEL - small[n].shape[1]))) for n in SMALL]
    rows.append(jnp.broadcast_to(loss_tile[0:1, 0:1], (1, D_MODEL)))
    rows.append(jnp.zeros((SMALL_ROWS - len(rows), D_MODEL), F32))
    (small_recv,), _ = _exchange([jnp.concatenate(rows, axis=0)], scatter=False, name="gather_small")

    def slab_of(d, fill):
        rows = [jnp.pad(d[n], ((0, 0), (0, D_MODEL - d[n].shape[1])), constant_values=fill) for n in SMALL]
        rows.append(jnp.full((SMALL_ROWS - len(rows), D_MODEL), fill, F32))
        return jnp.concatenate(rows, axis=0)

    grads, delta, new_m, new_v = {}, {}, {}, {}
    sg, sd, sm, sv = _adamw(slab_of(weights, 0.0), small_recv, slab_of(moments_m, 0.0), slab_of(moments_v, 1.0), name="adamw_small")
    loss = sg[LOSS_ROW, 0]
    for i, n in enumerate(SMALL):
        c = weights[n].shape[1]
        grads[n], delta[n], new_m[n], new_v[n] = sg[i:i + 1, :c], sd[i:i + 1, :c], sm[i:i + 1, :c], sv[i:i + 1, :c]
    for n in names:
        gw, d, m2, v2 = _adamw(weights[n][0], g_recv[n], moments_m[n][0], moments_v[n][0], name="adamw_" + n)
        grads[n], delta[n], new_m[n], new_v[n] = gw[None], d[None], m2[None], v2[None]

    return (loss, grad_x[None], *[grads[n] for n in WEIGHT_ORDER], *[delta[n] for n in WEIGHT_ORDER],
            *[new_m[n] for n in WEIGHT_ORDER], *[new_v[n] for n in WEIGHT_ORDER])
```

```python
import functools

import jax
import jax.numpy as jnp
from jax import lax
from jax.experimental import pallas as pl
from jax.experimental.pallas import tpu as pltpu

F32 = jnp.float32
BF16 = jnp.bfloat16

D_MODEL = 1024
N_DEV = 8
FOX_HEADS = 16
FOX_HEAD_DIM = 64
HEAD_PAIRS = FOX_HEADS // 2
MEM_HEADS = 4
MEM_HEAD_DIM = D_MODEL // MEM_HEADS
CONV_WIDTH = 31
CONV_HALO = 32
CONV_CHUNK = 16
SUBLANES = 8
FFN_HIDDEN = 2816
RMS_EPS = 1e-6
LN_EPS = 1e-5
ADAM_LR = 0.001
ADAM_B1 = 0.9
ADAM_B2 = 0.999
ADAM_EPS = 1e-08
ADAM_WD = 0.01
ADAM_STEP = 10
KV_STEP = 2
NEG_BIG = -1e30
LANES = 128

PB_Q, PB_K, PB_V, PB_A, PB_GATE, PB_QMEM, PB_G0 = 0, 1, 2, 3, 4, 5, 6
P_WIDTH = 9 * D_MODEL

NT_DIMS = (((1,), (1,)), ((), ()))
TN_DIMS = (((0,), (0,)), ((), ()))


def _cparams(sem, vmem_mb=None):
    kw = dict(dimension_semantics=sem)
    if vmem_mb is not None:
        kw["vmem_limit_bytes"] = vmem_mb * 1024 * 1024
    return pltpu.CompilerParams(**kw)


def _tile(dim, want):
    t = min(dim, want)
    assert dim % t == 0, (dim, want)
    return t


def _sigmoid(z):
    return 1.0 / (1.0 + jnp.exp(-z))


def _matmul(a, b, *, mode, out_dtype, name, tm=1024, tn=1024, tk=1024, after=None):
    if mode == "nn":
        (M, K), (K2, N) = a.shape, b.shape
    elif mode == "nt":
        (M, K), (N, K2) = a.shape, b.shape
    else:
        (K, M), (K2, N) = a.shape, b.shape
    assert K == K2, (a.shape, b.shape, mode)
    tm, tn, tk = _tile(M, tm), _tile(N, tn), _tile(K, tk)
    nk = K // tk
    dims = {"nn": (((1,), (0,)), ((), ())), "nt": NT_DIMS, "tn": TN_DIMS}[mode]

    n_extra = 0 if after is None else 1

    def body(a_ref, b_ref, *rest):
        o_ref, scratch = rest[n_extra], rest[n_extra + 1:]
        part = lax.dot_general(a_ref[...], b_ref[...], dims, preferred_element_type=F32)
        if nk == 1:
            o_ref[...] = part.astype(o_ref.dtype)
        else:
            acc_ref, = scratch
            k = pl.program_id(2)

            @pl.when(k == 0)
            def _():
                acc_ref[...] = part

            @pl.when(k > 0)
            def _():
                acc_ref[...] += part

            @pl.when(k == nk - 1)
            def _():
                o_ref[...] = acc_ref[...].astype(o_ref.dtype)

    a_spec = pl.BlockSpec((tk, tm), lambda j, i, k: (k, i)) if mode == "tn" else pl.BlockSpec((tm, tk), lambda j, i, k: (i, k))
    b_spec = pl.BlockSpec((tn, tk), lambda j, i, k: (j, k)) if mode == "nt" else pl.BlockSpec((tk, tn), lambda j, i, k: (k, j))
    return pl.pallas_call(
        body,
        name=name,
        grid=(N // tn, M // tm, nk),
        in_specs=[a_spec, b_spec] + [pl.BlockSpec((8, LANES), lambda j, i, k: (0, 0))] * n_extra,
        out_specs=pl.BlockSpec((tm, tn), lambda j, i, k: (i, j)),
        out_shape=jax.ShapeDtypeStruct((M, N), out_dtype),
        scratch_shapes=[pltpu.VMEM((tm, tn), F32)] if nk > 1 else [],
        compiler_params=_cparams(("parallel", "parallel", "arbitrary"), 56),
    )(a, b, *([] if after is None else [after]))


def _rms_fwd(x, g, *, name, tm=512):
    S, D = x.shape
    tm = _tile(S, tm)

    def body(x_ref, g_ref, o_ref):
        xv = x_ref[...]
        rstd = lax.rsqrt(jnp.mean(xv * xv, axis=-1, keepdims=True) + RMS_EPS)
        o_ref[...] = (xv * rstd * g_ref[...]).astype(o_ref.dtype)

    return pl.pallas_call(
        body, name=name, grid=(S // tm,),
        in_specs=[pl.BlockSpec((tm, D), lambda i: (i, 0)), pl.BlockSpec((1, D), lambda i: (0, 0))],
        out_specs=pl.BlockSpec((tm, D), lambda i: (i, 0)),
        out_shape=jax.ShapeDtypeStruct((S, D), BF16),
        compiler_params=_cparams(("parallel",)),
    )(x, g)


def _rms_bwd(xin, dys, g, res, *, out_dtype, name, tm=512):
    S, D = xin.shape
    tm = _tile(S, tm)
    n_dy = len(dys)
    has_res = res is not None

    def body(*refs):
        x_ref, g_ref = refs[0], refs[1]
        dy_refs = refs[2:2 + n_dy]
        pos = 2 + n_dy
        res_ref = refs[pos] if has_res else None
        pos += int(has_res)
        dx_ref, dg_ref = refs[pos], refs[pos + 1]
        i = pl.program_id(0)
        xv = x_ref[...]
        dy = dy_refs[0][...].astype(F32)
        for r in dy_refs[1:]:
            dy = dy + r[...].astype(F32)
        rstd = lax.rsqrt(jnp.mean(xv * xv, axis=-1, keepdims=True) + RMS_EPS)
        xhat = xv * rstd
        gy = dy * g_ref[...]
        dx = rstd * (gy - xhat * jnp.mean(gy * xhat, axis=-1, keepdims=True))
        if has_res:
            dx = dx + res_ref[...]
        dx_ref[...] = dx.astype(dx_ref.dtype)
        part = jnp.sum(dy * xhat, axis=0, keepdims=True)

        @pl.when(i == 0)
        def _():
            dg_ref[...] = part

        @pl.when(i > 0)
        def _():
            dg_ref[...] += part

    row = pl.BlockSpec((tm, D), lambda i: (i, 0))
    vec = pl.BlockSpec((1, D), lambda i: (0, 0))
    ins = [xin, g] + list(dys) + ([res] if has_res else [])
    return pl.pallas_call(
        body, name=name, grid=(S // tm,),
        in_specs=[row, vec] + [row] * n_dy + ([row] if has_res else []),
        out_specs=[row, vec],
        out_shape=[jax.ShapeDtypeStruct((S, D), out_dtype), jax.ShapeDtypeStruct((1, D), F32)],
        compiler_params=_cparams(("arbitrary",)),
    )(*ins)


def _head_mask(hh, shape):
    lane = lax.broadcasted_iota(jnp.int32, shape, len(shape) - 1)
    return (lane // FOX_HEAD_DIM) == hh


def _block_constants(cr_ref, i, tq):
    first = cr_ref[0, :, pl.ds(pl.multiple_of(i * tq, tq), LANES)]
    return [first[hh:hh + 1, 0:1] for hh in range(2)]


def _attn_fwd(P, crow, *, tq, name):
    S = P.shape[0]
    tq = _tile(S, tq)
    nq = S // tq
    scale = FOX_HEAD_DIM ** -0.5

    def body(q_ref, k_ref, v_ref, cr_ref, o_ref, o32_ref, lse_ref):
        i = pl.program_id(1)
        q = q_ref[...] * jnp.asarray(scale, BF16)
        row = lax.broadcasted_iota(jnp.int32, (tq, tq), 0)
        col = lax.broadcasted_iota(jnp.int32, (tq, tq), 1)
        causal = col <= row
        hms = [_head_mask(hh, (tq, LANES)) for hh in range(2)]
        qhs = [jnp.where(hm, q, jnp.zeros_like(q)) for hm in hms]
        cbs = _block_constants(cr_ref, i, tq)

        def step(block, n_blocks, carry, masked):
            tk = n_blocks * tq
            start = pl.multiple_of(block * tq, tk)
            kj = k_ref[pl.ds(start, tk), :]
            vj = v_ref[pl.ds(start, tk), :]
            lane_k = lax.broadcasted_iota(jnp.int32, (tk, LANES), 1)
            new = []
            for hh in range(2):
                m, acc = carry[hh]
                bias = cbs[hh] - cr_ref[0, hh:hh + 1, pl.ds(start, tk)]
                s = lax.dot_general(qhs[hh], kj, NT_DIMS, preferred_element_type=F32) + bias
                if masked:
                    s = jnp.where(causal, s, NEG_BIG)
                m_new = jnp.maximum(m, jnp.max(s, axis=1, keepdims=True))
                alpha = jnp.exp(m - m_new)
                p = jnp.exp(s - m_new)
                vh = jnp.where(lane_k == (1 - hh) * FOX_HEAD_DIM, jnp.ones_like(vj), vj)
                acc = alpha * acc + jnp.dot(p.astype(BF16), vh, preferred_element_type=F32)
                new.append((m_new, acc))
            return tuple(new)

        lane = lax.broadcasted_iota(jnp.int32, (tq, LANES), 1)
        ones_lane = [lane == (1 - hh) * FOX_HEAD_DIM for hh in range(2)]
        init = (jnp.full((tq, 1), NEG_BIG, F32), jnp.zeros((tq, LANES), F32))
        n_wide = i // KV_STEP
        carry = lax.fori_loop(0, n_wide, lambda j, c: step(j * KV_STEP, KV_STEP, c, False), (init, init))
        carry = lax.fori_loop(n_wide * KV_STEP, i, lambda j, c: step(j, 1, c, False), carry)
        carry = step(i, 1, carry, True)
        ls = [jnp.sum(jnp.where(ones_lane[hh], carry[hh][1], 0.0), axis=1, keepdims=True) for hh in range(2)]
        out = jnp.where(hms[0], carry[0][1] / ls[0], carry[1][1] / ls[1])
        for hh in range(2):
            lse_ref[hh] = carry[hh][0] + jnp.log(ls[hh])
        o_ref[...] = out.astype(o_ref.dtype)
        o32_ref[...] = out

    nblk = D_MODEL // LANES
    return pl.pallas_call(
        body, name=name, grid=(HEAD_PAIRS, nq),
        in_specs=[
            pl.BlockSpec((tq, LANES), lambda p, i: (i, PB_Q * nblk + p)),
            pl.BlockSpec((S, LANES), lambda p, i: (0, PB_K * nblk + p)),
            pl.BlockSpec((S, LANES), lambda p, i: (0, PB_V * nblk + p)),
            pl.BlockSpec((1, 2, S), lambda p, i: (p, 0, 0)),
        ],
        out_specs=[
            pl.BlockSpec((tq, LANES), lambda p, i: (i, p)),
            pl.BlockSpec((tq, LANES), lambda p, i: (i, p)),
            pl.BlockSpec((2, tq, 1), lambda p, i: (p, i, 0)),
        ],
        out_shape=[jax.ShapeDtypeStruct((S, D_MODEL), BF16), jax.ShapeDtypeStruct((S, D_MODEL), F32),
                   jax.ShapeDtypeStruct((FOX_HEADS, S, 1), F32)],
        compiler_params=_cparams(("parallel", "arbitrary"), 56),
    )(P, P, P, crow)


def _attn_bwd(P, do, o32, lse, crow, *, tq, name):
    S = P.shape[0]
    tq = _tile(S, tq)
    nq = S // tq
    scale = FOX_HEAD_DIM ** -0.5

    def body(q_ref, k_ref, v_ref, do_ref, o_ref, lse_ref, cr_ref, dq_ref, dk_out, dv_out, dc_ref, dcq_ref, dk_ref, dv_ref):
        i = pl.program_id(1)

        @pl.when(i == 0)
        def _():
            dk_ref[...] = jnp.zeros_like(dk_ref)
            dv_ref[...] = jnp.zeros_like(dv_ref)
            dc_ref[...] = jnp.zeros_like(dc_ref)

        q = q_ref[...] * jnp.asarray(scale, BF16)
        do_v = do_ref[...]
        row = lax.broadcasted_iota(jnp.int32, (tq, tq), 0)
        col = lax.broadcasted_iota(jnp.int32, (tq, tq), 1)
        causal = col <= row
        hms = [_head_mask(hh, (tq, LANES)) for hh in range(2)]
        qhs = [jnp.where(hm, q, jnp.zeros_like(q)) for hm in hms]
        dohs = [jnp.where(hm, do_v, jnp.zeros_like(do_v)) for hm in hms]
        cbs = _block_constants(cr_ref, i, tq)
        lses = [lse_ref[hh] for hh in range(2)]
        prod = do_v.astype(F32) * o_ref[...]
        dls = [jnp.sum(jnp.where(hm, prod, 0.0), axis=1, keepdims=True) for hm in hms]

        def step(block, n_blocks, carry, masked):
            dq_acc, row_sums = carry[0], list(carry[1:])
            tk = n_blocks * tq
            start = pl.multiple_of(block * tq, tk)
            kj = k_ref[pl.ds(start, tk), :]
            vj = v_ref[pl.ds(start, tk), :]
            head_k = [_head_mask(hh, (tk, LANES)) for hh in range(2)]
            dv_part = jnp.zeros((tk, LANES), F32)
            dk_part = jnp.zeros((tk, LANES), F32)
            for hh in range(2):
                bias = cbs[hh] - cr_ref[0, hh:hh + 1, pl.ds(start, tk)]
                s = lax.dot_general(qhs[hh], kj, NT_DIMS, preferred_element_type=F32) + bias
                if masked:
                    s = jnp.where(causal, s, NEG_BIG)
                p = jnp.exp(s - lses[hh])
                dp = lax.dot_general(dohs[hh], vj, NT_DIMS, preferred_element_type=F32)
                ds = p * (dp - dls[hh])
                pb = p.astype(BF16)
                dsb = ds.astype(BF16)
                dv_part = dv_part + lax.dot_general(pb, dohs[hh], TN_DIMS, preferred_element_type=F32)
                dk_part = dk_part + lax.dot_general(dsb, qhs[hh], TN_DIMS, preferred_element_type=F32)
                dc_ref[0, hh:hh + 1, pl.ds(start, tk)] -= jnp.sum(ds, axis=0, keepdims=True)
                row_sums[hh] = row_sums[hh] + jnp.sum(ds, axis=1, keepdims=True)
                kh = jnp.where(head_k[hh], kj, jnp.zeros_like(kj))
                dq_acc = dq_acc + jnp.dot(dsb, kh, preferred_element_type=F32)
            dv_ref[pl.ds(start, tk), :] += dv_part
            dk_ref[pl.ds(start, tk), :] += dk_part
            return (dq_acc, *row_sums)

        zero_col = jnp.zeros((tq, 1), F32)
        n_wide = i // KV_STEP
        carry = lax.fori_loop(0, n_wide, lambda j, c: step(j * KV_STEP, KV_STEP, c, False),
                              (jnp.zeros((tq, LANES), F32), zero_col, zero_col))
        carry = lax.fori_loop(n_wide * KV_STEP, i, lambda j, c: step(j, 1, c, False), carry)
        carry = step(i, 1, carry, True)
        dq_ref[...] = (carry[0] * scale).astype(dq_ref.dtype)
        for hh in range(2):
            dcq_ref[hh] = carry[1 + hh]

        @pl.when(i == nq - 1)
        def _():
            dk_out[...] = dk_ref[...].astype(dk_out.dtype)
            dv_out[...] = dv_ref[...].astype(dv_out.dtype)

    nblk = D_MODEL // LANES
    qblk = pl.BlockSpec((tq, LANES), lambda p, i: (i, p))
    stat = pl.BlockSpec((2, tq, 1), lambda p, i: (p, i, 0))
    full = pl.BlockSpec((S, LANES), lambda p, i: (0, p))
    return pl.pallas_call(
        body, name=name, grid=(HEAD_PAIRS, nq),
        in_specs=[
            pl.BlockSpec((tq, LANES), lambda p, i: (i, PB_Q * nblk + p)),
            pl.BlockSpec((S, LANES), lambda p, i: (0, PB_K * nblk + p)),
            pl.BlockSpec((S, LANES), lambda p, i: (0, PB_V * nblk + p)),
            qblk, qblk, stat,
            pl.BlockSpec((1, 2, S), lambda p, i: (p, 0, 0)),
        ],
        out_specs=[qblk, full, full, pl.BlockSpec((1, 2, S), lambda p, i: (p, 0, 0)), stat],
        out_shape=[
            jax.ShapeDtypeStruct((S, D_MODEL), BF16),
            jax.ShapeDtypeStruct((S, D_MODEL), BF16),
            jax.ShapeDtypeStruct((S, D_MODEL), BF16),
            jax.ShapeDtypeStruct((HEAD_PAIRS, 2, S), F32),
            jax.ShapeDtypeStruct((FOX_HEADS, S, 1), F32),
        ],
        scratch_shapes=[pltpu.VMEM((S, LANES), F32), pltpu.VMEM((S, LANES), F32)],
        compiler_params=_cparams(("parallel", "arbitrary"), 56),
    )(P, P, P, do, o32, lse, crow)


def _cumsum_lanes(xs, *, reverse, name):
    R, S = xs[0].shape
    nb = S // LANES
    n_in = len(xs)

    def body(*refs):
        x_refs, o_ref = refs[:n_in], refs[n_in]
        r = lax.broadcasted_iota(jnp.int32, (LANES, LANES), 0)
        c = lax.broadcasted_iota(jnp.int32, (LANES, LANES), 1)
        tri = ((r >= c) if reverse else (r <= c)).astype(F32)

        def step(b, carry):
            blk = (nb - 1 - b) if reverse else b
            start = pl.multiple_of(blk * LANES, LANES)
            xb = x_refs[0][:, pl.ds(start, LANES)]
            for r in x_refs[1:]:
                xb = xb + r[:, pl.ds(start, LANES)]
            y = jnp.dot(xb, tri, precision=lax.Precision.HIGHEST, preferred_element_type=F32) + carry
            o_ref[:, pl.ds(start, LANES)] = y
            return carry + jnp.sum(xb, axis=1, keepdims=True)

        lax.fori_loop(0, nb, step, jnp.zeros((R, 1), F32))

    return pl.pallas_call(
        body, name=name,
        in_specs=[pl.BlockSpec(memory_space=pltpu.VMEM)] * n_in,
        out_specs=pl.BlockSpec(memory_space=pltpu.VMEM),
        out_shape=jax.ShapeDtypeStruct((R, S), F32),
    )(*xs)


def _forget_fwd(f_logit, b_pad, *, name, tm=1024):
    S = f_logit.shape[0]
    tm = _tile(S, tm)

    def body(f_ref, b_ref, o_ref):
        z = f_ref[...] + b_ref[...]
        o_ref[...] = jnp.minimum(z, 0.0) - jnp.log(1.0 + jnp.exp(-jnp.abs(z)))

    blk = pl.BlockSpec((tm, LANES), lambda i: (i, 0))
    return pl.pallas_call(
        body, name=name, grid=(S // tm,),
        in_specs=[blk, pl.BlockSpec((1, LANES), lambda i: (0, 0))],
        out_specs=blk, out_shape=jax.ShapeDtypeStruct((S, LANES), F32),
        compiler_params=_cparams(("parallel",)),
    )(f_logit, b_pad)


def _forget_bwd(f_logit, b_pad, dlogf, *, name, tm=1024):
    S = f_logit.shape[0]
    tm = _tile(S, tm)

    def body(f_ref, b_ref, d_ref, o_ref, db_ref):
        i = pl.program_id(0)
        z = f_ref[...] + b_ref[...]
        dz = d_ref[...] * (1.0 - _sigmoid(z))
        o_ref[...] = dz.astype(o_ref.dtype)
        part = jnp.sum(dz, axis=0, keepdims=True)

        @pl.when(i == 0)
        def _():
            db_ref[...] = part

        @pl.when(i > 0)
        def _():
            db_ref[...] += part

    blk = pl.BlockSpec((tm, LANES), lambda i: (i, 0))
    vec = pl.BlockSpec((1, LANES), lambda i: (0, 0))
    return pl.pallas_call(
        body, name=name, grid=(S // tm,),
        in_specs=[blk, vec, blk], out_specs=[blk, vec],
        out_shape=[jax.ShapeDtypeStruct((S, LANES), BF16), jax.ShapeDtypeStruct((1, LANES), F32)],
        compiler_params=_cparams(("arbitrary",)),
    )(f_logit, b_pad, dlogf)


def _layernorm_stats(y):
    mu = jnp.mean(y, axis=-1, keepdims=True)
    yc = y - mu
    rstd = lax.rsqrt(jnp.mean(yc * yc, axis=-1, keepdims=True) + LN_EPS)
    return yc * rstd, rstd


def _fill_shifted(src_ref, sh_ref, tm):
    rows = tm + CONV_HALO - SUBLANES
    for s in range(1, SUBLANES):
        sh_ref[s - 1, 0:rows, :] = src_ref[pl.ds(s, rows), :]


def _shifted_rows(src_ref, sh_ref, r0, offset):
    s, base = offset % SUBLANES, offset - offset % SUBLANES
    rows = pl.ds(pl.multiple_of(r0 + base, SUBLANES), CONV_CHUNK)
    return src_ref[rows, :] if s == 0 else sh_ref[s - 1, rows, :]


def _fill_tap_rows(w_ref, wb_ref):
    for k in range(CONV_WIDTH):
        wb_ref[k * SUBLANES:(k + 1) * SUBLANES, :] = jnp.broadcast_to(w_ref[k:k + 1, :], (SUBLANES, w_ref.shape[1]))


def _tap_sum(wb_ref, rows_of_tap):
    n = CONV_CHUNK // SUBLANES
    accs = [None] * n
    for k in range(CONV_WIDTH):
        wk = wb_ref[k * SUBLANES:(k + 1) * SUBLANES, :]
        src = rows_of_tap(k)
        for h in range(n):
            term = wk * src[h * SUBLANES:(h + 1) * SUBLANES]
            accs[h] = term if accs[h] is None else accs[h] + term
    return jnp.concatenate(accs, axis=0)


def _conv_fwd(P, w_pad, conv_b, ln_g, ln_b, *, name, tm=256):
    S = P.shape[0]
    C = D_MODEL
    tm = _tile(S, tm)
    hb = tm // CONV_HALO

    def body(a_ref, gt_ref, ah_ref, gh_ref, w_ref, cb_ref, g_ref, b_ref, o_ref, y_ref, glu_ref, sh_ref, wb_ref):
        i = pl.program_id(0)
        halo = ah_ref[...].astype(F32) * _sigmoid(gh_ref[...].astype(F32))
        glu_ref[0:CONV_HALO, :] = jnp.where(i > 0, halo, 0.0)
        glu_ref[CONV_HALO:, :] = a_ref[...].astype(F32) * _sigmoid(gt_ref[...].astype(F32))
        _fill_shifted(glu_ref, sh_ref, tm)
        _fill_tap_rows(w_ref, wb_ref)

        def chunk(r, carry):
            r0 = pl.multiple_of(r * CONV_CHUNK, CONV_CHUNK)
            acc = _tap_sum(wb_ref, lambda k: _shifted_rows(glu_ref, sh_ref, r0, CONV_HALO - (CONV_WIDTH - 1) + k))
            y_ref[pl.ds(r0, CONV_CHUNK), :] = acc + cb_ref[...]
            return carry

        lax.fori_loop(0, tm // CONV_CHUNK, chunk, 0)
        xhat, _ = _layernorm_stats(y_ref[...])
        z = xhat * g_ref[...] + b_ref[...]
        o_ref[...] = (z * _sigmoid(z)).astype(o_ref.dtype)

    vec = pl.BlockSpec((1, C), lambda i: (0, 0))
    row = pl.BlockSpec((tm, C), lambda i: (i, 0))
    return pl.pallas_call(
        body, name=name, grid=(S // tm,),
        in_specs=[
            pl.BlockSpec((tm, C), lambda i: (i, PB_A)),
            pl.BlockSpec((tm, C), lambda i: (i, PB_GATE)),
            pl.BlockSpec((CONV_HALO, C), lambda i: (jnp.maximum(i * hb - 1, 0), PB_A)),
            pl.BlockSpec((CONV_HALO, C), lambda i: (jnp.maximum(i * hb - 1, 0), PB_GATE)),
            pl.BlockSpec((CONV_HALO, C), lambda i: (0, 0)),
            vec, vec, vec,
        ],
        out_specs=[row, row],
        out_shape=[jax.ShapeDtypeStruct((S, C), BF16), jax.ShapeDtypeStruct((S, C), F32)],
        scratch_shapes=[pltpu.VMEM((tm + CONV_HALO, C), F32), pltpu.VMEM((SUBLANES - 1, tm + CONV_HALO, C), F32),
                        pltpu.VMEM((CONV_HALO * SUBLANES, C), F32)],
        compiler_params=_cparams(("parallel",), 56),
    )(P, P, P, P, w_pad, conv_b, ln_g, ln_b)


def _conv_bwd(P, do, y, w_pad, ln_g, ln_b, *, name, tm=256):
    S = P.shape[0]
    C = D_MODEL
    tm = _tile(S, tm)
    hb = tm // CONV_HALO
    n_tiles = S // tm
    last_halo = S // CONV_HALO - 1

    def body(a_ref, gt_ref, ah_ref, gh_ref, do_ref, y_ref, don_ref, yn_ref, w_ref, g_ref, b_ref,
             dglu_ref, dw_ref, small_ref, glu_ref, dy_ref, gsh_ref, dsh_ref, dwacc_ref, wb_ref):
        i = pl.program_id(0)

        @pl.when(i == 0)
        def _():
            dwacc_ref[...] = jnp.zeros_like(dwacc_ref)
            small_ref[...] = jnp.zeros_like(small_ref)

        def ln_bwd(do_v, y_v):
            xhat, rstd = _layernorm_stats(y_v)
            z = xhat * g_ref[...] + b_ref[...]
            sg = _sigmoid(z)
            dz = do_v * (sg * (1.0 + z * (1.0 - sg)))
            dxh = dz * g_ref[...]
            dy = rstd * (dxh - jnp.mean(dxh, axis=-1, keepdims=True) - xhat * jnp.mean(dxh * xhat, axis=-1, keepdims=True))
            return dy, dz, xhat

        dy, dz, xhat = ln_bwd(do_ref[...], y_ref[...])
        dy_next, _, _ = ln_bwd(don_ref[...], yn_ref[...])
        small_ref[0:1, :] += jnp.sum(dz * xhat, axis=0, keepdims=True)
        small_ref[1:2, :] += jnp.sum(dz, axis=0, keepdims=True)
        small_ref[2:3, :] += jnp.sum(dy, axis=0, keepdims=True)
        dy_ref[0:tm, :] = dy
        dy_ref[tm:, :] = jnp.where(i < n_tiles - 1, dy_next, 0.0)

        halo = ah_ref[...].astype(F32) * _sigmoid(gh_ref[...].astype(F32))
        glu_ref[0:CONV_HALO, :] = jnp.where(i > 0, halo, 0.0)
        glu_ref[CONV_HALO:, :] = a_ref[...].astype(F32) * _sigmoid(gt_ref[...].astype(F32))
        _fill_shifted(glu_ref, gsh_ref, tm)
        _fill_shifted(dy_ref, dsh_ref, tm)
        _fill_tap_rows(w_ref, wb_ref)

        def chunk(r, carry):
            r0 = pl.multiple_of(r * CONV_CHUNK, CONV_CHUNK)
            rows = pl.ds(r0, CONV_CHUNK)
            dyc = dy_ref[rows, :]
            for k in range(CONV_WIDTH):
                prod = dyc * _shifted_rows(glu_ref, gsh_ref, r0, CONV_HALO - (CONV_WIDTH - 1) + k)
                dwacc_ref[k * SUBLANES:(k + 1) * SUBLANES, :] += prod[0:SUBLANES] + prod[SUBLANES:]
            dg = _tap_sum(wb_ref, lambda k: _shifted_rows(dy_ref, dsh_ref, r0, CONV_WIDTH - 1 - k))
            a = a_ref[rows, :].astype(F32)
            sig = _sigmoid(gt_ref[rows, :].astype(F32))
            dglu_ref[rows, 0:C] = (dg * sig).astype(dglu_ref.dtype)
            dglu_ref[rows, C:] = (dg * a * sig * (1.0 - sig)).astype(dglu_ref.dtype)
            return carry

        lax.fori_loop(0, tm // CONV_CHUNK, chunk, 0)

        @pl.when(i == n_tiles - 1)
        def _():
            for k in range(CONV_WIDTH):
                dw_ref[k:k + 1, :] = jnp.sum(dwacc_ref[k * SUBLANES:(k + 1) * SUBLANES, :], axis=0, keepdims=True)
            dw_ref[CONV_WIDTH:, :] = jnp.zeros((CONV_HALO - CONV_WIDTH, C), F32)

    vec = pl.BlockSpec((1, C), lambda i: (0, 0))
    row = pl.BlockSpec((tm, C), lambda i: (i, 0))
    nxt = pl.BlockSpec((CONV_HALO, C), lambda i: (jnp.minimum((i + 1) * hb, last_halo), 0))
    return pl.pallas_call(
        body, name=name, grid=(n_tiles,),
        in_specs=[
            pl.BlockSpec((tm, C), lambda i: (i, PB_A)),
            pl.BlockSpec((tm, C), lambda i: (i, PB_GATE)),
            pl.BlockSpec((CONV_HALO, C), lambda i: (jnp.maximum(i * hb - 1, 0), PB_A)),
            pl.BlockSpec((CONV_HALO, C), lambda i: (jnp.maximum(i * hb - 1, 0), PB_GATE)),
            row, row, nxt, nxt,
            pl.BlockSpec((CONV_HALO, C), lambda i: (0, 0)),
            vec, vec,
        ],
        out_specs=[
            pl.BlockSpec((tm, 2 * C), lambda i: (i, 0)),
            pl.BlockSpec((CONV_HALO, C), lambda i: (0, 0)),
            pl.BlockSpec((8, C), lambda i: (0, 0)),
        ],
        out_shape=[
            jax.ShapeDtypeStruct((S, 2 * C), BF16),
            jax.ShapeDtypeStruct((CONV_HALO, C), F32),
            jax.ShapeDtypeStruct((8, C), F32),
        ],
        scratch_shapes=[
            pltpu.VMEM((tm + CONV_HALO, C), F32), pltpu.VMEM((tm + CONV_HALO, C), F32),
            pltpu.VMEM((SUBLANES - 1, tm + CONV_HALO, C), F32), pltpu.VMEM((SUBLANES - 1, tm + CONV_HALO, C), F32),
            pltpu.VMEM((CONV_HALO * SUBLANES, C), F32), pltpu.VMEM((CONV_HALO * SUBLANES, C), F32),
        ],
        compiler_params=_cparams(("arbitrary",), 56),
    )(P, P, P, P, do, y, do, y, w_pad, ln_g, ln_b)


def _mem_softmax(qh, kh):
    s = lax.dot_general(qh, kh, NT_DIMS, preferred_element_type=F32)
    e = jnp.exp(s - jnp.max(s, axis=1, keepdims=True))
    return e / jnp.sum(e, axis=1, keepdims=True)


def _mem_fwd(P, kv, *, name, tm=512):
    S, M = P.shape[0], kv.shape[0]
    tm = _tile(S, tm)
    scale = MEM_HEAD_DIM ** -0.5

    def body(q_ref, k_ref, v_ref, o_ref):
        for h in range(MEM_HEADS):
            sl = slice(h * MEM_HEAD_DIM, (h + 1) * MEM_HEAD_DIM)
            qh = q_ref[:, sl] * jnp.asarray(scale, BF16)
            p = _mem_softmax(qh, k_ref[:, sl])
            o_ref[:, sl] = jnp.dot(p.astype(BF16), v_ref[:, sl], preferred_element_type=F32).astype(o_ref.dtype)

    return pl.pallas_call(
        body, name=name, grid=(S // tm,),
        in_specs=[
            pl.BlockSpec((tm, D_MODEL), lambda i: (i, PB_QMEM)),
            pl.BlockSpec((M, D_MODEL), lambda i: (0, 0)),
            pl.BlockSpec((M, D_MODEL), lambda i: (0, 1)),
        ],
        out_specs=pl.BlockSpec((tm, D_MODEL), lambda i: (i, 0)),
        out_shape=jax.ShapeDtypeStruct((S, D_MODEL), BF16),
        compiler_params=_cparams(("parallel",)),
    )(P, kv, kv)


def _mem_bwd(P, do, kv, *, name, tm=512):
    S, M = P.shape[0], kv.shape[0]
    tm = _tile(S, tm)
    scale = MEM_HEAD_DIM ** -0.5

    def body(q_ref, k_ref, v_ref, do_ref, dq_ref, dkv_ref):
        i = pl.program_id(0)

        @pl.when(i == 0)
        def _():
            dkv_ref[...] = jnp.zeros_like(dkv_ref)

        for h in range(MEM_HEADS):
            sl = slice(h * MEM_HEAD_DIM, (h + 1) * MEM_HEAD_DIM)
            slv = slice(D_MODEL + h * MEM_HEAD_DIM, D_MODEL + (h + 1) * MEM_HEAD_DIM)
            qh = q_ref[:, sl] * jnp.asarray(scale, BF16)
            kh, vh, doh = k_ref[:, sl], v_ref[:, sl], do_ref[:, sl]
            p = _mem_softmax(qh, kh)
            dp = lax.dot_general(doh, vh, NT_DIMS, preferred_element_type=F32)
            ds = p * (dp - jnp.sum(p * dp, axis=1, keepdims=True))
            dsb = ds.astype(BF16)
            dq_ref[:, sl] = (jnp.dot(dsb, kh, preferred_element_type=F32) * scale).astype(dq_ref.dtype)
            dkv_ref[:, sl] += lax.dot_general(dsb, qh, TN_DIMS, preferred_element_type=F32)
            dkv_ref[:, slv] += lax.dot_general(p.astype(BF16), doh, TN_DIMS, preferred_element_type=F32)

    row = pl.BlockSpec((tm, D_MODEL), lambda i: (i, 0))
    return pl.pallas_call(
        body, name=name, grid=(S // tm,),
        in_specs=[
            pl.BlockSpec((tm, D_MODEL), lambda i: (i, PB_QMEM)),
            pl.BlockSpec((M, D_MODEL), lambda i: (0, 0)),
            pl.BlockSpec((M, D_MODEL), lambda i: (0, 1)),
            row,
        ],
        out_specs=[row, pl.BlockSpec((M, 2 * D_MODEL), lambda i: (0, 0))],
        out_shape=[jax.ShapeDtypeStruct((S, D_MODEL), BF16), jax.ShapeDtypeStruct((M, 2 * D_MODEL), F32)],
        compiler_params=_cparams(("arbitrary",)),
    )(P, kv, kv, do)


def _merge_fwd(o_attn, o_conv, o_mem, P, wpa, wpc, wpm, w_out, *, name, tm=256):
    S = P.shape[0]
    D = D_MODEL
    tm = _tile(S, tm)

    def body(oa_ref, oc_ref, om_ref, g0_ref, g1_ref, g2_ref, wa_ref, wc_ref, wm_ref, wo_ref, mg_ref, y_ref, pb_ref):
        merged = jnp.zeros((tm, D), F32)
        for b, (o_ref, g_ref, w_ref) in enumerate(((oa_ref, g0_ref, wa_ref), (oc_ref, g1_ref, wc_ref), (om_ref, g2_ref, wm_ref))):
            pb = jnp.dot(o_ref[...], w_ref[...], preferred_element_type=F32)
            pb_ref[b] = pb.astype(pb_ref.dtype)
            merged = merged + _sigmoid(g_ref[...].astype(F32)) * pb
        mb = merged.astype(BF16)
        mg_ref[...] = mb
        y_ref[...] = jnp.dot(mb, wo_ref[...], preferred_element_type=F32)

    row = pl.BlockSpec((tm, D), lambda i: (i, 0))
    wsp = pl.BlockSpec((D, D), lambda i: (0, 0))
    return pl.pallas_call(
        body, name=name, grid=(S // tm,),
        in_specs=[row, row, row] + [pl.BlockSpec((tm, D), lambda i, b=b: (i, PB_G0 + b)) for b in range(3)] + [wsp] * 4,
        out_specs=[row, row, pl.BlockSpec((3, tm, D), lambda i: (0, i, 0))],
        out_shape=[jax.ShapeDtypeStruct((S, D), BF16), jax.ShapeDtypeStruct((S, D), F32), jax.ShapeDtypeStruct((3, S, D), BF16)],
        compiler_params=_cparams(("parallel",), 56),
    )(o_attn, o_conv, o_mem, P, P, P, wpa, wpc, wpm, w_out)


def _merge_bwd(dmerged, P, pb, *, name, tm=256):
    S = P.shape[0]
    D = D_MODEL
    tm = _tile(S, tm)

    def body(dm_ref, g0_ref, g1_ref, g2_ref, pb_ref, dpb_ref, dgl_ref):
        dm = dm_ref[...].astype(F32)
        for b, g_ref in enumerate((g0_ref, g1_ref, g2_ref)):
            g = _sigmoid(g_ref[...].astype(F32))
            dpb_ref[b] = (dm * g).astype(dpb_ref.dtype)
            dgl_ref[:, b * D:(b + 1) * D] = (dm * pb_ref[b].astype(F32) * g * (1.0 - g)).astype(dgl_ref.dtype)

    row = pl.BlockSpec((tm, D), lambda i: (i, 0))
    blk3 = pl.BlockSpec((3, tm, D), lambda i: (0, i, 0))
    return pl.pallas_call(
        body, name=name, grid=(S // tm,),
        in_specs=[row] + [pl.BlockSpec((tm, D), lambda i, b=b: (i, PB_G0 + b)) for b in range(3)] + [blk3],
        out_specs=[blk3, pl.BlockSpec((tm, 3 * D), lambda i: (i, 0))],
        out_shape=[jax.ShapeDtypeStruct((3, S, D), BF16), jax.ShapeDtypeStruct((S, 3 * D), BF16)],
        compiler_params=_cparams(("parallel",), 56),
    )(dmerged, P, P, P, pb)


def _resid_norm(x, y1, g_post, g_pre, *, name, tm=512):
    S, D = x.shape
    tm = _tile(S, tm)

    def body(x_ref, y_ref, gp_ref, gq_ref, x1_ref, h2_ref):
        yv = y_ref[...]
        x1 = x_ref[...] + yv * lax.rsqrt(jnp.mean(yv * yv, axis=-1, keepdims=True) + RMS_EPS) * gp_ref[...]
        x1_ref[...] = x1
        h2_ref[...] = (x1 * lax.rsqrt(jnp.mean(x1 * x1, axis=-1, keepdims=True) + RMS_EPS) * gq_ref[...]).astype(h2_ref.dtype)

    row = pl.BlockSpec((tm, D), lambda i: (i, 0))
    vec = pl.BlockSpec((1, D), lambda i: (0, 0))
    return pl.pallas_call(
        body, name=name, grid=(S // tm,),
        in_specs=[row, row, vec, vec], out_specs=[row, row],
        out_shape=[jax.ShapeDtypeStruct((S, D), F32), jax.ShapeDtypeStruct((S, D), BF16)],
        compiler_params=_cparams(("parallel",)),
    )(x, y1, g_post, g_pre)


def _swiglu_fwd(h2, w_gu, *, name, tm=512, tn=1408):
    S, D = h2.shape
    Fh = w_gu.shape[1] // 2
    tm, tn = _tile(S, tm), _tile(Fh, tn)
    nj = Fh // tn

    def body(h_ref, wg_ref, wu_ref, g_ref, u_ref, a_ref):
        hv = h_ref[...]
        g = jnp.dot(hv, wg_ref[...], preferred_element_type=F32)
        u = jnp.dot(hv, wu_ref[...], preferred_element_type=F32)
        g_ref[...] = g.astype(g_ref.dtype)
        u_ref[...] = u.astype(u_ref.dtype)
        a_ref[...] = (g * _sigmoid(g) * u).astype(a_ref.dtype)

    out = pl.BlockSpec((tm, tn), lambda j, i: (i, j))
    sds = jax.ShapeDtypeStruct((S, Fh), BF16)
    return pl.pallas_call(
        body, name=name, grid=(nj, S // tm),
        in_specs=[
            pl.BlockSpec((tm, D), lambda j, i: (i, 0)),
            pl.BlockSpec((D, tn), lambda j, i: (0, j)),
            pl.BlockSpec((D, tn), lambda j, i: (0, j + nj)),
        ],
        out_specs=[out, out, out], out_shape=[sds, sds, sds],
        compiler_params=_cparams(("parallel", "parallel"), 56),
    )(h2, w_gu, w_gu)


def _swiglu_bwd(dffn, w_down, g, u, *, name, tm=512, tn=1408):
    S, Fh = g.shape
    D = dffn.shape[1]
    tm, tn = _tile(S, tm), _tile(Fh, tn)

    def body(df_ref, w_ref, g_ref, u_ref, o_ref):
        dfv = df_ref[...]
        for j in range(Fh // tn):
            cols = slice(j * tn, (j + 1) * tn)
            da = lax.dot_general(dfv, w_ref[cols, :], NT_DIMS, preferred_element_type=F32)
            gv = g_ref[:, cols].astype(F32)
            sg = _sigmoid(gv)
            o_ref[:, cols] = (da * u_ref[:, cols].astype(F32) * (sg * (1.0 + gv * (1.0 - sg)))).astype(o_ref.dtype)
            o_ref[:, Fh + j * tn:Fh + (j + 1) * tn] = (da * gv * sg).astype(o_ref.dtype)

    act = pl.BlockSpec((tm, Fh), lambda i: (i, 0))
    return pl.pallas_call(
        body, name=name, grid=(S // tm,),
        in_specs=[pl.BlockSpec((tm, D), lambda i: (i, 0)), pl.BlockSpec((Fh, D), lambda i: (0, 0)), act, act],
        out_specs=pl.BlockSpec((tm, 2 * Fh), lambda i: (i, 0)),
        out_shape=jax.ShapeDtypeStruct((S, 2 * Fh), BF16),
        compiler_params=_cparams(("parallel",), 56),
    )(dffn, w_down, g, u)


def _final(x1, ffn, target, g, *, name, tm=512):
    S, D = x1.shape
    tm = _tile(S, tm)

    def body(x_ref, f_ref, t_ref, g_ref, dout_ref, dffn_ref, loss_ref, dg_ref):
        i = pl.program_id(0)
        fv = f_ref[...]
        rstd = lax.rsqrt(jnp.mean(fv * fv, axis=-1, keepdims=True) + RMS_EPS)
        r = fv * rstd
        e = x_ref[...] + r * g_ref[...] - t_ref[...]
        dout = e * (1.0 / D)
        dout_ref[...] = dout
        gy = dout * g_ref[...]
        dffn_ref[...] = (rstd * (gy - r * jnp.mean(gy * r, axis=-1, keepdims=True))).astype(dffn_ref.dtype)
        lpart = jnp.full((8, LANES), 0.5 * jnp.sum(jnp.mean(e * e, axis=-1, keepdims=True)), F32)
        gpart = jnp.sum(dout * r, axis=0, keepdims=True)

        @pl.when(i == 0)
        def _():
            loss_ref[...] = lpart
            dg_ref[...] = gpart

        @pl.when(i > 0)
        def _():
            loss_ref[...] += lpart
            dg_ref[...] += gpart

    row = pl.BlockSpec((tm, D), lambda i: (i, 0))
    vec = pl.BlockSpec((1, D), lambda i: (0, 0))
    return pl.pallas_call(
        body, name=name, grid=(S // tm,),
        in_specs=[row, row, row, vec],
        out_specs=[row, row, pl.BlockSpec((8, LANES), lambda i: (0, 0)), vec],
        out_shape=[jax.ShapeDtypeStruct((S, D), F32), jax.ShapeDtypeStruct((S, D), BF16),
                   jax.ShapeDtypeStruct((8, LANES), F32), jax.ShapeDtypeStruct((1, D), F32)],
        compiler_params=_cparams(("arbitrary",)),
    )(x1, ffn, target, g)


def _after(token, value):
    return value if token is None else value + token[0, 0].astype(value.dtype)


def _local_step(x, mem, target, gains, w, late_weights, send_grads, *, tq=512):
    S = x.shape[0]
    b_pad = jnp.pad(gains["b_forget"], ((0, 0), (0, LANES - FOX_HEADS)))
    gains = dict(gains)

    h = _rms_fwd(x, gains["norm_mix_pre"], name="rms_mix_pre")
    P = _matmul(h, w["wp"], mode="nn", out_dtype=BF16, name="proj_in")
    f_logit = _matmul(h, w["wf"], mode="nn", out_dtype=F32, name="proj_forget")
    logf = _forget_fwd(f_logit, b_pad, name="forget_fwd")
    c_row16 = _cumsum_lanes([logf[:, :FOX_HEADS].T], reverse=False, name="forget_cumsum")
    crow = c_row16.reshape(HEAD_PAIRS, 2, S)
    o_attn, o_attn32, lse = _attn_fwd(P, crow, tq=tq, name="attn_fwd")
    w = dict(w, **late_weights(o_attn))
    o_conv, y_conv = _conv_fwd(P, w["conv_w"], gains["conv_b"], gains["conv_ln_g"], gains["conv_ln_b"], name="conv_fwd")
    mem_n = _rms_fwd(mem, gains["norm_mem"], name="rms_mem")
    kv = _matmul(mem_n, w["w_kv"], mode="nn", out_dtype=BF16, name="mem_kv")
    o_mem = _mem_fwd(P, kv, name="mem_fwd")
    merged, y1, pb = _merge_fwd(o_attn, o_conv, o_mem, P, w["wpa"], w["wpc"], w["wpm"], w["w_out"], name="merge_fwd")
    x1, h2 = _resid_norm(x, y1, gains["norm_mix_post"], gains["norm_ffn_pre"], name="resid_norm")
    g_ffn, u_ffn, act = _swiglu_fwd(h2, w["w_gu"], name="swiglu_fwd")
    ffn = _matmul(act, w["w_down"], mode="nn", out_dtype=F32, name="ffn_down", tk=1408)

    dout, dffn, loss_tile, d_norm_ffn_post = _final(x1, ffn, target, gains["norm_ffn_post"], name="loss_head")
    d_w_down = _matmul(act, dffn, mode="tn", out_dtype=BF16, name="dw_down", tm=1408)
    dgu = _swiglu_bwd(dffn, w["w_down"], g_ffn, u_ffn, name="swiglu_bwd")
    dh2 = _matmul(dgu, w["w_gu"], mode="nt", out_dtype=F32, name="d_h2", tk=1408)
    d_w_gu = _matmul(h2, dgu, mode="tn", out_dtype=BF16, name="dw_gate_up", tn=1408)
    dx1, d_norm_ffn_pre = _rms_bwd(x1, [dh2], gains["norm_ffn_pre"], dout, out_dtype=F32, name="rms_ffn_pre_bwd")

    dy1, d_norm_mix_post = _rms_bwd(y1, [dx1], gains["norm_mix_post"], None, out_dtype=BF16, name="rms_mix_post_bwd")
    dmerged = _matmul(dy1, w["w_out"], mode="nt", out_dtype=F32, name="d_merged")
    d_w_out = _matmul(merged, dy1, mode="tn", out_dtype=BF16, name="dw_out")
    dpb, dgl = _merge_bwd(dmerged, P, pb, name="merge_bwd")
    do_attn = _matmul(dpb[0], w["wpa"], mode="nt", out_dtype=BF16, name="d_o_attn")
    do_conv = _matmul(dpb[1], w["wpc"], mode="nt", out_dtype=F32, name="d_o_conv")
    do_mem = _matmul(dpb[2], w["wpm"], mode="nt", out_dtype=BF16, name="d_o_mem")
    d_wpa = _matmul(o_attn, dpb[0], mode="tn", out_dtype=BF16, name="dw_proj_attn")
    d_wpc = _matmul(o_conv, dpb[1], mode="tn", out_dtype=BF16, name="dw_proj_conv")
    d_wpm = _matmul(o_mem, dpb[2], mode="tn", out_dtype=BF16, name="dw_proj_mem")

    dq_mem, dkv = _mem_bwd(P, do_mem, kv, name="mem_bwd")
    dkv_b = dkv.astype(BF16)
    d_w_kv = _matmul(mem_n, dkv_b, mode="tn", out_dtype=BF16, name="dw_kv")
    dmem_n = _matmul(dkv_b, w["w_kv"], mode="nt", out_dtype=F32, name="d_mem_n")
    _, d_norm_mem = _rms_bwd(mem, [dmem_n], gains["norm_mem"], None, out_dtype=BF16, name="rms_mem_bwd")

    dglu, d_conv_w, conv_small = _conv_bwd(P, do_conv, y_conv, w["conv_w"], gains["conv_ln_g"], gains["conv_ln_b"], name="conv_bwd")
    sent = send_grads("body", dict(conv_w=d_conv_w[:CONV_WIDTH].astype(BF16), w_kv_mem=d_w_kv, w_proj_attn=d_wpa,
                                   w_proj_conv=d_wpc, w_proj_mem=d_wpm, w_out=d_w_out, w_gate_up=d_w_gu, w_down=d_w_down))

    dq, dk, dv, dc, dcq = _attn_bwd(P, do_attn, o_attn32, lse, _after(sent, crow), tq=tq, name="attn_bwd")
    dlogf16 = _cumsum_lanes([dc.reshape(FOX_HEADS, S), dcq[:, :, 0]], reverse=True, name="forget_cumsum_bwd")
    dlogf = jnp.pad(dlogf16.T, ((0, 0), (0, LANES - FOX_HEADS)))
    df, d_b_pad = _forget_bwd(f_logit, b_pad, dlogf, name="forget_bwd")

    dP = jnp.concatenate([dq, dk, dv, dglu, dq_mem, dgl], axis=1)
    d_wp = _matmul(h, dP, mode="tn", out_dtype=BF16, name="dw_in")
    d_wf = _matmul(h, df, mode="tn", out_dtype=BF16, name="dw_forget")
    n_qkv = 3 * D_MODEL
    sent = send_grads("w_in", dict(w_in=jnp.concatenate([d_wp[:, :n_qkv], d_wf[:, :FOX_HEADS], d_wp[:, n_qkv:]], axis=1)))
    dh_p = _matmul(dP, w["wp"], mode="nt", out_dtype=F32, name="d_h", after=sent)
    dh_f = _matmul(df, w["wf"], mode="nt", out_dtype=F32, name="d_h_forget")
    grad_x, d_norm_mix_pre = _rms_bwd(x, [dh_p, dh_f], gains["norm_mix_pre"], dx1, out_dtype=F32, name="rms_mix_pre_bwd")

    small = dict(norm_mix_pre=d_norm_mix_pre, norm_mix_post=d_norm_mix_post, norm_mem=d_norm_mem,
                 conv_b=conv_small[2:3], conv_ln_g=conv_small[0:1], conv_ln_b=conv_small[1:2],
                 norm_ffn_pre=d_norm_ffn_pre, norm_ffn_post=d_norm_ffn_post, b_forget=d_b_pad[:, :FOX_HEADS])
    return loss_tile, grad_x, small


def _mesh_pos():
    return lax.axis_index("x"), lax.axis_index("y"), lax.axis_index("c")


def _flip(pos, d):
    x, y, c = pos
    return (1 - x if d & 4 else x, 1 - y if d & 2 else y, 1 - c if d & 1 else c)


def _flat(pos):
    x, y, c = pos
    return 4 * x + 2 * y + c


def _exchange(srcs, *, scatter, name):
    T = len(srcs)
    n_peer = N_DEV - 1

    def body(*refs):
        src_refs, out_refs = refs[:T], refs[T:2 * T]
        token, send_sems, recv_sems, local_sems = refs[2 * T:]
        me = _mesh_pos()
        my = _flat(me)

        def copy(t, d):
            peer = _flip(me, d)
            return pltpu.make_async_remote_copy(
                src_ref=src_refs[t].at[_flat(peer)] if scatter else src_refs[t],
                dst_ref=out_refs[t].at[my],
                send_sem=send_sems.at[t * n_peer + d - 1], recv_sem=recv_sems.at[t * n_peer + d - 1],
                device_id=peer, device_id_type=pl.DeviceIdType.MESH)

        def arrival(t, d):
            peer = _flip(me, d)
            return pltpu.make_async_remote_copy(
                src_ref=src_refs[t].at[my] if scatter else src_refs[t],
                dst_ref=out_refs[t].at[_flat(peer)],
                send_sem=send_sems.at[t * n_peer + d - 1], recv_sem=recv_sems.at[t * n_peer + d - 1],
                device_id=peer, device_id_type=pl.DeviceIdType.MESH)

        local = [pltpu.make_async_copy(src_refs[t].at[my] if scatter else src_refs[t], out_refs[t].at[my], local_sems.at[t])
                 for t in range(T)]
        for cp in local:
            cp.start()
        sends = [copy(t, d) for d in range(1, N_DEV) for t in range(T)]
        for cp in sends:
            cp.start()
        for d in range(1, N_DEV):
            for t in range(T):
                arrival(t, d).wait_recv()
        for cp in sends:
            cp.wait_send()
        for cp in local:
            cp.wait()
        token[...] = jnp.zeros_like(token)

    outs = pl.pallas_call(
        body, name=name,
        in_specs=[pl.BlockSpec(memory_space=pl.ANY)] * T,
        out_specs=[pl.BlockSpec(memory_space=pl.ANY)] * T + [pl.BlockSpec(memory_space=pltpu.VMEM)],
        out_shape=[jax.ShapeDtypeStruct((N_DEV,) + tuple(s.shape[-2:]), s.dtype) for s in srcs] + [jax.ShapeDtypeStruct((8, LANES), F32)],
        scratch_shapes=[pltpu.SemaphoreType.DMA((T * n_peer,)), pltpu.SemaphoreType.DMA((T * n_peer,)), pltpu.SemaphoreType.DMA((T,))],
    )(*srcs)
    return list(outs[:T]), outs[T]


HBM_SPEC = pl.BlockSpec(memory_space=pltpu.HBM)
SEM_SPEC = pl.BlockSpec(memory_space=pltpu.SEMAPHORE)


def _split_copies(src_refs, land_refs, send_sems, recv_sems, scatter):
    me = _mesh_pos()
    my = _flat(me)
    n_peer = N_DEV - 1
    out = []
    for d in range(1, N_DEV):
        peer = _flip(me, d)
        for t, (src, land) in enumerate(zip(src_refs, land_refs)):
            k = t * n_peer + d - 1
            start = pltpu.make_async_remote_copy(
                src_ref=src.at[_flat(peer)] if scatter else src, dst_ref=land.at[my],
                send_sem=send_sems.at[k], recv_sem=recv_sems.at[k], device_id=peer, device_id_type=pl.DeviceIdType.MESH)
            arrive = pltpu.make_async_remote_copy(
                src_ref=src.at[my] if scatter else src, dst_ref=land.at[_flat(peer)],
                send_sem=send_sems.at[k], recv_sem=recv_sems.at[k], device_id=peer, device_id_type=pl.DeviceIdType.MESH)
            out.append((start, arrive))
    return out


def _local_copies(src_refs, land_refs, local_sems, scatter):
    my = _flat(_mesh_pos())
    return [pltpu.make_async_copy(src.at[my] if scatter else src, land.at[my], local_sems.at[t])
            for t, (src, land) in enumerate(zip(src_refs, land_refs))]


def _exchange_start(srcs, *, scatter, name):
    T = len(srcs)
    n_sem = T * (N_DEV - 1)
    lands = [lax.empty((N_DEV,) + tuple(s.shape[-2:]), s.dtype) for s in srcs]

    def body(*refs):
        src_refs, land_refs = refs[:T], refs[T:2 * T]
        send_sems, recv_sems, local_sems = refs[2 * T:2 * T + 3]
        token = refs[-1]
        for cp in _local_copies(src_refs, land_refs, local_sems, scatter):
            cp.start()
        for start, _ in _split_copies(src_refs, land_refs, send_sems, recv_sems, scatter):
            start.start()
        token[...] = jnp.zeros_like(token)

    hbm = lambda a: pltpu.HBM(a.shape, a.dtype)
    outs = pl.pallas_call(
        body, name=name,
        in_specs=[HBM_SPEC] * (2 * T),
        out_specs=(SEM_SPEC, SEM_SPEC, SEM_SPEC, *[HBM_SPEC] * (2 * T), pl.BlockSpec(memory_space=pltpu.VMEM)),
        out_shape=(pltpu.SemaphoreType.DMA((n_sem,)), pltpu.SemaphoreType.DMA((n_sem,)), pltpu.SemaphoreType.DMA((T,)),
                   *[hbm(s) for s in srcs], *[hbm(a) for a in lands], jax.ShapeDtypeStruct((8, LANES), F32)),
        input_output_aliases={t: 3 + t for t in range(2 * T)},
        compiler_params=pltpu.CompilerParams(has_side_effects=pltpu.SideEffectType.DATAFLOW_SIDE_EFFECTING),
    )(*[pltpu.with_memory_space_constraint(s, pltpu.HBM) for s in srcs],
      *[pltpu.with_memory_space_constraint(a, pltpu.HBM) for a in lands])
    return (outs[0], outs[1], outs[2], list(outs[3:3 + T]), list(outs[3 + T:3 + 2 * T])), outs[-1]


def _exchange_wait(handle, after, *, scatter, name):
    send_sems, recv_sems, local_sems, srcs, lands = handle
    T = len(srcs)

    def body(*refs):
        src_refs, land_refs = refs[:T], refs[T:2 * T]
        s_sems, r_sems, l_sems = refs[2 * T:2 * T + 3]
        for cp in _local_copies(src_refs, land_refs, l_sems, scatter):
            cp.wait()
        for start, arrive in _split_copies(src_refs, land_refs, s_sems, r_sems, scatter):
            start.wait_send()
            arrive.wait_recv()

    hbm = lambda a: pltpu.HBM(a.shape, a.dtype)
    outs = pl.pallas_call(
        body, name=name,
        in_specs=[HBM_SPEC] * (2 * T) + [SEM_SPEC, SEM_SPEC, SEM_SPEC, pl.BlockSpec(memory_space=pl.ANY)],
        out_specs=[HBM_SPEC] * (2 * T),
        out_shape=[hbm(s) for s in srcs] + [hbm(a) for a in lands],
        input_output_aliases={t: t for t in range(2 * T)},
        compiler_params=pltpu.CompilerParams(has_side_effects=pltpu.SideEffectType.DATAFLOW_SIDE_EFFECTING),
    )(*srcs, *lands, send_sems, recv_sems, local_sems, after)
    return list(outs[T:])


def _adamw(w, slots, m, v, *, name):
    R, C = w.shape
    tr = R if R <= 512 else 256
    assert R % tr == 0, (R, tr)

    def body(w_ref, s_ref, m_ref, v_ref, g_ref, d_ref, m2_ref, v2_ref):
        gv = s_ref[0].astype(F32)
        for j in range(1, N_DEV):
            gv = gv + s_ref[j].astype(F32)
        g_ref[...] = gv
        m2 = ADAM_B1 * m_ref[...] + (1.0 - ADAM_B1) * gv
        v2 = ADAM_B2 * v_ref[...] + (1.0 - ADAM_B2) * (gv * gv)
        m_hat = m2 / (1.0 - ADAM_B1 ** ADAM_STEP)
        v_hat = v2 / (1.0 - ADAM_B2 ** ADAM_STEP)
        d_ref[...] = -ADAM_LR * (m_hat / (jnp.sqrt(v_hat) + ADAM_EPS) + ADAM_WD * w_ref[...])
        m2_ref[...] = m2
        v2_ref[...] = v2

    blk = pl.BlockSpec((tr, C), lambda i: (i, 0))
    sds = jax.ShapeDtypeStruct((R, C), F32)
    return pl.pallas_call(
        body, name=name, grid=(R // tr,),
        in_specs=[blk, pl.BlockSpec((N_DEV, tr, C), lambda i: (0, i, 0)), blk, blk],
        out_specs=[blk] * 4, out_shape=[sds] * 4,
        compiler_params=_cparams(("parallel",)),
    )(w, slots, m, v)


SHARDED = (
    ("w_in", (1024, 1154), "col"), ("conv_w", (31, 128), "col"), ("w_kv_mem", (1024, 256), "col"),
    ("w_proj_attn", (128, 1024), "row"), ("w_proj_conv", (128, 1024), "row"), ("w_proj_mem", (128, 1024), "row"),
    ("w_out", (128, 1024), "row"), ("w_gate_up", (1024, 704), "col"), ("w_down", (352, 1024), "row"),
)
SMALL = ("norm_mix_pre", "norm_mix_post", "norm_mem", "conv_b", "conv_ln_g", "conv_ln_b", "norm_ffn_pre", "norm_ffn_post", "b_forget")
SMALL_ROWS = 16
LOSS_ROW = len(SMALL)
WEIGHT_ORDER = ("norm_mix_pre", "norm_mix_post", "norm_mem", "w_in", "b_forget", "conv_w", "conv_b", "conv_ln_g", "conv_ln_b",
                "w_kv_mem", "w_proj_attn", "w_proj_conv", "w_proj_mem", "w_out", "norm_ffn_pre", "norm_ffn_post", "w_gate_up", "w_down")


def _to_full(blocks8, kind):
    n, r, c = blocks8.shape
    if kind == "col":
        return jnp.concatenate([blocks8[j] for j in range(n)], axis=1)
    return blocks8.reshape(n * r, c)


def _to_blocks(full, kind):
    if kind == "col":
        c = full.shape[1] // N_DEV
        return jnp.stack([full[:, j * c:(j + 1) * c] for j in range(N_DEV)])
    nr, c = full.shape
    return full.reshape(N_DEV, nr // N_DEV, c)


def kernel(x, mem, norm_mix_pre, norm_mix_post, norm_mem, w_in, b_forget, conv_w, conv_b, conv_ln_g, conv_ln_b, w_kv_mem, w_proj_attn, w_proj_conv, w_proj_mem, w_out, norm_ffn_pre, norm_ffn_post, w_gate_up, w_down, loss_target, m_norm_mix_pre, m_norm_mix_post, m_norm_mem, m_w_in, m_b_forget, m_conv_w, m_conv_b, m_conv_ln_g, m_conv_ln_b, m_w_kv_mem, m_w_proj_attn, m_w_proj_conv, m_w_proj_mem, m_w_out, m_norm_ffn_pre, m_norm_ffn_post, m_w_gate_up, m_w_down, v_norm_mix_pre, v_norm_mix_post, v_norm_mem, v_w_in, v_b_forget, v_conv_w, v_conv_b, v_conv_ln_g, v_conv_ln_b, v_w_kv_mem, v_w_proj_attn, v_w_proj_conv, v_w_proj_mem, v_w_out, v_norm_ffn_pre, v_norm_ffn_post, v_w_gate_up, v_w_down):
    given = dict(locals())
    weights = {n: given[n] for n in WEIGHT_ORDER}
    moments_m = {n: given["m_" + n] for n in WEIGHT_ORDER}
    moments_v = {n: given["v_" + n] for n in WEIGHT_ORDER}
    kind = {name: k for name, _, k in SHARDED}
    names = [name for name, _, _ in SHARDED]

    blocks = {n: weights[n][0].astype(BF16) for n in names if n != "conv_w"}
    blocks["conv_w"] = lax.bitcast_convert_type(conv_w[0], BF16).reshape(CONV_WIDTH, 2 * LANES)
    late = [n for n in names if n != "w_in"]
    (w_in_blocks,), w_in_token = _exchange([blocks["w_in"]], scatter=False, name="gather_w_in")
    blocks["w_kv_mem"] = _after(w_in_token, blocks["w_kv_mem"])
    late_handle, late_token = _exchange_start([blocks[n] for n in late], scatter=False, name="gather_rest_start")
    n_qkv = 3 * D_MODEL
    w_in_full = _to_full(w_in_blocks, "col")
    w_early = dict(
        wp=jnp.concatenate([w_in_full[:, :n_qkv], w_in_full[:, n_qkv + FOX_HEADS:]], axis=1),
        wf=jnp.pad(w_in_full[:, n_qkv:n_qkv + FOX_HEADS], ((0, 0), (0, LANES - FOX_HEADS))),
    )

    def late_weights(after):
        got = dict(zip(late, _exchange_wait(late_handle, after, scatter=False, name="gather_rest_wait")))
        conv_w_all = lax.bitcast_convert_type(got["conv_w"].reshape(N_DEV, CONV_WIDTH, LANES, 2), F32)
        full = {n: _to_full(got[n], kind[n]) for n in late if n != "conv_w"}
        return dict(
            conv_w=jnp.pad(_to_full(conv_w_all, "col"), ((0, CONV_HALO - CONV_WIDTH), (0, 0))),
            w_kv=full["w_kv_mem"], wpa=full["w_proj_attn"], wpc=full["w_proj_conv"], wpm=full["w_proj_mem"],
            w_out=full["w_out"], w_gu=full["w_gate_up"], w_down=full["w_down"],
        )

    sent = []

    def send_grads(group, g):
        order = [n for n in names if n in g]
        handle, token = _exchange_start([_to_blocks(g[n], kind[n]) for n in order], scatter=True, name="scatter_" + group + "_start")
        sent.append((group, order, handle))
        return token

    gains = {n: weights[n] for n in SMALL}
    gains["norm_mix_pre"] = _after(late_token, gains["norm_mix_pre"])

    loss_tile, grad_x, small = _local_step(x[0], mem[0], loss_target[0], gains, w_early, late_weights, send_grads)

    g_recv = {}
    for group, order, handle in sent:
        g_recv.update(zip(order, _exchange_wait(handle, grad_x, scatter=True, name="scatter_" + group + "_wait")))

    rows = [jnp.pad(small[n], ((0, 0), (0, D_MODEL - small[n].shape[1]))) for n in SMALL]
    rows.append(jnp.broadcast_to(loss_tile[0:1, 0:1], (1, D_MODEL)))
    rows.append(jnp.zeros((SMALL_ROWS - len(rows), D_MODEL), F32))
    (small_recv,), _ = _exchange([jnp.concatenate(rows, axis=0)], scatter=False, name="gather_small")

    def slab_of(d, fill):
        rows = [jnp.pad(d[n], ((0, 0), (0, D_MODEL - d[n].shape[1])), constant_values=fill) for n in SMALL]
        rows.append(jnp.full((SMALL_ROWS - len(rows), D_MODEL), fill, F32))
        return jnp.concatenate(rows, axis=0)

    grads, delta, new_m, new_v = {}, {}, {}, {}
    sg, sd, sm, sv = _adamw(slab_of(weights, 0.0), small_recv, slab_of(moments_m, 0.0), slab_of(moments_v, 1.0), name="adamw_small")
    loss = sg[LOSS_ROW, 0]
    for i, n in enumerate(SMALL):
        c = weights[n].shape[1]
        grads[n], delta[n], new_m[n], new_v[n] = sg[i:i + 1, :c], sd[i:i + 1, :c], sm[i:i + 1, :c], sv[i:i + 1, :c]
    for n in names:
        gw, d, m2, v2 = _adamw(weights[n][0], g_recv[n], moments_m[n][0], moments_v[n][0], name="adamw_" + n)
        grads[n], delta[n], new_m[n], new_v[n] = gw[None], d[None], m2[None], v2[None]

    return (loss, grad_x[None], *[grads[n] for n in WEIGHT_ORDER], *[delta[n] for n in WEIGHT_ORDER],
            *[new_m[n] for n in WEIGHT_ORDER], *[new_v[n] for n in WEIGHT_ORDER])
```

```python
import functools

import jax
import jax.numpy as jnp
from jax import lax
from jax.experimental import pallas as pl
from jax.experimental.pallas import tpu as pltpu

F32 = jnp.float32
BF16 = jnp.bfloat16

D_MODEL = 1024
N_DEV = 8
FOX_HEADS = 16
FOX_HEAD_DIM = 64
HEAD_PAIRS = FOX_HEADS // 2
MEM_HEADS = 4
MEM_HEAD_DIM = D_MODEL // MEM_HEADS
CONV_WIDTH = 31
CONV_HALO = 32
CONV_CHUNK = 16
SUBLANES = 8
FFN_HIDDEN = 2816
RMS_EPS = 1e-6
LN_EPS = 1e-5
ADAM_LR = 0.001
ADAM_B1 = 0.9
ADAM_B2 = 0.999
ADAM_EPS = 1e-08
ADAM_WD = 0.01
ADAM_STEP = 10
KV_STEP = 2
KV_STEP_FWD = 4
NEG_BIG = -1e30
LANES = 128

PB_Q, PB_K, PB_V, PB_A, PB_GATE, PB_QMEM, PB_G0 = 0, 1, 2, 3, 4, 5, 6
P_WIDTH = 9 * D_MODEL

NT_DIMS = (((1,), (1,)), ((), ()))
TN_DIMS = (((0,), (0,)), ((), ()))


def _cparams(sem, vmem_mb=None):
    kw = dict(dimension_semantics=sem)
    if vmem_mb is not None:
        kw["vmem_limit_bytes"] = vmem_mb * 1024 * 1024
    return pltpu.CompilerParams(**kw)


def _tile(dim, want):
    t = min(dim, want)
    assert dim % t == 0, (dim, want)
    return t


def _sigmoid(z):
    return 1.0 / (1.0 + jnp.exp(-z))


def _matmul(a, b, *, mode, out_dtype, name, tm=1024, tn=1024, tk=1024, after=None):
    if mode == "nn":
        (M, K), (K2, N) = a.shape, b.shape
    elif mode == "nt":
        (M, K), (N, K2) = a.shape, b.shape
    else:
        (K, M), (K2, N) = a.shape, b.shape
    assert K == K2, (a.shape, b.shape, mode)
    tm, tn, tk = _tile(M, tm), _tile(N, tn), _tile(K, tk)
    nk = K // tk
    dims = {"nn": (((1,), (0,)), ((), ())), "nt": NT_DIMS, "tn": TN_DIMS}[mode]

    n_extra = 0 if after is None else 1

    def body(a_ref, b_ref, *rest):
        o_ref, scratch = rest[n_extra], rest[n_extra + 1:]
        part = lax.dot_general(a_ref[...], b_ref[...], dims, preferred_element_type=F32)
        if nk == 1:
            o_ref[...] = part.astype(o_ref.dtype)
        else:
            acc_ref, = scratch
            k = pl.program_id(2)

            @pl.when(k == 0)
            def _():
                acc_ref[...] = part

            @pl.when(k > 0)
            def _():
                acc_ref[...] += part

            @pl.when(k == nk - 1)
            def _():
                o_ref[...] = acc_ref[...].astype(o_ref.dtype)

    a_spec = pl.BlockSpec((tk, tm), lambda j, i, k: (k, i)) if mode == "tn" else pl.BlockSpec((tm, tk), lambda j, i, k: (i, k))
    b_spec = pl.BlockSpec((tn, tk), lambda j, i, k: (j, k)) if mode == "nt" else pl.BlockSpec((tk, tn), lambda j, i, k: (k, j))
    return pl.pallas_call(
        body,
        name=name,
        grid=(N // tn, M // tm, nk),
        in_specs=[a_spec, b_spec] + [pl.BlockSpec((8, LANES), lambda j, i, k: (0, 0))] * n_extra,
        out_specs=pl.BlockSpec((tm, tn), lambda j, i, k: (i, j)),
        out_shape=jax.ShapeDtypeStruct((M, N), out_dtype),
        scratch_shapes=[pltpu.VMEM((tm, tn), F32)] if nk > 1 else [],
        compiler_params=_cparams(("parallel", "parallel", "arbitrary"), 56),
    )(a, b, *([] if after is None else [after]))


def _rms_fwd(x, g, *, name, tm=512):
    S, D = x.shape
    tm = _tile(S, tm)

    def body(x_ref, g_ref, o_ref):
        xv = x_ref[...]
        rstd = lax.rsqrt(jnp.mean(xv * xv, axis=-1, keepdims=True) + RMS_EPS)
        o_ref[...] = (xv * rstd * g_ref[...]).astype(o_ref.dtype)

    return pl.pallas_call(
        body, name=name, grid=(S // tm,),
        in_specs=[pl.BlockSpec((tm, D), lambda i: (i, 0)), pl.BlockSpec((1, D), lambda i: (0, 0))],
        out_specs=pl.BlockSpec((tm, D), lambda i: (i, 0)),
        out_shape=jax.ShapeDtypeStruct((S, D), BF16),
        compiler_params=_cparams(("parallel",)),
    )(x, g)


def _rms_bwd(xin, dys, g, res, *, out_dtype, name, tm=512):
    S, D = xin.shape
    tm = _tile(S, tm)
    n_dy = len(dys)
    has_res = res is not None

    def body(*refs):
        x_ref, g_ref = refs[0], refs[1]
        dy_refs = refs[2:2 + n_dy]
        pos = 2 + n_dy
        res_ref = refs[pos] if has_res else None
        pos += int(has_res)
        dx_ref, dg_ref = refs[pos], refs[pos + 1]
        i = pl.program_id(0)
        xv = x_ref[...]
        dy = dy_refs[0][...].astype(F32)
        for r in dy_refs[1:]:
            dy = dy + r[...].astype(F32)
        rstd = lax.rsqrt(jnp.mean(xv * xv, axis=-1, keepdims=True) + RMS_EPS)
        xhat = xv * rstd
        gy = dy * g_ref[...]
        dx = rstd * (gy - xhat * jnp.mean(gy * xhat, axis=-1, keepdims=True))
        if has_res:
            dx = dx + res_ref[...]
        dx_ref[...] = dx.astype(dx_ref.dtype)
        part = jnp.sum(dy * xhat, axis=0, keepdims=True)

        @pl.when(i == 0)
        def _():
            dg_ref[...] = part

        @pl.when(i > 0)
        def _():
            dg_ref[...] += part

    row = pl.BlockSpec((tm, D), lambda i: (i, 0))
    vec = pl.BlockSpec((1, D), lambda i: (0, 0))
    ins = [xin, g] + list(dys) + ([res] if has_res else [])
    return pl.pallas_call(
        body, name=name, grid=(S // tm,),
        in_specs=[row, vec] + [row] * n_dy + ([row] if has_res else []),
        out_specs=[row, vec],
        out_shape=[jax.ShapeDtypeStruct((S, D), out_dtype), jax.ShapeDtypeStruct((1, D), F32)],
        compiler_params=_cparams(("arbitrary",)),
    )(*ins)


def _head_mask(hh, shape):
    lane = lax.broadcasted_iota(jnp.int32, shape, len(shape) - 1)
    return (lane // FOX_HEAD_DIM) == hh


def _block_constants(cr_ref, i, tq):
    first = cr_ref[0, :, pl.ds(pl.multiple_of(i * tq, tq), LANES)]
    return [first[hh:hh + 1, 0:1] for hh in range(2)]


def _attn_fwd(P, crow, *, tq, name):
    S = P.shape[0]
    tq = _tile(S, tq)
    nq = S // tq
    scale = FOX_HEAD_DIM ** -0.5

    def body(q_ref, k_ref, v_ref, cr_ref, o_ref, o32_ref, lse_ref):
        i = pl.program_id(1)
        q = q_ref[...] * jnp.asarray(scale, BF16)
        row = lax.broadcasted_iota(jnp.int32, (tq, tq), 0)
        col = lax.broadcasted_iota(jnp.int32, (tq, tq), 1)
        causal = col <= row
        hms = [_head_mask(hh, (tq, LANES)) for hh in range(2)]
        qhs = [jnp.where(hm, q, jnp.zeros_like(q)) for hm in hms]
        cbs = _block_constants(cr_ref, i, tq)

        def step(block, n_blocks, carry, masked):
            tk = n_blocks * tq
            start = pl.multiple_of(block * tq, tk)
            kj = k_ref[pl.ds(start, tk), :]
            vj = v_ref[pl.ds(start, tk), :]
            lane_k = lax.broadcasted_iota(jnp.int32, (tk, LANES), 1)
            new = []
            for hh in range(2):
                m, acc = carry[hh]
                bias = cbs[hh] - cr_ref[0, hh:hh + 1, pl.ds(start, tk)]
                s = lax.dot_general(qhs[hh], kj, NT_DIMS, preferred_element_type=F32) + bias
                if masked:
                    s = jnp.where(causal, s, NEG_BIG)
                m_new = jnp.maximum(m, jnp.max(s, axis=1, keepdims=True))
                alpha = jnp.exp(m - m_new)
                p = jnp.exp(s - m_new)
                vh = jnp.where(lane_k == (1 - hh) * FOX_HEAD_DIM, jnp.ones_like(vj), vj)
                acc = alpha * acc + jnp.dot(p.astype(BF16), vh, preferred_element_type=F32)
                new.append((m_new, acc))
            return tuple(new)

        lane = lax.broadcasted_iota(jnp.int32, (tq, LANES), 1)
        ones_lane = [lane == (1 - hh) * FOX_HEAD_DIM for hh in range(2)]
        init = (jnp.full((tq, 1), NEG_BIG, F32), jnp.zeros((tq, LANES), F32))
        n_wide = i // KV_STEP_FWD
        carry = lax.fori_loop(0, n_wide, lambda j, c: step(j * KV_STEP_FWD, KV_STEP_FWD, c, False), (init, init))
        carry = lax.fori_loop(n_wide * KV_STEP_FWD, i, lambda j, c: step(j, 1, c, False), carry)
        carry = step(i, 1, carry, True)
        ls = [jnp.sum(jnp.where(ones_lane[hh], carry[hh][1], 0.0), axis=1, keepdims=True) for hh in range(2)]
        out = jnp.where(hms[0], carry[0][1] / ls[0], carry[1][1] / ls[1])
        for hh in range(2):
            lse_ref[hh] = carry[hh][0] + jnp.log(ls[hh])
        o_ref[...] = out.astype(o_ref.dtype)
        o32_ref[...] = out

    nblk = D_MODEL // LANES
    return pl.pallas_call(
        body, name=name, grid=(HEAD_PAIRS, nq),
        in_specs=[
            pl.BlockSpec((tq, LANES), lambda p, i: (i, PB_Q * nblk + p)),
            pl.BlockSpec((S, LANES), lambda p, i: (0, PB_K * nblk + p)),
            pl.BlockSpec((S, LANES), lambda p, i: (0, PB_V * nblk + p)),
            pl.BlockSpec((1, 2, S), lambda p, i: (p, 0, 0)),
        ],
        out_specs=[
            pl.BlockSpec((tq, LANES), lambda p, i: (i, p)),
            pl.BlockSpec((tq, LANES), lambda p, i: (i, p)),
            pl.BlockSpec((2, tq, 1), lambda p, i: (p, i, 0)),
        ],
        out_shape=[jax.ShapeDtypeStruct((S, D_MODEL), BF16), jax.ShapeDtypeStruct((S, D_MODEL), F32),
                   jax.ShapeDtypeStruct((FOX_HEADS, S, 1), F32)],
        compiler_params=_cparams(("parallel", "arbitrary"), 56),
    )(P, P, P, crow)


def _attn_bwd(P, do, o32, lse, crow, *, tq, name):
    S = P.shape[0]
    tq = _tile(S, tq)
    nq = S // tq
    scale = FOX_HEAD_DIM ** -0.5

    def body(q_ref, k_ref, v_ref, do_ref, o_ref, lse_ref, cr_ref, dq_ref, dk_out, dv_out, dc_ref, dcq_ref, dk_ref, dv_ref):
        i = pl.program_id(1)

        @pl.when(i == 0)
        def _():
            dk_ref[...] = jnp.zeros_like(dk_ref)
            dv_ref[...] = jnp.zeros_like(dv_ref)
            dc_ref[...] = jnp.zeros_like(dc_ref)

        q = q_ref[...] * jnp.asarray(scale, BF16)
        do_v = do_ref[...]
        row = lax.broadcasted_iota(jnp.int32, (tq, tq), 0)
        col = lax.broadcasted_iota(jnp.int32, (tq, tq), 1)
        causal = col <= row
        hms = [_head_mask(hh, (tq, LANES)) for hh in range(2)]
        qhs = [jnp.where(hm, q, jnp.zeros_like(q)) for hm in hms]
        dohs = [jnp.where(hm, do_v, jnp.zeros_like(do_v)) for hm in hms]
        cbs = _block_constants(cr_ref, i, tq)
        lses = [lse_ref[hh] for hh in range(2)]
        prod = do_v.astype(F32) * o_ref[...]
        dls = [jnp.sum(jnp.where(hm, prod, 0.0), axis=1, keepdims=True) for hm in hms]

        def step(block, n_blocks, carry, masked):
            dq_acc, row_sums = carry[0], list(carry[1:])
            tk = n_blocks * tq
            start = pl.multiple_of(block * tq, tk)
            kj = k_ref[pl.ds(start, tk), :]
            vj = v_ref[pl.ds(start, tk), :]
            head_k = [_head_mask(hh, (tk, LANES)) for hh in range(2)]
            dv_part = jnp.zeros((tk, LANES), F32)
            dk_part = jnp.zeros((tk, LANES), F32)
            for hh in range(2):
                bias = cbs[hh] - cr_ref[0, hh:hh + 1, pl.ds(start, tk)]
                s = lax.dot_general(qhs[hh], kj, NT_DIMS, preferred_element_type=F32) + bias
                if masked:
                    s = jnp.where(causal, s, NEG_BIG)
                p = jnp.exp(s - lses[hh])
                dp = lax.dot_general(dohs[hh], vj, NT_DIMS, preferred_element_type=F32)
                ds = p * (dp - dls[hh])
                pb = p.astype(BF16)
                dsb = ds.astype(BF16)
                dv_part = dv_part + lax.dot_general(pb, dohs[hh], TN_DIMS, preferred_element_type=F32)
                dk_part = dk_part + lax.dot_general(dsb, qhs[hh], TN_DIMS, preferred_element_type=F32)
                dc_ref[0, hh:hh + 1, pl.ds(start, tk)] -= jnp.sum(ds, axis=0, keepdims=True)
                row_sums[hh] = row_sums[hh] + jnp.sum(ds, axis=1, keepdims=True)
                kh = jnp.where(head_k[hh], kj, jnp.zeros_like(kj))
                dq_acc = dq_acc + jnp.dot(dsb, kh, preferred_element_type=F32)
            dv_ref[pl.ds(start, tk), :] += dv_part
            dk_ref[pl.ds(start, tk), :] += dk_part
            return (dq_acc, *row_sums)

        zero_col = jnp.zeros((tq, 1), F32)
        n_wide = i // KV_STEP
        carry = lax.fori_loop(0, n_wide, lambda j, c: step(j * KV_STEP, KV_STEP, c, False),
                              (jnp.zeros((tq, LANES), F32), zero_col, zero_col))
        carry = lax.fori_loop(n_wide * KV_STEP, i, lambda j, c: step(j, 1, c, False), carry)
        carry = step(i, 1, carry, True)
        dq_ref[...] = (carry[0] * scale).astype(dq_ref.dtype)
        for hh in range(2):
            dcq_ref[hh] = carry[1 + hh]

        @pl.when(i == nq - 1)
        def _():
            dk_out[...] = dk_ref[...].astype(dk_out.dtype)
            dv_out[...] = dv_ref[...].astype(dv_out.dtype)

    nblk = D_MODEL // LANES
    qblk = pl.BlockSpec((tq, LANES), lambda p, i: (i, p))
    stat = pl.BlockSpec((2, tq, 1), lambda p, i: (p, i, 0))
    full = pl.BlockSpec((S, LANES), lambda p, i: (0, p))
    return pl.pallas_call(
        body, name=name, grid=(HEAD_PAIRS, nq),
        in_specs=[
            pl.BlockSpec((tq, LANES), lambda p, i: (i, PB_Q * nblk + p)),
            pl.BlockSpec((S, LANES), lambda p, i: (0, PB_K * nblk + p)),
            pl.BlockSpec((S, LANES), lambda p, i: (0, PB_V * nblk + p)),
            qblk, qblk, stat,
            pl.BlockSpec((1, 2, S), lambda p, i: (p, 0, 0)),
        ],
        out_specs=[qblk, full, full, pl.BlockSpec((1, 2, S), lambda p, i: (p, 0, 0)), stat],
        out_shape=[
            jax.ShapeDtypeStruct((S, D_MODEL), BF16),
            jax.ShapeDtypeStruct((S, D_MODEL), BF16),
            jax.ShapeDtypeStruct((S, D_MODEL), BF16),
            jax.ShapeDtypeStruct((HEAD_PAIRS, 2, S), F32),
            jax.ShapeDtypeStruct((FOX_HEADS, S, 1), F32),
        ],
        scratch_shapes=[pltpu.VMEM((S, LANES), F32), pltpu.VMEM((S, LANES), F32)],
        compiler_params=_cparams(("parallel", "arbitrary"), 56),
    )(P, P, P, do, o32, lse, crow)


def _cumsum_lanes(xs, *, reverse, name):
    R, S = xs[0].shape
    nb = S // LANES
    n_in = len(xs)

    def body(*refs):
        x_refs, o_ref = refs[:n_in], refs[n_in]
        r = lax.broadcasted_iota(jnp.int32, (LANES, LANES), 0)
        c = lax.broadcasted_iota(jnp.int32, (LANES, LANES), 1)
        tri = ((r >= c) if reverse else (r <= c)).astype(F32)

        def step(b, carry):
            blk = (nb - 1 - b) if reverse else b
            start = pl.multiple_of(blk * LANES, LANES)
            xb = x_refs[0][:, pl.ds(start, LANES)]
            for r in x_refs[1:]:
                xb = xb + r[:, pl.ds(start, LANES)]
            y = jnp.dot(xb, tri, precision=lax.Precision.HIGHEST, preferred_element_type=F32) + carry
            o_ref[:, pl.ds(start, LANES)] = y
            return carry + jnp.sum(xb, axis=1, keepdims=True)

        lax.fori_loop(0, nb, step, jnp.zeros((R, 1), F32))

    return pl.pallas_call(
        body, name=name,
        in_specs=[pl.BlockSpec(memory_space=pltpu.VMEM)] * n_in,
        out_specs=pl.BlockSpec(memory_space=pltpu.VMEM),
        out_shape=jax.ShapeDtypeStruct((R, S), F32),
    )(*xs)


def _forget_fwd(f_logit, b_pad, *, name, tm=1024):
    S = f_logit.shape[0]
    tm = _tile(S, tm)

    def body(f_ref, b_ref, o_ref):
        z = f_ref[...] + b_ref[...]
        o_ref[...] = jnp.minimum(z, 0.0) - jnp.log(1.0 + jnp.exp(-jnp.abs(z)))

    blk = pl.BlockSpec((tm, LANES), lambda i: (i, 0))
    return pl.pallas_call(
        body, name=name, grid=(S // tm,),
        in_specs=[blk, pl.BlockSpec((1, LANES), lambda i: (0, 0))],
        out_specs=blk, out_shape=jax.ShapeDtypeStruct((S, LANES), F32),
        compiler_params=_cparams(("parallel",)),
    )(f_logit, b_pad)


def _forget_bwd(f_logit, b_pad, dlogf, *, name, tm=1024):
    S = f_logit.shape[0]
    tm = _tile(S, tm)

    def body(f_ref, b_ref, d_ref, o_ref, db_ref):
        i = pl.program_id(0)
        z = f_ref[...] + b_ref[...]
        dz = d_ref[...] * (1.0 - _sigmoid(z))
        o_ref[...] = dz.astype(o_ref.dtype)
        part = jnp.sum(dz, axis=0, keepdims=True)

        @pl.when(i == 0)
        def _():
            db_ref[...] = part

        @pl.when(i > 0)
        def _():
            db_ref[...] += part

    blk = pl.BlockSpec((tm, LANES), lambda i: (i, 0))
    vec = pl.BlockSpec((1, LANES), lambda i: (0, 0))
    return pl.pallas_call(
        body, name=name, grid=(S // tm,),
        in_specs=[blk, vec, blk], out_specs=[blk, vec],
        out_shape=[jax.ShapeDtypeStruct((S, LANES), BF16), jax.ShapeDtypeStruct((1, LANES), F32)],
        compiler_params=_cparams(("arbitrary",)),
    )(f_logit, b_pad, dlogf)


def _layernorm_stats(y):
    mu = jnp.mean(y, axis=-1, keepdims=True)
    yc = y - mu
    rstd = lax.rsqrt(jnp.mean(yc * yc, axis=-1, keepdims=True) + LN_EPS)
    return yc * rstd, rstd


def _fill_shifted(src_ref, sh_ref, tm):
    rows = tm + CONV_HALO - SUBLANES
    for s in range(1, SUBLANES):
        sh_ref[s - 1, 0:rows, :] = src_ref[pl.ds(s, rows), :]


def _shifted_rows(src_ref, sh_ref, r0, offset):
    s, base = offset % SUBLANES, offset - offset % SUBLANES
    rows = pl.ds(pl.multiple_of(r0 + base, SUBLANES), CONV_CHUNK)
    return src_ref[rows, :] if s == 0 else sh_ref[s - 1, rows, :]


def _fill_tap_rows(w_ref, wb_ref):
    for k in range(CONV_WIDTH):
        wb_ref[k * SUBLANES:(k + 1) * SUBLANES, :] = jnp.broadcast_to(w_ref[k:k + 1, :], (SUBLANES, w_ref.shape[1]))


def _tap_sum(wb_ref, rows_of_tap):
    n = CONV_CHUNK // SUBLANES
    accs = [None] * n
    for k in range(CONV_WIDTH):
        wk = wb_ref[k * SUBLANES:(k + 1) * SUBLANES, :]
        src = rows_of_tap(k)
        for h in range(n):
            term = wk * src[h * SUBLANES:(h + 1) * SUBLANES]
            accs[h] = term if accs[h] is None else accs[h] + term
    return jnp.concatenate(accs, axis=0)


def _conv_fwd(P, w_pad, conv_b, ln_g, ln_b, *, name, tm=256):
    S = P.shape[0]
    C = D_MODEL
    tm = _tile(S, tm)
    hb = tm // CONV_HALO

    def body(a_ref, gt_ref, ah_ref, gh_ref, w_ref, cb_ref, g_ref, b_ref, o_ref, y_ref, glu_ref, sh_ref, wb_ref):
        i = pl.program_id(0)
        halo = ah_ref[...].astype(F32) * _sigmoid(gh_ref[...].astype(F32))
        glu_ref[0:CONV_HALO, :] = jnp.where(i > 0, halo, 0.0)
        glu_ref[CONV_HALO:, :] = a_ref[...].astype(F32) * _sigmoid(gt_ref[...].astype(F32))
        _fill_shifted(glu_ref, sh_ref, tm)
        _fill_tap_rows(w_ref, wb_ref)

        def chunk(r, carry):
            r0 = pl.multiple_of(r * CONV_CHUNK, CONV_CHUNK)
            acc = _tap_sum(wb_ref, lambda k: _shifted_rows(glu_ref, sh_ref, r0, CONV_HALO - (CONV_WIDTH - 1) + k))
            y_ref[pl.ds(r0, CONV_CHUNK), :] = acc + cb_ref[...]
            return carry

        lax.fori_loop(0, tm // CONV_CHUNK, chunk, 0)
        xhat, _ = _layernorm_stats(y_ref[...])
        z = xhat * g_ref[...] + b_ref[...]
        o_ref[...] = (z * _sigmoid(z)).astype(o_ref.dtype)

    vec = pl.BlockSpec((1, C), lambda i: (0, 0))
    row = pl.BlockSpec((tm, C), lambda i: (i, 0))
    return pl.pallas_call(
        body, name=name, grid=(S // tm,),
        in_specs=[
            pl.BlockSpec((tm, C), lambda i: (i, PB_A)),
            pl.BlockSpec((tm, C), lambda i: (i, PB_GATE)),
            pl.BlockSpec((CONV_HALO, C), lambda i: (jnp.maximum(i * hb - 1, 0), PB_A)),
            pl.BlockSpec((CONV_HALO, C), lambda i: (jnp.maximum(i * hb - 1, 0), PB_GATE)),
            pl.BlockSpec((CONV_HALO, C), lambda i: (0, 0)),
            vec, vec, vec,
        ],
        out_specs=[row, row],
        out_shape=[jax.ShapeDtypeStruct((S, C), BF16), jax.ShapeDtypeStruct((S, C), F32)],
        scratch_shapes=[pltpu.VMEM((tm + CONV_HALO, C), F32), pltpu.VMEM((SUBLANES - 1, tm + CONV_HALO, C), F32),
                        pltpu.VMEM((CONV_HALO * SUBLANES, C), F32)],
        compiler_params=_cparams(("parallel",), 56),
    )(P, P, P, P, w_pad, conv_b, ln_g, ln_b)


def _conv_bwd(P, do, y, w_pad, ln_g, ln_b, *, name, tm=256):
    S = P.shape[0]
    C = D_MODEL
    tm = _tile(S, tm)
    hb = tm // CONV_HALO
    n_tiles = S // tm
    last_halo = S // CONV_HALO - 1

    def body(a_ref, gt_ref, ah_ref, gh_ref, do_ref, y_ref, don_ref, yn_ref, w_ref, g_ref, b_ref,
             dglu_ref, dw_ref, small_ref, glu_ref, dy_ref, gsh_ref, dsh_ref, dwacc_ref, wb_ref):
        i = pl.program_id(0)

        @pl.when(i == 0)
        def _():
            dwacc_ref[...] = jnp.zeros_like(dwacc_ref)
            small_ref[...] = jnp.zeros_like(small_ref)

        def ln_bwd(do_v, y_v):
            xhat, rstd = _layernorm_stats(y_v)
            z = xhat * g_ref[...] + b_ref[...]
            sg = _sigmoid(z)
            dz = do_v * (sg * (1.0 + z * (1.0 - sg)))
            dxh = dz * g_ref[...]
            dy = rstd * (dxh - jnp.mean(dxh, axis=-1, keepdims=True) - xhat * jnp.mean(dxh * xhat, axis=-1, keepdims=True))
            return dy, dz, xhat

        dy, dz, xhat = ln_bwd(do_ref[...], y_ref[...])
        dy_next, _, _ = ln_bwd(don_ref[...], yn_ref[...])
        small_ref[0:1, :] += jnp.sum(dz * xhat, axis=0, keepdims=True)
        small_ref[1:2, :] += jnp.sum(dz, axis=0, keepdims=True)
        small_ref[2:3, :] += jnp.sum(dy, axis=0, keepdims=True)
        dy_ref[0:tm, :] = dy
        dy_ref[tm:, :] = jnp.where(i < n_tiles - 1, dy_next, 0.0)

        halo = ah_ref[...].astype(F32) * _sigmoid(gh_ref[...].astype(F32))
        glu_ref[0:CONV_HALO, :] = jnp.where(i > 0, halo, 0.0)
        glu_ref[CONV_HALO:, :] = a_ref[...].astype(F32) * _sigmoid(gt_ref[...].astype(F32))
        _fill_shifted(glu_ref, gsh_ref, tm)
        _fill_shifted(dy_ref, dsh_ref, tm)
        _fill_tap_rows(w_ref, wb_ref)

        def chunk(r, carry):
            r0 = pl.multiple_of(r * CONV_CHUNK, CONV_CHUNK)
            rows = pl.ds(r0, CONV_CHUNK)
            dyc = dy_ref[rows, :]
            for k in range(CONV_WIDTH):
                prod = dyc * _shifted_rows(glu_ref, gsh_ref, r0, CONV_HALO - (CONV_WIDTH - 1) + k)
                dwacc_ref[k * SUBLANES:(k + 1) * SUBLANES, :] += prod[0:SUBLANES] + prod[SUBLANES:]
            dg = _tap_sum(wb_ref, lambda k: _shifted_rows(dy_ref, dsh_ref, r0, CONV_WIDTH - 1 - k))
            a = a_ref[rows, :].astype(F32)
            sig = _sigmoid(gt_ref[rows, :].astype(F32))
            dglu_ref[rows, 0:C] = (dg * sig).astype(dglu_ref.dtype)
            dglu_ref[rows, C:] = (dg * a * sig * (1.0 - sig)).astype(dglu_ref.dtype)
            return carry

        lax.fori_loop(0, tm // CONV_CHUNK, chunk, 0)

        @pl.when(i == n_tiles - 1)
        def _():
            for k in range(CONV_WIDTH):
                dw_ref[k:k + 1, :] = jnp.sum(dwacc_ref[k * SUBLANES:(k + 1) * SUBLANES, :], axis=0, keepdims=True)
            dw_ref[CONV_WIDTH:, :] = jnp.zeros((CONV_HALO - CONV_WIDTH, C), F32)

    vec = pl.BlockSpec((1, C), lambda i: (0, 0))
    row = pl.BlockSpec((tm, C), lambda i: (i, 0))
    nxt = pl.BlockSpec((CONV_HALO, C), lambda i: (jnp.minimum((i + 1) * hb, last_halo), 0))
    return pl.pallas_call(
        body, name=name, grid=(n_tiles,),
        in_specs=[
            pl.BlockSpec((tm, C), lambda i: (i, PB_A)),
            pl.BlockSpec((tm, C), lambda i: (i, PB_GATE)),
            pl.BlockSpec((CONV_HALO, C), lambda i: (jnp.maximum(i * hb - 1, 0), PB_A)),
            pl.BlockSpec((CONV_HALO, C), lambda i: (jnp.maximum(i * hb - 1, 0), PB_GATE)),
            row, row, nxt, nxt,
            pl.BlockSpec((CONV_HALO, C), lambda i: (0, 0)),
            vec, vec,
        ],
        out_specs=[
            pl.BlockSpec((tm, 2 * C), lambda i: (i, 0)),
            pl.BlockSpec((CONV_HALO, C), lambda i: (0, 0)),
            pl.BlockSpec((8, C), lambda i: (0, 0)),
        ],
        out_shape=[
            jax.ShapeDtypeStruct((S, 2 * C), BF16),
            jax.ShapeDtypeStruct((CONV_HALO, C), F32),
            jax.ShapeDtypeStruct((8, C), F32),
        ],
        scratch_shapes=[
            pltpu.VMEM((tm + CONV_HALO, C), F32), pltpu.VMEM((tm + CONV_HALO, C), F32),
            pltpu.VMEM((SUBLANES - 1, tm + CONV_HALO, C), F32), pltpu.VMEM((SUBLANES - 1, tm + CONV_HALO, C), F32),
            pltpu.VMEM((CONV_HALO * SUBLANES, C), F32), pltpu.VMEM((CONV_HALO * SUBLANES, C), F32),
        ],
        compiler_params=_cparams(("arbitrary",), 56),
    )(P, P, P, P, do, y, do, y, w_pad, ln_g, ln_b)


def _mem_softmax(qh, kh):
    s = lax.dot_general(qh, kh, NT_DIMS, preferred_element_type=F32)
    e = jnp.exp(s - jnp.max(s, axis=1, keepdims=True))
    return e / jnp.sum(e, axis=1, keepdims=True)


def _mem_fwd(P, kv, *, name, tm=512):
    S, M = P.shape[0], kv.shape[0]
    tm = _tile(S, tm)
    scale = MEM_HEAD_DIM ** -0.5

    def body(q_ref, k_ref, v_ref, o_ref):
        for h in range(MEM_HEADS):
            sl = slice(h * MEM_HEAD_DIM, (h + 1) * MEM_HEAD_DIM)
            qh = q_ref[:, sl] * jnp.asarray(scale, BF16)
            p = _mem_softmax(qh, k_ref[:, sl])
            o_ref[:, sl] = jnp.dot(p.astype(BF16), v_ref[:, sl], preferred_element_type=F32).astype(o_ref.dtype)

    return pl.pallas_call(
        body, name=name, grid=(S // tm,),
        in_specs=[
            pl.BlockSpec((tm, D_MODEL), lambda i: (i, PB_QMEM)),
            pl.BlockSpec((M, D_MODEL), lambda i: (0, 0)),
            pl.BlockSpec((M, D_MODEL), lambda i: (0, 1)),
        ],
        out_specs=pl.BlockSpec((tm, D_MODEL), lambda i: (i, 0)),
        out_shape=jax.ShapeDtypeStruct((S, D_MODEL), BF16),
        compiler_params=_cparams(("parallel",)),
    )(P, kv, kv)


def _mem_bwd(P, do, kv, *, name, tm=512):
    S, M = P.shape[0], kv.shape[0]
    tm = _tile(S, tm)
    scale = MEM_HEAD_DIM ** -0.5

    def body(q_ref, k_ref, v_ref, do_ref, dq_ref, dkv_ref):
        i = pl.program_id(0)

        @pl.when(i == 0)
        def _():
            dkv_ref[...] = jnp.zeros_like(dkv_ref)

        for h in range(MEM_HEADS):
            sl = slice(h * MEM_HEAD_DIM, (h + 1) * MEM_HEAD_DIM)
            slv = slice(D_MODEL + h * MEM_HEAD_DIM, D_MODEL + (h + 1) * MEM_HEAD_DIM)
            qh = q_ref[:, sl] * jnp.asarray(scale, BF16)
            kh, vh, doh = k_ref[:, sl], v_ref[:, sl], do_ref[:, sl]
            p = _mem_softmax(qh, kh)
            dp = lax.dot_general(doh, vh, NT_DIMS, preferred_element_type=F32)
            ds = p * (dp - jnp.sum(p * dp, axis=1, keepdims=True))
            dsb = ds.astype(BF16)
            dq_ref[:, sl] = (jnp.dot(dsb, kh, preferred_element_type=F32) * scale).astype(dq_ref.dtype)
            dkv_ref[:, sl] += lax.dot_general(dsb, qh, TN_DIMS, preferred_element_type=F32)
            dkv_ref[:, slv] += lax.dot_general(p.astype(BF16), doh, TN_DIMS, preferred_element_type=F32)

    row = pl.BlockSpec((tm, D_MODEL), lambda i: (i, 0))
    return pl.pallas_call(
        body, name=name, grid=(S // tm,),
        in_specs=[
            pl.BlockSpec((tm, D_MODEL), lambda i: (i, PB_QMEM)),
            pl.BlockSpec((M, D_MODEL), lambda i: (0, 0)),
            pl.BlockSpec((M, D_MODEL), lambda i: (0, 1)),
            row,
        ],
        out_specs=[row, pl.BlockSpec((M, 2 * D_MODEL), lambda i: (0, 0))],
        out_shape=[jax.ShapeDtypeStruct((S, D_MODEL), BF16), jax.ShapeDtypeStruct((M, 2 * D_MODEL), F32)],
        compiler_params=_cparams(("arbitrary",)),
    )(P, kv, kv, do)


def _merge_fwd(o_attn, o_conv, o_mem, P, wpa, wpc, wpm, w_out, *, name, tm=256):
    S = P.shape[0]
    D = D_MODEL
    tm = _tile(S, tm)

    def body(oa_ref, oc_ref, om_ref, g0_ref, g1_ref, g2_ref, wa_ref, wc_ref, wm_ref, wo_ref, mg_ref, y_ref, pb_ref):
        merged = jnp.zeros((tm, D), F32)
        for b, (o_ref, g_ref, w_ref) in enumerate(((oa_ref, g0_ref, wa_ref), (oc_ref, g1_ref, wc_ref), (om_ref, g2_ref, wm_ref))):
            pb = jnp.dot(o_ref[...], w_ref[...], preferred_element_type=F32)
            pb_ref[b] = pb.astype(pb_ref.dtype)
            merged = merged + _sigmoid(g_ref[...].astype(F32)) * pb
        mb = merged.astype(BF16)
        mg_ref[...] = mb
        y_ref[...] = jnp.dot(mb, wo_ref[...], preferred_element_type=F32)

    row = pl.BlockSpec((tm, D), lambda i: (i, 0))
    wsp = pl.BlockSpec((D, D), lambda i: (0, 0))
    return pl.pallas_call(
        body, name=name, grid=(S // tm,),
        in_specs=[row, row, row] + [pl.BlockSpec((tm, D), lambda i, b=b: (i, PB_G0 + b)) for b in range(3)] + [wsp] * 4,
        out_specs=[row, row, pl.BlockSpec((3, tm, D), lambda i: (0, i, 0))],
        out_shape=[jax.ShapeDtypeStruct((S, D), BF16), jax.ShapeDtypeStruct((S, D), F32), jax.ShapeDtypeStruct((3, S, D), BF16)],
        compiler_params=_cparams(("parallel",), 56),
    )(o_attn, o_conv, o_mem, P, P, P, wpa, wpc, wpm, w_out)


def _merge_bwd(dy1, w_out, P, pb, *, name, tm=256):
    S = P.shape[0]
    D = D_MODEL
    tm = _tile(S, tm)

    def body(dy_ref, wo_ref, g0_ref, g1_ref, g2_ref, pb_ref, d0_ref, d1_ref, d2_ref, dgl_ref):
        dm = lax.dot_general(dy_ref[...], wo_ref[...], NT_DIMS, preferred_element_type=F32)
        for b, (g_ref, d_ref) in enumerate(((g0_ref, d0_ref), (g1_ref, d1_ref), (g2_ref, d2_ref))):
            g = _sigmoid(g_ref[...].astype(F32))
            d_ref[...] = (dm * g).astype(d_ref.dtype)
            dgl_ref[:, b * D:(b + 1) * D] = (dm * pb_ref[b].astype(F32) * g * (1.0 - g)).astype(dgl_ref.dtype)

    row = pl.BlockSpec((tm, D), lambda i: (i, 0))
    blk3 = pl.BlockSpec((3, tm, D), lambda i: (0, i, 0))
    sds = jax.ShapeDtypeStruct((S, D), BF16)
    return pl.pallas_call(
        body, name=name, grid=(S // tm,),
        in_specs=[row, pl.BlockSpec((D, D), lambda i: (0, 0))]
        + [pl.BlockSpec((tm, D), lambda i, b=b: (i, PB_G0 + b)) for b in range(3)] + [blk3],
        out_specs=[row, row, row, pl.BlockSpec((tm, 3 * D), lambda i: (i, 0))],
        out_shape=[sds, sds, sds, jax.ShapeDtypeStruct((S, 3 * D), BF16)],
        compiler_params=_cparams(("parallel",), 56),
    )(dy1, w_out, P, P, P, pb)


def _resid_norm(x, y1, g_post, g_pre, *, name, tm=512):
    S, D = x.shape
    tm = _tile(S, tm)

    def body(x_ref, y_ref, gp_ref, gq_ref, x1_ref, h2_ref):
        yv = y_ref[...]
        x1 = x_ref[...] + yv * lax.rsqrt(jnp.mean(yv * yv, axis=-1, keepdims=True) + RMS_EPS) * gp_ref[...]
        x1_ref[...] = x1
        h2_ref[...] = (x1 * lax.rsqrt(jnp.mean(x1 * x1, axis=-1, keepdims=True) + RMS_EPS) * gq_ref[...]).astype(h2_ref.dtype)

    row = pl.BlockSpec((tm, D), lambda i: (i, 0))
    vec = pl.BlockSpec((1, D), lambda i: (0, 0))
    return pl.pallas_call(
        body, name=name, grid=(S // tm,),
        in_specs=[row, row, vec, vec], out_specs=[row, row],
        out_shape=[jax.ShapeDtypeStruct((S, D), F32), jax.ShapeDtypeStruct((S, D), BF16)],
        compiler_params=_cparams(("parallel",)),
    )(x, y1, g_post, g_pre)


def _swiglu_fwd(h2, w_gu_t, *, name, tm=512, tn=1408):
    S, D = h2.shape
    Fh = w_gu_t.shape[0] // 2
    tm, tn = _tile(S, tm), _tile(Fh, tn)
    nj = Fh // tn

    def body(h_ref, wg_ref, wu_ref, g_ref, u_ref, a_ref):
        hv = h_ref[...]
        g = lax.dot_general(hv, wg_ref[...], NT_DIMS, preferred_element_type=F32)
        u = lax.dot_general(hv, wu_ref[...], NT_DIMS, preferred_element_type=F32)
        g_ref[...] = g.astype(g_ref.dtype)
        u_ref[...] = u.astype(u_ref.dtype)
        a_ref[...] = (g * _sigmoid(g) * u).astype(a_ref.dtype)

    out = pl.BlockSpec((tm, tn), lambda j, i: (i, j))
    sds = jax.ShapeDtypeStruct((S, Fh), BF16)
    return pl.pallas_call(
        body, name=name, grid=(nj, S // tm),
        in_specs=[
            pl.BlockSpec((tm, D), lambda j, i: (i, 0)),
            pl.BlockSpec((tn, D), lambda j, i: (j, 0)),
            pl.BlockSpec((tn, D), lambda j, i: (j + nj, 0)),
        ],
        out_specs=[out, out, out], out_shape=[sds, sds, sds],
        compiler_params=_cparams(("parallel", "parallel"), 56),
    )(h2, w_gu_t, w_gu_t)


def _swiglu_bwd(dffn, w_down, g, u, *, name, tm=512, tn=1408):
    S, Fh = g.shape
    D = dffn.shape[1]
    tm, tn = _tile(S, tm), _tile(Fh, tn)

    def body(df_ref, w_ref, g_ref, u_ref, o_ref):
        dfv = df_ref[...]
        for j in range(Fh // tn):
            cols = slice(j * tn, (j + 1) * tn)
            da = lax.dot_general(dfv, w_ref[cols, :], NT_DIMS, preferred_element_type=F32)
            gv = g_ref[:, cols].astype(F32)
            sg = _sigmoid(gv)
            o_ref[:, cols] = (da * u_ref[:, cols].astype(F32) * (sg * (1.0 + gv * (1.0 - sg)))).astype(o_ref.dtype)
            o_ref[:, Fh + j * tn:Fh + (j + 1) * tn] = (da * gv * sg).astype(o_ref.dtype)

    act = pl.BlockSpec((tm, Fh), lambda i: (i, 0))
    return pl.pallas_call(
        body, name=name, grid=(S // tm,),
        in_specs=[pl.BlockSpec((tm, D), lambda i: (i, 0)), pl.BlockSpec((Fh, D), lambda i: (0, 0)), act, act],
        out_specs=pl.BlockSpec((tm, 2 * Fh), lambda i: (i, 0)),
        out_shape=jax.ShapeDtypeStruct((S, 2 * Fh), BF16),
        compiler_params=_cparams(("parallel",), 56),
    )(dffn, w_down, g, u)


def _final(x1, ffn, target, g, *, name, tm=512):
    S, D = x1.shape
    tm = _tile(S, tm)

    def body(x_ref, f_ref, t_ref, g_ref, dout_ref, dffn_ref, loss_ref, dg_ref):
        i = pl.program_id(0)
        fv = f_ref[...]
        rstd = lax.rsqrt(jnp.mean(fv * fv, axis=-1, keepdims=True) + RMS_EPS)
        r = fv * rstd
        e = x_ref[...] + r * g_ref[...] - t_ref[...]
        dout = e * (1.0 / D)
        dout_ref[...] = dout
        gy = dout * g_ref[...]
        dffn_ref[...] = (rstd * (gy - r * jnp.mean(gy * r, axis=-1, keepdims=True))).astype(dffn_ref.dtype)
        lpart = jnp.full((8, LANES), 0.5 * jnp.sum(jnp.mean(e * e, axis=-1, keepdims=True)), F32)
        gpart = jnp.sum(dout * r, axis=0, keepdims=True)

        @pl.when(i == 0)
        def _():
            loss_ref[...] = lpart
            dg_ref[...] = gpart

        @pl.when(i > 0)
        def _():
            loss_ref[...] += lpart
            dg_ref[...] += gpart

    row = pl.BlockSpec((tm, D), lambda i: (i, 0))
    vec = pl.BlockSpec((1, D), lambda i: (0, 0))
    return pl.pallas_call(
        body, name=name, grid=(S // tm,),
        in_specs=[row, row, row, vec],
        out_specs=[row, row, pl.BlockSpec((8, LANES), lambda i: (0, 0)), vec],
        out_shape=[jax.ShapeDtypeStruct((S, D), F32), jax.ShapeDtypeStruct((S, D), BF16),
                   jax.ShapeDtypeStruct((8, LANES), F32), jax.ShapeDtypeStruct((1, D), F32)],
        compiler_params=_cparams(("arbitrary",)),
    )(x1, ffn, target, g)


def _after(token, value):
    return value if token is None else value + token[0, 0].astype(value.dtype)


def _local_step(x, mem, target, gains, w, late_weights, send_grads, *, tq=512):
    S = x.shape[0]
    b_pad = jnp.pad(gains["b_forget"], ((0, 0), (0, LANES - FOX_HEADS)))
    gains = dict(gains)

    h = _rms_fwd(x, gains["norm_mix_pre"], name="rms_mix_pre")
    P = _matmul(h, w["wp_t"], mode="nt", out_dtype=BF16, name="proj_in")
    f_logit = _matmul(h, w["wf_t"], mode="nt", out_dtype=F32, name="proj_forget")
    logf = _forget_fwd(f_logit, b_pad, name="forget_fwd")
    c_row16 = _cumsum_lanes([logf[:, :FOX_HEADS].T], reverse=False, name="forget_cumsum")
    crow = c_row16.reshape(HEAD_PAIRS, 2, S)
    o_attn, o_attn32, lse = _attn_fwd(P, crow, tq=tq, name="attn_fwd")
    w = dict(w, **late_weights(o_attn))
    o_conv, y_conv = _conv_fwd(P, w["conv_w"], gains["conv_b"], gains["conv_ln_g"], gains["conv_ln_b"], name="conv_fwd")
    mem_n = _rms_fwd(mem, gains["norm_mem"], name="rms_mem")
    kv = _matmul(mem_n, w["w_kv"], mode="nn", out_dtype=BF16, name="mem_kv")
    o_mem = _mem_fwd(P, kv, name="mem_fwd")
    merged, y1, pb = _merge_fwd(o_attn, o_conv, o_mem, P, w["wpa"], w["wpc"], w["wpm"], w["w_out"], name="merge_fwd")
    x1, h2 = _resid_norm(x, y1, gains["norm_mix_post"], gains["norm_ffn_pre"], name="resid_norm")
    g_ffn, u_ffn, act = _swiglu_fwd(h2, w["w_gu_t"], name="swiglu_fwd")
    ffn = _matmul(act, w["w_down"], mode="nn", out_dtype=F32, name="ffn_down", tk=1408)

    dout, dffn, loss_tile, d_norm_ffn_post = _final(x1, ffn, target, gains["norm_ffn_post"], name="loss_head")
    d_w_down = _matmul(act, dffn, mode="tn", out_dtype=BF16, name="dw_down", tm=1408)
    dgu = _swiglu_bwd(dffn, w["w_down"], g_ffn, u_ffn, name="swiglu_bwd")
    dh2 = _matmul(dgu, w["w_gu_t"], mode="nn", out_dtype=F32, name="d_h2", tk=1408)
    d_w_gu_t = _matmul(dgu, h2, mode="tn", out_dtype=BF16, name="dw_gate_up", tm=1408)
    dx1, d_norm_ffn_pre = _rms_bwd(x1, [dh2], gains["norm_ffn_pre"], dout, out_dtype=F32, name="rms_ffn_pre_bwd")

    dy1, d_norm_mix_post = _rms_bwd(y1, [dx1], gains["norm_mix_post"], None, out_dtype=BF16, name="rms_mix_post_bwd")
    d_w_out = _matmul(merged, dy1, mode="tn", out_dtype=BF16, name="dw_out")
    *dpb, dgl = _merge_bwd(dy1, w["w_out"], P, pb, name="merge_bwd")
    do_attn = _matmul(dpb[0], w["wpa"], mode="nt", out_dtype=BF16, name="d_o_attn")
    do_conv = _matmul(dpb[1], w["wpc"], mode="nt", out_dtype=F32, name="d_o_conv")
    do_mem = _matmul(dpb[2], w["wpm"], mode="nt", out_dtype=BF16, name="d_o_mem")
    d_wpa = _matmul(o_attn, dpb[0], mode="tn", out_dtype=BF16, name="dw_proj_attn")
    d_wpc = _matmul(o_conv, dpb[1], mode="tn", out_dtype=BF16, name="dw_proj_conv")
    d_wpm = _matmul(o_mem, dpb[2], mode="tn", out_dtype=BF16, name="dw_proj_mem")

    dq_mem, dkv = _mem_bwd(P, do_mem, kv, name="mem_bwd")
    dkv_b = dkv.astype(BF16)
    d_w_kv = _matmul(mem_n, dkv_b, mode="tn", out_dtype=BF16, name="dw_kv")
    dmem_n = _matmul(dkv_b, w["w_kv"], mode="nt", out_dtype=F32, name="d_mem_n")
    _, d_norm_mem = _rms_bwd(mem, [dmem_n], gains["norm_mem"], None, out_dtype=BF16, name="rms_mem_bwd")

    dglu, d_conv_w, conv_small = _conv_bwd(P, do_conv, y_conv, w["conv_w"], gains["conv_ln_g"], gains["conv_ln_b"], name="conv_bwd")
    sent = send_grads("body", dict(conv_w=d_conv_w[:CONV_WIDTH].astype(BF16), w_kv_mem=d_w_kv, w_proj_attn=d_wpa,
                                   w_proj_conv=d_wpc, w_proj_mem=d_wpm, w_out=d_w_out, w_gate_up=d_w_gu_t, w_down=d_w_down))

    dq, dk, dv, dc, dcq = _attn_bwd(P, do_attn, o_attn32, lse, _after(sent, crow), tq=tq, name="attn_bwd")
    dlogf16 = _cumsum_lanes([dc.reshape(FOX_HEADS, S), dcq[:, :, 0]], reverse=True, name="forget_cumsum_bwd")
    dlogf = jnp.pad(dlogf16.T, ((0, 0), (0, LANES - FOX_HEADS)))
    df, d_b_pad = _forget_bwd(f_logit, b_pad, dlogf, name="forget_bwd")

    dP = jnp.concatenate([dq, dk, dv, dglu, dq_mem, dgl], axis=1)
    d_wp_t = _matmul(dP, h, mode="tn", out_dtype=BF16, name="dw_in")
    d_wf_t = _matmul(df, h, mode="tn", out_dtype=BF16, name="dw_forget")
    n_qkv = 3 * D_MODEL
    sent = send_grads("w_in", dict(w_in=jnp.concatenate([d_wp_t[:n_qkv], d_wf_t[:FOX_HEADS], d_wp_t[n_qkv:]], axis=0)))
    dh_p = _matmul(dP, w["wp_t"], mode="nn", out_dtype=F32, name="d_h", after=sent)
    dh_f = _matmul(df, w["wf_t"], mode="nn", out_dtype=F32, name="d_h_forget")
    grad_x, d_norm_mix_pre = _rms_bwd(x, [dh_p, dh_f], gains["norm_mix_pre"], dx1, out_dtype=F32, name="rms_mix_pre_bwd")

    small = dict(norm_mix_pre=d_norm_mix_pre, norm_mix_post=d_norm_mix_post, norm_mem=d_norm_mem,
                 conv_b=conv_small[2:3], conv_ln_g=conv_small[0:1], conv_ln_b=conv_small[1:2],
                 norm_ffn_pre=d_norm_ffn_pre, norm_ffn_post=d_norm_ffn_post, b_forget=d_b_pad[:, :FOX_HEADS])
    return loss_tile, grad_x, small


def _mesh_pos():
    return lax.axis_index("x"), lax.axis_index("y"), lax.axis_index("c")


def _flip(pos, d):
    x, y, c = pos
    return (1 - x if d & 4 else x, 1 - y if d & 2 else y, 1 - c if d & 1 else c)


def _flat(pos):
    x, y, c = pos
    return 4 * x + 2 * y + c


def _gather_two_level(src, *, name):
    def body(src_ref, out_ref, token, send_sems, recv_sems, local_sem):
        me = _mesh_pos()
        my = _flat(me)
        sibling = _flip(me, 1)
        far = [_flip(me, d) for d in (4, 2, 6)]

        def copy(k, block, to, own=False):
            return pltpu.make_async_remote_copy(
                src_ref=src_ref if own else out_ref.at[_flat(block)], dst_ref=out_ref.at[_flat(block)],
                send_sem=send_sems.at[k], recv_sem=recv_sems.at[k], device_id=to, device_id_type=pl.DeviceIdType.MESH)

        local = pltpu.make_async_copy(src_ref, out_ref.at[my], local_sem)
        local.start()
        first = [copy(0, me, sibling, own=True)] + [copy(1 + j, me, peer, own=True) for j, peer in enumerate(far)]
        for cp in first:
            cp.start()
        passed = [copy(4 + j, peer, sibling) for j, peer in enumerate(far)]
        for j, peer in enumerate(far):
            copy(1 + j, peer, me).wait_recv()
            passed[j].start()
        copy(0, sibling, me).wait_recv()
        for j, peer in enumerate(far):
            copy(4 + j, _flip(peer, 1), me).wait_recv()
        for cp in first + passed:
            cp.wait_send()
        local.wait()
        token[...] = jnp.zeros_like(token)

    n_copy = N_DEV - 1
    out, token = pl.pallas_call(
        body, name=name,
        in_specs=[pl.BlockSpec(memory_space=pl.ANY)],
        out_specs=[pl.BlockSpec(memory_space=pl.ANY), pl.BlockSpec(memory_space=pltpu.VMEM)],
        out_shape=[jax.ShapeDtypeStruct((N_DEV,) + tuple(src.shape), src.dtype), jax.ShapeDtypeStruct((8, LANES), F32)],
        scratch_shapes=[pltpu.SemaphoreType.DMA((n_copy,)), pltpu.SemaphoreType.DMA((n_copy,)), pltpu.SemaphoreType.DMA(())],
    )(src)
    return [out], token


def _exchange(srcs, *, scatter, name):
    T = len(srcs)
    n_peer = N_DEV - 1

    def body(*refs):
        src_refs, out_refs = refs[:T], refs[T:2 * T]
        token, send_sems, recv_sems, local_sems = refs[2 * T:]
        me = _mesh_pos()
        my = _flat(me)

        def copy(t, d):
            peer = _flip(me, d)
            return pltpu.make_async_remote_copy(
                src_ref=src_refs[t].at[_flat(peer)] if scatter else src_refs[t],
                dst_ref=out_refs[t].at[my],
                send_sem=send_sems.at[t * n_peer + d - 1], recv_sem=recv_sems.at[t * n_peer + d - 1],
                device_id=peer, device_id_type=pl.DeviceIdType.MESH)

        def arrival(t, d):
            peer = _flip(me, d)
            return pltpu.make_async_remote_copy(
                src_ref=src_refs[t].at[my] if scatter else src_refs[t],
                dst_ref=out_refs[t].at[_flat(peer)],
                send_sem=send_sems.at[t * n_peer + d - 1], recv_sem=recv_sems.at[t * n_peer + d - 1],
                device_id=peer, device_id_type=pl.DeviceIdType.MESH)

        local = [pltpu.make_async_copy(src_refs[t].at[my] if scatter else src_refs[t], out_refs[t].at[my], local_sems.at[t])
                 for t in range(T)]
        for cp in local:
            cp.start()
        sends = [copy(t, d) for d in range(1, N_DEV) for t in range(T)]
        for cp in sends:
            cp.start()
        for d in range(1, N_DEV):
            for t in range(T):
                arrival(t, d).wait_recv()
        for cp in sends:
            cp.wait_send()
        for cp in local:
            cp.wait()
        token[...] = jnp.zeros_like(token)

    outs = pl.pallas_call(
        body, name=name,
        in_specs=[pl.BlockSpec(memory_space=pl.ANY)] * T,
        out_specs=[pl.BlockSpec(memory_space=pl.ANY)] * T + [pl.BlockSpec(memory_space=pltpu.VMEM)],
        out_shape=[jax.ShapeDtypeStruct((N_DEV,) + tuple(s.shape[-2:]), s.dtype) for s in srcs] + [jax.ShapeDtypeStruct((8, LANES), F32)],
        scratch_shapes=[pltpu.SemaphoreType.DMA((T * n_peer,)), pltpu.SemaphoreType.DMA((T * n_peer,)), pltpu.SemaphoreType.DMA((T,))],
    )(*srcs)
    return list(outs[:T]), outs[T]


HBM_SPEC = pl.BlockSpec(memory_space=pltpu.HBM)
SEM_SPEC = pl.BlockSpec(memory_space=pltpu.SEMAPHORE)


def _split_copies(src_refs, land_refs, send_sems, recv_sems, scatter):
    me = _mesh_pos()
    my = _flat(me)
    n_peer = N_DEV - 1
    out = []
    for d in range(1, N_DEV):
        peer = _flip(me, d)
        for t, (src, land) in enumerate(zip(src_refs, land_refs)):
            k = t * n_peer + d - 1
            start = pltpu.make_async_remote_copy(
                src_ref=src.at[_flat(peer)] if scatter else src, dst_ref=land.at[my],
                send_sem=send_sems.at[k], recv_sem=recv_sems.at[k], device_id=peer, device_id_type=pl.DeviceIdType.MESH)
            arrive = pltpu.make_async_remote_copy(
                src_ref=src.at[my] if scatter else src, dst_ref=land.at[_flat(peer)],
                send_sem=send_sems.at[k], recv_sem=recv_sems.at[k], device_id=peer, device_id_type=pl.DeviceIdType.MESH)
            out.append((start, arrive))
    return out


def _local_copies(src_refs, land_refs, local_sems, scatter):
    my = _flat(_mesh_pos())
    return [pltpu.make_async_copy(src.at[my] if scatter else src, land.at[my], local_sems.at[t])
            for t, (src, land) in enumerate(zip(src_refs, land_refs))]


def _exchange_start(srcs, *, scatter, name):
    T = len(srcs)
    n_sem = T * (N_DEV - 1)
    lands = [lax.empty((N_DEV,) + tuple(s.shape[-2:]), s.dtype) for s in srcs]

    def body(*refs):
        src_refs, land_refs = refs[:T], refs[T:2 * T]
        send_sems, recv_sems, local_sems = refs[2 * T:2 * T + 3]
        token = refs[-1]
        for cp in _local_copies(src_refs, land_refs, local_sems, scatter):
            cp.start()
        for start, _ in _split_copies(src_refs, land_refs, send_sems, recv_sems, scatter):
            start.start()
        token[...] = jnp.zeros_like(token)

    hbm = lambda a: pltpu.HBM(a.shape, a.dtype)
    outs = pl.pallas_call(
        body, name=name,
        in_specs=[HBM_SPEC] * (2 * T),
        out_specs=(SEM_SPEC, SEM_SPEC, SEM_SPEC, *[HBM_SPEC] * (2 * T), pl.BlockSpec(memory_space=pltpu.VMEM)),
        out_shape=(pltpu.SemaphoreType.DMA((n_sem,)), pltpu.SemaphoreType.DMA((n_sem,)), pltpu.SemaphoreType.DMA((T,)),
                   *[hbm(s) for s in srcs], *[hbm(a) for a in lands], jax.ShapeDtypeStruct((8, LANES), F32)),
        input_output_aliases={t: 3 + t for t in range(2 * T)},
        compiler_params=pltpu.CompilerParams(has_side_effects=pltpu.SideEffectType.DATAFLOW_SIDE_EFFECTING),
    )(*[pltpu.with_memory_space_constraint(s, pltpu.HBM) for s in srcs],
      *[pltpu.with_memory_space_constraint(a, pltpu.HBM) for a in lands])
    return (outs[0], outs[1], outs[2], list(outs[3:3 + T]), list(outs[3 + T:3 + 2 * T])), outs[-1]


def _exchange_wait(handle, after, *, scatter, name):
    send_sems, recv_sems, local_sems, srcs, lands = handle
    T = len(srcs)

    def body(*refs):
        src_refs, land_refs = refs[:T], refs[T:2 * T]
        s_sems, r_sems, l_sems = refs[2 * T:2 * T + 3]
        for cp in _local_copies(src_refs, land_refs, l_sems, scatter):
            cp.wait()
        for start, arrive in _split_copies(src_refs, land_refs, s_sems, r_sems, scatter):
            start.wait_send()
            arrive.wait_recv()

    hbm = lambda a: pltpu.HBM(a.shape, a.dtype)
    outs = pl.pallas_call(
        body, name=name,
        in_specs=[HBM_SPEC] * (2 * T) + [SEM_SPEC, SEM_SPEC, SEM_SPEC, pl.BlockSpec(memory_space=pl.ANY)],
        out_specs=[HBM_SPEC] * (2 * T),
        out_shape=[hbm(s) for s in srcs] + [hbm(a) for a in lands],
        input_output_aliases={t: t for t in range(2 * T)},
        compiler_params=pltpu.CompilerParams(has_side_effects=pltpu.SideEffectType.DATAFLOW_SIDE_EFFECTING),
    )(*srcs, *lands, send_sems, recv_sems, local_sems, after)
    return list(outs[T:])


def _adamw(w, slots, m, v, *, name):
    R, C = w.shape
    row_tiles = [t for t in range(16, 513, 16) if R % t == 0]
    tr = R if R <= 512 or not row_tiles else max(row_tiles)
    tc = C if tr < R or R <= 512 else LANES
    assert R % tr == 0 and C % tc == 0, (R, C, tr, tc)

    def body(w_ref, s_ref, m_ref, v_ref, g_ref, d_ref, m2_ref, v2_ref):
        gv = s_ref[0].astype(F32)
        for j in range(1, N_DEV):
            gv = gv + s_ref[j].astype(F32)
        g_ref[...] = gv
        m2 = ADAM_B1 * m_ref[...] + (1.0 - ADAM_B1) * gv
        v2 = ADAM_B2 * v_ref[...] + (1.0 - ADAM_B2) * (gv * gv)
        m_hat = m2 / (1.0 - ADAM_B1 ** ADAM_STEP)
        v_hat = v2 / (1.0 - ADAM_B2 ** ADAM_STEP)
        d_ref[...] = -ADAM_LR * (m_hat / (jnp.sqrt(v_hat) + ADAM_EPS) + ADAM_WD * w_ref[...])
        m2_ref[...] = m2
        v2_ref[...] = v2

    blk = pl.BlockSpec((tr, tc), lambda i, j: (i, j))
    sds = jax.ShapeDtypeStruct((R, C), F32)
    return pl.pallas_call(
        body, name=name, grid=(R // tr, C // tc),
        in_specs=[blk, pl.BlockSpec((N_DEV, tr, tc), lambda i, j: (0, i, j)), blk, blk],
        out_specs=[blk] * 4, out_shape=[sds] * 4,
        compiler_params=_cparams(("parallel", "parallel")),
    )(w, slots, m, v)


SHARDED = (
    ("w_in", (1154, 1024), "row"), ("conv_w", (31, 128), "col"), ("w_kv_mem", (1024, 256), "col"),
    ("w_proj_attn", (128, 1024), "row"), ("w_proj_conv", (128, 1024), "row"), ("w_proj_mem", (128, 1024), "row"),
    ("w_out", (128, 1024), "row"), ("w_gate_up", (704, 1024), "row"), ("w_down", (352, 1024), "row"),
)
TRANSPOSED = ("w_in", "w_gate_up")
SMALL = ("norm_mix_pre", "norm_mix_post", "norm_mem", "conv_b", "conv_ln_g", "conv_ln_b", "norm_ffn_pre", "norm_ffn_post", "b_forget")
SMALL_ROWS = 16
LOSS_ROW = len(SMALL)
WEIGHT_ORDER = ("norm_mix_pre", "norm_mix_post", "norm_mem", "w_in", "b_forget", "conv_w", "conv_b", "conv_ln_g", "conv_ln_b",
                "w_kv_mem", "w_proj_attn", "w_proj_conv", "w_proj_mem", "w_out", "norm_ffn_pre", "norm_ffn_post", "w_gate_up", "w_down")


def _to_full(blocks8, kind):
    n, r, c = blocks8.shape
    if kind == "col":
        return jnp.concatenate([blocks8[j] for j in range(n)], axis=1)
    return blocks8.reshape(n * r, c)


def _to_blocks(full, kind):
    if kind == "col":
        c = full.shape[1] // N_DEV
        return jnp.stack([full[:, j * c:(j + 1) * c] for j in range(N_DEV)])
    nr, c = full.shape
    return full.reshape(N_DEV, nr // N_DEV, c)


def kernel(x, mem, norm_mix_pre, norm_mix_post, norm_mem, w_in, b_forget, conv_w, conv_b, conv_ln_g, conv_ln_b, w_kv_mem, w_proj_attn, w_proj_conv, w_proj_mem, w_out, norm_ffn_pre, norm_ffn_post, w_gate_up, w_down, loss_target, m_norm_mix_pre, m_norm_mix_post, m_norm_mem, m_w_in, m_b_forget, m_conv_w, m_conv_b, m_conv_ln_g, m_conv_ln_b, m_w_kv_mem, m_w_proj_attn, m_w_proj_conv, m_w_proj_mem, m_w_out, m_norm_ffn_pre, m_norm_ffn_post, m_w_gate_up, m_w_down, v_norm_mix_pre, v_norm_mix_post, v_norm_mem, v_w_in, v_b_forget, v_conv_w, v_conv_b, v_conv_ln_g, v_conv_ln_b, v_w_kv_mem, v_w_proj_attn, v_w_proj_conv, v_w_proj_mem, v_w_out, v_norm_ffn_pre, v_norm_ffn_post, v_w_gate_up, v_w_down):
    given = dict(locals())
    weights = {n: given[n] for n in WEIGHT_ORDER}
    moments_m = {n: given["m_" + n] for n in WEIGHT_ORDER}
    moments_v = {n: given["v_" + n] for n in WEIGHT_ORDER}
    kind = {name: k for name, _, k in SHARDED}
    names = [name for name, _, _ in SHARDED]

    def local(a, n):
        return jnp.swapaxes(a[0], 0, 1) if n in TRANSPOSED else a[0]

    blocks = {n: local(weights[n], n).astype(F32 if n == "conv_w" else BF16) for n in names}
    late = [n for n in names if n != "w_in"]
    (w_in_blocks,), w_in_token = _gather_two_level(blocks["w_in"], name="gather_w_in")
    blocks["w_kv_mem"] = _after(w_in_token, blocks["w_kv_mem"])
    late_handle, late_token = _exchange_start([blocks[n] for n in late], scatter=False, name="gather_rest_start")
    n_qkv = 3 * D_MODEL
    w_in_t = _to_full(w_in_blocks, "row")
    w_early = dict(
        wp_t=jnp.concatenate([w_in_t[:n_qkv], w_in_t[n_qkv + FOX_HEADS:]], axis=0),
        wf_t=jnp.pad(w_in_t[n_qkv:n_qkv + FOX_HEADS], ((0, LANES - FOX_HEADS), (0, 0))),
    )

    def late_weights(after):
        got = dict(zip(late, _exchange_wait(late_handle, after, scatter=False, name="gather_rest_wait")))
        full = {n: _to_full(got[n], kind[n]) for n in late}
        return dict(
            conv_w=jnp.pad(full["conv_w"], ((0, CONV_HALO - CONV_WIDTH), (0, 0))),
            w_kv=full["w_kv_mem"], wpa=full["w_proj_attn"], wpc=full["w_proj_conv"], wpm=full["w_proj_mem"],
            w_out=full["w_out"], w_gu_t=full["w_gate_up"], w_down=full["w_down"],
        )

    sent = []

    def send_grads(group, g):
        order = [n for n in names if n in g]
        handle, token = _exchange_start([_to_blocks(g[n], kind[n]) for n in order], scatter=True, name="scatter_" + group + "_start")
        sent.append((group, order, handle))
        return token

    gains = {n: weights[n] for n in SMALL}
    gains["norm_mix_pre"] = _after(late_token, gains["norm_mix_pre"])

    loss_tile, grad_x, small = _local_step(x[0], mem[0], loss_target[0], gains, w_early, late_weights, send_grads)

    g_recv = {}
    for group, order, handle in sent:
        g_recv.update(zip(order, _exchange_wait(handle, grad_x, scatter=True, name="scatter_" + group + "_wait")))

    rows = [jnp.pad(small[n], ((0, 0), (0, D_MODEL - small[n].shape[1]))) for n in SMALL]
    rows.append(jnp.broadcast_to(loss_tile[0:1, 0:1], (1, D_MODEL)))
    rows.append(jnp.zeros((SMALL_ROWS - len(rows), D_MODEL), F32))
    (small_recv,), _ = _exchange([jnp.concatenate(rows, axis=0)], scatter=False, name="gather_small")

    def slab_of(d, fill):
        rows = [jnp.pad(d[n], ((0, 0), (0, D_MODEL - d[n].shape[1])), constant_values=fill) for n in SMALL]
        rows.append(jnp.full((SMALL_ROWS - len(rows), D_MODEL), fill, F32))
        return jnp.concatenate(rows, axis=0)

    grads, delta, new_m, new_v = {}, {}, {}, {}
    sg, sd, sm, sv = _adamw(slab_of(weights, 0.0), small_recv, slab_of(moments_m, 0.0), slab_of(moments_v, 1.0), name="adamw_small")
    loss = sg[LOSS_ROW, 0]
    for i, n in enumerate(SMALL):
        c = weights[n].shape[1]
        grads[n], delta[n], new_m[n], new_v[n] = sg[i:i + 1, :c], sd[i:i + 1, :c], sm[i:i + 1, :c], sv[i:i + 1, :c]
    for n in names:
        outs = _adamw(local(weights[n], n), g_recv[n], local(moments_m[n], n), local(moments_v[n], n), name="adamw_" + n)
        grads[n], delta[n], new_m[n], new_v[n] = [(jnp.swapaxes(o, 0, 1) if n in TRANSPOSED else o)[None] for o in outs]

    return (loss, grad_x[None], *[grads[n] for n in WEIGHT_ORDER], *[delta[n] for n in WEIGHT_ORDER],
            *[new_m[n] for n in WEIGHT_ORDER], *[new_v[n] for n in WEIGHT_ORDER])
```

```python
import functools

import jax
import jax.numpy as jnp
from jax import lax
from jax.experimental import pallas as pl
from jax.experimental.pallas import tpu as pltpu

F32 = jnp.float32
BF16 = jnp.bfloat16

D_MODEL = 1024
N_DEV = 8
FOX_HEADS = 16
FOX_HEAD_DIM = 64
HEAD_PAIRS = FOX_HEADS // 2
MEM_HEADS = 4
MEM_HEAD_DIM = D_MODEL // MEM_HEADS
CONV_WIDTH = 31
CONV_HALO = 32
CONV_CHUNK = 16
SUBLANES = 8
FFN_HIDDEN = 2816
RMS_EPS = 1e-6
LN_EPS = 1e-5
ADAM_LR = 0.001
ADAM_B1 = 0.9
ADAM_B2 = 0.999
ADAM_EPS = 1e-08
ADAM_WD = 0.01
ADAM_STEP = 10
KV_STEP = 2
KV_STEP_FWD = 4
NEG_BIG = -1e30
LANES = 128

PB_Q, PB_K, PB_V, PB_A, PB_GATE, PB_QMEM, PB_G0 = 0, 1, 2, 3, 4, 5, 6
P_WIDTH = 9 * D_MODEL

NT_DIMS = (((1,), (1,)), ((), ()))
TN_DIMS = (((0,), (0,)), ((), ()))


def _cparams(sem, vmem_mb=None):
    kw = dict(dimension_semantics=sem)
    if vmem_mb is not None:
        kw["vmem_limit_bytes"] = vmem_mb * 1024 * 1024
    return pltpu.CompilerParams(**kw)


def _tile(dim, want):
    t = min(dim, want)
    assert dim % t == 0, (dim, want)
    return t


def _sigmoid(z):
    return 1.0 / (1.0 + jnp.exp(-z))


def _matmul(a, b, *, mode, out_dtype, name, tm=1024, tn=1024, tk=1024, after=None):
    if mode == "nn":
        (M, K), (K2, N) = a.shape, b.shape
    elif mode == "nt":
        (M, K), (N, K2) = a.shape, b.shape
    else:
        (K, M), (K2, N) = a.shape, b.shape
    assert K == K2, (a.shape, b.shape, mode)
    tm, tn, tk = _tile(M, tm), _tile(N, tn), _tile(K, tk)
    nk = K // tk
    dims = {"nn": (((1,), (0,)), ((), ())), "nt": NT_DIMS, "tn": TN_DIMS}[mode]

    n_extra = 0 if after is None else 1

    def body(a_ref, b_ref, *rest):
        o_ref, scratch = rest[n_extra], rest[n_extra + 1:]
        part = lax.dot_general(a_ref[...], b_ref[...], dims, preferred_element_type=F32)
        if nk == 1:
            o_ref[...] = part.astype(o_ref.dtype)
        else:
            acc_ref, = scratch
            k = pl.program_id(2)

            @pl.when(k == 0)
            def _():
                acc_ref[...] = part

            @pl.when(k > 0)
            def _():
                acc_ref[...] += part

            @pl.when(k == nk - 1)
            def _():
                o_ref[...] = acc_ref[...].astype(o_ref.dtype)

    a_spec = pl.BlockSpec((tk, tm), lambda j, i, k: (k, i)) if mode == "tn" else pl.BlockSpec((tm, tk), lambda j, i, k: (i, k))
    b_spec = pl.BlockSpec((tn, tk), lambda j, i, k: (j, k)) if mode == "nt" else pl.BlockSpec((tk, tn), lambda j, i, k: (k, j))
    return pl.pallas_call(
        body,
        name=name,
        grid=(N // tn, M // tm, nk),
        in_specs=[a_spec, b_spec] + [pl.BlockSpec((8, LANES), lambda j, i, k: (0, 0))] * n_extra,
        out_specs=pl.BlockSpec((tm, tn), lambda j, i, k: (i, j)),
        out_shape=jax.ShapeDtypeStruct((M, N), out_dtype),
        scratch_shapes=[pltpu.VMEM((tm, tn), F32)] if nk > 1 else [],
        compiler_params=_cparams(("parallel", "parallel", "arbitrary"), 56),
    )(a, b, *([] if after is None else [after]))


def _rms_fwd(x, g, *, name, tm=512):
    S, D = x.shape
    tm = _tile(S, tm)

    def body(x_ref, g_ref, o_ref):
        xv = x_ref[...]
        rstd = lax.rsqrt(jnp.mean(xv * xv, axis=-1, keepdims=True) + RMS_EPS)
        o_ref[...] = (xv * rstd * g_ref[...]).astype(o_ref.dtype)

    return pl.pallas_call(
        body, name=name, grid=(S // tm,),
        in_specs=[pl.BlockSpec((tm, D), lambda i: (i, 0)), pl.BlockSpec((1, D), lambda i: (0, 0))],
        out_specs=pl.BlockSpec((tm, D), lambda i: (i, 0)),
        out_shape=jax.ShapeDtypeStruct((S, D), BF16),
        compiler_params=_cparams(("parallel",)),
    )(x, g)


def _rms_bwd(xin, dys, g, res, *, out_dtype, name, tm=512):
    S, D = xin.shape
    tm = _tile(S, tm)
    n_dy = len(dys)
    has_res = res is not None

    def body(*refs):
        x_ref, g_ref = refs[0], refs[1]
        dy_refs = refs[2:2 + n_dy]
        pos = 2 + n_dy
        res_ref = refs[pos] if has_res else None
        pos += int(has_res)
        dx_ref, dg_ref = refs[pos], refs[pos + 1]
        i = pl.program_id(0)
        xv = x_ref[...]
        dy = dy_refs[0][...].astype(F32)
        for r in dy_refs[1:]:
            dy = dy + r[...].astype(F32)
        rstd = lax.rsqrt(jnp.mean(xv * xv, axis=-1, keepdims=True) + RMS_EPS)
        xhat = xv * rstd
        gy = dy * g_ref[...]
        dx = rstd * (gy - xhat * jnp.mean(gy * xhat, axis=-1, keepdims=True))
        if has_res:
            dx = dx + res_ref[...]
        dx_ref[...] = dx.astype(dx_ref.dtype)
        part = jnp.sum(dy * xhat, axis=0, keepdims=True)

        @pl.when(i == 0)
        def _():
            dg_ref[...] = part

        @pl.when(i > 0)
        def _():
            dg_ref[...] += part

    row = pl.BlockSpec((tm, D), lambda i: (i, 0))
    vec = pl.BlockSpec((1, D), lambda i: (0, 0))
    ins = [xin, g] + list(dys) + ([res] if has_res else [])
    return pl.pallas_call(
        body, name=name, grid=(S // tm,),
        in_specs=[row, vec] + [row] * n_dy + ([row] if has_res else []),
        out_specs=[row, vec],
        out_shape=[jax.ShapeDtypeStruct((S, D), out_dtype), jax.ShapeDtypeStruct((1, D), F32)],
        compiler_params=_cparams(("arbitrary",)),
    )(*ins)


def _head_mask(hh, shape):
    lane = lax.broadcasted_iota(jnp.int32, shape, len(shape) - 1)
    return (lane // FOX_HEAD_DIM) == hh


def _block_constants(cr_ref, i, tq):
    first = cr_ref[0, :, pl.ds(pl.multiple_of(i * tq, tq), LANES)]
    return [first[hh:hh + 1, 0:1] for hh in range(2)]


def _attn_fwd(P, crow, *, tq, name):
    S = P.shape[0]
    tq = _tile(S, tq)
    nq = S // tq
    scale = FOX_HEAD_DIM ** -0.5

    def body(q_ref, k_ref, v_ref, cr_ref, o_ref, o32_ref, lse_ref):
        i = pl.program_id(1)
        q = q_ref[...] * jnp.asarray(scale, BF16)
        row = lax.broadcasted_iota(jnp.int32, (tq, tq), 0)
        col = lax.broadcasted_iota(jnp.int32, (tq, tq), 1)
        causal = col <= row
        hms = [_head_mask(hh, (tq, LANES)) for hh in range(2)]
        qhs = [jnp.where(hm, q, jnp.zeros_like(q)) for hm in hms]
        cbs = _block_constants(cr_ref, i, tq)

        def step(block, n_blocks, carry, masked):
            tk = n_blocks * tq
            start = pl.multiple_of(block * tq, tk)
            kj = k_ref[pl.ds(start, tk), :]
            vj = v_ref[pl.ds(start, tk), :]
            lane_k = lax.broadcasted_iota(jnp.int32, (tk, LANES), 1)
            new = []
            for hh in range(2):
                m, acc = carry[hh]
                bias = cbs[hh] - cr_ref[0, hh:hh + 1, pl.ds(start, tk)]
                s = lax.dot_general(qhs[hh], kj, NT_DIMS, preferred_element_type=F32) + bias
                if masked:
                    s = jnp.where(causal, s, NEG_BIG)
                m_new = jnp.maximum(m, jnp.max(s, axis=1, keepdims=True))
                alpha = jnp.exp(m - m_new)
                p = jnp.exp(s - m_new)
                vh = jnp.where(lane_k == (1 - hh) * FOX_HEAD_DIM, jnp.ones_like(vj), vj)
                acc = alpha * acc + jnp.dot(p.astype(BF16), vh, preferred_element_type=F32)
                new.append((m_new, acc))
            return tuple(new)

        lane = lax.broadcasted_iota(jnp.int32, (tq, LANES), 1)
        ones_lane = [lane == (1 - hh) * FOX_HEAD_DIM for hh in range(2)]
        init = (jnp.full((tq, 1), NEG_BIG, F32), jnp.zeros((tq, LANES), F32))
        n_wide = i // KV_STEP_FWD
        carry = lax.fori_loop(0, n_wide, lambda j, c: step(j * KV_STEP_FWD, KV_STEP_FWD, c, False), (init, init))
        carry = lax.fori_loop(n_wide * KV_STEP_FWD, i, lambda j, c: step(j, 1, c, False), carry)
        carry = step(i, 1, carry, True)
        ls = [jnp.sum(jnp.where(ones_lane[hh], carry[hh][1], 0.0), axis=1, keepdims=True) for hh in range(2)]
        out = jnp.where(hms[0], carry[0][1] / ls[0], carry[1][1] / ls[1])
        for hh in range(2):
            lse_ref[hh] = carry[hh][0] + jnp.log(ls[hh])
        o_ref[...] = out.astype(o_ref.dtype)
        o32_ref[...] = out

    nblk = D_MODEL // LANES
    return pl.pallas_call(
        body, name=name, grid=(HEAD_PAIRS, nq),
        in_specs=[
            pl.BlockSpec((tq, LANES), lambda p, i: (i, PB_Q * nblk + p)),
            pl.BlockSpec((S, LANES), lambda p, i: (0, PB_K * nblk + p)),
            pl.BlockSpec((S, LANES), lambda p, i: (0, PB_V * nblk + p)),
            pl.BlockSpec((1, 2, S), lambda p, i: (p, 0, 0)),
        ],
        out_specs=[
            pl.BlockSpec((tq, LANES), lambda p, i: (i, p)),
            pl.BlockSpec((tq, LANES), lambda p, i: (i, p)),
            pl.BlockSpec((2, tq, 1), lambda p, i: (p, i, 0)),
        ],
        out_shape=[jax.ShapeDtypeStruct((S, D_MODEL), BF16), jax.ShapeDtypeStruct((S, D_MODEL), F32),
                   jax.ShapeDtypeStruct((FOX_HEADS, S, 1), F32)],
        compiler_params=_cparams(("parallel", "arbitrary"), 56),
    )(P, P, P, crow)


def _attn_bwd(P, do, o32, lse, crow, *, tq, name):
    S = P.shape[0]
    tq = _tile(S, tq)
    nq = S // tq
    scale = FOX_HEAD_DIM ** -0.5

    def body(q_ref, k_ref, v_ref, do_ref, o_ref, lse_ref, cr_ref, dq_ref, dk_out, dv_out, dc_ref, dcq_ref, dk_ref, dv_ref):
        i = pl.program_id(1)

        @pl.when(i == 0)
        def _():
            dk_ref[...] = jnp.zeros_like(dk_ref)
            dv_ref[...] = jnp.zeros_like(dv_ref)
            dc_ref[...] = jnp.zeros_like(dc_ref)

        q = q_ref[...] * jnp.asarray(scale, BF16)
        do_v = do_ref[...]
        row = lax.broadcasted_iota(jnp.int32, (tq, tq), 0)
        col = lax.broadcasted_iota(jnp.int32, (tq, tq), 1)
        causal = col <= row
        hms = [_head_mask(hh, (tq, LANES)) for hh in range(2)]
        qhs = [jnp.where(hm, q, jnp.zeros_like(q)) for hm in hms]
        dohs = [jnp.where(hm, do_v, jnp.zeros_like(do_v)) for hm in hms]
        cbs = _block_constants(cr_ref, i, tq)
        lses = [lse_ref[hh] for hh in range(2)]
        prod = do_v.astype(F32) * o_ref[...]
        dls = [jnp.sum(jnp.where(hm, prod, 0.0), axis=1, keepdims=True) for hm in hms]

        def step(block, n_blocks, carry, masked):
            dq_acc, row_sums = carry[0], list(carry[1:])
            tk = n_blocks * tq
            start = pl.multiple_of(block * tq, tk)
            kj = k_ref[pl.ds(start, tk), :]
            vj = v_ref[pl.ds(start, tk), :]
            head_k = [_head_mask(hh, (tk, LANES)) for hh in range(2)]
            dv_part = jnp.zeros((tk, LANES), F32)
            dk_part = jnp.zeros((tk, LANES), F32)
            for hh in range(2):
                bias = cbs[hh] - cr_ref[0, hh:hh + 1, pl.ds(start, tk)]
                s = lax.dot_general(qhs[hh], kj, NT_DIMS, preferred_element_type=F32) + bias
                if masked:
                    s = jnp.where(causal, s, NEG_BIG)
                p = jnp.exp(s - lses[hh])
                dp = lax.dot_general(dohs[hh], vj, NT_DIMS, preferred_element_type=F32)
                ds = p * (dp - dls[hh])
                pb = p.astype(BF16)
                dsb = ds.astype(BF16)
                dv_part = dv_part + lax.dot_general(pb, dohs[hh], TN_DIMS, preferred_element_type=F32)
                dk_part = dk_part + lax.dot_general(dsb, qhs[hh], TN_DIMS, preferred_element_type=F32)
                dc_ref[0, hh:hh + 1, pl.ds(start, tk)] -= jnp.sum(ds, axis=0, keepdims=True)
                row_sums[hh] = row_sums[hh] + jnp.sum(ds, axis=1, keepdims=True)
                kh = jnp.where(head_k[hh], kj, jnp.zeros_like(kj))
                dq_acc = dq_acc + jnp.dot(dsb, kh, preferred_element_type=F32)
            dv_ref[pl.ds(start, tk), :] += dv_part
            dk_ref[pl.ds(start, tk), :] += dk_part
            return (dq_acc, *row_sums)

        zero_col = jnp.zeros((tq, 1), F32)
        n_wide = i // KV_STEP
        carry = lax.fori_loop(0, n_wide, lambda j, c: step(j * KV_STEP, KV_STEP, c, False),
                              (jnp.zeros((tq, LANES), F32), zero_col, zero_col))
        carry = lax.fori_loop(n_wide * KV_STEP, i, lambda j, c: step(j, 1, c, False), carry)
        carry = step(i, 1, carry, True)
        dq_ref[...] = (carry[0] * scale).astype(dq_ref.dtype)
        for hh in range(2):
            dcq_ref[0, hh:hh + 1, :] = jnp.transpose(jnp.broadcast_to(carry[1 + hh], (tq, LANES)))[0:1, :]

        @pl.when(i == nq - 1)
        def _():
            dk_out[...] = dk_ref[...].astype(dk_out.dtype)
            dv_out[...] = dv_ref[...].astype(dv_out.dtype)

    nblk = D_MODEL // LANES
    qblk = pl.BlockSpec((tq, LANES), lambda p, i: (i, p))
    stat = pl.BlockSpec((2, tq, 1), lambda p, i: (p, i, 0))
    full = pl.BlockSpec((S, LANES), lambda p, i: (0, p))
    return pl.pallas_call(
        body, name=name, grid=(HEAD_PAIRS, nq),
        in_specs=[
            pl.BlockSpec((tq, LANES), lambda p, i: (i, PB_Q * nblk + p)),
            pl.BlockSpec((S, LANES), lambda p, i: (0, PB_K * nblk + p)),
            pl.BlockSpec((S, LANES), lambda p, i: (0, PB_V * nblk + p)),
            qblk, qblk, stat,
            pl.BlockSpec((1, 2, S), lambda p, i: (p, 0, 0)),
        ],
        out_specs=[qblk, full, full, pl.BlockSpec((1, 2, S), lambda p, i: (p, 0, 0)),
                   pl.BlockSpec((1, 2, tq), lambda p, i: (p, 0, i))],
        out_shape=[
            jax.ShapeDtypeStruct((S, D_MODEL), BF16),
            jax.ShapeDtypeStruct((S, D_MODEL), BF16),
            jax.ShapeDtypeStruct((S, D_MODEL), BF16),
            jax.ShapeDtypeStruct((HEAD_PAIRS, 2, S), F32),
            jax.ShapeDtypeStruct((HEAD_PAIRS, 2, S), F32),
        ],
        scratch_shapes=[pltpu.VMEM((S, LANES), F32), pltpu.VMEM((S, LANES), F32)],
        compiler_params=_cparams(("parallel", "arbitrary"), 56),
    )(P, P, P, do, o32, lse, crow)


def _cumsum_lanes(xs, *, reverse, name):
    R, S = xs[0].shape
    nb = S // LANES
    n_in = len(xs)

    def body(*refs):
        x_refs, o_ref = refs[:n_in], refs[n_in]
        r = lax.broadcasted_iota(jnp.int32, (LANES, LANES), 0)
        c = lax.broadcasted_iota(jnp.int32, (LANES, LANES), 1)
        tri = ((r >= c) if reverse else (r <= c)).astype(F32)

        def step(b, carry):
            blk = (nb - 1 - b) if reverse else b
            start = pl.multiple_of(blk * LANES, LANES)
            xb = x_refs[0][:, pl.ds(start, LANES)]
            for r in x_refs[1:]:
                xb = xb + r[:, pl.ds(start, LANES)]
            y = jnp.dot(xb, tri, precision=lax.Precision.HIGHEST, preferred_element_type=F32) + carry
            o_ref[:, pl.ds(start, LANES)] = y
            return carry + jnp.sum(xb, axis=1, keepdims=True)

        lax.fori_loop(0, nb, step, jnp.zeros((R, 1), F32))

    return pl.pallas_call(
        body, name=name,
        in_specs=[pl.BlockSpec(memory_space=pltpu.VMEM)] * n_in,
        out_specs=pl.BlockSpec(memory_space=pltpu.VMEM),
        out_shape=jax.ShapeDtypeStruct((R, S), F32),
    )(*xs)


def _forget_fwd(f_logit, b_pad, *, name, tm=1024):
    S = f_logit.shape[0]
    tm = _tile(S, tm)

    def body(f_ref, b_ref, o_ref):
        z = f_ref[...] + b_ref[...]
        o_ref[...] = jnp.minimum(z, 0.0) - jnp.log(1.0 + jnp.exp(-jnp.abs(z)))

    blk = pl.BlockSpec((tm, LANES), lambda i: (i, 0))
    return pl.pallas_call(
        body, name=name, grid=(S // tm,),
        in_specs=[blk, pl.BlockSpec((1, LANES), lambda i: (0, 0))],
        out_specs=blk, out_shape=jax.ShapeDtypeStruct((S, LANES), F32),
        compiler_params=_cparams(("parallel",)),
    )(f_logit, b_pad)


def _forget_bwd(f_logit, b_pad, dlogf, *, name, tm=1024):
    S = f_logit.shape[0]
    tm = _tile(S, tm)

    def body(f_ref, b_ref, d_ref, o_ref, db_ref):
        i = pl.program_id(0)
        z = f_ref[...] + b_ref[...]
        dz = d_ref[...] * (1.0 - _sigmoid(z))
        o_ref[...] = dz.astype(o_ref.dtype)
        part = jnp.sum(dz, axis=0, keepdims=True)

        @pl.when(i == 0)
        def _():
            db_ref[...] = part

        @pl.when(i > 0)
        def _():
            db_ref[...] += part

    blk = pl.BlockSpec((tm, LANES), lambda i: (i, 0))
    vec = pl.BlockSpec((1, LANES), lambda i: (0, 0))
    return pl.pallas_call(
        body, name=name, grid=(S // tm,),
        in_specs=[blk, vec, blk], out_specs=[blk, vec],
        out_shape=[jax.ShapeDtypeStruct((S, LANES), BF16), jax.ShapeDtypeStruct((1, LANES), F32)],
        compiler_params=_cparams(("arbitrary",)),
    )(f_logit, b_pad, dlogf)


def _layernorm_stats(y):
    mu = jnp.mean(y, axis=-1, keepdims=True)
    yc = y - mu
    rstd = lax.rsqrt(jnp.mean(yc * yc, axis=-1, keepdims=True) + LN_EPS)
    return yc * rstd, rstd


def _fill_shifted(src_ref, sh_ref, tm):
    rows = tm + CONV_HALO - SUBLANES
    for s in range(1, SUBLANES):
        sh_ref[s - 1, 0:rows, :] = src_ref[pl.ds(s, rows), :]


def _shifted_rows(src_ref, sh_ref, r0, offset):
    s, base = offset % SUBLANES, offset - offset % SUBLANES
    rows = pl.ds(pl.multiple_of(r0 + base, SUBLANES), CONV_CHUNK)
    return src_ref[rows, :] if s == 0 else sh_ref[s - 1, rows, :]


def _fill_tap_rows(w_ref, wb_ref):
    for k in range(CONV_WIDTH):
        for j in range(N_DEV):
            wb_ref[k * SUBLANES:(k + 1) * SUBLANES, j * LANES:(j + 1) * LANES] = jnp.broadcast_to(
                w_ref[j, k:k + 1, :], (SUBLANES, LANES))


def _tap_sum(wb_ref, rows_of_tap):
    n = CONV_CHUNK // SUBLANES
    accs = [None] * n
    for k in range(CONV_WIDTH):
        wk = wb_ref[k * SUBLANES:(k + 1) * SUBLANES, :]
        src = rows_of_tap(k)
        for h in range(n):
            term = wk * src[h * SUBLANES:(h + 1) * SUBLANES]
            accs[h] = term if accs[h] is None else accs[h] + term
    return jnp.concatenate(accs, axis=0)


def _conv_fwd(P, w_pad, conv_b, ln_g, ln_b, *, name, tm=256):
    S = P.shape[0]
    C = D_MODEL
    tm = _tile(S, tm)
    hb = tm // CONV_HALO

    def body(a_ref, gt_ref, ah_ref, gh_ref, w_ref, cb_ref, g_ref, b_ref, o_ref, y_ref, glu_ref, sh_ref, wb_ref):
        i = pl.program_id(0)
        halo = ah_ref[...].astype(F32) * _sigmoid(gh_ref[...].astype(F32))
        glu_ref[0:CONV_HALO, :] = jnp.where(i > 0, halo, 0.0)
        glu_ref[CONV_HALO:, :] = a_ref[...].astype(F32) * _sigmoid(gt_ref[...].astype(F32))
        _fill_shifted(glu_ref, sh_ref, tm)
        _fill_tap_rows(w_ref, wb_ref)

        def chunk(r, carry):
            r0 = pl.multiple_of(r * CONV_CHUNK, CONV_CHUNK)
            acc = _tap_sum(wb_ref, lambda k: _shifted_rows(glu_ref, sh_ref, r0, CONV_HALO - (CONV_WIDTH - 1) + k))
            y_ref[pl.ds(r0, CONV_CHUNK), :] = acc + cb_ref[...]
            return carry

        lax.fori_loop(0, tm // CONV_CHUNK, chunk, 0)
        xhat, _ = _layernorm_stats(y_ref[...])
        z = xhat * g_ref[...] + b_ref[...]
        o_ref[...] = (z * _sigmoid(z)).astype(o_ref.dtype)

    vec = pl.BlockSpec((1, C), lambda i: (0, 0))
    row = pl.BlockSpec((tm, C), lambda i: (i, 0))
    return pl.pallas_call(
        body, name=name, grid=(S // tm,),
        in_specs=[
            pl.BlockSpec((tm, C), lambda i: (i, PB_A)),
            pl.BlockSpec((tm, C), lambda i: (i, PB_GATE)),
            pl.BlockSpec((CONV_HALO, C), lambda i: (jnp.maximum(i * hb - 1, 0), PB_A)),
            pl.BlockSpec((CONV_HALO, C), lambda i: (jnp.maximum(i * hb - 1, 0), PB_GATE)),
            pl.BlockSpec((N_DEV, CONV_WIDTH, LANES), lambda i: (0, 0, 0)),
            vec, vec, vec,
        ],
        out_specs=[row, row],
        out_shape=[jax.ShapeDtypeStruct((S, C), BF16), jax.ShapeDtypeStruct((S, C), F32)],
        scratch_shapes=[pltpu.VMEM((tm + CONV_HALO, C), F32), pltpu.VMEM((SUBLANES - 1, tm + CONV_HALO, C), F32),
                        pltpu.VMEM((CONV_HALO * SUBLANES, C), F32)],
        compiler_params=_cparams(("parallel",), 56),
    )(P, P, P, P, w_pad, conv_b, ln_g, ln_b)


def _conv_bwd(P, do, y, w_pad, ln_g, ln_b, *, name, tm=256):
    S = P.shape[0]
    C = D_MODEL
    tm = _tile(S, tm)
    hb = tm // CONV_HALO
    n_tiles = S // tm
    last_halo = S // CONV_HALO - 1

    def body(a_ref, gt_ref, ah_ref, gh_ref, do_ref, y_ref, don_ref, yn_ref, w_ref, g_ref, b_ref,
             dglu_ref, dw_ref, small_ref, glu_ref, dy_ref, gsh_ref, dsh_ref, dwacc_ref, wb_ref):
        i = pl.program_id(0)

        @pl.when(i == 0)
        def _():
            dwacc_ref[...] = jnp.zeros_like(dwacc_ref)
            small_ref[...] = jnp.zeros_like(small_ref)

        def ln_bwd(do_v, y_v):
            xhat, rstd = _layernorm_stats(y_v)
            z = xhat * g_ref[...] + b_ref[...]
            sg = _sigmoid(z)
            dz = do_v * (sg * (1.0 + z * (1.0 - sg)))
            dxh = dz * g_ref[...]
            dy = rstd * (dxh - jnp.mean(dxh, axis=-1, keepdims=True) - xhat * jnp.mean(dxh * xhat, axis=-1, keepdims=True))
            return dy, dz, xhat

        dy, dz, xhat = ln_bwd(do_ref[...], y_ref[...])
        dy_next, _, _ = ln_bwd(don_ref[...], yn_ref[...])
        small_ref[0:1, :] += jnp.sum(dz * xhat, axis=0, keepdims=True)
        small_ref[1:2, :] += jnp.sum(dz, axis=0, keepdims=True)
        small_ref[2:3, :] += jnp.sum(dy, axis=0, keepdims=True)
        dy_ref[0:tm, :] = dy
        dy_ref[tm:, :] = jnp.where(i < n_tiles - 1, dy_next, 0.0)

        halo = ah_ref[...].astype(F32) * _sigmoid(gh_ref[...].astype(F32))
        glu_ref[0:CONV_HALO, :] = jnp.where(i > 0, halo, 0.0)
        glu_ref[CONV_HALO:, :] = a_ref[...].astype(F32) * _sigmoid(gt_ref[...].astype(F32))
        _fill_shifted(glu_ref, gsh_ref, tm)
        _fill_shifted(dy_ref, dsh_ref, tm)
        _fill_tap_rows(w_ref, wb_ref)

        def chunk(r, carry):
            r0 = pl.multiple_of(r * CONV_CHUNK, CONV_CHUNK)
            rows = pl.ds(r0, CONV_CHUNK)
            dyc = dy_ref[rows, :]
            for k in range(CONV_WIDTH):
                prod = dyc * _shifted_rows(glu_ref, gsh_ref, r0, CONV_HALO - (CONV_WIDTH - 1) + k)
                dwacc_ref[k * SUBLANES:(k + 1) * SUBLANES, :] += prod[0:SUBLANES] + prod[SUBLANES:]
            dg = _tap_sum(wb_ref, lambda k: _shifted_rows(dy_ref, dsh_ref, r0, CONV_WIDTH - 1 - k))
            a = a_ref[rows, :].astype(F32)
            sig = _sigmoid(gt_ref[rows, :].astype(F32))
            dglu_ref[rows, 0:C] = (dg * sig).astype(dglu_ref.dtype)
            dglu_ref[rows, C:] = (dg * a * sig * (1.0 - sig)).astype(dglu_ref.dtype)
            return carry

        lax.fori_loop(0, tm // CONV_CHUNK, chunk, 0)

        @pl.when(i == n_tiles - 1)
        def _():
            dw_ref[...] = jnp.zeros_like(dw_ref)
            for k in range(CONV_WIDTH):
                row_k = jnp.sum(dwacc_ref[k * SUBLANES:(k + 1) * SUBLANES, :], axis=0, keepdims=True)
                for j in range(N_DEV):
                    dw_ref[j, k:k + 1, :] = row_k[:, j * LANES:(j + 1) * LANES]

    vec = pl.BlockSpec((1, C), lambda i: (0, 0))
    row = pl.BlockSpec((tm, C), lambda i: (i, 0))
    nxt = pl.BlockSpec((CONV_HALO, C), lambda i: (jnp.minimum((i + 1) * hb, last_halo), 0))
    return pl.pallas_call(
        body, name=name, grid=(n_tiles,),
        in_specs=[
            pl.BlockSpec((tm, C), lambda i: (i, PB_A)),
            pl.BlockSpec((tm, C), lambda i: (i, PB_GATE)),
            pl.BlockSpec((CONV_HALO, C), lambda i: (jnp.maximum(i * hb - 1, 0), PB_A)),
            pl.BlockSpec((CONV_HALO, C), lambda i: (jnp.maximum(i * hb - 1, 0), PB_GATE)),
            row, row, nxt, nxt,
            pl.BlockSpec((N_DEV, CONV_WIDTH, LANES), lambda i: (0, 0, 0)),
            vec, vec,
        ],
        out_specs=[
            pl.BlockSpec((tm, 2 * C), lambda i: (i, 0)),
            pl.BlockSpec((N_DEV, CONV_HALO, LANES), lambda i: (0, 0, 0)),
            pl.BlockSpec((8, C), lambda i: (0, 0)),
        ],
        out_shape=[
            jax.ShapeDtypeStruct((S, 2 * C), BF16),
            jax.ShapeDtypeStruct((N_DEV, CONV_HALO, LANES), F32),
            jax.ShapeDtypeStruct((8, C), F32),
        ],
        scratch_shapes=[
            pltpu.VMEM((tm + CONV_HALO, C), F32), pltpu.VMEM((tm + CONV_HALO, C), F32),
            pltpu.VMEM((SUBLANES - 1, tm + CONV_HALO, C), F32), pltpu.VMEM((SUBLANES - 1, tm + CONV_HALO, C), F32),
            pltpu.VMEM((CONV_HALO * SUBLANES, C), F32), pltpu.VMEM((CONV_HALO * SUBLANES, C), F32),
        ],
        compiler_params=_cparams(("arbitrary",), 56),
    )(P, P, P, P, do, y, do, y, w_pad, ln_g, ln_b)


def _mem_softmax(qh, kh):
    s = lax.dot_general(qh, kh, NT_DIMS, preferred_element_type=F32)
    e = jnp.exp(s - jnp.max(s, axis=1, keepdims=True))
    return e / jnp.sum(e, axis=1, keepdims=True)


def _mem_fwd(P, kv, *, name, tm=512):
    S, M = P.shape[0], kv.shape[0]
    tm = _tile(S, tm)
    scale = MEM_HEAD_DIM ** -0.5

    def body(q_ref, k_ref, v_ref, o_ref):
        for h in range(MEM_HEADS):
            sl = slice(h * MEM_HEAD_DIM, (h + 1) * MEM_HEAD_DIM)
            qh = q_ref[:, sl] * jnp.asarray(scale, BF16)
            p = _mem_softmax(qh, k_ref[:, sl])
            o_ref[:, sl] = jnp.dot(p.astype(BF16), v_ref[:, sl], preferred_element_type=F32).astype(o_ref.dtype)

    return pl.pallas_call(
        body, name=name, grid=(S // tm,),
        in_specs=[
            pl.BlockSpec((tm, D_MODEL), lambda i: (i, PB_QMEM)),
            pl.BlockSpec((M, D_MODEL), lambda i: (0, 0)),
            pl.BlockSpec((M, D_MODEL), lambda i: (0, 1)),
        ],
        out_specs=pl.BlockSpec((tm, D_MODEL), lambda i: (i, 0)),
        out_shape=jax.ShapeDtypeStruct((S, D_MODEL), BF16),
        compiler_params=_cparams(("parallel",)),
    )(P, kv, kv)


def _mem_bwd(P, do, kv, *, name, tm=512):
    S, M = P.shape[0], kv.shape[0]
    tm = _tile(S, tm)
    scale = MEM_HEAD_DIM ** -0.5

    def body(q_ref, k_ref, v_ref, do_ref, dq_ref, dkv_ref):
        i = pl.program_id(0)

        @pl.when(i == 0)
        def _():
            dkv_ref[...] = jnp.zeros_like(dkv_ref)

        for h in range(MEM_HEADS):
            sl = slice(h * MEM_HEAD_DIM, (h + 1) * MEM_HEAD_DIM)
            slv = slice(D_MODEL + h * MEM_HEAD_DIM, D_MODEL + (h + 1) * MEM_HEAD_DIM)
            qh = q_ref[:, sl] * jnp.asarray(scale, BF16)
            kh, vh, doh = k_ref[:, sl], v_ref[:, sl], do_ref[:, sl]
            p = _mem_softmax(qh, kh)
            dp = lax.dot_general(doh, vh, NT_DIMS, preferred_element_type=F32)
            ds = p * (dp - jnp.sum(p * dp, axis=1, keepdims=True))
            dsb = ds.astype(BF16)
            dq_ref[:, sl] = (jnp.dot(dsb, kh, preferred_element_type=F32) * scale).astype(dq_ref.dtype)
            dkv_ref[:, sl] += lax.dot_general(dsb, qh, TN_DIMS, preferred_element_type=F32)
            dkv_ref[:, slv] += lax.dot_general(p.astype(BF16), doh, TN_DIMS, preferred_element_type=F32)

    row = pl.BlockSpec((tm, D_MODEL), lambda i: (i, 0))
    return pl.pallas_call(
        body, name=name, grid=(S // tm,),
        in_specs=[
            pl.BlockSpec((tm, D_MODEL), lambda i: (i, PB_QMEM)),
            pl.BlockSpec((M, D_MODEL), lambda i: (0, 0)),
            pl.BlockSpec((M, D_MODEL), lambda i: (0, 1)),
            row,
        ],
        out_specs=[row, pl.BlockSpec((M, 2 * D_MODEL), lambda i: (0, 0))],
        out_shape=[jax.ShapeDtypeStruct((S, D_MODEL), BF16), jax.ShapeDtypeStruct((M, 2 * D_MODEL), F32)],
        compiler_params=_cparams(("arbitrary",)),
    )(P, kv, kv, do)


def _merge_fwd(o_attn, o_conv, o_mem, P, wpa, wpc, wpm, w_out, *, name, tm=256):
    S = P.shape[0]
    D = D_MODEL
    tm = _tile(S, tm)

    def body(oa_ref, oc_ref, om_ref, g0_ref, g1_ref, g2_ref, wa_ref, wc_ref, wm_ref, wo_ref, mg_ref, y_ref, pb_ref):
        merged = jnp.zeros((tm, D), F32)
        for b, (o_ref, g_ref, w_ref) in enumerate(((oa_ref, g0_ref, wa_ref), (oc_ref, g1_ref, wc_ref), (om_ref, g2_ref, wm_ref))):
            pb = jnp.dot(o_ref[...], w_ref[...], preferred_element_type=F32)
            pb_ref[b] = pb.astype(pb_ref.dtype)
            merged = merged + _sigmoid(g_ref[...].astype(F32)) * pb
        mb = merged.astype(BF16)
        mg_ref[...] = mb
        y_ref[...] = jnp.dot(mb, wo_ref[...], preferred_element_type=F32)

    row = pl.BlockSpec((tm, D), lambda i: (i, 0))
    wsp = pl.BlockSpec((D, D), lambda i: (0, 0))
    return pl.pallas_call(
        body, name=name, grid=(S // tm,),
        in_specs=[row, row, row] + [pl.BlockSpec((tm, D), lambda i, b=b: (i, PB_G0 + b)) for b in range(3)] + [wsp] * 4,
        out_specs=[row, row, pl.BlockSpec((3, tm, D), lambda i: (0, i, 0))],
        out_shape=[jax.ShapeDtypeStruct((S, D), BF16), jax.ShapeDtypeStruct((S, D), F32), jax.ShapeDtypeStruct((3, S, D), BF16)],
        compiler_params=_cparams(("parallel",), 56),
    )(o_attn, o_conv, o_mem, P, P, P, wpa, wpc, wpm, w_out)


def _merge_bwd(dy1, w_out, P, pb, *, name, tm=256):
    S = P.shape[0]
    D = D_MODEL
    tm = _tile(S, tm)

    def body(dy_ref, wo_ref, g0_ref, g1_ref, g2_ref, pb_ref, d0_ref, d1_ref, d2_ref, dgl_ref):
        dm = lax.dot_general(dy_ref[...], wo_ref[...], NT_DIMS, preferred_element_type=F32)
        for b, (g_ref, d_ref) in enumerate(((g0_ref, d0_ref), (g1_ref, d1_ref), (g2_ref, d2_ref))):
            g = _sigmoid(g_ref[...].astype(F32))
            d_ref[...] = (dm * g).astype(d_ref.dtype)
            dgl_ref[:, b * D:(b + 1) * D] = (dm * pb_ref[b].astype(F32) * g * (1.0 - g)).astype(dgl_ref.dtype)

    row = pl.BlockSpec((tm, D), lambda i: (i, 0))
    blk3 = pl.BlockSpec((3, tm, D), lambda i: (0, i, 0))
    sds = jax.ShapeDtypeStruct((S, D), BF16)
    return pl.pallas_call(
        body, name=name, grid=(S // tm,),
        in_specs=[row, pl.BlockSpec((D, D), lambda i: (0, 0))]
        + [pl.BlockSpec((tm, D), lambda i, b=b: (i, PB_G0 + b)) for b in range(3)] + [blk3],
        out_specs=[row, row, row, pl.BlockSpec((tm, 3 * D), lambda i: (i, 0))],
        out_shape=[sds, sds, sds, jax.ShapeDtypeStruct((S, 3 * D), BF16)],
        compiler_params=_cparams(("parallel",), 56),
    )(dy1, w_out, P, P, P, pb)


def _resid_norm(x, y1, g_post, g_pre, *, name, tm=512):
    S, D = x.shape
    tm = _tile(S, tm)

    def body(x_ref, y_ref, gp_ref, gq_ref, x1_ref, h2_ref):
        yv = y_ref[...]
        x1 = x_ref[...] + yv * lax.rsqrt(jnp.mean(yv * yv, axis=-1, keepdims=True) + RMS_EPS) * gp_ref[...]
        x1_ref[...] = x1
        h2_ref[...] = (x1 * lax.rsqrt(jnp.mean(x1 * x1, axis=-1, keepdims=True) + RMS_EPS) * gq_ref[...]).astype(h2_ref.dtype)

    row = pl.BlockSpec((tm, D), lambda i: (i, 0))
    vec = pl.BlockSpec((1, D), lambda i: (0, 0))
    return pl.pallas_call(
        body, name=name, grid=(S // tm,),
        in_specs=[row, row, vec, vec], out_specs=[row, row],
        out_shape=[jax.ShapeDtypeStruct((S, D), F32), jax.ShapeDtypeStruct((S, D), BF16)],
        compiler_params=_cparams(("parallel",)),
    )(x, y1, g_post, g_pre)


def _swiglu_fwd(h2, w_gu_t, *, name, tm=512, tn=1408):
    S, D = h2.shape
    Fh = w_gu_t.shape[0] // 2
    tm, tn = _tile(S, tm), _tile(Fh, tn)
    nj = Fh // tn

    def body(h_ref, wg_ref, wu_ref, g_ref, u_ref, a_ref):
        hv = h_ref[...]
        g = lax.dot_general(hv, wg_ref[...], NT_DIMS, preferred_element_type=F32)
        u = lax.dot_general(hv, wu_ref[...], NT_DIMS, preferred_element_type=F32)
        g_ref[...] = g.astype(g_ref.dtype)
        u_ref[...] = u.astype(u_ref.dtype)
        a_ref[...] = (g * _sigmoid(g) * u).astype(a_ref.dtype)

    out = pl.BlockSpec((tm, tn), lambda j, i: (i, j))
    sds = jax.ShapeDtypeStruct((S, Fh), BF16)
    return pl.pallas_call(
        body, name=name, grid=(nj, S // tm),
        in_specs=[
            pl.BlockSpec((tm, D), lambda j, i: (i, 0)),
            pl.BlockSpec((tn, D), lambda j, i: (j, 0)),
            pl.BlockSpec((tn, D), lambda j, i: (j + nj, 0)),
        ],
        out_specs=[out, out, out], out_shape=[sds, sds, sds],
        compiler_params=_cparams(("parallel", "parallel"), 56),
    )(h2, w_gu_t, w_gu_t)


def _swiglu_bwd(dffn, w_down, g, u, *, name, tm=512, tn=1408):
    S, Fh = g.shape
    D = dffn.shape[1]
    tm, tn = _tile(S, tm), _tile(Fh, tn)

    def body(df_ref, w_ref, g_ref, u_ref, o_ref):
        dfv = df_ref[...]
        for j in range(Fh // tn):
            cols = slice(j * tn, (j + 1) * tn)
            da = lax.dot_general(dfv, w_ref[cols, :], NT_DIMS, preferred_element_type=F32)
            gv = g_ref[:, cols].astype(F32)
            sg = _sigmoid(gv)
            o_ref[:, cols] = (da * u_ref[:, cols].astype(F32) * (sg * (1.0 + gv * (1.0 - sg)))).astype(o_ref.dtype)
            o_ref[:, Fh + j * tn:Fh + (j + 1) * tn] = (da * gv * sg).astype(o_ref.dtype)

    act = pl.BlockSpec((tm, Fh), lambda i: (i, 0))
    return pl.pallas_call(
        body, name=name, grid=(S // tm,),
        in_specs=[pl.BlockSpec((tm, D), lambda i: (i, 0)), pl.BlockSpec((Fh, D), lambda i: (0, 0)), act, act],
        out_specs=pl.BlockSpec((tm, 2 * Fh), lambda i: (i, 0)),
        out_shape=jax.ShapeDtypeStruct((S, 2 * Fh), BF16),
        compiler_params=_cparams(("parallel",), 56),
    )(dffn, w_down, g, u)


def _final(x1, ffn, target, g, *, name, tm=512):
    S, D = x1.shape
    tm = _tile(S, tm)

    def body(x_ref, f_ref, t_ref, g_ref, dout_ref, dffn_ref, loss_ref, dg_ref):
        i = pl.program_id(0)
        fv = f_ref[...]
        rstd = lax.rsqrt(jnp.mean(fv * fv, axis=-1, keepdims=True) + RMS_EPS)
        r = fv * rstd
        e = x_ref[...] + r * g_ref[...] - t_ref[...]
        dout = e * (1.0 / D)
        dout_ref[...] = dout
        gy = dout * g_ref[...]
        dffn_ref[...] = (rstd * (gy - r * jnp.mean(gy * r, axis=-1, keepdims=True))).astype(dffn_ref.dtype)
        lpart = jnp.full((8, LANES), 0.5 * jnp.sum(jnp.mean(e * e, axis=-1, keepdims=True)), F32)
        gpart = jnp.sum(dout * r, axis=0, keepdims=True)

        @pl.when(i == 0)
        def _():
            loss_ref[...] = lpart
            dg_ref[...] = gpart

        @pl.when(i > 0)
        def _():
            loss_ref[...] += lpart
            dg_ref[...] += gpart

    row = pl.BlockSpec((tm, D), lambda i: (i, 0))
    vec = pl.BlockSpec((1, D), lambda i: (0, 0))
    return pl.pallas_call(
        body, name=name, grid=(S // tm,),
        in_specs=[row, row, row, vec],
        out_specs=[row, row, pl.BlockSpec((8, LANES), lambda i: (0, 0)), vec],
        out_shape=[jax.ShapeDtypeStruct((S, D), F32), jax.ShapeDtypeStruct((S, D), BF16),
                   jax.ShapeDtypeStruct((8, LANES), F32), jax.ShapeDtypeStruct((1, D), F32)],
        compiler_params=_cparams(("arbitrary",)),
    )(x1, ffn, target, g)


def _after(token, value):
    return value if token is None else value + token[0, 0].astype(value.dtype)


def _local_step(x, mem, target, gains, w, late_weights, send_grads, *, tq=512):
    S = x.shape[0]
    b_pad = jnp.pad(gains["b_forget"], ((0, 0), (0, LANES - FOX_HEADS)))
    gains = dict(gains)

    h = _rms_fwd(x, gains["norm_mix_pre"], name="rms_mix_pre")
    P = _matmul(h, w["wp_t"], mode="nt", out_dtype=BF16, name="proj_in", tm=2048)
    f_logit = _matmul(h, w["wf_t"], mode="nt", out_dtype=F32, name="proj_forget")
    logf = _forget_fwd(f_logit, b_pad, name="forget_fwd")
    c_row16 = _cumsum_lanes([logf[:, :FOX_HEADS].T], reverse=False, name="forget_cumsum")
    crow = c_row16.reshape(HEAD_PAIRS, 2, S)
    o_attn, o_attn32, lse = _attn_fwd(P, crow, tq=tq, name="attn_fwd")
    w = dict(w, **late_weights(o_attn))
    o_conv, y_conv = _conv_fwd(P, w["conv_w"], gains["conv_b"], gains["conv_ln_g"], gains["conv_ln_b"], name="conv_fwd")
    mem_n = _rms_fwd(mem, gains["norm_mem"], name="rms_mem")
    kv = _matmul(mem_n, w["w_kv"], mode="nn", out_dtype=BF16, name="mem_kv")
    o_mem = _mem_fwd(P, kv, name="mem_fwd")
    merged, y1, pb = _merge_fwd(o_attn, o_conv, o_mem, P, w["wpa"], w["wpc"], w["wpm"], w["w_out"], name="merge_fwd")
    x1, h2 = _resid_norm(x, y1, gains["norm_mix_post"], gains["norm_ffn_pre"], name="resid_norm")
    g_ffn, u_ffn, act = _swiglu_fwd(h2, w["w_gu_t"], name="swiglu_fwd")
    ffn = _matmul(act, w["w_down"], mode="nn", out_dtype=F32, name="ffn_down", tk=1408)

    dout, dffn, loss_tile, d_norm_ffn_post = _final(x1, ffn, target, gains["norm_ffn_post"], name="loss_head")
    d_w_down = _matmul(act, dffn, mode="tn", out_dtype=BF16, name="dw_down", tm=1408)
    dgu = _swiglu_bwd(dffn, w["w_down"], g_ffn, u_ffn, name="swiglu_bwd")
    dh2 = _matmul(dgu, w["w_gu_t"], mode="nn", out_dtype=F32, name="d_h2", tk=1408)
    d_w_gu_t = _matmul(dgu, h2, mode="tn", out_dtype=BF16, name="dw_gate_up", tm=1408)
    dx1, d_norm_ffn_pre = _rms_bwd(x1, [dh2], gains["norm_ffn_pre"], dout, out_dtype=F32, name="rms_ffn_pre_bwd")

    dy1, d_norm_mix_post = _rms_bwd(y1, [dx1], gains["norm_mix_post"], None, out_dtype=BF16, name="rms_mix_post_bwd")
    d_w_out = _matmul(merged, dy1, mode="tn", out_dtype=BF16, name="dw_out")
    *dpb, dgl = _merge_bwd(dy1, w["w_out"], P, pb, name="merge_bwd")
    do_attn = _matmul(dpb[0], w["wpa"], mode="nt", out_dtype=BF16, name="d_o_attn")
    do_conv = _matmul(dpb[1], w["wpc"], mode="nt", out_dtype=F32, name="d_o_conv")
    do_mem = _matmul(dpb[2], w["wpm"], mode="nt", out_dtype=BF16, name="d_o_mem")
    d_wpa = _matmul(o_attn, dpb[0], mode="tn", out_dtype=BF16, name="dw_proj_attn")
    d_wpc = _matmul(o_conv, dpb[1], mode="tn", out_dtype=BF16, name="dw_proj_conv")
    d_wpm = _matmul(o_mem, dpb[2], mode="tn", out_dtype=BF16, name="dw_proj_mem")

    dq_mem, dkv = _mem_bwd(P, do_mem, kv, name="mem_bwd")
    dkv_b = dkv.astype(BF16)
    d_w_kv = _matmul(mem_n, dkv_b, mode="tn", out_dtype=BF16, name="dw_kv")
    dmem_n = _matmul(dkv_b, w["w_kv"], mode="nt", out_dtype=F32, name="d_mem_n")
    _, d_norm_mem = _rms_bwd(mem, [dmem_n], gains["norm_mem"], None, out_dtype=BF16, name="rms_mem_bwd")

    dglu, d_conv_w, conv_small = _conv_bwd(P, do_conv, y_conv, w["conv_w"], gains["conv_ln_g"], gains["conv_ln_b"], name="conv_bwd")
    sent = send_grads("body", dict(conv_w=d_conv_w[:, :CONV_WIDTH].astype(BF16), w_kv_mem=d_w_kv, w_proj_attn=d_wpa,
                                   w_proj_conv=d_wpc, w_proj_mem=d_wpm, w_out=d_w_out, w_gate_up=d_w_gu_t, w_down=d_w_down))

    dq, dk, dv, dc, dcq = _attn_bwd(P, do_attn, o_attn32, lse, _after(sent, crow), tq=tq, name="attn_bwd")
    dlogf16 = _cumsum_lanes([dc.reshape(FOX_HEADS, S), dcq.reshape(FOX_HEADS, S)], reverse=True, name="forget_cumsum_bwd")
    dlogf = jnp.pad(dlogf16.T, ((0, 0), (0, LANES - FOX_HEADS)))
    df, d_b_pad = _forget_bwd(f_logit, b_pad, dlogf, name="forget_bwd")

    dP = jnp.concatenate([dq, dk, dv, dglu, dq_mem, dgl], axis=1)
    d_wp_t = _matmul(dP, h, mode="tn", out_dtype=BF16, name="dw_in", tm=1536)
    d_wf_t = _matmul(df, h, mode="tn", out_dtype=BF16, name="dw_forget")
    n_qkv = 3 * D_MODEL
    sent = send_grads("w_in", dict(w_in=jnp.concatenate([d_wp_t[:n_qkv], d_wf_t[:FOX_HEADS], d_wp_t[n_qkv:]], axis=0)))
    dh_p = _matmul(dP, w["wp_t"], mode="nn", out_dtype=F32, name="d_h", after=sent)
    dh_f = _matmul(df, w["wf_t"], mode="nn", out_dtype=F32, name="d_h_forget")
    grad_x, d_norm_mix_pre = _rms_bwd(x, [dh_p, dh_f], gains["norm_mix_pre"], dx1, out_dtype=F32, name="rms_mix_pre_bwd")

    small = dict(norm_mix_pre=d_norm_mix_pre, norm_mix_post=d_norm_mix_post, norm_mem=d_norm_mem,
                 conv_b=conv_small[2:3], conv_ln_g=conv_small[0:1], conv_ln_b=conv_small[1:2],
                 norm_ffn_pre=d_norm_ffn_pre, norm_ffn_post=d_norm_ffn_post, b_forget=d_b_pad[:, :FOX_HEADS])
    return loss_tile, grad_x, small


def _mesh_pos():
    return lax.axis_index("x"), lax.axis_index("y"), lax.axis_index("c")


def _flip(pos, d):
    x, y, c = pos
    return (1 - x if d & 4 else x, 1 - y if d & 2 else y, 1 - c if d & 1 else c)


def _flat(pos):
    x, y, c = pos
    return 4 * x + 2 * y + c


def _gather_two_level(src, *, name):
    def body(src_ref, out_ref, token, send_sems, recv_sems, local_sem):
        me = _mesh_pos()
        my = _flat(me)
        sibling = _flip(me, 1)
        far = [_flip(me, d) for d in (4, 2, 6)]

        def copy(k, block, to, own=False):
            return pltpu.make_async_remote_copy(
                src_ref=src_ref if own else out_ref.at[_flat(block)], dst_ref=out_ref.at[_flat(block)],
                send_sem=send_sems.at[k], recv_sem=recv_sems.at[k], device_id=to, device_id_type=pl.DeviceIdType.MESH)

        local = pltpu.make_async_copy(src_ref, out_ref.at[my], local_sem)
        local.start()
        first = [copy(0, me, sibling, own=True)] + [copy(1 + j, me, peer, own=True) for j, peer in enumerate(far)]
        for cp in first:
            cp.start()
        passed = [copy(4 + j, peer, sibling) for j, peer in enumerate(far)]
        for j, peer in enumerate(far):
            copy(1 + j, peer, me).wait_recv()
            passed[j].start()
        copy(0, sibling, me).wait_recv()
        for j, peer in enumerate(far):
            copy(4 + j, _flip(peer, 1), me).wait_recv()
        for cp in first + passed:
            cp.wait_send()
        local.wait()
        token[...] = jnp.zeros_like(token)

    n_copy = N_DEV - 1
    out, token = pl.pallas_call(
        body, name=name,
        in_specs=[pl.BlockSpec(memory_space=pl.ANY)],
        out_specs=[pl.BlockSpec(memory_space=pl.ANY), pl.BlockSpec(memory_space=pltpu.VMEM)],
        out_shape=[jax.ShapeDtypeStruct((N_DEV,) + tuple(src.shape), src.dtype), jax.ShapeDtypeStruct((8, LANES), F32)],
        scratch_shapes=[pltpu.SemaphoreType.DMA((n_copy,)), pltpu.SemaphoreType.DMA((n_copy,)), pltpu.SemaphoreType.DMA(())],
    )(src)
    return [out], token


def _exchange(srcs, *, scatter, name):
    T = len(srcs)
    n_peer = N_DEV - 1

    def body(*refs):
        src_refs, out_refs = refs[:T], refs[T:2 * T]
        token, send_sems, recv_sems, local_sems = refs[2 * T:]
        me = _mesh_pos()
        my = _flat(me)

        def copy(t, d):
            peer = _flip(me, d)
            return pltpu.make_async_remote_copy(
                src_ref=src_refs[t].at[_flat(peer)] if scatter else src_refs[t],
                dst_ref=out_refs[t].at[my],
                send_sem=send_sems.at[t * n_peer + d - 1], recv_sem=recv_sems.at[t * n_peer + d - 1],
                device_id=peer, device_id_type=pl.DeviceIdType.MESH)

        def arrival(t, d):
            peer = _flip(me, d)
            return pltpu.make_async_remote_copy(
                src_ref=src_refs[t].at[my] if scatter else src_refs[t],
                dst_ref=out_refs[t].at[_flat(peer)],
                send_sem=send_sems.at[t * n_peer + d - 1], recv_sem=recv_sems.at[t * n_peer + d - 1],
                device_id=peer, device_id_type=pl.DeviceIdType.MESH)

        local = [pltpu.make_async_copy(src_refs[t].at[my] if scatter else src_refs[t], out_refs[t].at[my], local_sems.at[t])
                 for t in range(T)]
        for cp in local:
            cp.start()
        sends = [copy(t, d) for d in range(1, N_DEV) for t in range(T)]
        for cp in sends:
            cp.start()
        for d in range(1, N_DEV):
            for t in range(T):
                arrival(t, d).wait_recv()
        for cp in sends:
            cp.wait_send()
        for cp in local:
            cp.wait()
        token[...] = jnp.zeros_like(token)

    outs = pl.pallas_call(
        body, name=name,
        in_specs=[pl.BlockSpec(memory_space=pl.ANY)] * T,
        out_specs=[pl.BlockSpec(memory_space=pl.ANY)] * T + [pl.BlockSpec(memory_space=pltpu.VMEM)],
        out_shape=[jax.ShapeDtypeStruct((N_DEV,) + tuple(s.shape[-2:]), s.dtype) for s in srcs] + [jax.ShapeDtypeStruct((8, LANES), F32)],
        scratch_shapes=[pltpu.SemaphoreType.DMA((T * n_peer,)), pltpu.SemaphoreType.DMA((T * n_peer,)), pltpu.SemaphoreType.DMA((T,))],
    )(*srcs)
    return list(outs[:T]), outs[T]


HBM_SPEC = pl.BlockSpec(memory_space=pltpu.HBM)
SEM_SPEC = pl.BlockSpec(memory_space=pltpu.SEMAPHORE)


def _split_copies(src_refs, land_refs, send_sems, recv_sems, scatter):
    me = _mesh_pos()
    my = _flat(me)
    n_peer = N_DEV - 1
    out = []
    for d in range(1, N_DEV):
        peer = _flip(me, d)
        for t, (src, land) in enumerate(zip(src_refs, land_refs)):
            k = t * n_peer + d - 1
            start = pltpu.make_async_remote_copy(
                src_ref=src.at[_flat(peer)] if scatter else src, dst_ref=land.at[my],
                send_sem=send_sems.at[k], recv_sem=recv_sems.at[k], device_id=peer, device_id_type=pl.DeviceIdType.MESH)
            arrive = pltpu.make_async_remote_copy(
                src_ref=src.at[my] if scatter else src, dst_ref=land.at[_flat(peer)],
                send_sem=send_sems.at[k], recv_sem=recv_sems.at[k], device_id=peer, device_id_type=pl.DeviceIdType.MESH)
            out.append((start, arrive))
    return out


def _local_copies(src_refs, land_refs, local_sems, scatter):
    my = _flat(_mesh_pos())
    return [pltpu.make_async_copy(src.at[my] if scatter else src, land.at[my], local_sems.at[t])
            for t, (src, land) in enumerate(zip(src_refs, land_refs))]


def _exchange_start(srcs, *, scatter, name):
    T = len(srcs)
    n_sem = T * (N_DEV - 1)
    lands = [lax.empty((N_DEV,) + tuple(s.shape[-2:]), s.dtype) for s in srcs]

    def body(*refs):
        src_refs, land_refs = refs[:T], refs[T:2 * T]
        send_sems, recv_sems, local_sems = refs[2 * T:2 * T + 3]
        token = refs[-1]
        for cp in _local_copies(src_refs, land_refs, local_sems, scatter):
            cp.start()
        for start, _ in _split_copies(src_refs, land_refs, send_sems, recv_sems, scatter):
            start.start()
        token[...] = jnp.zeros_like(token)

    hbm = lambda a: pltpu.HBM(a.shape, a.dtype)
    outs = pl.pallas_call(
        body, name=name,
        in_specs=[HBM_SPEC] * (2 * T),
        out_specs=(SEM_SPEC, SEM_SPEC, SEM_SPEC, *[HBM_SPEC] * (2 * T), pl.BlockSpec(memory_space=pltpu.VMEM)),
        out_shape=(pltpu.SemaphoreType.DMA((n_sem,)), pltpu.SemaphoreType.DMA((n_sem,)), pltpu.SemaphoreType.DMA((T,)),
                   *[hbm(s) for s in srcs], *[hbm(a) for a in lands], jax.ShapeDtypeStruct((8, LANES), F32)),
        input_output_aliases={t: 3 + t for t in range(2 * T)},
        compiler_params=pltpu.CompilerParams(has_side_effects=pltpu.SideEffectType.DATAFLOW_SIDE_EFFECTING),
    )(*[pltpu.with_memory_space_constraint(s, pltpu.HBM) for s in srcs],
      *[pltpu.with_memory_space_constraint(a, pltpu.HBM) for a in lands])
    return (outs[0], outs[1], outs[2], list(outs[3:3 + T]), list(outs[3 + T:3 + 2 * T])), outs[-1]


def _exchange_wait(handle, after, *, scatter, name):
    send_sems, recv_sems, local_sems, srcs, lands = handle
    T = len(srcs)

    def body(*refs):
        src_refs, land_refs = refs[:T], refs[T:2 * T]
        s_sems, r_sems, l_sems = refs[2 * T:2 * T + 3]
        for cp in _local_copies(src_refs, land_refs, l_sems, scatter):
            cp.wait()
        for start, arrive in _split_copies(src_refs, land_refs, s_sems, r_sems, scatter):
            start.wait_send()
            arrive.wait_recv()

    hbm = lambda a: pltpu.HBM(a.shape, a.dtype)
    outs = pl.pallas_call(
        body, name=name,
        in_specs=[HBM_SPEC] * (2 * T) + [SEM_SPEC, SEM_SPEC, SEM_SPEC, pl.BlockSpec(memory_space=pl.ANY)],
        out_specs=[HBM_SPEC] * (2 * T),
        out_shape=[hbm(s) for s in srcs] + [hbm(a) for a in lands],
        input_output_aliases={t: t for t in range(2 * T)},
        compiler_params=pltpu.CompilerParams(has_side_effects=pltpu.SideEffectType.DATAFLOW_SIDE_EFFECTING),
    )(*srcs, *lands, send_sems, recv_sems, local_sems, after)
    return list(outs[T:])


def _adamw(w, slots, m, v, *, name):
    R, C = w.shape
    row_tiles = [t for t in range(16, 513, 16) if R % t == 0]
    tr = R if R <= 512 or not row_tiles else max(row_tiles)
    tc = C if tr < R or R <= 512 else LANES
    assert R % tr == 0 and C % tc == 0, (R, C, tr, tc)

    def body(w_ref, s_ref, m_ref, v_ref, g_ref, d_ref, m2_ref, v2_ref):
        gv = s_ref[0].astype(F32)
        for j in range(1, N_DEV):
            gv = gv + s_ref[j].astype(F32)
        g_ref[...] = gv
        m2 = ADAM_B1 * m_ref[...] + (1.0 - ADAM_B1) * gv
        v2 = ADAM_B2 * v_ref[...] + (1.0 - ADAM_B2) * (gv * gv)
        m_hat = m2 / (1.0 - ADAM_B1 ** ADAM_STEP)
        v_hat = v2 / (1.0 - ADAM_B2 ** ADAM_STEP)
        d_ref[...] = -ADAM_LR * (m_hat / (jnp.sqrt(v_hat) + ADAM_EPS) + ADAM_WD * w_ref[...])
        m2_ref[...] = m2
        v2_ref[...] = v2

    blk = pl.BlockSpec((tr, tc), lambda i, j: (i, j))
    sds = jax.ShapeDtypeStruct((R, C), F32)
    return pl.pallas_call(
        body, name=name, grid=(R // tr, C // tc),
        in_specs=[blk, pl.BlockSpec((N_DEV, tr, tc), lambda i, j: (0, i, j)), blk, blk],
        out_specs=[blk] * 4, out_shape=[sds] * 4,
        compiler_params=_cparams(("parallel", "parallel")),
    )(w, slots, m, v)


SHARDED = (
    ("w_in", (1154, 1024), "row"), ("conv_w", (31, 128), "col"), ("w_kv_mem", (1024, 256), "col"),
    ("w_proj_attn", (128, 1024), "row"), ("w_proj_conv", (128, 1024), "row"), ("w_proj_mem", (128, 1024), "row"),
    ("w_out", (128, 1024), "row"), ("w_gate_up", (704, 1024), "row"), ("w_down", (352, 1024), "row"),
)
TRANSPOSED = ("w_in", "w_gate_up")
SMALL = ("norm_mix_pre", "norm_mix_post", "norm_mem", "conv_b", "conv_ln_g", "conv_ln_b", "norm_ffn_pre", "norm_ffn_post", "b_forget")
SMALL_ROWS = 16
LOSS_ROW = len(SMALL)
WEIGHT_ORDER = ("norm_mix_pre", "norm_mix_post", "norm_mem", "w_in", "b_forget", "conv_w", "conv_b", "conv_ln_g", "conv_ln_b",
                "w_kv_mem", "w_proj_attn", "w_proj_conv", "w_proj_mem", "w_out", "norm_ffn_pre", "norm_ffn_post", "w_gate_up", "w_down")


def _to_full(blocks8, kind):
    n, r, c = blocks8.shape
    if kind == "col":
        return jnp.concatenate([blocks8[j] for j in range(n)], axis=1)
    return blocks8.reshape(n * r, c)


def _to_blocks(full, kind):
    if kind == "col":
        c = full.shape[1] // N_DEV
        return jnp.stack([full[:, j * c:(j + 1) * c] for j in range(N_DEV)])
    nr, c = full.shape
    return full.reshape(N_DEV, nr // N_DEV, c)


def kernel(x, mem, norm_mix_pre, norm_mix_post, norm_mem, w_in, b_forget, conv_w, conv_b, conv_ln_g, conv_ln_b, w_kv_mem, w_proj_attn, w_proj_conv, w_proj_mem, w_out, norm_ffn_pre, norm_ffn_post, w_gate_up, w_down, loss_target, m_norm_mix_pre, m_norm_mix_post, m_norm_mem, m_w_in, m_b_forget, m_conv_w, m_conv_b, m_conv_ln_g, m_conv_ln_b, m_w_kv_mem, m_w_proj_attn, m_w_proj_conv, m_w_proj_mem, m_w_out, m_norm_ffn_pre, m_norm_ffn_post, m_w_gate_up, m_w_down, v_norm_mix_pre, v_norm_mix_post, v_norm_mem, v_w_in, v_b_forget, v_conv_w, v_conv_b, v_conv_ln_g, v_conv_ln_b, v_w_kv_mem, v_w_proj_attn, v_w_proj_conv, v_w_proj_mem, v_w_out, v_norm_ffn_pre, v_norm_ffn_post, v_w_gate_up, v_w_down):
    given = dict(locals())
    weights = {n: given[n] for n in WEIGHT_ORDER}
    moments_m = {n: given["m_" + n] for n in WEIGHT_ORDER}
    moments_v = {n: given["v_" + n] for n in WEIGHT_ORDER}
    kind = {name: k for name, _, k in SHARDED}
    names = [name for name, _, _ in SHARDED]

    def local(a, n):
        return jnp.swapaxes(a[0], 0, 1) if n in TRANSPOSED else a[0]

    blocks = {n: local(weights[n], n).astype(F32 if n == "conv_w" else BF16) for n in names}
    late = [n for n in names if n != "w_in"]
    (w_in_blocks,), w_in_token = _gather_two_level(blocks["w_in"], name="gather_w_in")
    blocks["w_kv_mem"] = _after(w_in_token, blocks["w_kv_mem"])
    late_handle, late_token = _exchange_start([blocks[n] for n in late], scatter=False, name="gather_rest_start")
    n_qkv = 3 * D_MODEL
    w_in_t = _to_full(w_in_blocks, "row")
    w_early = dict(
        wp_t=jnp.concatenate([w_in_t[:n_qkv], w_in_t[n_qkv + FOX_HEADS:]], axis=0),
        wf_t=jnp.pad(w_in_t[n_qkv:n_qkv + FOX_HEADS], ((0, LANES - FOX_HEADS), (0, 0))),
    )

    def late_weights(after):
        got = dict(zip(late, _exchange_wait(late_handle, after, scatter=False, name="gather_rest_wait")))
        full = {n: _to_full(got[n], kind[n]) for n in late if n != "conv_w"}
        return dict(
            conv_w=got["conv_w"],
            w_kv=full["w_kv_mem"], wpa=full["w_proj_attn"], wpc=full["w_proj_conv"], wpm=full["w_proj_mem"],
            w_out=full["w_out"], w_gu_t=full["w_gate_up"], w_down=full["w_down"],
        )

    sent = []

    def send_grads(group, g):
        order = [n for n in names if n in g]
        per_owner = [g[n] if g[n].ndim == 3 else _to_blocks(g[n], kind[n]) for n in order]
        handle, token = _exchange_start(per_owner, scatter=True, name="scatter_" + group + "_start")
        sent.append((group, order, handle))
        return token

    gains = {n: weights[n] for n in SMALL}
    gains["norm_mix_pre"] = _after(late_token, gains["norm_mix_pre"])

    loss_tile, grad_x, small = _local_step(x[0], mem[0], loss_target[0], gains, w_early, late_weights, send_grads)

    g_recv = {}
    for group, order, handle in sent:
        g_recv.update(zip(order, _exchange_wait(handle, grad_x, scatter=True, name="scatter_" + group + "_wait")))

    rows = [jnp.pad(small[n], ((0, 0), (0, D_MODEL - small[n].shape[1]))) for n in SMALL]
    rows.append(jnp.broadcast_to(loss_tile[0:1, 0:1], (1, D_MODEL)))
    rows.append(jnp.zeros((SMALL_ROWS - len(rows), D_MODEL), F32))
    (small_recv,), _ = _exchange([jnp.concatenate(rows, axis=0)], scatter=False, name="gather_small")

    def slab_of(d, fill):
        rows = [jnp.pad(d[n], ((0, 0), (0, D_MODEL - d[n].shape[1])), constant_values=fill) for n in SMALL]
        rows.append(jnp.full((SMALL_ROWS - len(rows), D_MODEL), fill, F32))
        return jnp.concatenate(rows, axis=0)

    grads, delta, new_m, new_v = {}, {}, {}, {}
    sg, sd, sm, sv = _adamw(slab_of(weights, 0.0), small_recv, slab_of(moments_m, 0.0), slab_of(moments_v, 1.0), name="adamw_small")
    loss = sg[LOSS_ROW, 0]
    for i, n in enumerate(SMALL):
        c = weights[n].shape[1]
        grads[n], delta[n], new_m[n], new_v[n] = sg[i:i + 1, :c], sd[i:i + 1, :c], sm[i:i + 1, :c], sv[i:i + 1, :c]
    for n in names:
        outs = _adamw(local(weights[n], n), g_recv[n], local(moments_m[n], n), local(moments_v[n], n), name="adamw_" + n)
        grads[n], delta[n], new_m[n], new_v[n] = [(jnp.swapaxes(o, 0, 1) if n in TRANSPOSED else o)[None] for o in outs]

    return (loss, grad_x[None], *[grads[n] for n in WEIGHT_ORDER], *[delta[n] for n in WEIGHT_ORDER],
            *[new_m[n] for n in WEIGHT_ORDER], *[new_v[n] for n in WEIGHT_ORDER])
```

```python
import functools

import jax
import jax.numpy as jnp
from jax import lax
from jax.experimental import pallas as pl
from jax.experimental.pallas import tpu as pltpu

F32 = jnp.float32
BF16 = jnp.bfloat16

D_MODEL = 1024
N_DEV = 8
FOX_HEADS = 16
FOX_HEAD_DIM = 64
HEAD_PAIRS = FOX_HEADS // 2
MEM_HEADS = 4
MEM_HEAD_DIM = D_MODEL // MEM_HEADS
CONV_WIDTH = 31
CONV_HALO = 32
CONV_CHUNK = 16
SUBLANES = 8
FFN_HIDDEN = 2816
RMS_EPS = 1e-6
LN_EPS = 1e-5
ADAM_LR = 0.001
ADAM_B1 = 0.9
ADAM_B2 = 0.999
ADAM_EPS = 1e-08
ADAM_WD = 0.01
ADAM_STEP = 10
KV_STEP = 2
KV_STEP_FWD = 4
NEG_BIG = -1e30
LANES = 128

PB_Q, PB_K, PB_V, PB_A, PB_GATE, PB_QMEM, PB_G0 = 0, 1, 2, 3, 4, 5, 6
P_WIDTH = 9 * D_MODEL

NT_DIMS = (((1,), (1,)), ((), ()))
TN_DIMS = (((0,), (0,)), ((), ()))


def _cparams(sem, vmem_mb=None):
    kw = dict(dimension_semantics=sem)
    if vmem_mb is not None:
        kw["vmem_limit_bytes"] = vmem_mb * 1024 * 1024
    return pltpu.CompilerParams(**kw)


def _tile(dim, want):
    t = min(dim, want)
    assert dim % t == 0, (dim, want)
    return t


def _sigmoid(z):
    return 1.0 / (1.0 + jnp.exp(-z))


def _matmul(a, b, *, mode, out_dtype, name, tm=1024, tn=1024, tk=1024, after=None):
    if mode == "nn":
        (M, K), (K2, N) = a.shape, b.shape
    elif mode == "nt":
        (M, K), (N, K2) = a.shape, b.shape
    else:
        (K, M), (K2, N) = a.shape, b.shape
    assert K == K2, (a.shape, b.shape, mode)
    tm, tn, tk = _tile(M, tm), _tile(N, tn), _tile(K, tk)
    nk = K // tk
    dims = {"nn": (((1,), (0,)), ((), ())), "nt": NT_DIMS, "tn": TN_DIMS}[mode]

    n_extra = 0 if after is None else 1

    def body(a_ref, b_ref, *rest):
        o_ref, scratch = rest[n_extra], rest[n_extra + 1:]
        part = lax.dot_general(a_ref[...], b_ref[...], dims, preferred_element_type=F32)
        if nk == 1:
            o_ref[...] = part.astype(o_ref.dtype)
        else:
            acc_ref, = scratch
            k = pl.program_id(2)

            @pl.when(k == 0)
            def _():
                acc_ref[...] = part

            @pl.when(k > 0)
            def _():
                acc_ref[...] += part

            @pl.when(k == nk - 1)
            def _():
                o_ref[...] = acc_ref[...].astype(o_ref.dtype)

    a_spec = pl.BlockSpec((tk, tm), lambda j, i, k: (k, i)) if mode == "tn" else pl.BlockSpec((tm, tk), lambda j, i, k: (i, k))
    b_spec = pl.BlockSpec((tn, tk), lambda j, i, k: (j, k)) if mode == "nt" else pl.BlockSpec((tk, tn), lambda j, i, k: (k, j))
    return pl.pallas_call(
        body,
        name=name,
        grid=(N // tn, M // tm, nk),
        in_specs=[a_spec, b_spec] + [pl.BlockSpec((8, LANES), lambda j, i, k: (0, 0))] * n_extra,
        out_specs=pl.BlockSpec((tm, tn), lambda j, i, k: (i, j)),
        out_shape=jax.ShapeDtypeStruct((M, N), out_dtype),
        scratch_shapes=[pltpu.VMEM((tm, tn), F32)] if nk > 1 else [],
        compiler_params=_cparams(("parallel", "parallel", "arbitrary"), 56),
    )(a, b, *([] if after is None else [after]))


def _matmul_pieces(pieces, b, *, mode, out_dtype, name, after=None):
    blk = 1024
    T = len(pieces)
    S = pieces[0].shape[0]
    counts = [p.shape[1] // blk for p in pieces]
    offs = [sum(counts[:t]) for t in range(T)]
    n_a = sum(counts)
    N = b.shape[1]
    n_extra = 0 if after is None else 1
    if mode == "nn":
        tm = _tile(S, blk)
        grid, n_red = (S // tm, n_a), n_a
        which = lambda i, k: k
        a_specs = [pl.BlockSpec((tm, blk), lambda i, k, o=o, c=c: (i, jnp.clip(k - o, 0, c - 1))) for o, c in zip(offs, counts)]
        b_spec = pl.BlockSpec((blk, N), lambda i, k: (k, 0))
        out_spec, out_rows, dims = pl.BlockSpec((tm, N), lambda i, k: (i, 0)), S, (((1,), (0,)), ((), ()))
    else:
        tk = _tile(S, blk)
        grid, n_red = (n_a, S // tk), S // tk
        which = lambda m, k: m

        def a_index(m, k, o, c):
            mine = (m >= o) & (m < o + c)
            return jnp.where(mine, k, 0), jnp.clip(m - o, 0, c - 1)

        a_specs = [pl.BlockSpec((tk, blk), functools.partial(a_index, o=o, c=c)) for o, c in zip(offs, counts)]
        b_spec = pl.BlockSpec((tk, N), lambda m, k: (k, 0))
        out_spec, out_rows, dims = pl.BlockSpec((blk, N), lambda m, k: (m, 0)), n_a * blk, TN_DIMS

    def body(*refs):
        a_refs, b_ref = refs[:T], refs[T]
        o_ref, acc_ref = refs[T + 1 + n_extra], refs[T + 2 + n_extra]
        a_blk = which(pl.program_id(0), pl.program_id(1))
        k = pl.program_id(1)
        for t in range(T):
            @pl.when((a_blk >= offs[t]) & (a_blk < offs[t] + counts[t]))
            def _(t=t):
                part = lax.dot_general(a_refs[t][...], b_ref[...], dims, preferred_element_type=F32)

                @pl.when(k == 0)
                def _():
                    acc_ref[...] = part

                @pl.when(k > 0)
                def _():
                    acc_ref[...] += part

        @pl.when(k == n_red - 1)
        def _():
            o_ref[...] = acc_ref[...].astype(o_ref.dtype)

    return pl.pallas_call(
        body, name=name, grid=grid,
        in_specs=a_specs + [b_spec] + [pl.BlockSpec((8, LANES), lambda i, k: (0, 0))] * n_extra,
        out_specs=out_spec,
        out_shape=jax.ShapeDtypeStruct((out_rows, N), out_dtype),
        scratch_shapes=[pltpu.VMEM(out_spec.block_shape, F32)],
        compiler_params=_cparams(("parallel", "arbitrary"), 56),
    )(*pieces, b, *([] if after is None else [after]))


def _rms_fwd(x, g, *, name, tm=512):
    S, D = x.shape
    tm = _tile(S, tm)

    def body(x_ref, g_ref, o_ref):
        xv = x_ref[...]
        rstd = lax.rsqrt(jnp.mean(xv * xv, axis=-1, keepdims=True) + RMS_EPS)
        o_ref[...] = (xv * rstd * g_ref[...]).astype(o_ref.dtype)

    return pl.pallas_call(
        body, name=name, grid=(S // tm,),
        in_specs=[pl.BlockSpec((tm, D), lambda i: (i, 0)), pl.BlockSpec((1, D), lambda i: (0, 0))],
        out_specs=pl.BlockSpec((tm, D), lambda i: (i, 0)),
        out_shape=jax.ShapeDtypeStruct((S, D), BF16),
        compiler_params=_cparams(("parallel",)),
    )(x, g)


def _rms_bwd(xin, dys, g, res, *, out_dtype, name, tm=512):
    S, D = xin.shape
    tm = _tile(S, tm)
    n_dy = len(dys)
    has_res = res is not None

    def body(*refs):
        x_ref, g_ref = refs[0], refs[1]
        dy_refs = refs[2:2 + n_dy]
        pos = 2 + n_dy
        res_ref = refs[pos] if has_res else None
        pos += int(has_res)
        dx_ref, dg_ref = refs[pos], refs[pos + 1]
        i = pl.program_id(0)
        xv = x_ref[...]
        dy = dy_refs[0][...].astype(F32)
        for r in dy_refs[1:]:
            dy = dy + r[...].astype(F32)
        rstd = lax.rsqrt(jnp.mean(xv * xv, axis=-1, keepdims=True) + RMS_EPS)
        xhat = xv * rstd
        gy = dy * g_ref[...]
        dx = rstd * (gy - xhat * jnp.mean(gy * xhat, axis=-1, keepdims=True))
        if has_res:
            dx = dx + res_ref[...]
        dx_ref[...] = dx.astype(dx_ref.dtype)
        part = jnp.sum(dy * xhat, axis=0, keepdims=True)

        @pl.when(i == 0)
        def _():
            dg_ref[...] = part

        @pl.when(i > 0)
        def _():
            dg_ref[...] += part

    row = pl.BlockSpec((tm, D), lambda i: (i, 0))
    vec = pl.BlockSpec((1, D), lambda i: (0, 0))
    ins = [xin, g] + list(dys) + ([res] if has_res else [])
    return pl.pallas_call(
        body, name=name, grid=(S // tm,),
        in_specs=[row, vec] + [row] * n_dy + ([row] if has_res else []),
        out_specs=[row, vec],
        out_shape=[jax.ShapeDtypeStruct((S, D), out_dtype), jax.ShapeDtypeStruct((1, D), F32)],
        compiler_params=_cparams(("arbitrary",)),
    )(*ins)


def _head_mask(hh, shape):
    lane = lax.broadcasted_iota(jnp.int32, shape, len(shape) - 1)
    return (lane // FOX_HEAD_DIM) == hh


def _block_constants(cr_ref, i, tq):
    first = cr_ref[0, :, pl.ds(pl.multiple_of(i * tq, tq), LANES)]
    return [first[hh:hh + 1, 0:1] for hh in range(2)]


def _attn_fwd(P, crow, *, tq, name):
    S = P.shape[0]
    tq = _tile(S, tq)
    nq = S // tq
    scale = FOX_HEAD_DIM ** -0.5

    def body(q_ref, k_ref, v_ref, cr_ref, o_ref, o32_ref, lse_ref):
        i = pl.program_id(1)
        q = q_ref[...] * jnp.asarray(scale, BF16)
        row = lax.broadcasted_iota(jnp.int32, (tq, tq), 0)
        col = lax.broadcasted_iota(jnp.int32, (tq, tq), 1)
        causal = col <= row
        hms = [_head_mask(hh, (tq, LANES)) for hh in range(2)]
        qhs = [jnp.where(hm, q, jnp.zeros_like(q)) for hm in hms]
        cbs = _block_constants(cr_ref, i, tq)

        def step(block, n_blocks, carry, masked):
            tk = n_blocks * tq
            start = pl.multiple_of(block * tq, tk)
            kj = k_ref[pl.ds(start, tk), :]
            vj = v_ref[pl.ds(start, tk), :]
            lane_k = lax.broadcasted_iota(jnp.int32, (tk, LANES), 1)
            new = []
            for hh in range(2):
                m, acc = carry[hh]
                bias = cbs[hh] - cr_ref[0, hh:hh + 1, pl.ds(start, tk)]
                s = lax.dot_general(qhs[hh], kj, NT_DIMS, preferred_element_type=F32) + bias
                if masked:
                    s = jnp.where(causal, s, NEG_BIG)
                m_new = jnp.maximum(m, jnp.max(s, axis=1, keepdims=True))
                alpha = jnp.exp(m - m_new)
                p = jnp.exp(s - m_new)
                vh = jnp.where(lane_k == (1 - hh) * FOX_HEAD_DIM, jnp.ones_like(vj), vj)
                acc = alpha * acc + jnp.dot(p.astype(BF16), vh, preferred_element_type=F32)
                new.append((m_new, acc))
            return tuple(new)

        lane = lax.broadcasted_iota(jnp.int32, (tq, LANES), 1)
        ones_lane = [lane == (1 - hh) * FOX_HEAD_DIM for hh in range(2)]
        init = (jnp.full((tq, 1), NEG_BIG, F32), jnp.zeros((tq, LANES), F32))
        n_wide = i // KV_STEP_FWD
        carry = lax.fori_loop(0, n_wide, lambda j, c: step(j * KV_STEP_FWD, KV_STEP_FWD, c, False), (init, init))
        carry = lax.fori_loop(n_wide * KV_STEP_FWD, i, lambda j, c: step(j, 1, c, False), carry)
        carry = step(i, 1, carry, True)
        ls = [jnp.sum(jnp.where(ones_lane[hh], carry[hh][1], 0.0), axis=1, keepdims=True) for hh in range(2)]
        out = jnp.where(hms[0], carry[0][1] / ls[0], carry[1][1] / ls[1])
        for hh in range(2):
            lse_ref[hh] = carry[hh][0] + jnp.log(ls[hh])
        o_ref[...] = out.astype(o_ref.dtype)
        o32_ref[...] = out

    nblk = D_MODEL // LANES
    return pl.pallas_call(
        body, name=name, grid=(HEAD_PAIRS, nq),
        in_specs=[
            pl.BlockSpec((tq, LANES), lambda p, i: (i, PB_Q * nblk + p)),
            pl.BlockSpec((S, LANES), lambda p, i: (0, PB_K * nblk + p)),
            pl.BlockSpec((S, LANES), lambda p, i: (0, PB_V * nblk + p)),
            pl.BlockSpec((1, 2, S), lambda p, i: (p, 0, 0)),
        ],
        out_specs=[
            pl.BlockSpec((tq, LANES), lambda p, i: (i, p)),
            pl.BlockSpec((tq, LANES), lambda p, i: (i, p)),
            pl.BlockSpec((2, tq, 1), lambda p, i: (p, i, 0)),
        ],
        out_shape=[jax.ShapeDtypeStruct((S, D_MODEL), BF16), jax.ShapeDtypeStruct((S, D_MODEL), F32),
                   jax.ShapeDtypeStruct((FOX_HEADS, S, 1), F32)],
        compiler_params=_cparams(("parallel", "arbitrary"), 56),
    )(P, P, P, crow)


def _attn_bwd(P, do, o32, lse, crow, *, tq, name):
    S = P.shape[0]
    tq = _tile(S, tq)
    nq = S // tq
    scale = FOX_HEAD_DIM ** -0.5

    def body(q_ref, k_ref, v_ref, do_ref, o_ref, lse_ref, cr_ref, dq_ref, dk_out, dv_out, dc_ref, dcq_ref, dk_ref, dv_ref):
        i = pl.program_id(1)

        @pl.when(i == 0)
        def _():
            dk_ref[...] = jnp.zeros_like(dk_ref)
            dv_ref[...] = jnp.zeros_like(dv_ref)
            dc_ref[...] = jnp.zeros_like(dc_ref)

        q = q_ref[...] * jnp.asarray(scale, BF16)
        do_v = do_ref[...]
        row = lax.broadcasted_iota(jnp.int32, (tq, tq), 0)
        col = lax.broadcasted_iota(jnp.int32, (tq, tq), 1)
        causal = col <= row
        hms = [_head_mask(hh, (tq, LANES)) for hh in range(2)]
        qhs = [jnp.where(hm, q, jnp.zeros_like(q)) for hm in hms]
        dohs = [jnp.where(hm, do_v, jnp.zeros_like(do_v)) for hm in hms]
        cbs = _block_constants(cr_ref, i, tq)
        lses = [lse_ref[hh] for hh in range(2)]
        prod = do_v.astype(F32) * o_ref[...]
        dls = [jnp.sum(jnp.where(hm, prod, 0.0), axis=1, keepdims=True) for hm in hms]

        def step(block, n_blocks, carry, masked):
            dq_acc, row_sums = carry[0], list(carry[1:])
            tk = n_blocks * tq
            start = pl.multiple_of(block * tq, tk)
            kj = k_ref[pl.ds(start, tk), :]
            vj = v_ref[pl.ds(start, tk), :]
            head_k = [_head_mask(hh, (tk, LANES)) for hh in range(2)]
            dv_part = jnp.zeros((tk, LANES), F32)
            dk_part = jnp.zeros((tk, LANES), F32)
            for hh in range(2):
                bias = cbs[hh] - cr_ref[0, hh:hh + 1, pl.ds(start, tk)]
                s = lax.dot_general(qhs[hh], kj, NT_DIMS, preferred_element_type=F32) + bias
                if masked:
                    s = jnp.where(causal, s, NEG_BIG)
                p = jnp.exp(s - lses[hh])
                dp = lax.dot_general(dohs[hh], vj, NT_DIMS, preferred_element_type=F32)
                ds = p * (dp - dls[hh])
                pb = p.astype(BF16)
                dsb = ds.astype(BF16)
                dv_part = dv_part + lax.dot_general(pb, dohs[hh], TN_DIMS, preferred_element_type=F32)
                dk_part = dk_part + lax.dot_general(dsb, qhs[hh], TN_DIMS, preferred_element_type=F32)
                dc_ref[0, hh:hh + 1, pl.ds(start, tk)] -= jnp.sum(ds, axis=0, keepdims=True)
                row_sums[hh] = row_sums[hh] + jnp.sum(ds, axis=1, keepdims=True)
                kh = jnp.where(head_k[hh], kj, jnp.zeros_like(kj))
                dq_acc = dq_acc + jnp.dot(dsb, kh, preferred_element_type=F32)
            dv_ref[pl.ds(start, tk), :] += dv_part
            dk_ref[pl.ds(start, tk), :] += dk_part
            return (dq_acc, *row_sums)

        zero_col = jnp.zeros((tq, 1), F32)
        n_wide = i // KV_STEP
        carry = lax.fori_loop(0, n_wide, lambda j, c: step(j * KV_STEP, KV_STEP, c, False),
                              (jnp.zeros((tq, LANES), F32), zero_col, zero_col))
        carry = lax.fori_loop(n_wide * KV_STEP, i, lambda j, c: step(j, 1, c, False), carry)
        carry = step(i, 1, carry, True)
        dq_ref[...] = (carry[0] * scale).astype(dq_ref.dtype)
        for hh in range(2):
            dcq_ref[0, hh:hh + 1, :] = jnp.transpose(jnp.broadcast_to(carry[1 + hh], (tq, LANES)))[0:1, :]

        @pl.when(i == nq - 1)
        def _():
            dk_out[...] = dk_ref[...].astype(dk_out.dtype)
            dv_out[...] = dv_ref[...].astype(dv_out.dtype)

    nblk = D_MODEL // LANES
    qblk = pl.BlockSpec((tq, LANES), lambda p, i: (i, p))
    stat = pl.BlockSpec((2, tq, 1), lambda p, i: (p, i, 0))
    full = pl.BlockSpec((S, LANES), lambda p, i: (0, p))
    return pl.pallas_call(
        body, name=name, grid=(HEAD_PAIRS, nq),
        in_specs=[
            pl.BlockSpec((tq, LANES), lambda p, i: (i, PB_Q * nblk + p)),
            pl.BlockSpec((S, LANES), lambda p, i: (0, PB_K * nblk + p)),
            pl.BlockSpec((S, LANES), lambda p, i: (0, PB_V * nblk + p)),
            qblk, qblk, stat,
            pl.BlockSpec((1, 2, S), lambda p, i: (p, 0, 0)),
        ],
        out_specs=[qblk, full, full, pl.BlockSpec((1, 2, S), lambda p, i: (p, 0, 0)),
                   pl.BlockSpec((1, 2, tq), lambda p, i: (p, 0, i))],
        out_shape=[
            jax.ShapeDtypeStruct((S, D_MODEL), BF16),
            jax.ShapeDtypeStruct((S, D_MODEL), BF16),
            jax.ShapeDtypeStruct((S, D_MODEL), BF16),
            jax.ShapeDtypeStruct((HEAD_PAIRS, 2, S), F32),
            jax.ShapeDtypeStruct((HEAD_PAIRS, 2, S), F32),
        ],
        scratch_shapes=[pltpu.VMEM((S, LANES), F32), pltpu.VMEM((S, LANES), F32)],
        compiler_params=_cparams(("parallel", "arbitrary"), 56),
    )(P, P, P, do, o32, lse, crow)


def _cumsum_lanes(xs, *, reverse, name):
    R, S = xs[0].shape
    nb = S // LANES
    n_in = len(xs)

    def body(*refs):
        x_refs, o_ref = refs[:n_in], refs[n_in]
        r = lax.broadcasted_iota(jnp.int32, (LANES, LANES), 0)
        c = lax.broadcasted_iota(jnp.int32, (LANES, LANES), 1)
        tri = ((r >= c) if reverse else (r <= c)).astype(F32)

        def step(b, carry):
            blk = (nb - 1 - b) if reverse else b
            start = pl.multiple_of(blk * LANES, LANES)
            xb = x_refs[0][:, pl.ds(start, LANES)]
            for r in x_refs[1:]:
                xb = xb + r[:, pl.ds(start, LANES)]
            y = jnp.dot(xb, tri, precision=lax.Precision.HIGHEST, preferred_element_type=F32) + carry
            o_ref[:, pl.ds(start, LANES)] = y
            return carry + jnp.sum(xb, axis=1, keepdims=True)

        lax.fori_loop(0, nb, step, jnp.zeros((R, 1), F32))

    return pl.pallas_call(
        body, name=name,
        in_specs=[pl.BlockSpec(memory_space=pltpu.VMEM)] * n_in,
        out_specs=pl.BlockSpec(memory_space=pltpu.VMEM),
        out_shape=jax.ShapeDtypeStruct((R, S), F32),
    )(*xs)


def _forget_fwd(f_logit, b_pad, *, name, tm=1024):
    S = f_logit.shape[0]
    tm = _tile(S, tm)

    def body(f_ref, b_ref, o_ref):
        z = f_ref[...] + b_ref[...]
        o_ref[...] = jnp.minimum(z, 0.0) - jnp.log(1.0 + jnp.exp(-jnp.abs(z)))

    blk = pl.BlockSpec((tm, LANES), lambda i: (i, 0))
    return pl.pallas_call(
        body, name=name, grid=(S // tm,),
        in_specs=[blk, pl.BlockSpec((1, LANES), lambda i: (0, 0))],
        out_specs=blk, out_shape=jax.ShapeDtypeStruct((S, LANES), F32),
        compiler_params=_cparams(("parallel",)),
    )(f_logit, b_pad)


def _forget_bwd(f_logit, b_pad, dlogf, *, name, tm=1024):
    S = f_logit.shape[0]
    tm = _tile(S, tm)

    def body(f_ref, b_ref, d_ref, o_ref, db_ref):
        i = pl.program_id(0)
        z = f_ref[...] + b_ref[...]
        dz = d_ref[...] * (1.0 - _sigmoid(z))
        o_ref[...] = dz.astype(o_ref.dtype)
        part = jnp.sum(dz, axis=0, keepdims=True)

        @pl.when(i == 0)
        def _():
            db_ref[...] = part

        @pl.when(i > 0)
        def _():
            db_ref[...] += part

    blk = pl.BlockSpec((tm, LANES), lambda i: (i, 0))
    vec = pl.BlockSpec((1, LANES), lambda i: (0, 0))
    return pl.pallas_call(
        body, name=name, grid=(S // tm,),
        in_specs=[blk, vec, blk], out_specs=[blk, vec],
        out_shape=[jax.ShapeDtypeStruct((S, LANES), BF16), jax.ShapeDtypeStruct((1, LANES), F32)],
        compiler_params=_cparams(("arbitrary",)),
    )(f_logit, b_pad, dlogf)


def _layernorm_stats(y):
    mu = jnp.mean(y, axis=-1, keepdims=True)
    yc = y - mu
    rstd = lax.rsqrt(jnp.mean(yc * yc, axis=-1, keepdims=True) + LN_EPS)
    return yc * rstd, rstd


def _fill_shifted(src_ref, sh_ref, tm):
    rows = tm + CONV_HALO - SUBLANES
    for s in range(1, SUBLANES):
        sh_ref[s - 1, 0:rows, :] = src_ref[pl.ds(s, rows), :]


def _shifted_rows(src_ref, sh_ref, r0, offset):
    s, base = offset % SUBLANES, offset - offset % SUBLANES
    rows = pl.ds(pl.multiple_of(r0 + base, SUBLANES), CONV_CHUNK)
    return src_ref[rows, :] if s == 0 else sh_ref[s - 1, rows, :]


def _fill_tap_rows(w_ref, wb_ref):
    for k in range(CONV_WIDTH):
        for j in range(N_DEV):
            wb_ref[k * SUBLANES:(k + 1) * SUBLANES, j * LANES:(j + 1) * LANES] = jnp.broadcast_to(
                w_ref[j, k:k + 1, :], (SUBLANES, LANES))


def _tap_sum(wb_ref, rows_of_tap):
    n = CONV_CHUNK // SUBLANES
    accs = [None] * n
    for k in range(CONV_WIDTH):
        wk = wb_ref[k * SUBLANES:(k + 1) * SUBLANES, :]
        src = rows_of_tap(k)
        for h in range(n):
            term = wk * src[h * SUBLANES:(h + 1) * SUBLANES]
            accs[h] = term if accs[h] is None else accs[h] + term
    return jnp.concatenate(accs, axis=0)


def _conv_fwd(P, w_pad, conv_b, ln_g, ln_b, *, name, tm=256):
    S = P.shape[0]
    C = D_MODEL
    tm = _tile(S, tm)
    hb = tm // CONV_HALO

    def body(a_ref, gt_ref, ah_ref, gh_ref, w_ref, cb_ref, g_ref, b_ref, o_ref, y_ref, glu_ref, sh_ref, wb_ref):
        i = pl.program_id(0)
        halo = ah_ref[...].astype(F32) * _sigmoid(gh_ref[...].astype(F32))
        glu_ref[0:CONV_HALO, :] = jnp.where(i > 0, halo, 0.0)
        glu_ref[CONV_HALO:, :] = a_ref[...].astype(F32) * _sigmoid(gt_ref[...].astype(F32))
        _fill_shifted(glu_ref, sh_ref, tm)
        _fill_tap_rows(w_ref, wb_ref)

        def chunk(r, carry):
            r0 = pl.multiple_of(r * CONV_CHUNK, CONV_CHUNK)
            acc = _tap_sum(wb_ref, lambda k: _shifted_rows(glu_ref, sh_ref, r0, CONV_HALO - (CONV_WIDTH - 1) + k))
            y_ref[pl.ds(r0, CONV_CHUNK), :] = acc + cb_ref[...]
            return carry

        lax.fori_loop(0, tm // CONV_CHUNK, chunk, 0)
        xhat, _ = _layernorm_stats(y_ref[...])
        z = xhat * g_ref[...] + b_ref[...]
        o_ref[...] = (z * _sigmoid(z)).astype(o_ref.dtype)

    vec = pl.BlockSpec((1, C), lambda i: (0, 0))
    row = pl.BlockSpec((tm, C), lambda i: (i, 0))
    return pl.pallas_call(
        body, name=name, grid=(S // tm,),
        in_specs=[
            pl.BlockSpec((tm, C), lambda i: (i, PB_A)),
            pl.BlockSpec((tm, C), lambda i: (i, PB_GATE)),
            pl.BlockSpec((CONV_HALO, C), lambda i: (jnp.maximum(i * hb - 1, 0), PB_A)),
            pl.BlockSpec((CONV_HALO, C), lambda i: (jnp.maximum(i * hb - 1, 0), PB_GATE)),
            pl.BlockSpec((N_DEV, CONV_WIDTH, LANES), lambda i: (0, 0, 0)),
            vec, vec, vec,
        ],
        out_specs=[row, row],
        out_shape=[jax.ShapeDtypeStruct((S, C), BF16), jax.ShapeDtypeStruct((S, C), F32)],
        scratch_shapes=[pltpu.VMEM((tm + CONV_HALO, C), F32), pltpu.VMEM((SUBLANES - 1, tm + CONV_HALO, C), F32),
                        pltpu.VMEM((CONV_HALO * SUBLANES, C), F32)],
        compiler_params=_cparams(("parallel",), 56),
    )(P, P, P, P, w_pad, conv_b, ln_g, ln_b)


def _conv_bwd(P, do, y, w_pad, ln_g, ln_b, *, name, tm=256):
    S = P.shape[0]
    C = D_MODEL
    tm = _tile(S, tm)
    hb = tm // CONV_HALO
    n_tiles = S // tm
    last_halo = S // CONV_HALO - 1

    def body(a_ref, gt_ref, ah_ref, gh_ref, do_ref, y_ref, don_ref, yn_ref, w_ref, g_ref, b_ref,
             dglu_ref, dw_ref, small_ref, glu_ref, dy_ref, gsh_ref, dsh_ref, dwacc_ref, wb_ref):
        i = pl.program_id(0)

        @pl.when(i == 0)
        def _():
            dwacc_ref[...] = jnp.zeros_like(dwacc_ref)
            small_ref[...] = jnp.zeros_like(small_ref)

        def ln_bwd(do_v, y_v):
            xhat, rstd = _layernorm_stats(y_v)
            z = xhat * g_ref[...] + b_ref[...]
            sg = _sigmoid(z)
            dz = do_v * (sg * (1.0 + z * (1.0 - sg)))
            dxh = dz * g_ref[...]
            dy = rstd * (dxh - jnp.mean(dxh, axis=-1, keepdims=True) - xhat * jnp.mean(dxh * xhat, axis=-1, keepdims=True))
            return dy, dz, xhat

        dy, dz, xhat = ln_bwd(do_ref[...], y_ref[...])
        dy_next, _, _ = ln_bwd(don_ref[...], yn_ref[...])
        small_ref[0:1, :] += jnp.sum(dz * xhat, axis=0, keepdims=True)
        small_ref[1:2, :] += jnp.sum(dz, axis=0, keepdims=True)
        small_ref[2:3, :] += jnp.sum(dy, axis=0, keepdims=True)
        dy_ref[0:tm, :] = dy
        dy_ref[tm:, :] = jnp.where(i < n_tiles - 1, dy_next, 0.0)

        halo = ah_ref[...].astype(F32) * _sigmoid(gh_ref[...].astype(F32))
        glu_ref[0:CONV_HALO, :] = jnp.where(i > 0, halo, 0.0)
        glu_ref[CONV_HALO:, :] = a_ref[...].astype(F32) * _sigmoid(gt_ref[...].astype(F32))
        _fill_shifted(glu_ref, gsh_ref, tm)
        _fill_shifted(dy_ref, dsh_ref, tm)
        _fill_tap_rows(w_ref, wb_ref)

        def chunk(r, carry):
            r0 = pl.multiple_of(r * CONV_CHUNK, CONV_CHUNK)
            rows = pl.ds(r0, CONV_CHUNK)
            dyc = dy_ref[rows, :]
            for k in range(CONV_WIDTH):
                prod = dyc * _shifted_rows(glu_ref, gsh_ref, r0, CONV_HALO - (CONV_WIDTH - 1) + k)
                dwacc_ref[k * SUBLANES:(k + 1) * SUBLANES, :] += prod[0:SUBLANES] + prod[SUBLANES:]
            dg = _tap_sum(wb_ref, lambda k: _shifted_rows(dy_ref, dsh_ref, r0, CONV_WIDTH - 1 - k))
            a = a_ref[rows, :].astype(F32)
            sig = _sigmoid(gt_ref[rows, :].astype(F32))
            dglu_ref[rows, 0:C] = (dg * sig).astype(dglu_ref.dtype)
            dglu_ref[rows, C:] = (dg * a * sig * (1.0 - sig)).astype(dglu_ref.dtype)
            return carry

        lax.fori_loop(0, tm // CONV_CHUNK, chunk, 0)

        @pl.when(i == n_tiles - 1)
        def _():
            dw_ref[...] = jnp.zeros_like(dw_ref)
            for k in range(CONV_WIDTH):
                row_k = jnp.sum(dwacc_ref[k * SUBLANES:(k + 1) * SUBLANES, :], axis=0, keepdims=True)
                for j in range(N_DEV):
                    dw_ref[j, k:k + 1, :] = row_k[:, j * LANES:(j + 1) * LANES]

    vec = pl.BlockSpec((1, C), lambda i: (0, 0))
    row = pl.BlockSpec((tm, C), lambda i: (i, 0))
    nxt = pl.BlockSpec((CONV_HALO, C), lambda i: (jnp.minimum((i + 1) * hb, last_halo), 0))
    return pl.pallas_call(
        body, name=name, grid=(n_tiles,),
        in_specs=[
            pl.BlockSpec((tm, C), lambda i: (i, PB_A)),
            pl.BlockSpec((tm, C), lambda i: (i, PB_GATE)),
            pl.BlockSpec((CONV_HALO, C), lambda i: (jnp.maximum(i * hb - 1, 0), PB_A)),
            pl.BlockSpec((CONV_HALO, C), lambda i: (jnp.maximum(i * hb - 1, 0), PB_GATE)),
            row, row, nxt, nxt,
            pl.BlockSpec((N_DEV, CONV_WIDTH, LANES), lambda i: (0, 0, 0)),
            vec, vec,
        ],
        out_specs=[
            pl.BlockSpec((tm, 2 * C), lambda i: (i, 0)),
            pl.BlockSpec((N_DEV, CONV_HALO, LANES), lambda i: (0, 0, 0)),
            pl.BlockSpec((8, C), lambda i: (0, 0)),
        ],
        out_shape=[
            jax.ShapeDtypeStruct((S, 2 * C), BF16),
            jax.ShapeDtypeStruct((N_DEV, CONV_HALO, LANES), F32),
            jax.ShapeDtypeStruct((8, C), F32),
        ],
        scratch_shapes=[
            pltpu.VMEM((tm + CONV_HALO, C), F32), pltpu.VMEM((tm + CONV_HALO, C), F32),
            pltpu.VMEM((SUBLANES - 1, tm + CONV_HALO, C), F32), pltpu.VMEM((SUBLANES - 1, tm + CONV_HALO, C), F32),
            pltpu.VMEM((CONV_HALO * SUBLANES, C), F32), pltpu.VMEM((CONV_HALO * SUBLANES, C), F32),
        ],
        compiler_params=_cparams(("arbitrary",), 56),
    )(P, P, P, P, do, y, do, y, w_pad, ln_g, ln_b)


def _mem_softmax(qh, kh):
    s = lax.dot_general(qh, kh, NT_DIMS, preferred_element_type=F32)
    e = jnp.exp(s - jnp.max(s, axis=1, keepdims=True))
    return e / jnp.sum(e, axis=1, keepdims=True)


def _mem_fwd(P, kv, *, name, tm=512):
    S, M = P.shape[0], kv.shape[0]
    tm = _tile(S, tm)
    scale = MEM_HEAD_DIM ** -0.5

    def body(q_ref, k_ref, v_ref, o_ref):
        for h in range(MEM_HEADS):
            sl = slice(h * MEM_HEAD_DIM, (h + 1) * MEM_HEAD_DIM)
            qh = q_ref[:, sl] * jnp.asarray(scale, BF16)
            p = _mem_softmax(qh, k_ref[:, sl])
            o_ref[:, sl] = jnp.dot(p.astype(BF16), v_ref[:, sl], preferred_element_type=F32).astype(o_ref.dtype)

    return pl.pallas_call(
        body, name=name, grid=(S // tm,),
        in_specs=[
            pl.BlockSpec((tm, D_MODEL), lambda i: (i, PB_QMEM)),
            pl.BlockSpec((M, D_MODEL), lambda i: (0, 0)),
            pl.BlockSpec((M, D_MODEL), lambda i: (0, 1)),
        ],
        out_specs=pl.BlockSpec((tm, D_MODEL), lambda i: (i, 0)),
        out_shape=jax.ShapeDtypeStruct((S, D_MODEL), BF16),
        compiler_params=_cparams(("parallel",)),
    )(P, kv, kv)


def _mem_bwd(P, do, kv, *, name, tm=512):
    S, M = P.shape[0], kv.shape[0]
    tm = _tile(S, tm)
    scale = MEM_HEAD_DIM ** -0.5

    def body(q_ref, k_ref, v_ref, do_ref, dq_ref, dkv_ref):
        i = pl.program_id(0)

        @pl.when(i == 0)
        def _():
            dkv_ref[...] = jnp.zeros_like(dkv_ref)

        for h in range(MEM_HEADS):
            sl = slice(h * MEM_HEAD_DIM, (h + 1) * MEM_HEAD_DIM)
            slv = slice(D_MODEL + h * MEM_HEAD_DIM, D_MODEL + (h + 1) * MEM_HEAD_DIM)
            qh = q_ref[:, sl] * jnp.asarray(scale, BF16)
            kh, vh, doh = k_ref[:, sl], v_ref[:, sl], do_ref[:, sl]
            p = _mem_softmax(qh, kh)
            dp = lax.dot_general(doh, vh, NT_DIMS, preferred_element_type=F32)
            ds = p * (dp - jnp.sum(p * dp, axis=1, keepdims=True))
            dsb = ds.astype(BF16)
            dq_ref[:, sl] = (jnp.dot(dsb, kh, preferred_element_type=F32) * scale).astype(dq_ref.dtype)
            dkv_ref[:, sl] += lax.dot_general(dsb, qh, TN_DIMS, preferred_element_type=F32)
            dkv_ref[:, slv] += lax.dot_general(p.astype(BF16), doh, TN_DIMS, preferred_element_type=F32)

    row = pl.BlockSpec((tm, D_MODEL), lambda i: (i, 0))
    return pl.pallas_call(
        body, name=name, grid=(S // tm,),
        in_specs=[
            pl.BlockSpec((tm, D_MODEL), lambda i: (i, PB_QMEM)),
            pl.BlockSpec((M, D_MODEL), lambda i: (0, 0)),
            pl.BlockSpec((M, D_MODEL), lambda i: (0, 1)),
            row,
        ],
        out_specs=[row, pl.BlockSpec((M, 2 * D_MODEL), lambda i: (0, 0))],
        out_shape=[jax.ShapeDtypeStruct((S, D_MODEL), BF16), jax.ShapeDtypeStruct((M, 2 * D_MODEL), F32)],
        compiler_params=_cparams(("arbitrary",)),
    )(P, kv, kv, do)


def _merge_fwd(o_attn, o_conv, o_mem, P, wpa, wpc, wpm, w_out, *, name, tm=256):
    S = P.shape[0]
    D = D_MODEL
    tm = _tile(S, tm)

    def body(oa_ref, oc_ref, om_ref, g0_ref, g1_ref, g2_ref, wa_ref, wc_ref, wm_ref, wo_ref, mg_ref, y_ref, pb_ref):
        merged = jnp.zeros((tm, D), F32)
        for b, (o_ref, g_ref, w_ref) in enumerate(((oa_ref, g0_ref, wa_ref), (oc_ref, g1_ref, wc_ref), (om_ref, g2_ref, wm_ref))):
            pb = jnp.dot(o_ref[...], w_ref[...], preferred_element_type=F32)
            pb_ref[b] = pb.astype(pb_ref.dtype)
            merged = merged + _sigmoid(g_ref[...].astype(F32)) * pb
        mb = merged.astype(BF16)
        mg_ref[...] = mb
        y_ref[...] = jnp.dot(mb, wo_ref[...], preferred_element_type=F32)

    row = pl.BlockSpec((tm, D), lambda i: (i, 0))
    wsp = pl.BlockSpec((D, D), lambda i: (0, 0))
    return pl.pallas_call(
        body, name=name, grid=(S // tm,),
        in_specs=[row, row, row] + [pl.BlockSpec((tm, D), lambda i, b=b: (i, PB_G0 + b)) for b in range(3)] + [wsp] * 4,
        out_specs=[row, row, pl.BlockSpec((3, tm, D), lambda i: (0, i, 0))],
        out_shape=[jax.ShapeDtypeStruct((S, D), BF16), jax.ShapeDtypeStruct((S, D), F32), jax.ShapeDtypeStruct((3, S, D), BF16)],
        compiler_params=_cparams(("parallel",), 56),
    )(o_attn, o_conv, o_mem, P, P, P, wpa, wpc, wpm, w_out)


def _merge_bwd(dy1, w_out, P, pb, *, name, tm=256):
    S = P.shape[0]
    D = D_MODEL
    tm = _tile(S, tm)

    def body(dy_ref, wo_ref, g0_ref, g1_ref, g2_ref, pb_ref, d0_ref, d1_ref, d2_ref, dgl_ref):
        dm = lax.dot_general(dy_ref[...], wo_ref[...], NT_DIMS, preferred_element_type=F32)
        for b, (g_ref, d_ref) in enumerate(((g0_ref, d0_ref), (g1_ref, d1_ref), (g2_ref, d2_ref))):
            g = _sigmoid(g_ref[...].astype(F32))
            d_ref[...] = (dm * g).astype(d_ref.dtype)
            dgl_ref[:, b * D:(b + 1) * D] = (dm * pb_ref[b].astype(F32) * g * (1.0 - g)).astype(dgl_ref.dtype)

    row = pl.BlockSpec((tm, D), lambda i: (i, 0))
    blk3 = pl.BlockSpec((3, tm, D), lambda i: (0, i, 0))
    sds = jax.ShapeDtypeStruct((S, D), BF16)
    return pl.pallas_call(
        body, name=name, grid=(S // tm,),
        in_specs=[row, pl.BlockSpec((D, D), lambda i: (0, 0))]
        + [pl.BlockSpec((tm, D), lambda i, b=b: (i, PB_G0 + b)) for b in range(3)] + [blk3],
        out_specs=[row, row, row, pl.BlockSpec((tm, 3 * D), lambda i: (i, 0))],
        out_shape=[sds, sds, sds, jax.ShapeDtypeStruct((S, 3 * D), BF16)],
        compiler_params=_cparams(("parallel",), 56),
    )(dy1, w_out, P, P, P, pb)


def _resid_norm(x, y1, g_post, g_pre, *, name, tm=512):
    S, D = x.shape
    tm = _tile(S, tm)

    def body(x_ref, y_ref, gp_ref, gq_ref, x1_ref, h2_ref):
        yv = y_ref[...]
        x1 = x_ref[...] + yv * lax.rsqrt(jnp.mean(yv * yv, axis=-1, keepdims=True) + RMS_EPS) * gp_ref[...]
        x1_ref[...] = x1
        h2_ref[...] = (x1 * lax.rsqrt(jnp.mean(x1 * x1, axis=-1, keepdims=True) + RMS_EPS) * gq_ref[...]).astype(h2_ref.dtype)

    row = pl.BlockSpec((tm, D), lambda i: (i, 0))
    vec = pl.BlockSpec((1, D), lambda i: (0, 0))
    return pl.pallas_call(
        body, name=name, grid=(S // tm,),
        in_specs=[row, row, vec, vec], out_specs=[row, row],
        out_shape=[jax.ShapeDtypeStruct((S, D), F32), jax.ShapeDtypeStruct((S, D), BF16)],
        compiler_params=_cparams(("parallel",)),
    )(x, y1, g_post, g_pre)


def _swiglu_fwd(h2, w_gu_t, *, name, tm=512, tn=1408):
    S, D = h2.shape
    Fh = w_gu_t.shape[0] // 2
    tm, tn = _tile(S, tm), _tile(Fh, tn)
    nj = Fh // tn

    def body(h_ref, wg_ref, wu_ref, g_ref, u_ref, a_ref):
        hv = h_ref[...]
        g = lax.dot_general(hv, wg_ref[...], NT_DIMS, preferred_element_type=F32)
        u = lax.dot_general(hv, wu_ref[...], NT_DIMS, preferred_element_type=F32)
        g_ref[...] = g.astype(g_ref.dtype)
        u_ref[...] = u.astype(u_ref.dtype)
        a_ref[...] = (g * _sigmoid(g) * u).astype(a_ref.dtype)

    out = pl.BlockSpec((tm, tn), lambda j, i: (i, j))
    sds = jax.ShapeDtypeStruct((S, Fh), BF16)
    return pl.pallas_call(
        body, name=name, grid=(nj, S // tm),
        in_specs=[
            pl.BlockSpec((tm, D), lambda j, i: (i, 0)),
            pl.BlockSpec((tn, D), lambda j, i: (j, 0)),
            pl.BlockSpec((tn, D), lambda j, i: (j + nj, 0)),
        ],
        out_specs=[out, out, out], out_shape=[sds, sds, sds],
        compiler_params=_cparams(("parallel", "parallel"), 56),
    )(h2, w_gu_t, w_gu_t)


def _swiglu_bwd(dffn, w_down, g, u, *, name, tm=512, tn=1408):
    S, Fh = g.shape
    D = dffn.shape[1]
    tm, tn = _tile(S, tm), _tile(Fh, tn)

    def body(df_ref, w_ref, g_ref, u_ref, o_ref):
        dfv = df_ref[...]
        for j in range(Fh // tn):
            cols = slice(j * tn, (j + 1) * tn)
            da = lax.dot_general(dfv, w_ref[cols, :], NT_DIMS, preferred_element_type=F32)
            gv = g_ref[:, cols].astype(F32)
            sg = _sigmoid(gv)
            o_ref[:, cols] = (da * u_ref[:, cols].astype(F32) * (sg * (1.0 + gv * (1.0 - sg)))).astype(o_ref.dtype)
            o_ref[:, Fh + j * tn:Fh + (j + 1) * tn] = (da * gv * sg).astype(o_ref.dtype)

    act = pl.BlockSpec((tm, Fh), lambda i: (i, 0))
    return pl.pallas_call(
        body, name=name, grid=(S // tm,),
        in_specs=[pl.BlockSpec((tm, D), lambda i: (i, 0)), pl.BlockSpec((Fh, D), lambda i: (0, 0)), act, act],
        out_specs=pl.BlockSpec((tm, 2 * Fh), lambda i: (i, 0)),
        out_shape=jax.ShapeDtypeStruct((S, 2 * Fh), BF16),
        compiler_params=_cparams(("parallel",), 56),
    )(dffn, w_down, g, u)


def _final(x1, ffn, target, g, *, name, tm=512):
    S, D = x1.shape
    tm = _tile(S, tm)

    def body(x_ref, f_ref, t_ref, g_ref, dout_ref, dffn_ref, loss_ref, dg_ref):
        i = pl.program_id(0)
        fv = f_ref[...]
        rstd = lax.rsqrt(jnp.mean(fv * fv, axis=-1, keepdims=True) + RMS_EPS)
        r = fv * rstd
        e = x_ref[...] + r * g_ref[...] - t_ref[...]
        dout = e * (1.0 / D)
        dout_ref[...] = dout
        gy = dout * g_ref[...]
        dffn_ref[...] = (rstd * (gy - r * jnp.mean(gy * r, axis=-1, keepdims=True))).astype(dffn_ref.dtype)
        lpart = jnp.full((8, LANES), 0.5 * jnp.sum(jnp.mean(e * e, axis=-1, keepdims=True)), F32)
        gpart = jnp.sum(dout * r, axis=0, keepdims=True)

        @pl.when(i == 0)
        def _():
            loss_ref[...] = lpart
            dg_ref[...] = gpart

        @pl.when(i > 0)
        def _():
            loss_ref[...] += lpart
            dg_ref[...] += gpart

    row = pl.BlockSpec((tm, D), lambda i: (i, 0))
    vec = pl.BlockSpec((1, D), lambda i: (0, 0))
    return pl.pallas_call(
        body, name=name, grid=(S // tm,),
        in_specs=[row, row, row, vec],
        out_specs=[row, row, pl.BlockSpec((8, LANES), lambda i: (0, 0)), vec],
        out_shape=[jax.ShapeDtypeStruct((S, D), F32), jax.ShapeDtypeStruct((S, D), BF16),
                   jax.ShapeDtypeStruct((8, LANES), F32), jax.ShapeDtypeStruct((1, D), F32)],
        compiler_params=_cparams(("arbitrary",)),
    )(x1, ffn, target, g)


def _after(token, value):
    return value if token is None else value + token[0, 0].astype(value.dtype)


def _local_step(x, mem, target, gains, w, late_weights, send_grads, *, tq=512):
    S = x.shape[0]
    b_pad = jnp.pad(gains["b_forget"], ((0, 0), (0, LANES - FOX_HEADS)))
    gains = dict(gains)

    h = _rms_fwd(x, gains["norm_mix_pre"], name="rms_mix_pre")
    P = _matmul(h, w["wp_t"], mode="nt", out_dtype=BF16, name="proj_in", tm=2048)
    f_logit = _matmul(h, w["wf_t"], mode="nt", out_dtype=F32, name="proj_forget")
    logf = _forget_fwd(f_logit, b_pad, name="forget_fwd")
    c_row16 = _cumsum_lanes([logf[:, :FOX_HEADS].T], reverse=False, name="forget_cumsum")
    crow = c_row16.reshape(HEAD_PAIRS, 2, S)
    o_attn, o_attn32, lse = _attn_fwd(P, crow, tq=tq, name="attn_fwd")
    w = dict(w, **late_weights(o_attn))
    o_conv, y_conv = _conv_fwd(P, w["conv_w"], gains["conv_b"], gains["conv_ln_g"], gains["conv_ln_b"], name="conv_fwd")
    mem_n = _rms_fwd(mem, gains["norm_mem"], name="rms_mem")
    kv = _matmul(mem_n, w["w_kv"], mode="nn", out_dtype=BF16, name="mem_kv")
    o_mem = _mem_fwd(P, kv, name="mem_fwd")
    merged, y1, pb = _merge_fwd(o_attn, o_conv, o_mem, P, w["wpa"], w["wpc"], w["wpm"], w["w_out"], name="merge_fwd")
    x1, h2 = _resid_norm(x, y1, gains["norm_mix_post"], gains["norm_ffn_pre"], name="resid_norm")
    g_ffn, u_ffn, act = _swiglu_fwd(h2, w["w_gu_t"], name="swiglu_fwd")
    ffn = _matmul(act, w["w_down"], mode="nn", out_dtype=F32, name="ffn_down", tk=1408)

    dout, dffn, loss_tile, d_norm_ffn_post = _final(x1, ffn, target, gains["norm_ffn_post"], name="loss_head")
    d_w_down = _matmul(act, dffn, mode="tn", out_dtype=BF16, name="dw_down", tm=1408)
    dgu = _swiglu_bwd(dffn, w["w_down"], g_ffn, u_ffn, name="swiglu_bwd")
    dh2 = _matmul(dgu, w["w_gu_t"], mode="nn", out_dtype=F32, name="d_h2", tk=1408)
    d_w_gu_t = _matmul(dgu, h2, mode="tn", out_dtype=BF16, name="dw_gate_up", tm=1408)
    dx1, d_norm_ffn_pre = _rms_bwd(x1, [dh2], gains["norm_ffn_pre"], dout, out_dtype=F32, name="rms_ffn_pre_bwd")

    dy1, d_norm_mix_post = _rms_bwd(y1, [dx1], gains["norm_mix_post"], None, out_dtype=BF16, name="rms_mix_post_bwd")
    d_w_out = _matmul(merged, dy1, mode="tn", out_dtype=BF16, name="dw_out")
    *dpb, dgl = _merge_bwd(dy1, w["w_out"], P, pb, name="merge_bwd")
    do_attn = _matmul(dpb[0], w["wpa"], mode="nt", out_dtype=BF16, name="d_o_attn")
    do_conv = _matmul(dpb[1], w["wpc"], mode="nt", out_dtype=F32, name="d_o_conv")
    do_mem = _matmul(dpb[2], w["wpm"], mode="nt", out_dtype=BF16, name="d_o_mem")
    d_wpa = _matmul(o_attn, dpb[0], mode="tn", out_dtype=BF16, name="dw_proj_attn")
    d_wpc = _matmul(o_conv, dpb[1], mode="tn", out_dtype=BF16, name="dw_proj_conv")
    d_wpm = _matmul(o_mem, dpb[2], mode="tn", out_dtype=BF16, name="dw_proj_mem")

    dq_mem, dkv = _mem_bwd(P, do_mem, kv, name="mem_bwd")
    dkv_b = dkv.astype(BF16)
    d_w_kv = _matmul(mem_n, dkv_b, mode="tn", out_dtype=BF16, name="dw_kv")
    dmem_n = _matmul(dkv_b, w["w_kv"], mode="nt", out_dtype=F32, name="d_mem_n")
    _, d_norm_mem = _rms_bwd(mem, [dmem_n], gains["norm_mem"], None, out_dtype=BF16, name="rms_mem_bwd")

    dglu, d_conv_w, conv_small = _conv_bwd(P, do_conv, y_conv, w["conv_w"], gains["conv_ln_g"], gains["conv_ln_b"], name="conv_bwd")
    sent = send_grads("body", dict(conv_w=d_conv_w[:, :CONV_WIDTH].astype(BF16), w_kv_mem=d_w_kv, w_proj_attn=d_wpa,
                                   w_proj_conv=d_wpc, w_proj_mem=d_wpm, w_out=d_w_out, w_gate_up=d_w_gu_t, w_down=d_w_down))

    dq, dk, dv, dc, dcq = _attn_bwd(P, do_attn, o_attn32, lse, _after(sent, crow), tq=tq, name="attn_bwd")
    dlogf16 = _cumsum_lanes([dc.reshape(FOX_HEADS, S), dcq.reshape(FOX_HEADS, S)], reverse=True, name="forget_cumsum_bwd")
    dlogf = jnp.pad(dlogf16.T, ((0, 0), (0, LANES - FOX_HEADS)))
    df, d_b_pad = _forget_bwd(f_logit, b_pad, dlogf, name="forget_bwd")

    d_proj = [dq, dk, dv, dglu, dq_mem, dgl]
    d_wp_t = _matmul_pieces(d_proj, h, mode="tn", out_dtype=BF16, name="dw_in")
    d_wf_t = _matmul(df, h, mode="tn", out_dtype=BF16, name="dw_forget")
    n_qkv = 3 * D_MODEL
    pieces = [(d_wp_t[:n_qkv], n_qkv), (d_wf_t[:FOX_HEADS], FOX_HEADS), (d_wp_t[n_qkv:], P_WIDTH - n_qkv)]
    n_own = (P_WIDTH + FOX_HEADS) // N_DEV
    blocks = [jnp.concatenate(_rows(pieces, j * n_own, (j + 1) * n_own), axis=0) for j in range(N_DEV)]
    sent = send_grads("w_in", dict(w_in=jnp.stack(blocks)))
    dh_p = _matmul_pieces(d_proj, w["wp_t"], mode="nn", out_dtype=F32, name="d_h", after=sent)
    dh_f = _matmul(df, w["wf_t"], mode="nn", out_dtype=F32, name="d_h_forget")
    grad_x, d_norm_mix_pre = _rms_bwd(x, [dh_p, dh_f], gains["norm_mix_pre"], dx1, out_dtype=F32, name="rms_mix_pre_bwd")

    small = dict(norm_mix_pre=d_norm_mix_pre, norm_mix_post=d_norm_mix_post, norm_mem=d_norm_mem,
                 conv_b=conv_small[2:3], conv_ln_g=conv_small[0:1], conv_ln_b=conv_small[1:2],
                 norm_ffn_pre=d_norm_ffn_pre, norm_ffn_post=d_norm_ffn_post, b_forget=d_b_pad[:, :FOX_HEADS])
    return loss_tile, grad_x, small


def _mesh_pos():
    return lax.axis_index("x"), lax.axis_index("y"), lax.axis_index("c")


def _flip(pos, d):
    x, y, c = pos
    return (1 - x if d & 4 else x, 1 - y if d & 2 else y, 1 - c if d & 1 else c)


def _flat(pos):
    x, y, c = pos
    return 4 * x + 2 * y + c


def _gather_two_level(src, *, name):
    def body(src_ref, out_ref, token, send_sems, recv_sems, local_sem):
        me = _mesh_pos()
        my = _flat(me)
        sibling = _flip(me, 1)
        far = [_flip(me, d) for d in (4, 2, 6)]

        def copy(k, block, to, own=False):
            return pltpu.make_async_remote_copy(
                src_ref=src_ref if own else out_ref.at[_flat(block)], dst_ref=out_ref.at[_flat(block)],
                send_sem=send_sems.at[k], recv_sem=recv_sems.at[k], device_id=to, device_id_type=pl.DeviceIdType.MESH)

        local = pltpu.make_async_copy(src_ref, out_ref.at[my], local_sem)
        local.start()
        first = [copy(0, me, sibling, own=True)] + [copy(1 + j, me, peer, own=True) for j, peer in enumerate(far)]
        for cp in first:
            cp.start()
        passed = [copy(4 + j, peer, sibling) for j, peer in enumerate(far)]
        for j, peer in enumerate(far):
            copy(1 + j, peer, me).wait_recv()
            passed[j].start()
        copy(0, sibling, me).wait_recv()
        for j, peer in enumerate(far):
            copy(4 + j, _flip(peer, 1), me).wait_recv()
        for cp in first + passed:
            cp.wait_send()
        local.wait()
        token[...] = jnp.zeros_like(token)

    n_copy = N_DEV - 1
    out, token = pl.pallas_call(
        body, name=name,
        in_specs=[pl.BlockSpec(memory_space=pl.ANY)],
        out_specs=[pl.BlockSpec(memory_space=pl.ANY), pl.BlockSpec(memory_space=pltpu.VMEM)],
        out_shape=[jax.ShapeDtypeStruct((N_DEV,) + tuple(src.shape), src.dtype), jax.ShapeDtypeStruct((8, LANES), F32)],
        scratch_shapes=[pltpu.SemaphoreType.DMA((n_copy,)), pltpu.SemaphoreType.DMA((n_copy,)), pltpu.SemaphoreType.DMA(())],
    )(src)
    return [out], token


def _exchange(srcs, *, scatter, name):
    T = len(srcs)
    n_peer = N_DEV - 1

    def body(*refs):
        src_refs, out_refs = refs[:T], refs[T:2 * T]
        token, send_sems, recv_sems, local_sems = refs[2 * T:]
        me = _mesh_pos()
        my = _flat(me)

        def copy(t, d):
            peer = _flip(me, d)
            return pltpu.make_async_remote_copy(
                src_ref=src_refs[t].at[_flat(peer)] if scatter else src_refs[t],
                dst_ref=out_refs[t].at[my],
                send_sem=send_sems.at[t * n_peer + d - 1], recv_sem=recv_sems.at[t * n_peer + d - 1],
                device_id=peer, device_id_type=pl.DeviceIdType.MESH)

        def arrival(t, d):
            peer = _flip(me, d)
            return pltpu.make_async_remote_copy(
                src_ref=src_refs[t].at[my] if scatter else src_refs[t],
                dst_ref=out_refs[t].at[_flat(peer)],
                send_sem=send_sems.at[t * n_peer + d - 1], recv_sem=recv_sems.at[t * n_peer + d - 1],
                device_id=peer, device_id_type=pl.DeviceIdType.MESH)

        local = [pltpu.make_async_copy(src_refs[t].at[my] if scatter else src_refs[t], out_refs[t].at[my], local_sems.at[t])
                 for t in range(T)]
        for cp in local:
            cp.start()
        sends = [copy(t, d) for d in range(1, N_DEV) for t in range(T)]
        for cp in sends:
            cp.start()
        for d in range(1, N_DEV):
            for t in range(T):
                arrival(t, d).wait_recv()
        for cp in sends:
            cp.wait_send()
        for cp in local:
            cp.wait()
        token[...] = jnp.zeros_like(token)

    outs = pl.pallas_call(
        body, name=name,
        in_specs=[pl.BlockSpec(memory_space=pl.ANY)] * T,
        out_specs=[pl.BlockSpec(memory_space=pl.ANY)] * T + [pl.BlockSpec(memory_space=pltpu.VMEM)],
        out_shape=[jax.ShapeDtypeStruct((N_DEV,) + tuple(s.shape[-2:]), s.dtype) for s in srcs] + [jax.ShapeDtypeStruct((8, LANES), F32)],
        scratch_shapes=[pltpu.SemaphoreType.DMA((T * n_peer,)), pltpu.SemaphoreType.DMA((T * n_peer,)), pltpu.SemaphoreType.DMA((T,))],
    )(*srcs)
    return list(outs[:T]), outs[T]


HBM_SPEC = pl.BlockSpec(memory_space=pltpu.HBM)
SEM_SPEC = pl.BlockSpec(memory_space=pltpu.SEMAPHORE)


def _split_copies(src_refs, land_refs, send_sems, recv_sems, scatter):
    me = _mesh_pos()
    my = _flat(me)
    n_peer = N_DEV - 1
    out = []
    for d in range(1, N_DEV):
        peer = _flip(me, d)
        for t, (src, land) in enumerate(zip(src_refs, land_refs)):
            k = t * n_peer + d - 1
            start = pltpu.make_async_remote_copy(
                src_ref=src.at[_flat(peer)] if scatter else src, dst_ref=land.at[my],
                send_sem=send_sems.at[k], recv_sem=recv_sems.at[k], device_id=peer, device_id_type=pl.DeviceIdType.MESH)
            arrive = pltpu.make_async_remote_copy(
                src_ref=src.at[my] if scatter else src, dst_ref=land.at[_flat(peer)],
                send_sem=send_sems.at[k], recv_sem=recv_sems.at[k], device_id=peer, device_id_type=pl.DeviceIdType.MESH)
            out.append((start, arrive))
    return out


def _local_copies(src_refs, land_refs, local_sems, scatter):
    my = _flat(_mesh_pos())
    return [pltpu.make_async_copy(src.at[my] if scatter else src, land.at[my], local_sems.at[t])
            for t, (src, land) in enumerate(zip(src_refs, land_refs))]


def _exchange_start(srcs, *, scatter, name):
    T = len(srcs)
    n_sem = T * (N_DEV - 1)
    lands = [lax.empty((N_DEV,) + tuple(s.shape[-2:]), s.dtype) for s in srcs]

    def body(*refs):
        src_refs, land_refs = refs[:T], refs[T:2 * T]
        send_sems, recv_sems, local_sems = refs[2 * T:2 * T + 3]
        token = refs[-1]
        for cp in _local_copies(src_refs, land_refs, local_sems, scatter):
            cp.start()
        for start, _ in _split_copies(src_refs, land_refs, send_sems, recv_sems, scatter):
            start.start()
        token[...] = jnp.zeros_like(token)

    hbm = lambda a: pltpu.HBM(a.shape, a.dtype)
    outs = pl.pallas_call(
        body, name=name,
        in_specs=[HBM_SPEC] * (2 * T),
        out_specs=(SEM_SPEC, SEM_SPEC, SEM_SPEC, *[HBM_SPEC] * (2 * T), pl.BlockSpec(memory_space=pltpu.VMEM)),
        out_shape=(pltpu.SemaphoreType.DMA((n_sem,)), pltpu.SemaphoreType.DMA((n_sem,)), pltpu.SemaphoreType.DMA((T,)),
                   *[hbm(s) for s in srcs], *[hbm(a) for a in lands], jax.ShapeDtypeStruct((8, LANES), F32)),
        input_output_aliases={t: 3 + t for t in range(2 * T)},
        compiler_params=pltpu.CompilerParams(has_side_effects=pltpu.SideEffectType.DATAFLOW_SIDE_EFFECTING),
    )(*[pltpu.with_memory_space_constraint(s, pltpu.HBM) for s in srcs],
      *[pltpu.with_memory_space_constraint(a, pltpu.HBM) for a in lands])
    return (outs[0], outs[1], outs[2], list(outs[3:3 + T]), list(outs[3 + T:3 + 2 * T])), outs[-1]


def _exchange_wait(handle, after, *, scatter, name):
    send_sems, recv_sems, local_sems, srcs, lands = handle
    T = len(srcs)

    def body(*refs):
        src_refs, land_refs = refs[:T], refs[T:2 * T]
        s_sems, r_sems, l_sems = refs[2 * T:2 * T + 3]
        for cp in _local_copies(src_refs, land_refs, l_sems, scatter):
            cp.wait()
        for start, arrive in _split_copies(src_refs, land_refs, s_sems, r_sems, scatter):
            start.wait_send()
            arrive.wait_recv()

    hbm = lambda a: pltpu.HBM(a.shape, a.dtype)
    outs = pl.pallas_call(
        body, name=name,
        in_specs=[HBM_SPEC] * (2 * T) + [SEM_SPEC, SEM_SPEC, SEM_SPEC, pl.BlockSpec(memory_space=pl.ANY)],
        out_specs=[HBM_SPEC] * (2 * T),
        out_shape=[hbm(s) for s in srcs] + [hbm(a) for a in lands],
        input_output_aliases={t: t for t in range(2 * T)},
        compiler_params=pltpu.CompilerParams(has_side_effects=pltpu.SideEffectType.DATAFLOW_SIDE_EFFECTING),
    )(*srcs, *lands, send_sems, recv_sems, local_sems, after)
    return list(outs[T:])


def _adamw(w, slots, m, v, *, name):
    R, C = w.shape
    row_tiles = [t for t in range(16, 513, 16) if R % t == 0]
    tr = R if R <= 512 or not row_tiles else max(row_tiles)
    tc = C if tr < R or R <= 512 else LANES
    assert R % tr == 0 and C % tc == 0, (R, C, tr, tc)

    def body(w_ref, s_ref, m_ref, v_ref, g_ref, d_ref, m2_ref, v2_ref):
        gv = s_ref[0].astype(F32)
        for j in range(1, N_DEV):
            gv = gv + s_ref[j].astype(F32)
        g_ref[...] = gv
        m2 = ADAM_B1 * m_ref[...] + (1.0 - ADAM_B1) * gv
        v2 = ADAM_B2 * v_ref[...] + (1.0 - ADAM_B2) * (gv * gv)
        m_hat = m2 / (1.0 - ADAM_B1 ** ADAM_STEP)
        v_hat = v2 / (1.0 - ADAM_B2 ** ADAM_STEP)
        d_ref[...] = -ADAM_LR * (m_hat / (jnp.sqrt(v_hat) + ADAM_EPS) + ADAM_WD * w_ref[...])
        m2_ref[...] = m2
        v2_ref[...] = v2

    blk = pl.BlockSpec((tr, tc), lambda i, j: (i, j))
    sds = jax.ShapeDtypeStruct((R, C), F32)
    return pl.pallas_call(
        body, name=name, grid=(R // tr, C // tc),
        in_specs=[blk, pl.BlockSpec((N_DEV, tr, tc), lambda i, j: (0, i, j)), blk, blk],
        out_specs=[blk] * 4, out_shape=[sds] * 4,
        compiler_params=_cparams(("parallel", "parallel")),
    )(w, slots, m, v)


SHARDED = (
    ("w_in", (1154, 1024), "row"), ("conv_w", (31, 128), "col"), ("w_kv_mem", (1024, 256), "col"),
    ("w_proj_attn", (128, 1024), "row"), ("w_proj_conv", (128, 1024), "row"), ("w_proj_mem", (128, 1024), "row"),
    ("w_out", (128, 1024), "row"), ("w_gate_up", (704, 1024), "row"), ("w_down", (352, 1024), "row"),
)
TRANSPOSED = ("w_in", "w_gate_up")
SMALL = ("norm_mix_pre", "norm_mix_post", "norm_mem", "conv_b", "conv_ln_g", "conv_ln_b", "norm_ffn_pre", "norm_ffn_post", "b_forget")
SMALL_ROWS = 16
LOSS_ROW = len(SMALL)
WEIGHT_ORDER = ("norm_mix_pre", "norm_mix_post", "norm_mem", "w_in", "b_forget", "conv_w", "conv_b", "conv_ln_g", "conv_ln_b",
                "w_kv_mem", "w_proj_attn", "w_proj_conv", "w_proj_mem", "w_out", "norm_ffn_pre", "norm_ffn_post", "w_gate_up", "w_down")


def _rows(pieces, lo, hi):
    out, start = [], 0
    for a, n in pieces:
        a0, a1 = max(lo, start), min(hi, start + n)
        if a0 < a1:
            out.append(a[a0 - start:a1 - start])
        start += n
    return out


def _to_full(blocks8, kind):
    n, r, c = blocks8.shape
    if kind == "col":
        return jnp.concatenate([blocks8[j] for j in range(n)], axis=1)
    return blocks8.reshape(n * r, c)


def _to_blocks(full, kind):
    if kind == "col":
        c = full.shape[1] // N_DEV
        return jnp.stack([full[:, j * c:(j + 1) * c] for j in range(N_DEV)])
    nr, c = full.shape
    return full.reshape(N_DEV, nr // N_DEV, c)


def kernel(x, mem, norm_mix_pre, norm_mix_post, norm_mem, w_in, b_forget, conv_w, conv_b, conv_ln_g, conv_ln_b, w_kv_mem, w_proj_attn, w_proj_conv, w_proj_mem, w_out, norm_ffn_pre, norm_ffn_post, w_gate_up, w_down, loss_target, m_norm_mix_pre, m_norm_mix_post, m_norm_mem, m_w_in, m_b_forget, m_conv_w, m_conv_b, m_conv_ln_g, m_conv_ln_b, m_w_kv_mem, m_w_proj_attn, m_w_proj_conv, m_w_proj_mem, m_w_out, m_norm_ffn_pre, m_norm_ffn_post, m_w_gate_up, m_w_down, v_norm_mix_pre, v_norm_mix_post, v_norm_mem, v_w_in, v_b_forget, v_conv_w, v_conv_b, v_conv_ln_g, v_conv_ln_b, v_w_kv_mem, v_w_proj_attn, v_w_proj_conv, v_w_proj_mem, v_w_out, v_norm_ffn_pre, v_norm_ffn_post, v_w_gate_up, v_w_down):
    given = dict(locals())
    weights = {n: given[n] for n in WEIGHT_ORDER}
    moments_m = {n: given["m_" + n] for n in WEIGHT_ORDER}
    moments_v = {n: given["v_" + n] for n in WEIGHT_ORDER}
    kind = {name: k for name, _, k in SHARDED}
    names = [name for name, _, _ in SHARDED]

    def local(a, n):
        return jnp.swapaxes(a[0], 0, 1) if n in TRANSPOSED else a[0]

    blocks = {n: local(weights[n], n).astype(F32 if n == "conv_w" else BF16) for n in names}
    late = [n for n in names if n != "w_in"]
    (w_in_blocks,), w_in_token = _gather_two_level(blocks["w_in"], name="gather_w_in")
    blocks["w_kv_mem"] = _after(w_in_token, blocks["w_kv_mem"])
    late_handle, late_token = _exchange_start([blocks[n] for n in late], scatter=False, name="gather_rest_start")
    n_qkv = 3 * D_MODEL
    w_in_t = _to_full(w_in_blocks, "row")
    w_early = dict(
        wp_t=jnp.concatenate([w_in_t[:n_qkv], w_in_t[n_qkv + FOX_HEADS:]], axis=0),
        wf_t=jnp.pad(w_in_t[n_qkv:n_qkv + FOX_HEADS], ((0, LANES - FOX_HEADS), (0, 0))),
    )

    def late_weights(after):
        got = dict(zip(late, _exchange_wait(late_handle, after, scatter=False, name="gather_rest_wait")))
        full = {n: _to_full(got[n], kind[n]) for n in late if n != "conv_w"}
        return dict(
            conv_w=got["conv_w"],
            w_kv=full["w_kv_mem"], wpa=full["w_proj_attn"], wpc=full["w_proj_conv"], wpm=full["w_proj_mem"],
            w_out=full["w_out"], w_gu_t=full["w_gate_up"], w_down=full["w_down"],
        )

    sent = []

    def send_grads(group, g):
        order = [n for n in names if n in g]
        per_owner = [g[n] if g[n].ndim == 3 else _to_blocks(g[n], kind[n]) for n in order]
        handle, token = _exchange_start(per_owner, scatter=True, name="scatter_" + group + "_start")
        sent.append((group, order, handle))
        return token

    gains = {n: weights[n] for n in SMALL}
    gains["norm_mix_pre"] = _after(late_token, gains["norm_mix_pre"])

    loss_tile, grad_x, small = _local_step(x[0], mem[0], loss_target[0], gains, w_early, late_weights, send_grads)

    g_recv = {}
    for group, order, handle in sent:
        g_recv.update(zip(order, _exchange_wait(handle, grad_x, scatter=True, name="scatter_" + group + "_wait")))

    rows = [jnp.pad(small[n], ((0, 0), (0, D_MODEL - small[n].shape[1]))) for n in SMALL]
    rows.append(jnp.broadcast_to(loss_tile[0:1, 0:1], (1, D_MODEL)))
    rows.append(jnp.zeros((SMALL_ROWS - len(rows), D_MODEL), F32))
    (small_recv,), _ = _exchange([jnp.concatenate(rows, axis=0)], scatter=False, name="gather_small")

    def slab_of(d, fill):
        rows = [jnp.pad(d[n], ((0, 0), (0, D_MODEL - d[n].shape[1])), constant_values=fill) for n in SMALL]
        rows.append(jnp.full((SMALL_ROWS - len(rows), D_MODEL), fill, F32))
        return jnp.concatenate(rows, axis=0)

    grads, delta, new_m, new_v = {}, {}, {}, {}
    sg, sd, sm, sv = _adamw(slab_of(weights, 0.0), small_recv, slab_of(moments_m, 0.0), slab_of(moments_v, 1.0), name="adamw_small")
    loss = sg[LOSS_ROW, 0]
    for i, n in enumerate(SMALL):
        c = weights[n].shape[1]
        grads[n], delta[n], new_m[n], new_v[n] = sg[i:i + 1, :c], sd[i:i + 1, :c], sm[i:i + 1, :c], sv[i:i + 1, :c]
    for n in names:
        outs = _adamw(local(weights[n], n), g_recv[n], local(moments_m[n], n), local(moments_v[n], n), name="adamw_" + n)
        grads[n], delta[n], new_m[n], new_v[n] = [(jnp.swapaxes(o, 0, 1) if n in TRANSPOSED else o)[None] for o in outs]

    return (loss, grad_x[None], *[grads[n] for n in WEIGHT_ORDER], *[delta[n] for n in WEIGHT_ORDER],
            *[new_m[n] for n in WEIGHT_ORDER], *[new_v[n] for n in WEIGHT_ORDER])
```

```python
import functools

import jax
import jax.numpy as jnp
from jax import lax
from jax.experimental import pallas as pl
from jax.experimental.pallas import tpu as pltpu

F32 = jnp.float32
BF16 = jnp.bfloat16

D_MODEL = 1024
N_DEV = 8
FOX_HEADS = 16
FOX_HEAD_DIM = 64
HEAD_PAIRS = FOX_HEADS // 2
MEM_HEADS = 4
MEM_HEAD_DIM = D_MODEL // MEM_HEADS
CONV_WIDTH = 31
CONV_HALO = 32
CONV_CHUNK = 16
SUBLANES = 8
FFN_HIDDEN = 2816
RMS_EPS = 1e-6
LN_EPS = 1e-5
ADAM_LR = 0.001
ADAM_B1 = 0.9
ADAM_B2 = 0.999
ADAM_EPS = 1e-08
ADAM_WD = 0.01
ADAM_STEP = 10
KV_STEP = 2
KV_STEP_FWD = 4
NEG_BIG = -1e30
LANES = 128

PB_Q, PB_K, PB_V, PB_A, PB_GATE, PB_QMEM, PB_G0 = 0, 1, 2, 3, 4, 5, 6
P_WIDTH = 9 * D_MODEL

NT_DIMS = (((1,), (1,)), ((), ()))
TN_DIMS = (((0,), (0,)), ((), ()))


def _cparams(sem, vmem_mb=None):
    kw = dict(dimension_semantics=sem)
    if vmem_mb is not None:
        kw["vmem_limit_bytes"] = vmem_mb * 1024 * 1024
    return pltpu.CompilerParams(**kw)


def _tile(dim, want):
    t = min(dim, want)
    assert dim % t == 0, (dim, want)
    return t


def _sigmoid(z):
    return 1.0 / (1.0 + jnp.exp(-z))


def _matmul(a, b, *, mode, out_dtype, name, tm=1024, tn=1024, tk=1024, after=None):
    if mode == "nn":
        (M, K), (K2, N) = a.shape, b.shape
    elif mode == "nt":
        (M, K), (N, K2) = a.shape, b.shape
    else:
        (K, M), (K2, N) = a.shape, b.shape
    assert K == K2, (a.shape, b.shape, mode)
    tm, tn, tk = _tile(M, tm), _tile(N, tn), _tile(K, tk)
    nk = K // tk
    dims = {"nn": (((1,), (0,)), ((), ())), "nt": NT_DIMS, "tn": TN_DIMS}[mode]

    n_extra = 0 if after is None else 1

    def body(a_ref, b_ref, *rest):
        o_ref, scratch = rest[n_extra], rest[n_extra + 1:]
        part = lax.dot_general(a_ref[...], b_ref[...], dims, preferred_element_type=F32)
        if nk == 1:
            o_ref[...] = part.astype(o_ref.dtype)
        else:
            acc_ref, = scratch
            k = pl.program_id(2)

            @pl.when(k == 0)
            def _():
                acc_ref[...] = part

            @pl.when(k > 0)
            def _():
                acc_ref[...] += part

            @pl.when(k == nk - 1)
            def _():
                o_ref[...] = acc_ref[...].astype(o_ref.dtype)

    a_spec = pl.BlockSpec((tk, tm), lambda j, i, k: (k, i)) if mode == "tn" else pl.BlockSpec((tm, tk), lambda j, i, k: (i, k))
    b_spec = pl.BlockSpec((tn, tk), lambda j, i, k: (j, k)) if mode == "nt" else pl.BlockSpec((tk, tn), lambda j, i, k: (k, j))
    return pl.pallas_call(
        body,
        name=name,
        grid=(N // tn, M // tm, nk),
        in_specs=[a_spec, b_spec] + [pl.BlockSpec((8, LANES), lambda j, i, k: (0, 0))] * n_extra,
        out_specs=pl.BlockSpec((tm, tn), lambda j, i, k: (i, j)),
        out_shape=jax.ShapeDtypeStruct((M, N), out_dtype),
        scratch_shapes=[pltpu.VMEM((tm, tn), F32)] if nk > 1 else [],
        compiler_params=_cparams(("parallel", "parallel", "arbitrary"), 56),
    )(a, b, *([] if after is None else [after]))


def _matmul_pieces(pieces, b, *, mode, out_dtype, name, after=None):
    blk = 1024
    T = len(pieces)
    S = pieces[0].shape[0]
    counts = [p.shape[1] // blk for p in pieces]
    offs = [sum(counts[:t]) for t in range(T)]
    n_a = sum(counts)
    N = b.shape[1]
    n_extra = 0 if after is None else 1
    if mode == "nn":
        tm = _tile(S, blk)
        grid, n_red = (S // tm, n_a), n_a
        which = lambda i, k: k
        a_specs = [pl.BlockSpec((tm, blk), lambda i, k, o=o, c=c: (i, jnp.clip(k - o, 0, c - 1))) for o, c in zip(offs, counts)]
        b_spec = pl.BlockSpec((blk, N), lambda i, k: (k, 0))
        out_spec, out_rows, dims = pl.BlockSpec((tm, N), lambda i, k: (i, 0)), S, (((1,), (0,)), ((), ()))
    else:
        tk = _tile(S, blk)
        grid, n_red = (n_a, S // tk), S // tk
        which = lambda m, k: m

        def a_index(m, k, o, c):
            mine = (m >= o) & (m < o + c)
            return jnp.where(mine, k, 0), jnp.clip(m - o, 0, c - 1)

        a_specs = [pl.BlockSpec((tk, blk), functools.partial(a_index, o=o, c=c)) for o, c in zip(offs, counts)]
        b_spec = pl.BlockSpec((tk, N), lambda m, k: (k, 0))
        out_spec, out_rows, dims = pl.BlockSpec((blk, N), lambda m, k: (m, 0)), n_a * blk, TN_DIMS

    def body(*refs):
        a_refs, b_ref = refs[:T], refs[T]
        o_ref, acc_ref = refs[T + 1 + n_extra], refs[T + 2 + n_extra]
        a_blk = which(pl.program_id(0), pl.program_id(1))
        k = pl.program_id(1)
        for t in range(T):
            @pl.when((a_blk >= offs[t]) & (a_blk < offs[t] + counts[t]))
            def _(t=t):
                part = lax.dot_general(a_refs[t][...], b_ref[...], dims, preferred_element_type=F32)

                @pl.when(k == 0)
                def _():
                    acc_ref[...] = part

                @pl.when(k > 0)
                def _():
                    acc_ref[...] += part

        @pl.when(k == n_red - 1)
        def _():
            o_ref[...] = acc_ref[...].astype(o_ref.dtype)

    return pl.pallas_call(
        body, name=name, grid=grid,
        in_specs=a_specs + [b_spec] + [pl.BlockSpec((8, LANES), lambda i, k: (0, 0))] * n_extra,
        out_specs=out_spec,
        out_shape=jax.ShapeDtypeStruct((out_rows, N), out_dtype),
        scratch_shapes=[pltpu.VMEM(out_spec.block_shape, F32)],
        compiler_params=_cparams(("parallel", "arbitrary"), 56),
    )(*pieces, b, *([] if after is None else [after]))


def _rms_fwd(x, g, *, name, tm=512):
    S, D = x.shape
    tm = _tile(S, tm)

    def body(x_ref, g_ref, o_ref):
        xv = x_ref[...]
        rstd = lax.rsqrt(jnp.mean(xv * xv, axis=-1, keepdims=True) + RMS_EPS)
        o_ref[...] = (xv * rstd * g_ref[...]).astype(o_ref.dtype)

    return pl.pallas_call(
        body, name=name, grid=(S // tm,),
        in_specs=[pl.BlockSpec((tm, D), lambda i: (i, 0)), pl.BlockSpec((1, D), lambda i: (0, 0))],
        out_specs=pl.BlockSpec((tm, D), lambda i: (i, 0)),
        out_shape=jax.ShapeDtypeStruct((S, D), BF16),
        compiler_params=_cparams(("parallel",)),
    )(x, g)


def _rms_bwd(xin, dys, g, res, *, out_dtype, name, tm=512):
    S, D = xin.shape
    tm = _tile(S, tm)
    n_dy = len(dys)
    has_res = res is not None

    def body(*refs):
        x_ref, g_ref = refs[0], refs[1]
        dy_refs = refs[2:2 + n_dy]
        pos = 2 + n_dy
        res_ref = refs[pos] if has_res else None
        pos += int(has_res)
        dx_ref, dg_ref = refs[pos], refs[pos + 1]
        i = pl.program_id(0)
        xv = x_ref[...]
        dy = dy_refs[0][...].astype(F32)
        for r in dy_refs[1:]:
            dy = dy + r[...].astype(F32)
        rstd = lax.rsqrt(jnp.mean(xv * xv, axis=-1, keepdims=True) + RMS_EPS)
        xhat = xv * rstd
        gy = dy * g_ref[...]
        dx = rstd * (gy - xhat * jnp.mean(gy * xhat, axis=-1, keepdims=True))
        if has_res:
            dx = dx + res_ref[...]
        dx_ref[...] = dx.astype(dx_ref.dtype)
        part = jnp.sum(dy * xhat, axis=0, keepdims=True)

        @pl.when(i == 0)
        def _():
            dg_ref[...] = part

        @pl.when(i > 0)
        def _():
            dg_ref[...] += part

    row = pl.BlockSpec((tm, D), lambda i: (i, 0))
    vec = pl.BlockSpec((1, D), lambda i: (0, 0))
    ins = [xin, g] + list(dys) + ([res] if has_res else [])
    return pl.pallas_call(
        body, name=name, grid=(S // tm,),
        in_specs=[row, vec] + [row] * n_dy + ([row] if has_res else []),
        out_specs=[row, vec],
        out_shape=[jax.ShapeDtypeStruct((S, D), out_dtype), jax.ShapeDtypeStruct((1, D), F32)],
        compiler_params=_cparams(("arbitrary",)),
    )(*ins)


def _head_mask(hh, shape):
    lane = lax.broadcasted_iota(jnp.int32, shape, len(shape) - 1)
    return (lane // FOX_HEAD_DIM) == hh


def _block_constants(cr_ref, i, tq):
    first = cr_ref[0, :, pl.ds(pl.multiple_of(i * tq, tq), LANES)]
    return [first[hh:hh + 1, 0:1] for hh in range(2)]


def _attn_fwd(P, crow, *, tq, name):
    S = P.shape[0]
    tq = _tile(S, tq)
    nq = S // tq
    scale = FOX_HEAD_DIM ** -0.5

    def body(q_ref, k_ref, v_ref, cr_ref, o_ref, o32_ref, lse_ref):
        i = pl.program_id(1)
        q = q_ref[...] * jnp.asarray(scale, BF16)
        row = lax.broadcasted_iota(jnp.int32, (tq, tq), 0)
        col = lax.broadcasted_iota(jnp.int32, (tq, tq), 1)
        causal = col <= row
        hms = [_head_mask(hh, (tq, LANES)) for hh in range(2)]
        qhs = [jnp.where(hm, q, jnp.zeros_like(q)) for hm in hms]
        cbs = _block_constants(cr_ref, i, tq)

        def step(block, n_blocks, carry, masked):
            tk = n_blocks * tq
            start = pl.multiple_of(block * tq, tk)
            kj = k_ref[pl.ds(start, tk), :]
            vj = v_ref[pl.ds(start, tk), :]
            lane_k = lax.broadcasted_iota(jnp.int32, (tk, LANES), 1)
            new = []
            for hh in range(2):
                m, acc = carry[hh]
                bias = cbs[hh] - cr_ref[0, hh:hh + 1, pl.ds(start, tk)]
                s = lax.dot_general(qhs[hh], kj, NT_DIMS, preferred_element_type=F32) + bias
                if masked:
                    s = jnp.where(causal, s, NEG_BIG)
                m_new = jnp.maximum(m, jnp.max(s, axis=1, keepdims=True))
                alpha = jnp.exp(m - m_new)
                p = jnp.exp(s - m_new)
                vh = jnp.where(lane_k == (1 - hh) * FOX_HEAD_DIM, jnp.ones_like(vj), vj)
                acc = alpha * acc + jnp.dot(p.astype(BF16), vh, preferred_element_type=F32)
                new.append((m_new, acc))
            return tuple(new)

        lane = lax.broadcasted_iota(jnp.int32, (tq, LANES), 1)
        ones_lane = [lane == (1 - hh) * FOX_HEAD_DIM for hh in range(2)]
        init = (jnp.full((tq, 1), NEG_BIG, F32), jnp.zeros((tq, LANES), F32))
        n_wide = i // KV_STEP_FWD
        done = n_wide * KV_STEP_FWD
        n_pair = (i - done) // 2
        carry = lax.fori_loop(0, n_wide, lambda j, c: step(j * KV_STEP_FWD, KV_STEP_FWD, c, False), (init, init))
        carry = lax.fori_loop(0, n_pair, lambda j, c: step(done + 2 * j, 2, c, False), carry)
        carry = lax.fori_loop(done + 2 * n_pair, i, lambda j, c: step(j, 1, c, False), carry)
        carry = step(i, 1, carry, True)
        ls = [jnp.sum(jnp.where(ones_lane[hh], carry[hh][1], 0.0), axis=1, keepdims=True) for hh in range(2)]
        out = jnp.where(hms[0], carry[0][1] / ls[0], carry[1][1] / ls[1])
        for hh in range(2):
            lse_ref[hh] = carry[hh][0] + jnp.log(ls[hh])
        o_ref[...] = out.astype(o_ref.dtype)
        o32_ref[...] = out

    nblk = D_MODEL // LANES
    return pl.pallas_call(
        body, name=name, grid=(HEAD_PAIRS, nq),
        in_specs=[
            pl.BlockSpec((tq, LANES), lambda p, i: (i, PB_Q * nblk + p)),
            pl.BlockSpec((S, LANES), lambda p, i: (0, PB_K * nblk + p)),
            pl.BlockSpec((S, LANES), lambda p, i: (0, PB_V * nblk + p)),
            pl.BlockSpec((1, 2, S), lambda p, i: (p, 0, 0)),
        ],
        out_specs=[
            pl.BlockSpec((tq, LANES), lambda p, i: (i, p)),
            pl.BlockSpec((tq, LANES), lambda p, i: (i, p)),
            pl.BlockSpec((2, tq, 1), lambda p, i: (p, i, 0)),
        ],
        out_shape=[jax.ShapeDtypeStruct((S, D_MODEL), BF16), jax.ShapeDtypeStruct((S, D_MODEL), F32),
                   jax.ShapeDtypeStruct((FOX_HEADS, S, 1), F32)],
        compiler_params=_cparams(("parallel", "arbitrary"), 56),
    )(P, P, P, crow)


def _attn_bwd(P, do, o32, lse, crow, *, tq, name):
    S = P.shape[0]
    tq = _tile(S, tq)
    nq = S // tq
    scale = FOX_HEAD_DIM ** -0.5

    def body(q_ref, k_ref, v_ref, do_ref, o_ref, lse_ref, cr_ref, dq_ref, dk_out, dv_out, dc_ref, dcq_ref, dk_ref, dv_ref):
        i = pl.program_id(1)

        @pl.when(i == 0)
        def _():
            dk_ref[...] = jnp.zeros_like(dk_ref)
            dv_ref[...] = jnp.zeros_like(dv_ref)
            dc_ref[...] = jnp.zeros_like(dc_ref)

        q = q_ref[...] * jnp.asarray(scale, BF16)
        do_v = do_ref[...]
        row = lax.broadcasted_iota(jnp.int32, (tq, tq), 0)
        col = lax.broadcasted_iota(jnp.int32, (tq, tq), 1)
        causal = col <= row
        hms = [_head_mask(hh, (tq, LANES)) for hh in range(2)]
        qhs = [jnp.where(hm, q, jnp.zeros_like(q)) for hm in hms]
        dohs = [jnp.where(hm, do_v, jnp.zeros_like(do_v)) for hm in hms]
        cbs = _block_constants(cr_ref, i, tq)
        lses = [lse_ref[hh] for hh in range(2)]
        prod = do_v.astype(F32) * o_ref[...]
        dls = [jnp.sum(jnp.where(hm, prod, 0.0), axis=1, keepdims=True) for hm in hms]

        def step(block, n_blocks, carry, masked):
            dq_acc, row_sums = carry[0], list(carry[1:])
            tk = n_blocks * tq
            start = pl.multiple_of(block * tq, tk)
            kj = k_ref[pl.ds(start, tk), :]
            vj = v_ref[pl.ds(start, tk), :]
            head_k = [_head_mask(hh, (tk, LANES)) for hh in range(2)]
            dv_part = jnp.zeros((tk, LANES), F32)
            dk_part = jnp.zeros((tk, LANES), F32)
            for hh in range(2):
                bias = cbs[hh] - cr_ref[0, hh:hh + 1, pl.ds(start, tk)]
                s = lax.dot_general(qhs[hh], kj, NT_DIMS, preferred_element_type=F32) + bias
                if masked:
                    s = jnp.where(causal, s, NEG_BIG)
                p = jnp.exp(s - lses[hh])
                dp = lax.dot_general(dohs[hh], vj, NT_DIMS, preferred_element_type=F32)
                ds = p * (dp - dls[hh])
                pb = p.astype(BF16)
                dsb = ds.astype(BF16)
                dv_part = dv_part + lax.dot_general(pb, dohs[hh], TN_DIMS, preferred_element_type=F32)
                dk_part = dk_part + lax.dot_general(dsb, qhs[hh], TN_DIMS, preferred_element_type=F32)
                dc_ref[0, hh:hh + 1, pl.ds(start, tk)] -= jnp.sum(ds, axis=0, keepdims=True)
                row_sums[hh] = row_sums[hh] + jnp.sum(ds, axis=1, keepdims=True)
                kh = jnp.where(head_k[hh], kj, jnp.zeros_like(kj))
                dq_acc = dq_acc + jnp.dot(dsb, kh, preferred_element_type=F32)
            dv_ref[pl.ds(start, tk), :] += dv_part
            dk_ref[pl.ds(start, tk), :] += dk_part
            return (dq_acc, *row_sums)

        zero_col = jnp.zeros((tq, 1), F32)
        n_wide = i // KV_STEP
        carry = lax.fori_loop(0, n_wide, lambda j, c: step(j * KV_STEP, KV_STEP, c, False),
                              (jnp.zeros((tq, LANES), F32), zero_col, zero_col))
        carry = lax.fori_loop(n_wide * KV_STEP, i, lambda j, c: step(j, 1, c, False), carry)
        carry = step(i, 1, carry, True)
        dq_ref[...] = (carry[0] * scale).astype(dq_ref.dtype)
        for hh in range(2):
            dcq_ref[0, hh:hh + 1, :] = jnp.transpose(jnp.broadcast_to(carry[1 + hh], (tq, LANES)))[0:1, :]

        @pl.when(i == nq - 1)
        def _():
            dk_out[...] = dk_ref[...].astype(dk_out.dtype)
            dv_out[...] = dv_ref[...].astype(dv_out.dtype)

    nblk = D_MODEL // LANES
    qblk = pl.BlockSpec((tq, LANES), lambda p, i: (i, p))
    stat = pl.BlockSpec((2, tq, 1), lambda p, i: (p, i, 0))
    full = pl.BlockSpec((S, LANES), lambda p, i: (0, p))
    return pl.pallas_call(
        body, name=name, grid=(HEAD_PAIRS, nq),
        in_specs=[
            pl.BlockSpec((tq, LANES), lambda p, i: (i, PB_Q * nblk + p)),
            pl.BlockSpec((S, LANES), lambda p, i: (0, PB_K * nblk + p)),
            pl.BlockSpec((S, LANES), lambda p, i: (0, PB_V * nblk + p)),
            qblk, qblk, stat,
            pl.BlockSpec((1, 2, S), lambda p, i: (p, 0, 0)),
        ],
        out_specs=[qblk, full, full, pl.BlockSpec((1, 2, S), lambda p, i: (p, 0, 0)),
                   pl.BlockSpec((1, 2, tq), lambda p, i: (p, 0, i))],
        out_shape=[
            jax.ShapeDtypeStruct((S, D_MODEL), BF16),
            jax.ShapeDtypeStruct((S, D_MODEL), BF16),
            jax.ShapeDtypeStruct((S, D_MODEL), BF16),
            jax.ShapeDtypeStruct((HEAD_PAIRS, 2, S), F32),
            jax.ShapeDtypeStruct((HEAD_PAIRS, 2, S), F32),
        ],
        scratch_shapes=[pltpu.VMEM((S, LANES), F32), pltpu.VMEM((S, LANES), F32)],
        compiler_params=_cparams(("parallel", "arbitrary"), 56),
    )(P, P, P, do, o32, lse, crow)


def _cumsum_lanes(xs, *, reverse, name):
    R, S = xs[0].shape
    nb = S // LANES
    n_in = len(xs)

    def body(*refs):
        x_refs, o_ref = refs[:n_in], refs[n_in]
        r = lax.broadcasted_iota(jnp.int32, (LANES, LANES), 0)
        c = lax.broadcasted_iota(jnp.int32, (LANES, LANES), 1)
        tri = ((r >= c) if reverse else (r <= c)).astype(F32)

        def step(b, carry):
            blk = (nb - 1 - b) if reverse else b
            start = pl.multiple_of(blk * LANES, LANES)
            xb = x_refs[0][:, pl.ds(start, LANES)]
            for r in x_refs[1:]:
                xb = xb + r[:, pl.ds(start, LANES)]
            y = jnp.dot(xb, tri, precision=lax.Precision.HIGHEST, preferred_element_type=F32) + carry
            o_ref[:, pl.ds(start, LANES)] = y
            return carry + jnp.sum(xb, axis=1, keepdims=True)

        lax.fori_loop(0, nb, step, jnp.zeros((R, 1), F32))

    return pl.pallas_call(
        body, name=name,
        in_specs=[pl.BlockSpec(memory_space=pltpu.VMEM)] * n_in,
        out_specs=pl.BlockSpec(memory_space=pltpu.VMEM),
        out_shape=jax.ShapeDtypeStruct((R, S), F32),
    )(*xs)


def _forget_fwd(f_logit, b_pad, *, name, tm=1024):
    S = f_logit.shape[0]
    tm = _tile(S, tm)

    def body(f_ref, b_ref, o_ref):
        z = f_ref[...] + b_ref[...]
        o_ref[...] = jnp.minimum(z, 0.0) - jnp.log(1.0 + jnp.exp(-jnp.abs(z)))

    blk = pl.BlockSpec((tm, LANES), lambda i: (i, 0))
    return pl.pallas_call(
        body, name=name, grid=(S // tm,),
        in_specs=[blk, pl.BlockSpec((1, LANES), lambda i: (0, 0))],
        out_specs=blk, out_shape=jax.ShapeDtypeStruct((S, LANES), F32),
        compiler_params=_cparams(("parallel",)),
    )(f_logit, b_pad)


def _forget_bwd(f_logit, b_pad, dlogf, *, name, tm=1024):
    S = f_logit.shape[0]
    tm = _tile(S, tm)

    def body(f_ref, b_ref, d_ref, o_ref, db_ref):
        i = pl.program_id(0)
        z = f_ref[...] + b_ref[...]
        dz = d_ref[...] * (1.0 - _sigmoid(z))
        o_ref[...] = dz.astype(o_ref.dtype)
        part = jnp.sum(dz, axis=0, keepdims=True)

        @pl.when(i == 0)
        def _():
            db_ref[...] = part

        @pl.when(i > 0)
        def _():
            db_ref[...] += part

    blk = pl.BlockSpec((tm, LANES), lambda i: (i, 0))
    vec = pl.BlockSpec((1, LANES), lambda i: (0, 0))
    return pl.pallas_call(
        body, name=name, grid=(S // tm,),
        in_specs=[blk, vec, blk], out_specs=[blk, vec],
        out_shape=[jax.ShapeDtypeStruct((S, LANES), BF16), jax.ShapeDtypeStruct((1, LANES), F32)],
        compiler_params=_cparams(("arbitrary",)),
    )(f_logit, b_pad, dlogf)


def _layernorm_stats(y):
    mu = jnp.mean(y, axis=-1, keepdims=True)
    yc = y - mu
    rstd = lax.rsqrt(jnp.mean(yc * yc, axis=-1, keepdims=True) + LN_EPS)
    return yc * rstd, rstd


def _fill_shifted(src_ref, sh_ref, tm):
    rows = tm + CONV_HALO - SUBLANES
    for s in range(1, SUBLANES):
        sh_ref[s - 1, 0:rows, :] = src_ref[pl.ds(s, rows), :]


def _shifted_rows(src_ref, sh_ref, r0, offset):
    s, base = offset % SUBLANES, offset - offset % SUBLANES
    rows = pl.ds(pl.multiple_of(r0 + base, SUBLANES), CONV_CHUNK)
    return src_ref[rows, :] if s == 0 else sh_ref[s - 1, rows, :]


def _fill_tap_rows(w_ref, wb_ref):
    for k in range(CONV_WIDTH):
        for j in range(N_DEV):
            wb_ref[k * SUBLANES:(k + 1) * SUBLANES, j * LANES:(j + 1) * LANES] = jnp.broadcast_to(
                w_ref[j, k:k + 1, :], (SUBLANES, LANES))


def _tap_sum(wb_ref, rows_of_tap):
    n = CONV_CHUNK // SUBLANES
    accs = [None] * n
    for k in range(CONV_WIDTH):
        wk = wb_ref[k * SUBLANES:(k + 1) * SUBLANES, :]
        src = rows_of_tap(k)
        for h in range(n):
            term = wk * src[h * SUBLANES:(h + 1) * SUBLANES]
            accs[h] = term if accs[h] is None else accs[h] + term
    return jnp.concatenate(accs, axis=0)


def _conv_fwd(P, w_pad, conv_b, ln_g, ln_b, *, name, tm=256):
    S = P.shape[0]
    C = D_MODEL
    tm = _tile(S, tm)
    hb = tm // CONV_HALO

    def body(a_ref, gt_ref, ah_ref, gh_ref, w_ref, cb_ref, g_ref, b_ref, o_ref, y_ref, glu_ref, sh_ref, wb_ref):
        i = pl.program_id(0)
        halo = ah_ref[...].astype(F32) * _sigmoid(gh_ref[...].astype(F32))
        glu_ref[0:CONV_HALO, :] = jnp.where(i > 0, halo, 0.0)
        glu_ref[CONV_HALO:, :] = a_ref[...].astype(F32) * _sigmoid(gt_ref[...].astype(F32))
        _fill_shifted(glu_ref, sh_ref, tm)
        _fill_tap_rows(w_ref, wb_ref)

        def chunk(r, carry):
            r0 = pl.multiple_of(r * CONV_CHUNK, CONV_CHUNK)
            acc = _tap_sum(wb_ref, lambda k: _shifted_rows(glu_ref, sh_ref, r0, CONV_HALO - (CONV_WIDTH - 1) + k))
            y_ref[pl.ds(r0, CONV_CHUNK), :] = acc + cb_ref[...]
            return carry

        lax.fori_loop(0, tm // CONV_CHUNK, chunk, 0)
        xhat, _ = _layernorm_stats(y_ref[...])
        z = xhat * g_ref[...] + b_ref[...]
        o_ref[...] = (z * _sigmoid(z)).astype(o_ref.dtype)

    vec = pl.BlockSpec((1, C), lambda i: (0, 0))
    row = pl.BlockSpec((tm, C), lambda i: (i, 0))
    return pl.pallas_call(
        body, name=name, grid=(S // tm,),
        in_specs=[
            pl.BlockSpec((tm, C), lambda i: (i, PB_A)),
            pl.BlockSpec((tm, C), lambda i: (i, PB_GATE)),
            pl.BlockSpec((CONV_HALO, C), lambda i: (jnp.maximum(i * hb - 1, 0), PB_A)),
            pl.BlockSpec((CONV_HALO, C), lambda i: (jnp.maximum(i * hb - 1, 0), PB_GATE)),
            pl.BlockSpec((N_DEV, CONV_WIDTH, LANES), lambda i: (0, 0, 0)),
            vec, vec, vec,
        ],
        out_specs=[row, row],
        out_shape=[jax.ShapeDtypeStruct((S, C), BF16), jax.ShapeDtypeStruct((S, C), F32)],
        scratch_shapes=[pltpu.VMEM((tm + CONV_HALO, C), F32), pltpu.VMEM((SUBLANES - 1, tm + CONV_HALO, C), F32),
                        pltpu.VMEM((CONV_HALO * SUBLANES, C), F32)],
        compiler_params=_cparams(("parallel",), 56),
    )(P, P, P, P, w_pad, conv_b, ln_g, ln_b)


def _conv_bwd(P, do, y, w_pad, ln_g, ln_b, *, name, tm=256):
    S = P.shape[0]
    C = D_MODEL
    tm = _tile(S, tm)
    hb = tm // CONV_HALO
    n_tiles = S // tm
    last_halo = S // CONV_HALO - 1

    def body(a_ref, gt_ref, ah_ref, gh_ref, do_ref, y_ref, don_ref, yn_ref, w_ref, g_ref, b_ref,
             dglu_ref, dw_ref, small_ref, glu_ref, dy_ref, gsh_ref, dsh_ref, dwacc_ref, wb_ref):
        i = pl.program_id(0)

        @pl.when(i == 0)
        def _():
            dwacc_ref[...] = jnp.zeros_like(dwacc_ref)
            small_ref[...] = jnp.zeros_like(small_ref)

        def ln_bwd(do_v, y_v):
            xhat, rstd = _layernorm_stats(y_v)
            z = xhat * g_ref[...] + b_ref[...]
            sg = _sigmoid(z)
            dz = do_v * (sg * (1.0 + z * (1.0 - sg)))
            dxh = dz * g_ref[...]
            dy = rstd * (dxh - jnp.mean(dxh, axis=-1, keepdims=True) - xhat * jnp.mean(dxh * xhat, axis=-1, keepdims=True))
            return dy, dz, xhat

        dy, dz, xhat = ln_bwd(do_ref[...], y_ref[...])
        dy_next, _, _ = ln_bwd(don_ref[...], yn_ref[...])
        small_ref[0:1, :] += jnp.sum(dz * xhat, axis=0, keepdims=True)
        small_ref[1:2, :] += jnp.sum(dz, axis=0, keepdims=True)
        small_ref[2:3, :] += jnp.sum(dy, axis=0, keepdims=True)
        dy_ref[0:tm, :] = dy
        dy_ref[tm:, :] = jnp.where(i < n_tiles - 1, dy_next, 0.0)

        halo = ah_ref[...].astype(F32) * _sigmoid(gh_ref[...].astype(F32))
        glu_ref[0:CONV_HALO, :] = jnp.where(i > 0, halo, 0.0)
        glu_ref[CONV_HALO:, :] = a_ref[...].astype(F32) * _sigmoid(gt_ref[...].astype(F32))
        _fill_shifted(glu_ref, gsh_ref, tm)
        _fill_shifted(dy_ref, dsh_ref, tm)
        _fill_tap_rows(w_ref, wb_ref)

        def chunk(r, carry):
            r0 = pl.multiple_of(r * CONV_CHUNK, CONV_CHUNK)
            rows = pl.ds(r0, CONV_CHUNK)
            dyc = dy_ref[rows, :]
            for k in range(CONV_WIDTH):
                prod = dyc * _shifted_rows(glu_ref, gsh_ref, r0, CONV_HALO - (CONV_WIDTH - 1) + k)
                dwacc_ref[k * SUBLANES:(k + 1) * SUBLANES, :] += prod[0:SUBLANES] + prod[SUBLANES:]
            dg = _tap_sum(wb_ref, lambda k: _shifted_rows(dy_ref, dsh_ref, r0, CONV_WIDTH - 1 - k))
            a = a_ref[rows, :].astype(F32)
            sig = _sigmoid(gt_ref[rows, :].astype(F32))
            dglu_ref[rows, 0:C] = (dg * sig).astype(dglu_ref.dtype)
            dglu_ref[rows, C:] = (dg * a * sig * (1.0 - sig)).astype(dglu_ref.dtype)
            return carry

        lax.fori_loop(0, tm // CONV_CHUNK, chunk, 0)

        @pl.when(i == n_tiles - 1)
        def _():
            dw_ref[...] = jnp.zeros_like(dw_ref)
            for k in range(CONV_WIDTH):
                row_k = jnp.sum(dwacc_ref[k * SUBLANES:(k + 1) * SUBLANES, :], axis=0, keepdims=True)
                for j in range(N_DEV):
                    dw_ref[j, k:k + 1, :] = row_k[:, j * LANES:(j + 1) * LANES]

    vec = pl.BlockSpec((1, C), lambda i: (0, 0))
    row = pl.BlockSpec((tm, C), lambda i: (i, 0))
    nxt = pl.BlockSpec((CONV_HALO, C), lambda i: (jnp.minimum((i + 1) * hb, last_halo), 0))
    return pl.pallas_call(
        body, name=name, grid=(n_tiles,),
        in_specs=[
            pl.BlockSpec((tm, C), lambda i: (i, PB_A)),
            pl.BlockSpec((tm, C), lambda i: (i, PB_GATE)),
            pl.BlockSpec((CONV_HALO, C), lambda i: (jnp.maximum(i * hb - 1, 0), PB_A)),
            pl.BlockSpec((CONV_HALO, C), lambda i: (jnp.maximum(i * hb - 1, 0), PB_GATE)),
            row, row, nxt, nxt,
            pl.BlockSpec((N_DEV, CONV_WIDTH, LANES), lambda i: (0, 0, 0)),
            vec, vec,
        ],
        out_specs=[
            pl.BlockSpec((tm, 2 * C), lambda i: (i, 0)),
            pl.BlockSpec((N_DEV, CONV_HALO, LANES), lambda i: (0, 0, 0)),
            pl.BlockSpec((8, C), lambda i: (0, 0)),
        ],
        out_shape=[
            jax.ShapeDtypeStruct((S, 2 * C), BF16),
            jax.ShapeDtypeStruct((N_DEV, CONV_HALO, LANES), F32),
            jax.ShapeDtypeStruct((8, C), F32),
        ],
        scratch_shapes=[
            pltpu.VMEM((tm + CONV_HALO, C), F32), pltpu.VMEM((tm + CONV_HALO, C), F32),
            pltpu.VMEM((SUBLANES - 1, tm + CONV_HALO, C), F32), pltpu.VMEM((SUBLANES - 1, tm + CONV_HALO, C), F32),
            pltpu.VMEM((CONV_HALO * SUBLANES, C), F32), pltpu.VMEM((CONV_HALO * SUBLANES, C), F32),
        ],
        compiler_params=_cparams(("arbitrary",), 56),
    )(P, P, P, P, do, y, do, y, w_pad, ln_g, ln_b)


def _mem_softmax(qh, kh):
    s = lax.dot_general(qh, kh, NT_DIMS, preferred_element_type=F32)
    e = jnp.exp(s - jnp.max(s, axis=1, keepdims=True))
    return e / jnp.sum(e, axis=1, keepdims=True)


def _mem_fwd(P, kv, *, name, tm=512):
    S, M = P.shape[0], kv.shape[0]
    tm = _tile(S, tm)
    scale = MEM_HEAD_DIM ** -0.5

    def body(q_ref, k_ref, v_ref, o_ref):
        for h in range(MEM_HEADS):
            sl = slice(h * MEM_HEAD_DIM, (h + 1) * MEM_HEAD_DIM)
            qh = q_ref[:, sl] * jnp.asarray(scale, BF16)
            p = _mem_softmax(qh, k_ref[:, sl])
            o_ref[:, sl] = jnp.dot(p.astype(BF16), v_ref[:, sl], preferred_element_type=F32).astype(o_ref.dtype)

    return pl.pallas_call(
        body, name=name, grid=(S // tm,),
        in_specs=[
            pl.BlockSpec((tm, D_MODEL), lambda i: (i, PB_QMEM)),
            pl.BlockSpec((M, D_MODEL), lambda i: (0, 0)),
            pl.BlockSpec((M, D_MODEL), lambda i: (0, 1)),
        ],
        out_specs=pl.BlockSpec((tm, D_MODEL), lambda i: (i, 0)),
        out_shape=jax.ShapeDtypeStruct((S, D_MODEL), BF16),
        compiler_params=_cparams(("parallel",)),
    )(P, kv, kv)


def _mem_bwd(P, do, kv, *, name, tm=512):
    S, M = P.shape[0], kv.shape[0]
    tm = _tile(S, tm)
    scale = MEM_HEAD_DIM ** -0.5

    def body(q_ref, k_ref, v_ref, do_ref, dq_ref, dkv_ref):
        i = pl.program_id(0)

        @pl.when(i == 0)
        def _():
            dkv_ref[...] = jnp.zeros_like(dkv_ref)

        for h in range(MEM_HEADS):
            sl = slice(h * MEM_HEAD_DIM, (h + 1) * MEM_HEAD_DIM)
            slv = slice(D_MODEL + h * MEM_HEAD_DIM, D_MODEL + (h + 1) * MEM_HEAD_DIM)
            qh = q_ref[:, sl] * jnp.asarray(scale, BF16)
            kh, vh, doh = k_ref[:, sl], v_ref[:, sl], do_ref[:, sl]
            p = _mem_softmax(qh, kh)
            dp = lax.dot_general(doh, vh, NT_DIMS, preferred_element_type=F32)
            ds = p * (dp - jnp.sum(p * dp, axis=1, keepdims=True))
            dsb = ds.astype(BF16)
            dq_ref[:, sl] = (jnp.dot(dsb, kh, preferred_element_type=F32) * scale).astype(dq_ref.dtype)
            dkv_ref[:, sl] += lax.dot_general(dsb, qh, TN_DIMS, preferred_element_type=F32)
            dkv_ref[:, slv] += lax.dot_general(p.astype(BF16), doh, TN_DIMS, preferred_element_type=F32)

    row = pl.BlockSpec((tm, D_MODEL), lambda i: (i, 0))
    return pl.pallas_call(
        body, name=name, grid=(S // tm,),
        in_specs=[
            pl.BlockSpec((tm, D_MODEL), lambda i: (i, PB_QMEM)),
            pl.BlockSpec((M, D_MODEL), lambda i: (0, 0)),
            pl.BlockSpec((M, D_MODEL), lambda i: (0, 1)),
            row,
        ],
        out_specs=[row, pl.BlockSpec((M, 2 * D_MODEL), lambda i: (0, 0))],
        out_shape=[jax.ShapeDtypeStruct((S, D_MODEL), BF16), jax.ShapeDtypeStruct((M, 2 * D_MODEL), F32)],
        compiler_params=_cparams(("arbitrary",)),
    )(P, kv, kv, do)


def _merge_fwd(o_attn, o_conv, o_mem, P, wpa, wpc, wpm, w_out, x, g_post, g_pre, *, name, tm=256):
    S = P.shape[0]
    D = D_MODEL
    tm = _tile(S, tm)

    def body(oa_ref, oc_ref, om_ref, g0_ref, g1_ref, g2_ref, wa_ref, wc_ref, wm_ref, wo_ref, x_ref, gp_ref, gq_ref,
             mg_ref, y_ref, pb_ref, x1_ref, h2_ref):
        merged = jnp.zeros((tm, D), F32)
        for b, (o_ref, g_ref, w_ref) in enumerate(((oa_ref, g0_ref, wa_ref), (oc_ref, g1_ref, wc_ref), (om_ref, g2_ref, wm_ref))):
            pb = jnp.dot(o_ref[...], w_ref[...], preferred_element_type=F32)
            pb_ref[b] = pb.astype(pb_ref.dtype)
            merged = merged + _sigmoid(g_ref[...].astype(F32)) * pb
        mb = merged.astype(BF16)
        mg_ref[...] = mb
        yv = jnp.dot(mb, wo_ref[...], preferred_element_type=F32)
        y_ref[...] = yv
        x1 = x_ref[...] + yv * lax.rsqrt(jnp.mean(yv * yv, axis=-1, keepdims=True) + RMS_EPS) * gp_ref[...]
        x1_ref[...] = x1
        h2_ref[...] = (x1 * lax.rsqrt(jnp.mean(x1 * x1, axis=-1, keepdims=True) + RMS_EPS) * gq_ref[...]).astype(h2_ref.dtype)

    row = pl.BlockSpec((tm, D), lambda i: (i, 0))
    vec = pl.BlockSpec((1, D), lambda i: (0, 0))
    wsp = pl.BlockSpec((D, D), lambda i: (0, 0))
    return pl.pallas_call(
        body, name=name, grid=(S // tm,),
        in_specs=[row, row, row] + [pl.BlockSpec((tm, D), lambda i, b=b: (i, PB_G0 + b)) for b in range(3)] + [wsp] * 4
        + [row, vec, vec],
        out_specs=[row, row, pl.BlockSpec((3, tm, D), lambda i: (0, i, 0)), row, row],
        out_shape=[jax.ShapeDtypeStruct((S, D), BF16), jax.ShapeDtypeStruct((S, D), F32), jax.ShapeDtypeStruct((3, S, D), BF16),
                   jax.ShapeDtypeStruct((S, D), F32), jax.ShapeDtypeStruct((S, D), BF16)],
        compiler_params=_cparams(("parallel",), 56),
    )(o_attn, o_conv, o_mem, P, P, P, wpa, wpc, wpm, w_out, x, g_post, g_pre)


def _merge_bwd(dy1, w_out, P, pb, *, name, tm=256):
    S = P.shape[0]
    D = D_MODEL
    tm = _tile(S, tm)

    def body(dy_ref, wo_ref, g0_ref, g1_ref, g2_ref, pb_ref, d0_ref, d1_ref, d2_ref, dgl_ref):
        dm = lax.dot_general(dy_ref[...], wo_ref[...], NT_DIMS, preferred_element_type=F32)
        for b, (g_ref, d_ref) in enumerate(((g0_ref, d0_ref), (g1_ref, d1_ref), (g2_ref, d2_ref))):
            g = _sigmoid(g_ref[...].astype(F32))
            d_ref[...] = (dm * g).astype(d_ref.dtype)
            dgl_ref[:, b * D:(b + 1) * D] = (dm * pb_ref[b].astype(F32) * g * (1.0 - g)).astype(dgl_ref.dtype)

    row = pl.BlockSpec((tm, D), lambda i: (i, 0))
    blk3 = pl.BlockSpec((3, tm, D), lambda i: (0, i, 0))
    sds = jax.ShapeDtypeStruct((S, D), BF16)
    return pl.pallas_call(
        body, name=name, grid=(S // tm,),
        in_specs=[row, pl.BlockSpec((D, D), lambda i: (0, 0))]
        + [pl.BlockSpec((tm, D), lambda i, b=b: (i, PB_G0 + b)) for b in range(3)] + [blk3],
        out_specs=[row, row, row, pl.BlockSpec((tm, 3 * D), lambda i: (i, 0))],
        out_shape=[sds, sds, sds, jax.ShapeDtypeStruct((S, 3 * D), BF16)],
        compiler_params=_cparams(("parallel",), 56),
    )(dy1, w_out, P, P, P, pb)


def _swiglu_fwd(h2, w_gu_t, *, name, tm=512, tn=1408):
    S, D = h2.shape
    Fh = w_gu_t.shape[0] // 2
    tm, tn = _tile(S, tm), _tile(Fh, tn)
    nj = Fh // tn

    def body(h_ref, wg_ref, wu_ref, g_ref, u_ref, a_ref):
        hv = h_ref[...]
        g = lax.dot_general(hv, wg_ref[...], NT_DIMS, preferred_element_type=F32)
        u = lax.dot_general(hv, wu_ref[...], NT_DIMS, preferred_element_type=F32)
        g_ref[...] = g.astype(g_ref.dtype)
        u_ref[...] = u.astype(u_ref.dtype)
        a_ref[...] = (g * _sigmoid(g) * u).astype(a_ref.dtype)

    out = pl.BlockSpec((tm, tn), lambda j, i: (i, j))
    sds = jax.ShapeDtypeStruct((S, Fh), BF16)
    return pl.pallas_call(
        body, name=name, grid=(nj, S // tm),
        in_specs=[
            pl.BlockSpec((tm, D), lambda j, i: (i, 0)),
            pl.BlockSpec((tn, D), lambda j, i: (j, 0)),
            pl.BlockSpec((tn, D), lambda j, i: (j + nj, 0)),
        ],
        out_specs=[out, out, out], out_shape=[sds, sds, sds],
        compiler_params=_cparams(("parallel", "parallel"), 56),
    )(h2, w_gu_t, w_gu_t)


def _swiglu_bwd(dffn, w_down, g, u, *, name, tm=512, tn=1408):
    S, Fh = g.shape
    D = dffn.shape[1]
    tm, tn = _tile(S, tm), _tile(Fh, tn)

    def body(df_ref, w_ref, g_ref, u_ref, o_ref):
        dfv = df_ref[...]
        for j in range(Fh // tn):
            cols = slice(j * tn, (j + 1) * tn)
            da = lax.dot_general(dfv, w_ref[cols, :], NT_DIMS, preferred_element_type=F32)
            gv = g_ref[:, cols].astype(F32)
            sg = _sigmoid(gv)
            o_ref[:, cols] = (da * u_ref[:, cols].astype(F32) * (sg * (1.0 + gv * (1.0 - sg)))).astype(o_ref.dtype)
            o_ref[:, Fh + j * tn:Fh + (j + 1) * tn] = (da * gv * sg).astype(o_ref.dtype)

    act = pl.BlockSpec((tm, Fh), lambda i: (i, 0))
    return pl.pallas_call(
        body, name=name, grid=(S // tm,),
        in_specs=[pl.BlockSpec((tm, D), lambda i: (i, 0)), pl.BlockSpec((Fh, D), lambda i: (0, 0)), act, act],
        out_specs=pl.BlockSpec((tm, 2 * Fh), lambda i: (i, 0)),
        out_shape=jax.ShapeDtypeStruct((S, 2 * Fh), BF16),
        compiler_params=_cparams(("parallel",), 56),
    )(dffn, w_down, g, u)


def _final(x1, ffn, target, g, *, name, tm=512):
    S, D = x1.shape
    tm = _tile(S, tm)

    def body(x_ref, f_ref, t_ref, g_ref, dout_ref, dffn_ref, loss_ref, dg_ref):
        i = pl.program_id(0)
        fv = f_ref[...]
        rstd = lax.rsqrt(jnp.mean(fv * fv, axis=-1, keepdims=True) + RMS_EPS)
        r = fv * rstd
        e = x_ref[...] + r * g_ref[...] - t_ref[...]
        dout = e * (1.0 / D)
        dout_ref[...] = dout
        gy = dout * g_ref[...]
        dffn_ref[...] = (rstd * (gy - r * jnp.mean(gy * r, axis=-1, keepdims=True))).astype(dffn_ref.dtype)
        lpart = jnp.full((8, LANES), 0.5 * jnp.sum(jnp.mean(e * e, axis=-1, keepdims=True)), F32)
        gpart = jnp.sum(dout * r, axis=0, keepdims=True)

        @pl.when(i == 0)
        def _():
            loss_ref[...] = lpart
            dg_ref[...] = gpart

        @pl.when(i > 0)
        def _():
            loss_ref[...] += lpart
            dg_ref[...] += gpart

    row = pl.BlockSpec((tm, D), lambda i: (i, 0))
    vec = pl.BlockSpec((1, D), lambda i: (0, 0))
    return pl.pallas_call(
        body, name=name, grid=(S // tm,),
        in_specs=[row, row, row, vec],
        out_specs=[row, row, pl.BlockSpec((8, LANES), lambda i: (0, 0)), vec],
        out_shape=[jax.ShapeDtypeStruct((S, D), F32), jax.ShapeDtypeStruct((S, D), BF16),
                   jax.ShapeDtypeStruct((8, LANES), F32), jax.ShapeDtypeStruct((1, D), F32)],
        compiler_params=_cparams(("arbitrary",)),
    )(x1, ffn, target, g)


def _after(token, value):
    return value if token is None else value + token[0, 0].astype(value.dtype)


def _local_step(x, mem, target, gains, w, late_weights, send_grads, *, tq=512):
    S = x.shape[0]
    b_pad = jnp.pad(gains["b_forget"], ((0, 0), (0, LANES - FOX_HEADS)))
    gains = dict(gains)

    h = _rms_fwd(x, gains["norm_mix_pre"], name="rms_mix_pre")
    P = _matmul(h, w["wp_t"], mode="nt", out_dtype=BF16, name="proj_in", tm=2048)
    f_logit = _matmul(h, w["wf_t"], mode="nt", out_dtype=F32, name="proj_forget")
    logf = _forget_fwd(f_logit, b_pad, name="forget_fwd")
    c_row16 = _cumsum_lanes([logf[:, :FOX_HEADS].T], reverse=False, name="forget_cumsum")
    crow = c_row16.reshape(HEAD_PAIRS, 2, S)
    o_attn, o_attn32, lse = _attn_fwd(P, crow, tq=tq, name="attn_fwd")
    w = dict(w, **late_weights(o_attn))
    o_conv, y_conv = _conv_fwd(P, w["conv_w"], gains["conv_b"], gains["conv_ln_g"], gains["conv_ln_b"], name="conv_fwd")
    mem_n = _rms_fwd(mem, gains["norm_mem"], name="rms_mem")
    kv = _matmul(mem_n, w["w_kv"], mode="nn", out_dtype=BF16, name="mem_kv")
    o_mem = _mem_fwd(P, kv, name="mem_fwd")
    merged, y1, pb, x1, h2 = _merge_fwd(o_attn, o_conv, o_mem, P, w["wpa"], w["wpc"], w["wpm"], w["w_out"], x,
                                        gains["norm_mix_post"], gains["norm_ffn_pre"], name="merge_fwd")
    g_ffn, u_ffn, act = _swiglu_fwd(h2, w["w_gu_t"], name="swiglu_fwd")
    ffn = _matmul(act, w["w_down"], mode="nn", out_dtype=F32, name="ffn_down", tk=1408)

    dout, dffn, loss_tile, d_norm_ffn_post = _final(x1, ffn, target, gains["norm_ffn_post"], name="loss_head")
    d_w_down = _matmul(act, dffn, mode="tn", out_dtype=BF16, name="dw_down", tm=1408)
    dgu = _swiglu_bwd(dffn, w["w_down"], g_ffn, u_ffn, name="swiglu_bwd")
    dh2 = _matmul(dgu, w["w_gu_t"], mode="nn", out_dtype=F32, name="d_h2", tk=1408)
    d_w_gu_t = _matmul(dgu, h2, mode="tn", out_dtype=BF16, name="dw_gate_up", tm=1408)
    dx1, d_norm_ffn_pre = _rms_bwd(x1, [dh2], gains["norm_ffn_pre"], dout, out_dtype=F32, name="rms_ffn_pre_bwd")

    dy1, d_norm_mix_post = _rms_bwd(y1, [dx1], gains["norm_mix_post"], None, out_dtype=BF16, name="rms_mix_post_bwd")
    d_w_out = _matmul(merged, dy1, mode="tn", out_dtype=BF16, name="dw_out")
    *dpb, dgl = _merge_bwd(dy1, w["w_out"], P, pb, name="merge_bwd")
    do_attn = _matmul(dpb[0], w["wpa"], mode="nt", out_dtype=BF16, name="d_o_attn")
    do_conv = _matmul(dpb[1], w["wpc"], mode="nt", out_dtype=F32, name="d_o_conv")
    do_mem = _matmul(dpb[2], w["wpm"], mode="nt", out_dtype=BF16, name="d_o_mem")
    d_wpa = _matmul(o_attn, dpb[0], mode="tn", out_dtype=BF16, name="dw_proj_attn")
    d_wpc = _matmul(o_conv, dpb[1], mode="tn", out_dtype=BF16, name="dw_proj_conv")
    d_wpm = _matmul(o_mem, dpb[2], mode="tn", out_dtype=BF16, name="dw_proj_mem")

    dq_mem, dkv = _mem_bwd(P, do_mem, kv, name="mem_bwd")
    dkv_b = dkv.astype(BF16)
    d_w_kv = _matmul(mem_n, dkv_b, mode="tn", out_dtype=BF16, name="dw_kv")
    dmem_n = _matmul(dkv_b, w["w_kv"], mode="nt", out_dtype=F32, name="d_mem_n")
    _, d_norm_mem = _rms_bwd(mem, [dmem_n], gains["norm_mem"], None, out_dtype=BF16, name="rms_mem_bwd")

    dglu, d_conv_w, conv_small = _conv_bwd(P, do_conv, y_conv, w["conv_w"], gains["conv_ln_g"], gains["conv_ln_b"], name="conv_bwd")
    sent = send_grads("body", dict(conv_w=d_conv_w[:, :CONV_WIDTH].astype(BF16), w_kv_mem=d_w_kv, w_proj_attn=d_wpa,
                                   w_proj_conv=d_wpc, w_proj_mem=d_wpm, w_out=d_w_out, w_gate_up=d_w_gu_t, w_down=d_w_down))

    dq, dk, dv, dc, dcq = _attn_bwd(P, do_attn, o_attn32, lse, _after(sent, crow), tq=tq, name="attn_bwd")
    dlogf16 = _cumsum_lanes([dc.reshape(FOX_HEADS, S), dcq.reshape(FOX_HEADS, S)], reverse=True, name="forget_cumsum_bwd")
    dlogf = jnp.pad(dlogf16.T, ((0, 0), (0, LANES - FOX_HEADS)))
    df, d_b_pad = _forget_bwd(f_logit, b_pad, dlogf, name="forget_bwd")

    d_proj = [dq, dk, dv, dglu, dq_mem, dgl]
    d_wp_t = _matmul_pieces(d_proj, h, mode="tn", out_dtype=BF16, name="dw_in")
    d_wf_t = _matmul(df, h, mode="tn", out_dtype=BF16, name="dw_forget")
    n_qkv = 3 * D_MODEL
    pieces = [(d_wp_t[:n_qkv], n_qkv), (d_wf_t[:FOX_HEADS], FOX_HEADS), (d_wp_t[n_qkv:], P_WIDTH - n_qkv)]
    n_own = (P_WIDTH + FOX_HEADS) // N_DEV
    blocks = [jnp.concatenate(_rows(pieces, j * n_own, (j + 1) * n_own), axis=0) for j in range(N_DEV)]
    sent = send_grads("w_in", dict(w_in=jnp.stack(blocks)))
    dh_p = _matmul_pieces(d_proj, w["wp_t"], mode="nn", out_dtype=F32, name="d_h", after=sent)
    dh_f = _matmul(df, w["wf_t"], mode="nn", out_dtype=F32, name="d_h_forget")
    grad_x, d_norm_mix_pre = _rms_bwd(x, [dh_p, dh_f], gains["norm_mix_pre"], dx1, out_dtype=F32, name="rms_mix_pre_bwd")

    small = dict(norm_mix_pre=d_norm_mix_pre, norm_mix_post=d_norm_mix_post, norm_mem=d_norm_mem,
                 conv_b=conv_small[2:3], conv_ln_g=conv_small[0:1], conv_ln_b=conv_small[1:2],
                 norm_ffn_pre=d_norm_ffn_pre, norm_ffn_post=d_norm_ffn_post, b_forget=d_b_pad[:, :FOX_HEADS])
    return loss_tile, grad_x, small


def _mesh_pos():
    return lax.axis_index("x"), lax.axis_index("y"), lax.axis_index("c")


def _flip(pos, d):
    x, y, c = pos
    return (1 - x if d & 4 else x, 1 - y if d & 2 else y, 1 - c if d & 1 else c)


def _flat(pos):
    x, y, c = pos
    return 4 * x + 2 * y + c


def _gather_two_level(src, *, name):
    def body(src_ref, out_ref, token, send_sems, recv_sems, local_sem):
        me = _mesh_pos()
        my = _flat(me)
        sibling = _flip(me, 1)
        far = [_flip(me, d) for d in (4, 2, 6)]

        def copy(k, block, to, own=False):
            return pltpu.make_async_remote_copy(
                src_ref=src_ref if own else out_ref.at[_flat(block)], dst_ref=out_ref.at[_flat(block)],
                send_sem=send_sems.at[k], recv_sem=recv_sems.at[k], device_id=to, device_id_type=pl.DeviceIdType.MESH)

        local = pltpu.make_async_copy(src_ref, out_ref.at[my], local_sem)
        local.start()
        first = [copy(0, me, sibling, own=True)] + [copy(1 + j, me, peer, own=True) for j, peer in enumerate(far)]
        for cp in first:
            cp.start()
        passed = [copy(4 + j, peer, sibling) for j, peer in enumerate(far)]
        for j, peer in enumerate(far):
            copy(1 + j, peer, me).wait_recv()
            passed[j].start()
        copy(0, sibling, me).wait_recv()
        for j, peer in enumerate(far):
            copy(4 + j, _flip(peer, 1), me).wait_recv()
        for cp in first + passed:
            cp.wait_send()
        local.wait()
        token[...] = jnp.zeros_like(token)

    n_copy = N_DEV - 1
    out, token = pl.pallas_call(
        body, name=name,
        in_specs=[pl.BlockSpec(memory_space=pl.ANY)],
        out_specs=[pl.BlockSpec(memory_space=pl.ANY), pl.BlockSpec(memory_space=pltpu.VMEM)],
        out_shape=[jax.ShapeDtypeStruct((N_DEV,) + tuple(src.shape), src.dtype), jax.ShapeDtypeStruct((8, LANES), F32)],
        scratch_shapes=[pltpu.SemaphoreType.DMA((n_copy,)), pltpu.SemaphoreType.DMA((n_copy,)), pltpu.SemaphoreType.DMA(())],
    )(src)
    return [out], token


def _exchange(srcs, *, scatter, name):
    T = len(srcs)
    n_peer = N_DEV - 1

    def body(*refs):
        src_refs, out_refs = refs[:T], refs[T:2 * T]
        token, send_sems, recv_sems, local_sems = refs[2 * T:]
        me = _mesh_pos()
        my = _flat(me)

        def copy(t, d):
            peer = _flip(me, d)
            return pltpu.make_async_remote_copy(
                src_ref=src_refs[t].at[_flat(peer)] if scatter else src_refs[t],
                dst_ref=out_refs[t].at[my],
                send_sem=send_sems.at[t * n_peer + d - 1], recv_sem=recv_sems.at[t * n_peer + d - 1],
                device_id=peer, device_id_type=pl.DeviceIdType.MESH)

        def arrival(t, d):
            peer = _flip(me, d)
            return pltpu.make_async_remote_copy(
                src_ref=src_refs[t].at[my] if scatter else src_refs[t],
                dst_ref=out_refs[t].at[_flat(peer)],
                send_sem=send_sems.at[t * n_peer + d - 1], recv_sem=recv_sems.at[t * n_peer + d - 1],
                device_id=peer, device_id_type=pl.DeviceIdType.MESH)

        local = [pltpu.make_async_copy(src_refs[t].at[my] if scatter else src_refs[t], out_refs[t].at[my], local_sems.at[t])
                 for t in range(T)]
        for cp in local:
            cp.start()
        sends = [copy(t, d) for d in range(1, N_DEV) for t in range(T)]
        for cp in sends:
            cp.start()
        for d in range(1, N_DEV):
            for t in range(T):
                arrival(t, d).wait_recv()
        for cp in sends:
            cp.wait_send()
        for cp in local:
            cp.wait()
        token[...] = jnp.zeros_like(token)

    outs = pl.pallas_call(
        body, name=name,
        in_specs=[pl.BlockSpec(memory_space=pl.ANY)] * T,
        out_specs=[pl.BlockSpec(memory_space=pl.ANY)] * T + [pl.BlockSpec(memory_space=pltpu.VMEM)],
        out_shape=[jax.ShapeDtypeStruct((N_DEV,) + tuple(s.shape[-2:]), s.dtype) for s in srcs] + [jax.ShapeDtypeStruct((8, LANES), F32)],
        scratch_shapes=[pltpu.SemaphoreType.DMA((T * n_peer,)), pltpu.SemaphoreType.DMA((T * n_peer,)), pltpu.SemaphoreType.DMA((T,))],
    )(*srcs)
    return list(outs[:T]), outs[T]


HBM_SPEC = pl.BlockSpec(memory_space=pltpu.HBM)
SEM_SPEC = pl.BlockSpec(memory_space=pltpu.SEMAPHORE)


def _split_copies(src_refs, land_refs, send_sems, recv_sems, scatter):
    me = _mesh_pos()
    my = _flat(me)
    n_peer = N_DEV - 1
    out = []
    for d in range(1, N_DEV):
        peer = _flip(me, d)
        for t, (src, land) in enumerate(zip(src_refs, land_refs)):
            k = t * n_peer + d - 1
            start = pltpu.make_async_remote_copy(
                src_ref=src.at[_flat(peer)] if scatter else src, dst_ref=land.at[my],
                send_sem=send_sems.at[k], recv_sem=recv_sems.at[k], device_id=peer, device_id_type=pl.DeviceIdType.MESH)
            arrive = pltpu.make_async_remote_copy(
                src_ref=src.at[my] if scatter else src, dst_ref=land.at[_flat(peer)],
                send_sem=send_sems.at[k], recv_sem=recv_sems.at[k], device_id=peer, device_id_type=pl.DeviceIdType.MESH)
            out.append((start, arrive))
    return out


def _local_copies(src_refs, land_refs, local_sems, scatter):
    my = _flat(_mesh_pos())
    return [pltpu.make_async_copy(src.at[my] if scatter else src, land.at[my], local_sems.at[t])
            for t, (src, land) in enumerate(zip(src_refs, land_refs))]


def _exchange_start(srcs, *, scatter, name):
    T = len(srcs)
    n_sem = T * (N_DEV - 1)
    lands = [lax.empty((N_DEV,) + tuple(s.shape[-2:]), s.dtype) for s in srcs]

    def body(*refs):
        src_refs, land_refs = refs[:T], refs[T:2 * T]
        send_sems, recv_sems, local_sems = refs[2 * T:2 * T + 3]
        token = refs[-1]
        for cp in _local_copies(src_refs, land_refs, local_sems, scatter):
            cp.start()
        for start, _ in _split_copies(src_refs, land_refs, send_sems, recv_sems, scatter):
            start.start()
        token[...] = jnp.zeros_like(token)

    hbm = lambda a: pltpu.HBM(a.shape, a.dtype)
    outs = pl.pallas_call(
        body, name=name,
        in_specs=[HBM_SPEC] * (2 * T),
        out_specs=(SEM_SPEC, SEM_SPEC, SEM_SPEC, *[HBM_SPEC] * (2 * T), pl.BlockSpec(memory_space=pltpu.VMEM)),
        out_shape=(pltpu.SemaphoreType.DMA((n_sem,)), pltpu.SemaphoreType.DMA((n_sem,)), pltpu.SemaphoreType.DMA((T,)),
                   *[hbm(s) for s in srcs], *[hbm(a) for a in lands], jax.ShapeDtypeStruct((8, LANES), F32)),
        input_output_aliases={t: 3 + t for t in range(2 * T)},
        compiler_params=pltpu.CompilerParams(has_side_effects=pltpu.SideEffectType.DATAFLOW_SIDE_EFFECTING),
    )(*[pltpu.with_memory_space_constraint(s, pltpu.HBM) for s in srcs],
      *[pltpu.with_memory_space_constraint(a, pltpu.HBM) for a in lands])
    return (outs[0], outs[1], outs[2], list(outs[3:3 + T]), list(outs[3 + T:3 + 2 * T])), outs[-1]


def _exchange_wait(handle, after, *, scatter, name):
    send_sems, recv_sems, local_sems, srcs, lands = handle
    T = len(srcs)

    def body(*refs):
        src_refs, land_refs = refs[:T], refs[T:2 * T]
        s_sems, r_sems, l_sems = refs[2 * T:2 * T + 3]
        for cp in _local_copies(src_refs, land_refs, l_sems, scatter):
            cp.wait()
        for start, arrive in _split_copies(src_refs, land_refs, s_sems, r_sems, scatter):
            start.wait_send()
            arrive.wait_recv()

    hbm = lambda a: pltpu.HBM(a.shape, a.dtype)
    outs = pl.pallas_call(
        body, name=name,
        in_specs=[HBM_SPEC] * (2 * T) + [SEM_SPEC, SEM_SPEC, SEM_SPEC, pl.BlockSpec(memory_space=pl.ANY)],
        out_specs=[HBM_SPEC] * (2 * T),
        out_shape=[hbm(s) for s in srcs] + [hbm(a) for a in lands],
        input_output_aliases={t: t for t in range(2 * T)},
        compiler_params=pltpu.CompilerParams(has_side_effects=pltpu.SideEffectType.DATAFLOW_SIDE_EFFECTING),
    )(*srcs, *lands, send_sems, recv_sems, local_sems, after)
    return list(outs[T:])


def _adamw(w, slots, m, v, *, name):
    R, C = w.shape
    row_tiles = [t for t in range(16, 513, 16) if R % t == 0]
    tr = R if R <= 512 or not row_tiles else max(row_tiles)
    tc = C if tr < R or R <= 512 else LANES
    assert R % tr == 0 and C % tc == 0, (R, C, tr, tc)

    def body(w_ref, s_ref, m_ref, v_ref, g_ref, d_ref, m2_ref, v2_ref):
        gv = s_ref[0].astype(F32)
        for j in range(1, N_DEV):
            gv = gv + s_ref[j].astype(F32)
        g_ref[...] = gv
        m2 = ADAM_B1 * m_ref[...] + (1.0 - ADAM_B1) * gv
        v2 = ADAM_B2 * v_ref[...] + (1.0 - ADAM_B2) * (gv * gv)
        m_hat = m2 / (1.0 - ADAM_B1 ** ADAM_STEP)
        v_hat = v2 / (1.0 - ADAM_B2 ** ADAM_STEP)
        d_ref[...] = -ADAM_LR * (m_hat / (jnp.sqrt(v_hat) + ADAM_EPS) + ADAM_WD * w_ref[...])
        m2_ref[...] = m2
        v2_ref[...] = v2

    blk = pl.BlockSpec((tr, tc), lambda i, j: (i, j))
    sds = jax.ShapeDtypeStruct((R, C), F32)
    return pl.pallas_call(
        body, name=name, grid=(R // tr, C // tc),
        in_specs=[blk, pl.BlockSpec((N_DEV, tr, tc), lambda i, j: (0, i, j)), blk, blk],
        out_specs=[blk] * 4, out_shape=[sds] * 4,
        compiler_params=_cparams(("parallel", "parallel")),
    )(w, slots, m, v)


SHARDED = (
    ("w_in", (1154, 1024), "row"), ("conv_w", (31, 128), "col"), ("w_kv_mem", (1024, 256), "col"),
    ("w_proj_attn", (128, 1024), "row"), ("w_proj_conv", (128, 1024), "row"), ("w_proj_mem", (128, 1024), "row"),
    ("w_out", (128, 1024), "row"), ("w_gate_up", (704, 1024), "row"), ("w_down", (352, 1024), "row"),
)
TRANSPOSED = ("w_in", "w_gate_up")
SMALL = ("norm_mix_pre", "norm_mix_post", "norm_mem", "conv_b", "conv_ln_g", "conv_ln_b", "norm_ffn_pre", "norm_ffn_post", "b_forget")
SMALL_ROWS = 16
LOSS_ROW = len(SMALL)
WEIGHT_ORDER = ("norm_mix_pre", "norm_mix_post", "norm_mem", "w_in", "b_forget", "conv_w", "conv_b", "conv_ln_g", "conv_ln_b",
                "w_kv_mem", "w_proj_attn", "w_proj_conv", "w_proj_mem", "w_out", "norm_ffn_pre", "norm_ffn_post", "w_gate_up", "w_down")


def _rows(pieces, lo, hi):
    out, start = [], 0
    for a, n in pieces:
        a0, a1 = max(lo, start), min(hi, start + n)
        if a0 < a1:
            out.append(a[a0 - start:a1 - start])
        start += n
    return out


def _to_full(blocks8, kind):
    n, r, c = blocks8.shape
    if kind == "col":
        return jnp.concatenate([blocks8[j] for j in range(n)], axis=1)
    return blocks8.reshape(n * r, c)


def _to_blocks(full, kind):
    if kind == "col":
        c = full.shape[1] // N_DEV
        return jnp.stack([full[:, j * c:(j + 1) * c] for j in range(N_DEV)])
    nr, c = full.shape
    return full.reshape(N_DEV, nr // N_DEV, c)


def kernel(x, mem, norm_mix_pre, norm_mix_post, norm_mem, w_in, b_forget, conv_w, conv_b, conv_ln_g, conv_ln_b, w_kv_mem, w_proj_attn, w_proj_conv, w_proj_mem, w_out, norm_ffn_pre, norm_ffn_post, w_gate_up, w_down, loss_target, m_norm_mix_pre, m_norm_mix_post, m_norm_mem, m_w_in, m_b_forget, m_conv_w, m_conv_b, m_conv_ln_g, m_conv_ln_b, m_w_kv_mem, m_w_proj_attn, m_w_proj_conv, m_w_proj_mem, m_w_out, m_norm_ffn_pre, m_norm_ffn_post, m_w_gate_up, m_w_down, v_norm_mix_pre, v_norm_mix_post, v_norm_mem, v_w_in, v_b_forget, v_conv_w, v_conv_b, v_conv_ln_g, v_conv_ln_b, v_w_kv_mem, v_w_proj_attn, v_w_proj_conv, v_w_proj_mem, v_w_out, v_norm_ffn_pre, v_norm_ffn_post, v_w_gate_up, v_w_down):
    given = dict(locals())
    weights = {n: given[n] for n in WEIGHT_ORDER}
    moments_m = {n: given["m_" + n] for n in WEIGHT_ORDER}
    moments_v = {n: given["v_" + n] for n in WEIGHT_ORDER}
    kind = {name: k for name, _, k in SHARDED}
    names = [name for name, _, _ in SHARDED]

    def local(a, n):
        return jnp.swapaxes(a[0], 0, 1) if n in TRANSPOSED else a[0]

    blocks = {n: local(weights[n], n).astype(F32 if n == "conv_w" else BF16) for n in names}
    late = [n for n in names if n != "w_in"]
    (w_in_blocks,), w_in_token = _gather_two_level(blocks["w_in"], name="gather_w_in")
    blocks["w_kv_mem"] = _after(w_in_token, blocks["w_kv_mem"])
    late_handle, late_token = _exchange_start([blocks[n] for n in late], scatter=False, name="gather_rest_start")
    n_qkv = 3 * D_MODEL
    w_in_t = _to_full(w_in_blocks, "row")
    w_early = dict(
        wp_t=jnp.concatenate([w_in_t[:n_qkv], w_in_t[n_qkv + FOX_HEADS:]], axis=0),
        wf_t=jnp.pad(w_in_t[n_qkv:n_qkv + FOX_HEADS], ((0, LANES - FOX_HEADS), (0, 0))),
    )

    def late_weights(after):
        got = dict(zip(late, _exchange_wait(late_handle, after, scatter=False, name="gather_rest_wait")))
        full = {n: _to_full(got[n], kind[n]) for n in late if n != "conv_w"}
        return dict(
            conv_w=got["conv_w"],
            w_kv=full["w_kv_mem"], wpa=full["w_proj_attn"], wpc=full["w_proj_conv"], wpm=full["w_proj_mem"],
            w_out=full["w_out"], w_gu_t=full["w_gate_up"], w_down=full["w_down"],
        )

    sent = []

    def send_grads(group, g):
        order = [n for n in names if n in g]
        per_owner = [g[n] if g[n].ndim == 3 else _to_blocks(g[n], kind[n]) for n in order]
        handle, token = _exchange_start(per_owner, scatter=True, name="scatter_" + group + "_start")
        sent.append((group, order, handle))
        return token

    gains = {n: weights[n] for n in SMALL}
    gains["norm_mix_pre"] = _after(late_token, gains["norm_mix_pre"])

    loss_tile, grad_x, small = _local_step(x[0], mem[0], loss_target[0], gains, w_early, late_weights, send_grads)

    g_recv = {}
    for group, order, handle in sent:
        g_recv.update(zip(order, _exchange_wait(handle, grad_x, scatter=True, name="scatter_" + group + "_wait")))

    rows = [jnp.pad(small[n], ((0, 0), (0, D_MODEL - small[n].shape[1]))) for n in SMALL]
    rows.append(jnp.broadcast_to(loss_tile[0:1, 0:1], (1, D_MODEL)))
    rows.append(jnp.zeros((SMALL_ROWS - len(rows), D_MODEL), F32))
    (small_recv,), _ = _exchange([jnp.concatenate(rows, axis=0)], scatter=False, name="gather_small")

    def slab_of(d, fill):
        rows = [jnp.pad(d[n], ((0, 0), (0, D_MODEL - d[n].shape[1])), constant_values=fill) for n in SMALL]
        rows.append(jnp.full((SMALL_ROWS - len(rows), D_MODEL), fill, F32))
        return jnp.concatenate(rows, axis=0)

    grads, delta, new_m, new_v = {}, {}, {}, {}
    sg, sd, sm, sv = _adamw(slab_of(weights, 0.0), small_recv, slab_of(moments_m, 0.0), slab_of(moments_v, 1.0), name="adamw_small")
    loss = sg[LOSS_ROW, 0]
    for i, n in enumerate(SMALL):
        c = weights[n].shape[1]
        grads[n], delta[n], new_m[n], new_v[n] = sg[i:i + 1, :c], sd[i:i + 1, :c], sm[i:i + 1, :c], sv[i:i + 1, :c]
    for n in names:
        outs = _adamw(local(weights[n], n), g_recv[n], local(moments_m[n], n), local(moments_v[n], n), name="adamw_" + n)
        grads[n], delta[n], new_m[n], new_v[n] = [(jnp.swapaxes(o, 0, 1) if n in TRANSPOSED else o)[None] for o in outs]

    return (loss, grad_x[None], *[grads[n] for n in WEIGHT_ORDER], *[delta[n] for n in WEIGHT_ORDER],
            *[new_m[n] for n in WEIGHT_ORDER], *[new_v[n] for n in WEIGHT_ORDER])
```

```python
import functools

import jax
import jax.numpy as jnp
from jax import lax
from jax.experimental import pallas as pl
from jax.experimental.pallas import tpu as pltpu

F32 = jnp.float32
BF16 = jnp.bfloat16

D_MODEL = 1024
N_DEV = 8
FOX_HEADS = 16
FOX_HEAD_DIM = 64
HEAD_PAIRS = FOX_HEADS // 2
MEM_HEADS = 4
MEM_HEAD_DIM = D_MODEL // MEM_HEADS
CONV_WIDTH = 31
CONV_HALO = 32
CONV_CHUNK = 16
SUBLANES = 8
FFN_HIDDEN = 2816
RMS_EPS = 1e-6
LN_EPS = 1e-5
ADAM_LR = 0.001
ADAM_B1 = 0.9
ADAM_B2 = 0.999
ADAM_EPS = 1e-08
ADAM_WD = 0.01
ADAM_STEP = 10
KV_STEP = 2
KV_STEP_FWD = 4
NEG_BIG = -1e30
LANES = 128

PB_Q, PB_K, PB_V, PB_A, PB_GATE, PB_QMEM, PB_G0 = 0, 1, 2, 3, 4, 5, 6
P_WIDTH = 9 * D_MODEL

NT_DIMS = (((1,), (1,)), ((), ()))
TN_DIMS = (((0,), (0,)), ((), ()))


def _cparams(sem, vmem_mb=None):
    kw = dict(dimension_semantics=sem)
    if vmem_mb is not None:
        kw["vmem_limit_bytes"] = vmem_mb * 1024 * 1024
    return pltpu.CompilerParams(**kw)


def _tile(dim, want):
    t = min(dim, want)
    assert dim % t == 0, (dim, want)
    return t


def _sigmoid(z):
    return 1.0 / (1.0 + jnp.exp(-z))


def _matmul(a, b, *, mode, out_dtype, name, tm=1024, tn=1024, tk=1024, after=None):
    if mode == "nn":
        (M, K), (K2, N) = a.shape, b.shape
    elif mode == "nt":
        (M, K), (N, K2) = a.shape, b.shape
    else:
        (K, M), (K2, N) = a.shape, b.shape
    assert K == K2, (a.shape, b.shape, mode)
    tm, tn, tk = _tile(M, tm), _tile(N, tn), _tile(K, tk)
    nk = K // tk
    dims = {"nn": (((1,), (0,)), ((), ())), "nt": NT_DIMS, "tn": TN_DIMS}[mode]

    n_extra = 0 if after is None else 1

    def body(a_ref, b_ref, *rest):
        o_ref, scratch = rest[n_extra], rest[n_extra + 1:]
        part = lax.dot_general(a_ref[...], b_ref[...], dims, preferred_element_type=F32)
        if nk == 1:
            o_ref[...] = part.astype(o_ref.dtype)
        else:
            acc_ref, = scratch
            k = pl.program_id(2)

            @pl.when(k == 0)
            def _():
                acc_ref[...] = part

            @pl.when(k > 0)
            def _():
                acc_ref[...] += part

            @pl.when(k == nk - 1)
            def _():
                o_ref[...] = acc_ref[...].astype(o_ref.dtype)

    a_spec = pl.BlockSpec((tk, tm), lambda j, i, k: (k, i)) if mode == "tn" else pl.BlockSpec((tm, tk), lambda j, i, k: (i, k))
    b_spec = pl.BlockSpec((tn, tk), lambda j, i, k: (j, k)) if mode == "nt" else pl.BlockSpec((tk, tn), lambda j, i, k: (k, j))
    return pl.pallas_call(
        body,
        name=name,
        grid=(N // tn, M // tm, nk),
        in_specs=[a_spec, b_spec] + [pl.BlockSpec((8, LANES), lambda j, i, k: (0, 0))] * n_extra,
        out_specs=pl.BlockSpec((tm, tn), lambda j, i, k: (i, j)),
        out_shape=jax.ShapeDtypeStruct((M, N), out_dtype),
        scratch_shapes=[pltpu.VMEM((tm, tn), F32)] if nk > 1 else [],
        compiler_params=_cparams(("parallel", "parallel", "arbitrary"), 56),
    )(a, b, *([] if after is None else [after]))


def _matmul_pieces(pieces, b, *, mode, out_dtype, name, after=None):
    blk = 1024
    T = len(pieces)
    S = pieces[0].shape[0]
    counts = [p.shape[1] // blk for p in pieces]
    offs = [sum(counts[:t]) for t in range(T)]
    n_a = sum(counts)
    N = b.shape[1]
    n_extra = 0 if after is None else 1
    if mode == "nn":
        tm = _tile(S, blk)
        grid, n_red = (S // tm, n_a), n_a
        which = lambda i, k: k
        a_specs = [pl.BlockSpec((tm, blk), lambda i, k, o=o, c=c: (i, jnp.clip(k - o, 0, c - 1))) for o, c in zip(offs, counts)]
        b_spec = pl.BlockSpec((blk, N), lambda i, k: (k, 0))
        out_spec, out_rows, dims = pl.BlockSpec((tm, N), lambda i, k: (i, 0)), S, (((1,), (0,)), ((), ()))
    else:
        tk = _tile(S, blk)
        grid, n_red = (n_a, S // tk), S // tk
        which = lambda m, k: m

        def a_index(m, k, o, c):
            mine = (m >= o) & (m < o + c)
            return jnp.where(mine, k, 0), jnp.clip(m - o, 0, c - 1)

        a_specs = [pl.BlockSpec((tk, blk), functools.partial(a_index, o=o, c=c)) for o, c in zip(offs, counts)]
        b_spec = pl.BlockSpec((tk, N), lambda m, k: (k, 0))
        out_spec, out_rows, dims = pl.BlockSpec((blk, N), lambda m, k: (m, 0)), n_a * blk, TN_DIMS

    def body(*refs):
        a_refs, b_ref = refs[:T], refs[T]
        o_ref, acc_ref = refs[T + 1 + n_extra], refs[T + 2 + n_extra]
        a_blk = which(pl.program_id(0), pl.program_id(1))
        k = pl.program_id(1)
        for t in range(T):
            @pl.when((a_blk >= offs[t]) & (a_blk < offs[t] + counts[t]))
            def _(t=t):
                part = lax.dot_general(a_refs[t][...], b_ref[...], dims, preferred_element_type=F32)

                @pl.when(k == 0)
                def _():
                    acc_ref[...] = part

                @pl.when(k > 0)
                def _():
                    acc_ref[...] += part

        @pl.when(k == n_red - 1)
        def _():
            o_ref[...] = acc_ref[...].astype(o_ref.dtype)

    return pl.pallas_call(
        body, name=name, grid=grid,
        in_specs=a_specs + [b_spec] + [pl.BlockSpec((8, LANES), lambda i, k: (0, 0))] * n_extra,
        out_specs=out_spec,
        out_shape=jax.ShapeDtypeStruct((out_rows, N), out_dtype),
        scratch_shapes=[pltpu.VMEM(out_spec.block_shape, F32)],
        compiler_params=_cparams(("parallel", "arbitrary"), 56),
    )(*pieces, b, *([] if after is None else [after]))


def _rms_fwd(x, g, *, name, tm=512):
    S, D = x.shape
    tm = _tile(S, tm)

    def body(x_ref, g_ref, o_ref):
        xv = x_ref[...]
        rstd = lax.rsqrt(jnp.mean(xv * xv, axis=-1, keepdims=True) + RMS_EPS)
        o_ref[...] = (xv * rstd * g_ref[...]).astype(o_ref.dtype)

    return pl.pallas_call(
        body, name=name, grid=(S // tm,),
        in_specs=[pl.BlockSpec((tm, D), lambda i: (i, 0)), pl.BlockSpec((1, D), lambda i: (0, 0))],
        out_specs=pl.BlockSpec((tm, D), lambda i: (i, 0)),
        out_shape=jax.ShapeDtypeStruct((S, D), BF16),
        compiler_params=_cparams(("parallel",)),
    )(x, g)


def _rms_bwd(xin, dys, g, res, *, out_dtype, name, tm=512):
    S, D = xin.shape
    tm = _tile(S, tm)
    n_dy = len(dys)
    has_res = res is not None

    def body(*refs):
        x_ref, g_ref = refs[0], refs[1]
        dy_refs = refs[2:2 + n_dy]
        pos = 2 + n_dy
        res_ref = refs[pos] if has_res else None
        pos += int(has_res)
        dx_ref, dg_ref = refs[pos], refs[pos + 1]
        i = pl.program_id(0)
        xv = x_ref[...]
        dy = dy_refs[0][...].astype(F32)
        for r in dy_refs[1:]:
            dy = dy + r[...].astype(F32)
        rstd = lax.rsqrt(jnp.mean(xv * xv, axis=-1, keepdims=True) + RMS_EPS)
        xhat = xv * rstd
        gy = dy * g_ref[...]
        dx = rstd * (gy - xhat * jnp.mean(gy * xhat, axis=-1, keepdims=True))
        if has_res:
            dx = dx + res_ref[...]
        dx_ref[...] = dx.astype(dx_ref.dtype)
        part = jnp.sum(dy * xhat, axis=0, keepdims=True)

        @pl.when(i == 0)
        def _():
            dg_ref[...] = part

        @pl.when(i > 0)
        def _():
            dg_ref[...] += part

    row = pl.BlockSpec((tm, D), lambda i: (i, 0))
    vec = pl.BlockSpec((1, D), lambda i: (0, 0))
    ins = [xin, g] + list(dys) + ([res] if has_res else [])
    return pl.pallas_call(
        body, name=name, grid=(S // tm,),
        in_specs=[row, vec] + [row] * n_dy + ([row] if has_res else []),
        out_specs=[row, vec],
        out_shape=[jax.ShapeDtypeStruct((S, D), out_dtype), jax.ShapeDtypeStruct((1, D), F32)],
        compiler_params=_cparams(("arbitrary",)),
    )(*ins)


def _head_mask(hh, shape):
    lane = lax.broadcasted_iota(jnp.int32, shape, len(shape) - 1)
    return (lane // FOX_HEAD_DIM) == hh


def _block_constants(cr_ref, i, tq):
    first = cr_ref[0, :, pl.ds(pl.multiple_of(i * tq, tq), LANES)]
    return [first[hh:hh + 1, 0:1] for hh in range(2)]


def _attn_fwd(P, crow, *, tq, name):
    S = P.shape[0]
    tq = _tile(S, tq)
    nq = S // tq
    scale = FOX_HEAD_DIM ** -0.5

    def body(q_ref, k_ref, v_ref, cr_ref, o_ref, o32_ref, lse_ref):
        i = pl.program_id(1)
        q = q_ref[...] * jnp.asarray(scale, BF16)
        row = lax.broadcasted_iota(jnp.int32, (tq, tq), 0)
        col = lax.broadcasted_iota(jnp.int32, (tq, tq), 1)
        causal = col <= row
        hms = [_head_mask(hh, (tq, LANES)) for hh in range(2)]
        qhs = [jnp.where(hm, q, jnp.zeros_like(q)) for hm in hms]
        cbs = _block_constants(cr_ref, i, tq)

        def step(block, n_blocks, carry, masked):
            tk = n_blocks * tq
            start = pl.multiple_of(block * tq, tk)
            kj = k_ref[pl.ds(start, tk), :]
            vj = v_ref[pl.ds(start, tk), :]
            lane_k = lax.broadcasted_iota(jnp.int32, (tk, LANES), 1)
            new = []
            for hh in range(2):
                m, acc = carry[hh]
                bias = cbs[hh] - cr_ref[0, hh:hh + 1, pl.ds(start, tk)]
                s = lax.dot_general(qhs[hh], kj, NT_DIMS, preferred_element_type=F32) + bias
                if masked:
                    s = jnp.where(causal, s, NEG_BIG)
                m_new = jnp.maximum(m, jnp.max(s, axis=1, keepdims=True))
                alpha = jnp.exp(m - m_new)
                p = jnp.exp(s - m_new)
                vh = jnp.where(lane_k == (1 - hh) * FOX_HEAD_DIM, jnp.ones_like(vj), vj)
                acc = alpha * acc + jnp.dot(p.astype(BF16), vh, preferred_element_type=F32)
                new.append((m_new, acc))
            return tuple(new)

        lane = lax.broadcasted_iota(jnp.int32, (tq, LANES), 1)
        ones_lane = [lane == (1 - hh) * FOX_HEAD_DIM for hh in range(2)]
        init = (jnp.full((tq, 1), NEG_BIG, F32), jnp.zeros((tq, LANES), F32))
        n_wide = i // KV_STEP_FWD
        done = n_wide * KV_STEP_FWD
        n_pair = (i - done) // 2
        carry = lax.fori_loop(0, n_wide, lambda j, c: step(j * KV_STEP_FWD, KV_STEP_FWD, c, False), (init, init))
        carry = lax.fori_loop(0, n_pair, lambda j, c: step(done + 2 * j, 2, c, False), carry)
        carry = lax.fori_loop(done + 2 * n_pair, i, lambda j, c: step(j, 1, c, False), carry)
        carry = step(i, 1, carry, True)
        ls = [jnp.sum(jnp.where(ones_lane[hh], carry[hh][1], 0.0), axis=1, keepdims=True) for hh in range(2)]
        out = jnp.where(hms[0], carry[0][1] / ls[0], carry[1][1] / ls[1])
        for hh in range(2):
            lse_ref[hh] = carry[hh][0] + jnp.log(ls[hh])
        o_ref[...] = out.astype(o_ref.dtype)
        o32_ref[...] = out

    nblk = D_MODEL // LANES
    return pl.pallas_call(
        body, name=name, grid=(HEAD_PAIRS, nq),
        in_specs=[
            pl.BlockSpec((tq, LANES), lambda p, i: (i, PB_Q * nblk + p)),
            pl.BlockSpec((S, LANES), lambda p, i: (0, PB_K * nblk + p)),
            pl.BlockSpec((S, LANES), lambda p, i: (0, PB_V * nblk + p)),
            pl.BlockSpec((1, 2, S), lambda p, i: (p, 0, 0)),
        ],
        out_specs=[
            pl.BlockSpec((tq, LANES), lambda p, i: (i, p)),
            pl.BlockSpec((tq, LANES), lambda p, i: (i, p)),
            pl.BlockSpec((2, tq, 1), lambda p, i: (p, i, 0)),
        ],
        out_shape=[jax.ShapeDtypeStruct((S, D_MODEL), BF16), jax.ShapeDtypeStruct((S, D_MODEL), F32),
                   jax.ShapeDtypeStruct((FOX_HEADS, S, 1), F32)],
        compiler_params=_cparams(("parallel", "arbitrary"), 56),
    )(P, P, P, crow)


def _attn_bwd(P, do, o32, lse, crow, *, tq, name):
    S = P.shape[0]
    tq = _tile(S, tq)
    nq = S // tq
    scale = FOX_HEAD_DIM ** -0.5

    def body(q_ref, k_ref, v_ref, do_ref, o_ref, lse_ref, cr_ref, dq_ref, dk_out, dv_out, dc_ref, dcq_ref, dk_ref, dv_ref):
        i = pl.program_id(1)

        @pl.when(i == 0)
        def _():
            dk_ref[...] = jnp.zeros_like(dk_ref)
            dv_ref[...] = jnp.zeros_like(dv_ref)
            dc_ref[...] = jnp.zeros_like(dc_ref)

        q = q_ref[...] * jnp.asarray(scale, BF16)
        do_v = do_ref[...]
        row = lax.broadcasted_iota(jnp.int32, (tq, tq), 0)
        col = lax.broadcasted_iota(jnp.int32, (tq, tq), 1)
        causal = col <= row
        hms = [_head_mask(hh, (tq, LANES)) for hh in range(2)]
        qhs = [jnp.where(hm, q, jnp.zeros_like(q)) for hm in hms]
        dohs = [jnp.where(hm, do_v, jnp.zeros_like(do_v)) for hm in hms]
        cbs = _block_constants(cr_ref, i, tq)
        lses = [lse_ref[hh] for hh in range(2)]
        prod = do_v.astype(F32) * o_ref[...]
        dls = [jnp.sum(jnp.where(hm, prod, 0.0), axis=1, keepdims=True) for hm in hms]

        def step(block, n_blocks, carry, masked):
            dq_acc, row_sums = carry[0], list(carry[1:])
            tk = n_blocks * tq
            start = pl.multiple_of(block * tq, tk)
            kj = k_ref[pl.ds(start, tk), :]
            vj = v_ref[pl.ds(start, tk), :]
            head_k = [_head_mask(hh, (tk, LANES)) for hh in range(2)]
            dv_part = jnp.zeros((tk, LANES), F32)
            dk_part = jnp.zeros((tk, LANES), F32)
            for hh in range(2):
                bias = cbs[hh] - cr_ref[0, hh:hh + 1, pl.ds(start, tk)]
                s = lax.dot_general(qhs[hh], kj, NT_DIMS, preferred_element_type=F32) + bias
                if masked:
                    s = jnp.where(causal, s, NEG_BIG)
                p = jnp.exp(s - lses[hh])
                dp = lax.dot_general(dohs[hh], vj, NT_DIMS, preferred_element_type=F32)
                ds = p * (dp - dls[hh])
                pb = p.astype(BF16)
                dsb = ds.astype(BF16)
                dv_part = dv_part + lax.dot_general(pb, dohs[hh], TN_DIMS, preferred_element_type=F32)
                dk_part = dk_part + lax.dot_general(dsb, qhs[hh], TN_DIMS, preferred_element_type=F32)
                dc_ref[0, hh:hh + 1, pl.ds(start, tk)] -= jnp.sum(ds, axis=0, keepdims=True)
                row_sums[hh] = row_sums[hh] + jnp.sum(ds, axis=1, keepdims=True)
                kh = jnp.where(head_k[hh], kj, jnp.zeros_like(kj))
                dq_acc = dq_acc + jnp.dot(dsb, kh, preferred_element_type=F32)
            dv_ref[pl.ds(start, tk), :] += dv_part
            dk_ref[pl.ds(start, tk), :] += dk_part
            return (dq_acc, *row_sums)

        zero_col = jnp.zeros((tq, 1), F32)
        n_wide = i // KV_STEP
        carry = lax.fori_loop(0, n_wide, lambda j, c: step(j * KV_STEP, KV_STEP, c, False),
                              (jnp.zeros((tq, LANES), F32), zero_col, zero_col))
        carry = lax.fori_loop(n_wide * KV_STEP, i, lambda j, c: step(j, 1, c, False), carry)
        carry = step(i, 1, carry, True)
        dq_ref[...] = (carry[0] * scale).astype(dq_ref.dtype)
        for hh in range(2):
            dcq_ref[0, hh:hh + 1, :] = jnp.transpose(jnp.broadcast_to(carry[1 + hh], (tq, LANES)))[0:1, :]

        @pl.when(i == nq - 1)
        def _():
            dk_out[...] = dk_ref[...].astype(dk_out.dtype)
            dv_out[...] = dv_ref[...].astype(dv_out.dtype)

    nblk = D_MODEL // LANES
    qblk = pl.BlockSpec((tq, LANES), lambda p, i: (i, p))
    stat = pl.BlockSpec((2, tq, 1), lambda p, i: (p, i, 0))
    full = pl.BlockSpec((S, LANES), lambda p, i: (0, p))
    return pl.pallas_call(
        body, name=name, grid=(HEAD_PAIRS, nq),
        in_specs=[
            pl.BlockSpec((tq, LANES), lambda p, i: (i, PB_Q * nblk + p)),
            pl.BlockSpec((S, LANES), lambda p, i: (0, PB_K * nblk + p)),
            pl.BlockSpec((S, LANES), lambda p, i: (0, PB_V * nblk + p)),
            qblk, qblk, stat,
            pl.BlockSpec((1, 2, S), lambda p, i: (p, 0, 0)),
        ],
        out_specs=[qblk, full, full, pl.BlockSpec((1, 2, S), lambda p, i: (p, 0, 0)),
                   pl.BlockSpec((1, 2, tq), lambda p, i: (p, 0, i))],
        out_shape=[
            jax.ShapeDtypeStruct((S, D_MODEL), BF16),
            jax.ShapeDtypeStruct((S, D_MODEL), BF16),
            jax.ShapeDtypeStruct((S, D_MODEL), BF16),
            jax.ShapeDtypeStruct((HEAD_PAIRS, 2, S), F32),
            jax.ShapeDtypeStruct((HEAD_PAIRS, 2, S), F32),
        ],
        scratch_shapes=[pltpu.VMEM((S, LANES), F32), pltpu.VMEM((S, LANES), F32)],
        compiler_params=_cparams(("parallel", "arbitrary"), 56),
    )(P, P, P, do, o32, lse, crow)


def _cumsum_lanes(xs, *, reverse, name):
    R, S = xs[0].shape
    nb = S // LANES
    n_in = len(xs)

    def body(*refs):
        x_refs, o_ref = refs[:n_in], refs[n_in]
        r = lax.broadcasted_iota(jnp.int32, (LANES, LANES), 0)
        c = lax.broadcasted_iota(jnp.int32, (LANES, LANES), 1)
        tri = ((r >= c) if reverse else (r <= c)).astype(F32)

        def step(b, carry):
            blk = (nb - 1 - b) if reverse else b
            start = pl.multiple_of(blk * LANES, LANES)
            xb = x_refs[0][:, pl.ds(start, LANES)]
            for r in x_refs[1:]:
                xb = xb + r[:, pl.ds(start, LANES)]
            y = jnp.dot(xb, tri, precision=lax.Precision.HIGHEST, preferred_element_type=F32) + carry
            o_ref[:, pl.ds(start, LANES)] = y
            return carry + jnp.sum(xb, axis=1, keepdims=True)

        lax.fori_loop(0, nb, step, jnp.zeros((R, 1), F32))

    return pl.pallas_call(
        body, name=name,
        in_specs=[pl.BlockSpec(memory_space=pltpu.VMEM)] * n_in,
        out_specs=pl.BlockSpec(memory_space=pltpu.VMEM),
        out_shape=jax.ShapeDtypeStruct((R, S), F32),
    )(*xs)


def _forget_fwd(f_logit, b_pad, *, name, tm=1024):
    S = f_logit.shape[0]
    tm = _tile(S, tm)

    def body(f_ref, b_ref, o_ref):
        z = f_ref[...] + b_ref[...]
        o_ref[...] = jnp.minimum(z, 0.0) - jnp.log(1.0 + jnp.exp(-jnp.abs(z)))

    blk = pl.BlockSpec((tm, LANES), lambda i: (i, 0))
    return pl.pallas_call(
        body, name=name, grid=(S // tm,),
        in_specs=[blk, pl.BlockSpec((1, LANES), lambda i: (0, 0))],
        out_specs=blk, out_shape=jax.ShapeDtypeStruct((S, LANES), F32),
        compiler_params=_cparams(("parallel",)),
    )(f_logit, b_pad)


def _forget_bwd(f_logit, b_pad, dlogf, *, name, tm=1024):
    S = f_logit.shape[0]
    tm = _tile(S, tm)

    def body(f_ref, b_ref, d_ref, o_ref, db_ref):
        i = pl.program_id(0)
        z = f_ref[...] + b_ref[...]
        dz = d_ref[...] * (1.0 - _sigmoid(z))
        o_ref[...] = dz.astype(o_ref.dtype)
        part = jnp.sum(dz, axis=0, keepdims=True)

        @pl.when(i == 0)
        def _():
            db_ref[...] = part

        @pl.when(i > 0)
        def _():
            db_ref[...] += part

    blk = pl.BlockSpec((tm, LANES), lambda i: (i, 0))
    vec = pl.BlockSpec((1, LANES), lambda i: (0, 0))
    return pl.pallas_call(
        body, name=name, grid=(S // tm,),
        in_specs=[blk, vec, blk], out_specs=[blk, vec],
        out_shape=[jax.ShapeDtypeStruct((S, LANES), BF16), jax.ShapeDtypeStruct((1, LANES), F32)],
        compiler_params=_cparams(("arbitrary",)),
    )(f_logit, b_pad, dlogf)


def _layernorm_stats(y):
    mu = jnp.mean(y, axis=-1, keepdims=True)
    yc = y - mu
    rstd = lax.rsqrt(jnp.mean(yc * yc, axis=-1, keepdims=True) + LN_EPS)
    return yc * rstd, rstd


def _fill_shifted(src_ref, sh_ref, tm):
    rows = tm + CONV_HALO - SUBLANES
    for s in range(1, SUBLANES):
        sh_ref[s - 1, 0:rows, :] = src_ref[pl.ds(s, rows), :]


def _shifted_rows(src_ref, sh_ref, r0, offset, chunk=CONV_CHUNK):
    s, base = offset % SUBLANES, offset - offset % SUBLANES
    rows = pl.ds(pl.multiple_of(r0 + base, SUBLANES), chunk)
    return src_ref[rows, :] if s == 0 else sh_ref[s - 1, rows, :]


def _fill_tap_rows(w_ref, wb_ref):
    for k in range(CONV_WIDTH):
        for j in range(N_DEV):
            wb_ref[k * SUBLANES:(k + 1) * SUBLANES, j * LANES:(j + 1) * LANES] = jnp.broadcast_to(
                w_ref[j, k:k + 1, :], (SUBLANES, LANES))


def _tap_sum(wb_ref, rows_of_tap, chunk=CONV_CHUNK):
    n = chunk // SUBLANES
    accs = [None] * n
    for k in range(CONV_WIDTH):
        wk = wb_ref[k * SUBLANES:(k + 1) * SUBLANES, :]
        src = rows_of_tap(k)
        for h in range(n):
            term = wk * src[h * SUBLANES:(h + 1) * SUBLANES]
            accs[h] = term if accs[h] is None else accs[h] + term
    return jnp.concatenate(accs, axis=0)


def _conv_fwd(P, w_pad, conv_b, ln_g, ln_b, *, name, tm=256):
    S = P.shape[0]
    C = D_MODEL
    tm = _tile(S, tm)
    hb = tm // CONV_HALO

    def body(a_ref, gt_ref, ah_ref, gh_ref, w_ref, cb_ref, g_ref, b_ref, o_ref, y_ref, glu_ref, sh_ref, wb_ref):
        i = pl.program_id(0)
        halo = ah_ref[...].astype(F32) * _sigmoid(gh_ref[...].astype(F32))
        glu_ref[0:CONV_HALO, :] = jnp.where(i > 0, halo, 0.0)
        glu_ref[CONV_HALO:, :] = a_ref[...].astype(F32) * _sigmoid(gt_ref[...].astype(F32))
        _fill_shifted(glu_ref, sh_ref, tm)
        _fill_tap_rows(w_ref, wb_ref)

        rc = 2 * CONV_CHUNK

        def chunk(r, carry):
            r0 = pl.multiple_of(r * rc, rc)
            acc = _tap_sum(wb_ref, lambda k: _shifted_rows(glu_ref, sh_ref, r0, CONV_HALO - (CONV_WIDTH - 1) + k, rc), rc)
            y_ref[pl.ds(r0, rc), :] = acc + cb_ref[...]
            return carry

        lax.fori_loop(0, tm // rc, chunk, 0)
        xhat, _ = _layernorm_stats(y_ref[...])
        z = xhat * g_ref[...] + b_ref[...]
        o_ref[...] = (z * _sigmoid(z)).astype(o_ref.dtype)

    vec = pl.BlockSpec((1, C), lambda i: (0, 0))
    row = pl.BlockSpec((tm, C), lambda i: (i, 0))
    return pl.pallas_call(
        body, name=name, grid=(S // tm,),
        in_specs=[
            pl.BlockSpec((tm, C), lambda i: (i, PB_A)),
            pl.BlockSpec((tm, C), lambda i: (i, PB_GATE)),
            pl.BlockSpec((CONV_HALO, C), lambda i: (jnp.maximum(i * hb - 1, 0), PB_A)),
            pl.BlockSpec((CONV_HALO, C), lambda i: (jnp.maximum(i * hb - 1, 0), PB_GATE)),
            pl.BlockSpec((N_DEV, CONV_WIDTH, LANES), lambda i: (0, 0, 0)),
            vec, vec, vec,
        ],
        out_specs=[row, row],
        out_shape=[jax.ShapeDtypeStruct((S, C), BF16), jax.ShapeDtypeStruct((S, C), F32)],
        scratch_shapes=[pltpu.VMEM((tm + CONV_HALO, C), F32), pltpu.VMEM((SUBLANES - 1, tm + CONV_HALO, C), F32),
                        pltpu.VMEM((CONV_HALO * SUBLANES, C), F32)],
        compiler_params=_cparams(("parallel",), 56),
    )(P, P, P, P, w_pad, conv_b, ln_g, ln_b)


def _conv_bwd(P, do, y, w_pad, ln_g, ln_b, *, name, tm=256):
    S = P.shape[0]
    C = D_MODEL
    tm = _tile(S, tm)
    hb = tm // CONV_HALO
    n_tiles = S // tm
    last_halo = S // CONV_HALO - 1

    def body(a_ref, gt_ref, ah_ref, gh_ref, do_ref, y_ref, don_ref, yn_ref, w_ref, g_ref, b_ref,
             dglu_ref, dw_ref, small_ref, glu_ref, dy_ref, gsh_ref, dsh_ref, dwacc_ref, wb_ref):
        i = pl.program_id(0)

        @pl.when(i == 0)
        def _():
            dwacc_ref[...] = jnp.zeros_like(dwacc_ref)
            small_ref[...] = jnp.zeros_like(small_ref)

        def ln_bwd(do_v, y_v):
            xhat, rstd = _layernorm_stats(y_v)
            z = xhat * g_ref[...] + b_ref[...]
            sg = _sigmoid(z)
            dz = do_v * (sg * (1.0 + z * (1.0 - sg)))
            dxh = dz * g_ref[...]
            dy = rstd * (dxh - jnp.mean(dxh, axis=-1, keepdims=True) - xhat * jnp.mean(dxh * xhat, axis=-1, keepdims=True))
            return dy, dz, xhat

        dy, dz, xhat = ln_bwd(do_ref[...], y_ref[...])
        dy_next, _, _ = ln_bwd(don_ref[...], yn_ref[...])
        small_ref[0:1, :] += jnp.sum(dz * xhat, axis=0, keepdims=True)
        small_ref[1:2, :] += jnp.sum(dz, axis=0, keepdims=True)
        small_ref[2:3, :] += jnp.sum(dy, axis=0, keepdims=True)
        dy_ref[0:tm, :] = dy
        dy_ref[tm:, :] = jnp.where(i < n_tiles - 1, dy_next, 0.0)

        halo = ah_ref[...].astype(F32) * _sigmoid(gh_ref[...].astype(F32))
        glu_ref[0:CONV_HALO, :] = jnp.where(i > 0, halo, 0.0)
        glu_ref[CONV_HALO:, :] = a_ref[...].astype(F32) * _sigmoid(gt_ref[...].astype(F32))
        _fill_shifted(glu_ref, gsh_ref, tm)
        _fill_shifted(dy_ref, dsh_ref, tm)
        _fill_tap_rows(w_ref, wb_ref)

        def chunk(r, carry):
            r0 = pl.multiple_of(r * CONV_CHUNK, CONV_CHUNK)
            rows = pl.ds(r0, CONV_CHUNK)
            dyc = dy_ref[rows, :]
            for k in range(CONV_WIDTH):
                prod = dyc * _shifted_rows(glu_ref, gsh_ref, r0, CONV_HALO - (CONV_WIDTH - 1) + k)
                dwacc_ref[k * SUBLANES:(k + 1) * SUBLANES, :] += prod[0:SUBLANES] + prod[SUBLANES:]
            dg = _tap_sum(wb_ref, lambda k: _shifted_rows(dy_ref, dsh_ref, r0, CONV_WIDTH - 1 - k))
            a = a_ref[rows, :].astype(F32)
            sig = _sigmoid(gt_ref[rows, :].astype(F32))
            dglu_ref[rows, 0:C] = (dg * sig).astype(dglu_ref.dtype)
            dglu_ref[rows, C:] = (dg * a * sig * (1.0 - sig)).astype(dglu_ref.dtype)
            return carry

        lax.fori_loop(0, tm // CONV_CHUNK, chunk, 0)

        @pl.when(i == n_tiles - 1)
        def _():
            dw_ref[...] = jnp.zeros_like(dw_ref)
            for k in range(CONV_WIDTH):
                row_k = jnp.sum(dwacc_ref[k * SUBLANES:(k + 1) * SUBLANES, :], axis=0, keepdims=True)
                for j in range(N_DEV):
                    dw_ref[j, k:k + 1, :] = row_k[:, j * LANES:(j + 1) * LANES]

    vec = pl.BlockSpec((1, C), lambda i: (0, 0))
    row = pl.BlockSpec((tm, C), lambda i: (i, 0))
    nxt = pl.BlockSpec((CONV_HALO, C), lambda i: (jnp.minimum((i + 1) * hb, last_halo), 0))
    return pl.pallas_call(
        body, name=name, grid=(n_tiles,),
        in_specs=[
            pl.BlockSpec((tm, C), lambda i: (i, PB_A)),
            pl.BlockSpec((tm, C), lambda i: (i, PB_GATE)),
            pl.BlockSpec((CONV_HALO, C), lambda i: (jnp.maximum(i * hb - 1, 0), PB_A)),
            pl.BlockSpec((CONV_HALO, C), lambda i: (jnp.maximum(i * hb - 1, 0), PB_GATE)),
            row, row, nxt, nxt,
            pl.BlockSpec((N_DEV, CONV_WIDTH, LANES), lambda i: (0, 0, 0)),
            vec, vec,
        ],
        out_specs=[
            pl.BlockSpec((tm, 2 * C), lambda i: (i, 0)),
            pl.BlockSpec((N_DEV, CONV_HALO, LANES), lambda i: (0, 0, 0)),
            pl.BlockSpec((8, C), lambda i: (0, 0)),
        ],
        out_shape=[
            jax.ShapeDtypeStruct((S, 2 * C), BF16),
            jax.ShapeDtypeStruct((N_DEV, CONV_HALO, LANES), F32),
            jax.ShapeDtypeStruct((8, C), F32),
        ],
        scratch_shapes=[
            pltpu.VMEM((tm + CONV_HALO, C), F32), pltpu.VMEM((tm + CONV_HALO, C), F32),
            pltpu.VMEM((SUBLANES - 1, tm + CONV_HALO, C), F32), pltpu.VMEM((SUBLANES - 1, tm + CONV_HALO, C), F32),
            pltpu.VMEM((CONV_HALO * SUBLANES, C), F32), pltpu.VMEM((CONV_HALO * SUBLANES, C), F32),
        ],
        compiler_params=_cparams(("arbitrary",), 56),
    )(P, P, P, P, do, y, do, y, w_pad, ln_g, ln_b)


def _mem_softmax(qh, kh):
    s = lax.dot_general(qh, kh, NT_DIMS, preferred_element_type=F32)
    e = jnp.exp(s - jnp.max(s, axis=1, keepdims=True))
    return e / jnp.sum(e, axis=1, keepdims=True)


def _mem_fwd(P, kv, *, name, tm=1024):
    S, M = P.shape[0], kv.shape[0]
    tm = _tile(S, tm)
    scale = MEM_HEAD_DIM ** -0.5

    def body(q_ref, k_ref, v_ref, o_ref):
        for h in range(MEM_HEADS):
            sl = slice(h * MEM_HEAD_DIM, (h + 1) * MEM_HEAD_DIM)
            qh = q_ref[:, sl] * jnp.asarray(scale, BF16)
            p = _mem_softmax(qh, k_ref[:, sl])
            o_ref[:, sl] = jnp.dot(p.astype(BF16), v_ref[:, sl], preferred_element_type=F32).astype(o_ref.dtype)

    return pl.pallas_call(
        body, name=name, grid=(S // tm,),
        in_specs=[
            pl.BlockSpec((tm, D_MODEL), lambda i: (i, PB_QMEM)),
            pl.BlockSpec((M, D_MODEL), lambda i: (0, 0)),
            pl.BlockSpec((M, D_MODEL), lambda i: (0, 1)),
        ],
        out_specs=pl.BlockSpec((tm, D_MODEL), lambda i: (i, 0)),
        out_shape=jax.ShapeDtypeStruct((S, D_MODEL), BF16),
        compiler_params=_cparams(("parallel",)),
    )(P, kv, kv)


def _mem_bwd(P, do, kv, *, name, tm=1024):
    S, M = P.shape[0], kv.shape[0]
    tm = _tile(S, tm)
    scale = MEM_HEAD_DIM ** -0.5

    def body(q_ref, k_ref, v_ref, do_ref, dq_ref, dkv_ref):
        i = pl.program_id(0)

        @pl.when(i == 0)
        def _():
            dkv_ref[...] = jnp.zeros_like(dkv_ref)

        for h in range(MEM_HEADS):
            sl = slice(h * MEM_HEAD_DIM, (h + 1) * MEM_HEAD_DIM)
            slv = slice(D_MODEL + h * MEM_HEAD_DIM, D_MODEL + (h + 1) * MEM_HEAD_DIM)
            qh = q_ref[:, sl] * jnp.asarray(scale, BF16)
            kh, vh, doh = k_ref[:, sl], v_ref[:, sl], do_ref[:, sl]
            p = _mem_softmax(qh, kh)
            dp = lax.dot_general(doh, vh, NT_DIMS, preferred_element_type=F32)
            ds = p * (dp - jnp.sum(p * dp, axis=1, keepdims=True))
            dsb = ds.astype(BF16)
            dq_ref[:, sl] = (jnp.dot(dsb, kh, preferred_element_type=F32) * scale).astype(dq_ref.dtype)
            dkv_ref[:, sl] += lax.dot_general(dsb, qh, TN_DIMS, preferred_element_type=F32)
            dkv_ref[:, slv] += lax.dot_general(p.astype(BF16), doh, TN_DIMS, preferred_element_type=F32)

    row = pl.BlockSpec((tm, D_MODEL), lambda i: (i, 0))
    return pl.pallas_call(
        body, name=name, grid=(S // tm,),
        in_specs=[
            pl.BlockSpec((tm, D_MODEL), lambda i: (i, PB_QMEM)),
            pl.BlockSpec((M, D_MODEL), lambda i: (0, 0)),
            pl.BlockSpec((M, D_MODEL), lambda i: (0, 1)),
            row,
        ],
        out_specs=[row, pl.BlockSpec((M, 2 * D_MODEL), lambda i: (0, 0))],
        out_shape=[jax.ShapeDtypeStruct((S, D_MODEL), BF16), jax.ShapeDtypeStruct((M, 2 * D_MODEL), F32)],
        compiler_params=_cparams(("arbitrary",)),
    )(P, kv, kv, do)


def _merge_fwd(o_attn, o_conv, o_mem, P, wpa, wpc, wpm, w_out, x, g_post, g_pre, *, name, tm=256):
    S = P.shape[0]
    D = D_MODEL
    tm = _tile(S, tm)

    def body(oa_ref, oc_ref, om_ref, g0_ref, g1_ref, g2_ref, wa_ref, wc_ref, wm_ref, wo_ref, x_ref, gp_ref, gq_ref,
             mg_ref, y_ref, pb_ref, x1_ref, h2_ref):
        merged = jnp.zeros((tm, D), F32)
        for b, (o_ref, g_ref, w_ref) in enumerate(((oa_ref, g0_ref, wa_ref), (oc_ref, g1_ref, wc_ref), (om_ref, g2_ref, wm_ref))):
            pb = jnp.dot(o_ref[...], w_ref[...], preferred_element_type=F32)
            pb_ref[b] = pb.astype(pb_ref.dtype)
            merged = merged + _sigmoid(g_ref[...].astype(F32)) * pb
        mb = merged.astype(BF16)
        mg_ref[...] = mb
        yv = jnp.dot(mb, wo_ref[...], preferred_element_type=F32)
        y_ref[...] = yv
        x1 = x_ref[...] + yv * lax.rsqrt(jnp.mean(yv * yv, axis=-1, keepdims=True) + RMS_EPS) * gp_ref[...]
        x1_ref[...] = x1
        h2_ref[...] = (x1 * lax.rsqrt(jnp.mean(x1 * x1, axis=-1, keepdims=True) + RMS_EPS) * gq_ref[...]).astype(h2_ref.dtype)

    row = pl.BlockSpec((tm, D), lambda i: (i, 0))
    vec = pl.BlockSpec((1, D), lambda i: (0, 0))
    wsp = pl.BlockSpec((D, D), lambda i: (0, 0))
    return pl.pallas_call(
        body, name=name, grid=(S // tm,),
        in_specs=[row, row, row] + [pl.BlockSpec((tm, D), lambda i, b=b: (i, PB_G0 + b)) for b in range(3)] + [wsp] * 4
        + [row, vec, vec],
        out_specs=[row, row, pl.BlockSpec((3, tm, D), lambda i: (0, i, 0)), row, row],
        out_shape=[jax.ShapeDtypeStruct((S, D), BF16), jax.ShapeDtypeStruct((S, D), F32), jax.ShapeDtypeStruct((3, S, D), BF16),
                   jax.ShapeDtypeStruct((S, D), F32), jax.ShapeDtypeStruct((S, D), BF16)],
        compiler_params=_cparams(("parallel",), 56),
    )(o_attn, o_conv, o_mem, P, P, P, wpa, wpc, wpm, w_out, x, g_post, g_pre)


def _merge_bwd(dy1, w_out, P, pb, *, name, tm=256):
    S = P.shape[0]
    D = D_MODEL
    tm = _tile(S, tm)

    def body(dy_ref, wo_ref, g0_ref, g1_ref, g2_ref, pb_ref, d0_ref, d1_ref, d2_ref, dgl_ref):
        dm = lax.dot_general(dy_ref[...], wo_ref[...], NT_DIMS, preferred_element_type=F32)
        for b, (g_ref, d_ref) in enumerate(((g0_ref, d0_ref), (g1_ref, d1_ref), (g2_ref, d2_ref))):
            g = _sigmoid(g_ref[...].astype(F32))
            d_ref[...] = (dm * g).astype(d_ref.dtype)
            dgl_ref[:, b * D:(b + 1) * D] = (dm * pb_ref[b].astype(F32) * g * (1.0 - g)).astype(dgl_ref.dtype)

    row = pl.BlockSpec((tm, D), lambda i: (i, 0))
    blk3 = pl.BlockSpec((3, tm, D), lambda i: (0, i, 0))
    sds = jax.ShapeDtypeStruct((S, D), BF16)
    return pl.pallas_call(
        body, name=name, grid=(S // tm,),
        in_specs=[row, pl.BlockSpec((D, D), lambda i: (0, 0))]
        + [pl.BlockSpec((tm, D), lambda i, b=b: (i, PB_G0 + b)) for b in range(3)] + [blk3],
        out_specs=[row, row, row, pl.BlockSpec((tm, 3 * D), lambda i: (i, 0))],
        out_shape=[sds, sds, sds, jax.ShapeDtypeStruct((S, 3 * D), BF16)],
        compiler_params=_cparams(("parallel",), 56),
    )(dy1, w_out, P, P, P, pb)


def _swiglu_fwd(h2, w_gu_t, *, name, tm=512, tn=1408):
    S, D = h2.shape
    Fh = w_gu_t.shape[0] // 2
    tm, tn = _tile(S, tm), _tile(Fh, tn)
    nj = Fh // tn

    def body(h_ref, wg_ref, wu_ref, g_ref, u_ref, a_ref):
        hv = h_ref[...]
        g = lax.dot_general(hv, wg_ref[...], NT_DIMS, preferred_element_type=F32)
        u = lax.dot_general(hv, wu_ref[...], NT_DIMS, preferred_element_type=F32)
        g_ref[...] = g.astype(g_ref.dtype)
        u_ref[...] = u.astype(u_ref.dtype)
        a_ref[...] = (g * _sigmoid(g) * u).astype(a_ref.dtype)

    out = pl.BlockSpec((tm, tn), lambda j, i: (i, j))
    sds = jax.ShapeDtypeStruct((S, Fh), BF16)
    return pl.pallas_call(
        body, name=name, grid=(nj, S // tm),
        in_specs=[
            pl.BlockSpec((tm, D), lambda j, i: (i, 0)),
            pl.BlockSpec((tn, D), lambda j, i: (j, 0)),
            pl.BlockSpec((tn, D), lambda j, i: (j + nj, 0)),
        ],
        out_specs=[out, out, out], out_shape=[sds, sds, sds],
        compiler_params=_cparams(("parallel", "parallel"), 56),
    )(h2, w_gu_t, w_gu_t)


def _swiglu_bwd(dffn, w_down, g, u, *, name, tm=512, tn=1408):
    S, Fh = g.shape
    D = dffn.shape[1]
    tm, tn = _tile(S, tm), _tile(Fh, tn)

    def body(df_ref, w_ref, g_ref, u_ref, o_ref):
        dfv = df_ref[...]
        for j in range(Fh // tn):
            cols = slice(j * tn, (j + 1) * tn)
            da = lax.dot_general(dfv, w_ref[cols, :], NT_DIMS, preferred_element_type=F32)
            gv = g_ref[:, cols].astype(F32)
            sg = _sigmoid(gv)
            o_ref[:, cols] = (da * u_ref[:, cols].astype(F32) * (sg * (1.0 + gv * (1.0 - sg)))).astype(o_ref.dtype)
            o_ref[:, Fh + j * tn:Fh + (j + 1) * tn] = (da * gv * sg).astype(o_ref.dtype)

    act = pl.BlockSpec((tm, Fh), lambda i: (i, 0))
    return pl.pallas_call(
        body, name=name, grid=(S // tm,),
        in_specs=[pl.BlockSpec((tm, D), lambda i: (i, 0)), pl.BlockSpec((Fh, D), lambda i: (0, 0)), act, act],
        out_specs=pl.BlockSpec((tm, 2 * Fh), lambda i: (i, 0)),
        out_shape=jax.ShapeDtypeStruct((S, 2 * Fh), BF16),
        compiler_params=_cparams(("parallel",), 56),
    )(dffn, w_down, g, u)


def _final(x1, ffn, target, g, *, name, tm=512):
    S, D = x1.shape
    tm = _tile(S, tm)

    def body(x_ref, f_ref, t_ref, g_ref, dout_ref, dffn_ref, loss_ref, dg_ref):
        i = pl.program_id(0)
        fv = f_ref[...]
        rstd = lax.rsqrt(jnp.mean(fv * fv, axis=-1, keepdims=True) + RMS_EPS)
        r = fv * rstd
        e = x_ref[...] + r * g_ref[...] - t_ref[...]
        dout = e * (1.0 / D)
        dout_ref[...] = dout
        gy = dout * g_ref[...]
        dffn_ref[...] = (rstd * (gy - r * jnp.mean(gy * r, axis=-1, keepdims=True))).astype(dffn_ref.dtype)
        lpart = jnp.full((8, LANES), 0.5 * jnp.sum(jnp.mean(e * e, axis=-1, keepdims=True)), F32)
        gpart = jnp.sum(dout * r, axis=0, keepdims=True)

        @pl.when(i == 0)
        def _():
            loss_ref[...] = lpart
            dg_ref[...] = gpart

        @pl.when(i > 0)
        def _():
            loss_ref[...] += lpart
            dg_ref[...] += gpart

    row = pl.BlockSpec((tm, D), lambda i: (i, 0))
    vec = pl.BlockSpec((1, D), lambda i: (0, 0))
    return pl.pallas_call(
        body, name=name, grid=(S // tm,),
        in_specs=[row, row, row, vec],
        out_specs=[row, row, pl.BlockSpec((8, LANES), lambda i: (0, 0)), vec],
        out_shape=[jax.ShapeDtypeStruct((S, D), F32), jax.ShapeDtypeStruct((S, D), BF16),
                   jax.ShapeDtypeStruct((8, LANES), F32), jax.ShapeDtypeStruct((1, D), F32)],
        compiler_params=_cparams(("arbitrary",)),
    )(x1, ffn, target, g)


def _after(token, value):
    return value if token is None else value + token[0, 0].astype(value.dtype)


def _local_step(x, mem, target, gains, w, late_weights, send_grads, *, tq=512):
    S = x.shape[0]
    b_pad = jnp.pad(gains["b_forget"], ((0, 0), (0, LANES - FOX_HEADS)))
    gains = dict(gains)

    h = _rms_fwd(x, gains["norm_mix_pre"], name="rms_mix_pre")
    P = _matmul(h, w["wp_t"], mode="nt", out_dtype=BF16, name="proj_in", tm=2048)
    f_logit = _matmul(h, w["wf_t"], mode="nt", out_dtype=F32, name="proj_forget")
    logf = _forget_fwd(f_logit, b_pad, name="forget_fwd")
    c_row16 = _cumsum_lanes([logf[:, :FOX_HEADS].T], reverse=False, name="forget_cumsum")
    crow = c_row16.reshape(HEAD_PAIRS, 2, S)
    o_attn, o_attn32, lse = _attn_fwd(P, crow, tq=tq, name="attn_fwd")
    w = dict(w, **late_weights(o_attn))
    o_conv, y_conv = _conv_fwd(P, w["conv_w"], gains["conv_b"], gains["conv_ln_g"], gains["conv_ln_b"], name="conv_fwd")
    mem_n = _rms_fwd(mem, gains["norm_mem"], name="rms_mem")
    kv = _matmul(mem_n, w["w_kv"], mode="nn", out_dtype=BF16, name="mem_kv")
    o_mem = _mem_fwd(P, kv, name="mem_fwd")
    merged, y1, pb, x1, h2 = _merge_fwd(o_attn, o_conv, o_mem, P, w["wpa"], w["wpc"], w["wpm"], w["w_out"], x,
                                        gains["norm_mix_post"], gains["norm_ffn_pre"], name="merge_fwd")
    g_ffn, u_ffn, act = _swiglu_fwd(h2, w["w_gu_t"], name="swiglu_fwd")
    ffn = _matmul(act, w["w_down"], mode="nn", out_dtype=F32, name="ffn_down", tk=1408)

    dout, dffn, loss_tile, d_norm_ffn_post = _final(x1, ffn, target, gains["norm_ffn_post"], name="loss_head")
    d_w_down = _matmul(act, dffn, mode="tn", out_dtype=BF16, name="dw_down", tm=1408)
    dgu = _swiglu_bwd(dffn, w["w_down"], g_ffn, u_ffn, name="swiglu_bwd")
    dh2 = _matmul(dgu, w["w_gu_t"], mode="nn", out_dtype=F32, name="d_h2", tk=1408)
    d_w_gu_t = _matmul(dgu, h2, mode="tn", out_dtype=BF16, name="dw_gate_up", tm=1408)
    dx1, d_norm_ffn_pre = _rms_bwd(x1, [dh2], gains["norm_ffn_pre"], dout, out_dtype=F32, name="rms_ffn_pre_bwd")

    dy1, d_norm_mix_post = _rms_bwd(y1, [dx1], gains["norm_mix_post"], None, out_dtype=BF16, name="rms_mix_post_bwd")
    d_w_out = _matmul(merged, dy1, mode="tn", out_dtype=BF16, name="dw_out")
    *dpb, dgl = _merge_bwd(dy1, w["w_out"], P, pb, name="merge_bwd")
    do_attn = _matmul(dpb[0], w["wpa"], mode="nt", out_dtype=BF16, name="d_o_attn")
    do_conv = _matmul(dpb[1], w["wpc"], mode="nt", out_dtype=F32, name="d_o_conv")
    do_mem = _matmul(dpb[2], w["wpm"], mode="nt", out_dtype=BF16, name="d_o_mem")
    d_wpa = _matmul(o_attn, dpb[0], mode="tn", out_dtype=BF16, name="dw_proj_attn")
    d_wpc = _matmul(o_conv, dpb[1], mode="tn", out_dtype=BF16, name="dw_proj_conv")
    d_wpm = _matmul(o_mem, dpb[2], mode="tn", out_dtype=BF16, name="dw_proj_mem")

    dq_mem, dkv = _mem_bwd(P, do_mem, kv, name="mem_bwd")
    dkv_b = dkv.astype(BF16)
    d_w_kv = _matmul(mem_n, dkv_b, mode="tn", out_dtype=BF16, name="dw_kv")
    dmem_n = _matmul(dkv_b, w["w_kv"], mode="nt", out_dtype=F32, name="d_mem_n")
    _, d_norm_mem = _rms_bwd(mem, [dmem_n], gains["norm_mem"], None, out_dtype=BF16, name="rms_mem_bwd")

    dglu, d_conv_w, conv_small = _conv_bwd(P, do_conv, y_conv, w["conv_w"], gains["conv_ln_g"], gains["conv_ln_b"], name="conv_bwd")
    sent = send_grads("body", dict(conv_w=d_conv_w[:, :CONV_WIDTH].astype(BF16), w_kv_mem=d_w_kv, w_proj_attn=d_wpa,
                                   w_proj_conv=d_wpc, w_proj_mem=d_wpm, w_out=d_w_out, w_gate_up=d_w_gu_t, w_down=d_w_down))

    dq, dk, dv, dc, dcq = _attn_bwd(P, do_attn, o_attn32, lse, _after(sent, crow), tq=tq, name="attn_bwd")
    dlogf16 = _cumsum_lanes([dc.reshape(FOX_HEADS, S), dcq.reshape(FOX_HEADS, S)], reverse=True, name="forget_cumsum_bwd")
    dlogf = jnp.pad(dlogf16.T, ((0, 0), (0, LANES - FOX_HEADS)))
    df, d_b_pad = _forget_bwd(f_logit, b_pad, dlogf, name="forget_bwd")

    d_proj = [dq, dk, dv, dglu, dq_mem, dgl]
    d_wp_t = _matmul_pieces(d_proj, h, mode="tn", out_dtype=BF16, name="dw_in")
    d_wf_t = _matmul(df, h, mode="tn", out_dtype=BF16, name="dw_forget")
    n_qkv = 3 * D_MODEL
    pieces = [(d_wp_t[:n_qkv], n_qkv), (d_wf_t[:FOX_HEADS], FOX_HEADS), (d_wp_t[n_qkv:], P_WIDTH - n_qkv)]
    n_own = (P_WIDTH + FOX_HEADS) // N_DEV
    blocks = [jnp.concatenate(_rows(pieces, j * n_own, (j + 1) * n_own), axis=0) for j in range(N_DEV)]
    sent = send_grads("w_in", dict(w_in=jnp.stack(blocks)))
    dh_p = _matmul_pieces(d_proj, w["wp_t"], mode="nn", out_dtype=F32, name="d_h", after=sent)
    dh_f = _matmul(df, w["wf_t"], mode="nn", out_dtype=F32, name="d_h_forget")
    grad_x, d_norm_mix_pre = _rms_bwd(x, [dh_p, dh_f], gains["norm_mix_pre"], dx1, out_dtype=F32, name="rms_mix_pre_bwd")

    small = dict(norm_mix_pre=d_norm_mix_pre, norm_mix_post=d_norm_mix_post, norm_mem=d_norm_mem,
                 conv_b=conv_small[2:3], conv_ln_g=conv_small[0:1], conv_ln_b=conv_small[1:2],
                 norm_ffn_pre=d_norm_ffn_pre, norm_ffn_post=d_norm_ffn_post, b_forget=d_b_pad[:, :FOX_HEADS])
    return loss_tile, grad_x, small


def _mesh_pos():
    return lax.axis_index("x"), lax.axis_index("y"), lax.axis_index("c")


def _flip(pos, d):
    x, y, c = pos
    return (1 - x if d & 4 else x, 1 - y if d & 2 else y, 1 - c if d & 1 else c)


def _flat(pos):
    x, y, c = pos
    return 4 * x + 2 * y + c


def _gather_two_level(src, *, name):
    def body(src_ref, out_ref, token, send_sems, recv_sems, local_sem):
        me = _mesh_pos()
        my = _flat(me)
        sibling = _flip(me, 1)
        far = [_flip(me, d) for d in (4, 2, 6)]

        def copy(k, block, to, own=False):
            return pltpu.make_async_remote_copy(
                src_ref=src_ref if own else out_ref.at[_flat(block)], dst_ref=out_ref.at[_flat(block)],
                send_sem=send_sems.at[k], recv_sem=recv_sems.at[k], device_id=to, device_id_type=pl.DeviceIdType.MESH)

        local = pltpu.make_async_copy(src_ref, out_ref.at[my], local_sem)
        local.start()
        first = [copy(0, me, sibling, own=True)] + [copy(1 + j, me, peer, own=True) for j, peer in enumerate(far)]
        for cp in first:
            cp.start()
        passed = [copy(4 + j, peer, sibling) for j, peer in enumerate(far)]
        for j, peer in enumerate(far):
            copy(1 + j, peer, me).wait_recv()
            passed[j].start()
        copy(0, sibling, me).wait_recv()
        for j, peer in enumerate(far):
            copy(4 + j, _flip(peer, 1), me).wait_recv()
        for cp in first + passed:
            cp.wait_send()
        local.wait()
        token[...] = jnp.zeros_like(token)

    n_copy = N_DEV - 1
    out, token = pl.pallas_call(
        body, name=name,
        in_specs=[pl.BlockSpec(memory_space=pl.ANY)],
        out_specs=[pl.BlockSpec(memory_space=pl.ANY), pl.BlockSpec(memory_space=pltpu.VMEM)],
        out_shape=[jax.ShapeDtypeStruct((N_DEV,) + tuple(src.shape), src.dtype), jax.ShapeDtypeStruct((8, LANES), F32)],
        scratch_shapes=[pltpu.SemaphoreType.DMA((n_copy,)), pltpu.SemaphoreType.DMA((n_copy,)), pltpu.SemaphoreType.DMA(())],
    )(src)
    return [out], token


def _exchange(srcs, *, scatter, name):
    T = len(srcs)
    n_peer = N_DEV - 1

    def body(*refs):
        src_refs, out_refs = refs[:T], refs[T:2 * T]
        token, send_sems, recv_sems, local_sems = refs[2 * T:]
        me = _mesh_pos()
        my = _flat(me)

        def copy(t, d):
            peer = _flip(me, d)
            return pltpu.make_async_remote_copy(
                src_ref=src_refs[t].at[_flat(peer)] if scatter else src_refs[t],
                dst_ref=out_refs[t].at[my],
                send_sem=send_sems.at[t * n_peer + d - 1], recv_sem=recv_sems.at[t * n_peer + d - 1],
                device_id=peer, device_id_type=pl.DeviceIdType.MESH)

        def arrival(t, d):
            peer = _flip(me, d)
            return pltpu.make_async_remote_copy(
                src_ref=src_refs[t].at[my] if scatter else src_refs[t],
                dst_ref=out_refs[t].at[_flat(peer)],
                send_sem=send_sems.at[t * n_peer + d - 1], recv_sem=recv_sems.at[t * n_peer + d - 1],
                device_id=peer, device_id_type=pl.DeviceIdType.MESH)

        local = [pltpu.make_async_copy(src_refs[t].at[my] if scatter else src_refs[t], out_refs[t].at[my], local_sems.at[t])
                 for t in range(T)]
        for cp in local:
            cp.start()
        sends = [copy(t, d) for d in range(1, N_DEV) for t in range(T)]
        for cp in sends:
            cp.start()
        for d in range(1, N_DEV):
            for t in range(T):
                arrival(t, d).wait_recv()
        for cp in sends:
            cp.wait_send()
        for cp in local:
            cp.wait()
        token[...] = jnp.zeros_like(token)

    outs = pl.pallas_call(
        body, name=name,
        in_specs=[pl.BlockSpec(memory_space=pl.ANY)] * T,
        out_specs=[pl.BlockSpec(memory_space=pl.ANY)] * T + [pl.BlockSpec(memory_space=pltpu.VMEM)],
        out_shape=[jax.ShapeDtypeStruct((N_DEV,) + tuple(s.shape[-2:]), s.dtype) for s in srcs] + [jax.ShapeDtypeStruct((8, LANES), F32)],
        scratch_shapes=[pltpu.SemaphoreType.DMA((T * n_peer,)), pltpu.SemaphoreType.DMA((T * n_peer,)), pltpu.SemaphoreType.DMA((T,))],
    )(*srcs)
    return list(outs[:T]), outs[T]


HBM_SPEC = pl.BlockSpec(memory_space=pltpu.HBM)
SEM_SPEC = pl.BlockSpec(memory_space=pltpu.SEMAPHORE)


def _split_copies(src_refs, land_refs, send_sems, recv_sems, scatter):
    me = _mesh_pos()
    my = _flat(me)
    n_peer = N_DEV - 1
    out = []
    for d in range(1, N_DEV):
        peer = _flip(me, d)
        for t, (src, land) in enumerate(zip(src_refs, land_refs)):
            k = t * n_peer + d - 1
            start = pltpu.make_async_remote_copy(
                src_ref=src.at[_flat(peer)] if scatter else src, dst_ref=land.at[my],
                send_sem=send_sems.at[k], recv_sem=recv_sems.at[k], device_id=peer, device_id_type=pl.DeviceIdType.MESH)
            arrive = pltpu.make_async_remote_copy(
                src_ref=src.at[my] if scatter else src, dst_ref=land.at[_flat(peer)],
                send_sem=send_sems.at[k], recv_sem=recv_sems.at[k], device_id=peer, device_id_type=pl.DeviceIdType.MESH)
            out.append((start, arrive))
    return out


def _local_copies(src_refs, land_refs, local_sems, scatter):
    my = _flat(_mesh_pos())
    return [pltpu.make_async_copy(src.at[my] if scatter else src, land.at[my], local_sems.at[t])
            for t, (src, land) in enumerate(zip(src_refs, land_refs))]


def _exchange_start(srcs, *, scatter, name):
    T = len(srcs)
    n_sem = T * (N_DEV - 1)
    lands = [lax.empty((N_DEV,) + tuple(s.shape[-2:]), s.dtype) for s in srcs]

    def body(*refs):
        src_refs, land_refs = refs[:T], refs[T:2 * T]
        send_sems, recv_sems, local_sems = refs[2 * T:2 * T + 3]
        token = refs[-1]
        for cp in _local_copies(src_refs, land_refs, local_sems, scatter):
            cp.start()
        for start, _ in _split_copies(src_refs, land_refs, send_sems, recv_sems, scatter):
            start.start()
        token[...] = jnp.zeros_like(token)

    hbm = lambda a: pltpu.HBM(a.shape, a.dtype)
    outs = pl.pallas_call(
        body, name=name,
        in_specs=[HBM_SPEC] * (2 * T),
        out_specs=(SEM_SPEC, SEM_SPEC, SEM_SPEC, *[HBM_SPEC] * (2 * T), pl.BlockSpec(memory_space=pltpu.VMEM)),
        out_shape=(pltpu.SemaphoreType.DMA((n_sem,)), pltpu.SemaphoreType.DMA((n_sem,)), pltpu.SemaphoreType.DMA((T,)),
                   *[hbm(s) for s in srcs], *[hbm(a) for a in lands], jax.ShapeDtypeStruct((8, LANES), F32)),
        input_output_aliases={t: 3 + t for t in range(2 * T)},
        compiler_params=pltpu.CompilerParams(has_side_effects=pltpu.SideEffectType.DATAFLOW_SIDE_EFFECTING),
    )(*[pltpu.with_memory_space_constraint(s, pltpu.HBM) for s in srcs],
      *[pltpu.with_memory_space_constraint(a, pltpu.HBM) for a in lands])
    return (outs[0], outs[1], outs[2], list(outs[3:3 + T]), list(outs[3 + T:3 + 2 * T])), outs[-1]


def _exchange_wait(handle, after, *, scatter, name):
    send_sems, recv_sems, local_sems, srcs, lands = handle
    T = len(srcs)

    def body(*refs):
        src_refs, land_refs = refs[:T], refs[T:2 * T]
        s_sems, r_sems, l_sems = refs[2 * T:2 * T + 3]
        for cp in _local_copies(src_refs, land_refs, l_sems, scatter):
            cp.wait()
        for start, arrive in _split_copies(src_refs, land_refs, s_sems, r_sems, scatter):
            start.wait_send()
            arrive.wait_recv()

    hbm = lambda a: pltpu.HBM(a.shape, a.dtype)
    outs = pl.pallas_call(
        body, name=name,
        in_specs=[HBM_SPEC] * (2 * T) + [SEM_SPEC, SEM_SPEC, SEM_SPEC, pl.BlockSpec(memory_space=pl.ANY)],
        out_specs=[HBM_SPEC] * (2 * T),
        out_shape=[hbm(s) for s in srcs] + [hbm(a) for a in lands],
        input_output_aliases={t: t for t in range(2 * T)},
        compiler_params=pltpu.CompilerParams(has_side_effects=pltpu.SideEffectType.DATAFLOW_SIDE_EFFECTING),
    )(*srcs, *lands, send_sems, recv_sems, local_sems, after)
    return list(outs[T:])


def _adamw(w, slots, m, v, *, name):
    R, C = w.shape
    row_tiles = [t for t in range(16, 513, 16) if R % t == 0]
    tr = R if R <= 512 or not row_tiles else max(row_tiles)
    tc = C if tr < R or R <= 512 else LANES
    assert R % tr == 0 and C % tc == 0, (R, C, tr, tc)

    def body(w_ref, s_ref, m_ref, v_ref, g_ref, d_ref, m2_ref, v2_ref):
        gv = s_ref[0].astype(F32)
        for j in range(1, N_DEV):
            gv = gv + s_ref[j].astype(F32)
        g_ref[...] = gv
        m2 = ADAM_B1 * m_ref[...] + (1.0 - ADAM_B1) * gv
        v2 = ADAM_B2 * v_ref[...] + (1.0 - ADAM_B2) * (gv * gv)
        m_hat = m2 / (1.0 - ADAM_B1 ** ADAM_STEP)
        v_hat = v2 / (1.0 - ADAM_B2 ** ADAM_STEP)
        d_ref[...] = -ADAM_LR * (m_hat / (jnp.sqrt(v_hat) + ADAM_EPS) + ADAM_WD * w_ref[...])
        m2_ref[...] = m2
        v2_ref[...] = v2

    blk = pl.BlockSpec((tr, tc), lambda i, j: (i, j))
    sds = jax.ShapeDtypeStruct((R, C), F32)
    return pl.pallas_call(
        body, name=name, grid=(R // tr, C // tc),
        in_specs=[blk, pl.BlockSpec((N_DEV, tr, tc), lambda i, j: (0, i, j)), blk, blk],
        out_specs=[blk] * 4, out_shape=[sds] * 4,
        compiler_params=_cparams(("parallel", "parallel")),
    )(w, slots, m, v)


SHARDED = (
    ("w_in", (1154, 1024), "row"), ("conv_w", (31, 128), "col"), ("w_kv_mem", (1024, 256), "col"),
    ("w_proj_attn", (128, 1024), "row"), ("w_proj_conv", (128, 1024), "row"), ("w_proj_mem", (128, 1024), "row"),
    ("w_out", (128, 1024), "row"), ("w_gate_up", (704, 1024), "row"), ("w_down", (352, 1024), "row"),
)
TRANSPOSED = ("w_in", "w_gate_up")
SMALL = ("norm_mix_pre", "norm_mix_post", "norm_mem", "conv_b", "conv_ln_g", "conv_ln_b", "norm_ffn_pre", "norm_ffn_post", "b_forget")
SMALL_ROWS = 16
LOSS_ROW = len(SMALL)
WEIGHT_ORDER = ("norm_mix_pre", "norm_mix_post", "norm_mem", "w_in", "b_forget", "conv_w", "conv_b", "conv_ln_g", "conv_ln_b",
                "w_kv_mem", "w_proj_attn", "w_proj_conv", "w_proj_mem", "w_out", "norm_ffn_pre", "norm_ffn_post", "w_gate_up", "w_down")


def _rows(pieces, lo, hi):
    out, start = [], 0
    for a, n in pieces:
        a0, a1 = max(lo, start), min(hi, start + n)
        if a0 < a1:
            out.append(a[a0 - start:a1 - start])
        start += n
    return out


def _to_full(blocks8, kind):
    n, r, c = blocks8.shape
    if kind == "col":
        return jnp.concatenate([blocks8[j] for j in range(n)], axis=1)
    return blocks8.reshape(n * r, c)


def _to_blocks(full, kind):
    if kind == "col":
        c = full.shape[1] // N_DEV
        return jnp.stack([full[:, j * c:(j + 1) * c] for j in range(N_DEV)])
    nr, c = full.shape
    return full.reshape(N_DEV, nr // N_DEV, c)


def kernel(x, mem, norm_mix_pre, norm_mix_post, norm_mem, w_in, b_forget, conv_w, conv_b, conv_ln_g, conv_ln_b, w_kv_mem, w_proj_attn, w_proj_conv, w_proj_mem, w_out, norm_ffn_pre, norm_ffn_post, w_gate_up, w_down, loss_target, m_norm_mix_pre, m_norm_mix_post, m_norm_mem, m_w_in, m_b_forget, m_conv_w, m_conv_b, m_conv_ln_g, m_conv_ln_b, m_w_kv_mem, m_w_proj_attn, m_w_proj_conv, m_w_proj_mem, m_w_out, m_norm_ffn_pre, m_norm_ffn_post, m_w_gate_up, m_w_down, v_norm_mix_pre, v_norm_mix_post, v_norm_mem, v_w_in, v_b_forget, v_conv_w, v_conv_b, v_conv_ln_g, v_conv_ln_b, v_w_kv_mem, v_w_proj_attn, v_w_proj_conv, v_w_proj_mem, v_w_out, v_norm_ffn_pre, v_norm_ffn_post, v_w_gate_up, v_w_down):
    given = dict(locals())
    weights = {n: given[n] for n in WEIGHT_ORDER}
    moments_m = {n: given["m_" + n] for n in WEIGHT_ORDER}
    moments_v = {n: given["v_" + n] for n in WEIGHT_ORDER}
    kind = {name: k for name, _, k in SHARDED}
    names = [name for name, _, _ in SHARDED]

    def local(a, n):
        return jnp.swapaxes(a[0], 0, 1) if n in TRANSPOSED else a[0]

    blocks = {n: local(weights[n], n).astype(F32 if n == "conv_w" else BF16) for n in names}
    late = [n for n in names if n != "w_in"]
    (w_in_blocks,), w_in_token = _gather_two_level(blocks["w_in"], name="gather_w_in")
    blocks["w_kv_mem"] = _after(w_in_token, blocks["w_kv_mem"])
    late_handle, late_token = _exchange_start([blocks[n] for n in late], scatter=False, name="gather_rest_start")
    n_qkv = 3 * D_MODEL
    w_in_t = _to_full(w_in_blocks, "row")
    w_early = dict(
        wp_t=jnp.concatenate([w_in_t[:n_qkv], w_in_t[n_qkv + FOX_HEADS:]], axis=0),
        wf_t=jnp.pad(w_in_t[n_qkv:n_qkv + FOX_HEADS], ((0, LANES - FOX_HEADS), (0, 0))),
    )

    def late_weights(after):
        got = dict(zip(late, _exchange_wait(late_handle, after, scatter=False, name="gather_rest_wait")))
        full = {n: _to_full(got[n], kind[n]) for n in late if n != "conv_w"}
        return dict(
            conv_w=got["conv_w"],
            w_kv=full["w_kv_mem"], wpa=full["w_proj_attn"], wpc=full["w_proj_conv"], wpm=full["w_proj_mem"],
            w_out=full["w_out"], w_gu_t=full["w_gate_up"], w_down=full["w_down"],
        )

    sent = []

    def send_grads(group, g):
        order = [n for n in names if n in g]
        per_owner = [g[n] if g[n].ndim == 3 else _to_blocks(g[n], kind[n]) for n in order]
        handle, token = _exchange_start(per_owner, scatter=True, name="scatter_" + group + "_start")
        sent.append((group, order, handle))
        return token

    gains = {n: weights[n] for n in SMALL}
    gains["norm_mix_pre"] = _after(late_token, gains["norm_mix_pre"])

    loss_tile, grad_x, small = _local_step(x[0], mem[0], loss_target[0], gains, w_early, late_weights, send_grads)

    g_recv = {}
    for group, order, handle in sent:
        g_recv.update(zip(order, _exchange_wait(handle, grad_x, scatter=True, name="scatter_" + group + "_wait")))

    rows = [jnp.pad(small[n], ((0, 0), (0, D_MODEL - small[n].shape[1]))) for n in SMALL]
    rows.append(jnp.broadcast_to(loss_tile[0:1, 0:1], (1, D_MODEL)))
    rows.append(jnp.zeros((SMALL_ROWS - len(rows), D_MODEL), F32))
    (small_recv,), _ = _exchange([jnp.concatenate(rows, axis=0)], scatter=False, name="gather_small")

    def slab_of(d, fill):
        rows = [jnp.pad(d[n], ((0, 0), (0, D_MODEL - d[n].shape[1])), constant_values=fill) for n in SMALL]
        rows.append(jnp.full((SMALL_ROWS - len(rows), D_MODEL), fill, F32))
        return jnp.concatenate(rows, axis=0)

    grads, delta, new_m, new_v = {}, {}, {}, {}
    sg, sd, sm, sv = _adamw(slab_of(weights, 0.0), small_recv, slab_of(moments_m, 0.0), slab_of(moments_v, 1.0), name="adamw_small")
    loss = sg[LOSS_ROW, 0]
    for i, n in enumerate(SMALL):
        c = weights[n].shape[1]
        grads[n], delta[n], new_m[n], new_v[n] = sg[i:i + 1, :c], sd[i:i + 1, :c], sm[i:i + 1, :c], sv[i:i + 1, :c]
    for n in names:
        outs = _adamw(local(weights[n], n), g_recv[n], local(moments_m[n], n), local(moments_v[n], n), name="adamw_" + n)
        grads[n], delta[n], new_m[n], new_v[n] = [(jnp.swapaxes(o, 0, 1) if n in TRANSPOSED else o)[None] for o in outs]

    return (loss, grad_x[None], *[grads[n] for n in WEIGHT_ORDER], *[delta[n] for n in WEIGHT_ORDER],
            *[new_m[n] for n in WEIGHT_ORDER], *[new_v[n] for n in WEIGHT_ORDER])
```

```python
import functools

import jax
import jax.numpy as jnp
from jax import lax
from jax.experimental import pallas as pl
from jax.experimental.pallas import tpu as pltpu

F32 = jnp.float32
BF16 = jnp.bfloat16

D_MODEL = 1024
N_DEV = 8
FOX_HEADS = 16
FOX_HEAD_DIM = 64
HEAD_PAIRS = FOX_HEADS // 2
MEM_HEADS = 4
MEM_HEAD_DIM = D_MODEL // MEM_HEADS
CONV_WIDTH = 31
CONV_HALO = 32
CONV_CHUNK = 16
SUBLANES = 8
FFN_HIDDEN = 2816
RMS_EPS = 1e-6
LN_EPS = 1e-5
ADAM_LR = 0.001
ADAM_B1 = 0.9
ADAM_B2 = 0.999
ADAM_EPS = 1e-08
ADAM_WD = 0.01
ADAM_STEP = 10
KV_STEP = 2
KV_STEP_FWD = 4
NEG_BIG = -1e30
LANES = 128

PB_Q, PB_K, PB_V, PB_A, PB_GATE, PB_QMEM, PB_G0 = 0, 1, 2, 3, 4, 5, 6
P_WIDTH = 9 * D_MODEL

NT_DIMS = (((1,), (1,)), ((), ()))
TN_DIMS = (((0,), (0,)), ((), ()))


def _cparams(sem, vmem_mb=None):
    kw = dict(dimension_semantics=sem)
    if vmem_mb is not None:
        kw["vmem_limit_bytes"] = vmem_mb * 1024 * 1024
    return pltpu.CompilerParams(**kw)


def _tile(dim, want):
    t = min(dim, want)
    assert dim % t == 0, (dim, want)
    return t


def _sigmoid(z):
    return 1.0 / (1.0 + jnp.exp(-z))


def _matmul(a, b, *, mode, out_dtype, name, tm=1024, tn=1024, tk=1024, after=None):
    if mode == "nn":
        (M, K), (K2, N) = a.shape, b.shape
    elif mode == "nt":
        (M, K), (N, K2) = a.shape, b.shape
    else:
        (K, M), (K2, N) = a.shape, b.shape
    assert K == K2, (a.shape, b.shape, mode)
    tm, tn, tk = _tile(M, tm), _tile(N, tn), _tile(K, tk)
    nk = K // tk
    dims = {"nn": (((1,), (0,)), ((), ())), "nt": NT_DIMS, "tn": TN_DIMS}[mode]

    n_extra = 0 if after is None else 1

    def body(a_ref, b_ref, *rest):
        o_ref, scratch = rest[n_extra], rest[n_extra + 1:]
        part = lax.dot_general(a_ref[...], b_ref[...], dims, preferred_element_type=F32)
        if nk == 1:
            o_ref[...] = part.astype(o_ref.dtype)
        else:
            acc_ref, = scratch
            k = pl.program_id(2)

            @pl.when(k == 0)
            def _():
                acc_ref[...] = part

            @pl.when(k > 0)
            def _():
                acc_ref[...] += part

            @pl.when(k == nk - 1)
            def _():
                o_ref[...] = acc_ref[...].astype(o_ref.dtype)

    a_spec = pl.BlockSpec((tk, tm), lambda j, i, k: (k, i)) if mode == "tn" else pl.BlockSpec((tm, tk), lambda j, i, k: (i, k))
    b_spec = pl.BlockSpec((tn, tk), lambda j, i, k: (j, k)) if mode == "nt" else pl.BlockSpec((tk, tn), lambda j, i, k: (k, j))
    return pl.pallas_call(
        body,
        name=name,
        grid=(N // tn, M // tm, nk),
        in_specs=[a_spec, b_spec] + [pl.BlockSpec((8, LANES), lambda j, i, k: (0, 0))] * n_extra,
        out_specs=pl.BlockSpec((tm, tn), lambda j, i, k: (i, j)),
        out_shape=jax.ShapeDtypeStruct((M, N), out_dtype),
        scratch_shapes=[pltpu.VMEM((tm, tn), F32)] if nk > 1 else [],
        compiler_params=_cparams(("parallel", "parallel", "arbitrary"), 56),
    )(a, b, *([] if after is None else [after]))


def _matmul_pieces(pieces, b, *, mode, out_dtype, name, after=None):
    blk = 1024
    T = len(pieces)
    S = pieces[0].shape[0]
    counts = [p.shape[1] // blk for p in pieces]
    offs = [sum(counts[:t]) for t in range(T)]
    n_a = sum(counts)
    N = b.shape[1]
    n_extra = 0 if after is None else 1
    if mode == "nn":
        tm = _tile(S, blk)
        grid, n_red = (S // tm, n_a), n_a
        which = lambda i, k: k
        a_specs = [pl.BlockSpec((tm, blk), lambda i, k, o=o, c=c: (i, jnp.clip(k - o, 0, c - 1))) for o, c in zip(offs, counts)]
        b_spec = pl.BlockSpec((blk, N), lambda i, k: (k, 0))
        out_spec, out_rows, dims = pl.BlockSpec((tm, N), lambda i, k: (i, 0)), S, (((1,), (0,)), ((), ()))
    else:
        tk = _tile(S, blk)
        grid, n_red = (n_a, S // tk), S // tk
        which = lambda m, k: m

        def a_index(m, k, o, c):
            mine = (m >= o) & (m < o + c)
            return jnp.where(mine, k, 0), jnp.clip(m - o, 0, c - 1)

        a_specs = [pl.BlockSpec((tk, blk), functools.partial(a_index, o=o, c=c)) for o, c in zip(offs, counts)]
        b_spec = pl.BlockSpec((tk, N), lambda m, k: (k, 0))
        out_spec, out_rows, dims = pl.BlockSpec((blk, N), lambda m, k: (m, 0)), n_a * blk, TN_DIMS

    def body(*refs):
        a_refs, b_ref = refs[:T], refs[T]
        o_ref, acc_ref = refs[T + 1 + n_extra], refs[T + 2 + n_extra]
        a_blk = which(pl.program_id(0), pl.program_id(1))
        k = pl.program_id(1)
        for t in range(T):
            @pl.when((a_blk >= offs[t]) & (a_blk < offs[t] + counts[t]))
            def _(t=t):
                part = lax.dot_general(a_refs[t][...], b_ref[...], dims, preferred_element_type=F32)

                @pl.when(k == 0)
                def _():
                    acc_ref[...] = part

                @pl.when(k > 0)
                def _():
                    acc_ref[...] += part

        @pl.when(k == n_red - 1)
        def _():
            o_ref[...] = acc_ref[...].astype(o_ref.dtype)

    return pl.pallas_call(
        body, name=name, grid=grid,
        in_specs=a_specs + [b_spec] + [pl.BlockSpec((8, LANES), lambda i, k: (0, 0))] * n_extra,
        out_specs=out_spec,
        out_shape=jax.ShapeDtypeStruct((out_rows, N), out_dtype),
        scratch_shapes=[pltpu.VMEM(out_spec.block_shape, F32)],
        compiler_params=_cparams(("parallel", "arbitrary"), 56),
    )(*pieces, b, *([] if after is None else [after]))


def _rms_fwd(x, g, *, name, tm=512):
    S, D = x.shape
    tm = _tile(S, tm)

    def body(x_ref, g_ref, o_ref):
        xv = x_ref[...]
        rstd = lax.rsqrt(jnp.mean(xv * xv, axis=-1, keepdims=True) + RMS_EPS)
        o_ref[...] = (xv * rstd * g_ref[...]).astype(o_ref.dtype)

    return pl.pallas_call(
        body, name=name, grid=(S // tm,),
        in_specs=[pl.BlockSpec((tm, D), lambda i: (i, 0)), pl.BlockSpec((1, D), lambda i: (0, 0))],
        out_specs=pl.BlockSpec((tm, D), lambda i: (i, 0)),
        out_shape=jax.ShapeDtypeStruct((S, D), BF16),
        compiler_params=_cparams(("parallel",)),
    )(x, g)


def _rms_bwd(xin, dys, g, res, *, out_dtype, name, tm=512):
    S, D = xin.shape
    tm = _tile(S, tm)
    n_dy = len(dys)
    has_res = res is not None

    def body(*refs):
        x_ref, g_ref = refs[0], refs[1]
        dy_refs = refs[2:2 + n_dy]
        pos = 2 + n_dy
        res_ref = refs[pos] if has_res else None
        pos += int(has_res)
        dx_ref, dg_ref = refs[pos], refs[pos + 1]
        i = pl.program_id(0)
        xv = x_ref[...]
        dy = dy_refs[0][...].astype(F32)
        for r in dy_refs[1:]:
            dy = dy + r[...].astype(F32)
        rstd = lax.rsqrt(jnp.mean(xv * xv, axis=-1, keepdims=True) + RMS_EPS)
        xhat = xv * rstd
        gy = dy * g_ref[...]
        dx = rstd * (gy - xhat * jnp.mean(gy * xhat, axis=-1, keepdims=True))
        if has_res:
            dx = dx + res_ref[...]
        dx_ref[...] = dx.astype(dx_ref.dtype)
        part = jnp.sum(dy * xhat, axis=0, keepdims=True)

        @pl.when(i == 0)
        def _():
            dg_ref[...] = part

        @pl.when(i > 0)
        def _():
            dg_ref[...] += part

    row = pl.BlockSpec((tm, D), lambda i: (i, 0))
    vec = pl.BlockSpec((1, D), lambda i: (0, 0))
    ins = [xin, g] + list(dys) + ([res] if has_res else [])
    return pl.pallas_call(
        body, name=name, grid=(S // tm,),
        in_specs=[row, vec] + [row] * n_dy + ([row] if has_res else []),
        out_specs=[row, vec],
        out_shape=[jax.ShapeDtypeStruct((S, D), out_dtype), jax.ShapeDtypeStruct((1, D), F32)],
        compiler_params=_cparams(("arbitrary",)),
    )(*ins)


def _rms_bwd_pair(x1, dh2, dout, g_pre, y1, g_post, *, name, tm=512):
    S, D = x1.shape
    tm = _tile(S, tm)

    def norm_bwd(xv, dy, g):
        rstd = lax.rsqrt(jnp.mean(xv * xv, axis=-1, keepdims=True) + RMS_EPS)
        xhat = xv * rstd
        gy = dy * g
        return rstd * (gy - xhat * jnp.mean(gy * xhat, axis=-1, keepdims=True)), jnp.sum(dy * xhat, axis=0, keepdims=True)

    def body(x_ref, dh_ref, do_ref, gq_ref, y_ref, gp_ref, dx_ref, dy_ref, dgq_ref, dgp_ref):
        i = pl.program_id(0)
        dx, part_q = norm_bwd(x_ref[...], dh_ref[...], gq_ref[...])
        dx1 = dx + do_ref[...]
        dx_ref[...] = dx1
        dy1, part_p = norm_bwd(y_ref[...], dx1, gp_ref[...])
        dy_ref[...] = dy1.astype(dy_ref.dtype)

        @pl.when(i == 0)
        def _():
            dgq_ref[...] = part_q
            dgp_ref[...] = part_p

        @pl.when(i > 0)
        def _():
            dgq_ref[...] += part_q
            dgp_ref[...] += part_p

    row = pl.BlockSpec((tm, D), lambda i: (i, 0))
    vec = pl.BlockSpec((1, D), lambda i: (0, 0))
    return pl.pallas_call(
        body, name=name, grid=(S // tm,),
        in_specs=[row, row, row, vec, row, vec], out_specs=[row, row, vec, vec],
        out_shape=[jax.ShapeDtypeStruct((S, D), F32), jax.ShapeDtypeStruct((S, D), BF16),
                   jax.ShapeDtypeStruct((1, D), F32), jax.ShapeDtypeStruct((1, D), F32)],
        compiler_params=_cparams(("arbitrary",)),
    )(x1, dh2, dout, g_pre, y1, g_post)


def _head_mask(hh, shape):
    lane = lax.broadcasted_iota(jnp.int32, shape, len(shape) - 1)
    return (lane // FOX_HEAD_DIM) == hh


def _block_constants(cr_ref, i, tq):
    first = cr_ref[0, :, pl.ds(pl.multiple_of(i * tq, tq), LANES)]
    return [first[hh:hh + 1, 0:1] for hh in range(2)]


def _attn_fwd(P, crow, *, tq, name):
    S = P.shape[0]
    tq = _tile(S, tq)
    nq = S // tq
    scale = FOX_HEAD_DIM ** -0.5

    def body(q_ref, k_ref, v_ref, cr_ref, o_ref, o32_ref, lse_ref):
        i = pl.program_id(1)
        q = q_ref[...] * jnp.asarray(scale, BF16)
        row = lax.broadcasted_iota(jnp.int32, (tq, tq), 0)
        col = lax.broadcasted_iota(jnp.int32, (tq, tq), 1)
        causal = col <= row
        hms = [_head_mask(hh, (tq, LANES)) for hh in range(2)]
        qhs = [jnp.where(hm, q, jnp.zeros_like(q)) for hm in hms]
        cbs = _block_constants(cr_ref, i, tq)

        def step(block, n_blocks, carry, masked):
            tk = n_blocks * tq
            start = pl.multiple_of(block * tq, tk)
            kj = k_ref[pl.ds(start, tk), :]
            vj = v_ref[pl.ds(start, tk), :]
            lane_k = lax.broadcasted_iota(jnp.int32, (tk, LANES), 1)
            new = []
            for hh in range(2):
                m, acc = carry[hh]
                bias = cbs[hh] - cr_ref[0, hh:hh + 1, pl.ds(start, tk)]
                s = lax.dot_general(qhs[hh], kj, NT_DIMS, preferred_element_type=F32) + bias
                if masked:
                    s = jnp.where(causal, s, NEG_BIG)
                m_new = jnp.maximum(m, jnp.max(s, axis=1, keepdims=True))
                alpha = jnp.exp(m - m_new)
                p = jnp.exp(s - m_new)
                vh = jnp.where(lane_k == (1 - hh) * FOX_HEAD_DIM, jnp.ones_like(vj), vj)
                acc = alpha * acc + jnp.dot(p.astype(BF16), vh, preferred_element_type=F32)
                new.append((m_new, acc))
            return tuple(new)

        lane = lax.broadcasted_iota(jnp.int32, (tq, LANES), 1)
        ones_lane = [lane == (1 - hh) * FOX_HEAD_DIM for hh in range(2)]
        init = (jnp.full((tq, 1), NEG_BIG, F32), jnp.zeros((tq, LANES), F32))
        n_wide = i // KV_STEP_FWD
        done = n_wide * KV_STEP_FWD
        n_pair = (i - done) // 2
        carry = lax.fori_loop(0, n_wide, lambda j, c: step(j * KV_STEP_FWD, KV_STEP_FWD, c, False), (init, init))
        carry = lax.fori_loop(0, n_pair, lambda j, c: step(done + 2 * j, 2, c, False), carry)
        carry = lax.fori_loop(done + 2 * n_pair, i, lambda j, c: step(j, 1, c, False), carry)
        carry = step(i, 1, carry, True)
        ls = [jnp.sum(jnp.where(ones_lane[hh], carry[hh][1], 0.0), axis=1, keepdims=True) for hh in range(2)]
        out = jnp.where(hms[0], carry[0][1] / ls[0], carry[1][1] / ls[1])
        for hh in range(2):
            lse_ref[hh] = carry[hh][0] + jnp.log(ls[hh])
        o_ref[...] = out.astype(o_ref.dtype)
        o32_ref[...] = out

    nblk = D_MODEL // LANES
    return pl.pallas_call(
        body, name=name, grid=(HEAD_PAIRS, nq),
        in_specs=[
            pl.BlockSpec((tq, LANES), lambda p, i: (i, PB_Q * nblk + p)),
            pl.BlockSpec((S, LANES), lambda p, i: (0, PB_K * nblk + p)),
            pl.BlockSpec((S, LANES), lambda p, i: (0, PB_V * nblk + p)),
            pl.BlockSpec((1, 2, S), lambda p, i: (p, 0, 0)),
        ],
        out_specs=[
            pl.BlockSpec((tq, LANES), lambda p, i: (i, p)),
            pl.BlockSpec((tq, LANES), lambda p, i: (i, p)),
            pl.BlockSpec((2, tq, 1), lambda p, i: (p, i, 0)),
        ],
        out_shape=[jax.ShapeDtypeStruct((S, D_MODEL), BF16), jax.ShapeDtypeStruct((S, D_MODEL), F32),
                   jax.ShapeDtypeStruct((FOX_HEADS, S, 1), F32)],
        compiler_params=_cparams(("parallel", "arbitrary"), 56),
    )(P, P, P, crow)


def _attn_bwd(P, do, o32, lse, crow, *, tq, name):
    S = P.shape[0]
    tq = _tile(S, tq)
    nq = S // tq
    scale = FOX_HEAD_DIM ** -0.5

    def body(q_ref, k_ref, v_ref, do_ref, o_ref, lse_ref, cr_ref, dq_ref, dk_out, dv_out, dc_ref, dcq_ref, dk_ref, dv_ref):
        i = pl.program_id(1)

        @pl.when(i == 0)
        def _():
            dk_ref[...] = jnp.zeros_like(dk_ref)
            dv_ref[...] = jnp.zeros_like(dv_ref)
            dc_ref[...] = jnp.zeros_like(dc_ref)

        q = q_ref[...] * jnp.asarray(scale, BF16)
        do_v = do_ref[...]
        row = lax.broadcasted_iota(jnp.int32, (tq, tq), 0)
        col = lax.broadcasted_iota(jnp.int32, (tq, tq), 1)
        causal = col <= row
        hms = [_head_mask(hh, (tq, LANES)) for hh in range(2)]
        qhs = [jnp.where(hm, q, jnp.zeros_like(q)) for hm in hms]
        dohs = [jnp.where(hm, do_v, jnp.zeros_like(do_v)) for hm in hms]
        cbs = _block_constants(cr_ref, i, tq)
        lses = [lse_ref[hh] for hh in range(2)]
        prod = do_v.astype(F32) * o_ref[...]
        dls = [jnp.sum(jnp.where(hm, prod, 0.0), axis=1, keepdims=True) for hm in hms]

        def step(block, n_blocks, carry, masked):
            dq_acc, row_sums = carry[0], list(carry[1:])
            tk = n_blocks * tq
            start = pl.multiple_of(block * tq, tk)
            kj = k_ref[pl.ds(start, tk), :]
            vj = v_ref[pl.ds(start, tk), :]
            head_k = [_head_mask(hh, (tk, LANES)) for hh in range(2)]
            dv_part = jnp.zeros((tk, LANES), F32)
            dk_part = jnp.zeros((tk, LANES), F32)
            for hh in range(2):
                bias = cbs[hh] - cr_ref[0, hh:hh + 1, pl.ds(start, tk)]
                s = lax.dot_general(qhs[hh], kj, NT_DIMS, preferred_element_type=F32) + bias
                if masked:
                    s = jnp.where(causal, s, NEG_BIG)
                p = jnp.exp(s - lses[hh])
                dp = lax.dot_general(dohs[hh], vj, NT_DIMS, preferred_element_type=F32)
                ds = p * (dp - dls[hh])
                pb = p.astype(BF16)
                dsb = ds.astype(BF16)
                dv_part = dv_part + lax.dot_general(pb, dohs[hh], TN_DIMS, preferred_element_type=F32)
                dk_part = dk_part + lax.dot_general(dsb, qhs[hh], TN_DIMS, preferred_element_type=F32)
                dc_ref[0, hh:hh + 1, pl.ds(start, tk)] -= jnp.sum(ds, axis=0, keepdims=True)
                row_sums[hh] = row_sums[hh] + jnp.sum(ds, axis=1, keepdims=True)
                kh = jnp.where(head_k[hh], kj, jnp.zeros_like(kj))
                dq_acc = dq_acc + jnp.dot(dsb, kh, preferred_element_type=F32)
            dv_ref[pl.ds(start, tk), :] += dv_part
            dk_ref[pl.ds(start, tk), :] += dk_part
            return (dq_acc, *row_sums)

        zero_col = jnp.zeros((tq, 1), F32)
        n_wide = i // KV_STEP
        carry = lax.fori_loop(0, n_wide, lambda j, c: step(j * KV_STEP, KV_STEP, c, False),
                              (jnp.zeros((tq, LANES), F32), zero_col, zero_col))
        carry = lax.fori_loop(n_wide * KV_STEP, i, lambda j, c: step(j, 1, c, False), carry)
        carry = step(i, 1, carry, True)
        dq_ref[...] = (carry[0] * scale).astype(dq_ref.dtype)
        for hh in range(2):
            dcq_ref[0, hh:hh + 1, :] = jnp.transpose(jnp.broadcast_to(carry[1 + hh], (tq, LANES)))[0:1, :]

        @pl.when(i == nq - 1)
        def _():
            dk_out[...] = dk_ref[...].astype(dk_out.dtype)
            dv_out[...] = dv_ref[...].astype(dv_out.dtype)

    nblk = D_MODEL // LANES
    qblk = pl.BlockSpec((tq, LANES), lambda p, i: (i, p))
    stat = pl.BlockSpec((2, tq, 1), lambda p, i: (p, i, 0))
    full = pl.BlockSpec((S, LANES), lambda p, i: (0, p))
    return pl.pallas_call(
        body, name=name, grid=(HEAD_PAIRS, nq),
        in_specs=[
            pl.BlockSpec((tq, LANES), lambda p, i: (i, PB_Q * nblk + p)),
            pl.BlockSpec((S, LANES), lambda p, i: (0, PB_K * nblk + p)),
            pl.BlockSpec((S, LANES), lambda p, i: (0, PB_V * nblk + p)),
            qblk, qblk, stat,
            pl.BlockSpec((1, 2, S), lambda p, i: (p, 0, 0)),
        ],
        out_specs=[qblk, full, full, pl.BlockSpec((1, 2, S), lambda p, i: (p, 0, 0)),
                   pl.BlockSpec((1, 2, tq), lambda p, i: (p, 0, i))],
        out_shape=[
            jax.ShapeDtypeStruct((S, D_MODEL), BF16),
            jax.ShapeDtypeStruct((S, D_MODEL), BF16),
            jax.ShapeDtypeStruct((S, D_MODEL), BF16),
            jax.ShapeDtypeStruct((HEAD_PAIRS, 2, S), F32),
            jax.ShapeDtypeStruct((HEAD_PAIRS, 2, S), F32),
        ],
        scratch_shapes=[pltpu.VMEM((S, LANES), F32), pltpu.VMEM((S, LANES), F32)],
        compiler_params=_cparams(("parallel", "arbitrary"), 56),
    )(P, P, P, do, o32, lse, crow)


def _cumsum_lanes(xs, *, reverse, name):
    R, S = xs[0].shape
    nb = S // LANES
    n_in = len(xs)

    def body(*refs):
        x_refs, o_ref = refs[:n_in], refs[n_in]
        r = lax.broadcasted_iota(jnp.int32, (LANES, LANES), 0)
        c = lax.broadcasted_iota(jnp.int32, (LANES, LANES), 1)
        tri = ((r >= c) if reverse else (r <= c)).astype(F32)

        def step(b, carry):
            blk = (nb - 1 - b) if reverse else b
            start = pl.multiple_of(blk * LANES, LANES)
            xb = x_refs[0][:, pl.ds(start, LANES)]
            for r in x_refs[1:]:
                xb = xb + r[:, pl.ds(start, LANES)]
            y = jnp.dot(xb, tri, precision=lax.Precision.HIGHEST, preferred_element_type=F32) + carry
            o_ref[:, pl.ds(start, LANES)] = y
            return carry + jnp.sum(xb, axis=1, keepdims=True)

        lax.fori_loop(0, nb, step, jnp.zeros((R, 1), F32))

    return pl.pallas_call(
        body, name=name,
        in_specs=[pl.BlockSpec(memory_space=pltpu.VMEM)] * n_in,
        out_specs=pl.BlockSpec(memory_space=pltpu.VMEM),
        out_shape=jax.ShapeDtypeStruct((R, S), F32),
    )(*xs)


def _forget_fwd(f_logit, b_pad, *, name, tm=1024):
    S = f_logit.shape[0]
    tm = _tile(S, tm)

    def body(f_ref, b_ref, o_ref):
        z = f_ref[...] + b_ref[...]
        o_ref[...] = jnp.minimum(z, 0.0) - jnp.log(1.0 + jnp.exp(-jnp.abs(z)))

    blk = pl.BlockSpec((tm, LANES), lambda i: (i, 0))
    return pl.pallas_call(
        body, name=name, grid=(S // tm,),
        in_specs=[blk, pl.BlockSpec((1, LANES), lambda i: (0, 0))],
        out_specs=blk, out_shape=jax.ShapeDtypeStruct((S, LANES), F32),
        compiler_params=_cparams(("parallel",)),
    )(f_logit, b_pad)


def _forget_bwd(f_logit, b_pad, dlogf, *, name, tm=1024):
    S = f_logit.shape[0]
    tm = _tile(S, tm)

    def body(f_ref, b_ref, d_ref, o_ref, db_ref):
        i = pl.program_id(0)
        z = f_ref[...] + b_ref[...]
        dz = d_ref[...] * (1.0 - _sigmoid(z))
        o_ref[...] = dz.astype(o_ref.dtype)
        part = jnp.sum(dz, axis=0, keepdims=True)

        @pl.when(i == 0)
        def _():
            db_ref[...] = part

        @pl.when(i > 0)
        def _():
            db_ref[...] += part

    blk = pl.BlockSpec((tm, LANES), lambda i: (i, 0))
    vec = pl.BlockSpec((1, LANES), lambda i: (0, 0))
    return pl.pallas_call(
        body, name=name, grid=(S // tm,),
        in_specs=[blk, vec, blk], out_specs=[blk, vec],
        out_shape=[jax.ShapeDtypeStruct((S, LANES), BF16), jax.ShapeDtypeStruct((1, LANES), F32)],
        compiler_params=_cparams(("arbitrary",)),
    )(f_logit, b_pad, dlogf)


def _layernorm_stats(y):
    mu = jnp.mean(y, axis=-1, keepdims=True)
    yc = y - mu
    rstd = lax.rsqrt(jnp.mean(yc * yc, axis=-1, keepdims=True) + LN_EPS)
    return yc * rstd, rstd


def _fill_shifted(src_ref, sh_ref, tm):
    rows = tm + CONV_HALO - SUBLANES
    for s in range(1, SUBLANES):
        sh_ref[s - 1, 0:rows, :] = src_ref[pl.ds(s, rows), :]


def _shifted_rows(src_ref, sh_ref, r0, offset, chunk=CONV_CHUNK):
    s, base = offset % SUBLANES, offset - offset % SUBLANES
    rows = pl.ds(pl.multiple_of(r0 + base, SUBLANES), chunk)
    return src_ref[rows, :] if s == 0 else sh_ref[s - 1, rows, :]


def _fill_tap_rows(w_ref, wb_ref):
    for k in range(CONV_WIDTH):
        for j in range(N_DEV):
            wb_ref[k * SUBLANES:(k + 1) * SUBLANES, j * LANES:(j + 1) * LANES] = jnp.broadcast_to(
                w_ref[j, k:k + 1, :], (SUBLANES, LANES))


def _tap_sum(wb_ref, rows_of_tap, chunk=CONV_CHUNK):
    n = chunk // SUBLANES
    accs = [None] * n
    for k in range(CONV_WIDTH):
        wk = wb_ref[k * SUBLANES:(k + 1) * SUBLANES, :]
        src = rows_of_tap(k)
        for h in range(n):
            term = wk * src[h * SUBLANES:(h + 1) * SUBLANES]
            accs[h] = term if accs[h] is None else accs[h] + term
    return jnp.concatenate(accs, axis=0)


def _conv_fwd(P, w_pad, conv_b, ln_g, ln_b, *, name, tm=256):
    S = P.shape[0]
    C = D_MODEL
    tm = _tile(S, tm)
    hb = tm // CONV_HALO

    def body(a_ref, gt_ref, ah_ref, gh_ref, w_ref, cb_ref, g_ref, b_ref, o_ref, y_ref, glu_ref, sh_ref, wb_ref):
        i = pl.program_id(0)
        halo = ah_ref[...].astype(F32) * _sigmoid(gh_ref[...].astype(F32))
        glu_ref[0:CONV_HALO, :] = jnp.where(i > 0, halo, 0.0)
        glu_ref[CONV_HALO:, :] = a_ref[...].astype(F32) * _sigmoid(gt_ref[...].astype(F32))
        _fill_shifted(glu_ref, sh_ref, tm)
        _fill_tap_rows(w_ref, wb_ref)

        rc = 2 * CONV_CHUNK

        def chunk(r, carry):
            r0 = pl.multiple_of(r * rc, rc)
            acc = _tap_sum(wb_ref, lambda k: _shifted_rows(glu_ref, sh_ref, r0, CONV_HALO - (CONV_WIDTH - 1) + k, rc), rc)
            y_ref[pl.ds(r0, rc), :] = acc + cb_ref[...]
            return carry

        lax.fori_loop(0, tm // rc, chunk, 0)
        xhat, _ = _layernorm_stats(y_ref[...])
        z = xhat * g_ref[...] + b_ref[...]
        o_ref[...] = (z * _sigmoid(z)).astype(o_ref.dtype)

    vec = pl.BlockSpec((1, C), lambda i: (0, 0))
    row = pl.BlockSpec((tm, C), lambda i: (i, 0))
    return pl.pallas_call(
        body, name=name, grid=(S // tm,),
        in_specs=[
            pl.BlockSpec((tm, C), lambda i: (i, PB_A)),
            pl.BlockSpec((tm, C), lambda i: (i, PB_GATE)),
            pl.BlockSpec((CONV_HALO, C), lambda i: (jnp.maximum(i * hb - 1, 0), PB_A)),
            pl.BlockSpec((CONV_HALO, C), lambda i: (jnp.maximum(i * hb - 1, 0), PB_GATE)),
            pl.BlockSpec((N_DEV, CONV_WIDTH, LANES), lambda i: (0, 0, 0)),
            vec, vec, vec,
        ],
        out_specs=[row, row],
        out_shape=[jax.ShapeDtypeStruct((S, C), BF16), jax.ShapeDtypeStruct((S, C), F32)],
        scratch_shapes=[pltpu.VMEM((tm + CONV_HALO, C), F32), pltpu.VMEM((SUBLANES - 1, tm + CONV_HALO, C), F32),
                        pltpu.VMEM((CONV_HALO * SUBLANES, C), F32)],
        compiler_params=_cparams(("parallel",), 56),
    )(P, P, P, P, w_pad, conv_b, ln_g, ln_b)


def _conv_bwd(P, do, y, w_pad, ln_g, ln_b, *, name, tm=256):
    S = P.shape[0]
    C = D_MODEL
    tm = _tile(S, tm)
    hb = tm // CONV_HALO
    n_tiles = S // tm
    last_halo = S // CONV_HALO - 1

    def body(a_ref, gt_ref, ah_ref, gh_ref, do_ref, y_ref, don_ref, yn_ref, w_ref, g_ref, b_ref,
             dglu_ref, dw_ref, small_ref, glu_ref, dy_ref, gsh_ref, dsh_ref, dwacc_ref, wb_ref):
        i = pl.program_id(0)

        @pl.when(i == 0)
        def _():
            dwacc_ref[...] = jnp.zeros_like(dwacc_ref)
            small_ref[...] = jnp.zeros_like(small_ref)

        def ln_bwd(do_v, y_v):
            xhat, rstd = _layernorm_stats(y_v)
            z = xhat * g_ref[...] + b_ref[...]
            sg = _sigmoid(z)
            dz = do_v * (sg * (1.0 + z * (1.0 - sg)))
            dxh = dz * g_ref[...]
            dy = rstd * (dxh - jnp.mean(dxh, axis=-1, keepdims=True) - xhat * jnp.mean(dxh * xhat, axis=-1, keepdims=True))
            return dy, dz, xhat

        dy, dz, xhat = ln_bwd(do_ref[...], y_ref[...])
        dy_next, _, _ = ln_bwd(don_ref[...], yn_ref[...])
        small_ref[0:1, :] += jnp.sum(dz * xhat, axis=0, keepdims=True)
        small_ref[1:2, :] += jnp.sum(dz, axis=0, keepdims=True)
        small_ref[2:3, :] += jnp.sum(dy, axis=0, keepdims=True)
        dy_ref[0:tm, :] = dy
        dy_ref[tm:, :] = jnp.where(i < n_tiles - 1, dy_next, 0.0)

        halo = ah_ref[...].astype(F32) * _sigmoid(gh_ref[...].astype(F32))
        glu_ref[0:CONV_HALO, :] = jnp.where(i > 0, halo, 0.0)
        glu_ref[CONV_HALO:, :] = a_ref[...].astype(F32) * _sigmoid(gt_ref[...].astype(F32))
        _fill_shifted(glu_ref, gsh_ref, tm)
        _fill_shifted(dy_ref, dsh_ref, tm)
        _fill_tap_rows(w_ref, wb_ref)

        def chunk(r, carry):
            r0 = pl.multiple_of(r * CONV_CHUNK, CONV_CHUNK)
            rows = pl.ds(r0, CONV_CHUNK)
            dyc = dy_ref[rows, :]
            for k in range(CONV_WIDTH):
                prod = dyc * _shifted_rows(glu_ref, gsh_ref, r0, CONV_HALO - (CONV_WIDTH - 1) + k)
                dwacc_ref[k * SUBLANES:(k + 1) * SUBLANES, :] += prod[0:SUBLANES] + prod[SUBLANES:]
            dg = _tap_sum(wb_ref, lambda k: _shifted_rows(dy_ref, dsh_ref, r0, CONV_WIDTH - 1 - k))
            a = a_ref[rows, :].astype(F32)
            sig = _sigmoid(gt_ref[rows, :].astype(F32))
            dglu_ref[rows, 0:C] = (dg * sig).astype(dglu_ref.dtype)
            dglu_ref[rows, C:] = (dg * a * sig * (1.0 - sig)).astype(dglu_ref.dtype)
            return carry

        lax.fori_loop(0, tm // CONV_CHUNK, chunk, 0)

        @pl.when(i == n_tiles - 1)
        def _():
            dw_ref[...] = jnp.zeros_like(dw_ref)
            for k in range(CONV_WIDTH):
                row_k = jnp.sum(dwacc_ref[k * SUBLANES:(k + 1) * SUBLANES, :], axis=0, keepdims=True)
                for j in range(N_DEV):
                    dw_ref[j, k:k + 1, :] = row_k[:, j * LANES:(j + 1) * LANES]

    vec = pl.BlockSpec((1, C), lambda i: (0, 0))
    row = pl.BlockSpec((tm, C), lambda i: (i, 0))
    nxt = pl.BlockSpec((CONV_HALO, C), lambda i: (jnp.minimum((i + 1) * hb, last_halo), 0))
    return pl.pallas_call(
        body, name=name, grid=(n_tiles,),
        in_specs=[
            pl.BlockSpec((tm, C), lambda i: (i, PB_A)),
            pl.BlockSpec((tm, C), lambda i: (i, PB_GATE)),
            pl.BlockSpec((CONV_HALO, C), lambda i: (jnp.maximum(i * hb - 1, 0), PB_A)),
            pl.BlockSpec((CONV_HALO, C), lambda i: (jnp.maximum(i * hb - 1, 0), PB_GATE)),
            row, row, nxt, nxt,
            pl.BlockSpec((N_DEV, CONV_WIDTH, LANES), lambda i: (0, 0, 0)),
            vec, vec,
        ],
        out_specs=[
            pl.BlockSpec((tm, 2 * C), lambda i: (i, 0)),
            pl.BlockSpec((N_DEV, CONV_HALO, LANES), lambda i: (0, 0, 0)),
            pl.BlockSpec((8, C), lambda i: (0, 0)),
        ],
        out_shape=[
            jax.ShapeDtypeStruct((S, 2 * C), BF16),
            jax.ShapeDtypeStruct((N_DEV, CONV_HALO, LANES), F32),
            jax.ShapeDtypeStruct((8, C), F32),
        ],
        scratch_shapes=[
            pltpu.VMEM((tm + CONV_HALO, C), F32), pltpu.VMEM((tm + CONV_HALO, C), F32),
            pltpu.VMEM((SUBLANES - 1, tm + CONV_HALO, C), F32), pltpu.VMEM((SUBLANES - 1, tm + CONV_HALO, C), F32),
            pltpu.VMEM((CONV_HALO * SUBLANES, C), F32), pltpu.VMEM((CONV_HALO * SUBLANES, C), F32),
        ],
        compiler_params=_cparams(("arbitrary",), 56),
    )(P, P, P, P, do, y, do, y, w_pad, ln_g, ln_b)


def _mem_softmax(qh, kh):
    s = lax.dot_general(qh, kh, NT_DIMS, preferred_element_type=F32)
    e = jnp.exp(s - jnp.max(s, axis=1, keepdims=True))
    return e / jnp.sum(e, axis=1, keepdims=True)


def _mem_fwd(P, kv, *, name, tm=1024):
    S, M = P.shape[0], kv.shape[0]
    tm = _tile(S, tm)
    scale = MEM_HEAD_DIM ** -0.5

    def body(q_ref, k_ref, v_ref, o_ref):
        for h in range(MEM_HEADS):
            sl = slice(h * MEM_HEAD_DIM, (h + 1) * MEM_HEAD_DIM)
            qh = q_ref[:, sl] * jnp.asarray(scale, BF16)
            p = _mem_softmax(qh, k_ref[:, sl])
            o_ref[:, sl] = jnp.dot(p.astype(BF16), v_ref[:, sl], preferred_element_type=F32).astype(o_ref.dtype)

    return pl.pallas_call(
        body, name=name, grid=(S // tm,),
        in_specs=[
            pl.BlockSpec((tm, D_MODEL), lambda i: (i, PB_QMEM)),
            pl.BlockSpec((M, D_MODEL), lambda i: (0, 0)),
            pl.BlockSpec((M, D_MODEL), lambda i: (0, 1)),
        ],
        out_specs=pl.BlockSpec((tm, D_MODEL), lambda i: (i, 0)),
        out_shape=jax.ShapeDtypeStruct((S, D_MODEL), BF16),
        compiler_params=_cparams(("parallel",)),
    )(P, kv, kv)


def _mem_bwd(P, do, kv, *, name, tm=1024):
    S, M = P.shape[0], kv.shape[0]
    tm = _tile(S, tm)
    scale = MEM_HEAD_DIM ** -0.5

    def body(q_ref, k_ref, v_ref, do_ref, dq_ref, dkv_ref):
        i = pl.program_id(0)

        @pl.when(i == 0)
        def _():
            dkv_ref[...] = jnp.zeros_like(dkv_ref)

        for h in range(MEM_HEADS):
            sl = slice(h * MEM_HEAD_DIM, (h + 1) * MEM_HEAD_DIM)
            slv = slice(D_MODEL + h * MEM_HEAD_DIM, D_MODEL + (h + 1) * MEM_HEAD_DIM)
            qh = q_ref[:, sl] * jnp.asarray(scale, BF16)
            kh, vh, doh = k_ref[:, sl], v_ref[:, sl], do_ref[:, sl]
            p = _mem_softmax(qh, kh)
            dp = lax.dot_general(doh, vh, NT_DIMS, preferred_element_type=F32)
            ds = p * (dp - jnp.sum(p * dp, axis=1, keepdims=True))
            dsb = ds.astype(BF16)
            dq_ref[:, sl] = (jnp.dot(dsb, kh, preferred_element_type=F32) * scale).astype(dq_ref.dtype)
            dkv_ref[:, sl] += lax.dot_general(dsb, qh, TN_DIMS, preferred_element_type=F32)
            dkv_ref[:, slv] += lax.dot_general(p.astype(BF16), doh, TN_DIMS, preferred_element_type=F32)

    row = pl.BlockSpec((tm, D_MODEL), lambda i: (i, 0))
    return pl.pallas_call(
        body, name=name, grid=(S // tm,),
        in_specs=[
            pl.BlockSpec((tm, D_MODEL), lambda i: (i, PB_QMEM)),
            pl.BlockSpec((M, D_MODEL), lambda i: (0, 0)),
            pl.BlockSpec((M, D_MODEL), lambda i: (0, 1)),
            row,
        ],
        out_specs=[row, pl.BlockSpec((M, 2 * D_MODEL), lambda i: (0, 0))],
        out_shape=[jax.ShapeDtypeStruct((S, D_MODEL), BF16), jax.ShapeDtypeStruct((M, 2 * D_MODEL), F32)],
        compiler_params=_cparams(("arbitrary",)),
    )(P, kv, kv, do)


def _merge_fwd(o_attn, o_conv, o_mem, P, wpa, wpc, wpm, w_out, x, g_post, g_pre, *, name, tm=256):
    S = P.shape[0]
    D = D_MODEL
    tm = _tile(S, tm)

    def body(oa_ref, oc_ref, om_ref, g0_ref, g1_ref, g2_ref, wa_ref, wc_ref, wm_ref, wo_ref, x_ref, gp_ref, gq_ref,
             mg_ref, y_ref, pb_ref, x1_ref, h2_ref):
        merged = jnp.zeros((tm, D), F32)
        for b, (o_ref, g_ref, w_ref) in enumerate(((oa_ref, g0_ref, wa_ref), (oc_ref, g1_ref, wc_ref), (om_ref, g2_ref, wm_ref))):
            pb = jnp.dot(o_ref[...], w_ref[...], preferred_element_type=F32)
            pb_ref[b] = pb.astype(pb_ref.dtype)
            merged = merged + _sigmoid(g_ref[...].astype(F32)) * pb
        mb = merged.astype(BF16)
        mg_ref[...] = mb
        yv = jnp.dot(mb, wo_ref[...], preferred_element_type=F32)
        y_ref[...] = yv
        x1 = x_ref[...] + yv * lax.rsqrt(jnp.mean(yv * yv, axis=-1, keepdims=True) + RMS_EPS) * gp_ref[...]
        x1_ref[...] = x1
        h2_ref[...] = (x1 * lax.rsqrt(jnp.mean(x1 * x1, axis=-1, keepdims=True) + RMS_EPS) * gq_ref[...]).astype(h2_ref.dtype)

    row = pl.BlockSpec((tm, D), lambda i: (i, 0))
    vec = pl.BlockSpec((1, D), lambda i: (0, 0))
    wsp = pl.BlockSpec((D, D), lambda i: (0, 0))
    return pl.pallas_call(
        body, name=name, grid=(S // tm,),
        in_specs=[row, row, row] + [pl.BlockSpec((tm, D), lambda i, b=b: (i, PB_G0 + b)) for b in range(3)] + [wsp] * 4
        + [row, vec, vec],
        out_specs=[row, row, pl.BlockSpec((3, tm, D), lambda i: (0, i, 0)), row, row],
        out_shape=[jax.ShapeDtypeStruct((S, D), BF16), jax.ShapeDtypeStruct((S, D), F32), jax.ShapeDtypeStruct((3, S, D), BF16),
                   jax.ShapeDtypeStruct((S, D), F32), jax.ShapeDtypeStruct((S, D), BF16)],
        compiler_params=_cparams(("parallel",), 56),
    )(o_attn, o_conv, o_mem, P, P, P, wpa, wpc, wpm, w_out, x, g_post, g_pre)


def _merge_bwd(dy1, w_out, P, pb, *, name, tm=256):
    S = P.shape[0]
    D = D_MODEL
    tm = _tile(S, tm)

    def body(dy_ref, wo_ref, g0_ref, g1_ref, g2_ref, pb_ref, d0_ref, d1_ref, d2_ref, dgl_ref):
        dm = lax.dot_general(dy_ref[...], wo_ref[...], NT_DIMS, preferred_element_type=F32)
        for b, (g_ref, d_ref) in enumerate(((g0_ref, d0_ref), (g1_ref, d1_ref), (g2_ref, d2_ref))):
            g = _sigmoid(g_ref[...].astype(F32))
            d_ref[...] = (dm * g).astype(d_ref.dtype)
            dgl_ref[:, b * D:(b + 1) * D] = (dm * pb_ref[b].astype(F32) * g * (1.0 - g)).astype(dgl_ref.dtype)

    row = pl.BlockSpec((tm, D), lambda i: (i, 0))
    blk3 = pl.BlockSpec((3, tm, D), lambda i: (0, i, 0))
    sds = jax.ShapeDtypeStruct((S, D), BF16)
    return pl.pallas_call(
        body, name=name, grid=(S // tm,),
        in_specs=[row, pl.BlockSpec((D, D), lambda i: (0, 0))]
        + [pl.BlockSpec((tm, D), lambda i, b=b: (i, PB_G0 + b)) for b in range(3)] + [blk3],
        out_specs=[row, row, row, pl.BlockSpec((tm, 3 * D), lambda i: (i, 0))],
        out_shape=[sds, sds, sds, jax.ShapeDtypeStruct((S, 3 * D), BF16)],
        compiler_params=_cparams(("parallel",), 56),
    )(dy1, w_out, P, P, P, pb)


def _swiglu_fwd(h2, w_gu_t, *, name, tm=512, tn=1408):
    S, D = h2.shape
    Fh = w_gu_t.shape[0] // 2
    tm, tn = _tile(S, tm), _tile(Fh, tn)
    nj = Fh // tn

    def body(h_ref, wg_ref, wu_ref, g_ref, u_ref, a_ref):
        hv = h_ref[...]
        g = lax.dot_general(hv, wg_ref[...], NT_DIMS, preferred_element_type=F32)
        u = lax.dot_general(hv, wu_ref[...], NT_DIMS, preferred_element_type=F32)
        g_ref[...] = g.astype(g_ref.dtype)
        u_ref[...] = u.astype(u_ref.dtype)
        a_ref[...] = (g * _sigmoid(g) * u).astype(a_ref.dtype)

    out = pl.BlockSpec((tm, tn), lambda j, i: (i, j))
    sds = jax.ShapeDtypeStruct((S, Fh), BF16)
    return pl.pallas_call(
        body, name=name, grid=(nj, S // tm),
        in_specs=[
            pl.BlockSpec((tm, D), lambda j, i: (i, 0)),
            pl.BlockSpec((tn, D), lambda j, i: (j, 0)),
            pl.BlockSpec((tn, D), lambda j, i: (j + nj, 0)),
        ],
        out_specs=[out, out, out], out_shape=[sds, sds, sds],
        compiler_params=_cparams(("parallel", "parallel"), 56),
    )(h2, w_gu_t, w_gu_t)


def _swiglu_bwd(dffn, w_down, g, u, *, name, tm=512, tn=1408):
    S, Fh = g.shape
    D = dffn.shape[1]
    tm, tn = _tile(S, tm), _tile(Fh, tn)

    def body(df_ref, w_ref, g_ref, u_ref, o_ref):
        dfv = df_ref[...]
        for j in range(Fh // tn):
            cols = slice(j * tn, (j + 1) * tn)
            da = lax.dot_general(dfv, w_ref[cols, :], NT_DIMS, preferred_element_type=F32)
            gv = g_ref[:, cols].astype(F32)
            sg = _sigmoid(gv)
            o_ref[:, cols] = (da * u_ref[:, cols].astype(F32) * (sg * (1.0 + gv * (1.0 - sg)))).astype(o_ref.dtype)
            o_ref[:, Fh + j * tn:Fh + (j + 1) * tn] = (da * gv * sg).astype(o_ref.dtype)

    act = pl.BlockSpec((tm, Fh), lambda i: (i, 0))
    return pl.pallas_call(
        body, name=name, grid=(S // tm,),
        in_specs=[pl.BlockSpec((tm, D), lambda i: (i, 0)), pl.BlockSpec((Fh, D), lambda i: (0, 0)), act, act],
        out_specs=pl.BlockSpec((tm, 2 * Fh), lambda i: (i, 0)),
        out_shape=jax.ShapeDtypeStruct((S, 2 * Fh), BF16),
        compiler_params=_cparams(("parallel",), 56),
    )(dffn, w_down, g, u)


def _final(x1, ffn, target, g, *, name, tm=512):
    S, D = x1.shape
    tm = _tile(S, tm)

    def body(x_ref, f_ref, t_ref, g_ref, dout_ref, dffn_ref, loss_ref, dg_ref):
        i = pl.program_id(0)
        fv = f_ref[...]
        rstd = lax.rsqrt(jnp.mean(fv * fv, axis=-1, keepdims=True) + RMS_EPS)
        r = fv * rstd
        e = x_ref[...] + r * g_ref[...] - t_ref[...]
        dout = e * (1.0 / D)
        dout_ref[...] = dout
        gy = dout * g_ref[...]
        dffn_ref[...] = (rstd * (gy - r * jnp.mean(gy * r, axis=-1, keepdims=True))).astype(dffn_ref.dtype)
        lpart = jnp.full((8, LANES), 0.5 * jnp.sum(jnp.mean(e * e, axis=-1, keepdims=True)), F32)
        gpart = jnp.sum(dout * r, axis=0, keepdims=True)

        @pl.when(i == 0)
        def _():
            loss_ref[...] = lpart
            dg_ref[...] = gpart

        @pl.when(i > 0)
        def _():
            loss_ref[...] += lpart
            dg_ref[...] += gpart

    row = pl.BlockSpec((tm, D), lambda i: (i, 0))
    vec = pl.BlockSpec((1, D), lambda i: (0, 0))
    return pl.pallas_call(
        body, name=name, grid=(S // tm,),
        in_specs=[row, row, row, vec],
        out_specs=[row, row, pl.BlockSpec((8, LANES), lambda i: (0, 0)), vec],
        out_shape=[jax.ShapeDtypeStruct((S, D), F32), jax.ShapeDtypeStruct((S, D), BF16),
                   jax.ShapeDtypeStruct((8, LANES), F32), jax.ShapeDtypeStruct((1, D), F32)],
        compiler_params=_cparams(("arbitrary",)),
    )(x1, ffn, target, g)


def _after(token, value):
    return value if token is None else value + token[0, 0].astype(value.dtype)


def _local_step(x, mem, target, gains, w, late_weights, send_grads, *, tq=512):
    S = x.shape[0]
    b_pad = jnp.pad(gains["b_forget"], ((0, 0), (0, LANES - FOX_HEADS)))
    gains = dict(gains)

    h = _rms_fwd(x, gains["norm_mix_pre"], name="rms_mix_pre")
    P = _matmul(h, w["wp_t"], mode="nt", out_dtype=BF16, name="proj_in", tm=2048)
    f_logit = _matmul(h, w["wf_t"], mode="nt", out_dtype=F32, name="proj_forget")
    logf = _forget_fwd(f_logit, b_pad, name="forget_fwd")
    c_row16 = _cumsum_lanes([logf[:, :FOX_HEADS].T], reverse=False, name="forget_cumsum")
    crow = c_row16.reshape(HEAD_PAIRS, 2, S)
    o_attn, o_attn32, lse = _attn_fwd(P, crow, tq=tq, name="attn_fwd")
    w = dict(w, **late_weights(o_attn))
    o_conv, y_conv = _conv_fwd(P, w["conv_w"], gains["conv_b"], gains["conv_ln_g"], gains["conv_ln_b"], name="conv_fwd")
    mem_n = _rms_fwd(mem, gains["norm_mem"], name="rms_mem")
    kv = _matmul(mem_n, w["w_kv"], mode="nn", out_dtype=BF16, name="mem_kv")
    o_mem = _mem_fwd(P, kv, name="mem_fwd")
    merged, y1, pb, x1, h2 = _merge_fwd(o_attn, o_conv, o_mem, P, w["wpa"], w["wpc"], w["wpm"], w["w_out"], x,
                                        gains["norm_mix_post"], gains["norm_ffn_pre"], name="merge_fwd")
    g_ffn, u_ffn, act = _swiglu_fwd(h2, w["w_gu_t"], name="swiglu_fwd")
    ffn = _matmul(act, w["w_down"], mode="nn", out_dtype=F32, name="ffn_down", tk=1408)

    dout, dffn, loss_tile, d_norm_ffn_post = _final(x1, ffn, target, gains["norm_ffn_post"], name="loss_head")
    d_w_down = _matmul(act, dffn, mode="tn", out_dtype=BF16, name="dw_down", tm=1408)
    dgu = _swiglu_bwd(dffn, w["w_down"], g_ffn, u_ffn, name="swiglu_bwd")
    dh2 = _matmul(dgu, w["w_gu_t"], mode="nn", out_dtype=F32, name="d_h2", tk=1408)
    d_w_gu_t = _matmul(dgu, h2, mode="tn", out_dtype=BF16, name="dw_gate_up", tm=1408)
    dx1, dy1, d_norm_ffn_pre, d_norm_mix_post = _rms_bwd_pair(x1, dh2, dout, gains["norm_ffn_pre"], y1, gains["norm_mix_post"],
                                                              name="rms_ffn_pre_mix_post_bwd")

    d_w_out = _matmul(merged, dy1, mode="tn", out_dtype=BF16, name="dw_out")
    *dpb, dgl = _merge_bwd(dy1, w["w_out"], P, pb, name="merge_bwd")
    do_attn = _matmul(dpb[0], w["wpa"], mode="nt", out_dtype=BF16, name="d_o_attn")
    do_conv = _matmul(dpb[1], w["wpc"], mode="nt", out_dtype=F32, name="d_o_conv")
    do_mem = _matmul(dpb[2], w["wpm"], mode="nt", out_dtype=BF16, name="d_o_mem")
    d_wpa = _matmul(o_attn, dpb[0], mode="tn", out_dtype=BF16, name="dw_proj_attn")
    d_wpc = _matmul(o_conv, dpb[1], mode="tn", out_dtype=BF16, name="dw_proj_conv")
    d_wpm = _matmul(o_mem, dpb[2], mode="tn", out_dtype=BF16, name="dw_proj_mem")

    dq_mem, dkv = _mem_bwd(P, do_mem, kv, name="mem_bwd")
    dkv_b = dkv.astype(BF16)
    d_w_kv = _matmul(mem_n, dkv_b, mode="tn", out_dtype=BF16, name="dw_kv")
    dmem_n = _matmul(dkv_b, w["w_kv"], mode="nt", out_dtype=F32, name="d_mem_n")
    _, d_norm_mem = _rms_bwd(mem, [dmem_n], gains["norm_mem"], None, out_dtype=BF16, name="rms_mem_bwd")

    dglu, d_conv_w, conv_small = _conv_bwd(P, do_conv, y_conv, w["conv_w"], gains["conv_ln_g"], gains["conv_ln_b"], name="conv_bwd")
    sent = send_grads("body", dict(conv_w=d_conv_w[:, :CONV_WIDTH].astype(BF16), w_kv_mem=d_w_kv, w_proj_attn=d_wpa,
                                   w_proj_conv=d_wpc, w_proj_mem=d_wpm, w_out=d_w_out, w_gate_up=d_w_gu_t, w_down=d_w_down))

    dq, dk, dv, dc, dcq = _attn_bwd(P, do_attn, o_attn32, lse, _after(sent, crow), tq=tq, name="attn_bwd")
    dlogf16 = _cumsum_lanes([dc.reshape(FOX_HEADS, S), dcq.reshape(FOX_HEADS, S)], reverse=True, name="forget_cumsum_bwd")
    dlogf = jnp.pad(dlogf16.T, ((0, 0), (0, LANES - FOX_HEADS)))
    df, d_b_pad = _forget_bwd(f_logit, b_pad, dlogf, name="forget_bwd")

    d_proj = [dq, dk, dv, dglu, dq_mem, dgl]
    d_wp_t = _matmul_pieces(d_proj, h, mode="tn", out_dtype=BF16, name="dw_in")
    d_wf_t = _matmul(df, h, mode="tn", out_dtype=BF16, name="dw_forget")
    n_qkv = 3 * D_MODEL
    pieces = [(d_wp_t[:n_qkv], n_qkv), (d_wf_t[:FOX_HEADS], FOX_HEADS), (d_wp_t[n_qkv:], P_WIDTH - n_qkv)]
    n_own = (P_WIDTH + FOX_HEADS) // N_DEV
    blocks = [jnp.concatenate(_rows(pieces, j * n_own, (j + 1) * n_own), axis=0) for j in range(N_DEV)]
    sent = send_grads("w_in", dict(w_in=jnp.stack(blocks)))
    dh_p = _matmul_pieces(d_proj, w["wp_t"], mode="nn", out_dtype=F32, name="d_h", after=sent)
    dh_f = _matmul(df, w["wf_t"], mode="nn", out_dtype=F32, name="d_h_forget")
    grad_x, d_norm_mix_pre = _rms_bwd(x, [dh_p, dh_f], gains["norm_mix_pre"], dx1, out_dtype=F32, name="rms_mix_pre_bwd")

    small = dict(norm_mix_pre=d_norm_mix_pre, norm_mix_post=d_norm_mix_post, norm_mem=d_norm_mem,
                 conv_b=conv_small[2:3], conv_ln_g=conv_small[0:1], conv_ln_b=conv_small[1:2],
                 norm_ffn_pre=d_norm_ffn_pre, norm_ffn_post=d_norm_ffn_post, b_forget=d_b_pad[:, :FOX_HEADS])
    return loss_tile, grad_x, small


def _mesh_pos():
    return lax.axis_index("x"), lax.axis_index("y"), lax.axis_index("c")


def _flip(pos, d):
    x, y, c = pos
    return (1 - x if d & 4 else x, 1 - y if d & 2 else y, 1 - c if d & 1 else c)


def _flat(pos):
    x, y, c = pos
    return 4 * x + 2 * y + c


def _gather_two_level(src, *, name):
    def body(src_ref, out_ref, token, send_sems, recv_sems, local_sem):
        me = _mesh_pos()
        my = _flat(me)
        sibling = _flip(me, 1)
        far = [_flip(me, d) for d in (4, 2, 6)]

        def copy(k, block, to, own=False):
            return pltpu.make_async_remote_copy(
                src_ref=src_ref if own else out_ref.at[_flat(block)], dst_ref=out_ref.at[_flat(block)],
                send_sem=send_sems.at[k], recv_sem=recv_sems.at[k], device_id=to, device_id_type=pl.DeviceIdType.MESH)

        local = pltpu.make_async_copy(src_ref, out_ref.at[my], local_sem)
        local.start()
        first = [copy(0, me, sibling, own=True)] + [copy(1 + j, me, peer, own=True) for j, peer in enumerate(far)]
        for cp in first:
            cp.start()
        passed = [copy(4 + j, peer, sibling) for j, peer in enumerate(far)]
        for j, peer in enumerate(far):
            copy(1 + j, peer, me).wait_recv()
            passed[j].start()
        copy(0, sibling, me).wait_recv()
        for j, peer in enumerate(far):
            copy(4 + j, _flip(peer, 1), me).wait_recv()
        for cp in first + passed:
            cp.wait_send()
        local.wait()
        token[...] = jnp.zeros_like(token)

    n_copy = N_DEV - 1
    out, token = pl.pallas_call(
        body, name=name,
        in_specs=[pl.BlockSpec(memory_space=pl.ANY)],
        out_specs=[pl.BlockSpec(memory_space=pl.ANY), pl.BlockSpec(memory_space=pltpu.VMEM)],
        out_shape=[jax.ShapeDtypeStruct((N_DEV,) + tuple(src.shape), src.dtype), jax.ShapeDtypeStruct((8, LANES), F32)],
        scratch_shapes=[pltpu.SemaphoreType.DMA((n_copy,)), pltpu.SemaphoreType.DMA((n_copy,)), pltpu.SemaphoreType.DMA(())],
    )(src)
    return [out], token


def _exchange(srcs, *, scatter, name):
    T = len(srcs)
    n_peer = N_DEV - 1

    def body(*refs):
        src_refs, out_refs = refs[:T], refs[T:2 * T]
        token, send_sems, recv_sems, local_sems = refs[2 * T:]
        me = _mesh_pos()
        my = _flat(me)

        def copy(t, d):
            peer = _flip(me, d)
            return pltpu.make_async_remote_copy(
                src_ref=src_refs[t].at[_flat(peer)] if scatter else src_refs[t],
                dst_ref=out_refs[t].at[my],
                send_sem=send_sems.at[t * n_peer + d - 1], recv_sem=recv_sems.at[t * n_peer + d - 1],
                device_id=peer, device_id_type=pl.DeviceIdType.MESH)

        def arrival(t, d):
            peer = _flip(me, d)
            return pltpu.make_async_remote_copy(
                src_ref=src_refs[t].at[my] if scatter else src_refs[t],
                dst_ref=out_refs[t].at[_flat(peer)],
                send_sem=send_sems.at[t * n_peer + d - 1], recv_sem=recv_sems.at[t * n_peer + d - 1],
                device_id=peer, device_id_type=pl.DeviceIdType.MESH)

        local = [pltpu.make_async_copy(src_refs[t].at[my] if scatter else src_refs[t], out_refs[t].at[my], local_sems.at[t])
                 for t in range(T)]
        for cp in local:
            cp.start()
        sends = [copy(t, d) for d in range(1, N_DEV) for t in range(T)]
        for cp in sends:
            cp.start()
        for d in range(1, N_DEV):
            for t in range(T):
                arrival(t, d).wait_recv()
        for cp in sends:
            cp.wait_send()
        for cp in local:
            cp.wait()
        token[...] = jnp.zeros_like(token)

    outs = pl.pallas_call(
        body, name=name,
        in_specs=[pl.BlockSpec(memory_space=pl.ANY)] * T,
        out_specs=[pl.BlockSpec(memory_space=pl.ANY)] * T + [pl.BlockSpec(memory_space=pltpu.VMEM)],
        out_shape=[jax.ShapeDtypeStruct((N_DEV,) + tuple(s.shape[-2:]), s.dtype) for s in srcs] + [jax.ShapeDtypeStruct((8, LANES), F32)],
        scratch_shapes=[pltpu.SemaphoreType.DMA((T * n_peer,)), pltpu.SemaphoreType.DMA((T * n_peer,)), pltpu.SemaphoreType.DMA((T,))],
    )(*srcs)
    return list(outs[:T]), outs[T]


HBM_SPEC = pl.BlockSpec(memory_space=pltpu.HBM)
SEM_SPEC = pl.BlockSpec(memory_space=pltpu.SEMAPHORE)


def _split_copies(src_refs, land_refs, send_sems, recv_sems, scatter):
    me = _mesh_pos()
    my = _flat(me)
    n_peer = N_DEV - 1
    out = []
    for d in range(1, N_DEV):
        peer = _flip(me, d)
        for t, (src, land) in enumerate(zip(src_refs, land_refs)):
            k = t * n_peer + d - 1
            start = pltpu.make_async_remote_copy(
                src_ref=src.at[_flat(peer)] if scatter else src, dst_ref=land.at[my],
                send_sem=send_sems.at[k], recv_sem=recv_sems.at[k], device_id=peer, device_id_type=pl.DeviceIdType.MESH)
            arrive = pltpu.make_async_remote_copy(
                src_ref=src.at[my] if scatter else src, dst_ref=land.at[_flat(peer)],
                send_sem=send_sems.at[k], recv_sem=recv_sems.at[k], device_id=peer, device_id_type=pl.DeviceIdType.MESH)
            out.append((start, arrive))
    return out


def _local_copies(src_refs, land_refs, local_sems, scatter):
    my = _flat(_mesh_pos())
    return [pltpu.make_async_copy(src.at[my] if scatter else src, land.at[my], local_sems.at[t])
            for t, (src, land) in enumerate(zip(src_refs, land_refs))]


def _exchange_start(srcs, *, scatter, name):
    T = len(srcs)
    n_sem = T * (N_DEV - 1)
    lands = [lax.empty((N_DEV,) + tuple(s.shape[-2:]), s.dtype) for s in srcs]

    def body(*refs):
        src_refs, land_refs = refs[:T], refs[T:2 * T]
        send_sems, recv_sems, local_sems = refs[2 * T:2 * T + 3]
        token = refs[-1]
        for cp in _local_copies(src_refs, land_refs, local_sems, scatter):
            cp.start()
        for start, _ in _split_copies(src_refs, land_refs, send_sems, recv_sems, scatter):
            start.start()
        token[...] = jnp.zeros_like(token)

    hbm = lambda a: pltpu.HBM(a.shape, a.dtype)
    outs = pl.pallas_call(
        body, name=name,
        in_specs=[HBM_SPEC] * (2 * T),
        out_specs=(SEM_SPEC, SEM_SPEC, SEM_SPEC, *[HBM_SPEC] * (2 * T), pl.BlockSpec(memory_space=pltpu.VMEM)),
        out_shape=(pltpu.SemaphoreType.DMA((n_sem,)), pltpu.SemaphoreType.DMA((n_sem,)), pltpu.SemaphoreType.DMA((T,)),
                   *[hbm(s) for s in srcs], *[hbm(a) for a in lands], jax.ShapeDtypeStruct((8, LANES), F32)),
        input_output_aliases={t: 3 + t for t in range(2 * T)},
        compiler_params=pltpu.CompilerParams(has_side_effects=pltpu.SideEffectType.DATAFLOW_SIDE_EFFECTING),
    )(*[pltpu.with_memory_space_constraint(s, pltpu.HBM) for s in srcs],
      *[pltpu.with_memory_space_constraint(a, pltpu.HBM) for a in lands])
    return (outs[0], outs[1], outs[2], list(outs[3:3 + T]), list(outs[3 + T:3 + 2 * T])), outs[-1]


def _exchange_wait(handle, after, *, scatter, name):
    send_sems, recv_sems, local_sems, srcs, lands = handle
    T = len(srcs)

    def body(*refs):
        src_refs, land_refs = refs[:T], refs[T:2 * T]
        s_sems, r_sems, l_sems = refs[2 * T:2 * T + 3]
        for cp in _local_copies(src_refs, land_refs, l_sems, scatter):
            cp.wait()
        for start, arrive in _split_copies(src_refs, land_refs, s_sems, r_sems, scatter):
            start.wait_send()
            arrive.wait_recv()

    hbm = lambda a: pltpu.HBM(a.shape, a.dtype)
    outs = pl.pallas_call(
        body, name=name,
        in_specs=[HBM_SPEC] * (2 * T) + [SEM_SPEC, SEM_SPEC, SEM_SPEC, pl.BlockSpec(memory_space=pl.ANY)],
        out_specs=[HBM_SPEC] * (2 * T),
        out_shape=[hbm(s) for s in srcs] + [hbm(a) for a in lands],
        input_output_aliases={t: t for t in range(2 * T)},
        compiler_params=pltpu.CompilerParams(has_side_effects=pltpu.SideEffectType.DATAFLOW_SIDE_EFFECTING),
    )(*srcs, *lands, send_sems, recv_sems, local_sems, after)
    return list(outs[T:])


def _adamw(w, slots, m, v, *, name):
    R, C = w.shape
    row_tiles = [t for t in range(16, 513, 16) if R % t == 0]
    tr = R if R <= 512 or not row_tiles else max(row_tiles)
    tc = C if tr < R or R <= 512 else LANES
    assert R % tr == 0 and C % tc == 0, (R, C, tr, tc)

    def body(w_ref, s_ref, m_ref, v_ref, g_ref, d_ref, m2_ref, v2_ref):
        gv = s_ref[0].astype(F32)
        for j in range(1, N_DEV):
            gv = gv + s_ref[j].astype(F32)
        g_ref[...] = gv
        m2 = ADAM_B1 * m_ref[...] + (1.0 - ADAM_B1) * gv
        v2 = ADAM_B2 * v_ref[...] + (1.0 - ADAM_B2) * (gv * gv)
        m_hat = m2 / (1.0 - ADAM_B1 ** ADAM_STEP)
        v_hat = v2 / (1.0 - ADAM_B2 ** ADAM_STEP)
        d_ref[...] = -ADAM_LR * (m_hat / (jnp.sqrt(v_hat) + ADAM_EPS) + ADAM_WD * w_ref[...])
        m2_ref[...] = m2
        v2_ref[...] = v2

    blk = pl.BlockSpec((tr, tc), lambda i, j: (i, j))
    sds = jax.ShapeDtypeStruct((R, C), F32)
    return pl.pallas_call(
        body, name=name, grid=(R // tr, C // tc),
        in_specs=[blk, pl.BlockSpec((N_DEV, tr, tc), lambda i, j: (0, i, j)), blk, blk],
        out_specs=[blk] * 4, out_shape=[sds] * 4,
        compiler_params=_cparams(("parallel", "parallel")),
    )(w, slots, m, v)


SHARDED = (
    ("w_in", (1154, 1024), "row"), ("conv_w", (31, 128), "col"), ("w_kv_mem", (1024, 256), "col"),
    ("w_proj_attn", (128, 1024), "row"), ("w_proj_conv", (128, 1024), "row"), ("w_proj_mem", (128, 1024), "row"),
    ("w_out", (128, 1024), "row"), ("w_gate_up", (704, 1024), "row"), ("w_down", (352, 1024), "row"),
)
TRANSPOSED = ("w_in", "w_gate_up")
SMALL = ("norm_mix_pre", "norm_mix_post", "norm_mem", "conv_b", "conv_ln_g", "conv_ln_b", "norm_ffn_pre", "norm_ffn_post", "b_forget")
SMALL_ROWS = 16
LOSS_ROW = len(SMALL)
WEIGHT_ORDER = ("norm_mix_pre", "norm_mix_post", "norm_mem", "w_in", "b_forget", "conv_w", "conv_b", "conv_ln_g", "conv_ln_b",
                "w_kv_mem", "w_proj_attn", "w_proj_conv", "w_proj_mem", "w_out", "norm_ffn_pre", "norm_ffn_post", "w_gate_up", "w_down")


def _rows(pieces, lo, hi):
    out, start = [], 0
    for a, n in pieces:
        a0, a1 = max(lo, start), min(hi, start + n)
        if a0 < a1:
            out.append(a[a0 - start:a1 - start])
        start += n
    return out


def _to_full(blocks8, kind):
    n, r, c = blocks8.shape
    if kind == "col":
        return jnp.concatenate([blocks8[j] for j in range(n)], axis=1)
    return blocks8.reshape(n * r, c)


def _to_blocks(full, kind):
    if kind == "col":
        c = full.shape[1] // N_DEV
        return jnp.stack([full[:, j * c:(j + 1) * c] for j in range(N_DEV)])
    nr, c = full.shape
    return full.reshape(N_DEV, nr // N_DEV, c)


def kernel(x, mem, norm_mix_pre, norm_mix_post, norm_mem, w_in, b_forget, conv_w, conv_b, conv_ln_g, conv_ln_b, w_kv_mem, w_proj_attn, w_proj_conv, w_proj_mem, w_out, norm_ffn_pre, norm_ffn_post, w_gate_up, w_down, loss_target, m_norm_mix_pre, m_norm_mix_post, m_norm_mem, m_w_in, m_b_forget, m_conv_w, m_conv_b, m_conv_ln_g, m_conv_ln_b, m_w_kv_mem, m_w_proj_attn, m_w_proj_conv, m_w_proj_mem, m_w_out, m_norm_ffn_pre, m_norm_ffn_post, m_w_gate_up, m_w_down, v_norm_mix_pre, v_norm_mix_post, v_norm_mem, v_w_in, v_b_forget, v_conv_w, v_conv_b, v_conv_ln_g, v_conv_ln_b, v_w_kv_mem, v_w_proj_attn, v_w_proj_conv, v_w_proj_mem, v_w_out, v_norm_ffn_pre, v_norm_ffn_post, v_w_gate_up, v_w_down):
    given = dict(locals())
    weights = {n: given[n] for n in WEIGHT_ORDER}
    moments_m = {n: given["m_" + n] for n in WEIGHT_ORDER}
    moments_v = {n: given["v_" + n] for n in WEIGHT_ORDER}
    kind = {name: k for name, _, k in SHARDED}
    names = [name for name, _, _ in SHARDED]

    def local(a, n):
        return jnp.swapaxes(a[0], 0, 1) if n in TRANSPOSED else a[0]

    blocks = {n: local(weights[n], n).astype(F32 if n == "conv_w" else BF16) for n in names}
    late = [n for n in names if n != "w_in"]
    (w_in_blocks,), w_in_token = _gather_two_level(blocks["w_in"], name="gather_w_in")
    blocks["w_kv_mem"] = _after(w_in_token, blocks["w_kv_mem"])
    late_handle, late_token = _exchange_start([blocks[n] for n in late], scatter=False, name="gather_rest_start")
    n_qkv = 3 * D_MODEL
    w_in_t = _to_full(w_in_blocks, "row")
    w_early = dict(
        wp_t=jnp.concatenate([w_in_t[:n_qkv], w_in_t[n_qkv + FOX_HEADS:]], axis=0),
        wf_t=jnp.pad(w_in_t[n_qkv:n_qkv + FOX_HEADS], ((0, LANES - FOX_HEADS), (0, 0))),
    )

    def late_weights(after):
        got = dict(zip(late, _exchange_wait(late_handle, after, scatter=False, name="gather_rest_wait")))
        full = {n: _to_full(got[n], kind[n]) for n in late if n != "conv_w"}
        return dict(
            conv_w=got["conv_w"],
            w_kv=full["w_kv_mem"], wpa=full["w_proj_attn"], wpc=full["w_proj_conv"], wpm=full["w_proj_mem"],
            w_out=full["w_out"], w_gu_t=full["w_gate_up"], w_down=full["w_down"],
        )

    sent = []

    def send_grads(group, g):
        order = [n for n in names if n in g]
        per_owner = [g[n] if g[n].ndim == 3 else _to_blocks(g[n], kind[n]) for n in order]
        handle, token = _exchange_start(per_owner, scatter=True, name="scatter_" + group + "_start")
        sent.append((group, order, handle))
        return token

    gains = {n: weights[n] for n in SMALL}
    gains["norm_mix_pre"] = _after(late_token, gains["norm_mix_pre"])

    loss_tile, grad_x, small = _local_step(x[0], mem[0], loss_target[0], gains, w_early, late_weights, send_grads)

    g_recv = {}
    for group, order, handle in sent:
        g_recv.update(zip(order, _exchange_wait(handle, grad_x, scatter=True, name="scatter_" + group + "_wait")))

    rows = [jnp.pad(small[n], ((0, 0), (0, D_MODEL - small[n].shape[1]))) for n in SMALL]
    rows.append(jnp.broadcast_to(loss_tile[0:1, 0:1], (1, D_MODEL)))
    rows.append(jnp.zeros((SMALL_ROWS - len(rows), D_MODEL), F32))
    (small_recv,), _ = _exchange([jnp.concatenate(rows, axis=0)], scatter=False, name="gather_small")

    def slab_of(d, fill):
        rows = [jnp.pad(d[n], ((0, 0), (0, D_MODEL - d[n].shape[1])), constant_values=fill) for n in SMALL]
        rows.append(jnp.full((SMALL_ROWS - len(rows), D_MODEL), fill, F32))
        return jnp.concatenate(rows, axis=0)

    grads, delta, new_m, new_v = {}, {}, {}, {}
    sg, sd, sm, sv = _adamw(slab_of(weights, 0.0), small_recv, slab_of(moments_m, 0.0), slab_of(moments_v, 1.0), name="adamw_small")
    loss = sg[LOSS_ROW, 0]
    for i, n in enumerate(SMALL):
        c = weights[n].shape[1]
        grads[n], delta[n], new_m[n], new_v[n] = sg[i:i + 1, :c], sd[i:i + 1, :c], sm[i:i + 1, :c], sv[i:i + 1, :c]
    for n in names:
        outs = _adamw(local(weights[n], n), g_recv[n], local(moments_m[n], n), local(moments_v[n], n), name="adamw_" + n)
        grads[n], delta[n], new_m[n], new_v[n] = [(jnp.swapaxes(o, 0, 1) if n in TRANSPOSED else o)[None] for o in outs]

    return (loss, grad_x[None], *[grads[n] for n in WEIGHT_ORDER], *[delta[n] for n in WEIGHT_ORDER],
            *[new_m[n] for n in WEIGHT_ORDER], *[new_v[n] for n in WEIGHT_ORDER])
```

```python
import functools

import jax
import jax.numpy as jnp
from jax import lax
from jax.experimental import pallas as pl
from jax.experimental.pallas import tpu as pltpu

F32 = jnp.float32
BF16 = jnp.bfloat16

D_MODEL = 1024
N_DEV = 8
FOX_HEADS = 16
FOX_HEAD_DIM = 64
HEAD_PAIRS = FOX_HEADS // 2
MEM_HEADS = 4
MEM_HEAD_DIM = D_MODEL // MEM_HEADS
CONV_WIDTH = 31
CONV_HALO = 32
CONV_CHUNK = 16
SUBLANES = 8
FFN_HIDDEN = 2816
RMS_EPS = 1e-6
LN_EPS = 1e-5
ADAM_LR = 0.001
ADAM_B1 = 0.9
ADAM_B2 = 0.999
ADAM_EPS = 1e-08
ADAM_WD = 0.01
ADAM_STEP = 10
KV_STEP = 2
KV_STEP_FWD = 4
NEG_BIG = -1e30
LANES = 128

PB_Q, PB_K, PB_V, PB_A, PB_GATE, PB_QMEM, PB_G0 = 0, 1, 2, 3, 4, 5, 6
P_WIDTH = 9 * D_MODEL

NT_DIMS = (((1,), (1,)), ((), ()))
TN_DIMS = (((0,), (0,)), ((), ()))


def _cparams(sem, vmem_mb=None):
    kw = dict(dimension_semantics=sem)
    if vmem_mb is not None:
        kw["vmem_limit_bytes"] = vmem_mb * 1024 * 1024
    return pltpu.CompilerParams(**kw)


def _tile(dim, want):
    t = min(dim, want)
    assert dim % t == 0, (dim, want)
    return t


def _sigmoid(z):
    return 1.0 / (1.0 + jnp.exp(-z))


def _matmul(a, b, *, mode, out_dtype, name, tm=1024, tn=1024, tk=1024, after=None):
    if mode == "nn":
        (M, K), (K2, N) = a.shape, b.shape
    elif mode == "nt":
        (M, K), (N, K2) = a.shape, b.shape
    else:
        (K, M), (K2, N) = a.shape, b.shape
    assert K == K2, (a.shape, b.shape, mode)
    tm, tn, tk = _tile(M, tm), _tile(N, tn), _tile(K, tk)
    nk = K // tk
    dims = {"nn": (((1,), (0,)), ((), ())), "nt": NT_DIMS, "tn": TN_DIMS}[mode]

    n_extra = 0 if after is None else 1

    def body(a_ref, b_ref, *rest):
        o_ref, scratch = rest[n_extra], rest[n_extra + 1:]
        part = lax.dot_general(a_ref[...], b_ref[...], dims, preferred_element_type=F32)
        if nk == 1:
            o_ref[...] = part.astype(o_ref.dtype)
        else:
            acc_ref, = scratch
            k = pl.program_id(2)

            @pl.when(k == 0)
            def _():
                acc_ref[...] = part

            @pl.when(k > 0)
            def _():
                acc_ref[...] += part

            @pl.when(k == nk - 1)
            def _():
                o_ref[...] = acc_ref[...].astype(o_ref.dtype)

    a_spec = pl.BlockSpec((tk, tm), lambda j, i, k: (k, i)) if mode == "tn" else pl.BlockSpec((tm, tk), lambda j, i, k: (i, k))
    b_spec = pl.BlockSpec((tn, tk), lambda j, i, k: (j, k)) if mode == "nt" else pl.BlockSpec((tk, tn), lambda j, i, k: (k, j))
    return pl.pallas_call(
        body,
        name=name,
        grid=(N // tn, M // tm, nk),
        in_specs=[a_spec, b_spec] + [pl.BlockSpec((8, LANES), lambda j, i, k: (0, 0))] * n_extra,
        out_specs=pl.BlockSpec((tm, tn), lambda j, i, k: (i, j)),
        out_shape=jax.ShapeDtypeStruct((M, N), out_dtype),
        scratch_shapes=[pltpu.VMEM((tm, tn), F32)] if nk > 1 else [],
        compiler_params=_cparams(("parallel", "parallel", "arbitrary"), 56),
    )(a, b, *([] if after is None else [after]))


def _matmul_pieces(pieces, b, *, mode, out_dtype, name, after=None):
    blk = 1024
    T = len(pieces)
    S = pieces[0].shape[0]
    counts = [p.shape[1] // blk for p in pieces]
    offs = [sum(counts[:t]) for t in range(T)]
    n_a = sum(counts)
    N = b.shape[1]
    n_extra = 0 if after is None else 1
    if mode == "nn":
        tm = _tile(S, blk)
        grid, n_red = (S // tm, n_a), n_a
        which = lambda i, k: k
        a_specs = [pl.BlockSpec((tm, blk), lambda i, k, o=o, c=c: (i, jnp.clip(k - o, 0, c - 1))) for o, c in zip(offs, counts)]
        b_spec = pl.BlockSpec((blk, N), lambda i, k: (k, 0))
        out_spec, out_rows, dims = pl.BlockSpec((tm, N), lambda i, k: (i, 0)), S, (((1,), (0,)), ((), ()))
    else:
        tk = _tile(S, blk)
        grid, n_red = (n_a, S // tk), S // tk
        which = lambda m, k: m

        def a_index(m, k, o, c):
            mine = (m >= o) & (m < o + c)
            return jnp.where(mine, k, 0), jnp.clip(m - o, 0, c - 1)

        a_specs = [pl.BlockSpec((tk, blk), functools.partial(a_index, o=o, c=c)) for o, c in zip(offs, counts)]
        b_spec = pl.BlockSpec((tk, N), lambda m, k: (k, 0))
        out_spec, out_rows, dims = pl.BlockSpec((blk, N), lambda m, k: (m, 0)), n_a * blk, TN_DIMS

    def body(*refs):
        a_refs, b_ref = refs[:T], refs[T]
        o_ref, acc_ref = refs[T + 1 + n_extra], refs[T + 2 + n_extra]
        a_blk = which(pl.program_id(0), pl.program_id(1))
        k = pl.program_id(1)
        for t in range(T):
            @pl.when((a_blk >= offs[t]) & (a_blk < offs[t] + counts[t]))
            def _(t=t):
                part = lax.dot_general(a_refs[t][...], b_ref[...], dims, preferred_element_type=F32)

                @pl.when(k == 0)
                def _():
                    acc_ref[...] = part

                @pl.when(k > 0)
                def _():
                    acc_ref[...] += part

        @pl.when(k == n_red - 1)
        def _():
            o_ref[...] = acc_ref[...].astype(o_ref.dtype)

    return pl.pallas_call(
        body, name=name, grid=grid,
        in_specs=a_specs + [b_spec] + [pl.BlockSpec((8, LANES), lambda i, k: (0, 0))] * n_extra,
        out_specs=out_spec,
        out_shape=jax.ShapeDtypeStruct((out_rows, N), out_dtype),
        scratch_shapes=[pltpu.VMEM(out_spec.block_shape, F32)],
        compiler_params=_cparams(("parallel", "arbitrary"), 56),
    )(*pieces, b, *([] if after is None else [after]))


def _rms_fwd(x, g, *, name, tm=512):
    S, D = x.shape
    tm = _tile(S, tm)

    def body(x_ref, g_ref, o_ref):
        xv = x_ref[...]
        rstd = lax.rsqrt(jnp.mean(xv * xv, axis=-1, keepdims=True) + RMS_EPS)
        o_ref[...] = (xv * rstd * g_ref[...]).astype(o_ref.dtype)

    return pl.pallas_call(
        body, name=name, grid=(S // tm,),
        in_specs=[pl.BlockSpec((tm, D), lambda i: (i, 0)), pl.BlockSpec((1, D), lambda i: (0, 0))],
        out_specs=pl.BlockSpec((tm, D), lambda i: (i, 0)),
        out_shape=jax.ShapeDtypeStruct((S, D), BF16),
        compiler_params=_cparams(("parallel",)),
    )(x, g)


def _rms_bwd(xin, dys, g, res, *, out_dtype, name, tm=512, proj=None):
    S, D = xin.shape
    tm = _tile(S, tm)
    n_dy = len(dys)
    has_res = res is not None
    n_proj = 0 if proj is None else 2

    def body(*refs):
        x_ref, g_ref = refs[0], refs[1]
        dy_refs = refs[2:2 + n_dy]
        pos = 2 + n_dy
        res_ref = refs[pos] if has_res else None
        pos += int(has_res)
        proj_refs = refs[pos:pos + n_proj]
        pos += n_proj
        dx_ref, dg_ref = refs[pos], refs[pos + 1]
        i = pl.program_id(0)
        xv = x_ref[...]
        dy = dy_refs[0][...].astype(F32)
        for r in dy_refs[1:]:
            dy = dy + r[...].astype(F32)
        if n_proj:
            dy = dy + jnp.dot(proj_refs[0][...], proj_refs[1][...], preferred_element_type=F32)
        rstd = lax.rsqrt(jnp.mean(xv * xv, axis=-1, keepdims=True) + RMS_EPS)
        xhat = xv * rstd
        gy = dy * g_ref[...]
        dx = rstd * (gy - xhat * jnp.mean(gy * xhat, axis=-1, keepdims=True))
        if has_res:
            dx = dx + res_ref[...]
        dx_ref[...] = dx.astype(dx_ref.dtype)
        part = jnp.sum(dy * xhat, axis=0, keepdims=True)

        @pl.when(i == 0)
        def _():
            dg_ref[...] = part

        @pl.when(i > 0)
        def _():
            dg_ref[...] += part

    row = pl.BlockSpec((tm, D), lambda i: (i, 0))
    vec = pl.BlockSpec((1, D), lambda i: (0, 0))
    ins = [xin, g] + list(dys) + ([res] if has_res else []) + (list(proj) if n_proj else [])
    proj_specs = [] if not n_proj else [pl.BlockSpec((tm, proj[0].shape[1]), lambda i: (i, 0)),
                                        pl.BlockSpec(proj[1].shape, lambda i: (0, 0))]
    return pl.pallas_call(
        body, name=name, grid=(S // tm,),
        in_specs=[row, vec] + [row] * n_dy + ([row] if has_res else []) + proj_specs,
        out_specs=[row, vec],
        out_shape=[jax.ShapeDtypeStruct((S, D), out_dtype), jax.ShapeDtypeStruct((1, D), F32)],
        compiler_params=_cparams(("arbitrary",)),
    )(*ins)


def _rms_bwd_pair(x1, dh2, dout, g_pre, y1, g_post, *, name, tm=512):
    S, D = x1.shape
    tm = _tile(S, tm)

    def norm_bwd(xv, dy, g):
        rstd = lax.rsqrt(jnp.mean(xv * xv, axis=-1, keepdims=True) + RMS_EPS)
        xhat = xv * rstd
        gy = dy * g
        return rstd * (gy - xhat * jnp.mean(gy * xhat, axis=-1, keepdims=True)), jnp.sum(dy * xhat, axis=0, keepdims=True)

    def body(x_ref, dh_ref, do_ref, gq_ref, y_ref, gp_ref, dx_ref, dy_ref, dgq_ref, dgp_ref):
        i = pl.program_id(0)
        dx, part_q = norm_bwd(x_ref[...], dh_ref[...], gq_ref[...])
        dx1 = dx + do_ref[...]
        dx_ref[...] = dx1
        dy1, part_p = norm_bwd(y_ref[...], dx1, gp_ref[...])
        dy_ref[...] = dy1.astype(dy_ref.dtype)

        @pl.when(i == 0)
        def _():
            dgq_ref[...] = part_q
            dgp_ref[...] = part_p

        @pl.when(i > 0)
        def _():
            dgq_ref[...] += part_q
            dgp_ref[...] += part_p

    row = pl.BlockSpec((tm, D), lambda i: (i, 0))
    vec = pl.BlockSpec((1, D), lambda i: (0, 0))
    return pl.pallas_call(
        body, name=name, grid=(S // tm,),
        in_specs=[row, row, row, vec, row, vec], out_specs=[row, row, vec, vec],
        out_shape=[jax.ShapeDtypeStruct((S, D), F32), jax.ShapeDtypeStruct((S, D), BF16),
                   jax.ShapeDtypeStruct((1, D), F32), jax.ShapeDtypeStruct((1, D), F32)],
        compiler_params=_cparams(("arbitrary",)),
    )(x1, dh2, dout, g_pre, y1, g_post)


def _head_mask(hh, shape):
    lane = lax.broadcasted_iota(jnp.int32, shape, len(shape) - 1)
    return (lane // FOX_HEAD_DIM) == hh


def _block_constants(cr_ref, i, tq):
    first = cr_ref[0, :, pl.ds(pl.multiple_of(i * tq, tq), LANES)]
    return [first[hh:hh + 1, 0:1] for hh in range(2)]


def _attn_fwd(P, crow, *, tq, name):
    S = P.shape[0]
    tq = _tile(S, tq)
    nq = S // tq
    scale = FOX_HEAD_DIM ** -0.5

    def body(q_ref, k_ref, v_ref, cr_ref, o_ref, o32_ref, lse_ref):
        i = pl.program_id(1)
        q = q_ref[...] * jnp.asarray(scale, BF16)
        row = lax.broadcasted_iota(jnp.int32, (tq, tq), 0)
        col = lax.broadcasted_iota(jnp.int32, (tq, tq), 1)
        causal = col <= row
        hms = [_head_mask(hh, (tq, LANES)) for hh in range(2)]
        qhs = [jnp.where(hm, q, jnp.zeros_like(q)) for hm in hms]
        cbs = _block_constants(cr_ref, i, tq)

        def step(block, n_blocks, carry, masked):
            tk = n_blocks * tq
            start = pl.multiple_of(block * tq, tk)
            kj = k_ref[pl.ds(start, tk), :]
            vj = v_ref[pl.ds(start, tk), :]
            lane_k = lax.broadcasted_iota(jnp.int32, (tk, LANES), 1)
            new = []
            for hh in range(2):
                m, acc = carry[hh]
                bias = cbs[hh] - cr_ref[0, hh:hh + 1, pl.ds(start, tk)]
                s = lax.dot_general(qhs[hh], kj, NT_DIMS, preferred_element_type=F32) + bias
                if masked:
                    s = jnp.where(causal, s, NEG_BIG)
                m_new = jnp.maximum(m, jnp.max(s, axis=1, keepdims=True))
                alpha = jnp.exp(m - m_new)
                p = jnp.exp(s - m_new)
                vh = jnp.where(lane_k == (1 - hh) * FOX_HEAD_DIM, jnp.ones_like(vj), vj)
                acc = alpha * acc + jnp.dot(p.astype(BF16), vh, preferred_element_type=F32)
                new.append((m_new, acc))
            return tuple(new)

        lane = lax.broadcasted_iota(jnp.int32, (tq, LANES), 1)
        ones_lane = [lane == (1 - hh) * FOX_HEAD_DIM for hh in range(2)]
        init = (jnp.full((tq, 1), NEG_BIG, F32), jnp.zeros((tq, LANES), F32))
        n_wide = i // KV_STEP_FWD
        done = n_wide * KV_STEP_FWD
        n_pair = (i - done) // 2
        carry = lax.fori_loop(0, n_wide, lambda j, c: step(j * KV_STEP_FWD, KV_STEP_FWD, c, False), (init, init))
        carry = lax.fori_loop(0, n_pair, lambda j, c: step(done + 2 * j, 2, c, False), carry)
        carry = lax.fori_loop(done + 2 * n_pair, i, lambda j, c: step(j, 1, c, False), carry)
        carry = step(i, 1, carry, True)
        ls = [jnp.sum(jnp.where(ones_lane[hh], carry[hh][1], 0.0), axis=1, keepdims=True) for hh in range(2)]
        out = jnp.where(hms[0], carry[0][1] / ls[0], carry[1][1] / ls[1])
        for hh in range(2):
            lse_ref[hh] = carry[hh][0] + jnp.log(ls[hh])
        o_ref[...] = out.astype(o_ref.dtype)
        o32_ref[...] = out

    nblk = D_MODEL // LANES
    return pl.pallas_call(
        body, name=name, grid=(HEAD_PAIRS, nq),
        in_specs=[
            pl.BlockSpec((tq, LANES), lambda p, i: (i, PB_Q * nblk + p)),
            pl.BlockSpec((S, LANES), lambda p, i: (0, PB_K * nblk + p)),
            pl.BlockSpec((S, LANES), lambda p, i: (0, PB_V * nblk + p)),
            pl.BlockSpec((1, 2, S), lambda p, i: (p, 0, 0)),
        ],
        out_specs=[
            pl.BlockSpec((tq, LANES), lambda p, i: (i, p)),
            pl.BlockSpec((tq, LANES), lambda p, i: (i, p)),
            pl.BlockSpec((2, tq, 1), lambda p, i: (p, i, 0)),
        ],
        out_shape=[jax.ShapeDtypeStruct((S, D_MODEL), BF16), jax.ShapeDtypeStruct((S, D_MODEL), F32),
                   jax.ShapeDtypeStruct((FOX_HEADS, S, 1), F32)],
        compiler_params=_cparams(("parallel", "arbitrary"), 56),
    )(P, P, P, crow)


def _attn_bwd(P, do, o32, lse, crow, *, tq, name):
    S = P.shape[0]
    tq = _tile(S, tq)
    nq = S // tq
    scale = FOX_HEAD_DIM ** -0.5

    def body(q_ref, k_ref, v_ref, do_ref, o_ref, lse_ref, cr_ref, dq_ref, dk_out, dv_out, dc_ref, dcq_ref, dk_ref, dv_ref):
        i = pl.program_id(1)

        @pl.when(i == 0)
        def _():
            dk_ref[...] = jnp.zeros_like(dk_ref)
            dv_ref[...] = jnp.zeros_like(dv_ref)
            dc_ref[...] = jnp.zeros_like(dc_ref)

        q = q_ref[...] * jnp.asarray(scale, BF16)
        do_v = do_ref[...]
        row = lax.broadcasted_iota(jnp.int32, (tq, tq), 0)
        col = lax.broadcasted_iota(jnp.int32, (tq, tq), 1)
        causal = col <= row
        hms = [_head_mask(hh, (tq, LANES)) for hh in range(2)]
        qhs = [jnp.where(hm, q, jnp.zeros_like(q)) for hm in hms]
        dohs = [jnp.where(hm, do_v, jnp.zeros_like(do_v)) for hm in hms]
        cbs = _block_constants(cr_ref, i, tq)
        lses = [lse_ref[hh] for hh in range(2)]
        prod = do_v.astype(F32) * o_ref[...]
        dls = [jnp.sum(jnp.where(hm, prod, 0.0), axis=1, keepdims=True) for hm in hms]

        def step(block, n_blocks, carry, masked):
            dq_acc, row_sums = carry[0], list(carry[1:])
            tk = n_blocks * tq
            start = pl.multiple_of(block * tq, tk)
            kj = k_ref[pl.ds(start, tk), :]
            vj = v_ref[pl.ds(start, tk), :]
            head_k = [_head_mask(hh, (tk, LANES)) for hh in range(2)]
            dv_part = jnp.zeros((tk, LANES), F32)
            dk_part = jnp.zeros((tk, LANES), F32)
            for hh in range(2):
                bias = cbs[hh] - cr_ref[0, hh:hh + 1, pl.ds(start, tk)]
                s = lax.dot_general(qhs[hh], kj, NT_DIMS, preferred_element_type=F32) + bias
                if masked:
                    s = jnp.where(causal, s, NEG_BIG)
                p = jnp.exp(s - lses[hh])
                dp = lax.dot_general(dohs[hh], vj, NT_DIMS, preferred_element_type=F32)
                ds = p * (dp - dls[hh])
                pb = p.astype(BF16)
                dsb = ds.astype(BF16)
                dv_part = dv_part + lax.dot_general(pb, dohs[hh], TN_DIMS, preferred_element_type=F32)
                dk_part = dk_part + lax.dot_general(dsb, qhs[hh], TN_DIMS, preferred_element_type=F32)
                dc_ref[0, hh:hh + 1, pl.ds(start, tk)] -= jnp.sum(ds, axis=0, keepdims=True)
                row_sums[hh] = row_sums[hh] + jnp.sum(ds, axis=1, keepdims=True)
                kh = jnp.where(head_k[hh], kj, jnp.zeros_like(kj))
                dq_acc = dq_acc + jnp.dot(dsb, kh, preferred_element_type=F32)
            dv_ref[pl.ds(start, tk), :] += dv_part
            dk_ref[pl.ds(start, tk), :] += dk_part
            return (dq_acc, *row_sums)

        zero_col = jnp.zeros((tq, 1), F32)
        n_wide = i // KV_STEP
        carry = lax.fori_loop(0, n_wide, lambda j, c: step(j * KV_STEP, KV_STEP, c, False),
                              (jnp.zeros((tq, LANES), F32), zero_col, zero_col))
        carry = lax.fori_loop(n_wide * KV_STEP, i, lambda j, c: step(j, 1, c, False), carry)
        carry = step(i, 1, carry, True)
        dq_ref[...] = (carry[0] * scale).astype(dq_ref.dtype)
        for hh in range(2):
            dcq_ref[0, hh:hh + 1, :] = jnp.transpose(jnp.broadcast_to(carry[1 + hh], (tq, LANES)))[0:1, :]

        @pl.when(i == nq - 1)
        def _():
            dk_out[...] = dk_ref[...].astype(dk_out.dtype)
            dv_out[...] = dv_ref[...].astype(dv_out.dtype)

    nblk = D_MODEL // LANES
    qblk = pl.BlockSpec((tq, LANES), lambda p, i: (i, p))
    stat = pl.BlockSpec((2, tq, 1), lambda p, i: (p, i, 0))
    full = pl.BlockSpec((S, LANES), lambda p, i: (0, p))
    return pl.pallas_call(
        body, name=name, grid=(HEAD_PAIRS, nq),
        in_specs=[
            pl.BlockSpec((tq, LANES), lambda p, i: (i, PB_Q * nblk + p)),
            pl.BlockSpec((S, LANES), lambda p, i: (0, PB_K * nblk + p)),
            pl.BlockSpec((S, LANES), lambda p, i: (0, PB_V * nblk + p)),
            qblk, qblk, stat,
            pl.BlockSpec((1, 2, S), lambda p, i: (p, 0, 0)),
        ],
        out_specs=[qblk, full, full, pl.BlockSpec((1, 2, S), lambda p, i: (p, 0, 0)),
                   pl.BlockSpec((1, 2, tq), lambda p, i: (p, 0, i))],
        out_shape=[
            jax.ShapeDtypeStruct((S, D_MODEL), BF16),
            jax.ShapeDtypeStruct((S, D_MODEL), BF16),
            jax.ShapeDtypeStruct((S, D_MODEL), BF16),
            jax.ShapeDtypeStruct((HEAD_PAIRS, 2, S), F32),
            jax.ShapeDtypeStruct((HEAD_PAIRS, 2, S), F32),
        ],
        scratch_shapes=[pltpu.VMEM((S, LANES), F32), pltpu.VMEM((S, LANES), F32)],
        compiler_params=_cparams(("parallel", "arbitrary"), 56),
    )(P, P, P, do, o32, lse, crow)


def _cumsum_lanes(xs, *, reverse, name):
    R, S = xs[0].shape
    nb = S // LANES
    n_in = len(xs)

    def body(*refs):
        x_refs, o_ref = refs[:n_in], refs[n_in]
        r = lax.broadcasted_iota(jnp.int32, (LANES, LANES), 0)
        c = lax.broadcasted_iota(jnp.int32, (LANES, LANES), 1)
        tri = ((r >= c) if reverse else (r <= c)).astype(F32)

        def step(b, carry):
            blk = (nb - 1 - b) if reverse else b
            start = pl.multiple_of(blk * LANES, LANES)
            xb = x_refs[0][:, pl.ds(start, LANES)]
            for r in x_refs[1:]:
                xb = xb + r[:, pl.ds(start, LANES)]
            y = jnp.dot(xb, tri, precision=lax.Precision.HIGHEST, preferred_element_type=F32) + carry
            o_ref[:, pl.ds(start, LANES)] = y
            return carry + jnp.sum(xb, axis=1, keepdims=True)

        lax.fori_loop(0, nb, step, jnp.zeros((R, 1), F32))

    return pl.pallas_call(
        body, name=name,
        in_specs=[pl.BlockSpec(memory_space=pltpu.VMEM)] * n_in,
        out_specs=pl.BlockSpec(memory_space=pltpu.VMEM),
        out_shape=jax.ShapeDtypeStruct((R, S), F32),
    )(*xs)


def _forget_fwd(f_logit, b_pad, *, name, tm=1024):
    S = f_logit.shape[0]
    tm = _tile(S, tm)

    def body(f_ref, b_ref, o_ref):
        z = f_ref[...] + b_ref[...]
        o_ref[...] = jnp.minimum(z, 0.0) - jnp.log(1.0 + jnp.exp(-jnp.abs(z)))

    blk = pl.BlockSpec((tm, LANES), lambda i: (i, 0))
    return pl.pallas_call(
        body, name=name, grid=(S // tm,),
        in_specs=[blk, pl.BlockSpec((1, LANES), lambda i: (0, 0))],
        out_specs=blk, out_shape=jax.ShapeDtypeStruct((S, LANES), F32),
        compiler_params=_cparams(("parallel",)),
    )(f_logit, b_pad)


def _forget_bwd(f_logit, b_pad, dlogf, *, name, tm=1024):
    S = f_logit.shape[0]
    tm = _tile(S, tm)

    def body(f_ref, b_ref, d_ref, o_ref, db_ref):
        i = pl.program_id(0)
        z = f_ref[...] + b_ref[...]
        dz = d_ref[...] * (1.0 - _sigmoid(z))
        o_ref[...] = dz.astype(o_ref.dtype)
        part = jnp.sum(dz, axis=0, keepdims=True)

        @pl.when(i == 0)
        def _():
            db_ref[...] = part

        @pl.when(i > 0)
        def _():
            db_ref[...] += part

    blk = pl.BlockSpec((tm, LANES), lambda i: (i, 0))
    vec = pl.BlockSpec((1, LANES), lambda i: (0, 0))
    return pl.pallas_call(
        body, name=name, grid=(S // tm,),
        in_specs=[blk, vec, blk], out_specs=[blk, vec],
        out_shape=[jax.ShapeDtypeStruct((S, LANES), BF16), jax.ShapeDtypeStruct((1, LANES), F32)],
        compiler_params=_cparams(("arbitrary",)),
    )(f_logit, b_pad, dlogf)


def _layernorm_stats(y):
    mu = jnp.mean(y, axis=-1, keepdims=True)
    yc = y - mu
    rstd = lax.rsqrt(jnp.mean(yc * yc, axis=-1, keepdims=True) + LN_EPS)
    return yc * rstd, rstd


def _fill_shifted(src_ref, sh_ref, tm):
    rows = tm + CONV_HALO - SUBLANES
    for s in range(1, SUBLANES):
        sh_ref[s - 1, 0:rows, :] = src_ref[pl.ds(s, rows), :]


def _shifted_rows(src_ref, sh_ref, r0, offset, chunk=CONV_CHUNK):
    s, base = offset % SUBLANES, offset - offset % SUBLANES
    rows = pl.ds(pl.multiple_of(r0 + base, SUBLANES), chunk)
    return src_ref[rows, :] if s == 0 else sh_ref[s - 1, rows, :]


def _fill_tap_rows(w_ref, wb_ref):
    for k in range(CONV_WIDTH):
        for j in range(N_DEV):
            wb_ref[k * SUBLANES:(k + 1) * SUBLANES, j * LANES:(j + 1) * LANES] = jnp.broadcast_to(
                w_ref[j, k:k + 1, :], (SUBLANES, LANES))


def _tap_sum(wb_ref, rows_of_tap, chunk=CONV_CHUNK):
    n = chunk // SUBLANES
    accs = [None] * n
    for k in range(CONV_WIDTH):
        wk = wb_ref[k * SUBLANES:(k + 1) * SUBLANES, :]
        src = rows_of_tap(k)
        for h in range(n):
            term = wk * src[h * SUBLANES:(h + 1) * SUBLANES]
            accs[h] = term if accs[h] is None else accs[h] + term
    return jnp.concatenate(accs, axis=0)


def _conv_fwd(P, w_pad, conv_b, ln_g, ln_b, *, name, tm=256):
    S = P.shape[0]
    C = D_MODEL
    tm = _tile(S, tm)
    hb = tm // CONV_HALO

    def body(a_ref, gt_ref, ah_ref, gh_ref, w_ref, cb_ref, g_ref, b_ref, o_ref, y_ref, glu_ref, sh_ref, wb_ref):
        i = pl.program_id(0)
        halo = ah_ref[...].astype(F32) * _sigmoid(gh_ref[...].astype(F32))
        glu_ref[0:CONV_HALO, :] = jnp.where(i > 0, halo, 0.0)
        glu_ref[CONV_HALO:, :] = a_ref[...].astype(F32) * _sigmoid(gt_ref[...].astype(F32))
        _fill_shifted(glu_ref, sh_ref, tm)
        _fill_tap_rows(w_ref, wb_ref)

        rc = 2 * CONV_CHUNK

        def chunk(r, carry):
            r0 = pl.multiple_of(r * rc, rc)
            acc = _tap_sum(wb_ref, lambda k: _shifted_rows(glu_ref, sh_ref, r0, CONV_HALO - (CONV_WIDTH - 1) + k, rc), rc)
            y_ref[pl.ds(r0, rc), :] = acc + cb_ref[...]
            return carry

        lax.fori_loop(0, tm // rc, chunk, 0)
        xhat, _ = _layernorm_stats(y_ref[...])
        z = xhat * g_ref[...] + b_ref[...]
        o_ref[...] = (z * _sigmoid(z)).astype(o_ref.dtype)

    vec = pl.BlockSpec((1, C), lambda i: (0, 0))
    row = pl.BlockSpec((tm, C), lambda i: (i, 0))
    return pl.pallas_call(
        body, name=name, grid=(S // tm,),
        in_specs=[
            pl.BlockSpec((tm, C), lambda i: (i, PB_A)),
            pl.BlockSpec((tm, C), lambda i: (i, PB_GATE)),
            pl.BlockSpec((CONV_HALO, C), lambda i: (jnp.maximum(i * hb - 1, 0), PB_A)),
            pl.BlockSpec((CONV_HALO, C), lambda i: (jnp.maximum(i * hb - 1, 0), PB_GATE)),
            pl.BlockSpec((N_DEV, CONV_WIDTH, LANES), lambda i: (0, 0, 0)),
            vec, vec, vec,
        ],
        out_specs=[row, row],
        out_shape=[jax.ShapeDtypeStruct((S, C), BF16), jax.ShapeDtypeStruct((S, C), F32)],
        scratch_shapes=[pltpu.VMEM((tm + CONV_HALO, C), F32), pltpu.VMEM((SUBLANES - 1, tm + CONV_HALO, C), F32),
                        pltpu.VMEM((CONV_HALO * SUBLANES, C), F32)],
        compiler_params=_cparams(("parallel",), 56),
    )(P, P, P, P, w_pad, conv_b, ln_g, ln_b)


def _conv_bwd(P, do, y, w_pad, ln_g, ln_b, *, name, tm=256):
    S = P.shape[0]
    C = D_MODEL
    tm = _tile(S, tm)
    hb = tm // CONV_HALO
    n_tiles = S // tm
    last_halo = S // CONV_HALO - 1

    def body(a_ref, gt_ref, ah_ref, gh_ref, do_ref, y_ref, don_ref, yn_ref, w_ref, g_ref, b_ref,
             dglu_ref, dw_ref, small_ref, glu_ref, dy_ref, gsh_ref, dsh_ref, dwacc_ref, wb_ref):
        i = pl.program_id(0)

        @pl.when(i == 0)
        def _():
            dwacc_ref[...] = jnp.zeros_like(dwacc_ref)
            small_ref[...] = jnp.zeros_like(small_ref)

        def ln_bwd(do_v, y_v):
            xhat, rstd = _layernorm_stats(y_v)
            z = xhat * g_ref[...] + b_ref[...]
            sg = _sigmoid(z)
            dz = do_v * (sg * (1.0 + z * (1.0 - sg)))
            dxh = dz * g_ref[...]
            dy = rstd * (dxh - jnp.mean(dxh, axis=-1, keepdims=True) - xhat * jnp.mean(dxh * xhat, axis=-1, keepdims=True))
            return dy, dz, xhat

        dy, dz, xhat = ln_bwd(do_ref[...], y_ref[...])
        dy_next, _, _ = ln_bwd(don_ref[...], yn_ref[...])
        small_ref[0:1, :] += jnp.sum(dz * xhat, axis=0, keepdims=True)
        small_ref[1:2, :] += jnp.sum(dz, axis=0, keepdims=True)
        small_ref[2:3, :] += jnp.sum(dy, axis=0, keepdims=True)
        dy_ref[0:tm, :] = dy
        dy_ref[tm:, :] = jnp.where(i < n_tiles - 1, dy_next, 0.0)

        halo = ah_ref[...].astype(F32) * _sigmoid(gh_ref[...].astype(F32))
        glu_ref[0:CONV_HALO, :] = jnp.where(i > 0, halo, 0.0)
        glu_ref[CONV_HALO:, :] = a_ref[...].astype(F32) * _sigmoid(gt_ref[...].astype(F32))
        _fill_shifted(glu_ref, gsh_ref, tm)
        _fill_shifted(dy_ref, dsh_ref, tm)
        _fill_tap_rows(w_ref, wb_ref)

        def chunk(r, carry):
            r0 = pl.multiple_of(r * CONV_CHUNK, CONV_CHUNK)
            rows = pl.ds(r0, CONV_CHUNK)
            dyc = dy_ref[rows, :]
            for k in range(CONV_WIDTH):
                prod = dyc * _shifted_rows(glu_ref, gsh_ref, r0, CONV_HALO - (CONV_WIDTH - 1) + k)
                dwacc_ref[k * SUBLANES:(k + 1) * SUBLANES, :] += prod[0:SUBLANES] + prod[SUBLANES:]
            dg = _tap_sum(wb_ref, lambda k: _shifted_rows(dy_ref, dsh_ref, r0, CONV_WIDTH - 1 - k))
            a = a_ref[rows, :].astype(F32)
            sig = _sigmoid(gt_ref[rows, :].astype(F32))
            dglu_ref[rows, 0:C] = (dg * sig).astype(dglu_ref.dtype)
            dglu_ref[rows, C:] = (dg * a * sig * (1.0 - sig)).astype(dglu_ref.dtype)
            return carry

        lax.fori_loop(0, tm // CONV_CHUNK, chunk, 0)

        @pl.when(i == n_tiles - 1)
        def _():
            dw_ref[...] = jnp.zeros_like(dw_ref)
            for k in range(CONV_WIDTH):
                row_k = jnp.sum(dwacc_ref[k * SUBLANES:(k + 1) * SUBLANES, :], axis=0, keepdims=True)
                for j in range(N_DEV):
                    dw_ref[j, k:k + 1, :] = row_k[:, j * LANES:(j + 1) * LANES]

    vec = pl.BlockSpec((1, C), lambda i: (0, 0))
    row = pl.BlockSpec((tm, C), lambda i: (i, 0))
    nxt = pl.BlockSpec((CONV_HALO, C), lambda i: (jnp.minimum((i + 1) * hb, last_halo), 0))
    return pl.pallas_call(
        body, name=name, grid=(n_tiles,),
        in_specs=[
            pl.BlockSpec((tm, C), lambda i: (i, PB_A)),
            pl.BlockSpec((tm, C), lambda i: (i, PB_GATE)),
            pl.BlockSpec((CONV_HALO, C), lambda i: (jnp.maximum(i * hb - 1, 0), PB_A)),
            pl.BlockSpec((CONV_HALO, C), lambda i: (jnp.maximum(i * hb - 1, 0), PB_GATE)),
            row, row, nxt, nxt,
            pl.BlockSpec((N_DEV, CONV_WIDTH, LANES), lambda i: (0, 0, 0)),
            vec, vec,
        ],
        out_specs=[
            pl.BlockSpec((tm, 2 * C), lambda i: (i, 0)),
            pl.BlockSpec((N_DEV, CONV_HALO, LANES), lambda i: (0, 0, 0)),
            pl.BlockSpec((8, C), lambda i: (0, 0)),
        ],
        out_shape=[
            jax.ShapeDtypeStruct((S, 2 * C), BF16),
            jax.ShapeDtypeStruct((N_DEV, CONV_HALO, LANES), F32),
            jax.ShapeDtypeStruct((8, C), F32),
        ],
        scratch_shapes=[
            pltpu.VMEM((tm + CONV_HALO, C), F32), pltpu.VMEM((tm + CONV_HALO, C), F32),
            pltpu.VMEM((SUBLANES - 1, tm + CONV_HALO, C), F32), pltpu.VMEM((SUBLANES - 1, tm + CONV_HALO, C), F32),
            pltpu.VMEM((CONV_HALO * SUBLANES, C), F32), pltpu.VMEM((CONV_HALO * SUBLANES, C), F32),
        ],
        compiler_params=_cparams(("arbitrary",), 56),
    )(P, P, P, P, do, y, do, y, w_pad, ln_g, ln_b)


def _mem_softmax(qh, kh):
    s = lax.dot_general(qh, kh, NT_DIMS, preferred_element_type=F32)
    e = jnp.exp(s - jnp.max(s, axis=1, keepdims=True))
    return e / jnp.sum(e, axis=1, keepdims=True)


def _mem_fwd(P, kv, *, name, tm=1024):
    S, M = P.shape[0], kv.shape[0]
    tm = _tile(S, tm)
    scale = MEM_HEAD_DIM ** -0.5

    def body(q_ref, k_ref, v_ref, o_ref):
        for h in range(MEM_HEADS):
            sl = slice(h * MEM_HEAD_DIM, (h + 1) * MEM_HEAD_DIM)
            qh = q_ref[:, sl] * jnp.asarray(scale, BF16)
            p = _mem_softmax(qh, k_ref[:, sl])
            o_ref[:, sl] = jnp.dot(p.astype(BF16), v_ref[:, sl], preferred_element_type=F32).astype(o_ref.dtype)

    return pl.pallas_call(
        body, name=name, grid=(S // tm,),
        in_specs=[
            pl.BlockSpec((tm, D_MODEL), lambda i: (i, PB_QMEM)),
            pl.BlockSpec((M, D_MODEL), lambda i: (0, 0)),
            pl.BlockSpec((M, D_MODEL), lambda i: (0, 1)),
        ],
        out_specs=pl.BlockSpec((tm, D_MODEL), lambda i: (i, 0)),
        out_shape=jax.ShapeDtypeStruct((S, D_MODEL), BF16),
        compiler_params=_cparams(("parallel",)),
    )(P, kv, kv)


def _mem_bwd(P, do, kv, *, name, tm=1024):
    S, M = P.shape[0], kv.shape[0]
    tm = _tile(S, tm)
    scale = MEM_HEAD_DIM ** -0.5

    def body(q_ref, k_ref, v_ref, do_ref, dq_ref, dkv_ref):
        i = pl.program_id(0)

        @pl.when(i == 0)
        def _():
            dkv_ref[...] = jnp.zeros_like(dkv_ref)

        for h in range(MEM_HEADS):
            sl = slice(h * MEM_HEAD_DIM, (h + 1) * MEM_HEAD_DIM)
            slv = slice(D_MODEL + h * MEM_HEAD_DIM, D_MODEL + (h + 1) * MEM_HEAD_DIM)
            qh = q_ref[:, sl] * jnp.asarray(scale, BF16)
            kh, vh, doh = k_ref[:, sl], v_ref[:, sl], do_ref[:, sl]
            p = _mem_softmax(qh, kh)
            dp = lax.dot_general(doh, vh, NT_DIMS, preferred_element_type=F32)
            ds = p * (dp - jnp.sum(p * dp, axis=1, keepdims=True))
            dsb = ds.astype(BF16)
            dq_ref[:, sl] = (jnp.dot(dsb, kh, preferred_element_type=F32) * scale).astype(dq_ref.dtype)
            dkv_ref[:, sl] += lax.dot_general(dsb, qh, TN_DIMS, preferred_element_type=F32)
            dkv_ref[:, slv] += lax.dot_general(p.astype(BF16), doh, TN_DIMS, preferred_element_type=F32)

    row = pl.BlockSpec((tm, D_MODEL), lambda i: (i, 0))
    return pl.pallas_call(
        body, name=name, grid=(S // tm,),
        in_specs=[
            pl.BlockSpec((tm, D_MODEL), lambda i: (i, PB_QMEM)),
            pl.BlockSpec((M, D_MODEL), lambda i: (0, 0)),
            pl.BlockSpec((M, D_MODEL), lambda i: (0, 1)),
            row,
        ],
        out_specs=[row, pl.BlockSpec((M, 2 * D_MODEL), lambda i: (0, 0))],
        out_shape=[jax.ShapeDtypeStruct((S, D_MODEL), BF16), jax.ShapeDtypeStruct((M, 2 * D_MODEL), F32)],
        compiler_params=_cparams(("arbitrary",)),
    )(P, kv, kv, do)


def _merge_fwd(o_attn, o_conv, o_mem, P, wpa, wpc, wpm, w_out, x, g_post, g_pre, *, name, tm=256):
    S = P.shape[0]
    D = D_MODEL
    tm = _tile(S, tm)

    def body(oa_ref, oc_ref, om_ref, g0_ref, g1_ref, g2_ref, wa_ref, wc_ref, wm_ref, wo_ref, x_ref, gp_ref, gq_ref,
             mg_ref, y_ref, pb_ref, x1_ref, h2_ref):
        merged = jnp.zeros((tm, D), F32)
        for b, (o_ref, g_ref, w_ref) in enumerate(((oa_ref, g0_ref, wa_ref), (oc_ref, g1_ref, wc_ref), (om_ref, g2_ref, wm_ref))):
            pb = jnp.dot(o_ref[...], w_ref[...], preferred_element_type=F32)
            pb_ref[b] = pb.astype(pb_ref.dtype)
            merged = merged + _sigmoid(g_ref[...].astype(F32)) * pb
        mb = merged.astype(BF16)
        mg_ref[...] = mb
        yv = jnp.dot(mb, wo_ref[...], preferred_element_type=F32)
        y_ref[...] = yv
        x1 = x_ref[...] + yv * lax.rsqrt(jnp.mean(yv * yv, axis=-1, keepdims=True) + RMS_EPS) * gp_ref[...]
        x1_ref[...] = x1
        h2_ref[...] = (x1 * lax.rsqrt(jnp.mean(x1 * x1, axis=-1, keepdims=True) + RMS_EPS) * gq_ref[...]).astype(h2_ref.dtype)

    row = pl.BlockSpec((tm, D), lambda i: (i, 0))
    vec = pl.BlockSpec((1, D), lambda i: (0, 0))
    wsp = pl.BlockSpec((D, D), lambda i: (0, 0))
    return pl.pallas_call(
        body, name=name, grid=(S // tm,),
        in_specs=[row, row, row] + [pl.BlockSpec((tm, D), lambda i, b=b: (i, PB_G0 + b)) for b in range(3)] + [wsp] * 4
        + [row, vec, vec],
        out_specs=[row, row, pl.BlockSpec((3, tm, D), lambda i: (0, i, 0)), row, row],
        out_shape=[jax.ShapeDtypeStruct((S, D), BF16), jax.ShapeDtypeStruct((S, D), F32), jax.ShapeDtypeStruct((3, S, D), BF16),
                   jax.ShapeDtypeStruct((S, D), F32), jax.ShapeDtypeStruct((S, D), BF16)],
        compiler_params=_cparams(("parallel",), 56),
    )(o_attn, o_conv, o_mem, P, P, P, wpa, wpc, wpm, w_out, x, g_post, g_pre)


def _merge_bwd(dy1, w_out, P, pb, *, name, tm=256):
    S = P.shape[0]
    D = D_MODEL
    tm = _tile(S, tm)

    def body(dy_ref, wo_ref, g0_ref, g1_ref, g2_ref, pb_ref, d0_ref, d1_ref, d2_ref, dgl_ref):
        dm = lax.dot_general(dy_ref[...], wo_ref[...], NT_DIMS, preferred_element_type=F32)
        for b, (g_ref, d_ref) in enumerate(((g0_ref, d0_ref), (g1_ref, d1_ref), (g2_ref, d2_ref))):
            g = _sigmoid(g_ref[...].astype(F32))
            d_ref[...] = (dm * g).astype(d_ref.dtype)
            dgl_ref[:, b * D:(b + 1) * D] = (dm * pb_ref[b].astype(F32) * g * (1.0 - g)).astype(dgl_ref.dtype)

    row = pl.BlockSpec((tm, D), lambda i: (i, 0))
    blk3 = pl.BlockSpec((3, tm, D), lambda i: (0, i, 0))
    sds = jax.ShapeDtypeStruct((S, D), BF16)
    return pl.pallas_call(
        body, name=name, grid=(S // tm,),
        in_specs=[row, pl.BlockSpec((D, D), lambda i: (0, 0))]
        + [pl.BlockSpec((tm, D), lambda i, b=b: (i, PB_G0 + b)) for b in range(3)] + [blk3],
        out_specs=[row, row, row, pl.BlockSpec((tm, 3 * D), lambda i: (i, 0))],
        out_shape=[sds, sds, sds, jax.ShapeDtypeStruct((S, 3 * D), BF16)],
        compiler_params=_cparams(("parallel",), 56),
    )(dy1, w_out, P, P, P, pb)


def _swiglu_fwd(h2, w_gu_t, *, name, tm=512, tn=1408):
    S, D = h2.shape
    Fh = w_gu_t.shape[0] // 2
    tm, tn = _tile(S, tm), _tile(Fh, tn)
    nj = Fh // tn

    def body(h_ref, wg_ref, wu_ref, g_ref, u_ref, a_ref):
        hv = h_ref[...]
        g = lax.dot_general(hv, wg_ref[...], NT_DIMS, preferred_element_type=F32)
        u = lax.dot_general(hv, wu_ref[...], NT_DIMS, preferred_element_type=F32)
        g_ref[...] = g.astype(g_ref.dtype)
        u_ref[...] = u.astype(u_ref.dtype)
        a_ref[...] = (g * _sigmoid(g) * u).astype(a_ref.dtype)

    out = pl.BlockSpec((tm, tn), lambda j, i: (i, j))
    sds = jax.ShapeDtypeStruct((S, Fh), BF16)
    return pl.pallas_call(
        body, name=name, grid=(nj, S // tm),
        in_specs=[
            pl.BlockSpec((tm, D), lambda j, i: (i, 0)),
            pl.BlockSpec((tn, D), lambda j, i: (j, 0)),
            pl.BlockSpec((tn, D), lambda j, i: (j + nj, 0)),
        ],
        out_specs=[out, out, out], out_shape=[sds, sds, sds],
        compiler_params=_cparams(("parallel", "parallel"), 56),
    )(h2, w_gu_t, w_gu_t)


def _swiglu_bwd(dffn, w_down, g, u, *, name, tm=512, tn=1408):
    S, Fh = g.shape
    D = dffn.shape[1]
    tm, tn = _tile(S, tm), _tile(Fh, tn)

    def body(df_ref, w_ref, g_ref, u_ref, o_ref):
        dfv = df_ref[...]
        for j in range(Fh // tn):
            cols = slice(j * tn, (j + 1) * tn)
            da = lax.dot_general(dfv, w_ref[cols, :], NT_DIMS, preferred_element_type=F32)
            gv = g_ref[:, cols].astype(F32)
            sg = _sigmoid(gv)
            o_ref[:, cols] = (da * u_ref[:, cols].astype(F32) * (sg * (1.0 + gv * (1.0 - sg)))).astype(o_ref.dtype)
            o_ref[:, Fh + j * tn:Fh + (j + 1) * tn] = (da * gv * sg).astype(o_ref.dtype)

    act = pl.BlockSpec((tm, Fh), lambda i: (i, 0))
    return pl.pallas_call(
        body, name=name, grid=(S // tm,),
        in_specs=[pl.BlockSpec((tm, D), lambda i: (i, 0)), pl.BlockSpec((Fh, D), lambda i: (0, 0)), act, act],
        out_specs=pl.BlockSpec((tm, 2 * Fh), lambda i: (i, 0)),
        out_shape=jax.ShapeDtypeStruct((S, 2 * Fh), BF16),
        compiler_params=_cparams(("parallel",), 56),
    )(dffn, w_down, g, u)


def _final(x1, ffn, target, g, *, name, tm=512):
    S, D = x1.shape
    tm = _tile(S, tm)

    def body(x_ref, f_ref, t_ref, g_ref, dout_ref, dffn_ref, loss_ref, dg_ref):
        i = pl.program_id(0)
        fv = f_ref[...]
        rstd = lax.rsqrt(jnp.mean(fv * fv, axis=-1, keepdims=True) + RMS_EPS)
        r = fv * rstd
        e = x_ref[...] + r * g_ref[...] - t_ref[...]
        dout = e * (1.0 / D)
        dout_ref[...] = dout
        gy = dout * g_ref[...]
        dffn_ref[...] = (rstd * (gy - r * jnp.mean(gy * r, axis=-1, keepdims=True))).astype(dffn_ref.dtype)
        lpart = jnp.full((8, LANES), 0.5 * jnp.sum(jnp.mean(e * e, axis=-1, keepdims=True)), F32)
        gpart = jnp.sum(dout * r, axis=0, keepdims=True)

        @pl.when(i == 0)
        def _():
            loss_ref[...] = lpart
            dg_ref[...] = gpart

        @pl.when(i > 0)
        def _():
            loss_ref[...] += lpart
            dg_ref[...] += gpart

    row = pl.BlockSpec((tm, D), lambda i: (i, 0))
    vec = pl.BlockSpec((1, D), lambda i: (0, 0))
    return pl.pallas_call(
        body, name=name, grid=(S // tm,),
        in_specs=[row, row, row, vec],
        out_specs=[row, row, pl.BlockSpec((8, LANES), lambda i: (0, 0)), vec],
        out_shape=[jax.ShapeDtypeStruct((S, D), F32), jax.ShapeDtypeStruct((S, D), BF16),
                   jax.ShapeDtypeStruct((8, LANES), F32), jax.ShapeDtypeStruct((1, D), F32)],
        compiler_params=_cparams(("arbitrary",)),
    )(x1, ffn, target, g)


def _after(token, value):
    return value if token is None else value + token[0, 0].astype(value.dtype)


def _local_step(x, mem, target, gains, w, late_weights, send_grads, *, tq=512):
    S = x.shape[0]
    b_pad = jnp.pad(gains["b_forget"], ((0, 0), (0, LANES - FOX_HEADS)))
    gains = dict(gains)

    h = _rms_fwd(x, gains["norm_mix_pre"], name="rms_mix_pre")
    P = _matmul(h, w["wp_t"], mode="nt", out_dtype=BF16, name="proj_in", tm=2048)
    f_logit = _matmul(h, w["wf_t"], mode="nt", out_dtype=F32, name="proj_forget")
    logf = _forget_fwd(f_logit, b_pad, name="forget_fwd")
    c_row16 = _cumsum_lanes([logf[:, :FOX_HEADS].T], reverse=False, name="forget_cumsum")
    crow = c_row16.reshape(HEAD_PAIRS, 2, S)
    o_attn, o_attn32, lse = _attn_fwd(P, crow, tq=tq, name="attn_fwd")
    w = dict(w, **late_weights(o_attn))
    o_conv, y_conv = _conv_fwd(P, w["conv_w"], gains["conv_b"], gains["conv_ln_g"], gains["conv_ln_b"], name="conv_fwd")
    mem_n = _rms_fwd(mem, gains["norm_mem"], name="rms_mem")
    kv = _matmul(mem_n, w["w_kv"], mode="nn", out_dtype=BF16, name="mem_kv")
    o_mem = _mem_fwd(P, kv, name="mem_fwd")
    merged, y1, pb, x1, h2 = _merge_fwd(o_attn, o_conv, o_mem, P, w["wpa"], w["wpc"], w["wpm"], w["w_out"], x,
                                        gains["norm_mix_post"], gains["norm_ffn_pre"], name="merge_fwd")
    g_ffn, u_ffn, act = _swiglu_fwd(h2, w["w_gu_t"], name="swiglu_fwd")
    ffn = _matmul(act, w["w_down"], mode="nn", out_dtype=F32, name="ffn_down", tk=1408)

    dout, dffn, loss_tile, d_norm_ffn_post = _final(x1, ffn, target, gains["norm_ffn_post"], name="loss_head")
    d_w_down = _matmul(act, dffn, mode="tn", out_dtype=BF16, name="dw_down", tm=1408)
    dgu = _swiglu_bwd(dffn, w["w_down"], g_ffn, u_ffn, name="swiglu_bwd")
    dh2 = _matmul(dgu, w["w_gu_t"], mode="nn", out_dtype=F32, name="d_h2", tk=1408)
    d_w_gu_t = _matmul(dgu, h2, mode="tn", out_dtype=BF16, name="dw_gate_up", tm=1408)
    dx1, dy1, d_norm_ffn_pre, d_norm_mix_post = _rms_bwd_pair(x1, dh2, dout, gains["norm_ffn_pre"], y1, gains["norm_mix_post"],
                                                              name="rms_ffn_pre_mix_post_bwd")

    d_w_out = _matmul(merged, dy1, mode="tn", out_dtype=BF16, name="dw_out")
    *dpb, dgl = _merge_bwd(dy1, w["w_out"], P, pb, name="merge_bwd")
    do_attn = _matmul(dpb[0], w["wpa"], mode="nt", out_dtype=BF16, name="d_o_attn")
    do_conv = _matmul(dpb[1], w["wpc"], mode="nt", out_dtype=F32, name="d_o_conv")
    do_mem = _matmul(dpb[2], w["wpm"], mode="nt", out_dtype=BF16, name="d_o_mem")
    d_wpa = _matmul(o_attn, dpb[0], mode="tn", out_dtype=BF16, name="dw_proj_attn")
    d_wpc = _matmul(o_conv, dpb[1], mode="tn", out_dtype=BF16, name="dw_proj_conv")
    d_wpm = _matmul(o_mem, dpb[2], mode="tn", out_dtype=BF16, name="dw_proj_mem")

    dq_mem, dkv = _mem_bwd(P, do_mem, kv, name="mem_bwd")
    dkv_b = dkv.astype(BF16)
    d_w_kv = _matmul(mem_n, dkv_b, mode="tn", out_dtype=BF16, name="dw_kv")
    dmem_n = _matmul(dkv_b, w["w_kv"], mode="nt", out_dtype=F32, name="d_mem_n")
    _, d_norm_mem = _rms_bwd(mem, [dmem_n], gains["norm_mem"], None, out_dtype=BF16, name="rms_mem_bwd")

    dglu, d_conv_w, conv_small = _conv_bwd(P, do_conv, y_conv, w["conv_w"], gains["conv_ln_g"], gains["conv_ln_b"], name="conv_bwd")
    sent = send_grads("body", dict(conv_w=d_conv_w[:, :CONV_WIDTH].astype(BF16), w_kv_mem=d_w_kv, w_proj_attn=d_wpa,
                                   w_proj_conv=d_wpc, w_proj_mem=d_wpm, w_out=d_w_out, w_gate_up=d_w_gu_t, w_down=d_w_down))

    dq, dk, dv, dc, dcq = _attn_bwd(P, do_attn, o_attn32, lse, _after(sent, crow), tq=tq, name="attn_bwd")
    dlogf16 = _cumsum_lanes([dc.reshape(FOX_HEADS, S), dcq.reshape(FOX_HEADS, S)], reverse=True, name="forget_cumsum_bwd")
    dlogf = jnp.pad(dlogf16.T, ((0, 0), (0, LANES - FOX_HEADS)))
    df, d_b_pad = _forget_bwd(f_logit, b_pad, dlogf, name="forget_bwd")

    d_proj = [dq, dk, dv, dglu, dq_mem, dgl]
    d_wp_t = _matmul_pieces(d_proj, h, mode="tn", out_dtype=BF16, name="dw_in")
    d_wf_t = _matmul(df, h, mode="tn", out_dtype=BF16, name="dw_forget")
    n_qkv = 3 * D_MODEL
    pieces = [(d_wp_t[:n_qkv], n_qkv), (d_wf_t[:FOX_HEADS], FOX_HEADS), (d_wp_t[n_qkv:], P_WIDTH - n_qkv)]
    n_own = (P_WIDTH + FOX_HEADS) // N_DEV
    blocks = [jnp.concatenate(_rows(pieces, j * n_own, (j + 1) * n_own), axis=0) for j in range(N_DEV)]
    sent = send_grads("w_in", dict(w_in=jnp.stack(blocks)))
    dh_p = _matmul_pieces(d_proj, w["wp_t"], mode="nn", out_dtype=F32, name="d_h", after=sent)
    grad_x, d_norm_mix_pre = _rms_bwd(x, [dh_p], gains["norm_mix_pre"], dx1, out_dtype=F32, name="rms_mix_pre_bwd",
                                      proj=(df, w["wf_t"]))

    small = dict(norm_mix_pre=d_norm_mix_pre, norm_mix_post=d_norm_mix_post, norm_mem=d_norm_mem,
                 conv_b=conv_small[2:3], conv_ln_g=conv_small[0:1], conv_ln_b=conv_small[1:2],
                 norm_ffn_pre=d_norm_ffn_pre, norm_ffn_post=d_norm_ffn_post, b_forget=d_b_pad[:, :FOX_HEADS])
    return loss_tile, grad_x, small


def _mesh_pos():
    return lax.axis_index("x"), lax.axis_index("y"), lax.axis_index("c")


def _flip(pos, d):
    x, y, c = pos
    return (1 - x if d & 4 else x, 1 - y if d & 2 else y, 1 - c if d & 1 else c)


def _flat(pos):
    x, y, c = pos
    return 4 * x + 2 * y + c


def _gather_two_level(src, *, name):
    def body(src_ref, out_ref, token, send_sems, recv_sems, local_sem):
        me = _mesh_pos()
        my = _flat(me)
        sibling = _flip(me, 1)
        far = [_flip(me, d) for d in (4, 2, 6)]

        def copy(k, block, to, own=False):
            return pltpu.make_async_remote_copy(
                src_ref=src_ref if own else out_ref.at[_flat(block)], dst_ref=out_ref.at[_flat(block)],
                send_sem=send_sems.at[k], recv_sem=recv_sems.at[k], device_id=to, device_id_type=pl.DeviceIdType.MESH)

        local = pltpu.make_async_copy(src_ref, out_ref.at[my], local_sem)
        local.start()
        first = [copy(0, me, sibling, own=True)] + [copy(1 + j, me, peer, own=True) for j, peer in enumerate(far)]
        for cp in first:
            cp.start()
        passed = [copy(4 + j, peer, sibling) for j, peer in enumerate(far)]
        for j, peer in enumerate(far):
            copy(1 + j, peer, me).wait_recv()
            passed[j].start()
        copy(0, sibling, me).wait_recv()
        for j, peer in enumerate(far):
            copy(4 + j, _flip(peer, 1), me).wait_recv()
        for cp in first + passed:
            cp.wait_send()
        local.wait()
        token[...] = jnp.zeros_like(token)

    n_copy = N_DEV - 1
    out, token = pl.pallas_call(
        body, name=name,
        in_specs=[pl.BlockSpec(memory_space=pl.ANY)],
        out_specs=[pl.BlockSpec(memory_space=pl.ANY), pl.BlockSpec(memory_space=pltpu.VMEM)],
        out_shape=[jax.ShapeDtypeStruct((N_DEV,) + tuple(src.shape), src.dtype), jax.ShapeDtypeStruct((8, LANES), F32)],
        scratch_shapes=[pltpu.SemaphoreType.DMA((n_copy,)), pltpu.SemaphoreType.DMA((n_copy,)), pltpu.SemaphoreType.DMA(())],
    )(src)
    return [out], token


def _exchange(srcs, *, scatter, name):
    T = len(srcs)
    n_peer = N_DEV - 1

    def body(*refs):
        src_refs, out_refs = refs[:T], refs[T:2 * T]
        token, send_sems, recv_sems, local_sems = refs[2 * T:]
        me = _mesh_pos()
        my = _flat(me)

        def copy(t, d):
            peer = _flip(me, d)
            return pltpu.make_async_remote_copy(
                src_ref=src_refs[t].at[_flat(peer)] if scatter else src_refs[t],
                dst_ref=out_refs[t].at[my],
                send_sem=send_sems.at[t * n_peer + d - 1], recv_sem=recv_sems.at[t * n_peer + d - 1],
                device_id=peer, device_id_type=pl.DeviceIdType.MESH)

        def arrival(t, d):
            peer = _flip(me, d)
            return pltpu.make_async_remote_copy(
                src_ref=src_refs[t].at[my] if scatter else src_refs[t],
                dst_ref=out_refs[t].at[_flat(peer)],
                send_sem=send_sems.at[t * n_peer + d - 1], recv_sem=recv_sems.at[t * n_peer + d - 1],
                device_id=peer, device_id_type=pl.DeviceIdType.MESH)

        local = [pltpu.make_async_copy(src_refs[t].at[my] if scatter else src_refs[t], out_refs[t].at[my], local_sems.at[t])
                 for t in range(T)]
        for cp in local:
            cp.start()
        sends = [copy(t, d) for d in range(1, N_DEV) for t in range(T)]
        for cp in sends:
            cp.start()
        for d in range(1, N_DEV):
            for t in range(T):
                arrival(t, d).wait_recv()
        for cp in sends:
            cp.wait_send()
        for cp in local:
            cp.wait()
        token[...] = jnp.zeros_like(token)

    outs = pl.pallas_call(
        body, name=name,
        in_specs=[pl.BlockSpec(memory_space=pl.ANY)] * T,
        out_specs=[pl.BlockSpec(memory_space=pl.ANY)] * T + [pl.BlockSpec(memory_space=pltpu.VMEM)],
        out_shape=[jax.ShapeDtypeStruct((N_DEV,) + tuple(s.shape[-2:]), s.dtype) for s in srcs] + [jax.ShapeDtypeStruct((8, LANES), F32)],
        scratch_shapes=[pltpu.SemaphoreType.DMA((T * n_peer,)), pltpu.SemaphoreType.DMA((T * n_peer,)), pltpu.SemaphoreType.DMA((T,))],
    )(*srcs)
    return list(outs[:T]), outs[T]


HBM_SPEC = pl.BlockSpec(memory_space=pltpu.HBM)
SEM_SPEC = pl.BlockSpec(memory_space=pltpu.SEMAPHORE)


def _split_copies(src_refs, land_refs, send_sems, recv_sems, scatter):
    me = _mesh_pos()
    my = _flat(me)
    n_peer = N_DEV - 1
    out = []
    for d in range(1, N_DEV):
        peer = _flip(me, d)
        for t, (src, land) in enumerate(zip(src_refs, land_refs)):
            k = t * n_peer + d - 1
            start = pltpu.make_async_remote_copy(
                src_ref=src.at[_flat(peer)] if scatter else src, dst_ref=land.at[my],
                send_sem=send_sems.at[k], recv_sem=recv_sems.at[k], device_id=peer, device_id_type=pl.DeviceIdType.MESH)
            arrive = pltpu.make_async_remote_copy(
                src_ref=src.at[my] if scatter else src, dst_ref=land.at[_flat(peer)],
                send_sem=send_sems.at[k], recv_sem=recv_sems.at[k], device_id=peer, device_id_type=pl.DeviceIdType.MESH)
            out.append((start, arrive))
    return out


def _local_copies(src_refs, land_refs, local_sems, scatter):
    my = _flat(_mesh_pos())
    return [pltpu.make_async_copy(src.at[my] if scatter else src, land.at[my], local_sems.at[t])
            for t, (src, land) in enumerate(zip(src_refs, land_refs))]


def _exchange_start(srcs, *, scatter, name):
    T = len(srcs)
    n_sem = T * (N_DEV - 1)
    lands = [lax.empty((N_DEV,) + tuple(s.shape[-2:]), s.dtype) for s in srcs]

    def body(*refs):
        src_refs, land_refs = refs[:T], refs[T:2 * T]
        send_sems, recv_sems, local_sems = refs[2 * T:2 * T + 3]
        token = refs[-1]
        for cp in _local_copies(src_refs, land_refs, local_sems, scatter):
            cp.start()
        for start, _ in _split_copies(src_refs, land_refs, send_sems, recv_sems, scatter):
            start.start()
        token[...] = jnp.zeros_like(token)

    hbm = lambda a: pltpu.HBM(a.shape, a.dtype)
    outs = pl.pallas_call(
        body, name=name,
        in_specs=[HBM_SPEC] * (2 * T),
        out_specs=(SEM_SPEC, SEM_SPEC, SEM_SPEC, *[HBM_SPEC] * (2 * T), pl.BlockSpec(memory_space=pltpu.VMEM)),
        out_shape=(pltpu.SemaphoreType.DMA((n_sem,)), pltpu.SemaphoreType.DMA((n_sem,)), pltpu.SemaphoreType.DMA((T,)),
                   *[hbm(s) for s in srcs], *[hbm(a) for a in lands], jax.ShapeDtypeStruct((8, LANES), F32)),
        input_output_aliases={t: 3 + t for t in range(2 * T)},
        compiler_params=pltpu.CompilerParams(has_side_effects=pltpu.SideEffectType.DATAFLOW_SIDE_EFFECTING),
    )(*[pltpu.with_memory_space_constraint(s, pltpu.HBM) for s in srcs],
      *[pltpu.with_memory_space_constraint(a, pltpu.HBM) for a in lands])
    return (outs[0], outs[1], outs[2], list(outs[3:3 + T]), list(outs[3 + T:3 + 2 * T])), outs[-1]


def _exchange_wait(handle, after, *, scatter, name):
    send_sems, recv_sems, local_sems, srcs, lands = handle
    T = len(srcs)

    def body(*refs):
        src_refs, land_refs = refs[:T], refs[T:2 * T]
        s_sems, r_sems, l_sems = refs[2 * T:2 * T + 3]
        for cp in _local_copies(src_refs, land_refs, l_sems, scatter):
            cp.wait()
        for start, arrive in _split_copies(src_refs, land_refs, s_sems, r_sems, scatter):
            start.wait_send()
            arrive.wait_recv()

    hbm = lambda a: pltpu.HBM(a.shape, a.dtype)
    outs = pl.pallas_call(
        body, name=name,
        in_specs=[HBM_SPEC] * (2 * T) + [SEM_SPEC, SEM_SPEC, SEM_SPEC, pl.BlockSpec(memory_space=pl.ANY)],
        out_specs=[HBM_SPEC] * (2 * T),
        out_shape=[hbm(s) for s in srcs] + [hbm(a) for a in lands],
        input_output_aliases={t: t for t in range(2 * T)},
        compiler_params=pltpu.CompilerParams(has_side_effects=pltpu.SideEffectType.DATAFLOW_SIDE_EFFECTING),
    )(*srcs, *lands, send_sems, recv_sems, local_sems, after)
    return list(outs[T:])


def _adamw(w, slots, m, v, *, name):
    R, C = w.shape
    row_tiles = [t for t in range(16, 513, 16) if R % t == 0]
    tr = R if R <= 512 or not row_tiles else max(row_tiles)
    tc = C if tr < R or R <= 512 else LANES
    assert R % tr == 0 and C % tc == 0, (R, C, tr, tc)

    def body(w_ref, s_ref, m_ref, v_ref, g_ref, d_ref, m2_ref, v2_ref):
        gv = s_ref[0].astype(F32)
        for j in range(1, N_DEV):
            gv = gv + s_ref[j].astype(F32)
        g_ref[...] = gv
        m2 = ADAM_B1 * m_ref[...] + (1.0 - ADAM_B1) * gv
        v2 = ADAM_B2 * v_ref[...] + (1.0 - ADAM_B2) * (gv * gv)
        m_hat = m2 / (1.0 - ADAM_B1 ** ADAM_STEP)
        v_hat = v2 / (1.0 - ADAM_B2 ** ADAM_STEP)
        d_ref[...] = -ADAM_LR * (m_hat / (jnp.sqrt(v_hat) + ADAM_EPS) + ADAM_WD * w_ref[...])
        m2_ref[...] = m2
        v2_ref[...] = v2

    blk = pl.BlockSpec((tr, tc), lambda i, j: (i, j))
    sds = jax.ShapeDtypeStruct((R, C), F32)
    return pl.pallas_call(
        body, name=name, grid=(R // tr, C // tc),
        in_specs=[blk, pl.BlockSpec((N_DEV, tr, tc), lambda i, j: (0, i, j)), blk, blk],
        out_specs=[blk] * 4, out_shape=[sds] * 4,
        compiler_params=_cparams(("parallel", "parallel")),
    )(w, slots, m, v)


SHARDED = (
    ("w_in", (1154, 1024), "row"), ("conv_w", (31, 128), "col"), ("w_kv_mem", (1024, 256), "col"),
    ("w_proj_attn", (128, 1024), "row"), ("w_proj_conv", (128, 1024), "row"), ("w_proj_mem", (128, 1024), "row"),
    ("w_out", (128, 1024), "row"), ("w_gate_up", (704, 1024), "row"), ("w_down", (352, 1024), "row"),
)
TRANSPOSED = ("w_in", "w_gate_up")
SMALL = ("norm_mix_pre", "norm_mix_post", "norm_mem", "conv_b", "conv_ln_g", "conv_ln_b", "norm_ffn_pre", "norm_ffn_post", "b_forget")
SMALL_ROWS = 16
LOSS_ROW = len(SMALL)
WEIGHT_ORDER = ("norm_mix_pre", "norm_mix_post", "norm_mem", "w_in", "b_forget", "conv_w", "conv_b", "conv_ln_g", "conv_ln_b",
                "w_kv_mem", "w_proj_attn", "w_proj_conv", "w_proj_mem", "w_out", "norm_ffn_pre", "norm_ffn_post", "w_gate_up", "w_down")


def _rows(pieces, lo, hi):
    out, start = [], 0
    for a, n in pieces:
        a0, a1 = max(lo, start), min(hi, start + n)
        if a0 < a1:
            out.append(a[a0 - start:a1 - start])
        start += n
    return out


def _to_full(blocks8, kind):
    n, r, c = blocks8.shape
    if kind == "col":
        return jnp.concatenate([blocks8[j] for j in range(n)], axis=1)
    return blocks8.reshape(n * r, c)


def _to_blocks(full, kind):
    if kind == "col":
        c = full.shape[1] // N_DEV
        return jnp.stack([full[:, j * c:(j + 1) * c] for j in range(N_DEV)])
    nr, c = full.shape
    return full.reshape(N_DEV, nr // N_DEV, c)


def kernel(x, mem, norm_mix_pre, norm_mix_post, norm_mem, w_in, b_forget, conv_w, conv_b, conv_ln_g, conv_ln_b, w_kv_mem, w_proj_attn, w_proj_conv, w_proj_mem, w_out, norm_ffn_pre, norm_ffn_post, w_gate_up, w_down, loss_target, m_norm_mix_pre, m_norm_mix_post, m_norm_mem, m_w_in, m_b_forget, m_conv_w, m_conv_b, m_conv_ln_g, m_conv_ln_b, m_w_kv_mem, m_w_proj_attn, m_w_proj_conv, m_w_proj_mem, m_w_out, m_norm_ffn_pre, m_norm_ffn_post, m_w_gate_up, m_w_down, v_norm_mix_pre, v_norm_mix_post, v_norm_mem, v_w_in, v_b_forget, v_conv_w, v_conv_b, v_conv_ln_g, v_conv_ln_b, v_w_kv_mem, v_w_proj_attn, v_w_proj_conv, v_w_proj_mem, v_w_out, v_norm_ffn_pre, v_norm_ffn_post, v_w_gate_up, v_w_down):
    given = dict(locals())
    weights = {n: given[n] for n in WEIGHT_ORDER}
    moments_m = {n: given["m_" + n] for n in WEIGHT_ORDER}
    moments_v = {n: given["v_" + n] for n in WEIGHT_ORDER}
    kind = {name: k for name, _, k in SHARDED}
    names = [name for name, _, _ in SHARDED]

    def local(a, n):
        return jnp.swapaxes(a[0], 0, 1) if n in TRANSPOSED else a[0]

    blocks = {n: local(weights[n], n).astype(F32 if n == "conv_w" else BF16) for n in names}
    late = [n for n in names if n != "w_in"]
    (w_in_blocks,), w_in_token = _gather_two_level(blocks["w_in"], name="gather_w_in")
    blocks["w_kv_mem"] = _after(w_in_token, blocks["w_kv_mem"])
    late_handle, late_token = _exchange_start([blocks[n] for n in late], scatter=False, name="gather_rest_start")
    n_qkv = 3 * D_MODEL
    w_in_t = _to_full(w_in_blocks, "row")
    w_early = dict(
        wp_t=jnp.concatenate([w_in_t[:n_qkv], w_in_t[n_qkv + FOX_HEADS:]], axis=0),
        wf_t=jnp.pad(w_in_t[n_qkv:n_qkv + FOX_HEADS], ((0, LANES - FOX_HEADS), (0, 0))),
    )

    def late_weights(after):
        got = dict(zip(late, _exchange_wait(late_handle, after, scatter=False, name="gather_rest_wait")))
        full = {n: _to_full(got[n], kind[n]) for n in late if n != "conv_w"}
        return dict(
            conv_w=got["conv_w"],
            w_kv=full["w_kv_mem"], wpa=full["w_proj_attn"], wpc=full["w_proj_conv"], wpm=full["w_proj_mem"],
            w_out=full["w_out"], w_gu_t=full["w_gate_up"], w_down=full["w_down"],
        )

    sent = []

    def send_grads(group, g):
        order = [n for n in names if n in g]
        per_owner = [g[n] if g[n].ndim == 3 else _to_blocks(g[n], kind[n]) for n in order]
        handle, token = _exchange_start(per_owner, scatter=True, name="scatter_" + group + "_start")
        sent.append((group, order, handle))
        return token

    gains = {n: weights[n] for n in SMALL}
    gains["norm_mix_pre"] = _after(late_token, gains["norm_mix_pre"])

    loss_tile, grad_x, small = _local_step(x[0], mem[0], loss_target[0], gains, w_early, late_weights, send_grads)

    g_recv = {}
    for group, order, handle in sent:
        g_recv.update(zip(order, _exchange_wait(handle, grad_x, scatter=True, name="scatter_" + group + "_wait")))

    rows = [jnp.pad(small[n], ((0, 0), (0, D_MODEL - small[n].shape[1]))) for n in SMALL]
    rows.append(jnp.broadcast_to(loss_tile[0:1, 0:1], (1, D_MODEL)))
    rows.append(jnp.zeros((SMALL_ROWS - len(rows), D_MODEL), F32))
    (small_recv,), _ = _exchange([jnp.concatenate(rows, axis=0)], scatter=False, name="gather_small")

    def slab_of(d, fill):
        rows = [jnp.pad(d[n], ((0, 0), (0, D_MODEL - d[n].shape[1])), constant_values=fill) for n in SMALL]
        rows.append(jnp.full((SMALL_ROWS - len(rows), D_MODEL), fill, F32))
        return jnp.concatenate(rows, axis=0)

    grads, delta, new_m, new_v = {}, {}, {}, {}
    sg, sd, sm, sv = _adamw(slab_of(weights, 0.0), small_recv, slab_of(moments_m, 0.0), slab_of(moments_v, 1.0), name="adamw_small")
    loss = sg[LOSS_ROW, 0]
    for i, n in enumerate(SMALL):
        c = weights[n].shape[1]
        grads[n], delta[n], new_m[n], new_v[n] = sg[i:i + 1, :c], sd[i:i + 1, :c], sm[i:i + 1, :c], sv[i:i + 1, :c]
    for n in names:
        outs = _adamw(local(weights[n], n), g_recv[n], local(moments_m[n], n), local(moments_v[n], n), name="adamw_" + n)
        grads[n], delta[n], new_m[n], new_v[n] = [(jnp.swapaxes(o, 0, 1) if n in TRANSPOSED else o)[None] for o in outs]

    return (loss, grad_x[None], *[grads[n] for n in WEIGHT_ORDER], *[delta[n] for n in WEIGHT_ORDER],
            *[new_m[n] for n in WEIGHT_ORDER], *[new_v[n] for n in WEIGHT_ORDER])
```

```python
import functools

import jax
import jax.numpy as jnp
from jax import lax
from jax.experimental import pallas as pl
from jax.experimental.pallas import tpu as pltpu

F32 = jnp.float32
BF16 = jnp.bfloat16

D_MODEL = 1024
N_DEV = 8
FOX_HEADS = 16
FOX_HEAD_DIM = 64
HEAD_PAIRS = FOX_HEADS // 2
MEM_HEADS = 4
MEM_HEAD_DIM = D_MODEL // MEM_HEADS
CONV_WIDTH = 31
CONV_HALO = 32
CONV_CHUNK = 16
SUBLANES = 8
FFN_HIDDEN = 2816
RMS_EPS = 1e-6
LN_EPS = 1e-5
ADAM_LR = 0.001
ADAM_B1 = 0.9
ADAM_B2 = 0.999
ADAM_EPS = 1e-08
ADAM_WD = 0.01
ADAM_STEP = 10
KV_STEP = 2
KV_STEP_FWD = 4
NEG_BIG = -1e30
LANES = 128

PB_Q, PB_K, PB_V, PB_A, PB_GATE, PB_QMEM, PB_G0 = 0, 1, 2, 3, 4, 5, 6
P_WIDTH = 9 * D_MODEL

NT_DIMS = (((1,), (1,)), ((), ()))
TN_DIMS = (((0,), (0,)), ((), ()))


def _cparams(sem, vmem_mb=None):
    kw = dict(dimension_semantics=sem)
    if vmem_mb is not None:
        kw["vmem_limit_bytes"] = vmem_mb * 1024 * 1024
    return pltpu.CompilerParams(**kw)


def _tile(dim, want):
    t = min(dim, want)
    assert dim % t == 0, (dim, want)
    return t


def _sigmoid(z):
    return 1.0 / (1.0 + jnp.exp(-z))


def _matmul(a, b, *, mode, out_dtype, name, tm=1024, tn=1024, tk=1024, after=None):
    if mode == "nn":
        (M, K), (K2, N) = a.shape, b.shape
    elif mode == "nt":
        (M, K), (N, K2) = a.shape, b.shape
    else:
        (K, M), (K2, N) = a.shape, b.shape
    assert K == K2, (a.shape, b.shape, mode)
    tm, tn, tk = _tile(M, tm), _tile(N, tn), _tile(K, tk)
    nk = K // tk
    dims = {"nn": (((1,), (0,)), ((), ())), "nt": NT_DIMS, "tn": TN_DIMS}[mode]

    n_extra = 0 if after is None else 1

    def body(a_ref, b_ref, *rest):
        o_ref, scratch = rest[n_extra], rest[n_extra + 1:]
        part = lax.dot_general(a_ref[...], b_ref[...], dims, preferred_element_type=F32)
        if nk == 1:
            o_ref[...] = part.astype(o_ref.dtype)
        else:
            acc_ref, = scratch
            k = pl.program_id(2)

            @pl.when(k == 0)
            def _():
                acc_ref[...] = part

            @pl.when(k > 0)
            def _():
                acc_ref[...] += part

            @pl.when(k == nk - 1)
            def _():
                o_ref[...] = acc_ref[...].astype(o_ref.dtype)

    a_spec = pl.BlockSpec((tk, tm), lambda j, i, k: (k, i)) if mode == "tn" else pl.BlockSpec((tm, tk), lambda j, i, k: (i, k))
    b_spec = pl.BlockSpec((tn, tk), lambda j, i, k: (j, k)) if mode == "nt" else pl.BlockSpec((tk, tn), lambda j, i, k: (k, j))
    return pl.pallas_call(
        body,
        name=name,
        grid=(N // tn, M // tm, nk),
        in_specs=[a_spec, b_spec] + [pl.BlockSpec((8, LANES), lambda j, i, k: (0, 0))] * n_extra,
        out_specs=pl.BlockSpec((tm, tn), lambda j, i, k: (i, j)),
        out_shape=jax.ShapeDtypeStruct((M, N), out_dtype),
        scratch_shapes=[pltpu.VMEM((tm, tn), F32)] if nk > 1 else [],
        compiler_params=_cparams(("parallel", "parallel", "arbitrary"), 56),
    )(a, b, *([] if after is None else [after]))


def _matmul_pieces(pieces, b, *, mode, out_dtype, name, after=None):
    blk = 1024
    T = len(pieces)
    S = pieces[0].shape[0]
    counts = [p.shape[1] // blk for p in pieces]
    offs = [sum(counts[:t]) for t in range(T)]
    n_a = sum(counts)
    N = b.shape[1]
    n_extra = 0 if after is None else 1
    if mode == "nn":
        tm = _tile(S, blk)
        grid, n_red = (S // tm, n_a), n_a
        which = lambda i, k: k
        a_specs = [pl.BlockSpec((tm, blk), lambda i, k, o=o, c=c: (i, jnp.clip(k - o, 0, c - 1))) for o, c in zip(offs, counts)]
        b_spec = pl.BlockSpec((blk, N), lambda i, k: (k, 0))
        out_spec, out_rows, dims = pl.BlockSpec((tm, N), lambda i, k: (i, 0)), S, (((1,), (0,)), ((), ()))
    else:
        tk = _tile(S, blk)
        grid, n_red = (n_a, S // tk), S // tk
        which = lambda m, k: m

        def a_index(m, k, o, c):
            mine = (m >= o) & (m < o + c)
            return jnp.where(mine, k, 0), jnp.clip(m - o, 0, c - 1)

        a_specs = [pl.BlockSpec((tk, blk), functools.partial(a_index, o=o, c=c)) for o, c in zip(offs, counts)]
        b_spec = pl.BlockSpec((tk, N), lambda m, k: (k, 0))
        out_spec, out_rows, dims = pl.BlockSpec((blk, N), lambda m, k: (m, 0)), n_a * blk, TN_DIMS

    def body(*refs):
        a_refs, b_ref = refs[:T], refs[T]
        o_ref, acc_ref = refs[T + 1 + n_extra], refs[T + 2 + n_extra]
        a_blk = which(pl.program_id(0), pl.program_id(1))
        k = pl.program_id(1)
        for t in range(T):
            @pl.when((a_blk >= offs[t]) & (a_blk < offs[t] + counts[t]))
            def _(t=t):
                part = lax.dot_general(a_refs[t][...], b_ref[...], dims, preferred_element_type=F32)

                @pl.when(k == 0)
                def _():
                    acc_ref[...] = part

                @pl.when(k > 0)
                def _():
                    acc_ref[...] += part

        @pl.when(k == n_red - 1)
        def _():
            o_ref[...] = acc_ref[...].astype(o_ref.dtype)

    return pl.pallas_call(
        body, name=name, grid=grid,
        in_specs=a_specs + [b_spec] + [pl.BlockSpec((8, LANES), lambda i, k: (0, 0))] * n_extra,
        out_specs=out_spec,
        out_shape=jax.ShapeDtypeStruct((out_rows, N), out_dtype),
        scratch_shapes=[pltpu.VMEM(out_spec.block_shape, F32)],
        compiler_params=_cparams(("parallel", "arbitrary"), 56),
    )(*pieces, b, *([] if after is None else [after]))


def _rms_fwd(x, g, *, name, tm=512, proj_t=None):
    S, D = x.shape
    tm = _tile(S, tm)
    has_proj = proj_t is not None

    def body(x_ref, g_ref, *rest):
        o_ref = rest[int(has_proj)]
        xv = x_ref[...]
        rstd = lax.rsqrt(jnp.mean(xv * xv, axis=-1, keepdims=True) + RMS_EPS)
        hb = (xv * rstd * g_ref[...]).astype(o_ref.dtype)
        o_ref[...] = hb
        if has_proj:
            rest[2][...] = lax.dot_general(hb, rest[0][...], NT_DIMS, preferred_element_type=F32)

    row = pl.BlockSpec((tm, D), lambda i: (i, 0))
    outs = pl.pallas_call(
        body, name=name, grid=(S // tm,),
        in_specs=[row, pl.BlockSpec((1, D), lambda i: (0, 0))] + ([pl.BlockSpec(proj_t.shape, lambda i: (0, 0))] if has_proj else []),
        out_specs=[row] + ([pl.BlockSpec((tm, proj_t.shape[0]), lambda i: (i, 0))] if has_proj else []),
        out_shape=[jax.ShapeDtypeStruct((S, D), BF16)] + ([jax.ShapeDtypeStruct((S, proj_t.shape[0]), F32)] if has_proj else []),
        compiler_params=_cparams(("parallel",)),
    )(x, g, *([proj_t] if has_proj else []))
    return tuple(outs) if has_proj else outs[0]


def _rms_bwd(xin, dys, g, res, *, out_dtype, name, tm=512, proj=None):
    S, D = xin.shape
    tm = _tile(S, tm)
    n_dy = len(dys)
    has_res = res is not None
    n_proj = 0 if proj is None else 2

    def body(*refs):
        x_ref, g_ref = refs[0], refs[1]
        dy_refs = refs[2:2 + n_dy]
        pos = 2 + n_dy
        res_ref = refs[pos] if has_res else None
        pos += int(has_res)
        proj_refs = refs[pos:pos + n_proj]
        pos += n_proj
        dx_ref, dg_ref = refs[pos], refs[pos + 1]
        i = pl.program_id(0)
        xv = x_ref[...]
        dy = dy_refs[0][...].astype(F32)
        for r in dy_refs[1:]:
            dy = dy + r[...].astype(F32)
        if n_proj:
            dy = dy + jnp.dot(proj_refs[0][...], proj_refs[1][...], preferred_element_type=F32)
        rstd = lax.rsqrt(jnp.mean(xv * xv, axis=-1, keepdims=True) + RMS_EPS)
        xhat = xv * rstd
        gy = dy * g_ref[...]
        dx = rstd * (gy - xhat * jnp.mean(gy * xhat, axis=-1, keepdims=True))
        if has_res:
            dx = dx + res_ref[...]
        dx_ref[...] = dx.astype(dx_ref.dtype)
        part = jnp.sum(dy * xhat, axis=0, keepdims=True)

        @pl.when(i == 0)
        def _():
            dg_ref[...] = part

        @pl.when(i > 0)
        def _():
            dg_ref[...] += part

    row = pl.BlockSpec((tm, D), lambda i: (i, 0))
    vec = pl.BlockSpec((1, D), lambda i: (0, 0))
    ins = [xin, g] + list(dys) + ([res] if has_res else []) + (list(proj) if n_proj else [])
    proj_specs = [] if not n_proj else [pl.BlockSpec((tm, proj[0].shape[1]), lambda i: (i, 0)),
                                        pl.BlockSpec(proj[1].shape, lambda i: (0, 0))]
    return pl.pallas_call(
        body, name=name, grid=(S // tm,),
        in_specs=[row, vec] + [row] * n_dy + ([row] if has_res else []) + proj_specs,
        out_specs=[row, vec],
        out_shape=[jax.ShapeDtypeStruct((S, D), out_dtype), jax.ShapeDtypeStruct((1, D), F32)],
        compiler_params=_cparams(("arbitrary",)),
    )(*ins)


def _rms_bwd_pair(x1, dh2, dout, g_pre, y1, g_post, *, name, tm=512):
    S, D = x1.shape
    tm = _tile(S, tm)

    def norm_bwd(xv, dy, g):
        rstd = lax.rsqrt(jnp.mean(xv * xv, axis=-1, keepdims=True) + RMS_EPS)
        xhat = xv * rstd
        gy = dy * g
        return rstd * (gy - xhat * jnp.mean(gy * xhat, axis=-1, keepdims=True)), jnp.sum(dy * xhat, axis=0, keepdims=True)

    def body(x_ref, dh_ref, do_ref, gq_ref, y_ref, gp_ref, dx_ref, dy_ref, dgq_ref, dgp_ref):
        i = pl.program_id(0)
        dx, part_q = norm_bwd(x_ref[...], dh_ref[...], gq_ref[...])
        dx1 = dx + do_ref[...]
        dx_ref[...] = dx1
        dy1, part_p = norm_bwd(y_ref[...], dx1, gp_ref[...])
        dy_ref[...] = dy1.astype(dy_ref.dtype)

        @pl.when(i == 0)
        def _():
            dgq_ref[...] = part_q
            dgp_ref[...] = part_p

        @pl.when(i > 0)
        def _():
            dgq_ref[...] += part_q
            dgp_ref[...] += part_p

    row = pl.BlockSpec((tm, D), lambda i: (i, 0))
    vec = pl.BlockSpec((1, D), lambda i: (0, 0))
    return pl.pallas_call(
        body, name=name, grid=(S // tm,),
        in_specs=[row, row, row, vec, row, vec], out_specs=[row, row, vec, vec],
        out_shape=[jax.ShapeDtypeStruct((S, D), F32), jax.ShapeDtypeStruct((S, D), BF16),
                   jax.ShapeDtypeStruct((1, D), F32), jax.ShapeDtypeStruct((1, D), F32)],
        compiler_params=_cparams(("arbitrary",)),
    )(x1, dh2, dout, g_pre, y1, g_post)


def _head_mask(hh, shape):
    lane = lax.broadcasted_iota(jnp.int32, shape, len(shape) - 1)
    return (lane // FOX_HEAD_DIM) == hh


def _block_constants(cr_ref, i, tq):
    first = cr_ref[0, :, pl.ds(pl.multiple_of(i * tq, tq), LANES)]
    return [first[hh:hh + 1, 0:1] for hh in range(2)]


def _attn_fwd(P, crow, *, tq, name):
    S = P.shape[0]
    tq = _tile(S, tq)
    nq = S // tq
    scale = FOX_HEAD_DIM ** -0.5

    def body(q_ref, k_ref, v_ref, cr_ref, o_ref, o32_ref, lse_ref):
        i = pl.program_id(1)
        q = q_ref[...] * jnp.asarray(scale, BF16)
        row = lax.broadcasted_iota(jnp.int32, (tq, tq), 0)
        col = lax.broadcasted_iota(jnp.int32, (tq, tq), 1)
        causal = col <= row
        hms = [_head_mask(hh, (tq, LANES)) for hh in range(2)]
        qhs = [jnp.where(hm, q, jnp.zeros_like(q)) for hm in hms]
        cbs = _block_constants(cr_ref, i, tq)

        def step(block, n_blocks, carry, masked):
            tk = n_blocks * tq
            start = pl.multiple_of(block * tq, tk)
            kj = k_ref[pl.ds(start, tk), :]
            vj = v_ref[pl.ds(start, tk), :]
            lane_k = lax.broadcasted_iota(jnp.int32, (tk, LANES), 1)
            new = []
            for hh in range(2):
                m, acc = carry[hh]
                bias = cbs[hh] - cr_ref[0, hh:hh + 1, pl.ds(start, tk)]
                s = lax.dot_general(qhs[hh], kj, NT_DIMS, preferred_element_type=F32) + bias
                if masked:
                    s = jnp.where(causal, s, NEG_BIG)
                m_new = jnp.maximum(m, jnp.max(s, axis=1, keepdims=True))
                alpha = jnp.exp(m - m_new)
                p = jnp.exp(s - m_new)
                vh = jnp.where(lane_k == (1 - hh) * FOX_HEAD_DIM, jnp.ones_like(vj), vj)
                acc = alpha * acc + jnp.dot(p.astype(BF16), vh, preferred_element_type=F32)
                new.append((m_new, acc))
            return tuple(new)

        lane = lax.broadcasted_iota(jnp.int32, (tq, LANES), 1)
        ones_lane = [lane == (1 - hh) * FOX_HEAD_DIM for hh in range(2)]
        init = (jnp.full((tq, 1), NEG_BIG, F32), jnp.zeros((tq, LANES), F32))
        n_wide = i // KV_STEP_FWD
        done = n_wide * KV_STEP_FWD
        n_pair = (i - done) // 2
        carry = lax.fori_loop(0, n_wide, lambda j, c: step(j * KV_STEP_FWD, KV_STEP_FWD, c, False), (init, init))
        carry = lax.fori_loop(0, n_pair, lambda j, c: step(done + 2 * j, 2, c, False), carry)
        carry = lax.fori_loop(done + 2 * n_pair, i, lambda j, c: step(j, 1, c, False), carry)
        carry = step(i, 1, carry, True)
        ls = [jnp.sum(jnp.where(ones_lane[hh], carry[hh][1], 0.0), axis=1, keepdims=True) for hh in range(2)]
        out = jnp.where(hms[0], carry[0][1] / ls[0], carry[1][1] / ls[1])
        for hh in range(2):
            lse_ref[hh] = carry[hh][0] + jnp.log(ls[hh])
        o_ref[...] = out.astype(o_ref.dtype)
        o32_ref[...] = out

    nblk = D_MODEL // LANES
    return pl.pallas_call(
        body, name=name, grid=(HEAD_PAIRS, nq),
        in_specs=[
            pl.BlockSpec((tq, LANES), lambda p, i: (i, PB_Q * nblk + p)),
            pl.BlockSpec((S, LANES), lambda p, i: (0, PB_K * nblk + p)),
            pl.BlockSpec((S, LANES), lambda p, i: (0, PB_V * nblk + p)),
            pl.BlockSpec((1, 2, S), lambda p, i: (p, 0, 0)),
        ],
        out_specs=[
            pl.BlockSpec((tq, LANES), lambda p, i: (i, p)),
            pl.BlockSpec((tq, LANES), lambda p, i: (i, p)),
            pl.BlockSpec((2, tq, 1), lambda p, i: (p, i, 0)),
        ],
        out_shape=[jax.ShapeDtypeStruct((S, D_MODEL), BF16), jax.ShapeDtypeStruct((S, D_MODEL), F32),
                   jax.ShapeDtypeStruct((FOX_HEADS, S, 1), F32)],
        compiler_params=_cparams(("parallel", "arbitrary"), 56),
    )(P, P, P, crow)


def _attn_bwd(P, do, o32, lse, crow, *, tq, name):
    S = P.shape[0]
    tq = _tile(S, tq)
    nq = S // tq
    scale = FOX_HEAD_DIM ** -0.5

    def body(q_ref, k_ref, v_ref, do_ref, o_ref, lse_ref, cr_ref, dq_ref, dk_out, dv_out, dc_ref, dcq_ref, dk_ref, dv_ref):
        i = pl.program_id(1)

        @pl.when(i == 0)
        def _():
            dk_ref[...] = jnp.zeros_like(dk_ref)
            dv_ref[...] = jnp.zeros_like(dv_ref)
            dc_ref[...] = jnp.zeros_like(dc_ref)

        q = q_ref[...] * jnp.asarray(scale, BF16)
        do_v = do_ref[...]
        row = lax.broadcasted_iota(jnp.int32, (tq, tq), 0)
        col = lax.broadcasted_iota(jnp.int32, (tq, tq), 1)
        causal = col <= row
        hms = [_head_mask(hh, (tq, LANES)) for hh in range(2)]
        qhs = [jnp.where(hm, q, jnp.zeros_like(q)) for hm in hms]
        dohs = [jnp.where(hm, do_v, jnp.zeros_like(do_v)) for hm in hms]
        cbs = _block_constants(cr_ref, i, tq)
        lses = [lse_ref[hh] for hh in range(2)]
        prod = do_v.astype(F32) * o_ref[...]
        dls = [jnp.sum(jnp.where(hm, prod, 0.0), axis=1, keepdims=True) for hm in hms]

        def step(block, n_blocks, carry, masked):
            dq_acc, row_sums = carry[0], list(carry[1:])
            tk = n_blocks * tq
            start = pl.multiple_of(block * tq, tk)
            kj = k_ref[pl.ds(start, tk), :]
            vj = v_ref[pl.ds(start, tk), :]
            head_k = [_head_mask(hh, (tk, LANES)) for hh in range(2)]
            dv_part = jnp.zeros((tk, LANES), F32)
            dk_part = jnp.zeros((tk, LANES), F32)
            for hh in range(2):
                bias = cbs[hh] - cr_ref[0, hh:hh + 1, pl.ds(start, tk)]
                s = lax.dot_general(qhs[hh], kj, NT_DIMS, preferred_element_type=F32) + bias
                if masked:
                    s = jnp.where(causal, s, NEG_BIG)
                p = jnp.exp(s - lses[hh])
                dp = lax.dot_general(dohs[hh], vj, NT_DIMS, preferred_element_type=F32)
                ds = p * (dp - dls[hh])
                pb = p.astype(BF16)
                dsb = ds.astype(BF16)
                dv_part = dv_part + lax.dot_general(pb, dohs[hh], TN_DIMS, preferred_element_type=F32)
                dk_part = dk_part + lax.dot_general(dsb, qhs[hh], TN_DIMS, preferred_element_type=F32)
                dc_ref[0, hh:hh + 1, pl.ds(start, tk)] -= jnp.sum(ds, axis=0, keepdims=True)
                row_sums[hh] = row_sums[hh] + jnp.sum(ds, axis=1, keepdims=True)
                kh = jnp.where(head_k[hh], kj, jnp.zeros_like(kj))
                dq_acc = dq_acc + jnp.dot(dsb, kh, preferred_element_type=F32)
            dv_ref[pl.ds(start, tk), :] += dv_part
            dk_ref[pl.ds(start, tk), :] += dk_part
            return (dq_acc, *row_sums)

        zero_col = jnp.zeros((tq, 1), F32)
        n_wide = i // KV_STEP
        carry = lax.fori_loop(0, n_wide, lambda j, c: step(j * KV_STEP, KV_STEP, c, False),
                              (jnp.zeros((tq, LANES), F32), zero_col, zero_col))
        carry = lax.fori_loop(n_wide * KV_STEP, i, lambda j, c: step(j, 1, c, False), carry)
        carry = step(i, 1, carry, True)
        dq_ref[...] = (carry[0] * scale).astype(dq_ref.dtype)
        for hh in range(2):
            dcq_ref[0, hh:hh + 1, :] = jnp.transpose(jnp.broadcast_to(carry[1 + hh], (tq, LANES)))[0:1, :]

        @pl.when(i == nq - 1)
        def _():
            dk_out[...] = dk_ref[...].astype(dk_out.dtype)
            dv_out[...] = dv_ref[...].astype(dv_out.dtype)

    nblk = D_MODEL // LANES
    qblk = pl.BlockSpec((tq, LANES), lambda p, i: (i, p))
    stat = pl.BlockSpec((2, tq, 1), lambda p, i: (p, i, 0))
    full = pl.BlockSpec((S, LANES), lambda p, i: (0, p))
    return pl.pallas_call(
        body, name=name, grid=(HEAD_PAIRS, nq),
        in_specs=[
            pl.BlockSpec((tq, LANES), lambda p, i: (i, PB_Q * nblk + p)),
            pl.BlockSpec((S, LANES), lambda p, i: (0, PB_K * nblk + p)),
            pl.BlockSpec((S, LANES), lambda p, i: (0, PB_V * nblk + p)),
            qblk, qblk, stat,
            pl.BlockSpec((1, 2, S), lambda p, i: (p, 0, 0)),
        ],
        out_specs=[qblk, full, full, pl.BlockSpec((1, 2, S), lambda p, i: (p, 0, 0)),
                   pl.BlockSpec((1, 2, tq), lambda p, i: (p, 0, i))],
        out_shape=[
            jax.ShapeDtypeStruct((S, D_MODEL), BF16),
            jax.ShapeDtypeStruct((S, D_MODEL), BF16),
            jax.ShapeDtypeStruct((S, D_MODEL), BF16),
            jax.ShapeDtypeStruct((HEAD_PAIRS, 2, S), F32),
            jax.ShapeDtypeStruct((HEAD_PAIRS, 2, S), F32),
        ],
        scratch_shapes=[pltpu.VMEM((S, LANES), F32), pltpu.VMEM((S, LANES), F32)],
        compiler_params=_cparams(("parallel", "arbitrary"), 56),
    )(P, P, P, do, o32, lse, crow)


def _cumsum_lanes(xs, *, reverse, name):
    R, S = xs[0].shape
    nb = S // LANES
    n_in = len(xs)

    def body(*refs):
        x_refs, o_ref = refs[:n_in], refs[n_in]
        r = lax.broadcasted_iota(jnp.int32, (LANES, LANES), 0)
        c = lax.broadcasted_iota(jnp.int32, (LANES, LANES), 1)
        tri = ((r >= c) if reverse else (r <= c)).astype(F32)

        def step(b, carry):
            blk = (nb - 1 - b) if reverse else b
            start = pl.multiple_of(blk * LANES, LANES)
            xb = x_refs[0][:, pl.ds(start, LANES)]
            for r in x_refs[1:]:
                xb = xb + r[:, pl.ds(start, LANES)]
            y = jnp.dot(xb, tri, precision=lax.Precision.HIGHEST, preferred_element_type=F32) + carry
            o_ref[:, pl.ds(start, LANES)] = y
            return carry + jnp.sum(xb, axis=1, keepdims=True)

        lax.fori_loop(0, nb, step, jnp.zeros((R, 1), F32))

    return pl.pallas_call(
        body, name=name,
        in_specs=[pl.BlockSpec(memory_space=pltpu.VMEM)] * n_in,
        out_specs=pl.BlockSpec(memory_space=pltpu.VMEM),
        out_shape=jax.ShapeDtypeStruct((R, S), F32),
    )(*xs)


def _forget_fwd(f_logit, b_pad, *, name, tm=1024):
    S = f_logit.shape[0]
    tm = _tile(S, tm)

    def body(f_ref, b_ref, o_ref):
        z = f_ref[...] + b_ref[...]
        o_ref[...] = jnp.minimum(z, 0.0) - jnp.log(1.0 + jnp.exp(-jnp.abs(z)))

    blk = pl.BlockSpec((tm, LANES), lambda i: (i, 0))
    return pl.pallas_call(
        body, name=name, grid=(S // tm,),
        in_specs=[blk, pl.BlockSpec((1, LANES), lambda i: (0, 0))],
        out_specs=blk, out_shape=jax.ShapeDtypeStruct((S, LANES), F32),
        compiler_params=_cparams(("parallel",)),
    )(f_logit, b_pad)


def _forget_bwd(f_logit, b_pad, dlogf, *, name, tm=1024):
    S = f_logit.shape[0]
    tm = _tile(S, tm)

    def body(f_ref, b_ref, d_ref, o_ref, db_ref):
        i = pl.program_id(0)
        z = f_ref[...] + b_ref[...]
        dz = d_ref[...] * (1.0 - _sigmoid(z))
        o_ref[...] = dz.astype(o_ref.dtype)
        part = jnp.sum(dz, axis=0, keepdims=True)

        @pl.when(i == 0)
        def _():
            db_ref[...] = part

        @pl.when(i > 0)
        def _():
            db_ref[...] += part

    blk = pl.BlockSpec((tm, LANES), lambda i: (i, 0))
    vec = pl.BlockSpec((1, LANES), lambda i: (0, 0))
    return pl.pallas_call(
        body, name=name, grid=(S // tm,),
        in_specs=[blk, vec, blk], out_specs=[blk, vec],
        out_shape=[jax.ShapeDtypeStruct((S, LANES), BF16), jax.ShapeDtypeStruct((1, LANES), F32)],
        compiler_params=_cparams(("arbitrary",)),
    )(f_logit, b_pad, dlogf)


def _layernorm_stats(y):
    mu = jnp.mean(y, axis=-1, keepdims=True)
    yc = y - mu
    rstd = lax.rsqrt(jnp.mean(yc * yc, axis=-1, keepdims=True) + LN_EPS)
    return yc * rstd, rstd


def _fill_shifted(src_ref, sh_ref, tm):
    rows = tm + CONV_HALO - SUBLANES
    for s in range(1, SUBLANES):
        sh_ref[s - 1, 0:rows, :] = src_ref[pl.ds(s, rows), :]


def _shifted_rows(src_ref, sh_ref, r0, offset, chunk=CONV_CHUNK):
    s, base = offset % SUBLANES, offset - offset % SUBLANES
    rows = pl.ds(pl.multiple_of(r0 + base, SUBLANES), chunk)
    return src_ref[rows, :] if s == 0 else sh_ref[s - 1, rows, :]


def _fill_tap_rows(w_ref, wb_ref):
    for k in range(CONV_WIDTH):
        for j in range(N_DEV):
            wb_ref[k * SUBLANES:(k + 1) * SUBLANES, j * LANES:(j + 1) * LANES] = jnp.broadcast_to(
                w_ref[j, k:k + 1, :], (SUBLANES, LANES))


def _tap_sum(wb_ref, rows_of_tap, chunk=CONV_CHUNK):
    n = chunk // SUBLANES
    accs = [None] * n
    for k in range(CONV_WIDTH):
        wk = wb_ref[k * SUBLANES:(k + 1) * SUBLANES, :]
        src = rows_of_tap(k)
        for h in range(n):
            term = wk * src[h * SUBLANES:(h + 1) * SUBLANES]
            accs[h] = term if accs[h] is None else accs[h] + term
    return jnp.concatenate(accs, axis=0)


def _conv_fwd(P, w_pad, conv_b, ln_g, ln_b, *, name, tm=256):
    S = P.shape[0]
    C = D_MODEL
    tm = _tile(S, tm)
    hb = tm // CONV_HALO

    def body(a_ref, gt_ref, ah_ref, gh_ref, w_ref, cb_ref, g_ref, b_ref, o_ref, y_ref, glu_ref, sh_ref, wb_ref):
        i = pl.program_id(0)
        halo = ah_ref[...].astype(F32) * _sigmoid(gh_ref[...].astype(F32))
        glu_ref[0:CONV_HALO, :] = jnp.where(i > 0, halo, 0.0)
        glu_ref[CONV_HALO:, :] = a_ref[...].astype(F32) * _sigmoid(gt_ref[...].astype(F32))
        _fill_shifted(glu_ref, sh_ref, tm)
        _fill_tap_rows(w_ref, wb_ref)

        rc = 2 * CONV_CHUNK

        def chunk(r, carry):
            r0 = pl.multiple_of(r * rc, rc)
            acc = _tap_sum(wb_ref, lambda k: _shifted_rows(glu_ref, sh_ref, r0, CONV_HALO - (CONV_WIDTH - 1) + k, rc), rc)
            y_ref[pl.ds(r0, rc), :] = acc + cb_ref[...]
            return carry

        lax.fori_loop(0, tm // rc, chunk, 0)
        xhat, _ = _layernorm_stats(y_ref[...])
        z = xhat * g_ref[...] + b_ref[...]
        o_ref[...] = (z * _sigmoid(z)).astype(o_ref.dtype)

    vec = pl.BlockSpec((1, C), lambda i: (0, 0))
    row = pl.BlockSpec((tm, C), lambda i: (i, 0))
    return pl.pallas_call(
        body, name=name, grid=(S // tm,),
        in_specs=[
            pl.BlockSpec((tm, C), lambda i: (i, PB_A)),
            pl.BlockSpec((tm, C), lambda i: (i, PB_GATE)),
            pl.BlockSpec((CONV_HALO, C), lambda i: (jnp.maximum(i * hb - 1, 0), PB_A)),
            pl.BlockSpec((CONV_HALO, C), lambda i: (jnp.maximum(i * hb - 1, 0), PB_GATE)),
            pl.BlockSpec((N_DEV, CONV_WIDTH, LANES), lambda i: (0, 0, 0)),
            vec, vec, vec,
        ],
        out_specs=[row, row],
        out_shape=[jax.ShapeDtypeStruct((S, C), BF16), jax.ShapeDtypeStruct((S, C), F32)],
        scratch_shapes=[pltpu.VMEM((tm + CONV_HALO, C), F32), pltpu.VMEM((SUBLANES - 1, tm + CONV_HALO, C), F32),
                        pltpu.VMEM((CONV_HALO * SUBLANES, C), F32)],
        compiler_params=_cparams(("parallel",), 56),
    )(P, P, P, P, w_pad, conv_b, ln_g, ln_b)


def _conv_bwd(P, do, y, w_pad, ln_g, ln_b, *, name, tm=256):
    S = P.shape[0]
    C = D_MODEL
    tm = _tile(S, tm)
    hb = tm // CONV_HALO
    n_tiles = S // tm
    last_halo = S // CONV_HALO - 1

    def body(a_ref, gt_ref, ah_ref, gh_ref, do_ref, y_ref, don_ref, yn_ref, w_ref, g_ref, b_ref,
             dglu_ref, dw_ref, small_ref, glu_ref, dy_ref, gsh_ref, dsh_ref, dwacc_ref, wb_ref):
        i = pl.program_id(0)

        @pl.when(i == 0)
        def _():
            dwacc_ref[...] = jnp.zeros_like(dwacc_ref)
            small_ref[...] = jnp.zeros_like(small_ref)

        def ln_bwd(do_v, y_v):
            xhat, rstd = _layernorm_stats(y_v)
            z = xhat * g_ref[...] + b_ref[...]
            sg = _sigmoid(z)
            dz = do_v * (sg * (1.0 + z * (1.0 - sg)))
            dxh = dz * g_ref[...]
            dy = rstd * (dxh - jnp.mean(dxh, axis=-1, keepdims=True) - xhat * jnp.mean(dxh * xhat, axis=-1, keepdims=True))
            return dy, dz, xhat

        dy, dz, xhat = ln_bwd(do_ref[...], y_ref[...])
        dy_next, _, _ = ln_bwd(don_ref[...], yn_ref[...])
        small_ref[0:1, :] += jnp.sum(dz * xhat, axis=0, keepdims=True)
        small_ref[1:2, :] += jnp.sum(dz, axis=0, keepdims=True)
        small_ref[2:3, :] += jnp.sum(dy, axis=0, keepdims=True)
        dy_ref[0:tm, :] = dy
        dy_ref[tm:, :] = jnp.where(i < n_tiles - 1, dy_next, 0.0)

        halo = ah_ref[...].astype(F32) * _sigmoid(gh_ref[...].astype(F32))
        glu_ref[0:CONV_HALO, :] = jnp.where(i > 0, halo, 0.0)
        glu_ref[CONV_HALO:, :] = a_ref[...].astype(F32) * _sigmoid(gt_ref[...].astype(F32))
        _fill_shifted(glu_ref, gsh_ref, tm)
        _fill_shifted(dy_ref, dsh_ref, tm)
        _fill_tap_rows(w_ref, wb_ref)

        def chunk(r, carry):
            r0 = pl.multiple_of(r * CONV_CHUNK, CONV_CHUNK)
            rows = pl.ds(r0, CONV_CHUNK)
            dyc = dy_ref[rows, :]
            for k in range(CONV_WIDTH):
                prod = dyc * _shifted_rows(glu_ref, gsh_ref, r0, CONV_HALO - (CONV_WIDTH - 1) + k)
                dwacc_ref[k * SUBLANES:(k + 1) * SUBLANES, :] += prod[0:SUBLANES] + prod[SUBLANES:]
            dg = _tap_sum(wb_ref, lambda k: _shifted_rows(dy_ref, dsh_ref, r0, CONV_WIDTH - 1 - k))
            a = a_ref[rows, :].astype(F32)
            sig = _sigmoid(gt_ref[rows, :].astype(F32))
            dglu_ref[rows, 0:C] = (dg * sig).astype(dglu_ref.dtype)
            dglu_ref[rows, C:] = (dg * a * sig * (1.0 - sig)).astype(dglu_ref.dtype)
            return carry

        lax.fori_loop(0, tm // CONV_CHUNK, chunk, 0)

        @pl.when(i == n_tiles - 1)
        def _():
            dw_ref[...] = jnp.zeros_like(dw_ref)
            for k in range(CONV_WIDTH):
                row_k = jnp.sum(dwacc_ref[k * SUBLANES:(k + 1) * SUBLANES, :], axis=0, keepdims=True)
                for j in range(N_DEV):
                    dw_ref[j, k:k + 1, :] = row_k[:, j * LANES:(j + 1) * LANES]

    vec = pl.BlockSpec((1, C), lambda i: (0, 0))
    row = pl.BlockSpec((tm, C), lambda i: (i, 0))
    nxt = pl.BlockSpec((CONV_HALO, C), lambda i: (jnp.minimum((i + 1) * hb, last_halo), 0))
    return pl.pallas_call(
        body, name=name, grid=(n_tiles,),
        in_specs=[
            pl.BlockSpec((tm, C), lambda i: (i, PB_A)),
            pl.BlockSpec((tm, C), lambda i: (i, PB_GATE)),
            pl.BlockSpec((CONV_HALO, C), lambda i: (jnp.maximum(i * hb - 1, 0), PB_A)),
            pl.BlockSpec((CONV_HALO, C), lambda i: (jnp.maximum(i * hb - 1, 0), PB_GATE)),
            row, row, nxt, nxt,
            pl.BlockSpec((N_DEV, CONV_WIDTH, LANES), lambda i: (0, 0, 0)),
            vec, vec,
        ],
        out_specs=[
            pl.BlockSpec((tm, 2 * C), lambda i: (i, 0)),
            pl.BlockSpec((N_DEV, CONV_HALO, LANES), lambda i: (0, 0, 0)),
            pl.BlockSpec((8, C), lambda i: (0, 0)),
        ],
        out_shape=[
            jax.ShapeDtypeStruct((S, 2 * C), BF16),
            jax.ShapeDtypeStruct((N_DEV, CONV_HALO, LANES), F32),
            jax.ShapeDtypeStruct((8, C), F32),
        ],
        scratch_shapes=[
            pltpu.VMEM((tm + CONV_HALO, C), F32), pltpu.VMEM((tm + CONV_HALO, C), F32),
            pltpu.VMEM((SUBLANES - 1, tm + CONV_HALO, C), F32), pltpu.VMEM((SUBLANES - 1, tm + CONV_HALO, C), F32),
            pltpu.VMEM((CONV_HALO * SUBLANES, C), F32), pltpu.VMEM((CONV_HALO * SUBLANES, C), F32),
        ],
        compiler_params=_cparams(("arbitrary",), 56),
    )(P, P, P, P, do, y, do, y, w_pad, ln_g, ln_b)


def _mem_softmax(qh, kh):
    s = lax.dot_general(qh, kh, NT_DIMS, preferred_element_type=F32)
    e = jnp.exp(s - jnp.max(s, axis=1, keepdims=True))
    return e / jnp.sum(e, axis=1, keepdims=True)


def _mem_fwd(P, kv, *, name, tm=1024):
    S, M = P.shape[0], kv.shape[0]
    tm = _tile(S, tm)
    scale = MEM_HEAD_DIM ** -0.5

    def body(q_ref, k_ref, v_ref, o_ref):
        for h in range(MEM_HEADS):
            sl = slice(h * MEM_HEAD_DIM, (h + 1) * MEM_HEAD_DIM)
            qh = q_ref[:, sl] * jnp.asarray(scale, BF16)
            p = _mem_softmax(qh, k_ref[:, sl])
            o_ref[:, sl] = jnp.dot(p.astype(BF16), v_ref[:, sl], preferred_element_type=F32).astype(o_ref.dtype)

    return pl.pallas_call(
        body, name=name, grid=(S // tm,),
        in_specs=[
            pl.BlockSpec((tm, D_MODEL), lambda i: (i, PB_QMEM)),
            pl.BlockSpec((M, D_MODEL), lambda i: (0, 0)),
            pl.BlockSpec((M, D_MODEL), lambda i: (0, 1)),
        ],
        out_specs=pl.BlockSpec((tm, D_MODEL), lambda i: (i, 0)),
        out_shape=jax.ShapeDtypeStruct((S, D_MODEL), BF16),
        compiler_params=_cparams(("parallel",)),
    )(P, kv, kv)


def _mem_bwd(P, do, kv, *, name, tm=1024):
    S, M = P.shape[0], kv.shape[0]
    tm = _tile(S, tm)
    scale = MEM_HEAD_DIM ** -0.5

    def body(q_ref, k_ref, v_ref, do_ref, dq_ref, dkv_ref):
        i = pl.program_id(0)

        @pl.when(i == 0)
        def _():
            dkv_ref[...] = jnp.zeros_like(dkv_ref)

        for h in range(MEM_HEADS):
            sl = slice(h * MEM_HEAD_DIM, (h + 1) * MEM_HEAD_DIM)
            slv = slice(D_MODEL + h * MEM_HEAD_DIM, D_MODEL + (h + 1) * MEM_HEAD_DIM)
            qh = q_ref[:, sl] * jnp.asarray(scale, BF16)
            kh, vh, doh = k_ref[:, sl], v_ref[:, sl], do_ref[:, sl]
            p = _mem_softmax(qh, kh)
            dp = lax.dot_general(doh, vh, NT_DIMS, preferred_element_type=F32)
            ds = p * (dp - jnp.sum(p * dp, axis=1, keepdims=True))
            dsb = ds.astype(BF16)
            dq_ref[:, sl] = (jnp.dot(dsb, kh, preferred_element_type=F32) * scale).astype(dq_ref.dtype)
            dkv_ref[:, sl] += lax.dot_general(dsb, qh, TN_DIMS, preferred_element_type=F32)
            dkv_ref[:, slv] += lax.dot_general(p.astype(BF16), doh, TN_DIMS, preferred_element_type=F32)

    row = pl.BlockSpec((tm, D_MODEL), lambda i: (i, 0))
    return pl.pallas_call(
        body, name=name, grid=(S // tm,),
        in_specs=[
            pl.BlockSpec((tm, D_MODEL), lambda i: (i, PB_QMEM)),
            pl.BlockSpec((M, D_MODEL), lambda i: (0, 0)),
            pl.BlockSpec((M, D_MODEL), lambda i: (0, 1)),
            row,
        ],
        out_specs=[row, pl.BlockSpec((M, 2 * D_MODEL), lambda i: (0, 0))],
        out_shape=[jax.ShapeDtypeStruct((S, D_MODEL), BF16), jax.ShapeDtypeStruct((M, 2 * D_MODEL), F32)],
        compiler_params=_cparams(("arbitrary",)),
    )(P, kv, kv, do)


def _merge_fwd(o_attn, o_conv, o_mem, P, wpa, wpc, wpm, w_out, x, g_post, g_pre, *, name, tm=256):
    S = P.shape[0]
    D = D_MODEL
    tm = _tile(S, tm)

    def body(oa_ref, oc_ref, om_ref, g0_ref, g1_ref, g2_ref, wa_ref, wc_ref, wm_ref, wo_ref, x_ref, gp_ref, gq_ref,
             mg_ref, y_ref, pb_ref, x1_ref, h2_ref):
        merged = jnp.zeros((tm, D), F32)
        for b, (o_ref, g_ref, w_ref) in enumerate(((oa_ref, g0_ref, wa_ref), (oc_ref, g1_ref, wc_ref), (om_ref, g2_ref, wm_ref))):
            pb = jnp.dot(o_ref[...], w_ref[...], preferred_element_type=F32)
            pb_ref[b] = pb.astype(pb_ref.dtype)
            merged = merged + _sigmoid(g_ref[...].astype(F32)) * pb
        mb = merged.astype(BF16)
        mg_ref[...] = mb
        yv = jnp.dot(mb, wo_ref[...], preferred_element_type=F32)
        y_ref[...] = yv
        x1 = x_ref[...] + yv * lax.rsqrt(jnp.mean(yv * yv, axis=-1, keepdims=True) + RMS_EPS) * gp_ref[...]
        x1_ref[...] = x1
        h2_ref[...] = (x1 * lax.rsqrt(jnp.mean(x1 * x1, axis=-1, keepdims=True) + RMS_EPS) * gq_ref[...]).astype(h2_ref.dtype)

    row = pl.BlockSpec((tm, D), lambda i: (i, 0))
    vec = pl.BlockSpec((1, D), lambda i: (0, 0))
    wsp = pl.BlockSpec((D, D), lambda i: (0, 0))
    return pl.pallas_call(
        body, name=name, grid=(S // tm,),
        in_specs=[row, row, row] + [pl.BlockSpec((tm, D), lambda i, b=b: (i, PB_G0 + b)) for b in range(3)] + [wsp] * 4
        + [row, vec, vec],
        out_specs=[row, row, pl.BlockSpec((3, tm, D), lambda i: (0, i, 0)), row, row],
        out_shape=[jax.ShapeDtypeStruct((S, D), BF16), jax.ShapeDtypeStruct((S, D), F32), jax.ShapeDtypeStruct((3, S, D), BF16),
                   jax.ShapeDtypeStruct((S, D), F32), jax.ShapeDtypeStruct((S, D), BF16)],
        compiler_params=_cparams(("parallel",), 56),
    )(o_attn, o_conv, o_mem, P, P, P, wpa, wpc, wpm, w_out, x, g_post, g_pre)


def _merge_bwd(dy1, w_out, P, pb, *, name, tm=256):
    S = P.shape[0]
    D = D_MODEL
    tm = _tile(S, tm)

    def body(dy_ref, wo_ref, g0_ref, g1_ref, g2_ref, pb_ref, d0_ref, d1_ref, d2_ref, dgl_ref):
        dm = lax.dot_general(dy_ref[...], wo_ref[...], NT_DIMS, preferred_element_type=F32)
        for b, (g_ref, d_ref) in enumerate(((g0_ref, d0_ref), (g1_ref, d1_ref), (g2_ref, d2_ref))):
            g = _sigmoid(g_ref[...].astype(F32))
            d_ref[...] = (dm * g).astype(d_ref.dtype)
            dgl_ref[:, b * D:(b + 1) * D] = (dm * pb_ref[b].astype(F32) * g * (1.0 - g)).astype(dgl_ref.dtype)

    row = pl.BlockSpec((tm, D), lambda i: (i, 0))
    blk3 = pl.BlockSpec((3, tm, D), lambda i: (0, i, 0))
    sds = jax.ShapeDtypeStruct((S, D), BF16)
    return pl.pallas_call(
        body, name=name, grid=(S // tm,),
        in_specs=[row, pl.BlockSpec((D, D), lambda i: (0, 0))]
        + [pl.BlockSpec((tm, D), lambda i, b=b: (i, PB_G0 + b)) for b in range(3)] + [blk3],
        out_specs=[row, row, row, pl.BlockSpec((tm, 3 * D), lambda i: (i, 0))],
        out_shape=[sds, sds, sds, jax.ShapeDtypeStruct((S, 3 * D), BF16)],
        compiler_params=_cparams(("parallel",), 56),
    )(dy1, w_out, P, P, P, pb)


def _swiglu_fwd(h2, w_gu_t, *, name, tm=512, tn=1408):
    S, D = h2.shape
    Fh = w_gu_t.shape[0] // 2
    tm, tn = _tile(S, tm), _tile(Fh, tn)
    nj = Fh // tn

    def body(h_ref, wg_ref, wu_ref, g_ref, u_ref, a_ref):
        hv = h_ref[...]
        g = lax.dot_general(hv, wg_ref[...], NT_DIMS, preferred_element_type=F32)
        u = lax.dot_general(hv, wu_ref[...], NT_DIMS, preferred_element_type=F32)
        g_ref[...] = g.astype(g_ref.dtype)
        u_ref[...] = u.astype(u_ref.dtype)
        a_ref[...] = (g * _sigmoid(g) * u).astype(a_ref.dtype)

    out = pl.BlockSpec((tm, tn), lambda j, i: (i, j))
    sds = jax.ShapeDtypeStruct((S, Fh), BF16)
    return pl.pallas_call(
        body, name=name, grid=(nj, S // tm),
        in_specs=[
            pl.BlockSpec((tm, D), lambda j, i: (i, 0)),
            pl.BlockSpec((tn, D), lambda j, i: (j, 0)),
            pl.BlockSpec((tn, D), lambda j, i: (j + nj, 0)),
        ],
        out_specs=[out, out, out], out_shape=[sds, sds, sds],
        compiler_params=_cparams(("parallel", "parallel"), 56),
    )(h2, w_gu_t, w_gu_t)


def _swiglu_bwd(dffn, w_down, g, u, *, name, tm=512, tn=1408):
    S, Fh = g.shape
    D = dffn.shape[1]
    tm, tn = _tile(S, tm), _tile(Fh, tn)

    def body(df_ref, w_ref, g_ref, u_ref, o_ref):
        dfv = df_ref[...]
        for j in range(Fh // tn):
            cols = slice(j * tn, (j + 1) * tn)
            da = lax.dot_general(dfv, w_ref[cols, :], NT_DIMS, preferred_element_type=F32)
            gv = g_ref[:, cols].astype(F32)
            sg = _sigmoid(gv)
            o_ref[:, cols] = (da * u_ref[:, cols].astype(F32) * (sg * (1.0 + gv * (1.0 - sg)))).astype(o_ref.dtype)
            o_ref[:, Fh + j * tn:Fh + (j + 1) * tn] = (da * gv * sg).astype(o_ref.dtype)

    act = pl.BlockSpec((tm, Fh), lambda i: (i, 0))
    return pl.pallas_call(
        body, name=name, grid=(S // tm,),
        in_specs=[pl.BlockSpec((tm, D), lambda i: (i, 0)), pl.BlockSpec((Fh, D), lambda i: (0, 0)), act, act],
        out_specs=pl.BlockSpec((tm, 2 * Fh), lambda i: (i, 0)),
        out_shape=jax.ShapeDtypeStruct((S, 2 * Fh), BF16),
        compiler_params=_cparams(("parallel",), 56),
    )(dffn, w_down, g, u)


def _final(x1, ffn, target, g, *, name, tm=512):
    S, D = x1.shape
    tm = _tile(S, tm)

    def body(x_ref, f_ref, t_ref, g_ref, dout_ref, dffn_ref, loss_ref, dg_ref):
        i = pl.program_id(0)
        fv = f_ref[...]
        rstd = lax.rsqrt(jnp.mean(fv * fv, axis=-1, keepdims=True) + RMS_EPS)
        r = fv * rstd
        e = x_ref[...] + r * g_ref[...] - t_ref[...]
        dout = e * (1.0 / D)
        dout_ref[...] = dout
        gy = dout * g_ref[...]
        dffn_ref[...] = (rstd * (gy - r * jnp.mean(gy * r, axis=-1, keepdims=True))).astype(dffn_ref.dtype)
        lpart = jnp.full((8, LANES), 0.5 * jnp.sum(jnp.mean(e * e, axis=-1, keepdims=True)), F32)
        gpart = jnp.sum(dout * r, axis=0, keepdims=True)

        @pl.when(i == 0)
        def _():
            loss_ref[...] = lpart
            dg_ref[...] = gpart

        @pl.when(i > 0)
        def _():
            loss_ref[...] += lpart
            dg_ref[...] += gpart

    row = pl.BlockSpec((tm, D), lambda i: (i, 0))
    vec = pl.BlockSpec((1, D), lambda i: (0, 0))
    return pl.pallas_call(
        body, name=name, grid=(S // tm,),
        in_specs=[row, row, row, vec],
        out_specs=[row, row, pl.BlockSpec((8, LANES), lambda i: (0, 0)), vec],
        out_shape=[jax.ShapeDtypeStruct((S, D), F32), jax.ShapeDtypeStruct((S, D), BF16),
                   jax.ShapeDtypeStruct((8, LANES), F32), jax.ShapeDtypeStruct((1, D), F32)],
        compiler_params=_cparams(("arbitrary",)),
    )(x1, ffn, target, g)


def _after(token, value):
    return value if token is None else value + token[0, 0].astype(value.dtype)


def _local_step(x, mem, target, gains, w, late_weights, send_grads, *, tq=512):
    S = x.shape[0]
    b_pad = jnp.pad(gains["b_forget"], ((0, 0), (0, LANES - FOX_HEADS)))
    gains = dict(gains)

    h, f_logit = _rms_fwd(x, gains["norm_mix_pre"], name="rms_mix_pre", proj_t=w["wf_t"])
    P = _matmul(h, w["wp_t"], mode="nt", out_dtype=BF16, name="proj_in", tm=2048)
    logf = _forget_fwd(f_logit, b_pad, name="forget_fwd")
    c_row16 = _cumsum_lanes([logf[:, :FOX_HEADS].T], reverse=False, name="forget_cumsum")
    crow = c_row16.reshape(HEAD_PAIRS, 2, S)
    o_attn, o_attn32, lse = _attn_fwd(P, crow, tq=tq, name="attn_fwd")
    w = dict(w, **late_weights(o_attn))
    o_conv, y_conv = _conv_fwd(P, w["conv_w"], gains["conv_b"], gains["conv_ln_g"], gains["conv_ln_b"], name="conv_fwd")
    mem_n = _rms_fwd(mem, gains["norm_mem"], name="rms_mem")
    kv = _matmul(mem_n, w["w_kv"], mode="nn", out_dtype=BF16, name="mem_kv")
    o_mem = _mem_fwd(P, kv, name="mem_fwd")
    merged, y1, pb, x1, h2 = _merge_fwd(o_attn, o_conv, o_mem, P, w["wpa"], w["wpc"], w["wpm"], w["w_out"], x,
                                        gains["norm_mix_post"], gains["norm_ffn_pre"], name="merge_fwd")
    g_ffn, u_ffn, act = _swiglu_fwd(h2, w["w_gu_t"], name="swiglu_fwd")
    ffn = _matmul(act, w["w_down"], mode="nn", out_dtype=F32, name="ffn_down", tk=1408)

    dout, dffn, loss_tile, d_norm_ffn_post = _final(x1, ffn, target, gains["norm_ffn_post"], name="loss_head")
    d_w_down = _matmul(act, dffn, mode="tn", out_dtype=BF16, name="dw_down", tm=1408)
    dgu = _swiglu_bwd(dffn, w["w_down"], g_ffn, u_ffn, name="swiglu_bwd")
    dh2 = _matmul(dgu, w["w_gu_t"], mode="nn", out_dtype=F32, name="d_h2", tk=1408)
    d_w_gu_t = _matmul(dgu, h2, mode="tn", out_dtype=BF16, name="dw_gate_up", tm=1408)
    dx1, dy1, d_norm_ffn_pre, d_norm_mix_post = _rms_bwd_pair(x1, dh2, dout, gains["norm_ffn_pre"], y1, gains["norm_mix_post"],
                                                              name="rms_ffn_pre_mix_post_bwd")

    d_w_out = _matmul(merged, dy1, mode="tn", out_dtype=BF16, name="dw_out")
    *dpb, dgl = _merge_bwd(dy1, w["w_out"], P, pb, name="merge_bwd")
    do_attn = _matmul(dpb[0], w["wpa"], mode="nt", out_dtype=BF16, name="d_o_attn")
    do_conv = _matmul(dpb[1], w["wpc"], mode="nt", out_dtype=F32, name="d_o_conv")
    do_mem = _matmul(dpb[2], w["wpm"], mode="nt", out_dtype=BF16, name="d_o_mem")
    d_wpa = _matmul(o_attn, dpb[0], mode="tn", out_dtype=BF16, name="dw_proj_attn")
    d_wpc = _matmul(o_conv, dpb[1], mode="tn", out_dtype=BF16, name="dw_proj_conv")
    d_wpm = _matmul(o_mem, dpb[2], mode="tn", out_dtype=BF16, name="dw_proj_mem")

    dq_mem, dkv = _mem_bwd(P, do_mem, kv, name="mem_bwd")
    dkv_b = dkv.astype(BF16)
    d_w_kv = _matmul(mem_n, dkv_b, mode="tn", out_dtype=BF16, name="dw_kv")
    dmem_n = _matmul(dkv_b, w["w_kv"], mode="nt", out_dtype=F32, name="d_mem_n")
    _, d_norm_mem = _rms_bwd(mem, [dmem_n], gains["norm_mem"], None, out_dtype=BF16, name="rms_mem_bwd")

    dglu, d_conv_w, conv_small = _conv_bwd(P, do_conv, y_conv, w["conv_w"], gains["conv_ln_g"], gains["conv_ln_b"], name="conv_bwd")
    sent = send_grads("body", dict(conv_w=d_conv_w[:, :CONV_WIDTH].astype(BF16), w_kv_mem=d_w_kv, w_proj_attn=d_wpa,
                                   w_proj_conv=d_wpc, w_proj_mem=d_wpm, w_out=d_w_out, w_gate_up=d_w_gu_t, w_down=d_w_down))

    dq, dk, dv, dc, dcq = _attn_bwd(P, do_attn, o_attn32, lse, _after(sent, crow), tq=tq, name="attn_bwd")
    dlogf16 = _cumsum_lanes([dc.reshape(FOX_HEADS, S), dcq.reshape(FOX_HEADS, S)], reverse=True, name="forget_cumsum_bwd")
    dlogf = jnp.pad(dlogf16.T, ((0, 0), (0, LANES - FOX_HEADS)))
    df, d_b_pad = _forget_bwd(f_logit, b_pad, dlogf, name="forget_bwd")

    d_proj = [dq, dk, dv, dglu, dq_mem, dgl]
    d_wp_t = _matmul_pieces(d_proj, h, mode="tn", out_dtype=BF16, name="dw_in")
    d_wf_t = _matmul(df, h, mode="tn", out_dtype=BF16, name="dw_forget")
    n_qkv = 3 * D_MODEL
    pieces = [(d_wp_t[:n_qkv], n_qkv), (d_wf_t[:FOX_HEADS], FOX_HEADS), (d_wp_t[n_qkv:], P_WIDTH - n_qkv)]
    n_own = (P_WIDTH + FOX_HEADS) // N_DEV
    blocks = [jnp.concatenate(_rows(pieces, j * n_own, (j + 1) * n_own), axis=0) for j in range(N_DEV)]
    sent = send_grads("w_in", dict(w_in=jnp.stack(blocks)))
    dh_p = _matmul_pieces(d_proj, w["wp_t"], mode="nn", out_dtype=F32, name="d_h", after=sent)
    grad_x, d_norm_mix_pre = _rms_bwd(x, [dh_p], gains["norm_mix_pre"], dx1, out_dtype=F32, name="rms_mix_pre_bwd",
                                      proj=(df, w["wf_t"]))

    small = dict(norm_mix_pre=d_norm_mix_pre, norm_mix_post=d_norm_mix_post, norm_mem=d_norm_mem,
                 conv_b=conv_small[2:3], conv_ln_g=conv_small[0:1], conv_ln_b=conv_small[1:2],
                 norm_ffn_pre=d_norm_ffn_pre, norm_ffn_post=d_norm_ffn_post, b_forget=d_b_pad[:, :FOX_HEADS])
    return loss_tile, grad_x, small


def _mesh_pos():
    return lax.axis_index("x"), lax.axis_index("y"), lax.axis_index("c")


def _flip(pos, d):
    x, y, c = pos
    return (1 - x if d & 4 else x, 1 - y if d & 2 else y, 1 - c if d & 1 else c)


def _flat(pos):
    x, y, c = pos
    return 4 * x + 2 * y + c


def _gather_two_level(src, *, name):
    def body(src_ref, out_ref, token, send_sems, recv_sems, local_sem):
        me = _mesh_pos()
        my = _flat(me)
        sibling = _flip(me, 1)
        far = [_flip(me, d) for d in (4, 2, 6)]

        def copy(k, block, to, own=False):
            return pltpu.make_async_remote_copy(
                src_ref=src_ref if own else out_ref.at[_flat(block)], dst_ref=out_ref.at[_flat(block)],
                send_sem=send_sems.at[k], recv_sem=recv_sems.at[k], device_id=to, device_id_type=pl.DeviceIdType.MESH)

        local = pltpu.make_async_copy(src_ref, out_ref.at[my], local_sem)
        local.start()
        first = [copy(0, me, sibling, own=True)] + [copy(1 + j, me, peer, own=True) for j, peer in enumerate(far)]
        for cp in first:
            cp.start()
        passed = [copy(4 + j, peer, sibling) for j, peer in enumerate(far)]
        for j, peer in enumerate(far):
            copy(1 + j, peer, me).wait_recv()
            passed[j].start()
        copy(0, sibling, me).wait_recv()
        for j, peer in enumerate(far):
            copy(4 + j, _flip(peer, 1), me).wait_recv()
        for cp in first + passed:
            cp.wait_send()
        local.wait()
        token[...] = jnp.zeros_like(token)

    n_copy = N_DEV - 1
    out, token = pl.pallas_call(
        body, name=name,
        in_specs=[pl.BlockSpec(memory_space=pl.ANY)],
        out_specs=[pl.BlockSpec(memory_space=pl.ANY), pl.BlockSpec(memory_space=pltpu.VMEM)],
        out_shape=[jax.ShapeDtypeStruct((N_DEV,) + tuple(src.shape), src.dtype), jax.ShapeDtypeStruct((8, LANES), F32)],
        scratch_shapes=[pltpu.SemaphoreType.DMA((n_copy,)), pltpu.SemaphoreType.DMA((n_copy,)), pltpu.SemaphoreType.DMA(())],
    )(src)
    return [out], token


def _exchange(srcs, *, scatter, name):
    T = len(srcs)
    n_peer = N_DEV - 1

    def body(*refs):
        src_refs, out_refs = refs[:T], refs[T:2 * T]
        token, send_sems, recv_sems, local_sems = refs[2 * T:]
        me = _mesh_pos()
        my = _flat(me)

        def copy(t, d):
            peer = _flip(me, d)
            return pltpu.make_async_remote_copy(
                src_ref=src_refs[t].at[_flat(peer)] if scatter else src_refs[t],
                dst_ref=out_refs[t].at[my],
                send_sem=send_sems.at[t * n_peer + d - 1], recv_sem=recv_sems.at[t * n_peer + d - 1],
                device_id=peer, device_id_type=pl.DeviceIdType.MESH)

        def arrival(t, d):
            peer = _flip(me, d)
            return pltpu.make_async_remote_copy(
                src_ref=src_refs[t].at[my] if scatter else src_refs[t],
                dst_ref=out_refs[t].at[_flat(peer)],
                send_sem=send_sems.at[t * n_peer + d - 1], recv_sem=recv_sems.at[t * n_peer + d - 1],
                device_id=peer, device_id_type=pl.DeviceIdType.MESH)

        local = [pltpu.make_async_copy(src_refs[t].at[my] if scatter else src_refs[t], out_refs[t].at[my], local_sems.at[t])
                 for t in range(T)]
        for cp in local:
            cp.start()
        sends = [copy(t, d) for d in range(1, N_DEV) for t in range(T)]
        for cp in sends:
            cp.start()
        for d in range(1, N_DEV):
            for t in range(T):
                arrival(t, d).wait_recv()
        for cp in sends:
            cp.wait_send()
        for cp in local:
            cp.wait()
        token[...] = jnp.zeros_like(token)

    outs = pl.pallas_call(
        body, name=name,
        in_specs=[pl.BlockSpec(memory_space=pl.ANY)] * T,
        out_specs=[pl.BlockSpec(memory_space=pl.ANY)] * T + [pl.BlockSpec(memory_space=pltpu.VMEM)],
        out_shape=[jax.ShapeDtypeStruct((N_DEV,) + tuple(s.shape[-2:]), s.dtype) for s in srcs] + [jax.ShapeDtypeStruct((8, LANES), F32)],
        scratch_shapes=[pltpu.SemaphoreType.DMA((T * n_peer,)), pltpu.SemaphoreType.DMA((T * n_peer,)), pltpu.SemaphoreType.DMA((T,))],
    )(*srcs)
    return list(outs[:T]), outs[T]


HBM_SPEC = pl.BlockSpec(memory_space=pltpu.HBM)
SEM_SPEC = pl.BlockSpec(memory_space=pltpu.SEMAPHORE)


def _split_copies(src_refs, land_refs, send_sems, recv_sems, scatter):
    me = _mesh_pos()
    my = _flat(me)
    n_peer = N_DEV - 1
    out = []
    for d in range(1, N_DEV):
        peer = _flip(me, d)
        for t, (src, land) in enumerate(zip(src_refs, land_refs)):
            k = t * n_peer + d - 1
            start = pltpu.make_async_remote_copy(
                src_ref=src.at[_flat(peer)] if scatter else src, dst_ref=land.at[my],
                send_sem=send_sems.at[k], recv_sem=recv_sems.at[k], device_id=peer, device_id_type=pl.DeviceIdType.MESH)
            arrive = pltpu.make_async_remote_copy(
                src_ref=src.at[my] if scatter else src, dst_ref=land.at[_flat(peer)],
                send_sem=send_sems.at[k], recv_sem=recv_sems.at[k], device_id=peer, device_id_type=pl.DeviceIdType.MESH)
            out.append((start, arrive))
    return out


def _local_copies(src_refs, land_refs, local_sems, scatter):
    my = _flat(_mesh_pos())
    return [pltpu.make_async_copy(src.at[my] if scatter else src, land.at[my], local_sems.at[t])
            for t, (src, land) in enumerate(zip(src_refs, land_refs))]


def _exchange_start(srcs, *, scatter, name):
    T = len(srcs)
    n_sem = T * (N_DEV - 1)
    lands = [lax.empty((N_DEV,) + tuple(s.shape[-2:]), s.dtype) for s in srcs]

    def body(*refs):
        src_refs, land_refs = refs[:T], refs[T:2 * T]
        send_sems, recv_sems, local_sems = refs[2 * T:2 * T + 3]
        token = refs[-1]
        for cp in _local_copies(src_refs, land_refs, local_sems, scatter):
            cp.start()
        for start, _ in _split_copies(src_refs, land_refs, send_sems, recv_sems, scatter):
            start.start()
        token[...] = jnp.zeros_like(token)

    hbm = lambda a: pltpu.HBM(a.shape, a.dtype)
    outs = pl.pallas_call(
        body, name=name,
        in_specs=[HBM_SPEC] * (2 * T),
        out_specs=(SEM_SPEC, SEM_SPEC, SEM_SPEC, *[HBM_SPEC] * (2 * T), pl.BlockSpec(memory_space=pltpu.VMEM)),
        out_shape=(pltpu.SemaphoreType.DMA((n_sem,)), pltpu.SemaphoreType.DMA((n_sem,)), pltpu.SemaphoreType.DMA((T,)),
                   *[hbm(s) for s in srcs], *[hbm(a) for a in lands], jax.ShapeDtypeStruct((8, LANES), F32)),
        input_output_aliases={t: 3 + t for t in range(2 * T)},
        compiler_params=pltpu.CompilerParams(has_side_effects=pltpu.SideEffectType.DATAFLOW_SIDE_EFFECTING),
    )(*[pltpu.with_memory_space_constraint(s, pltpu.HBM) for s in srcs],
      *[pltpu.with_memory_space_constraint(a, pltpu.HBM) for a in lands])
    return (outs[0], outs[1], outs[2], list(outs[3:3 + T]), list(outs[3 + T:3 + 2 * T])), outs[-1]


def _exchange_wait(handle, after, *, scatter, name):
    send_sems, recv_sems, local_sems, srcs, lands = handle
    T = len(srcs)

    def body(*refs):
        src_refs, land_refs = refs[:T], refs[T:2 * T]
        s_sems, r_sems, l_sems = refs[2 * T:2 * T + 3]
        for cp in _local_copies(src_refs, land_refs, l_sems, scatter):
            cp.wait()
        for start, arrive in _split_copies(src_refs, land_refs, s_sems, r_sems, scatter):
            start.wait_send()
            arrive.wait_recv()

    hbm = lambda a: pltpu.HBM(a.shape, a.dtype)
    outs = pl.pallas_call(
        body, name=name,
        in_specs=[HBM_SPEC] * (2 * T) + [SEM_SPEC, SEM_SPEC, SEM_SPEC, pl.BlockSpec(memory_space=pl.ANY)],
        out_specs=[HBM_SPEC] * (2 * T),
        out_shape=[hbm(s) for s in srcs] + [hbm(a) for a in lands],
        input_output_aliases={t: t for t in range(2 * T)},
        compiler_params=pltpu.CompilerParams(has_side_effects=pltpu.SideEffectType.DATAFLOW_SIDE_EFFECTING),
    )(*srcs, *lands, send_sems, recv_sems, local_sems, after)
    return list(outs[T:])


def _adamw(w, slots, m, v, *, name):
    R, C = w.shape
    row_tiles = [t for t in range(16, 513, 16) if R % t == 0]
    tr = R if R <= 512 or not row_tiles else max(row_tiles)
    tc = C if tr < R or R <= 512 else LANES
    assert R % tr == 0 and C % tc == 0, (R, C, tr, tc)

    def body(w_ref, s_ref, m_ref, v_ref, g_ref, d_ref, m2_ref, v2_ref):
        gv = s_ref[0].astype(F32)
        for j in range(1, N_DEV):
            gv = gv + s_ref[j].astype(F32)
        g_ref[...] = gv
        m2 = ADAM_B1 * m_ref[...] + (1.0 - ADAM_B1) * gv
        v2 = ADAM_B2 * v_ref[...] + (1.0 - ADAM_B2) * (gv * gv)
        m_hat = m2 / (1.0 - ADAM_B1 ** ADAM_STEP)
        v_hat = v2 / (1.0 - ADAM_B2 ** ADAM_STEP)
        d_ref[...] = -ADAM_LR * (m_hat / (jnp.sqrt(v_hat) + ADAM_EPS) + ADAM_WD * w_ref[...])
        m2_ref[...] = m2
        v2_ref[...] = v2

    blk = pl.BlockSpec((tr, tc), lambda i, j: (i, j))
    sds = jax.ShapeDtypeStruct((R, C), F32)
    return pl.pallas_call(
        body, name=name, grid=(R // tr, C // tc),
        in_specs=[blk, pl.BlockSpec((N_DEV, tr, tc), lambda i, j: (0, i, j)), blk, blk],
        out_specs=[blk] * 4, out_shape=[sds] * 4,
        compiler_params=_cparams(("parallel", "parallel")),
    )(w, slots, m, v)


SHARDED = (
    ("w_in", (1154, 1024), "row"), ("conv_w", (31, 128), "col"), ("w_kv_mem", (1024, 256), "col"),
    ("w_proj_attn", (128, 1024), "row"), ("w_proj_conv", (128, 1024), "row"), ("w_proj_mem", (128, 1024), "row"),
    ("w_out", (128, 1024), "row"), ("w_gate_up", (704, 1024), "row"), ("w_down", (352, 1024), "row"),
)
TRANSPOSED = ("w_in", "w_gate_up")
SMALL = ("norm_mix_pre", "norm_mix_post", "norm_mem", "conv_b", "conv_ln_g", "conv_ln_b", "norm_ffn_pre", "norm_ffn_post", "b_forget")
SMALL_ROWS = 16
LOSS_ROW = len(SMALL)
WEIGHT_ORDER = ("norm_mix_pre", "norm_mix_post", "norm_mem", "w_in", "b_forget", "conv_w", "conv_b", "conv_ln_g", "conv_ln_b",
                "w_kv_mem", "w_proj_attn", "w_proj_conv", "w_proj_mem", "w_out", "norm_ffn_pre", "norm_ffn_post", "w_gate_up", "w_down")


def _rows(pieces, lo, hi):
    out, start = [], 0
    for a, n in pieces:
        a0, a1 = max(lo, start), min(hi, start + n)
        if a0 < a1:
            out.append(a[a0 - start:a1 - start])
        start += n
    return out


def _to_full(blocks8, kind):
    n, r, c = blocks8.shape
    if kind == "col":
        return jnp.concatenate([blocks8[j] for j in range(n)], axis=1)
    return blocks8.reshape(n * r, c)


def _to_blocks(full, kind):
    if kind == "col":
        c = full.shape[1] // N_DEV
        return jnp.stack([full[:, j * c:(j + 1) * c] for j in range(N_DEV)])
    nr, c = full.shape
    return full.reshape(N_DEV, nr // N_DEV, c)


def kernel(x, mem, norm_mix_pre, norm_mix_post, norm_mem, w_in, b_forget, conv_w, conv_b, conv_ln_g, conv_ln_b, w_kv_mem, w_proj_attn, w_proj_conv, w_proj_mem, w_out, norm_ffn_pre, norm_ffn_post, w_gate_up, w_down, loss_target, m_norm_mix_pre, m_norm_mix_post, m_norm_mem, m_w_in, m_b_forget, m_conv_w, m_conv_b, m_conv_ln_g, m_conv_ln_b, m_w_kv_mem, m_w_proj_attn, m_w_proj_conv, m_w_proj_mem, m_w_out, m_norm_ffn_pre, m_norm_ffn_post, m_w_gate_up, m_w_down, v_norm_mix_pre, v_norm_mix_post, v_norm_mem, v_w_in, v_b_forget, v_conv_w, v_conv_b, v_conv_ln_g, v_conv_ln_b, v_w_kv_mem, v_w_proj_attn, v_w_proj_conv, v_w_proj_mem, v_w_out, v_norm_ffn_pre, v_norm_ffn_post, v_w_gate_up, v_w_down):
    given = dict(locals())
    weights = {n: given[n] for n in WEIGHT_ORDER}
    moments_m = {n: given["m_" + n] for n in WEIGHT_ORDER}
    moments_v = {n: given["v_" + n] for n in WEIGHT_ORDER}
    kind = {name: k for name, _, k in SHARDED}
    names = [name for name, _, _ in SHARDED]

    def local(a, n):
        return jnp.swapaxes(a[0], 0, 1) if n in TRANSPOSED else a[0]

    blocks = {n: local(weights[n], n).astype(F32 if n == "conv_w" else BF16) for n in names}
    late = [n for n in names if n != "w_in"]
    (w_in_blocks,), w_in_token = _gather_two_level(blocks["w_in"], name="gather_w_in")
    blocks["w_kv_mem"] = _after(w_in_token, blocks["w_kv_mem"])
    late_handle, late_token = _exchange_start([blocks[n] for n in late], scatter=False, name="gather_rest_start")
    n_qkv = 3 * D_MODEL
    w_in_t = _to_full(w_in_blocks, "row")
    w_early = dict(
        wp_t=jnp.concatenate([w_in_t[:n_qkv], w_in_t[n_qkv + FOX_HEADS:]], axis=0),
        wf_t=jnp.pad(w_in_t[n_qkv:n_qkv + FOX_HEADS], ((0, LANES - FOX_HEADS), (0, 0))),
    )

    def late_weights(after):
        got = dict(zip(late, _exchange_wait(late_handle, after, scatter=False, name="gather_rest_wait")))
        full = {n: _to_full(got[n], kind[n]) for n in late if n != "conv_w"}
        return dict(
            conv_w=got["conv_w"],
            w_kv=full["w_kv_mem"], wpa=full["w_proj_attn"], wpc=full["w_proj_conv"], wpm=full["w_proj_mem"],
            w_out=full["w_out"], w_gu_t=full["w_gate_up"], w_down=full["w_down"],
        )

    sent = []

    def send_grads(group, g):
        order = [n for n in names if n in g]
        per_owner = [g[n] if g[n].ndim == 3 else _to_blocks(g[n], kind[n]) for n in order]
        handle, token = _exchange_start(per_owner, scatter=True, name="scatter_" + group + "_start")
        sent.append((group, order, handle))
        return token

    gains = {n: weights[n] for n in SMALL}
    gains["norm_mix_pre"] = _after(late_token, gains["norm_mix_pre"])

    loss_tile, grad_x, small = _local_step(x[0], mem[0], loss_target[0], gains, w_early, late_weights, send_grads)

    g_recv = {}
    for group, order, handle in sent:
        g_recv.update(zip(order, _exchange_wait(handle, grad_x, scatter=True, name="scatter_" + group + "_wait")))

    rows = [jnp.pad(small[n], ((0, 0), (0, D_MODEL - small[n].shape[1]))) for n in SMALL]
    rows.append(jnp.broadcast_to(loss_tile[0:1, 0:1], (1, D_MODEL)))
    rows.append(jnp.zeros((SMALL_ROWS - len(rows), D_MODEL), F32))
    (small_recv,), _ = _exchange([jnp.concatenate(rows, axis=0)], scatter=False, name="gather_small")

    def slab_of(d, fill):
        rows = [jnp.pad(d[n], ((0, 0), (0, D_MODEL - d[n].shape[1])), constant_values=fill) for n in SMALL]
        rows.append(jnp.full((SMALL_ROWS - len(rows), D_MODEL), fill, F32))
        return jnp.concatenate(rows, axis=0)

    grads, delta, new_m, new_v = {}, {}, {}, {}
    sg, sd, sm, sv = _adamw(slab_of(weights, 0.0), small_recv, slab_of(moments_m, 0.0), slab_of(moments_v, 1.0), name="adamw_small")
    loss = sg[LOSS_ROW, 0]
    for i, n in enumerate(SMALL):
        c = weights[n].shape[1]
        grads[n], delta[n], new_m[n], new_v[n] = sg[i:i + 1, :c], sd[i:i + 1, :c], sm[i:i + 1, :c], sv[i:i + 1, :c]
    for n in names:
        outs = _adamw(local(weights[n], n), g_recv[n], local(moments_m[n], n), local(moments_v[n], n), name="adamw_" + n)
        grads[n], delta[n], new_m[n], new_v[n] = [(jnp.swapaxes(o, 0, 1) if n in TRANSPOSED else o)[None] for o in outs]

    return (loss, grad_x[None], *[grads[n] for n in WEIGHT_ORDER], *[delta[n] for n in WEIGHT_ORDER],
            *[new_m[n] for n in WEIGHT_ORDER], *[new_v[n] for n in WEIGHT_ORDER])
```
